```python
import math
import jax, jax.numpy as jnp
from jax import lax
import numpy as np

D_MODEL = 4096
BATCH = 8
SEQ = 4096
DEPTH = 1

HEAD_DIM = 128
D_MIX = D_MODEL
W_A = D_MIX // 2
W_B = D_MIX - W_A
G_A = W_A // HEAD_DIM
H_B = W_B // HEAD_DIM
CHUNK = 128
BLK = 128
DILATED = ((128, 1), (512, 4), (2048, 16))
NUM_BUCKETS = 32
MAX_DISTANCE = 2048
D_PLE = 256
EPS = 1e-6
NEG_INF = -1e30
PROJ_SPLITS = (W_A, W_A, W_A, W_B, W_B, W_B, W_B)
D_IN = W_A * 3 + W_B * 4

kernel_name = "hybrid_gmlp_dilated_attn_layer"


def _rmsnorm(x, g):
    xf = x.astype(jnp.float32)
    y = xf * lax.rsqrt(jnp.mean(xf * xf, axis=-1, keepdims=True) + EPS)
    return (y * g.astype(jnp.float32)).astype(x.dtype)


def _layernorm(x, g, b):
    xf = x.astype(jnp.float32)
    mu = jnp.mean(xf, axis=-1, keepdims=True)
    xc = xf - mu
    y = xc * lax.rsqrt(jnp.mean(xc * xc, axis=-1, keepdims=True) + EPS)
    return (y * g.astype(jnp.float32) + b.astype(jnp.float32)).astype(x.dtype)


def _rel_bucket(dist):
    max_exact = NUM_BUCKETS // 2
    d = jnp.maximum(dist, 1).astype(jnp.float32)
    large = max_exact + (jnp.log(d / max_exact) / math.log(MAX_DISTANCE / max_exact)
                         * (NUM_BUCKETS - max_exact)).astype(jnp.int32)
    large = jnp.minimum(large, NUM_BUCKETS - 1)
    return jnp.where(dist < max_exact, dist, large)


def _dilated_window(q, k, v, rel_bias, window, dil):
    b, s, h, dh = q.shape
    span = dil * BLK
    s_pad = -(-s // span) * span
    L = s_pad // dil
    nb = L // BLK
    pad = ((0, 0), (0, s_pad - s), (0, 0), (0, 0))

    def to_sub(t):
        t = jnp.pad(t, pad).reshape(b, L, dil, h, dh).transpose(0, 2, 1, 3, 4)
        return t.reshape(b, dil, nb, BLK, h, dh)

    def with_prev(t):
        prev = jnp.pad(t, ((0, 0), (0, 0), (1, 0), (0, 0), (0, 0), (0, 0)))[:, :, :-1]
        return jnp.concatenate([prev, t], axis=3)

    qs = to_sub(q)
    kc = with_prev(to_sub(k))
    vc = with_prev(to_sub(v))
    logits = jnp.einsum('brnqhd,brnkhd->brnhqk', qs, kc)

    qi = BLK + jnp.arange(BLK)
    kj = jnp.arange(2 * BLK)
    delta = qi[:, None] - kj[None, :]
    band = (delta >= 0) & (delta <= window // dil)
    bucket = _rel_bucket(jnp.clip(delta, 0, None) * dil)
    bias = rel_bias[bucket].astype(jnp.float32).transpose(2, 0, 1)
    has_prev = (jnp.arange(nb)[:, None, None] > 0) | (kj[None, None, :] >= BLK)
    mask = band[None] & has_prev

    logits = jnp.where(mask[None, None, :, None], logits + bias[None, None, None], NEG_INF)
    m = jnp.max(logits, axis=-1, keepdims=True)
    e = jnp.exp(logits - m)
    den = jnp.sum(e, axis=-1, keepdims=True)
    lse = (m + jnp.log(den))[..., 0].transpose(0, 1, 2, 4, 3)
    o = jnp.einsum('brnhqk,brnkhd->brnqhd', e, vc) / den[..., 0].transpose(0, 1, 2, 4, 3)[..., None]

    def from_sub(t):
        rest = t.shape[4:]
        t = t.reshape((b, dil, L) + rest).swapaxes(1, 2)
        return t.reshape((b, s_pad) + rest)[:, :s]

    return from_sub(o), from_sub(lse)


def _fwd_setup_inputs(seed: int = 0) -> dict:
    key = jax.random.key(seed)
    ks = jax.random.split(key, 17)
    n = lambda k, shape: jax.random.normal(k, shape, dtype=jnp.float32)
    return {
        "x": n(ks[0], (BATCH, SEQ, D_MODEL)),
        "p": n(ks[1], (DEPTH, BATCH, SEQ, D_PLE)),
        "g_pre": 1.0 + 0.02 * n(ks[2], (DEPTH, D_MODEL)),
        "w_in": n(ks[3], (DEPTH, D_MODEL, D_IN)) * D_MODEL ** -0.5,
        "w_s": n(ks[4], (DEPTH, G_A, CHUNK, CHUNK)) * CHUNK ** -0.5,
        "b_s": 1.0 + 0.1 * n(ks[5], (DEPTH, G_A, CHUNK)),
        "ln_v_g": 1.0 + 0.02 * n(ks[6], (DEPTH, W_A)),
        "ln_v_b": 0.02 * n(ks[7], (DEPTH, W_A)),
        "g_q": 1.0 + 0.02 * n(ks[8], (DEPTH, HEAD_DIM)),
        "g_k": 1.0 + 0.02 * n(ks[9], (DEPTH, HEAD_DIM)),
        "rel_bias": 0.1 * n(ks[10], (NUM_BUCKETS, H_B)),
        "g_out_a": 1.0 + 0.02 * n(ks[11], (DEPTH, W_A)),
        "g_out_b": 1.0 + 0.02 * n(ks[12], (DEPTH, W_B)),
        "w_out": n(ks[13], (DEPTH, D_MIX, D_MODEL)) * D_MIX ** -0.5,
        "g_ple": 1.0 + 0.02 * n(ks[14], (DEPTH, D_MODEL)),
        "w_ple_gate": n(ks[15], (DEPTH, D_MODEL, D_MODEL)) * D_MODEL ** -0.5,
        "w_ple_up": n(ks[16], (DEPTH, D_PLE, D_MODEL)) * D_PLE ** -0.5,
    }


def _fwd_reference(x, p, g_pre, w_in, w_s, b_s, ln_v_g, ln_v_b, g_q, g_k, rel_bias,
              g_out_a, g_out_b, w_out, g_ple, w_ple_gate, w_ple_up):
    b, s, _ = x.shape
    nc = s // CHUNK
    split_idx = [W_A, 2 * W_A, 3 * W_A, 3 * W_A + W_B, 3 * W_A + 2 * W_B, 3 * W_A + 3 * W_B]
    causal = jnp.tril(jnp.ones((CHUNK, CHUNK), dtype=w_s.dtype))
    for i in range(DEPTH):
        hn = _rmsnorm(x, g_pre[i])
        proj = hn @ w_in[i]
        a_u, a_v, a_z, q, k, v, b_z = jnp.split(proj, split_idx, axis=-1)

        a_u = jax.nn.gelu(a_u, approximate=False)
        a_v = _layernorm(jax.nn.gelu(a_v, approximate=False), ln_v_g[i], ln_v_b[i])
        vg = a_v.reshape(b, nc, CHUNK, G_A, HEAD_DIM)
        z = jnp.einsum('gts,bnsgc->bntgc', w_s[i] * causal, vg) \
            + b_s[i].T[None, None, :, :, None]
        y_a = a_u * z.reshape(b, s, W_A)

        qh = _rmsnorm(q.reshape(b, s, H_B, HEAD_DIM), g_q[i]).astype(jnp.float32) * (HEAD_DIM ** -0.5)
        kh = _rmsnorm(k.reshape(b, s, H_B, HEAD_DIM), g_k[i]).astype(jnp.float32)
        vh = v.reshape(b, s, H_B, HEAD_DIM).astype(jnp.float32)
        outs, lses = [], []
        for window, dil in DILATED:
            o_g, l_g = _dilated_window(qh, kh, vh, rel_bias, window, dil)
            outs.append(o_g)
            lses.append(l_g)
        alpha = jax.nn.softmax(jnp.stack(lses, axis=0), axis=0)
        y_b = jnp.sum(alpha[..., None] * jnp.stack(outs, axis=0), axis=0)
        y_b = y_b.reshape(b, s, W_B).astype(x.dtype)

        y = jnp.concatenate([_rmsnorm(y_a, g_out_a[i]) * jax.nn.silu(a_z),
                             _rmsnorm(y_b, g_out_b[i]) * jax.nn.silu(b_z)], axis=-1)
        h = x + y @ w_out[i]

        gate = jax.nn.sigmoid(_rmsnorm(h, g_ple[i]) @ w_ple_gate[i])
        x = h + gate * (p[i] @ w_ple_up[i])
    return x


import jax as _jax
import jax.numpy as _jnp

TWIN_FORMAT = 'train_step'
FWD_PARAMS = ['x', 'p', 'g_pre', 'w_in', 'w_s', 'b_s', 'ln_v_g', 'ln_v_b', 'g_q', 'g_k', 'rel_bias', 'g_out_a', 'g_out_b', 'w_out', 'g_ple', 'w_ple_gate', 'w_ple_up']
TWIN_WEIGHTS = ['g_pre', 'w_in', 'w_s', 'b_s', 'ln_v_g', 'ln_v_b', 'g_q', 'g_k', 'rel_bias', 'g_out_a', 'g_out_b', 'w_out', 'g_ple', 'w_ple_gate', 'w_ple_up']
TWIN_DIFF_INPUT = 'x'
TWIN_INPUTS = ['x', 'p', 'g_pre', 'w_in', 'w_s', 'b_s', 'ln_v_g', 'ln_v_b', 'g_q', 'g_k', 'rel_bias', 'g_out_a', 'g_out_b', 'w_out', 'g_ple', 'w_ple_gate', 'w_ple_up', 'loss_target', 'm_g_pre', 'm_w_in', 'm_w_s', 'm_b_s', 'm_ln_v_g', 'm_ln_v_b', 'm_g_q', 'm_g_k', 'm_rel_bias', 'm_g_out_a', 'm_g_out_b', 'm_w_out', 'm_g_ple', 'm_w_ple_gate', 'm_w_ple_up', 'v_g_pre', 'v_w_in', 'v_w_s', 'v_b_s', 'v_ln_v_g', 'v_ln_v_b', 'v_g_q', 'v_g_k', 'v_rel_bias', 'v_g_out_a', 'v_g_out_b', 'v_w_out', 'v_g_ple', 'v_w_ple_gate', 'v_w_ple_up']
TWIN_OUTPUTS = ['loss', 'grad_x', 'grad_g_pre', 'grad_w_in', 'grad_w_s', 'grad_b_s', 'grad_ln_v_g', 'grad_ln_v_b', 'grad_g_q', 'grad_g_k', 'grad_rel_bias', 'grad_g_out_a', 'grad_g_out_b', 'grad_w_out', 'grad_g_ple', 'grad_w_ple_gate', 'grad_w_ple_up', 'delta_g_pre', 'delta_w_in', 'delta_w_s', 'delta_b_s', 'delta_ln_v_g', 'delta_ln_v_b', 'delta_g_q', 'delta_g_k', 'delta_rel_bias', 'delta_g_out_a', 'delta_g_out_b', 'delta_w_out', 'delta_g_ple', 'delta_w_ple_gate', 'delta_w_ple_up', 'new_m_g_pre', 'new_m_w_in', 'new_m_w_s', 'new_m_b_s', 'new_m_ln_v_g', 'new_m_ln_v_b', 'new_m_g_q', 'new_m_g_k', 'new_m_rel_bias', 'new_m_g_out_a', 'new_m_g_out_b', 'new_m_w_out', 'new_m_g_ple', 'new_m_w_ple_gate', 'new_m_w_ple_up', 'new_v_g_pre', 'new_v_w_in', 'new_v_w_s', 'new_v_b_s', 'new_v_ln_v_g', 'new_v_ln_v_b', 'new_v_g_q', 'new_v_g_k', 'new_v_rel_bias', 'new_v_g_out_a', 'new_v_g_out_b', 'new_v_w_out', 'new_v_g_ple', 'new_v_w_ple_gate', 'new_v_w_ple_up']
TWIN_LEAF_KINDS = {'loss': 'loss', 'grad_x': 'grad_x', 'grad_g_pre': 'grad_w', 'grad_w_in': 'grad_w', 'grad_w_s': 'grad_w', 'grad_b_s': 'grad_w', 'grad_ln_v_g': 'grad_w', 'grad_ln_v_b': 'grad_w', 'grad_g_q': 'grad_w', 'grad_g_k': 'grad_w', 'grad_rel_bias': 'grad_w', 'grad_g_out_a': 'grad_w', 'grad_g_out_b': 'grad_w', 'grad_w_out': 'grad_w', 'grad_g_ple': 'grad_w', 'grad_w_ple_gate': 'grad_w', 'grad_w_ple_up': 'grad_w', 'delta_g_pre': 'delta_w', 'delta_w_in': 'delta_w', 'delta_w_s': 'delta_w', 'delta_b_s': 'delta_w', 'delta_ln_v_g': 'delta_w', 'delta_ln_v_b': 'delta_w', 'delta_g_q': 'delta_w', 'delta_g_k': 'delta_w', 'delta_rel_bias': 'delta_w', 'delta_g_out_a': 'delta_w', 'delta_g_out_b': 'delta_w', 'delta_w_out': 'delta_w', 'delta_g_ple': 'delta_w', 'delta_w_ple_gate': 'delta_w', 'delta_w_ple_up': 'delta_w', 'new_m_g_pre': 'new_m', 'new_m_w_in': 'new_m', 'new_m_w_s': 'new_m', 'new_m_b_s': 'new_m', 'new_m_ln_v_g': 'new_m', 'new_m_ln_v_b': 'new_m', 'new_m_g_q': 'new_m', 'new_m_g_k': 'new_m', 'new_m_rel_bias': 'new_m', 'new_m_g_out_a': 'new_m', 'new_m_g_out_b': 'new_m', 'new_m_w_out': 'new_m', 'new_m_g_ple': 'new_m', 'new_m_w_ple_gate': 'new_m', 'new_m_w_ple_up': 'new_m', 'new_v_g_pre': 'new_v', 'new_v_w_in': 'new_v', 'new_v_w_s': 'new_v', 'new_v_b_s': 'new_v', 'new_v_ln_v_g': 'new_v', 'new_v_ln_v_b': 'new_v', 'new_v_g_q': 'new_v', 'new_v_g_k': 'new_v', 'new_v_rel_bias': 'new_v', 'new_v_g_out_a': 'new_v', 'new_v_g_out_b': 'new_v', 'new_v_w_out': 'new_v', 'new_v_g_ple': 'new_v', 'new_v_w_ple_gate': 'new_v', 'new_v_w_ple_up': 'new_v'}


def _forward(args):
    return _fwd_reference(*[args[k] for k in FWD_PARAMS])


def _output_shape():
    out = _jax.eval_shape(lambda: _forward(_fwd_setup_inputs(0)))
    return out.shape, out.dtype

N_MICROBATCH = 1
ADAM_LR = 0.001
ADAM_B1 = 0.9
ADAM_B2 = 0.999
ADAM_EPS = 1e-08
ADAM_WD = 0.01
ADAM_STEP = 10
PER_EXAMPLE_BATCH_AXIS = {'x': 0, 'p': 1, 'loss_target': 0}
SHARED_INPUTS = []
_WEIGHT_DTYPES = {'g_pre': _jnp.float32, 'w_in': _jnp.float32, 'w_s': _jnp.float32, 'b_s': _jnp.float32, 'ln_v_g': _jnp.float32, 'ln_v_b': _jnp.float32, 'g_q': _jnp.float32, 'g_k': _jnp.float32, 'rel_bias': _jnp.float32, 'g_out_a': _jnp.float32, 'g_out_b': _jnp.float32, 'w_out': _jnp.float32, 'g_ple': _jnp.float32, 'w_ple_gate': _jnp.float32, 'w_ple_up': _jnp.float32}
MOMENT_SCALE = {'g_pre': 3.242457e+00, 'w_in': 8.013373e-02, 'w_s': 6.485647e-02, 'b_s': 9.343013e-02, 'ln_v_g': 8.210199e-02, 'ln_v_b': 6.379336e-02, 'g_q': 2.315721e-01, 'g_k': 2.317333e-01, 'rel_bias': 1.097490e-01, 'g_out_a': 2.685251e+00, 'g_out_b': 2.797562e+00, 'w_out': 5.720139e-02, 'g_ple': 2.370929e-01, 'w_ple_gate': 2.830269e-02, 'w_ple_up': 1.052111e-01}


def _to_microbatches(a, axis):
    t = _jnp.moveaxis(a, axis, 0)
    t = t.reshape((N_MICROBATCH, t.shape[0] // N_MICROBATCH) + t.shape[1:])
    return _jnp.moveaxis(t, 1, axis + 1)


def setup_inputs(seed: int = 0) -> dict:
    inp = _fwd_setup_inputs(seed)
    key = _jax.random.fold_in(_jax.random.key(seed), 7919)
    shape, _ = _output_shape()
    out = dict(inp)
    out["loss_target"] = _jax.random.normal(_jax.random.fold_in(key, 0), shape, _jnp.float32)
    for i, name in enumerate(TWIN_WEIGHTS):
        w = inp[name].astype(_jnp.float32)
        if MOMENT_SCALE is None:
            s = _jnp.sqrt(_jnp.mean(_jnp.square(w)) + 1e-30)
        else:
            s = MOMENT_SCALE[name]
        km, kv = _jax.random.split(_jax.random.fold_in(key, i + 1))
        out[name] = w
        out["m_" + name] = s * _jax.random.normal(km, w.shape, _jnp.float32)
        out["v_" + name] = (s * s) * _jax.random.uniform(kv, w.shape, _jnp.float32, 0.5, 1.5)
    if N_MICROBATCH > 1:
        for name, axis in PER_EXAMPLE_BATCH_AXIS.items():
            out[name] = _to_microbatches(out[name], axis)
    return {'x': out['x'], 'p': out['p'], 'g_pre': out['g_pre'], 'w_in': out['w_in'], 'w_s': out['w_s'], 'b_s': out['b_s'], 'ln_v_g': out['ln_v_g'], 'ln_v_b': out['ln_v_b'], 'g_q': out['g_q'], 'g_k': out['g_k'], 'rel_bias': out['rel_bias'], 'g_out_a': out['g_out_a'], 'g_out_b': out['g_out_b'], 'w_out': out['w_out'], 'g_ple': out['g_ple'], 'w_ple_gate': out['w_ple_gate'], 'w_ple_up': out['w_ple_up'], 'loss_target': out['loss_target'], 'm_g_pre': out['m_g_pre'], 'm_w_in': out['m_w_in'], 'm_w_s': out['m_w_s'], 'm_b_s': out['m_b_s'], 'm_ln_v_g': out['m_ln_v_g'], 'm_ln_v_b': out['m_ln_v_b'], 'm_g_q': out['m_g_q'], 'm_g_k': out['m_g_k'], 'm_rel_bias': out['m_rel_bias'], 'm_g_out_a': out['m_g_out_a'], 'm_g_out_b': out['m_g_out_b'], 'm_w_out': out['m_w_out'], 'm_g_ple': out['m_g_ple'], 'm_w_ple_gate': out['m_w_ple_gate'], 'm_w_ple_up': out['m_w_ple_up'], 'v_g_pre': out['v_g_pre'], 'v_w_in': out['v_w_in'], 'v_w_s': out['v_w_s'], 'v_b_s': out['v_b_s'], 'v_ln_v_g': out['v_ln_v_g'], 'v_ln_v_b': out['v_ln_v_b'], 'v_g_q': out['v_g_q'], 'v_g_k': out['v_g_k'], 'v_rel_bias': out['v_rel_bias'], 'v_g_out_a': out['v_g_out_a'], 'v_g_out_b': out['v_g_out_b'], 'v_w_out': out['v_w_out'], 'v_g_ple': out['v_g_ple'], 'v_w_ple_gate': out['v_w_ple_gate'], 'v_w_ple_up': out['v_w_ple_up']}


def _loss(weights, diff, rest, loss_target):
    with _jax.named_scope("forward"):
        args = {**rest, TWIN_DIFF_INPUT: diff, **{k: w.astype(_WEIGHT_DTYPES[k]) for k, w in weights.items()}}
        y = _forward(args)
    with _jax.named_scope("loss_head"):
        err = _jnp.square(y.astype(_jnp.float32) - loss_target)
        return 0.5 * _jnp.sum(_jnp.mean(err, axis=-1)) if err.ndim else 0.5 * err


def _adamw(w, g, m, v):
    m = ADAM_B1 * m + (1.0 - ADAM_B1) * g
    v = ADAM_B2 * v + (1.0 - ADAM_B2) * _jnp.square(g)
    m_hat = m / (1.0 - ADAM_B1 ** ADAM_STEP)
    v_hat = v / (1.0 - ADAM_B2 ** ADAM_STEP)
    delta = -ADAM_LR * (m_hat / (_jnp.sqrt(v_hat) + ADAM_EPS) + ADAM_WD * w)
    return delta, m, v


def reference(x, p, g_pre, w_in, w_s, b_s, ln_v_g, ln_v_b, g_q, g_k, rel_bias, g_out_a, g_out_b, w_out, g_ple, w_ple_gate, w_ple_up, loss_target, m_g_pre, m_w_in, m_w_s, m_b_s, m_ln_v_g, m_ln_v_b, m_g_q, m_g_k, m_rel_bias, m_g_out_a, m_g_out_b, m_w_out, m_g_ple, m_w_ple_gate, m_w_ple_up, v_g_pre, v_w_in, v_w_s, v_b_s, v_ln_v_g, v_ln_v_b, v_g_q, v_g_k, v_rel_bias, v_g_out_a, v_g_out_b, v_w_out, v_g_ple, v_w_ple_gate, v_w_ple_up):
    given = dict(x=x, p=p, g_pre=g_pre, w_in=w_in, w_s=w_s, b_s=b_s, ln_v_g=ln_v_g, ln_v_b=ln_v_b, g_q=g_q, g_k=g_k, rel_bias=rel_bias, g_out_a=g_out_a, g_out_b=g_out_b, w_out=w_out, g_ple=g_ple, w_ple_gate=w_ple_gate, w_ple_up=w_ple_up, loss_target=loss_target, m_g_pre=m_g_pre, m_w_in=m_w_in, m_w_s=m_w_s, m_b_s=m_b_s, m_ln_v_g=m_ln_v_g, m_ln_v_b=m_ln_v_b, m_g_q=m_g_q, m_g_k=m_g_k, m_rel_bias=m_rel_bias, m_g_out_a=m_g_out_a, m_g_out_b=m_g_out_b, m_w_out=m_w_out, m_g_ple=m_g_ple, m_w_ple_gate=m_w_ple_gate, m_w_ple_up=m_w_ple_up, v_g_pre=v_g_pre, v_w_in=v_w_in, v_w_s=v_w_s, v_b_s=v_b_s, v_ln_v_g=v_ln_v_g, v_ln_v_b=v_ln_v_b, v_g_q=v_g_q, v_g_k=v_g_k, v_rel_bias=v_rel_bias, v_g_out_a=v_g_out_a, v_g_out_b=v_g_out_b, v_w_out=v_w_out, v_g_ple=v_g_ple, v_w_ple_gate=v_w_ple_gate, v_w_ple_up=v_w_ple_up)
    weights = {n: given[n] for n in TWIN_WEIGHTS}
    shared = {n: given[n] for n in SHARED_INPUTS}
    per_example = {n: given[n] for n in ['x', 'p']}
    grad_fn = _jax.value_and_grad(_loss, argnums=(0, 1))

    def one_microbatch(ex, loss_target):
        ex = dict(ex)
        diff = ex.pop(TWIN_DIFF_INPUT)
        return grad_fn(weights, diff, {**shared, **ex}, loss_target)

    if N_MICROBATCH == 1:
        loss, (grad_w, grad_x) = one_microbatch(per_example, given["loss_target"])
    else:
        def body(carry, xs):
            loss_sum, grad_sum = carry
            l_k, (gw_k, gx_k) = one_microbatch(xs[0], xs[1])
            with _jax.named_scope("update"):
                return (loss_sum + l_k, _jax.tree.map(_jnp.add, grad_sum, gw_k)), gx_k

        init = (_jnp.zeros((), _jnp.float32), _jax.tree.map(_jnp.zeros_like, weights))
        (loss, grad_w), grad_x = _jax.lax.scan(body, init, (per_example, given["loss_target"]))
    with _jax.named_scope("update"):
        delta_w, new_m, new_v = {}, {}, {}
        for n in TWIN_WEIGHTS:
            delta_w[n], new_m[n], new_v[n] = _adamw(weights[n], grad_w[n], given["m_" + n], given["v_" + n])
    return (loss, grad_x, *[grad_w[n] for n in TWIN_WEIGHTS], *[delta_w[n] for n in TWIN_WEIGHTS],
            *[new_m[n] for n in TWIN_WEIGHTS], *[new_v[n] for n in TWIN_WEIGHTS])
```

```python
import functools
import math

import numpy as np

import jax
import jax.numpy as jnp
from jax import lax
from jax.experimental import pallas as pl
from jax.experimental.pallas import tpu as pltpu

F32 = jnp.float32
BF16 = jnp.bfloat16

HEAD_DIM = 128
CHUNK = 128
BLK = 128
DILATED = ((128, 1), (512, 4), (2048, 16))
NUM_BUCKETS = 32
MAX_DISTANCE = 2048
EPS = 1e-6
NEG_INF = -1e30
N_CHIPS = 4
N_DEV = 8
MESH_AXES = ("x", "y", "c")

ADAM_LR = 0.001
ADAM_B1 = 0.9
ADAM_B2 = 0.999
ADAM_EPS = 1e-08
ADAM_WD = 0.01
ADAM_STEP = 10

V7X_VMEM_LIMIT = 56 * 1024 * 1024
LANE = 128
MESH = pl.DeviceIdType.MESH
ANY = pl.BlockSpec(memory_space=pl.ANY)


def _params(*sem):
    return pltpu.CompilerParams(dimension_semantics=sem or None, vmem_limit_bytes=V7X_VMEM_LIMIT)


def _tile(n, target):
    if n <= target:
        return n
    t = (target // LANE) * LANE
    while t > LANE and n % t:
        t -= LANE
    assert n % t == 0, (n, target)
    return t


def _gelu(x):
    return 0.5 * x * (1.0 + lax.erf(x * (1.0 / math.sqrt(2.0))))


def _gelu_grad(x):
    return 0.5 * (1.0 + lax.erf(x * (1.0 / math.sqrt(2.0)))) + x * jnp.exp(-0.5 * x * x) * (1.0 / math.sqrt(2.0 * math.pi))


def _silu_and_grad(x):
    s = jax.nn.sigmoid(x)
    return x * s, s * (1.0 + x * (1.0 - s))


def _dot(a, b, ta=False, tb=False):
    return lax.dot_general(a, b, (((0 if ta else 1,), (1 if tb else 0,)), ((), ())), preferred_element_type=F32)


def _colsum8(v):
    return jnp.sum(v.reshape(v.shape[0] // 8, 8, v.shape[1]), axis=0)


def _matmul(name, a, b, a_spec, b_spec, grid, acc_shape, out_shape, out_specs, ta=False, tb=False,
            extras=(), extra_specs=(), epilogue=None):
    nk = grid[2]
    n_extra = len(extras)
    single = not isinstance(out_shape, (tuple, list))
    outs_shape = (out_shape,) if single else tuple(out_shape)
    outs_specs = (out_specs,) if single else tuple(out_specs)
    n_out = len(outs_shape)

    def finish(acc, extra_refs, out_refs):
        if epilogue is None:
            out_refs[0][...] = acc.astype(out_refs[0].dtype)
        else:
            epilogue(acc, extra_refs, out_refs)

    if nk == 1:
        def body(a_ref, b_ref, *rest):
            finish(_dot(a_ref[...], b_ref[...], ta, tb), rest[:n_extra], rest[n_extra:n_extra + n_out])
        scratch = []
    else:
        def body(a_ref, b_ref, *rest):
            acc_ref = rest[-1]
            k = pl.program_id(2)

            @pl.when(k == 0)
            def _():
                acc_ref[...] = jnp.zeros_like(acc_ref)

            acc_ref[...] += _dot(a_ref[...], b_ref[...], ta, tb)

            @pl.when(k == nk - 1)
            def _():
                finish(acc_ref[...], rest[:n_extra], rest[n_extra:n_extra + n_out])
        scratch = [pltpu.VMEM(acc_shape, F32)]

    res = pl.pallas_call(
        body, name=name, grid=grid,
        in_specs=[a_spec, b_spec, *extra_specs],
        out_specs=outs_specs, out_shape=outs_shape, scratch_shapes=scratch,
        compiler_params=_params("parallel", "parallel", "arbitrary"),
    )(a, b, *extras)
    return res[0] if single else res


def _shard_spec(rows, cols, per_shard, row_axis, col_axis):
    def index(i, j, k):
        g = (i, j, k)
        return (g[col_axis] // per_shard, g[row_axis], g[col_axis] % per_shard)
    return pl.BlockSpec((None, rows, cols), index)


def _rms_fwd(name, x, g):
    s, d = x.shape
    tm = _tile(s, 256)

    def body(x_ref, g_ref, y_ref, r_ref):
        xf = x_ref[...]
        r = lax.rsqrt(jnp.mean(xf * xf, axis=-1, keepdims=True) + EPS)
        y_ref[...] = (xf * r * g_ref[...]).astype(BF16)
        r_ref[...] = r

    return pl.pallas_call(
        body, name=name, grid=(s // tm,),
        in_specs=[pl.BlockSpec((tm, d), lambda i: (i, 0)), pl.BlockSpec((1, d), lambda i: (0, 0))],
        out_specs=(pl.BlockSpec((tm, d), lambda i: (i, 0)), pl.BlockSpec((tm, 1), lambda i: (i, 0))),
        out_shape=(jax.ShapeDtypeStruct((s, d), BF16), jax.ShapeDtypeStruct((s, 1), F32)),
        compiler_params=_params("parallel"),
    )(x, g)


def _rms_bwd(name, dy, x, r, g, skip, with_bf16):
    s, d = x.shape
    tm = _tile(s, 256)
    n = s // tm

    def body(dy_ref, x_ref, r_ref, g_ref, skip_ref, *outs):
        dx_ref = outs[0]
        dg_ref = outs[-2]
        acc_ref = outs[-1]
        i = pl.program_id(0)
        dyv, xv, rv = dy_ref[...], x_ref[...], r_ref[...]

        @pl.when(i == 0)
        def _():
            acc_ref[...] = jnp.zeros_like(acc_ref)

        acc_ref[...] += _colsum8(dyv * xv * rv)
        dg = dyv * g_ref[...]
        dx = skip_ref[...] + rv * (dg - xv * (rv * rv) * jnp.mean(dg * xv, axis=-1, keepdims=True))
        dx_ref[...] = dx
        if with_bf16:
            outs[1][...] = dx.astype(BF16)

        @pl.when(i == n - 1)
        def _():
            dg_ref[...] = jnp.sum(acc_ref[...], axis=0, keepdims=True)

    row = pl.BlockSpec((tm, d), lambda i: (i, 0))
    vec = pl.BlockSpec((1, d), lambda i: (0, 0))
    out_specs = [row] + ([row] if with_bf16 else []) + [vec]
    out_shape = [jax.ShapeDtypeStruct((s, d), F32)] + ([jax.ShapeDtypeStruct((s, d), BF16)] if with_bf16 else []) \
        + [jax.ShapeDtypeStruct((1, d), F32)]
    return pl.pallas_call(
        body, name=name, grid=(n,),
        in_specs=[row, row, pl.BlockSpec((tm, 1), lambda i: (i, 0)), vec, row],
        out_specs=out_specs, out_shape=out_shape, scratch_shapes=[pltpu.VMEM((8, d), F32)],
        compiler_params=_params("arbitrary"),
    )(dy, x, r, g, skip)


def _causal(w):
    t = lax.broadcasted_iota(jnp.int32, w.shape, 0)
    s_ = lax.broadcasted_iota(jnp.int32, w.shape, 1)
    return jnp.where(t >= s_, w, 0.0)


def _gmlp_fwd(proj, w_s, b_st, ln_g, ln_b, g_out, d_model):
    s = proj.shape[0]
    wa = d_model // 2
    groups = wa // HEAD_DIM
    tm = _tile(s, 256)
    n_chunks = tm // CHUNK

    def body(au_ref, av_ref, az_ref, ws_ref, bst_ref, lng_ref, lnb_ref, go_ref,
             y_ref, z_ref, vn_ref, mu_ref, rs_ref, ra_ref):
        gv = _gelu(av_ref[...])
        mu = jnp.mean(gv, axis=-1, keepdims=True)
        xc = gv - mu
        rs = lax.rsqrt(jnp.mean(xc * xc, axis=-1, keepdims=True) + EPS)
        vn = (xc * rs * lng_ref[...] + lnb_ref[...]).astype(BF16)
        vn_ref[...] = vn
        mu_ref[...] = mu
        rs_ref[...] = rs
        for g in range(groups):
            wm = _causal(ws_ref[g]).astype(BF16)
            cols = slice(g * HEAD_DIM, (g + 1) * HEAD_DIM)
            for ch in range(n_chunks):
                rows = slice(ch * CHUNK, (ch + 1) * CHUNK)
                z_ref[rows, cols] = _dot(wm, vn_ref[rows, cols]) + bst_ref[:, g:g + 1]
        ya = _gelu(au_ref[...]) * z_ref[...]
        ra = lax.rsqrt(jnp.mean(ya * ya, axis=-1, keepdims=True) + EPS)
        ra_ref[...] = ra
        sz, _ = _silu_and_grad(az_ref[...])
        y_ref[...] = (ya * ra * go_ref[...] * sz).astype(BF16)

    def col(j):
        return pl.BlockSpec((tm, wa), lambda i: (i, j))
    vec = pl.BlockSpec((1, wa), lambda i: (0, 0))
    stat = pl.BlockSpec((tm, 1), lambda i: (i, 0))
    return pl.pallas_call(
        body, name="gmlp_fwd", grid=(s // tm,),
        in_specs=[col(0), col(1), col(2),
                  pl.BlockSpec((groups, CHUNK, CHUNK), lambda i: (0, 0, 0)),
                  pl.BlockSpec((CHUNK, groups), lambda i: (0, 0)), vec, vec, vec],
        out_specs=(col(0), col(0), col(0), stat, stat, stat),
        out_shape=(jax.ShapeDtypeStruct((s, d_model), BF16), jax.ShapeDtypeStruct((s, wa), F32),
                   jax.ShapeDtypeStruct((s, wa), BF16), jax.ShapeDtypeStruct((s, 1), F32),
                   jax.ShapeDtypeStruct((s, 1), F32), jax.ShapeDtypeStruct((s, 1), F32)),
        compiler_params=_params("parallel"),
    )(proj, proj, proj, w_s, b_st, ln_g, ln_b, g_out)


def _gmlp_bwd(dproj, dy, proj, z, vn, mu, rs, ra, w_s, ln_g, g_out, d_model):
    s = proj.shape[0]
    wa = d_model // 2
    groups = wa // HEAD_DIM
    tm = _tile(s, 256)
    n_chunks = tm // CHUNK
    n = s // tm

    def causal_stack(w):
        t = lax.broadcasted_iota(jnp.int32, w.shape, 1)
        s_ = lax.broadcasted_iota(jnp.int32, w.shape, 2)
        return jnp.where(t >= s_, w, 0.0)

    def body(dproj_in, dy_ref, au_ref, av_ref, az_ref, z_ref, vn_ref, mu_ref, rs_ref, ra_ref, ws_ref, lng_ref, go_ref,
             dp_ref, gws_ref, dzs_ref, glg_ref, glb_ref, ggo_ref,
             dz_s, dvn_s, acc_lg, acc_lb, acc_go):
        del dproj_in
        i = pl.program_id(0)

        @pl.when(i == 0)
        def _():
            gws_ref[...] = jnp.zeros_like(gws_ref)
            dzs_ref[...] = jnp.zeros_like(dzs_ref)
            acc_lg[...] = jnp.zeros_like(acc_lg)
            acc_lb[...] = jnp.zeros_like(acc_lb)
            acc_go[...] = jnp.zeros_like(acc_go)

        au, az, zv, rav = au_ref[...], az_ref[...], z_ref[...], ra_ref[...]
        u = _gelu(au)
        ya = u * zv
        sz, dsz = _silu_and_grad(az)
        dyv = dy_ref[...]
        dp_ref[:, 2 * wa:3 * wa] = (dyv * (ya * rav * go_ref[...]) * dsz).astype(BF16)
        dn = dyv * sz
        acc_go[...] += _colsum8(dn * ya * rav)
        dyg = dn * go_ref[...]
        dya = rav * (dyg - ya * (rav * rav) * jnp.mean(dyg * ya, axis=-1, keepdims=True))
        dp_ref[:, 0:wa] = (dya * zv * _gelu_grad(au)).astype(BF16)
        dz_s[...] = dya * u
        for ch in range(n_chunks):
            dzs_ref[...] += dz_s[ch * CHUNK:(ch + 1) * CHUNK, :]
        for g in range(groups):
            wm = _causal(ws_ref[g]).astype(BF16)
            cols = slice(g * HEAD_DIM, (g + 1) * HEAD_DIM)
            for ch in range(n_chunks):
                rows = slice(ch * CHUNK, (ch + 1) * CHUNK)
                dzb = dz_s[rows, cols].astype(BF16)
                gws_ref[g] += _dot(dzb, vn_ref[rows, cols], tb=True)
                dvn_s[rows, cols] = _dot(wm, dzb, ta=True)
        av = av_ref[...]
        xh = (_gelu(av) - mu_ref[...]) * rs_ref[...]
        dvn = dvn_s[...]
        acc_lb[...] += _colsum8(dvn)
        acc_lg[...] += _colsum8(dvn * xh)
        dxh = dvn * lng_ref[...]
        dgv = rs_ref[...] * (dxh - jnp.mean(dxh, axis=-1, keepdims=True) - xh * jnp.mean(dxh * xh, axis=-1, keepdims=True))
        dp_ref[:, wa:2 * wa] = (dgv * _gelu_grad(av)).astype(BF16)

        @pl.when(i == n - 1)
        def _():
            gws_ref[...] = causal_stack(gws_ref[...])
            glg_ref[...] = jnp.sum(acc_lg[...], axis=0, keepdims=True)
            glb_ref[...] = jnp.sum(acc_lb[...], axis=0, keepdims=True)
            ggo_ref[...] = jnp.sum(acc_go[...], axis=0, keepdims=True)

    def col(j):
        return pl.BlockSpec((tm, wa), lambda i: (i, j))
    vec = pl.BlockSpec((1, wa), lambda i: (0, 0))
    stat = pl.BlockSpec((tm, 1), lambda i: (i, 0))
    ws_spec = pl.BlockSpec((groups, CHUNK, CHUNK), lambda i: (0, 0, 0))
    d_in = dproj.shape[1]
    return pl.pallas_call(
        body, name="gmlp_bwd", grid=(n,),
        in_specs=[ANY, col(0), col(0), col(1), col(2), col(0), col(0), stat, stat, stat, ws_spec, vec, vec],
        out_specs=(pl.BlockSpec((tm, 3 * wa), lambda i: (i, 0)), ws_spec,
                   pl.BlockSpec((CHUNK, wa), lambda i: (0, 0)), vec, vec, vec),
        out_shape=(jax.ShapeDtypeStruct((s, d_in), BF16),
                   jax.ShapeDtypeStruct((groups, CHUNK, CHUNK), F32), jax.ShapeDtypeStruct((CHUNK, wa), F32))
        + (jax.ShapeDtypeStruct((1, wa), F32),) * 3,
        scratch_shapes=[pltpu.VMEM((tm, wa), F32), pltpu.VMEM((tm, wa), F32)] + [pltpu.VMEM((8, wa), F32)] * 3,
        input_output_aliases={0: 0},
        compiler_params=_params("arbitrary"),
    )(dproj, dy, proj, proj, proj, z, vn, mu, rs, ra, w_s, ln_g, g_out)


def _gate_fwd(y_in, yb, proj, g_out, d_model):
    s = proj.shape[0]
    wa = d_model // 2
    tm = _tile(s, 256)

    def body(y_any, yb_ref, bz_ref, go_ref, y_ref, rb_ref):
        del y_any
        ybv = yb_ref[...]
        rb = lax.rsqrt(jnp.mean(ybv * ybv, axis=-1, keepdims=True) + EPS)
        rb_ref[...] = rb
        sz, _ = _silu_and_grad(bz_ref[...])
        y_ref[...] = (ybv * rb * go_ref[...] * sz).astype(BF16)

    return pl.pallas_call(
        body, name="gate_b_fwd", grid=(s // tm,),
        in_specs=[ANY, pl.BlockSpec((tm, wa), lambda i: (i, 0)), pl.BlockSpec((tm, wa), lambda i: (i, 6)),
                  pl.BlockSpec((1, wa), lambda i: (0, 0))],
        out_specs=(pl.BlockSpec((tm, wa), lambda i: (i, 1)), pl.BlockSpec((tm, 1), lambda i: (i, 0))),
        out_shape=(jax.ShapeDtypeStruct((s, d_model), BF16), jax.ShapeDtypeStruct((s, 1), F32)),
        input_output_aliases={0: 0},
        compiler_params=_params("parallel"),
    )(y_in, yb, proj, g_out)


def _gate_bwd(dy, yb, proj, rb, g_out, d_model):
    s, d_in = proj.shape
    wa = d_model // 2
    tm = _tile(s, 256)
    n = s // tm

    def body(dy_ref, yb_ref, bz_ref, rb_ref, go_ref, dp_ref, do_ref, ggo_ref, acc):
        i = pl.program_id(0)

        @pl.when(i == 0)
        def _():
            acc[...] = jnp.zeros_like(acc)

        dyv, ybv, rbv = dy_ref[...], yb_ref[...], rb_ref[...]
        sz, dsz = _silu_and_grad(bz_ref[...])
        dp_ref[...] = (dyv * (ybv * rbv * go_ref[...]) * dsz).astype(BF16)
        dn = dyv * sz
        acc[...] += _colsum8(dn * ybv * rbv)
        dyg = dn * go_ref[...]
        do_ref[...] = rbv * (dyg - ybv * (rbv * rbv) * jnp.mean(dyg * ybv, axis=-1, keepdims=True))

        @pl.when(i == n - 1)
        def _():
            ggo_ref[...] = jnp.sum(acc[...], axis=0, keepdims=True)

    vec = pl.BlockSpec((1, wa), lambda i: (0, 0))
    return pl.pallas_call(
        body, name="gate_b_bwd", grid=(n,),
        in_specs=[pl.BlockSpec((tm, wa), lambda i: (i, 1)), pl.BlockSpec((tm, wa), lambda i: (i, 0)),
                  pl.BlockSpec((tm, wa), lambda i: (i, 6)), pl.BlockSpec((tm, 1), lambda i: (i, 0)), vec],
        out_specs=(pl.BlockSpec((tm, wa), lambda i: (i, 6)), pl.BlockSpec((tm, wa), lambda i: (i, 0)), vec),
        out_shape=(jax.ShapeDtypeStruct((s, d_in), BF16), jax.ShapeDtypeStruct((s, wa), F32),
                   jax.ShapeDtypeStruct((1, wa), F32)),
        scratch_shapes=[pltpu.VMEM((8, wa), F32)],
        compiler_params=_params("arbitrary"),
    )(dy, yb, proj, rb, g_out)


def _bucket_tables():
    qi = BLK + np.arange(BLK)
    kj = np.arange(2 * BLK)
    delta = qi[:, None] - kj[None, :]
    max_exact = NUM_BUCKETS // 2
    tabs = []
    for window, dil in DILATED:
        band = (delta >= 0) & (delta <= window // dil)
        dist = np.clip(delta, 0, None) * dil
        d = np.maximum(dist, 1).astype(np.float32)
        large = max_exact + (np.log(d / np.float32(max_exact)) / np.float32(math.log(MAX_DISTANCE / max_exact))
                             * np.float32(NUM_BUCKETS - max_exact)).astype(np.int32)
        large = np.minimum(large, NUM_BUCKETS - 1)
        tabs.append(np.where(band, np.where(dist < max_exact, dist, large), -1).astype(np.int32))
    return np.stack(tabs)


def _bias_tiles(tab_ref, rb_ref, h, bias_s):
    for i in range(len(DILATED)):
        t = tab_ref[i]
        bias = jnp.zeros(t.shape, F32)
        for b in range(NUM_BUCKETS):
            bias = jnp.where(t == b, rb_ref[b, h], bias)
        bias_s[i] = jnp.where(t >= 0, bias, NEG_INF)


def _block_rows(b, n_blocks, dil):
    nb = n_blocks // dil
    r = b // nb
    n = b % nb
    start = r + dil * BLK * n
    pstart = jnp.where(n > 0, start - dil * BLK, start)
    if dil == 1:
        return pl.ds(pl.multiple_of(start, BLK), BLK), pl.ds(pl.multiple_of(pstart, BLK), BLK), n
    return pl.ds(start, BLK, stride=dil), pl.ds(pstart, BLK, stride=dil), n


def _attn_fwd(proj, g_q, g_k, rel_bias, d_model):
    s = proj.shape[0]
    heads = d_model // 2 // HEAD_DIM
    n_blocks = s // BLK
    scale = HEAD_DIM ** -0.5
    tables = jnp.asarray(_bucket_tables())

    def body(q_ref, k_ref, v_ref, gq_ref, gk_ref, rb_ref, tab_ref, yb_ref, lse_ref, qn, kn, m_s, l_s, acc_s, bias_s):
        h = pl.program_id(0)
        q = q_ref[...]
        qn[...] = q * lax.rsqrt(jnp.mean(q * q, axis=-1, keepdims=True) + EPS) * gq_ref[...] * scale
        k = k_ref[...]
        kn[...] = k * lax.rsqrt(jnp.mean(k * k, axis=-1, keepdims=True) + EPS) * gk_ref[...]
        _bias_tiles(tab_ref, rb_ref, h, bias_s)
        col = lax.broadcasted_iota(jnp.int32, (BLK, 2 * BLK), 1)

        for i, (_, dil) in enumerate(DILATED):
            def block(b, carry, i=i, dil=dil):
                rows, prows, n = _block_rows(b, n_blocks, dil)
                qb = qn[rows, :].astype(BF16)
                kc = jnp.concatenate([kn[prows, :], kn[rows, :]], axis=0).astype(BF16)
                vc = jnp.concatenate([v_ref[prows, :], v_ref[rows, :]], axis=0).astype(BF16)
                sc = _dot(qb, kc, tb=True) + bias_s[i]
                sc = jnp.where(jnp.logical_or(col >= BLK, n > 0), sc, NEG_INF)
                m_i = jnp.max(sc, axis=-1, keepdims=True)
                pr = jnp.exp(sc - m_i)
                l_i = jnp.sum(pr, axis=-1, keepdims=True)
                a_i = _dot(pr.astype(BF16), vc)
                if i == 0:
                    m_s[rows, :] = m_i
                    l_s[rows, :] = l_i
                    acc_s[rows, :] = a_i
                else:
                    m_o = m_s[rows, :]
                    m_n = jnp.maximum(m_o, m_i)
                    c_o = jnp.exp(m_o - m_n)
                    c_i = jnp.exp(m_i - m_n)
                    m_s[rows, :] = m_n
                    l_s[rows, :] = l_s[rows, :] * c_o + l_i * c_i
                    acc_s[rows, :] = acc_s[rows, :] * c_o + a_i * c_i
                return carry
            lax.fori_loop(0, n_blocks, block, 0)

        l = l_s[...]
        yb_ref[...] = acc_s[...] / l
        lse_ref[...] = m_s[...] + jnp.log(l)

    def head_col(off):
        return pl.BlockSpec((s, HEAD_DIM), lambda h: (0, off * heads + h))
    vec = pl.BlockSpec((1, HEAD_DIM), lambda h: (0, 0))
    return pl.pallas_call(
        body, name="attn_fwd", grid=(heads,),
        in_specs=[head_col(3), head_col(4), head_col(5), vec, vec,
                  pl.BlockSpec(memory_space=pltpu.SMEM),
                  pl.BlockSpec((len(DILATED), BLK, 2 * BLK), lambda h: (0, 0, 0))],
        out_specs=(pl.BlockSpec((s, HEAD_DIM), lambda h: (0, h)), pl.BlockSpec((None, s, 1), lambda h: (h, 0, 0))),
        out_shape=(jax.ShapeDtypeStruct((s, heads * HEAD_DIM), F32), jax.ShapeDtypeStruct((heads, s, 1), F32)),
        scratch_shapes=[pltpu.VMEM((s, HEAD_DIM), F32), pltpu.VMEM((s, HEAD_DIM), F32),
                        pltpu.VMEM((s, 1), F32), pltpu.VMEM((s, 1), F32), pltpu.VMEM((s, HEAD_DIM), F32),
                        pltpu.VMEM((len(DILATED), BLK, 2 * BLK), F32)],
        compiler_params=_params("arbitrary"),
    )(proj, proj, proj, g_q, g_k, rel_bias, tables)


def _attn_bwd(dproj, d_o, yb, lse, proj, g_q, g_k, rel_bias, d_model):
    s, d_in = proj.shape
    heads = d_model // 2 // HEAD_DIM
    n_blocks = s // BLK
    scale = HEAD_DIM ** -0.5
    tables = jnp.asarray(_bucket_tables())
    n_dil = len(DILATED)

    def body(dp_any, q_ref, k_ref, v_ref, do_ref, o_ref, lse_ref, gq_ref, gk_ref, rb_ref, tab_ref,
             dp_out, ggq_ref, ggk_ref, db_ref,
             qn, kn, dqn, dkn, dvv, delta_s, bias_s, dbias_s, obuf, sems):
        del dp_any
        h = pl.program_id(0)

        @pl.when(h == 0)
        def _():
            ggq_ref[...] = jnp.zeros_like(ggq_ref)
            ggk_ref[...] = jnp.zeros_like(ggk_ref)

        q = q_ref[...]
        rq = lax.rsqrt(jnp.mean(q * q, axis=-1, keepdims=True) + EPS)
        qn[...] = q * rq * gq_ref[...] * scale
        k = k_ref[...]
        rk = lax.rsqrt(jnp.mean(k * k, axis=-1, keepdims=True) + EPS)
        kn[...] = k * rk * gk_ref[...]
        delta_s[...] = jnp.sum(do_ref[...] * o_ref[...], axis=-1, keepdims=True)
        dqn[...] = jnp.zeros_like(dqn)
        dkn[...] = jnp.zeros_like(dkn)
        dvv[...] = jnp.zeros_like(dvv)
        dbias_s[...] = jnp.zeros_like(dbias_s)
        _bias_tiles(tab_ref, rb_ref, h, bias_s)
        col = lax.broadcasted_iota(jnp.int32, (BLK, 2 * BLK), 1)

        for i, (_, dil) in enumerate(DILATED):
            def block(b, carry, i=i, dil=dil):
                rows, prows, n = _block_rows(b, n_blocks, dil)
                qb = qn[rows, :].astype(BF16)
                kc = jnp.concatenate([kn[prows, :], kn[rows, :]], axis=0).astype(BF16)
                vc = jnp.concatenate([v_ref[prows, :], v_ref[rows, :]], axis=0).astype(BF16)
                dob = do_ref[rows, :].astype(BF16)
                sc = _dot(qb, kc, tb=True) + bias_s[i]
                sc = jnp.where(jnp.logical_or(col >= BLK, n > 0), sc, NEG_INF)
                pr = jnp.exp(sc - lse_ref[rows, :])
                dpr = _dot(dob, vc, tb=True)
                ds = pr * (dpr - delta_s[rows, :])
                dbias_s[i] += ds
                dsb = ds.astype(BF16)
                dqn[rows, :] += _dot(dsb, kc)
                dkc = _dot(dsb, qb, ta=True)
                dkn[prows, :] += dkc[:BLK]
                dkn[rows, :] += dkc[BLK:]
                dvc = _dot(pr.astype(BF16), dob, ta=True)
                dvv[prows, :] += dvc[:BLK]
                dvv[rows, :] += dvc[BLK:]
                return carry
            lax.fori_loop(0, n_blocks, block, 0)

        dq_n = dqn[...]
        ggq_ref[...] += jnp.sum(dq_n * (q * rq) * scale, axis=0, keepdims=True)
        dg = dq_n * gq_ref[...] * scale
        obuf[0] = (rq * (dg - q * (rq * rq) * jnp.mean(dg * q, axis=-1, keepdims=True))).astype(BF16)
        dk_n = dkn[...]
        ggk_ref[...] += jnp.sum(dk_n * (k * rk), axis=0, keepdims=True)
        dg = dk_n * gk_ref[...]
        obuf[1] = (rk * (dg - k * (rk * rk) * jnp.mean(dg * k, axis=-1, keepdims=True))).astype(BF16)
        obuf[2] = dvv[...].astype(BF16)
        copies = [pltpu.make_async_copy(obuf.at[j], dp_out.at[:, pl.ds(pl.multiple_of(((3 + j) * heads + h) * HEAD_DIM, HEAD_DIM), HEAD_DIM)],
                                        sems.at[j]) for j in range(3)]
        for cp in copies:
            cp.start()
        for b in range(NUM_BUCKETS):
            tot = jnp.zeros((BLK, 2 * BLK), F32)
            for i in range(n_dil):
                tot = tot + jnp.where(tab_ref[i] == b, dbias_s[i], 0.0)
            db_ref[b:b + 1, :] = jnp.full((1, LANE), jnp.sum(tot), F32)
        for cp in copies:
            cp.wait()

    def head_col(off):
        return pl.BlockSpec((s, HEAD_DIM), lambda h: (0, off * heads + h))
    own_col = pl.BlockSpec((s, HEAD_DIM), lambda h: (0, h))
    vec = pl.BlockSpec((1, HEAD_DIM), lambda h: (0, 0))
    big = pltpu.VMEM((s, HEAD_DIM), F32)
    return pl.pallas_call(
        body, name="attn_bwd", grid=(heads,),
        in_specs=[ANY, head_col(3), head_col(4), head_col(5), own_col, own_col,
                  pl.BlockSpec((None, s, 1), lambda h: (h, 0, 0)), vec, vec,
                  pl.BlockSpec(memory_space=pltpu.SMEM),
                  pl.BlockSpec((n_dil, BLK, 2 * BLK), lambda h: (0, 0, 0))],
        out_specs=(ANY, vec, vec, pl.BlockSpec((None, NUM_BUCKETS, LANE), lambda h: (h, 0, 0))),
        out_shape=(jax.ShapeDtypeStruct((s, d_in), BF16), jax.ShapeDtypeStruct((1, HEAD_DIM), F32),
                   jax.ShapeDtypeStruct((1, HEAD_DIM), F32), jax.ShapeDtypeStruct((heads, NUM_BUCKETS, LANE), F32)),
        scratch_shapes=[big, big, big, big, big, pltpu.VMEM((s, 1), F32),
                        pltpu.VMEM((n_dil, BLK, 2 * BLK), F32), pltpu.VMEM((n_dil, BLK, 2 * BLK), F32),
                        pltpu.VMEM((3, s, HEAD_DIM), BF16), pltpu.SemaphoreType.DMA((3,))],
        input_output_aliases={0: 0},
        compiler_params=_params("arbitrary"),
    )(dproj, proj, proj, proj, d_o, yb, lse, g_q, g_k, rel_bias, tables)


def _local_step(xs, ps, tgt, sp, win4, wout, wgate, wup4):
    s, d = xs.shape
    wa = d // 2
    heads = wa // HEAD_DIM
    d_in = 7 * wa
    c4 = d_in // N_CHIPS
    dple = ps.shape[1]

    tm, tn = _tile(s, 512), _tile(d, 512)
    tn_in = _tile(c4, 512)
    per_in = c4 // tn_in
    tn_up = _tile(d // N_CHIPS, 512)
    per_up = (d // N_CHIPS) // tn_up
    gm, gn = s // tm, d // tn

    hn, rx = _rms_fwd("rms_pre", xs, sp["g_pre"])
    proj = _matmul(
        "mm_in", hn, win4, pl.BlockSpec((tm, d), lambda i, j, k: (i, 0)), _shard_spec(d, tn_in, per_in, 2, 1),
        (gm, d_in // tn_in, 1), None, jax.ShapeDtypeStruct((s, d_in), F32), pl.BlockSpec((tm, tn_in), lambda i, j, k: (i, j)))
    b_st = sp["b_s"].T
    y, z, vn, mu_v, rs_v, ra = _gmlp_fwd(proj, sp["w_s"], b_st, sp["ln_v_g"], sp["ln_v_b"], sp["g_out_a"], d)
    yb, lse = _attn_fwd(proj, sp["g_q"], sp["g_k"], sp["rel_bias"], d)
    y, rb = _gate_fwd(y, yb, proj, sp["g_out_b"], d)

    def residual(acc, extra, outs):
        outs[0][...] = extra[0][...] + acc

    tile = pl.BlockSpec((tm, tn), lambda i, j, k: (i, j))
    h = _matmul(
        "mm_out", y, wout, pl.BlockSpec((tm, d), lambda i, j, k: (i, 0)), pl.BlockSpec((d, tn), lambda i, j, k: (0, j)),
        (gm, gn, 1), None, jax.ShapeDtypeStruct((s, d), F32), tile, extras=(xs,), extra_specs=(tile,), epilogue=residual)
    hp, rh = _rms_fwd("rms_ple", h, sp["g_ple"])
    p16 = ps.astype(BF16)

    def head(acc, extra, outs):
        p_ref, wup_ref, h_ref, t_ref = extra
        dout_ref, dgl_ref, dup_ref, loss_ref = outs
        up = _dot(p_ref[...], wup_ref[...])
        gate = jax.nn.sigmoid(acc)
        err = h_ref[...] + gate * up - t_ref[...]
        sq = _colsum8(err * err)
        part = sq[:, 0:LANE]
        for c in range(1, sq.shape[1] // LANE):
            part = part + sq[:, c * LANE:(c + 1) * LANE]
        loss_ref[...] = part
        dout = err * (1.0 / d)
        dout_ref[...] = dout
        dup_ref[...] = (dout * gate).astype(BF16)
        dgl_ref[...] = (dout * up * gate * (1.0 - gate)).astype(BF16)

    tile_up = pl.BlockSpec((tm, tn_up), lambda i, j, k: (i, j))
    dout, dglin, dup, loss_parts = _matmul(
        "mm_gate_loss", hp, wgate, pl.BlockSpec((tm, d), lambda i, j, k: (i, 0)), pl.BlockSpec((d, tn_up), lambda i, j, k: (0, j)),
        (gm, d // tn_up, 1), None,
        (jax.ShapeDtypeStruct((s, d), F32), jax.ShapeDtypeStruct((s, d), BF16), jax.ShapeDtypeStruct((s, d), BF16),
         jax.ShapeDtypeStruct((gm * 8, (d // tn_up) * LANE), F32)),
        (tile_up, tile_up, tile_up, pl.BlockSpec((8, LANE), lambda i, j, k: (i, j))),
        extras=(p16, wup4, h, tgt),
        extra_specs=(pl.BlockSpec((tm, dple), lambda i, j, k: (i, 0)), _shard_spec(dple, tn_up, per_up, 2, 1),
                     tile_up, tile_up),
        epilogue=head)
    loss = 0.5 * jnp.sum(loss_parts) * (1.0 / d)

    dhp = _matmul(
        "mm_dhp", dglin, wgate, pl.BlockSpec((tm, d), lambda i, j, k: (i, 0)), pl.BlockSpec((tn, d), lambda i, j, k: (j, 0)),
        (gm, gn, 1), None, jax.ShapeDtypeStruct((s, d), F32), tile, tb=True)
    dh, dh16, g_g_ple = _rms_bwd("rms_ple_bwd", dhp, h, rh, sp["g_ple"], dout, True)
    tmw = _tile(d, 512)
    g_wgate = _matmul(
        "mm_gwgate", hp, dglin, pl.BlockSpec((s, tmw), lambda i, j, k: (0, i)), pl.BlockSpec((s, tn), lambda i, j, k: (0, j)),
        (d // tmw, gn, 1), None, jax.ShapeDtypeStruct((d, d), F32), pl.BlockSpec((tmw, tn), lambda i, j, k: (i, j)), ta=True)
    g_wup = _matmul(
        "mm_gwup", p16, dup, pl.BlockSpec((s, dple), lambda i, j, k: (0, 0)), pl.BlockSpec((s, tn_up), lambda i, j, k: (0, j)),
        (1, d // tn_up, 1), None, jax.ShapeDtypeStruct((N_CHIPS, dple, d // N_CHIPS), F32),
        _shard_spec(dple, tn_up, per_up, 0, 1), ta=True)
    dy = _matmul(
        "mm_dy", dh16, wout, pl.BlockSpec((tm, d), lambda i, j, k: (i, 0)), pl.BlockSpec((tn, d), lambda i, j, k: (j, 0)),
        (gm, gn, 1), None, jax.ShapeDtypeStruct((s, d), F32), tile, tb=True)
    g_wout = _matmul(
        "mm_gwout", y, dh16, pl.BlockSpec((s, tmw), lambda i, j, k: (0, i)), pl.BlockSpec((s, tn), lambda i, j, k: (0, j)),
        (d // tmw, gn, 1), None, jax.ShapeDtypeStruct((d, d), F32), pl.BlockSpec((tmw, tn), lambda i, j, k: (i, j)), ta=True)

    dproj, d_o, g_g_out_b = _gate_bwd(dy, yb, proj, rb, sp["g_out_b"], d)
    dproj, g_g_q, g_g_k, dbias = _attn_bwd(dproj, d_o, yb, lse, proj, sp["g_q"], sp["g_k"], sp["rel_bias"], d)
    dproj, g_w_s, dz_sum, g_ln_g, g_ln_b, g_g_out_a = _gmlp_bwd(
        dproj, dy, proj, z, vn, mu_v, rs_v, ra, sp["w_s"], sp["ln_v_g"], sp["g_out_a"], d)

    tk = tn_in
    tmh, tnh = _tile(s, 1024), _tile(d, 1024)
    dhn = _matmul(
        "mm_dhn", dproj, win4, pl.BlockSpec((tmh, tk), lambda i, j, k: (i, k)), _shard_spec(tnh, tk, per_in, 1, 2),
        (s // tmh, d // tnh, d_in // tk), (tmh, tnh), jax.ShapeDtypeStruct((s, d), F32),
        pl.BlockSpec((tmh, tnh), lambda i, j, k: (i, j)), tb=True)
    grad_x, g_g_pre = _rms_bwd("rms_pre_bwd", dhn, xs, rx, sp["g_pre"], dh, False)
    g_win = _matmul(
        "mm_gwin", hn, dproj, pl.BlockSpec((s, tmw), lambda i, j, k: (0, i)), pl.BlockSpec((s, tn_in), lambda i, j, k: (0, j)),
        (d // tmw, d_in // tn_in, 1), None, jax.ShapeDtypeStruct((N_CHIPS, d, c4), F32),
        _shard_spec(tmw, tn_in, per_in, 0, 1), ta=True)

    small = {
        "g_pre": g_g_pre,
        "w_s": g_w_s,
        "b_s": jnp.sum(dz_sum.reshape(CHUNK, heads, HEAD_DIM), axis=-1).T,
        "ln_v_g": g_ln_g, "ln_v_b": g_ln_b,
        "g_q": g_g_q, "g_k": g_g_k,
        "rel_bias": dbias[:, :, 0].T,
        "g_out_a": g_g_out_a, "g_out_b": g_g_out_b,
        "g_ple": g_g_ple,
    }
    big = (g_win, g_wout.reshape(N_CHIPS, d // N_CHIPS, d), g_wgate.reshape(N_CHIPS, d // N_CHIPS, d), g_wup)
    return loss, grad_x, big, small


def _place():
    x, y, c = lax.axis_index("x"), lax.axis_index("y"), lax.axis_index("c")
    chips = [(1 - x, y), (x, 1 - y), (1 - x, 1 - y)]
    return x, y, c, chips


def _remote(src, dst, send_sems, recv_sems, k, to):
    return pltpu.make_async_remote_copy(src_ref=src, dst_ref=dst, send_sem=send_sems.at[k], recv_sem=recv_sems.at[k],
                                        device_id=to, device_id_type=MESH)


def _cast_bf16(name, w):
    r, c = w.shape
    tm = _tile(r, 256)

    def body(w_ref, o_ref):
        o_ref[...] = w_ref[...].astype(BF16)

    spec = pl.BlockSpec((tm, c), lambda i: (i, 0))
    return pl.pallas_call(body, name=name, grid=(r // tm,), in_specs=[spec], out_specs=spec,
                          out_shape=jax.ShapeDtypeStruct((r, c), BF16), compiler_params=_params("parallel"))(w)


def _gather_weights(shards):
    nw = len(shards)

    def body(*refs):
        ins, outs = refs[:nw], refs[nw:2 * nw]
        send_sems, recv_sems, local_sems = refs[2 * nw:]
        x, y, c, chips = _place()
        me = 2 * x + y
        sibling = (x, y, 1 - c)
        started = []
        for w in range(nw):
            half = ins[w].shape[0] // 2
            mine = pltpu.make_async_copy(ins[w], outs[w].at[me], local_sems.at[w])
            mine.start()
            started.append(mine)
        sends = []
        for w in range(nw):
            half = ins[w].shape[0] // 2
            for q, chip in enumerate(chips):
                cp = _remote(ins[w].at[pl.ds(c * half, half)], outs[w].at[me, pl.ds(c * half, half)],
                             send_sems, recv_sems, 6 * w + q, (*chip, c))
                cp.start()
                sends.append(cp)
        for w in range(nw):
            half = ins[w].shape[0] // 2
            for q, chip in enumerate(chips):
                blk = outs[w].at[2 * chip[0] + chip[1], pl.ds(c * half, half)]
                _remote(blk, blk, send_sems, recv_sems, 6 * w + q, sibling).wait_recv()
                fwd = _remote(blk, blk, send_sems, recv_sems, 6 * w + 3 + q, sibling)
                fwd.start()
                sends.append(fwd)
        for w in range(nw):
            half = ins[w].shape[0] // 2
            for q, chip in enumerate(chips):
                blk = outs[w].at[2 * chip[0] + chip[1], pl.ds((1 - c) * half, half)]
                _remote(blk, blk, send_sems, recv_sems, 6 * w + 3 + q, sibling).wait_recv()
        for cp in sends:
            cp.wait_send()
        for cp in started:
            cp.wait()

    return pl.pallas_call(
        body, name="gather_weights",
        in_specs=[ANY] * nw, out_specs=[ANY] * nw,
        out_shape=[jax.ShapeDtypeStruct((N_CHIPS, *w.shape), w.dtype) for w in shards],
        scratch_shapes=[pltpu.SemaphoreType.DMA((6 * nw,)), pltpu.SemaphoreType.DMA((6 * nw,)),
                        pltpu.SemaphoreType.DMA((nw,))],
    )(*shards)


def _swap_halves(grads):
    nw = len(grads)

    def body(*refs):
        ins, outs = refs[:nw], refs[nw:2 * nw]
        send_sems, recv_sems = refs[2 * nw:]
        x, y, c, _ = _place()
        copies = []
        for w in range(nw):
            half = ins[w].shape[1] // 2
            cp = _remote(ins[w].at[:, pl.ds((1 - c) * half, half)], outs[w], send_sems, recv_sems, w, (x, y, 1 - c))
            cp.start()
            copies.append(cp)
        for cp in copies:
            cp.wait()

    return pl.pallas_call(
        body, name="swap_halves",
        in_specs=[ANY] * nw, out_specs=[ANY] * nw,
        out_shape=[jax.ShapeDtypeStruct((N_CHIPS, g.shape[1] // 2, g.shape[2]), g.dtype) for g in grads],
        scratch_shapes=[pltpu.SemaphoreType.DMA((nw,)), pltpu.SemaphoreType.DMA((nw,))],
    )(*grads)


def _chip_sum(name, core, grad, got):
    _, r, c = grad.shape
    half = r // 2
    th = _tile(half, 128)
    n = half // th

    def body(core_ref, g_ref, o_ref, s32_ref, s16_ref):
        del core_ref
        v = g_ref[...] + o_ref[...]
        s32_ref[...] = v
        s16_ref[...] = v.astype(BF16)

    blk = pl.BlockSpec((None, th, c), lambda k, i, core_ref: (k, i, 0))
    return pl.pallas_call(
        body, name=name,
        grid_spec=pltpu.PrefetchScalarGridSpec(
            num_scalar_prefetch=1, grid=(N_CHIPS, n),
            in_specs=[pl.BlockSpec((None, th, c), lambda k, i, core_ref: (k, core_ref[0] * n + i, 0)), blk],
            out_specs=[blk, blk]),
        out_shape=[jax.ShapeDtypeStruct((N_CHIPS, half, c), F32), jax.ShapeDtypeStruct((N_CHIPS, half, c), BF16)],
        compiler_params=_params("parallel", "parallel"),
    )(core, grad, got)


def _exchange_chips(sums16):
    nw = len(sums16)

    def body(*refs):
        ins, outs = refs[:nw], refs[nw:2 * nw]
        send_sems, recv_sems = refs[2 * nw:]
        x, y, c, chips = _place()
        me = 2 * x + y
        sends = []
        for w in range(nw):
            for q, chip in enumerate(chips):
                cp = _remote(ins[w].at[2 * chip[0] + chip[1]], outs[w].at[me], send_sems, recv_sems, 3 * w + q, (*chip, c))
                cp.start()
                sends.append(cp)
        for w in range(nw):
            for q, chip in enumerate(chips):
                blk = outs[w].at[2 * chip[0] + chip[1]]
                _remote(blk, blk, send_sems, recv_sems, 3 * w + q, (*chip, c)).wait_recv()
        for cp in sends:
            cp.wait_send()

    return pl.pallas_call(
        body, name="exchange_chips",
        in_specs=[ANY] * nw, out_specs=[ANY] * nw,
        out_shape=[jax.ShapeDtypeStruct(g.shape, g.dtype) for g in sums16],
        scratch_shapes=[pltpu.SemaphoreType.DMA((3 * nw,)), pltpu.SemaphoreType.DMA((3 * nw,))],
    )(*sums16)


def _total(name, chip, sum32, got16):
    _, half, c = sum32.shape
    th = _tile(half, 128)

    def body(chip_ref, own_ref, a_ref, b_ref, c_ref, o_ref):
        del chip_ref
        o_ref[...] = ((own_ref[...] + a_ref[...].astype(F32)) + b_ref[...].astype(F32)) + c_ref[...].astype(F32)

    def other(step):
        return pl.BlockSpec((None, th, c), lambda i, chip_ref: ((chip_ref[0] + step) % N_CHIPS, i, 0))

    return pl.pallas_call(
        body, name=name,
        grid_spec=pltpu.PrefetchScalarGridSpec(
            num_scalar_prefetch=1, grid=(half // th,),
            in_specs=[other(0), other(1), other(2), other(3)],
            out_specs=pl.BlockSpec((th, c), lambda i, chip_ref: (i, 0))),
        out_shape=jax.ShapeDtypeStruct((half, c), F32),
        compiler_params=_params("parallel"),
    )(chip, sum32, got16, got16, got16)


def _join_halves(totals):
    nw = len(totals)

    def body(*refs):
        ins, outs = refs[:nw], refs[nw:2 * nw]
        send_sems, recv_sems, local_sems = refs[2 * nw:]
        x, y, c, _ = _place()
        copies = []
        for w in range(nw):
            half = ins[w].shape[0]
            rows = outs[w].at[pl.ds(c * half, half)]
            mine = pltpu.make_async_copy(ins[w], rows, local_sems.at[w])
            mine.start()
            cp = _remote(ins[w], rows, send_sems, recv_sems, w, (x, y, 1 - c))
            cp.start()
            copies += [mine, cp]
        for cp in copies:
            cp.wait()

    return pl.pallas_call(
        body, name="join_halves",
        in_specs=[ANY] * nw, out_specs=[ANY] * nw,
        out_shape=[jax.ShapeDtypeStruct((2 * t.shape[0], t.shape[1]), t.dtype) for t in totals],
        scratch_shapes=[pltpu.SemaphoreType.DMA((nw,)), pltpu.SemaphoreType.DMA((nw,)), pltpu.SemaphoreType.DMA((nw,))],
    )(*totals)


def _gather_small(pack):
    m_per, n = pack.shape

    def body(x_ref, out_ref, send_sems, recv_sems, local_sem):
        x, y, c, chips = _place()
        me, sibling = (x, y, c), (x, y, 1 - c)

        def rows(px, py, pc):
            return out_ref.at[pl.ds((4 * px + 2 * py + pc) * m_per, m_per), :]

        def copy(k, block, to, src=None):
            return _remote(rows(*block) if src is None else src, rows(*block), send_sems, recv_sems, k, to)

        mine = pltpu.make_async_copy(x_ref, rows(*me), local_sem)
        mine.start()
        first = [copy(0, me, sibling, src=x_ref)]
        first += [copy(1 + j, me, (*chip, c), src=x_ref) for j, chip in enumerate(chips)]
        for cp in first:
            cp.start()
        passed = [copy(4 + j, (*chip, c), sibling) for j, chip in enumerate(chips)]
        for j, chip in enumerate(chips):
            copy(1 + j, (*chip, c), me).wait_recv()
            passed[j].start()
        copy(0, sibling, me).wait_recv()
        for j, chip in enumerate(chips):
            copy(4 + j, (*chip, 1 - c), me).wait_recv()
        for cp in first + passed:
            cp.wait_send()
        mine.wait()

    return pl.pallas_call(
        body, name="gather_small",
        out_shape=jax.ShapeDtypeStruct((N_DEV * m_per, n), pack.dtype),
        in_specs=[pl.BlockSpec(memory_space=pltpu.VMEM)],
        out_specs=pl.BlockSpec(memory_space=pltpu.VMEM),
        scratch_shapes=[pltpu.SemaphoreType.DMA((7,)), pltpu.SemaphoreType.DMA((7,)), pltpu.SemaphoreType.DMA],
        compiler_params=_params(),
    )(pack)


def _adamw_math(w, g, m, v):
    m = ADAM_B1 * m + (1.0 - ADAM_B1) * g
    v = ADAM_B2 * v + (1.0 - ADAM_B2) * (g * g)
    m_hat = m / (1.0 - ADAM_B1 ** ADAM_STEP)
    v_hat = v / (1.0 - ADAM_B2 ** ADAM_STEP)
    delta = -ADAM_LR * (m_hat / (jnp.sqrt(v_hat) + ADAM_EPS) + ADAM_WD * w)
    return delta, m, v


def _adamw(name, w, g, m, v):
    r, c = w.shape
    tm = _tile(r, 128)

    def body(w_ref, g_ref, m_ref, v_ref, d_out, m_out, v_out):
        d_out[...], m_out[...], v_out[...] = _adamw_math(w_ref[...], g_ref[...], m_ref[...], v_ref[...])

    spec = pl.BlockSpec((tm, c), lambda i: (i, 0))
    return pl.pallas_call(
        body, name=name, grid=(r // tm,), in_specs=[spec] * 4, out_specs=[spec] * 3,
        out_shape=[jax.ShapeDtypeStruct((r, c), F32)] * 3, compiler_params=_params("parallel"),
    )(w, g, m, v)


def _adamw_small(gathered, w, m, v):
    rows = w.shape[0]

    def body(all_ref, w_ref, m_ref, v_ref, g_out, d_out, m_out, v_out):
        g = all_ref[0:rows, :]
        for dev in range(1, N_DEV):
            g = g + all_ref[dev * rows:(dev + 1) * rows, :]
        g_out[...] = g
        d_out[...], m_out[...], v_out[...] = _adamw_math(w_ref[...], g, m_ref[...], v_ref[...])

    return pl.pallas_call(
        body, name="adamw_small", out_shape=[jax.ShapeDtypeStruct(w.shape, F32)] * 4, compiler_params=_params(),
    )(gathered, w, m, v)


SMALL = ("g_pre", "w_s", "b_s", "ln_v_g", "ln_v_b", "g_q", "g_k", "rel_bias", "g_out_a", "g_out_b", "g_ple")
LARGE = ("w_in", "w_out", "w_ple_gate", "w_ple_up")
WEIGHTS = ("g_pre", "w_in", "w_s", "b_s", "ln_v_g", "ln_v_b", "g_q", "g_k", "rel_bias", "g_out_a", "g_out_b", "w_out",
           "g_ple", "w_ple_gate", "w_ple_up")


def _pack(parts):
    flat = jnp.concatenate([parts[n].reshape(-1).astype(F32) for n in SMALL])
    rows = -(-flat.shape[0] // (8 * LANE)) * 8
    return jnp.pad(flat, (0, rows * LANE - flat.shape[0])).reshape(rows, LANE)


def _unpack(pack, like):
    flat = pack.reshape(-1)
    out, at = {}, 0
    for n in SMALL:
        size = math.prod(like[n].shape)
        out[n] = flat[at:at + size].reshape(like[n].shape)
        at += size
    return out


def kernel(x, p, g_pre, w_in, w_s, b_s, ln_v_g, ln_v_b, g_q, g_k, rel_bias, g_out_a, g_out_b, w_out, g_ple, w_ple_gate, w_ple_up, loss_target, m_g_pre, m_w_in, m_w_s, m_b_s, m_ln_v_g, m_ln_v_b, m_g_q, m_g_k, m_rel_bias, m_g_out_a, m_g_out_b, m_w_out, m_g_ple, m_w_ple_gate, m_w_ple_up, v_g_pre, v_w_in, v_w_s, v_b_s, v_ln_v_g, v_ln_v_b, v_g_q, v_g_k, v_rel_bias, v_g_out_a, v_g_out_b, v_w_out, v_g_ple, v_w_ple_gate, v_w_ple_up):
    given = dict(locals())
    weights = {n: given[n] for n in WEIGHTS}
    mom_m = {n: given["m_" + n] for n in WEIGHTS}
    mom_v = {n: given["v_" + n] for n in WEIGHTS}
    xs, ps, tgt = x[0], p[0, 0], loss_target[0]
    d = xs.shape[1]

    core = lax.axis_index("c").astype(jnp.int32).reshape(1)
    chip = (2 * lax.axis_index("x") + lax.axis_index("y")).astype(jnp.int32).reshape(1)

    shards16 = [_cast_bf16("cast_" + n, weights[n][0]) for n in LARGE]
    win4, wout4, wgate4, wup4 = _gather_weights(shards16)

    sp = {
        "g_pre": g_pre, "w_s": w_s[0], "b_s": b_s[0], "ln_v_g": ln_v_g, "ln_v_b": ln_v_b, "g_q": g_q, "g_k": g_k,
        "rel_bias": rel_bias, "g_out_a": g_out_a, "g_out_b": g_out_b, "g_ple": g_ple,
    }
    loss_local, grad_x, big, small = _local_step(xs, ps, tgt, sp, win4, wout4.reshape(d, d), wgate4.reshape(d, d), wup4)
    loss = lax.psum(loss_local, MESH_AXES)

    got = _swap_halves(list(big))
    sums = [_chip_sum("chip_sum_" + n, core, g, o) for n, g, o in zip(LARGE, big, got)]
    got16 = _exchange_chips([s16 for _, s16 in sums])
    totals = [_total("total_" + n, chip, s32, o) for n, (s32, _), o in zip(LARGE, sums, got16)]
    grads = dict(zip(LARGE, _join_halves(totals)))

    out_g, out_d, out_m, out_v = {}, {}, {}, {}
    for n in LARGE:
        out_g[n] = grads[n][None]
        dl, mn, vn = _adamw("adamw_" + n, weights[n][0], grads[n], mom_m[n][0], mom_v[n][0])
        out_d[n], out_m[n], out_v[n] = dl[None], mn[None], vn[None]

    gathered = _gather_small(_pack(small))
    pg, pd, pm, pv = _adamw_small(gathered, _pack(weights), _pack(mom_m), _pack(mom_v))
    for packed, out in ((pg, out_g), (pd, out_d), (pm, out_m), (pv, out_v)):
        out.update(_unpack(packed, weights))

    return (loss, grad_x[None], *[out_g[n] for n in WEIGHTS], *[out_d[n] for n in WEIGHTS],
            *[out_m[n] for n in WEIGHTS], *[out_v[n] for n in WEIGHTS])
```

```python
import functools
import math

import numpy as np

import jax
import jax.numpy as jnp
from jax import lax
from jax.experimental import pallas as pl
from jax.experimental.pallas import tpu as pltpu

F32 = jnp.float32
BF16 = jnp.bfloat16

HEAD_DIM = 128
CHUNK = 128
BLK = 128
DILATED = ((128, 1), (512, 4), (2048, 16))
NUM_BUCKETS = 32
MAX_DISTANCE = 2048
EPS = 1e-6
NEG_INF = -1e30
N_CHIPS = 4
N_DEV = 8
MESH_AXES = ("x", "y", "c")

ADAM_LR = 0.001
ADAM_B1 = 0.9
ADAM_B2 = 0.999
ADAM_EPS = 1e-08
ADAM_WD = 0.01
ADAM_STEP = 10

V7X_VMEM_LIMIT = 56 * 1024 * 1024
LANE = 128
MESH = pl.DeviceIdType.MESH
ANY = pl.BlockSpec(memory_space=pl.ANY)


def _params(*sem):
    return pltpu.CompilerParams(dimension_semantics=sem or None, vmem_limit_bytes=V7X_VMEM_LIMIT)


def _tile(n, target):
    if n <= target:
        return n
    t = (target // LANE) * LANE
    while t > LANE and n % t:
        t -= LANE
    assert n % t == 0, (n, target)
    return t


def _gelu(x):
    return 0.5 * x * (1.0 + lax.erf(x * (1.0 / math.sqrt(2.0))))


def _gelu_grad(x):
    return 0.5 * (1.0 + lax.erf(x * (1.0 / math.sqrt(2.0)))) + x * jnp.exp(-0.5 * x * x) * (1.0 / math.sqrt(2.0 * math.pi))


def _silu_and_grad(x):
    s = jax.nn.sigmoid(x)
    return x * s, s * (1.0 + x * (1.0 - s))


def _dot(a, b, ta=False, tb=False):
    return lax.dot_general(a, b, (((0 if ta else 1,), (1 if tb else 0,)), ((), ())), preferred_element_type=F32)


def _colsum8(v):
    return jnp.sum(v.reshape(v.shape[0] // 8, 8, v.shape[1]), axis=0)


def _matmul(name, a, b, a_spec, b_spec, grid, acc_shape, out_shape, out_specs, ta=False, tb=False,
            extras=(), extra_specs=(), epilogue=None, comm=None):
    nk = grid[2]
    n_extra = len(extras)
    single = not isinstance(out_shape, (tuple, list))
    outs_shape = (out_shape,) if single else tuple(out_shape)
    outs_specs = (out_specs,) if single else tuple(out_specs)
    n_out = len(outs_shape)
    c_ins = list(comm.ins) if comm else []
    c_outs = list(comm.out_shape) if comm else []
    c_scratch = list(comm.scratch) if comm else []
    n_cin, n_cout = len(c_ins), len(c_outs)

    def finish(acc, extra_refs, out_refs):
        if epilogue is None:
            out_refs[0][...] = acc.astype(out_refs[0].dtype)
        else:
            epilogue(acc, extra_refs, out_refs)

    def body(a_ref, b_ref, *rest):
        extra_refs = rest[:n_extra]
        cin_refs = rest[n_extra:n_extra + n_cin]
        out_refs = rest[n_extra + n_cin:n_extra + n_cin + n_out]
        cout_refs = rest[n_extra + n_cin + n_out:n_extra + n_cin + n_out + n_cout]
        scratch_refs = rest[n_extra + n_cin + n_out + n_cout:]
        ids = [pl.program_id(ax) for ax in range(3)]
        if comm:
            sems = scratch_refs[len(scratch_refs) - len(c_scratch):]

            @pl.when((ids[0] == 0) & (ids[1] == 0) & (ids[2] == 0))
            def _():
                comm.start(cin_refs, cout_refs, sems)

        if nk == 1:
            finish(_dot(a_ref[...], b_ref[...], ta, tb), extra_refs, out_refs)
        else:
            acc_ref = scratch_refs[0]

            @pl.when(ids[2] == 0)
            def _():
                acc_ref[...] = jnp.zeros_like(acc_ref)

            acc_ref[...] += _dot(a_ref[...], b_ref[...], ta, tb)

            @pl.when(ids[2] == nk - 1)
            def _():
                finish(acc_ref[...], extra_refs, out_refs)

        if comm:
            @pl.when((ids[0] == grid[0] - 1) & (ids[1] == grid[1] - 1) & (ids[2] == nk - 1))
            def _():
                comm.finish(cin_refs, cout_refs, sems)

    scratch = ([] if nk == 1 else [pltpu.VMEM(acc_shape, F32)]) + c_scratch
    aliases = {2 + n_extra + i: n_out + i for i in range(n_cin)} if (comm and comm.in_place) else {}
    res = pl.pallas_call(
        body, name=name, grid=grid,
        in_specs=[a_spec, b_spec, *extra_specs] + [ANY] * n_cin,
        out_specs=list(outs_specs) + [ANY] * n_cout, out_shape=list(outs_shape) + c_outs, scratch_shapes=scratch,
        input_output_aliases=aliases,
        compiler_params=_params(*(("arbitrary",) * 3 if comm else ("parallel", "parallel", "arbitrary"))),
    )(a, b, *extras, *c_ins)
    if comm:
        main = res[:n_out]
        return (main[0] if single else main), res[n_out:]
    return res[0] if single else res


def _shard_spec(rows, cols, per_shard, row_axis, col_axis):
    def index(i, j, k):
        g = (i, j, k)
        return (g[col_axis] // per_shard, g[row_axis], g[col_axis] % per_shard)
    return pl.BlockSpec((None, rows, cols), index)


def _rms_fwd(name, x, g):
    s, d = x.shape
    tm = _tile(s, 256)

    def body(x_ref, g_ref, y_ref, r_ref):
        xf = x_ref[...]
        r = lax.rsqrt(jnp.mean(xf * xf, axis=-1, keepdims=True) + EPS)
        y_ref[...] = (xf * r * g_ref[...]).astype(BF16)
        r_ref[...] = r

    return pl.pallas_call(
        body, name=name, grid=(s // tm,),
        in_specs=[pl.BlockSpec((tm, d), lambda i: (i, 0)), pl.BlockSpec((1, d), lambda i: (0, 0))],
        out_specs=(pl.BlockSpec((tm, d), lambda i: (i, 0)), pl.BlockSpec((tm, 1), lambda i: (i, 0))),
        out_shape=(jax.ShapeDtypeStruct((s, d), BF16), jax.ShapeDtypeStruct((s, 1), F32)),
        compiler_params=_params("parallel"),
    )(x, g)


def _rms_bwd(name, dy, x, r, g, skip, with_bf16):
    s, d = x.shape
    tm = _tile(s, 256)
    n = s // tm

    def body(dy_ref, x_ref, r_ref, g_ref, skip_ref, *outs):
        dx_ref = outs[0]
        dg_ref = outs[-2]
        acc_ref = outs[-1]
        i = pl.program_id(0)
        dyv, xv, rv = dy_ref[...], x_ref[...], r_ref[...]

        @pl.when(i == 0)
        def _():
            acc_ref[...] = jnp.zeros_like(acc_ref)

        acc_ref[...] += _colsum8(dyv * xv * rv)
        dg = dyv * g_ref[...]
        dx = skip_ref[...] + rv * (dg - xv * (rv * rv) * jnp.mean(dg * xv, axis=-1, keepdims=True))
        dx_ref[...] = dx
        if with_bf16:
            outs[1][...] = dx.astype(BF16)

        @pl.when(i == n - 1)
        def _():
            dg_ref[...] = jnp.sum(acc_ref[...], axis=0, keepdims=True)

    row = pl.BlockSpec((tm, d), lambda i: (i, 0))
    vec = pl.BlockSpec((1, d), lambda i: (0, 0))
    out_specs = [row] + ([row] if with_bf16 else []) + [vec]
    out_shape = [jax.ShapeDtypeStruct((s, d), F32)] + ([jax.ShapeDtypeStruct((s, d), BF16)] if with_bf16 else []) \
        + [jax.ShapeDtypeStruct((1, d), F32)]
    return pl.pallas_call(
        body, name=name, grid=(n,),
        in_specs=[row, row, pl.BlockSpec((tm, 1), lambda i: (i, 0)), vec, row],
        out_specs=out_specs, out_shape=out_shape, scratch_shapes=[pltpu.VMEM((8, d), F32)],
        compiler_params=_params("arbitrary"),
    )(dy, x, r, g, skip)


def _causal(w):
    t = lax.broadcasted_iota(jnp.int32, w.shape, 0)
    s_ = lax.broadcasted_iota(jnp.int32, w.shape, 1)
    return jnp.where(t >= s_, w, 0.0)


def _gmlp_fwd(proj, w_s, b_st, ln_g, ln_b, g_out, d_model):
    s = proj.shape[0]
    wa = d_model // 2
    groups = wa // HEAD_DIM
    tm = _tile(s, 256)
    n_chunks = tm // CHUNK

    def body(au_ref, av_ref, az_ref, ws_ref, bst_ref, lng_ref, lnb_ref, go_ref,
             y_ref, z_ref, vn_ref, mu_ref, rs_ref, ra_ref):
        gv = _gelu(av_ref[...])
        mu = jnp.mean(gv, axis=-1, keepdims=True)
        xc = gv - mu
        rs = lax.rsqrt(jnp.mean(xc * xc, axis=-1, keepdims=True) + EPS)
        vn = (xc * rs * lng_ref[...] + lnb_ref[...]).astype(BF16)
        vn_ref[...] = vn
        mu_ref[...] = mu
        rs_ref[...] = rs
        for g in range(groups):
            wm = _causal(ws_ref[g]).astype(BF16)
            cols = slice(g * HEAD_DIM, (g + 1) * HEAD_DIM)
            for ch in range(n_chunks):
                rows = slice(ch * CHUNK, (ch + 1) * CHUNK)
                z_ref[rows, cols] = _dot(wm, vn_ref[rows, cols]) + bst_ref[:, g:g + 1]
        ya = _gelu(au_ref[...]) * z_ref[...]
        ra = lax.rsqrt(jnp.mean(ya * ya, axis=-1, keepdims=True) + EPS)
        ra_ref[...] = ra
        sz, _ = _silu_and_grad(az_ref[...])
        y_ref[...] = (ya * ra * go_ref[...] * sz).astype(BF16)

    def col(j):
        return pl.BlockSpec((tm, wa), lambda i: (i, j))
    vec = pl.BlockSpec((1, wa), lambda i: (0, 0))
    stat = pl.BlockSpec((tm, 1), lambda i: (i, 0))
    return pl.pallas_call(
        body, name="gmlp_fwd", grid=(s // tm,),
        in_specs=[col(0), col(1), col(2),
                  pl.BlockSpec((groups, CHUNK, CHUNK), lambda i: (0, 0, 0)),
                  pl.BlockSpec((CHUNK, groups), lambda i: (0, 0)), vec, vec, vec],
        out_specs=(col(0), col(0), col(0), stat, stat, stat),
        out_shape=(jax.ShapeDtypeStruct((s, d_model), BF16), jax.ShapeDtypeStruct((s, wa), F32),
                   jax.ShapeDtypeStruct((s, wa), BF16), jax.ShapeDtypeStruct((s, 1), F32),
                   jax.ShapeDtypeStruct((s, 1), F32), jax.ShapeDtypeStruct((s, 1), F32)),
        compiler_params=_params("parallel"),
    )(proj, proj, proj, w_s, b_st, ln_g, ln_b, g_out)


def _gmlp_bwd(dproj, dy, proj, z, vn, mu, rs, ra, w_s, ln_g, g_out, d_model):
    s = proj.shape[0]
    wa = d_model // 2
    groups = wa // HEAD_DIM
    tm = _tile(s, 256)
    n_chunks = tm // CHUNK
    n = s // tm

    def causal_stack(w):
        t = lax.broadcasted_iota(jnp.int32, w.shape, 1)
        s_ = lax.broadcasted_iota(jnp.int32, w.shape, 2)
        return jnp.where(t >= s_, w, 0.0)

    def body(dproj_in, dy_ref, au_ref, av_ref, az_ref, z_ref, vn_ref, mu_ref, rs_ref, ra_ref, ws_ref, lng_ref, go_ref,
             dp_ref, gws_ref, dzs_ref, glg_ref, glb_ref, ggo_ref,
             dz_s, dvn_s, acc_lg, acc_lb, acc_go):
        del dproj_in
        i = pl.program_id(0)

        @pl.when(i == 0)
        def _():
            gws_ref[...] = jnp.zeros_like(gws_ref)
            dzs_ref[...] = jnp.zeros_like(dzs_ref)
            acc_lg[...] = jnp.zeros_like(acc_lg)
            acc_lb[...] = jnp.zeros_like(acc_lb)
            acc_go[...] = jnp.zeros_like(acc_go)

        au, az, zv, rav = au_ref[...], az_ref[...], z_ref[...], ra_ref[...]
        u = _gelu(au)
        ya = u * zv
        sz, dsz = _silu_and_grad(az)
        dyv = dy_ref[...]
        dp_ref[:, 2 * wa:3 * wa] = (dyv * (ya * rav * go_ref[...]) * dsz).astype(BF16)
        dn = dyv * sz
        acc_go[...] += _colsum8(dn * ya * rav)
        dyg = dn * go_ref[...]
        dya = rav * (dyg - ya * (rav * rav) * jnp.mean(dyg * ya, axis=-1, keepdims=True))
        dp_ref[:, 0:wa] = (dya * zv * _gelu_grad(au)).astype(BF16)
        dz_s[...] = dya * u
        for ch in range(n_chunks):
            dzs_ref[...] += dz_s[ch * CHUNK:(ch + 1) * CHUNK, :]
        for g in range(groups):
            wm = _causal(ws_ref[g]).astype(BF16)
            cols = slice(g * HEAD_DIM, (g + 1) * HEAD_DIM)
            for ch in range(n_chunks):
                rows = slice(ch * CHUNK, (ch + 1) * CHUNK)
                dzb = dz_s[rows, cols].astype(BF16)
                gws_ref[g] += _dot(dzb, vn_ref[rows, cols], tb=True)
                dvn_s[rows, cols] = _dot(wm, dzb, ta=True)
        av = av_ref[...]
        xh = (_gelu(av) - mu_ref[...]) * rs_ref[...]
        dvn = dvn_s[...]
        acc_lb[...] += _colsum8(dvn)
        acc_lg[...] += _colsum8(dvn * xh)
        dxh = dvn * lng_ref[...]
        dgv = rs_ref[...] * (dxh - jnp.mean(dxh, axis=-1, keepdims=True) - xh * jnp.mean(dxh * xh, axis=-1, keepdims=True))
        dp_ref[:, wa:2 * wa] = (dgv * _gelu_grad(av)).astype(BF16)

        @pl.when(i == n - 1)
        def _():
            gws_ref[...] = causal_stack(gws_ref[...])
            glg_ref[...] = jnp.sum(acc_lg[...], axis=0, keepdims=True)
            glb_ref[...] = jnp.sum(acc_lb[...], axis=0, keepdims=True)
            ggo_ref[...] = jnp.sum(acc_go[...], axis=0, keepdims=True)

    def col(j):
        return pl.BlockSpec((tm, wa), lambda i: (i, j))
    vec = pl.BlockSpec((1, wa), lambda i: (0, 0))
    stat = pl.BlockSpec((tm, 1), lambda i: (i, 0))
    ws_spec = pl.BlockSpec((groups, CHUNK, CHUNK), lambda i: (0, 0, 0))
    d_in = dproj.shape[1]
    return pl.pallas_call(
        body, name="gmlp_bwd", grid=(n,),
        in_specs=[ANY, col(0), col(0), col(1), col(2), col(0), col(0), stat, stat, stat, ws_spec, vec, vec],
        out_specs=(pl.BlockSpec((tm, 3 * wa), lambda i: (i, 0)), ws_spec,
                   pl.BlockSpec((CHUNK, wa), lambda i: (0, 0)), vec, vec, vec),
        out_shape=(jax.ShapeDtypeStruct((s, d_in), BF16),
                   jax.ShapeDtypeStruct((groups, CHUNK, CHUNK), F32), jax.ShapeDtypeStruct((CHUNK, wa), F32))
        + (jax.ShapeDtypeStruct((1, wa), F32),) * 3,
        scratch_shapes=[pltpu.VMEM((tm, wa), F32), pltpu.VMEM((tm, wa), F32)] + [pltpu.VMEM((8, wa), F32)] * 3,
        input_output_aliases={0: 0},
        compiler_params=_params("arbitrary"),
    )(dproj, dy, proj, proj, proj, z, vn, mu, rs, ra, w_s, ln_g, g_out)


def _gate_fwd(y_in, yb, proj, g_out, d_model):
    s = proj.shape[0]
    wa = d_model // 2
    tm = _tile(s, 256)

    def body(y_any, yb_ref, bz_ref, go_ref, y_ref, rb_ref):
        del y_any
        ybv = yb_ref[...]
        rb = lax.rsqrt(jnp.mean(ybv * ybv, axis=-1, keepdims=True) + EPS)
        rb_ref[...] = rb
        sz, _ = _silu_and_grad(bz_ref[...])
        y_ref[...] = (ybv * rb * go_ref[...] * sz).astype(BF16)

    return pl.pallas_call(
        body, name="gate_b_fwd", grid=(s // tm,),
        in_specs=[ANY, pl.BlockSpec((tm, wa), lambda i: (i, 0)), pl.BlockSpec((tm, wa), lambda i: (i, 6)),
                  pl.BlockSpec((1, wa), lambda i: (0, 0))],
        out_specs=(pl.BlockSpec((tm, wa), lambda i: (i, 1)), pl.BlockSpec((tm, 1), lambda i: (i, 0))),
        out_shape=(jax.ShapeDtypeStruct((s, d_model), BF16), jax.ShapeDtypeStruct((s, 1), F32)),
        input_output_aliases={0: 0},
        compiler_params=_params("parallel"),
    )(y_in, yb, proj, g_out)


def _gate_bwd(dy, yb, proj, rb, g_out, d_model):
    s, d_in = proj.shape
    wa = d_model // 2
    tm = _tile(s, 256)
    n = s // tm

    def body(dy_ref, yb_ref, bz_ref, rb_ref, go_ref, dp_ref, do_ref, ggo_ref, acc):
        i = pl.program_id(0)

        @pl.when(i == 0)
        def _():
            acc[...] = jnp.zeros_like(acc)

        dyv, ybv, rbv = dy_ref[...], yb_ref[...], rb_ref[...]
        sz, dsz = _silu_and_grad(bz_ref[...])
        dp_ref[...] = (dyv * (ybv * rbv * go_ref[...]) * dsz).astype(BF16)
        dn = dyv * sz
        acc[...] += _colsum8(dn * ybv * rbv)
        dyg = dn * go_ref[...]
        do_ref[...] = rbv * (dyg - ybv * (rbv * rbv) * jnp.mean(dyg * ybv, axis=-1, keepdims=True))

        @pl.when(i == n - 1)
        def _():
            ggo_ref[...] = jnp.sum(acc[...], axis=0, keepdims=True)

    vec = pl.BlockSpec((1, wa), lambda i: (0, 0))
    return pl.pallas_call(
        body, name="gate_b_bwd", grid=(n,),
        in_specs=[pl.BlockSpec((tm, wa), lambda i: (i, 1)), pl.BlockSpec((tm, wa), lambda i: (i, 0)),
                  pl.BlockSpec((tm, wa), lambda i: (i, 6)), pl.BlockSpec((tm, 1), lambda i: (i, 0)), vec],
        out_specs=(pl.BlockSpec((tm, wa), lambda i: (i, 6)), pl.BlockSpec((tm, wa), lambda i: (i, 0)), vec),
        out_shape=(jax.ShapeDtypeStruct((s, d_in), BF16), jax.ShapeDtypeStruct((s, wa), F32),
                   jax.ShapeDtypeStruct((1, wa), F32)),
        scratch_shapes=[pltpu.VMEM((8, wa), F32)],
        compiler_params=_params("arbitrary"),
    )(dy, yb, proj, rb, g_out)


def _bucket_tables():
    qi = BLK + np.arange(BLK)
    kj = np.arange(2 * BLK)
    delta = qi[:, None] - kj[None, :]
    max_exact = NUM_BUCKETS // 2
    tabs = []
    for window, dil in DILATED:
        band = (delta >= 0) & (delta <= window // dil)
        dist = np.clip(delta, 0, None) * dil
        d = np.maximum(dist, 1).astype(np.float32)
        large = max_exact + (np.log(d / np.float32(max_exact)) / np.float32(math.log(MAX_DISTANCE / max_exact))
                             * np.float32(NUM_BUCKETS - max_exact)).astype(np.int32)
        large = np.minimum(large, NUM_BUCKETS - 1)
        tabs.append(np.where(band, np.where(dist < max_exact, dist, large), -1).astype(np.int32))
    return np.stack(tabs)


def _bias_tiles(tab_ref, rb_ref, h, bias_s):
    for i in range(len(DILATED)):
        t = tab_ref[i]
        bias = jnp.zeros(t.shape, F32)
        for b in range(NUM_BUCKETS):
            bias = jnp.where(t == b, rb_ref[b, h], bias)
        bias_s[i] = jnp.where(t >= 0, bias, NEG_INF)


def _block_rows(b, n_blocks, dil):
    nb = n_blocks // dil
    r = b // nb
    n = b % nb
    start = r + dil * BLK * n
    pstart = jnp.where(n > 0, start - dil * BLK, start)
    if dil == 1:
        return pl.ds(pl.multiple_of(start, BLK), BLK), pl.ds(pl.multiple_of(pstart, BLK), BLK), n
    return pl.ds(start, BLK, stride=dil), pl.ds(pstart, BLK, stride=dil), n


def _attn_fwd(proj, g_q, g_k, rel_bias, d_model):
    s = proj.shape[0]
    heads = d_model // 2 // HEAD_DIM
    n_blocks = s // BLK
    scale = HEAD_DIM ** -0.5
    tables = jnp.asarray(_bucket_tables())

    def body(q_ref, k_ref, v_ref, gq_ref, gk_ref, rb_ref, tab_ref, yb_ref, lse_ref, qn, kn, m_s, l_s, acc_s, bias_s):
        h = pl.program_id(0)
        q = q_ref[...]
        qn[...] = q * lax.rsqrt(jnp.mean(q * q, axis=-1, keepdims=True) + EPS) * gq_ref[...] * scale
        k = k_ref[...]
        kn[...] = k * lax.rsqrt(jnp.mean(k * k, axis=-1, keepdims=True) + EPS) * gk_ref[...]
        _bias_tiles(tab_ref, rb_ref, h, bias_s)
        col = lax.broadcasted_iota(jnp.int32, (BLK, 2 * BLK), 1)

        for i, (_, dil) in enumerate(DILATED):
            def block(b, carry, i=i, dil=dil):
                rows, prows, n = _block_rows(b, n_blocks, dil)
                qb = qn[rows, :].astype(BF16)
                kc = jnp.concatenate([kn[prows, :], kn[rows, :]], axis=0).astype(BF16)
                vc = jnp.concatenate([v_ref[prows, :], v_ref[rows, :]], axis=0).astype(BF16)
                sc = _dot(qb, kc, tb=True) + bias_s[i]
                sc = jnp.where(jnp.logical_or(col >= BLK, n > 0), sc, NEG_INF)
                m_i = jnp.max(sc, axis=-1, keepdims=True)
                pr = jnp.exp(sc - m_i)
                l_i = jnp.sum(pr, axis=-1, keepdims=True)
                a_i = _dot(pr.astype(BF16), vc)
                if i == 0:
                    m_s[rows, :] = m_i
                    l_s[rows, :] = l_i
                    acc_s[rows, :] = a_i
                else:
                    m_o = m_s[rows, :]
                    m_n = jnp.maximum(m_o, m_i)
                    c_o = jnp.exp(m_o - m_n)
                    c_i = jnp.exp(m_i - m_n)
                    m_s[rows, :] = m_n
                    l_s[rows, :] = l_s[rows, :] * c_o + l_i * c_i
                    acc_s[rows, :] = acc_s[rows, :] * c_o + a_i * c_i
                return carry
            lax.fori_loop(0, n_blocks, block, 0)

        l = l_s[...]
        yb_ref[...] = acc_s[...] / l
        lse_ref[...] = m_s[...] + jnp.log(l)

    def head_col(off):
        return pl.BlockSpec((s, HEAD_DIM), lambda h: (0, off * heads + h))
    vec = pl.BlockSpec((1, HEAD_DIM), lambda h: (0, 0))
    return pl.pallas_call(
        body, name="attn_fwd", grid=(heads,),
        in_specs=[head_col(3), head_col(4), head_col(5), vec, vec,
                  pl.BlockSpec(memory_space=pltpu.SMEM),
                  pl.BlockSpec((len(DILATED), BLK, 2 * BLK), lambda h: (0, 0, 0))],
        out_specs=(pl.BlockSpec((s, HEAD_DIM), lambda h: (0, h)), pl.BlockSpec((None, s, 1), lambda h: (h, 0, 0))),
        out_shape=(jax.ShapeDtypeStruct((s, heads * HEAD_DIM), F32), jax.ShapeDtypeStruct((heads, s, 1), F32)),
        scratch_shapes=[pltpu.VMEM((s, HEAD_DIM), F32), pltpu.VMEM((s, HEAD_DIM), F32),
                        pltpu.VMEM((s, 1), F32), pltpu.VMEM((s, 1), F32), pltpu.VMEM((s, HEAD_DIM), F32),
                        pltpu.VMEM((len(DILATED), BLK, 2 * BLK), F32)],
        compiler_params=_params("arbitrary"),
    )(proj, proj, proj, g_q, g_k, rel_bias, tables)


def _attn_bwd(dproj, d_o, yb, lse, proj, g_q, g_k, rel_bias, d_model):
    s, d_in = proj.shape
    heads = d_model // 2 // HEAD_DIM
    n_blocks = s // BLK
    scale = HEAD_DIM ** -0.5
    tables = jnp.asarray(_bucket_tables())
    n_dil = len(DILATED)

    def body(dp_any, q_ref, k_ref, v_ref, do_ref, o_ref, lse_ref, gq_ref, gk_ref, rb_ref, tab_ref,
             dp_out, ggq_ref, ggk_ref, db_ref,
             qn, kn, dqn, dkn, dvv, delta_s, bias_s, dbias_s, obuf, sems):
        del dp_any
        h = pl.program_id(0)

        @pl.when(h == 0)
        def _():
            ggq_ref[...] = jnp.zeros_like(ggq_ref)
            ggk_ref[...] = jnp.zeros_like(ggk_ref)

        q = q_ref[...]
        rq = lax.rsqrt(jnp.mean(q * q, axis=-1, keepdims=True) + EPS)
        qn[...] = q * rq * gq_ref[...] * scale
        k = k_ref[...]
        rk = lax.rsqrt(jnp.mean(k * k, axis=-1, keepdims=True) + EPS)
        kn[...] = k * rk * gk_ref[...]
        delta_s[...] = jnp.sum(do_ref[...] * o_ref[...], axis=-1, keepdims=True)
        dqn[...] = jnp.zeros_like(dqn)
        dkn[...] = jnp.zeros_like(dkn)
        dvv[...] = jnp.zeros_like(dvv)
        dbias_s[...] = jnp.zeros_like(dbias_s)
        _bias_tiles(tab_ref, rb_ref, h, bias_s)
        col = lax.broadcasted_iota(jnp.int32, (BLK, 2 * BLK), 1)

        for i, (_, dil) in enumerate(DILATED):
            def block(b, carry, i=i, dil=dil):
                rows, prows, n = _block_rows(b, n_blocks, dil)
                qb = qn[rows, :].astype(BF16)
                kc = jnp.concatenate([kn[prows, :], kn[rows, :]], axis=0).astype(BF16)
                vc = jnp.concatenate([v_ref[prows, :], v_ref[rows, :]], axis=0).astype(BF16)
                dob = do_ref[rows, :].astype(BF16)
                sc = _dot(qb, kc, tb=True) + bias_s[i]
                sc = jnp.where(jnp.logical_or(col >= BLK, n > 0), sc, NEG_INF)
                pr = jnp.exp(sc - lse_ref[rows, :])
                dpr = _dot(dob, vc, tb=True)
                ds = pr * (dpr - delta_s[rows, :])
                dbias_s[i] += ds
                dsb = ds.astype(BF16)
                dqn[rows, :] += _dot(dsb, kc)
                dkc = _dot(dsb, qb, ta=True)
                dkn[prows, :] += dkc[:BLK]
                dkn[rows, :] += dkc[BLK:]
                dvc = _dot(pr.astype(BF16), dob, ta=True)
                dvv[prows, :] += dvc[:BLK]
                dvv[rows, :] += dvc[BLK:]
                return carry
            lax.fori_loop(0, n_blocks, block, 0)

        dq_n = dqn[...]
        ggq_ref[...] += jnp.sum(dq_n * (q * rq) * scale, axis=0, keepdims=True)
        dg = dq_n * gq_ref[...] * scale
        obuf[0] = (rq * (dg - q * (rq * rq) * jnp.mean(dg * q, axis=-1, keepdims=True))).astype(BF16)
        dk_n = dkn[...]
        ggk_ref[...] += jnp.sum(dk_n * (k * rk), axis=0, keepdims=True)
        dg = dk_n * gk_ref[...]
        obuf[1] = (rk * (dg - k * (rk * rk) * jnp.mean(dg * k, axis=-1, keepdims=True))).astype(BF16)
        obuf[2] = dvv[...].astype(BF16)
        copies = [pltpu.make_async_copy(obuf.at[j], dp_out.at[:, pl.ds(pl.multiple_of(((3 + j) * heads + h) * HEAD_DIM, HEAD_DIM), HEAD_DIM)],
                                        sems.at[j]) for j in range(3)]
        for cp in copies:
            cp.start()
        for b in range(NUM_BUCKETS):
            tot = jnp.zeros((BLK, 2 * BLK), F32)
            for i in range(n_dil):
                tot = tot + jnp.where(tab_ref[i] == b, dbias_s[i], 0.0)
            db_ref[b:b + 1, :] = jnp.full((1, LANE), jnp.sum(tot), F32)
        for cp in copies:
            cp.wait()

    def head_col(off):
        return pl.BlockSpec((s, HEAD_DIM), lambda h: (0, off * heads + h))
    own_col = pl.BlockSpec((s, HEAD_DIM), lambda h: (0, h))
    vec = pl.BlockSpec((1, HEAD_DIM), lambda h: (0, 0))
    big = pltpu.VMEM((s, HEAD_DIM), F32)
    return pl.pallas_call(
        body, name="attn_bwd", grid=(heads,),
        in_specs=[ANY, head_col(3), head_col(4), head_col(5), own_col, own_col,
                  pl.BlockSpec((None, s, 1), lambda h: (h, 0, 0)), vec, vec,
                  pl.BlockSpec(memory_space=pltpu.SMEM),
                  pl.BlockSpec((n_dil, BLK, 2 * BLK), lambda h: (0, 0, 0))],
        out_specs=(ANY, vec, vec, pl.BlockSpec((None, NUM_BUCKETS, LANE), lambda h: (h, 0, 0))),
        out_shape=(jax.ShapeDtypeStruct((s, d_in), BF16), jax.ShapeDtypeStruct((1, HEAD_DIM), F32),
                   jax.ShapeDtypeStruct((1, HEAD_DIM), F32), jax.ShapeDtypeStruct((heads, NUM_BUCKETS, LANE), F32)),
        scratch_shapes=[big, big, big, big, big, pltpu.VMEM((s, 1), F32),
                        pltpu.VMEM((n_dil, BLK, 2 * BLK), F32), pltpu.VMEM((n_dil, BLK, 2 * BLK), F32),
                        pltpu.VMEM((3, s, HEAD_DIM), BF16), pltpu.SemaphoreType.DMA((3,))],
        input_output_aliases={0: 0},
        compiler_params=_params("arbitrary"),
    )(dproj, proj, proj, proj, d_o, yb, lse, g_q, g_k, rel_bias, tables)


def _local_step(xs, ps, tgt, sp, win4, wout4, wgate4, wup4, dist=None):
    s, d = xs.shape
    wa = d // 2
    heads = wa // HEAD_DIM
    d_in = 7 * wa
    c4 = d_in // N_CHIPS
    dple = ps.shape[1]

    tm, tn = _tile(s, 512), _tile(d, 512)
    tn_in = _tile(c4, 512)
    per_in = c4 // tn_in
    tn_up = _tile(d // N_CHIPS, 512)
    per_up = (d // N_CHIPS) // tn_up
    gm, gn = s // tm, d // tn

    hn, rx = _rms_fwd("rms_pre", xs, sp["g_pre"])
    proj = _matmul(
        "mm_in", hn, win4, pl.BlockSpec((tm, d), lambda i, j, k: (i, 0)), _shard_spec(d, tn_in, per_in, 2, 1),
        (gm, d_in // tn_in, 1), None, jax.ShapeDtypeStruct((s, d_in), F32), pl.BlockSpec((tm, tn_in), lambda i, j, k: (i, j)),
        comm=_Gather([wout4, wgate4, wup4]) if dist else None)
    if dist:
        proj, (wout4, wgate4, wup4) = proj
    wout, wgate = wout4.reshape(d, d), wgate4.reshape(d, d)
    b_st = sp["b_s"].T
    y, z, vn, mu_v, rs_v, ra = _gmlp_fwd(proj, sp["w_s"], b_st, sp["ln_v_g"], sp["ln_v_b"], sp["g_out_a"], d)
    yb, lse = _attn_fwd(proj, sp["g_q"], sp["g_k"], sp["rel_bias"], d)
    y, rb = _gate_fwd(y, yb, proj, sp["g_out_b"], d)

    def residual(acc, extra, outs):
        outs[0][...] = extra[0][...] + acc

    tile = pl.BlockSpec((tm, tn), lambda i, j, k: (i, j))
    h = _matmul(
        "mm_out", y, wout, pl.BlockSpec((tm, d), lambda i, j, k: (i, 0)), pl.BlockSpec((d, tn), lambda i, j, k: (0, j)),
        (gm, gn, 1), None, jax.ShapeDtypeStruct((s, d), F32), tile, extras=(xs,), extra_specs=(tile,), epilogue=residual)
    hp, rh = _rms_fwd("rms_ple", h, sp["g_ple"])
    p16 = ps.astype(BF16)

    def head(acc, extra, outs):
        p_ref, wup_ref, h_ref, t_ref = extra
        dout_ref, dgl_ref, dup_ref, loss_ref = outs
        up = _dot(p_ref[...], wup_ref[...])
        gate = jax.nn.sigmoid(acc)
        err = h_ref[...] + gate * up - t_ref[...]
        sq = _colsum8(err * err)
        part = sq[:, 0:LANE]
        for c in range(1, sq.shape[1] // LANE):
            part = part + sq[:, c * LANE:(c + 1) * LANE]
        loss_ref[...] = part
        dout = err * (1.0 / d)
        dout_ref[...] = dout
        dup_ref[...] = (dout * gate).astype(BF16)
        dgl_ref[...] = (dout * up * gate * (1.0 - gate)).astype(BF16)

    tile_up = pl.BlockSpec((tm, tn_up), lambda i, j, k: (i, j))
    dout, dglin, dup, loss_parts = _matmul(
        "mm_gate_loss", hp, wgate, pl.BlockSpec((tm, d), lambda i, j, k: (i, 0)), pl.BlockSpec((d, tn_up), lambda i, j, k: (0, j)),
        (gm, d // tn_up, 1), None,
        (jax.ShapeDtypeStruct((s, d), F32), jax.ShapeDtypeStruct((s, d), BF16), jax.ShapeDtypeStruct((s, d), BF16),
         jax.ShapeDtypeStruct((gm * 8, (d // tn_up) * LANE), F32)),
        (tile_up, tile_up, tile_up, pl.BlockSpec((8, LANE), lambda i, j, k: (i, j))),
        extras=(p16, wup4, h, tgt),
        extra_specs=(pl.BlockSpec((tm, dple), lambda i, j, k: (i, 0)), _shard_spec(dple, tn_up, per_up, 2, 1),
                     tile_up, tile_up),
        epilogue=head)
    loss = 0.5 * jnp.sum(loss_parts) * (1.0 / d)

    dhp = _matmul(
        "mm_dhp", dglin, wgate, pl.BlockSpec((tm, d), lambda i, j, k: (i, 0)), pl.BlockSpec((tn, d), lambda i, j, k: (j, 0)),
        (gm, gn, 1), None, jax.ShapeDtypeStruct((s, d), F32), tile, tb=True)
    dh, dh16, g_g_ple = _rms_bwd("rms_ple_bwd", dhp, h, rh, sp["g_ple"], dout, True)
    tmw = _tile(d, 512)
    g_wgate = _matmul(
        "mm_gwgate", hp, dglin, pl.BlockSpec((s, tmw), lambda i, j, k: (0, i)), pl.BlockSpec((s, tn), lambda i, j, k: (0, j)),
        (d // tmw, gn, 1), None, jax.ShapeDtypeStruct((d, d), F32), pl.BlockSpec((tmw, tn), lambda i, j, k: (i, j)), ta=True)
    g_wup = _matmul(
        "mm_gwup", p16, dup, pl.BlockSpec((s, dple), lambda i, j, k: (0, 0)), pl.BlockSpec((s, tn_up), lambda i, j, k: (0, j)),
        (1, d // tn_up, 1), None, jax.ShapeDtypeStruct((N_CHIPS, dple, d // N_CHIPS), F32),
        _shard_spec(dple, tn_up, per_up, 0, 1), ta=True)
    dy = _matmul(
        "mm_dy", dh16, wout, pl.BlockSpec((tm, d), lambda i, j, k: (i, 0)), pl.BlockSpec((tn, d), lambda i, j, k: (j, 0)),
        (gm, gn, 1), None, jax.ShapeDtypeStruct((s, d), F32), tile, tb=True)
    g_wout = _matmul(
        "mm_gwout", y, dh16, pl.BlockSpec((s, tmw), lambda i, j, k: (0, i)), pl.BlockSpec((s, tn), lambda i, j, k: (0, j)),
        (d // tmw, gn, 1), None, jax.ShapeDtypeStruct((d, d), F32), pl.BlockSpec((tmw, tn), lambda i, j, k: (i, j)), ta=True)

    dproj, d_o, g_g_out_b = _gate_bwd(dy, yb, proj, rb, sp["g_out_b"], d)
    dproj, g_g_q, g_g_k, dbias = _attn_bwd(dproj, d_o, yb, lse, proj, sp["g_q"], sp["g_k"], sp["rel_bias"], d)
    dproj, g_w_s, dz_sum, g_ln_g, g_ln_b, g_g_out_a = _gmlp_bwd(
        dproj, dy, proj, z, vn, mu_v, rs_v, ra, sp["w_s"], sp["ln_v_g"], sp["g_out_a"], d)

    g_wout4 = g_wout.reshape(N_CHIPS, d // N_CHIPS, d)
    g_wgate4 = g_wgate.reshape(N_CHIPS, d // N_CHIPS, d)
    early = (g_wout4, g_wgate4, g_wup)
    if dist:
        early_sums = [_chip_sum("chip_sum_" + n, dist["core"], g, o)
                      for n, g, o in zip(LARGE[1:], early, _swap_halves("swap_early", list(early)))]
    g_win = _matmul(
        "mm_gwin", hn, dproj, pl.BlockSpec((s, tmw), lambda i, j, k: (0, i)), pl.BlockSpec((s, tn_in), lambda i, j, k: (0, j)),
        (d // tmw, d_in // tn_in, 1), None, jax.ShapeDtypeStruct((N_CHIPS, d, c4), F32),
        _shard_spec(tmw, tn_in, per_in, 0, 1), ta=True,
        comm=_ExchangeChips([s16 for _, s16 in early_sums]) if dist else None)
    if dist:
        g_win, early_got = g_win
        win_sum = _chip_sum("chip_sum_w_in", dist["core"], g_win, _swap_halves("swap_w_in", [g_win])[0])
    tk = tn_in
    tmh, tnh = _tile(s, 1024), _tile(d, 1024)
    dhn = _matmul(
        "mm_dhn", dproj, win4, pl.BlockSpec((tmh, tk), lambda i, j, k: (i, k)), _shard_spec(tnh, tk, per_in, 1, 2),
        (s // tmh, d // tnh, d_in // tk), (tmh, tnh), jax.ShapeDtypeStruct((s, d), F32),
        pl.BlockSpec((tmh, tnh), lambda i, j, k: (i, j)), tb=True,
        comm=_ExchangeChips([win_sum[1]]) if dist else None)
    if dist:
        dhn, win_got = dhn
    grad_x, g_g_pre = _rms_bwd("rms_pre_bwd", dhn, xs, rx, sp["g_pre"], dh, False)

    small = {
        "g_pre": g_g_pre,
        "w_s": g_w_s,
        "b_s": jnp.sum(dz_sum.reshape(CHUNK, heads, HEAD_DIM), axis=-1).T,
        "ln_v_g": g_ln_g, "ln_v_b": g_ln_b,
        "g_q": g_g_q, "g_k": g_g_k,
        "rel_bias": dbias[:, :, 0].T,
        "g_out_a": g_g_out_a, "g_out_b": g_g_out_b,
        "g_ple": g_g_ple,
    }
    if not dist:
        return loss, grad_x, (g_win, *early), small
    sums32 = [win_sum[0]] + [s32 for s32, _ in early_sums]
    blocks = [_total("total_" + n, dist["chip"], dist["core"], s32, o)
              for n, s32, o in zip(LARGE, sums32, list(win_got) + list(early_got))]
    return loss, grad_x, _join_halves(blocks), small


def _place():
    x, y, c = lax.axis_index("x"), lax.axis_index("y"), lax.axis_index("c")
    chips = [(1 - x, y), (x, 1 - y), (1 - x, 1 - y)]
    return x, y, c, chips


def _remote(src, dst, send_sems, recv_sems, k, to):
    return pltpu.make_async_remote_copy(src_ref=src, dst_ref=dst, send_sem=send_sems.at[k], recv_sem=recv_sems.at[k],
                                        device_id=to, device_id_type=MESH)


def _cast_bf16(name, chip, w):
    r, c = w.shape
    tm = _tile(r, 256)

    def body(chip_ref, w_ref, o_ref):
        del chip_ref
        o_ref[...] = w_ref[...].astype(BF16)

    return pl.pallas_call(
        body, name=name,
        grid_spec=pltpu.PrefetchScalarGridSpec(
            num_scalar_prefetch=1, grid=(r // tm,),
            in_specs=[pl.BlockSpec((tm, c), lambda i, chip_ref: (i, 0))],
            out_specs=pl.BlockSpec((None, tm, c), lambda i, chip_ref: (chip_ref[0], i, 0))),
        out_shape=jax.ShapeDtypeStruct((N_CHIPS, r, c), BF16), compiler_params=_params("parallel"),
    )(chip, w)


class _Gather:
    in_place = True

    def __init__(self, fulls):
        self.ins = list(fulls)
        self.out_shape = [jax.ShapeDtypeStruct(f.shape, f.dtype) for f in fulls]
        n = len(fulls)
        self.scratch = [pltpu.SemaphoreType.DMA((6 * n,)), pltpu.SemaphoreType.DMA((6 * n,))]

    @staticmethod
    def _sends(outs, sems):
        send_sems, recv_sems = sems
        x, y, c, chips = _place()
        cps = []
        for w, ref in enumerate(outs):
            half = ref.shape[1] // 2
            blk = ref.at[2 * x + y, pl.ds(c * half, half)]
            cps += [_remote(blk, blk, send_sems, recv_sems, 6 * w + q, (*chip, c)) for q, chip in enumerate(chips)]
        return cps

    def start(self, ins, outs, sems):
        for cp in self._sends(outs, sems):
            cp.start()

    def finish(self, ins, outs, sems):
        send_sems, recv_sems = sems
        x, y, c, chips = _place()
        sibling = (x, y, 1 - c)
        forwards = []
        for w, ref in enumerate(outs):
            half = ref.shape[1] // 2
            for q, chip in enumerate(chips):
                blk = ref.at[2 * chip[0] + chip[1], pl.ds(c * half, half)]
                _remote(blk, blk, send_sems, recv_sems, 6 * w + q, sibling).wait_recv()
                fwd = _remote(blk, blk, send_sems, recv_sems, 6 * w + 3 + q, sibling)
                fwd.start()
                forwards.append(fwd)
        for w, ref in enumerate(outs):
            half = ref.shape[1] // 2
            for q, chip in enumerate(chips):
                blk = ref.at[2 * chip[0] + chip[1], pl.ds((1 - c) * half, half)]
                _remote(blk, blk, send_sems, recv_sems, 6 * w + 3 + q, sibling).wait_recv()
        for cp in self._sends(outs, sems) + forwards:
            cp.wait_send()


class _ExchangeChips:
    in_place = False

    def __init__(self, sums16):
        self.ins = list(sums16)
        self.out_shape = [jax.ShapeDtypeStruct(g.shape, g.dtype) for g in sums16]
        n = len(sums16)
        self.scratch = [pltpu.SemaphoreType.DMA((3 * n,)), pltpu.SemaphoreType.DMA((3 * n,))]

    @staticmethod
    def _sends(ins, outs, sems):
        send_sems, recv_sems = sems
        x, y, c, chips = _place()
        return [_remote(ins[w].at[2 * chip[0] + chip[1]], outs[w].at[2 * x + y], send_sems, recv_sems, 3 * w + q, (*chip, c))
                for w in range(len(ins)) for q, chip in enumerate(chips)]

    def start(self, ins, outs, sems):
        for cp in self._sends(ins, outs, sems):
            cp.start()

    def finish(self, ins, outs, sems):
        send_sems, recv_sems = sems
        x, y, c, chips = _place()
        for w in range(len(ins)):
            for q, chip in enumerate(chips):
                blk = outs[w].at[2 * chip[0] + chip[1]]
                _remote(blk, blk, send_sems, recv_sems, 3 * w + q, (*chip, c)).wait_recv()
        for cp in self._sends(ins, outs, sems):
            cp.wait_send()


def _run_comm(name, comm):
    n_in, n_out = len(comm.ins), len(comm.out_shape)

    def body(*refs):
        ins, outs, sems = refs[:n_in], refs[n_in:n_in + n_out], refs[n_in + n_out:]
        comm.start(ins, outs, sems)
        comm.finish(ins, outs, sems)

    return pl.pallas_call(
        body, name=name, in_specs=[ANY] * n_in, out_specs=[ANY] * n_out, out_shape=comm.out_shape,
        scratch_shapes=comm.scratch, input_output_aliases={i: i for i in range(n_in)} if comm.in_place else {},
    )(*comm.ins)


def _swap_halves(name, grads):
    nw = len(grads)

    def body(*refs):
        ins, outs = refs[:nw], refs[nw:2 * nw]
        send_sems, recv_sems = refs[2 * nw:]
        x, y, c, _ = _place()
        copies = []
        for w in range(nw):
            half = ins[w].shape[1] // 2
            cp = _remote(ins[w].at[:, pl.ds((1 - c) * half, half)], outs[w], send_sems, recv_sems, w, (x, y, 1 - c))
            cp.start()
            copies.append(cp)
        for cp in copies:
            cp.wait()

    return pl.pallas_call(
        body, name=name,
        in_specs=[ANY] * nw, out_specs=[ANY] * nw,
        out_shape=[jax.ShapeDtypeStruct((N_CHIPS, g.shape[1] // 2, g.shape[2]), g.dtype) for g in grads],
        scratch_shapes=[pltpu.SemaphoreType.DMA((nw,)), pltpu.SemaphoreType.DMA((nw,))],
    )(*grads)


def _chip_sum(name, core, grad, got):
    _, r, c = grad.shape
    half = r // 2
    th = _tile(half, 128)
    n = half // th

    def body(core_ref, g_ref, o_ref, s32_ref, s16_ref):
        del core_ref
        v = g_ref[...] + o_ref[...]
        s32_ref[...] = v
        s16_ref[...] = v.astype(BF16)

    blk = pl.BlockSpec((None, th, c), lambda k, i, core_ref: (k, i, 0))
    return pl.pallas_call(
        body, name=name,
        grid_spec=pltpu.PrefetchScalarGridSpec(
            num_scalar_prefetch=1, grid=(N_CHIPS, n),
            in_specs=[pl.BlockSpec((None, th, c), lambda k, i, core_ref: (k, core_ref[0] * n + i, 0)), blk],
            out_specs=[blk, blk]),
        out_shape=[jax.ShapeDtypeStruct((N_CHIPS, half, c), F32), jax.ShapeDtypeStruct((N_CHIPS, half, c), BF16)],
        compiler_params=_params("parallel", "parallel"),
    )(core, grad, got)


def _total(name, chip, core, sum32, got16):
    _, half, c = sum32.shape
    th = _tile(half, 128)
    n = half // th

    def body(chip_ref, core_ref, own_ref, a_ref, b_ref, c_ref, o_ref):
        del chip_ref, core_ref
        o_ref[...] = ((own_ref[...] + a_ref[...].astype(F32)) + b_ref[...].astype(F32)) + c_ref[...].astype(F32)

    def other(step):
        return pl.BlockSpec((None, th, c), lambda i, chip_ref, core_ref: ((chip_ref[0] + step) % N_CHIPS, i, 0))

    return pl.pallas_call(
        body, name=name,
        grid_spec=pltpu.PrefetchScalarGridSpec(
            num_scalar_prefetch=2, grid=(n,),
            in_specs=[other(0), other(1), other(2), other(3)],
            out_specs=pl.BlockSpec((th, c), lambda i, chip_ref, core_ref: (core_ref[0] * n + i, 0))),
        out_shape=jax.ShapeDtypeStruct((2 * half, c), F32),
        compiler_params=_params("parallel"),
    )(chip, core, sum32, got16, got16, got16)


def _join_halves(blocks):
    nw = len(blocks)

    def body(*refs):
        outs = refs[nw:2 * nw]
        send_sems, recv_sems = refs[2 * nw:]
        x, y, c, _ = _place()
        copies = []
        for w in range(nw):
            half = outs[w].shape[0] // 2
            rows = outs[w].at[pl.ds(c * half, half)]
            cp = _remote(rows, rows, send_sems, recv_sems, w, (x, y, 1 - c))
            cp.start()
            copies.append(cp)
        for w, cp in enumerate(copies):
            cp.wait_send()
            half = outs[w].shape[0] // 2
            rows = outs[w].at[pl.ds((1 - c) * half, half)]
            _remote(rows, rows, send_sems, recv_sems, w, (x, y, 1 - c)).wait_recv()

    return pl.pallas_call(
        body, name="join_halves",
        in_specs=[ANY] * nw, out_specs=[ANY] * nw,
        out_shape=[jax.ShapeDtypeStruct(t.shape, t.dtype) for t in blocks],
        scratch_shapes=[pltpu.SemaphoreType.DMA((nw,)), pltpu.SemaphoreType.DMA((nw,))],
        input_output_aliases={i: i for i in range(nw)},
    )(*blocks)


def _gather_small(pack):
    m_per, n = pack.shape

    def body(x_ref, out_ref, send_sems, recv_sems, local_sem):
        x, y, c, chips = _place()
        me, sibling = (x, y, c), (x, y, 1 - c)

        def rows(px, py, pc):
            return out_ref.at[pl.ds((4 * px + 2 * py + pc) * m_per, m_per), :]

        def copy(k, block, to, src=None):
            return _remote(rows(*block) if src is None else src, rows(*block), send_sems, recv_sems, k, to)

        mine = pltpu.make_async_copy(x_ref, rows(*me), local_sem)
        mine.start()
        first = [copy(0, me, sibling, src=x_ref)]
        first += [copy(1 + j, me, (*chip, c), src=x_ref) for j, chip in enumerate(chips)]
        for cp in first:
            cp.start()
        passed = [copy(4 + j, (*chip, c), sibling) for j, chip in enumerate(chips)]
        for j, chip in enumerate(chips):
            copy(1 + j, (*chip, c), me).wait_recv()
            passed[j].start()
        copy(0, sibling, me).wait_recv()
        for j, chip in enumerate(chips):
            copy(4 + j, (*chip, 1 - c), me).wait_recv()
        for cp in first + passed:
            cp.wait_send()
        mine.wait()

    return pl.pallas_call(
        body, name="gather_small",
        out_shape=jax.ShapeDtypeStruct((N_DEV * m_per, n), pack.dtype),
        in_specs=[pl.BlockSpec(memory_space=pltpu.VMEM)],
        out_specs=pl.BlockSpec(memory_space=pltpu.VMEM),
        scratch_shapes=[pltpu.SemaphoreType.DMA((7,)), pltpu.SemaphoreType.DMA((7,)), pltpu.SemaphoreType.DMA],
        compiler_params=_params(),
    )(pack)


def _adamw_math(w, g, m, v):
    m = ADAM_B1 * m + (1.0 - ADAM_B1) * g
    v = ADAM_B2 * v + (1.0 - ADAM_B2) * (g * g)
    m_hat = m / (1.0 - ADAM_B1 ** ADAM_STEP)
    v_hat = v / (1.0 - ADAM_B2 ** ADAM_STEP)
    delta = -ADAM_LR * (m_hat / (jnp.sqrt(v_hat) + ADAM_EPS) + ADAM_WD * w)
    return delta, m, v


def _adamw(name, w, g, m, v):
    r, c = w.shape
    tm = _tile(r, 128)

    def body(w_ref, g_ref, m_ref, v_ref, d_out, m_out, v_out):
        d_out[...], m_out[...], v_out[...] = _adamw_math(w_ref[...], g_ref[...], m_ref[...], v_ref[...])

    spec = pl.BlockSpec((tm, c), lambda i: (i, 0))
    return pl.pallas_call(
        body, name=name, grid=(r // tm,), in_specs=[spec] * 4, out_specs=[spec] * 3,
        out_shape=[jax.ShapeDtypeStruct((r, c), F32)] * 3, compiler_params=_params("parallel"),
    )(w, g, m, v)


def _adamw_small(gathered, w, m, v):
    rows = w.shape[0]

    def body(all_ref, w_ref, m_ref, v_ref, g_out, d_out, m_out, v_out):
        g = all_ref[0:rows, :]
        for dev in range(1, N_DEV):
            g = g + all_ref[dev * rows:(dev + 1) * rows, :]
        g_out[...] = g
        d_out[...], m_out[...], v_out[...] = _adamw_math(w_ref[...], g, m_ref[...], v_ref[...])

    return pl.pallas_call(
        body, name="adamw_small", out_shape=[jax.ShapeDtypeStruct(w.shape, F32)] * 4, compiler_params=_params(),
    )(gathered, w, m, v)


SMALL = ("g_pre", "w_s", "b_s", "ln_v_g", "ln_v_b", "g_q", "g_k", "rel_bias", "g_out_a", "g_out_b", "g_ple")
LARGE = ("w_in", "w_out", "w_ple_gate", "w_ple_up")
WEIGHTS = ("g_pre", "w_in", "w_s", "b_s", "ln_v_g", "ln_v_b", "g_q", "g_k", "rel_bias", "g_out_a", "g_out_b", "w_out",
           "g_ple", "w_ple_gate", "w_ple_up")


def _pack(parts):
    flat = jnp.concatenate([parts[n].reshape(-1).astype(F32) for n in SMALL])
    rows = -(-flat.shape[0] // (8 * LANE)) * 8
    return jnp.pad(flat, (0, rows * LANE - flat.shape[0])).reshape(rows, LANE)


def _unpack(pack, like):
    flat = pack.reshape(-1)
    out, at = {}, 0
    for n in SMALL:
        size = math.prod(like[n].shape)
        out[n] = flat[at:at + size].reshape(like[n].shape)
        at += size
    return out


def kernel(x, p, g_pre, w_in, w_s, b_s, ln_v_g, ln_v_b, g_q, g_k, rel_bias, g_out_a, g_out_b, w_out, g_ple, w_ple_gate, w_ple_up, loss_target, m_g_pre, m_w_in, m_w_s, m_b_s, m_ln_v_g, m_ln_v_b, m_g_q, m_g_k, m_rel_bias, m_g_out_a, m_g_out_b, m_w_out, m_g_ple, m_w_ple_gate, m_w_ple_up, v_g_pre, v_w_in, v_w_s, v_b_s, v_ln_v_g, v_ln_v_b, v_g_q, v_g_k, v_rel_bias, v_g_out_a, v_g_out_b, v_w_out, v_g_ple, v_w_ple_gate, v_w_ple_up):
    given = dict(locals())
    weights = {n: given[n] for n in WEIGHTS}
    mom_m = {n: given["m_" + n] for n in WEIGHTS}
    mom_v = {n: given["v_" + n] for n in WEIGHTS}
    xs, ps, tgt = x[0], p[0, 0], loss_target[0]
    d = xs.shape[1]

    core = lax.axis_index("c").astype(jnp.int32).reshape(1)
    chip = (2 * lax.axis_index("x") + lax.axis_index("y")).astype(jnp.int32).reshape(1)

    win4, wout4, wgate4, wup4 = [_cast_bf16("cast_" + n, chip, weights[n][0]) for n in LARGE]
    win4, = _run_comm("gather_w_in", _Gather([win4]))

    sp = {
        "g_pre": g_pre, "w_s": w_s[0], "b_s": b_s[0], "ln_v_g": ln_v_g, "ln_v_b": ln_v_b, "g_q": g_q, "g_k": g_k,
        "rel_bias": rel_bias, "g_out_a": g_out_a, "g_out_b": g_out_b, "g_ple": g_ple,
    }
    loss_local, grad_x, blocks, small = _local_step(xs, ps, tgt, sp, win4, wout4, wgate4, wup4,
                                                    dist={"chip": chip, "core": core})
    loss = lax.psum(loss_local, MESH_AXES)
    grads = dict(zip(LARGE, blocks))

    out_g, out_d, out_m, out_v = {}, {}, {}, {}
    for n in LARGE:
        out_g[n] = grads[n][None]
        dl, mn, vn = _adamw("adamw_" + n, weights[n][0], grads[n], mom_m[n][0], mom_v[n][0])
        out_d[n], out_m[n], out_v[n] = dl[None], mn[None], vn[None]

    gathered = _gather_small(_pack(small))
    pg, pd, pm, pv = _adamw_small(gathered, _pack(weights), _pack(mom_m), _pack(mom_v))
    for packed, out in ((pg, out_g), (pd, out_d), (pm, out_m), (pv, out_v)):
        out.update(_unpack(packed, weights))

    return (loss, grad_x[None], *[out_g[n] for n in WEIGHTS], *[out_d[n] for n in WEIGHTS],
            *[out_m[n] for n in WEIGHTS], *[out_v[n] for n in WEIGHTS])
```

```python
import functools
import math

import numpy as np

import jax
import jax.numpy as jnp
from jax import lax
from jax.experimental import pallas as pl
from jax.experimental.pallas import tpu as pltpu

F32 = jnp.float32
BF16 = jnp.bfloat16

HEAD_DIM = 128
CHUNK = 128
BLK = 128
DILATED = ((128, 1), (512, 4), (2048, 16))
NUM_BUCKETS = 32
MAX_DISTANCE = 2048
EPS = 1e-6
NEG_INF = -1e30
N_CHIPS = 4
N_DEV = 8
MESH_AXES = ("x", "y", "c")

ADAM_LR = 0.001
ADAM_B1 = 0.9
ADAM_B2 = 0.999
ADAM_EPS = 1e-08
ADAM_WD = 0.01
ADAM_STEP = 10

V7X_VMEM_LIMIT = 56 * 1024 * 1024
LANE = 128
MESH = pl.DeviceIdType.MESH
ANY = pl.BlockSpec(memory_space=pl.ANY)


def _params(*sem):
    return pltpu.CompilerParams(dimension_semantics=sem or None, vmem_limit_bytes=V7X_VMEM_LIMIT)


def _tile(n, target):
    if n <= target:
        return n
    t = (target // LANE) * LANE
    while t > LANE and n % t:
        t -= LANE
    assert n % t == 0, (n, target)
    return t


def _gelu(x):
    return 0.5 * x * (1.0 + lax.erf(x * (1.0 / math.sqrt(2.0))))


def _gelu_grad(x):
    return 0.5 * (1.0 + lax.erf(x * (1.0 / math.sqrt(2.0)))) + x * jnp.exp(-0.5 * x * x) * (1.0 / math.sqrt(2.0 * math.pi))


def _silu_and_grad(x):
    s = jax.nn.sigmoid(x)
    return x * s, s * (1.0 + x * (1.0 - s))


def _dot(a, b, ta=False, tb=False):
    return lax.dot_general(a, b, (((0 if ta else 1,), (1 if tb else 0,)), ((), ())), preferred_element_type=F32)


def _colsum8(v):
    return jnp.sum(v.reshape(v.shape[0] // 8, 8, v.shape[1]), axis=0)


def _matmul(name, a, b, a_spec, b_spec, grid, acc_shape, out_shape, out_specs, ta=False, tb=False,
            extras=(), extra_specs=(), epilogue=None, comm=None):
    nk = grid[2]
    n_extra = len(extras)
    single = not isinstance(out_shape, (tuple, list))
    outs_shape = (out_shape,) if single else tuple(out_shape)
    outs_specs = (out_specs,) if single else tuple(out_specs)
    n_out = len(outs_shape)
    c_ins = list(comm.ins) if comm else []
    c_outs = list(comm.out_shape) if comm else []
    c_scratch = list(comm.scratch) if comm else []
    n_cin, n_cout = len(c_ins), len(c_outs)

    def finish(acc, extra_refs, out_refs):
        if epilogue is None:
            out_refs[0][...] = acc.astype(out_refs[0].dtype)
        else:
            epilogue(acc, extra_refs, out_refs)

    def body(a_ref, b_ref, *rest):
        extra_refs = rest[:n_extra]
        cin_refs = rest[n_extra:n_extra + n_cin]
        out_refs = rest[n_extra + n_cin:n_extra + n_cin + n_out]
        cout_refs = rest[n_extra + n_cin + n_out:n_extra + n_cin + n_out + n_cout]
        scratch_refs = rest[n_extra + n_cin + n_out + n_cout:]
        ids = [pl.program_id(ax) for ax in range(3)]
        if comm:
            sems = scratch_refs[len(scratch_refs) - len(c_scratch):]

            @pl.when((ids[0] == 0) & (ids[1] == 0) & (ids[2] == 0))
            def _():
                comm.start(cin_refs, cout_refs, sems)

        if nk == 1:
            finish(_dot(a_ref[...], b_ref[...], ta, tb), extra_refs, out_refs)
        else:
            acc_ref = scratch_refs[0]

            @pl.when(ids[2] == 0)
            def _():
                acc_ref[...] = jnp.zeros_like(acc_ref)

            acc_ref[...] += _dot(a_ref[...], b_ref[...], ta, tb)

            @pl.when(ids[2] == nk - 1)
            def _():
                finish(acc_ref[...], extra_refs, out_refs)

        if comm:
            @pl.when((ids[0] == grid[0] - 1) & (ids[1] == grid[1] - 1) & (ids[2] == nk - 1))
            def _():
                comm.finish(cin_refs, cout_refs, sems)

    scratch = ([] if nk == 1 else [pltpu.VMEM(acc_shape, F32)]) + c_scratch
    aliases = {2 + n_extra + i: n_out + i for i in range(n_cin)} if (comm and comm.in_place) else {}
    res = pl.pallas_call(
        body, name=name, grid=grid,
        in_specs=[a_spec, b_spec, *extra_specs] + [ANY] * n_cin,
        out_specs=list(outs_specs) + [ANY] * n_cout, out_shape=list(outs_shape) + c_outs, scratch_shapes=scratch,
        input_output_aliases=aliases,
        compiler_params=_params(*(("arbitrary",) * 3 if comm else ("parallel", "parallel", "arbitrary"))),
    )(a, b, *extras, *c_ins)
    if comm:
        main = res[:n_out]
        return (main[0] if single else main), res[n_out:]
    return res[0] if single else res


def _shard_spec(rows, cols, per_shard, row_axis, col_axis):
    def index(i, j, k):
        g = (i, j, k)
        return (g[col_axis] // per_shard, g[row_axis], g[col_axis] % per_shard)
    return pl.BlockSpec((None, rows, cols), index)


def _rms_fwd(name, x, g):
    s, d = x.shape
    tm = _tile(s, 256)

    def body(x_ref, g_ref, y_ref, r_ref):
        xf = x_ref[...]
        r = lax.rsqrt(jnp.mean(xf * xf, axis=-1, keepdims=True) + EPS)
        y_ref[...] = (xf * r * g_ref[...]).astype(BF16)
        r_ref[...] = r

    return pl.pallas_call(
        body, name=name, grid=(s // tm,),
        in_specs=[pl.BlockSpec((tm, d), lambda i: (i, 0)), pl.BlockSpec((1, d), lambda i: (0, 0))],
        out_specs=(pl.BlockSpec((tm, d), lambda i: (i, 0)), pl.BlockSpec((tm, 1), lambda i: (i, 0))),
        out_shape=(jax.ShapeDtypeStruct((s, d), BF16), jax.ShapeDtypeStruct((s, 1), F32)),
        compiler_params=_params("parallel"),
    )(x, g)


def _rms_bwd(name, dy, x, r, g, skip, with_bf16):
    s, d = x.shape
    tm = _tile(s, 256)
    n = s // tm

    def body(dy_ref, x_ref, r_ref, g_ref, skip_ref, *outs):
        dx_ref = outs[0]
        dg_ref = outs[-2]
        acc_ref = outs[-1]
        i = pl.program_id(0)
        dyv, xv, rv = dy_ref[...], x_ref[...], r_ref[...]

        @pl.when(i == 0)
        def _():
            acc_ref[...] = jnp.zeros_like(acc_ref)

        acc_ref[...] += _colsum8(dyv * xv * rv)
        dg = dyv * g_ref[...]
        dx = skip_ref[...] + rv * (dg - xv * (rv * rv) * jnp.mean(dg * xv, axis=-1, keepdims=True))
        dx_ref[...] = dx
        if with_bf16:
            outs[1][...] = dx.astype(BF16)

        @pl.when(i == n - 1)
        def _():
            dg_ref[...] = jnp.sum(acc_ref[...], axis=0, keepdims=True)

    row = pl.BlockSpec((tm, d), lambda i: (i, 0))
    vec = pl.BlockSpec((1, d), lambda i: (0, 0))
    out_specs = [row] + ([row] if with_bf16 else []) + [vec]
    out_shape = [jax.ShapeDtypeStruct((s, d), F32)] + ([jax.ShapeDtypeStruct((s, d), BF16)] if with_bf16 else []) \
        + [jax.ShapeDtypeStruct((1, d), F32)]
    return pl.pallas_call(
        body, name=name, grid=(n,),
        in_specs=[row, row, pl.BlockSpec((tm, 1), lambda i: (i, 0)), vec, row],
        out_specs=out_specs, out_shape=out_shape, scratch_shapes=[pltpu.VMEM((8, d), F32)],
        compiler_params=_params("arbitrary"),
    )(dy, x, r, g, skip)


def _causal(w):
    t = lax.broadcasted_iota(jnp.int32, w.shape, 0)
    s_ = lax.broadcasted_iota(jnp.int32, w.shape, 1)
    return jnp.where(t >= s_, w, 0.0)


def _gmlp_fwd(proj, w_s, b_st, ln_g, ln_b, g_out, d_model):
    s = proj.shape[0]
    wa = d_model // 2
    groups = wa // HEAD_DIM
    tm = _tile(s, 256)
    n_chunks = tm // CHUNK

    def body(au_ref, av_ref, az_ref, ws_ref, bst_ref, lng_ref, lnb_ref, go_ref,
             y_ref, z_ref, vn_ref, mu_ref, rs_ref, ra_ref):
        gv = _gelu(av_ref[...])
        mu = jnp.mean(gv, axis=-1, keepdims=True)
        xc = gv - mu
        rs = lax.rsqrt(jnp.mean(xc * xc, axis=-1, keepdims=True) + EPS)
        vn = (xc * rs * lng_ref[...] + lnb_ref[...]).astype(BF16)
        vn_ref[...] = vn
        mu_ref[...] = mu
        rs_ref[...] = rs
        for g in range(groups):
            wm = _causal(ws_ref[g]).astype(BF16)
            cols = slice(g * HEAD_DIM, (g + 1) * HEAD_DIM)
            for ch in range(n_chunks):
                rows = slice(ch * CHUNK, (ch + 1) * CHUNK)
                z_ref[rows, cols] = _dot(wm, vn_ref[rows, cols]) + bst_ref[:, g:g + 1]
        ya = _gelu(au_ref[...]) * z_ref[...]
        ra = lax.rsqrt(jnp.mean(ya * ya, axis=-1, keepdims=True) + EPS)
        ra_ref[...] = ra
        sz, _ = _silu_and_grad(az_ref[...])
        y_ref[...] = (ya * ra * go_ref[...] * sz).astype(BF16)

    def col(j):
        return pl.BlockSpec((tm, wa), lambda i: (i, j))
    vec = pl.BlockSpec((1, wa), lambda i: (0, 0))
    stat = pl.BlockSpec((tm, 1), lambda i: (i, 0))
    return pl.pallas_call(
        body, name="gmlp_fwd", grid=(s // tm,),
        in_specs=[col(0), col(1), col(2),
                  pl.BlockSpec((groups, CHUNK, CHUNK), lambda i: (0, 0, 0)),
                  pl.BlockSpec((CHUNK, groups), lambda i: (0, 0)), vec, vec, vec],
        out_specs=(col(0), col(0), col(0), stat, stat, stat),
        out_shape=(jax.ShapeDtypeStruct((s, d_model), BF16), jax.ShapeDtypeStruct((s, wa), F32),
                   jax.ShapeDtypeStruct((s, wa), BF16), jax.ShapeDtypeStruct((s, 1), F32),
                   jax.ShapeDtypeStruct((s, 1), F32), jax.ShapeDtypeStruct((s, 1), F32)),
        compiler_params=_params("parallel"),
    )(proj, proj, proj, w_s, b_st, ln_g, ln_b, g_out)


def _gmlp_bwd(dproj, dy, proj, z, vn, mu, rs, ra, w_s, ln_g, g_out, d_model):
    s = proj.shape[0]
    wa = d_model // 2
    groups = wa // HEAD_DIM
    tm = _tile(s, 256)
    n_chunks = tm // CHUNK
    n = s // tm

    def causal_stack(w):
        t = lax.broadcasted_iota(jnp.int32, w.shape, 1)
        s_ = lax.broadcasted_iota(jnp.int32, w.shape, 2)
        return jnp.where(t >= s_, w, 0.0)

    def body(dproj_in, dy_ref, au_ref, av_ref, az_ref, z_ref, vn_ref, mu_ref, rs_ref, ra_ref, ws_ref, lng_ref, go_ref,
             dp_ref, gws_ref, dzs_ref, glg_ref, glb_ref, ggo_ref,
             dz_s, dvn_s, acc_lg, acc_lb, acc_go):
        del dproj_in
        i = pl.program_id(0)

        @pl.when(i == 0)
        def _():
            gws_ref[...] = jnp.zeros_like(gws_ref)
            dzs_ref[...] = jnp.zeros_like(dzs_ref)
            acc_lg[...] = jnp.zeros_like(acc_lg)
            acc_lb[...] = jnp.zeros_like(acc_lb)
            acc_go[...] = jnp.zeros_like(acc_go)

        au, az, zv, rav = au_ref[...], az_ref[...], z_ref[...], ra_ref[...]
        u = _gelu(au)
        ya = u * zv
        sz, dsz = _silu_and_grad(az)
        dyv = dy_ref[...]
        dp_ref[:, 2 * wa:3 * wa] = (dyv * (ya * rav * go_ref[...]) * dsz).astype(BF16)
        dn = dyv * sz
        acc_go[...] += _colsum8(dn * ya * rav)
        dyg = dn * go_ref[...]
        dya = rav * (dyg - ya * (rav * rav) * jnp.mean(dyg * ya, axis=-1, keepdims=True))
        dp_ref[:, 0:wa] = (dya * zv * _gelu_grad(au)).astype(BF16)
        dz_s[...] = dya * u
        for ch in range(n_chunks):
            dzs_ref[...] += dz_s[ch * CHUNK:(ch + 1) * CHUNK, :]
        for g in range(groups):
            wm = _causal(ws_ref[g]).astype(BF16)
            cols = slice(g * HEAD_DIM, (g + 1) * HEAD_DIM)
            for ch in range(n_chunks):
                rows = slice(ch * CHUNK, (ch + 1) * CHUNK)
                dzb = dz_s[rows, cols].astype(BF16)
                gws_ref[g] += _dot(dzb, vn_ref[rows, cols], tb=True)
                dvn_s[rows, cols] = _dot(wm, dzb, ta=True)
        av = av_ref[...]
        xh = (_gelu(av) - mu_ref[...]) * rs_ref[...]
        dvn = dvn_s[...]
        acc_lb[...] += _colsum8(dvn)
        acc_lg[...] += _colsum8(dvn * xh)
        dxh = dvn * lng_ref[...]
        dgv = rs_ref[...] * (dxh - jnp.mean(dxh, axis=-1, keepdims=True) - xh * jnp.mean(dxh * xh, axis=-1, keepdims=True))
        dp_ref[:, wa:2 * wa] = (dgv * _gelu_grad(av)).astype(BF16)

        @pl.when(i == n - 1)
        def _():
            gws_ref[...] = causal_stack(gws_ref[...])
            glg_ref[...] = jnp.sum(acc_lg[...], axis=0, keepdims=True)
            glb_ref[...] = jnp.sum(acc_lb[...], axis=0, keepdims=True)
            ggo_ref[...] = jnp.sum(acc_go[...], axis=0, keepdims=True)

    def col(j):
        return pl.BlockSpec((tm, wa), lambda i: (i, j))
    vec = pl.BlockSpec((1, wa), lambda i: (0, 0))
    stat = pl.BlockSpec((tm, 1), lambda i: (i, 0))
    ws_spec = pl.BlockSpec((groups, CHUNK, CHUNK), lambda i: (0, 0, 0))
    d_in = dproj.shape[1]
    return pl.pallas_call(
        body, name="gmlp_bwd", grid=(n,),
        in_specs=[ANY, col(0), col(0), col(1), col(2), col(0), col(0), stat, stat, stat, ws_spec, vec, vec],
        out_specs=(pl.BlockSpec((tm, 3 * wa), lambda i: (i, 0)), ws_spec,
                   pl.BlockSpec((CHUNK, wa), lambda i: (0, 0)), vec, vec, vec),
        out_shape=(jax.ShapeDtypeStruct((s, d_in), BF16),
                   jax.ShapeDtypeStruct((groups, CHUNK, CHUNK), F32), jax.ShapeDtypeStruct((CHUNK, wa), F32))
        + (jax.ShapeDtypeStruct((1, wa), F32),) * 3,
        scratch_shapes=[pltpu.VMEM((tm, wa), F32), pltpu.VMEM((tm, wa), F32)] + [pltpu.VMEM((8, wa), F32)] * 3,
        input_output_aliases={0: 0},
        compiler_params=_params("arbitrary"),
    )(dproj, dy, proj, proj, proj, z, vn, mu, rs, ra, w_s, ln_g, g_out)


def _gate_fwd(y_in, yb, proj, g_out, d_model):
    s = proj.shape[0]
    wa = d_model // 2
    tm = _tile(s, 256)

    def body(y_any, yb_ref, bz_ref, go_ref, y_ref, rb_ref):
        del y_any
        ybv = yb_ref[...]
        rb = lax.rsqrt(jnp.mean(ybv * ybv, axis=-1, keepdims=True) + EPS)
        rb_ref[...] = rb
        sz, _ = _silu_and_grad(bz_ref[...])
        y_ref[...] = (ybv * rb * go_ref[...] * sz).astype(BF16)

    return pl.pallas_call(
        body, name="gate_b_fwd", grid=(s // tm,),
        in_specs=[ANY, pl.BlockSpec((tm, wa), lambda i: (i, 0)), pl.BlockSpec((tm, wa), lambda i: (i, 6)),
                  pl.BlockSpec((1, wa), lambda i: (0, 0))],
        out_specs=(pl.BlockSpec((tm, wa), lambda i: (i, 1)), pl.BlockSpec((tm, 1), lambda i: (i, 0))),
        out_shape=(jax.ShapeDtypeStruct((s, d_model), BF16), jax.ShapeDtypeStruct((s, 1), F32)),
        input_output_aliases={0: 0},
        compiler_params=_params("parallel"),
    )(y_in, yb, proj, g_out)


def _gate_bwd(dy, yb, proj, rb, g_out, d_model):
    s, d_in = proj.shape
    wa = d_model // 2
    tm = _tile(s, 256)
    n = s // tm

    def body(dy_ref, yb_ref, bz_ref, rb_ref, go_ref, dp_ref, do_ref, ggo_ref, acc):
        i = pl.program_id(0)

        @pl.when(i == 0)
        def _():
            acc[...] = jnp.zeros_like(acc)

        dyv, ybv, rbv = dy_ref[...], yb_ref[...], rb_ref[...]
        sz, dsz = _silu_and_grad(bz_ref[...])
        dp_ref[...] = (dyv * (ybv * rbv * go_ref[...]) * dsz).astype(BF16)
        dn = dyv * sz
        acc[...] += _colsum8(dn * ybv * rbv)
        dyg = dn * go_ref[...]
        do_ref[...] = rbv * (dyg - ybv * (rbv * rbv) * jnp.mean(dyg * ybv, axis=-1, keepdims=True))

        @pl.when(i == n - 1)
        def _():
            ggo_ref[...] = jnp.sum(acc[...], axis=0, keepdims=True)

    vec = pl.BlockSpec((1, wa), lambda i: (0, 0))
    return pl.pallas_call(
        body, name="gate_b_bwd", grid=(n,),
        in_specs=[pl.BlockSpec((tm, wa), lambda i: (i, 1)), pl.BlockSpec((tm, wa), lambda i: (i, 0)),
                  pl.BlockSpec((tm, wa), lambda i: (i, 6)), pl.BlockSpec((tm, 1), lambda i: (i, 0)), vec],
        out_specs=(pl.BlockSpec((tm, wa), lambda i: (i, 6)), pl.BlockSpec((tm, wa), lambda i: (i, 0)), vec),
        out_shape=(jax.ShapeDtypeStruct((s, d_in), BF16), jax.ShapeDtypeStruct((s, wa), F32),
                   jax.ShapeDtypeStruct((1, wa), F32)),
        scratch_shapes=[pltpu.VMEM((8, wa), F32)],
        compiler_params=_params("arbitrary"),
    )(dy, yb, proj, rb, g_out)


def _bucket_tables():
    qi = BLK + np.arange(BLK)
    kj = np.arange(2 * BLK)
    delta = qi[:, None] - kj[None, :]
    max_exact = NUM_BUCKETS // 2
    tabs = []
    for window, dil in DILATED:
        band = (delta >= 0) & (delta <= window // dil)
        dist = np.clip(delta, 0, None) * dil
        d = np.maximum(dist, 1).astype(np.float32)
        large = max_exact + (np.log(d / np.float32(max_exact)) / np.float32(math.log(MAX_DISTANCE / max_exact))
                             * np.float32(NUM_BUCKETS - max_exact)).astype(np.int32)
        large = np.minimum(large, NUM_BUCKETS - 1)
        tabs.append(np.where(band, np.where(dist < max_exact, dist, large), -1).astype(np.int32))
    return np.stack(tabs)


def _bias_tiles(tab_ref, rb_ref, h, bias_s):
    col = lax.broadcasted_iota(jnp.int32, (BLK, 2 * BLK), 1)
    for i in range(len(DILATED)):
        t = tab_ref[i]
        bias = jnp.zeros(t.shape, F32)
        for b in range(NUM_BUCKETS):
            bias = jnp.where(t == b, rb_ref[b, h], bias)
        bias = jnp.where(t >= 0, bias, NEG_INF)
        bias_s[2 * i] = jnp.where(col >= BLK, bias, NEG_INF)
        bias_s[2 * i + 1] = bias


def _block_rows(b, n_blocks, dil):
    nb = n_blocks // dil
    r = b // nb
    n = b % nb
    start = r + dil * BLK * n
    if dil == 1:
        return pl.ds(pl.multiple_of(start, BLK), BLK), n
    return pl.ds(start, BLK, stride=dil), n


def _sub_block(b, rows=BLK, pad=0):
    return pl.ds(pl.multiple_of(b * BLK + pad, BLK), rows)


def _rms_rows(x, gain):
    return x * lax.rsqrt(jnp.mean(x * x, axis=-1, keepdims=True) + EPS) * gain


def _attn_fwd(proj, g_q, g_k, rel_bias, d_model, comm=None):
    s = proj.shape[0]
    heads = d_model // 2 // HEAD_DIM
    n_blocks = s // BLK
    scale = HEAD_DIM ** -0.5
    tables = jnp.asarray(_bucket_tables())
    c_ins = list(comm.ins) if comm else []
    c_outs = list(comm.out_shape) if comm else []
    c_scratch = list(comm.scratch) if comm else []

    def body(q_ref, k_ref, v_ref, gq_ref, gk_ref, rb_ref, tab_ref, *rest):
        cin_refs, rest = rest[:len(c_ins)], rest[len(c_ins):]
        yb_ref, lse_ref = rest[:2]
        cout_refs, rest = rest[2:2 + len(c_outs)], rest[2 + len(c_outs):]
        qs, ks, vs, m_s, l_s, acc_s, bias_s = rest[:7]
        sems = rest[7:]
        h = pl.program_id(0)
        if comm:
            @pl.when(h == 0)
            def _():
                comm.start(cin_refs, cout_refs, sems)

        _bias_tiles(tab_ref, rb_ref, h, bias_s)
        gq = gq_ref[...] * scale
        gk = gk_ref[...]
        ks[0:BLK, :] = jnp.zeros((BLK, HEAD_DIM), BF16)
        vs[0:BLK, :] = jnp.zeros((BLK, HEAD_DIM), BF16)

        for i, (_, dil) in enumerate(DILATED):
            def prepare(b, carry, dil=dil):
                rows, _ = _block_rows(b, n_blocks, dil)
                qs[_sub_block(b), :] = _rms_rows(q_ref[rows, :], gq).astype(BF16)
                ks[_sub_block(b, pad=BLK), :] = _rms_rows(k_ref[rows, :], gk).astype(BF16)
                vs[_sub_block(b, pad=BLK), :] = v_ref[rows, :].astype(BF16)
                return carry
            lax.fori_loop(0, n_blocks, prepare, 0, unroll=2)

            def block(b, carry, i=i, dil=dil):
                rows, n = _block_rows(b, n_blocks, dil)
                qb = qs[_sub_block(b), :]
                kc = ks[_sub_block(b, 2 * BLK), :]
                vc = vs[_sub_block(b, 2 * BLK), :]
                sc = _dot(qb, kc, tb=True) + bias_s[2 * i + jnp.minimum(n, 1)]
                m_i = jnp.max(sc, axis=-1, keepdims=True)
                pr = jnp.exp(sc - m_i)
                l_i = jnp.sum(pr, axis=-1, keepdims=True)
                a_i = _dot(pr.astype(BF16), vc)
                if i == 0:
                    m_s[rows, :] = m_i
                    l_s[rows, :] = l_i
                    acc_s[rows, :] = a_i
                else:
                    m_o = m_s[rows, :]
                    m_n = jnp.maximum(m_o, m_i)
                    c_o = jnp.exp(m_o - m_n)
                    c_i = jnp.exp(m_i - m_n)
                    m_s[rows, :] = m_n
                    l_s[rows, :] = l_s[rows, :] * c_o + l_i * c_i
                    acc_s[rows, :] = acc_s[rows, :] * c_o + a_i * c_i
                return carry
            lax.fori_loop(0, n_blocks, block, 0, unroll=2)

        l = l_s[...]
        yb_ref[...] = acc_s[...] / l
        lse_ref[...] = m_s[...] + jnp.log(l)
        if comm:
            @pl.when(h == heads - 1)
            def _():
                comm.finish(cin_refs, cout_refs, sems)

    def head_col(off):
        return pl.BlockSpec((s, HEAD_DIM), lambda h: (0, off * heads + h))
    vec = pl.BlockSpec((1, HEAD_DIM), lambda h: (0, 0))
    res = pl.pallas_call(
        body, name="attn_fwd", grid=(heads,),
        in_specs=[head_col(3), head_col(4), head_col(5), vec, vec,
                  pl.BlockSpec(memory_space=pltpu.SMEM),
                  pl.BlockSpec((len(DILATED), BLK, 2 * BLK), lambda h: (0, 0, 0))] + [ANY] * len(c_ins),
        out_specs=[pl.BlockSpec((s, HEAD_DIM), lambda h: (0, h)), pl.BlockSpec((None, s, 1), lambda h: (h, 0, 0))]
        + [ANY] * len(c_outs),
        out_shape=[jax.ShapeDtypeStruct((s, heads * HEAD_DIM), F32), jax.ShapeDtypeStruct((heads, s, 1), F32)] + c_outs,
        scratch_shapes=[pltpu.VMEM((s, HEAD_DIM), BF16), pltpu.VMEM((s + BLK, HEAD_DIM), BF16),
                        pltpu.VMEM((s + BLK, HEAD_DIM), BF16),
                        pltpu.VMEM((s, 1), F32), pltpu.VMEM((s, 1), F32), pltpu.VMEM((s, HEAD_DIM), F32),
                        pltpu.VMEM((2 * len(DILATED), BLK, 2 * BLK), F32)] + c_scratch,
        input_output_aliases={7 + i: 2 + i for i in range(len(c_ins))} if (comm and comm.in_place) else {},
        compiler_params=_params("arbitrary"),
    )(proj, proj, proj, g_q, g_k, rel_bias, tables, *c_ins)
    return res[0], res[1], res[2:]


def _attn_bwd(dproj, d_o, yb, lse, proj, g_q, g_k, rel_bias, d_model):
    s, d_in = proj.shape
    heads = d_model // 2 // HEAD_DIM
    n_blocks = s // BLK
    scale = HEAD_DIM ** -0.5
    tables = jnp.asarray(_bucket_tables())
    n_dil = len(DILATED)

    def body(dp_any, q_ref, k_ref, v_ref, do_ref, o_ref, lse_ref, gq_ref, gk_ref, rb_ref, tab_ref,
             dp_out, ggq_ref, ggk_ref, db_ref,
             qs, ks, vs, dos, lse_s, del_s, dqn, dkn, dvv, dq_u, dk_u, dv_u, bias_s, dbias_s, obuf, sems):
        del dp_any
        h = pl.program_id(0)

        @pl.when(h == 0)
        def _():
            ggq_ref[...] = jnp.zeros_like(ggq_ref)
            ggk_ref[...] = jnp.zeros_like(ggk_ref)

        dbias_s[...] = jnp.zeros_like(dbias_s)
        _bias_tiles(tab_ref, rb_ref, h, bias_s)
        gq = gq_ref[...] * scale
        gk = gk_ref[...]
        ks[0:BLK, :] = jnp.zeros((BLK, HEAD_DIM), BF16)
        vs[0:BLK, :] = jnp.zeros((BLK, HEAD_DIM), BF16)

        for i, (_, dil) in enumerate(DILATED):
            def prepare(b, carry, dil=dil):
                rows, _ = _block_rows(b, n_blocks, dil)
                qs[_sub_block(b), :] = _rms_rows(q_ref[rows, :], gq).astype(BF16)
                ks[_sub_block(b, pad=BLK), :] = _rms_rows(k_ref[rows, :], gk).astype(BF16)
                vs[_sub_block(b, pad=BLK), :] = v_ref[rows, :].astype(BF16)
                do = do_ref[rows, :]
                dos[_sub_block(b), :] = do.astype(BF16)
                del_s[_sub_block(b), :] = jnp.sum(do * o_ref[rows, :], axis=-1, keepdims=True)
                lse_s[_sub_block(b), :] = lse_ref[rows, :]
                return carry
            lax.fori_loop(0, n_blocks, prepare, 0, unroll=2)
            dk_u[...] = jnp.zeros_like(dk_u)
            dv_u[...] = jnp.zeros_like(dv_u)

            def block(b, carry, i=i, dil=dil):
                _, n = _block_rows(b, n_blocks, dil)
                qb = qs[_sub_block(b), :]
                kc = ks[_sub_block(b, 2 * BLK), :]
                vc = vs[_sub_block(b, 2 * BLK), :]
                dob = dos[_sub_block(b), :]
                sc = _dot(qb, kc, tb=True) + bias_s[2 * i + jnp.minimum(n, 1)]
                pr = jnp.exp(sc - lse_s[_sub_block(b), :])
                dpr = _dot(dob, vc, tb=True)
                ds = pr * (dpr - del_s[_sub_block(b), :])
                dbias_s[i] += ds
                dsb = ds.astype(BF16)
                dq_u[_sub_block(b), :] = _dot(dsb, kc)
                dk_u[_sub_block(b, 2 * BLK), :] += _dot(dsb, qb, ta=True)
                dv_u[_sub_block(b, 2 * BLK), :] += _dot(pr.astype(BF16), dob, ta=True)
                return carry
            lax.fori_loop(0, n_blocks, block, 0, unroll=2)

            def scatter(b, carry, i=i, dil=dil):
                rows, _ = _block_rows(b, n_blocks, dil)
                for acc, part, pad in ((dqn, dq_u, 0), (dkn, dk_u, BLK), (dvv, dv_u, BLK)):
                    val = part[_sub_block(b, pad=pad), :]
                    acc[rows, :] = val if i == 0 else acc[rows, :] + val
                return carry
            lax.fori_loop(0, n_blocks, scatter, 0, unroll=2)

        q = q_ref[...]
        rq = lax.rsqrt(jnp.mean(q * q, axis=-1, keepdims=True) + EPS)
        k = k_ref[...]
        rk = lax.rsqrt(jnp.mean(k * k, axis=-1, keepdims=True) + EPS)
        dq_n = dqn[...]
        ggq_ref[...] += jnp.sum(dq_n * (q * rq) * scale, axis=0, keepdims=True)
        dg = dq_n * gq_ref[...] * scale
        obuf[0] = (rq * (dg - q * (rq * rq) * jnp.mean(dg * q, axis=-1, keepdims=True))).astype(BF16)
        dk_n = dkn[...]
        ggk_ref[...] += jnp.sum(dk_n * (k * rk), axis=0, keepdims=True)
        dg = dk_n * gk_ref[...]
        obuf[1] = (rk * (dg - k * (rk * rk) * jnp.mean(dg * k, axis=-1, keepdims=True))).astype(BF16)
        obuf[2] = dvv[...].astype(BF16)
        copies = [pltpu.make_async_copy(obuf.at[j], dp_out.at[:, pl.ds(pl.multiple_of(((3 + j) * heads + h) * HEAD_DIM, HEAD_DIM), HEAD_DIM)],
                                        sems.at[j]) for j in range(3)]
        for cp in copies:
            cp.start()
        for b in range(NUM_BUCKETS):
            tot = jnp.zeros((BLK, 2 * BLK), F32)
            for i in range(n_dil):
                tot = tot + jnp.where(tab_ref[i] == b, dbias_s[i], 0.0)
            db_ref[b:b + 1, :] = jnp.full((1, LANE), jnp.sum(tot), F32)
        for cp in copies:
            cp.wait()

    def head_col(off):
        return pl.BlockSpec((s, HEAD_DIM), lambda h: (0, off * heads + h))
    own_col = pl.BlockSpec((s, HEAD_DIM), lambda h: (0, h))
    vec = pl.BlockSpec((1, HEAD_DIM), lambda h: (0, 0))
    big = pltpu.VMEM((s, HEAD_DIM), F32)
    padded32 = pltpu.VMEM((s + BLK, HEAD_DIM), F32)
    padded16 = pltpu.VMEM((s + BLK, HEAD_DIM), BF16)
    return pl.pallas_call(
        body, name="attn_bwd", grid=(heads,),
        in_specs=[ANY, head_col(3), head_col(4), head_col(5), own_col, own_col,
                  pl.BlockSpec((None, s, 1), lambda h: (h, 0, 0)), vec, vec,
                  pl.BlockSpec(memory_space=pltpu.SMEM),
                  pl.BlockSpec((n_dil, BLK, 2 * BLK), lambda h: (0, 0, 0))],
        out_specs=(ANY, vec, vec, pl.BlockSpec((None, NUM_BUCKETS, LANE), lambda h: (h, 0, 0))),
        out_shape=(jax.ShapeDtypeStruct((s, d_in), BF16), jax.ShapeDtypeStruct((1, HEAD_DIM), F32),
                   jax.ShapeDtypeStruct((1, HEAD_DIM), F32), jax.ShapeDtypeStruct((heads, NUM_BUCKETS, LANE), F32)),
        scratch_shapes=[pltpu.VMEM((s, HEAD_DIM), BF16), padded16, padded16, pltpu.VMEM((s, HEAD_DIM), BF16),
                        pltpu.VMEM((s, 1), F32), pltpu.VMEM((s, 1), F32),
                        big, big, big, big, padded32, padded32,
                        pltpu.VMEM((2 * n_dil, BLK, 2 * BLK), F32), pltpu.VMEM((n_dil, BLK, 2 * BLK), F32),
                        pltpu.VMEM((3, s, HEAD_DIM), BF16), pltpu.SemaphoreType.DMA((3,))],
        input_output_aliases={0: 0},
        compiler_params=_params("arbitrary"),
    )(dproj, proj, proj, proj, d_o, yb, lse, g_q, g_k, rel_bias, tables)


def _local_step(xs, ps, tgt, sp, win4, wout4, wgate4, wup4, dist=None):
    s, d = xs.shape
    wa = d // 2
    heads = wa // HEAD_DIM
    d_in = 7 * wa
    c4 = d_in // N_CHIPS
    dple = ps.shape[1]

    tm, tn = _tile(s, 512), _tile(d, 512)
    tn_in = _tile(c4, 512)
    per_in = c4 // tn_in
    tn_up = _tile(d // N_CHIPS, 512)
    per_up = (d // N_CHIPS) // tn_up
    gm, gn = s // tm, d // tn

    hn, rx = _rms_fwd("rms_pre", xs, sp["g_pre"])
    if dist:
        proj, win4 = _mm_in_gather(dist["chip"], hn, win4)
    else:
        proj = _matmul(
            "mm_in", hn, win4, pl.BlockSpec((tm, d), lambda i, j, k: (i, 0)), _shard_spec(d, tn_in, per_in, 2, 1),
            (gm, d_in // tn_in, 1), None, jax.ShapeDtypeStruct((s, d_in), F32),
            pl.BlockSpec((tm, tn_in), lambda i, j, k: (i, j)))
    b_st = sp["b_s"].T
    y, z, vn, mu_v, rs_v, ra = _gmlp_fwd(proj, sp["w_s"], b_st, sp["ln_v_g"], sp["ln_v_b"], sp["g_out_a"], d)
    yb, lse, gathered = _attn_fwd(proj, sp["g_q"], sp["g_k"], sp["rel_bias"], d,
                                  comm=_Gather([wout4, wgate4, wup4]) if dist else None)
    if dist:
        wout4, wgate4, wup4 = gathered
    wout, wgate = wout4.reshape(d, d), wgate4.reshape(d, d)
    y, rb = _gate_fwd(y, yb, proj, sp["g_out_b"], d)

    def residual(acc, extra, outs):
        outs[0][...] = extra[0][...] + acc

    tile = pl.BlockSpec((tm, tn), lambda i, j, k: (i, j))
    h = _matmul(
        "mm_out", y, wout, pl.BlockSpec((tm, d), lambda i, j, k: (i, 0)), pl.BlockSpec((d, tn), lambda i, j, k: (0, j)),
        (gm, gn, 1), None, jax.ShapeDtypeStruct((s, d), F32), tile, extras=(xs,), extra_specs=(tile,), epilogue=residual)
    hp, rh = _rms_fwd("rms_ple", h, sp["g_ple"])
    p16 = ps.astype(BF16)

    def head(acc, extra, outs):
        p_ref, wup_ref, h_ref, t_ref = extra
        dout_ref, dgl_ref, dup_ref, loss_ref = outs
        up = _dot(p_ref[...], wup_ref[...])
        gate = jax.nn.sigmoid(acc)
        err = h_ref[...] + gate * up - t_ref[...]
        sq = _colsum8(err * err)
        part = sq[:, 0:LANE]
        for c in range(1, sq.shape[1] // LANE):
            part = part + sq[:, c * LANE:(c + 1) * LANE]
        loss_ref[...] = part
        dout = err * (1.0 / d)
        dout_ref[...] = dout
        dup_ref[...] = (dout * gate).astype(BF16)
        dgl_ref[...] = (dout * up * gate * (1.0 - gate)).astype(BF16)

    tile_up = pl.BlockSpec((tm, tn_up), lambda i, j, k: (i, j))
    dout, dglin, dup, loss_parts = _matmul(
        "mm_gate_loss", hp, wgate, pl.BlockSpec((tm, d), lambda i, j, k: (i, 0)), pl.BlockSpec((d, tn_up), lambda i, j, k: (0, j)),
        (gm, d // tn_up, 1), None,
        (jax.ShapeDtypeStruct((s, d), F32), jax.ShapeDtypeStruct((s, d), BF16), jax.ShapeDtypeStruct((s, d), BF16),
         jax.ShapeDtypeStruct((gm * 8, (d // tn_up) * LANE), F32)),
        (tile_up, tile_up, tile_up, pl.BlockSpec((8, LANE), lambda i, j, k: (i, j))),
        extras=(p16, wup4, h, tgt),
        extra_specs=(pl.BlockSpec((tm, dple), lambda i, j, k: (i, 0)), _shard_spec(dple, tn_up, per_up, 2, 1),
                     tile_up, tile_up),
        epilogue=head)
    loss = 0.5 * jnp.sum(loss_parts) * (1.0 / d)

    dhp = _matmul(
        "mm_dhp", dglin, wgate, pl.BlockSpec((tm, d), lambda i, j, k: (i, 0)), pl.BlockSpec((tn, d), lambda i, j, k: (j, 0)),
        (gm, gn, 1), None, jax.ShapeDtypeStruct((s, d), F32), tile, tb=True)
    dh, dh16, g_g_ple = _rms_bwd("rms_ple_bwd", dhp, h, rh, sp["g_ple"], dout, True)
    tmw = _tile(d, 512)
    g_wgate = _matmul(
        "mm_gwgate", hp, dglin, pl.BlockSpec((s, tmw), lambda i, j, k: (0, i)), pl.BlockSpec((s, tn), lambda i, j, k: (0, j)),
        (d // tmw, gn, 1), None, jax.ShapeDtypeStruct((d, d), F32), pl.BlockSpec((tmw, tn), lambda i, j, k: (i, j)), ta=True)
    g_wup = _matmul(
        "mm_gwup", p16, dup, pl.BlockSpec((s, dple), lambda i, j, k: (0, 0)), pl.BlockSpec((s, tn_up), lambda i, j, k: (0, j)),
        (1, d // tn_up, 1), None, jax.ShapeDtypeStruct((N_CHIPS, dple, d // N_CHIPS), F32),
        _shard_spec(dple, tn_up, per_up, 0, 1), ta=True)
    dy = _matmul(
        "mm_dy", dh16, wout, pl.BlockSpec((tm, d), lambda i, j, k: (i, 0)), pl.BlockSpec((tn, d), lambda i, j, k: (j, 0)),
        (gm, gn, 1), None, jax.ShapeDtypeStruct((s, d), F32), tile, tb=True)
    g_wout = _matmul(
        "mm_gwout", y, dh16, pl.BlockSpec((s, tmw), lambda i, j, k: (0, i)), pl.BlockSpec((s, tn), lambda i, j, k: (0, j)),
        (d // tmw, gn, 1), None, jax.ShapeDtypeStruct((d, d), F32), pl.BlockSpec((tmw, tn), lambda i, j, k: (i, j)), ta=True)

    dproj, d_o, g_g_out_b = _gate_bwd(dy, yb, proj, rb, sp["g_out_b"], d)
    dproj, g_g_q, g_g_k, dbias = _attn_bwd(dproj, d_o, yb, lse, proj, sp["g_q"], sp["g_k"], sp["rel_bias"], d)
    dproj, g_w_s, dz_sum, g_ln_g, g_ln_b, g_g_out_a = _gmlp_bwd(
        dproj, dy, proj, z, vn, mu_v, rs_v, ra, sp["w_s"], sp["ln_v_g"], sp["g_out_a"], d)

    g_wout4 = g_wout.reshape(N_CHIPS, d // N_CHIPS, d)
    g_wgate4 = g_wgate.reshape(N_CHIPS, d // N_CHIPS, d)
    early = (g_wout4, g_wgate4, g_wup)
    if dist:
        early_sums = [_chip_sum("chip_sum_" + n, dist["core"], g, o)
                      for n, g, o in zip(LARGE[1:], early, _swap_halves("swap_early", list(early)))]
    g_win = _matmul(
        "mm_gwin", hn, dproj, pl.BlockSpec((s, tmw), lambda i, j, k: (0, i)), pl.BlockSpec((s, tn_in), lambda i, j, k: (0, j)),
        (d // tmw, d_in // tn_in, 1), None, jax.ShapeDtypeStruct((N_CHIPS, d, c4), F32),
        _shard_spec(tmw, tn_in, per_in, 0, 1), ta=True,
        comm=_ExchangeChips([s16 for _, s16 in early_sums]) if dist else None)
    if dist:
        g_win, early_got = g_win
        win_sum = _chip_sum("chip_sum_w_in", dist["core"], g_win, _swap_halves("swap_w_in", [g_win])[0])
    tk = c4
    tmh, tnh = _tile(s, 1024), _tile(d, 1024)
    dhn = _matmul(
        "mm_dhn", dproj, win4, pl.BlockSpec((tmh, tk), lambda i, j, k: (i, k)), _shard_spec(tnh, tk, 1, 1, 2),
        (s // tmh, d // tnh, d_in // tk), (tmh, tnh), jax.ShapeDtypeStruct((s, d), F32),
        pl.BlockSpec((tmh, tnh), lambda i, j, k: (i, j)), tb=True,
        comm=_ExchangeChips([win_sum[1]]) if dist else None)
    if dist:
        dhn, win_got = dhn
    grad_x, g_g_pre = _rms_bwd("rms_pre_bwd", dhn, xs, rx, sp["g_pre"], dh, False)

    small = {
        "g_pre": g_g_pre,
        "w_s": g_w_s,
        "b_s": jnp.sum(dz_sum.reshape(CHUNK, heads, HEAD_DIM), axis=-1).T,
        "ln_v_g": g_ln_g, "ln_v_b": g_ln_b,
        "g_q": g_g_q, "g_k": g_g_k,
        "rel_bias": dbias[:, :, 0].T,
        "g_out_a": g_g_out_a, "g_out_b": g_g_out_b,
        "g_ple": g_g_ple,
    }
    if not dist:
        return loss, grad_x, (g_win, *early), small
    sums32 = [win_sum[0]] + [s32 for s32, _ in early_sums]
    blocks = [_total("total_" + n, dist["chip"], dist["core"], s32, o)
              for n, s32, o in zip(LARGE, sums32, list(win_got) + list(early_got))]
    return loss, grad_x, _join_halves(blocks), small


def _place():
    x, y, c = lax.axis_index("x"), lax.axis_index("y"), lax.axis_index("c")
    chips = [(1 - x, y), (x, 1 - y), (1 - x, 1 - y)]
    return x, y, c, chips


def _remote(src, dst, send_sems, recv_sems, k, to):
    return pltpu.make_async_remote_copy(src_ref=src, dst_ref=dst, send_sem=send_sems.at[k], recv_sem=recv_sems.at[k],
                                        device_id=to, device_id_type=MESH)


def _cast_bf16(name, chip, w):
    r, c = w.shape
    tm = _tile(r, 256)

    def body(chip_ref, w_ref, o_ref):
        del chip_ref
        o_ref[...] = w_ref[...].astype(BF16)

    return pl.pallas_call(
        body, name=name,
        grid_spec=pltpu.PrefetchScalarGridSpec(
            num_scalar_prefetch=1, grid=(r // tm,),
            in_specs=[pl.BlockSpec((tm, c), lambda i, chip_ref: (i, 0))],
            out_specs=pl.BlockSpec((None, tm, c), lambda i, chip_ref: (chip_ref[0], i, 0))),
        out_shape=jax.ShapeDtypeStruct((N_CHIPS, r, c), BF16), compiler_params=_params("parallel"),
    )(chip, w)


class _Gather:
    in_place = True

    def __init__(self, fulls):
        self.ins = list(fulls)
        self.out_shape = [jax.ShapeDtypeStruct(f.shape, f.dtype) for f in fulls]
        n = len(fulls)
        self.scratch = [pltpu.SemaphoreType.DMA((6 * n,)), pltpu.SemaphoreType.DMA((6 * n,))]

    @staticmethod
    def _sends(outs, sems):
        send_sems, recv_sems = sems
        x, y, c, chips = _place()
        cps = []
        for w, ref in enumerate(outs):
            half = ref.shape[1] // 2
            blk = ref.at[2 * x + y, pl.ds(c * half, half)]
            cps += [_remote(blk, blk, send_sems, recv_sems, 6 * w + q, (*chip, c)) for q, chip in enumerate(chips)]
        return cps

    def start(self, ins, outs, sems):
        for cp in self._sends(outs, sems):
            cp.start()

    def finish(self, ins, outs, sems):
        send_sems, recv_sems = sems
        x, y, c, chips = _place()
        sibling = (x, y, 1 - c)
        forwards = []
        for w, ref in enumerate(outs):
            half = ref.shape[1] // 2
            for q, chip in enumerate(chips):
                blk = ref.at[2 * chip[0] + chip[1], pl.ds(c * half, half)]
                _remote(blk, blk, send_sems, recv_sems, 6 * w + q, sibling).wait_recv()
                fwd = _remote(blk, blk, send_sems, recv_sems, 6 * w + 3 + q, sibling)
                fwd.start()
                forwards.append(fwd)
        for w, ref in enumerate(outs):
            half = ref.shape[1] // 2
            for q, chip in enumerate(chips):
                blk = ref.at[2 * chip[0] + chip[1], pl.ds((1 - c) * half, half)]
                _remote(blk, blk, send_sems, recv_sems, 6 * w + 3 + q, sibling).wait_recv()
        for cp in self._sends(outs, sems) + forwards:
            cp.wait_send()


class _ExchangeChips:
    in_place = False

    def __init__(self, sums16):
        self.ins = list(sums16)
        self.out_shape = [jax.ShapeDtypeStruct(g.shape, g.dtype) for g in sums16]
        n = len(sums16)
        self.scratch = [pltpu.SemaphoreType.DMA((3 * n,)), pltpu.SemaphoreType.DMA((3 * n,))]

    @staticmethod
    def _sends(ins, outs, sems):
        send_sems, recv_sems = sems
        x, y, c, chips = _place()
        return [_remote(ins[w].at[2 * chip[0] + chip[1]], outs[w].at[2 * x + y], send_sems, recv_sems, 3 * w + q, (*chip, c))
                for w in range(len(ins)) for q, chip in enumerate(chips)]

    def start(self, ins, outs, sems):
        for cp in self._sends(ins, outs, sems):
            cp.start()

    def finish(self, ins, outs, sems):
        send_sems, recv_sems = sems
        x, y, c, chips = _place()
        for w in range(len(ins)):
            for q, chip in enumerate(chips):
                blk = outs[w].at[2 * chip[0] + chip[1]]
                _remote(blk, blk, send_sems, recv_sems, 3 * w + q, (*chip, c)).wait_recv()
        for cp in self._sends(ins, outs, sems):
            cp.wait_send()


def _mm_in_gather(chip, hn, win4):
    s, d = hn.shape
    c4 = win4.shape[2]
    tm, tn = _tile(s, 512), _tile(c4, 512)
    per = c4 // tn
    nj, gm = N_CHIPS * per, s // tm
    half = d // 2

    def block_of(j, chip_ref):
        o = j // per
        return chip_ref[0] ^ (((o & 1) << 1) | (o >> 1))

    def body(chip_ref, hn_ref, w_any, proj_ref, w_ref, wbuf, wsem, send_sems, recv_sems):
        del w_any
        j, i = pl.program_id(0), pl.program_id(1)
        x, y, c, chips = _place()
        sibling = (x, y, 1 - c)

        def send_mine(q):
            blk = w_ref.at[2 * x + y, pl.ds(c * half, half)]
            return _remote(blk, blk, send_sems, recv_sems, q, (*chips[q], c))

        def forward(q):
            chip = chips[q]
            blk = w_ref.at[2 * chip[0] + chip[1], pl.ds(c * half, half)]
            return _remote(blk, blk, send_sems, recv_sems, 3 + q, sibling)

        def arrive(q):
            chip = chips[q]
            blk = w_ref.at[2 * chip[0] + chip[1], pl.ds(c * half, half)]
            _remote(blk, blk, send_sems, recv_sems, q, sibling).wait_recv()
            forward(q).start()
            oth = w_ref.at[2 * chip[0] + chip[1], pl.ds((1 - c) * half, half)]
            _remote(oth, oth, send_sems, recv_sems, 3 + q, sibling).wait_recv()

        def tile(jj, slot):
            return pltpu.make_async_copy(w_ref.at[block_of(jj, chip_ref), :, pl.ds(pl.multiple_of((jj % per) * tn, tn), tn)],
                                         wbuf.at[slot], wsem.at[slot])

        @pl.when((j == 0) & (i == 0))
        def _():
            send_mine(0).start()
            send_mine(1).start()
            tile(0, 0).start()

        @pl.when(i == 0)
        def _():
            for o in range(1, N_CHIPS):
                @pl.when(j + 1 == o * per)
                def _():
                    arrive(o - 1)
                    if o == 1:
                        send_mine(2).start()

            @pl.when(j + 1 < nj)
            def _():
                tile(j + 1, (j + 1) % 2).start()

            tile(j, j % 2).wait()

        proj_ref[...] = _dot(hn_ref[...], wbuf[j % 2])

        @pl.when((j == nj - 1) & (i == gm - 1))
        def _():
            for q in range(3):
                send_mine(q).wait_send()
                forward(q).wait_send()

    return pl.pallas_call(
        body, name="mm_in_gather",
        grid_spec=pltpu.PrefetchScalarGridSpec(
            num_scalar_prefetch=1, grid=(nj, gm),
            in_specs=[pl.BlockSpec((tm, d), lambda j, i, chip_ref: (i, 0)), ANY],
            out_specs=[pl.BlockSpec((tm, tn), lambda j, i, chip_ref: (i, block_of(j, chip_ref) * per + j % per)), ANY],
            scratch_shapes=[pltpu.VMEM((2, d, tn), BF16), pltpu.SemaphoreType.DMA((2,)),
                            pltpu.SemaphoreType.DMA((6,)), pltpu.SemaphoreType.DMA((6,))]),
        out_shape=[jax.ShapeDtypeStruct((s, N_CHIPS * c4), F32), jax.ShapeDtypeStruct(win4.shape, win4.dtype)],
        input_output_aliases={2: 1},
        compiler_params=_params("arbitrary", "arbitrary"),
    )(chip, hn, win4)


def _run_comm(name, comm):
    n_in, n_out = len(comm.ins), len(comm.out_shape)

    def body(*refs):
        ins, outs, sems = refs[:n_in], refs[n_in:n_in + n_out], refs[n_in + n_out:]
        comm.start(ins, outs, sems)
        comm.finish(ins, outs, sems)

    return pl.pallas_call(
        body, name=name, in_specs=[ANY] * n_in, out_specs=[ANY] * n_out, out_shape=comm.out_shape,
        scratch_shapes=comm.scratch, input_output_aliases={i: i for i in range(n_in)} if comm.in_place else {},
    )(*comm.ins)


def _swap_halves(name, grads):
    nw = len(grads)

    def body(*refs):
        ins, outs = refs[:nw], refs[nw:2 * nw]
        send_sems, recv_sems = refs[2 * nw:]
        x, y, c, _ = _place()
        copies = []
        for w in range(nw):
            half = ins[w].shape[1] // 2
            cp = _remote(ins[w].at[:, pl.ds((1 - c) * half, half)], outs[w], send_sems, recv_sems, w, (x, y, 1 - c))
            cp.start()
            copies.append(cp)
        for cp in copies:
            cp.wait()

    return pl.pallas_call(
        body, name=name,
        in_specs=[ANY] * nw, out_specs=[ANY] * nw,
        out_shape=[jax.ShapeDtypeStruct((N_CHIPS, g.shape[1] // 2, g.shape[2]), g.dtype) for g in grads],
        scratch_shapes=[pltpu.SemaphoreType.DMA((nw,)), pltpu.SemaphoreType.DMA((nw,))],
    )(*grads)


def _chip_sum(name, core, grad, got):
    _, r, c = grad.shape
    half = r // 2
    th = _tile(half, 128)
    n = half // th

    def body(core_ref, g_ref, o_ref, s32_ref, s16_ref):
        del core_ref
        v = g_ref[...] + o_ref[...]
        s32_ref[...] = v
        s16_ref[...] = v.astype(BF16)

    blk = pl.BlockSpec((None, th, c), lambda k, i, core_ref: (k, i, 0))
    return pl.pallas_call(
        body, name=name,
        grid_spec=pltpu.PrefetchScalarGridSpec(
            num_scalar_prefetch=1, grid=(N_CHIPS, n),
            in_specs=[pl.BlockSpec((None, th, c), lambda k, i, core_ref: (k, core_ref[0] * n + i, 0)), blk],
            out_specs=[blk, blk]),
        out_shape=[jax.ShapeDtypeStruct((N_CHIPS, half, c), F32), jax.ShapeDtypeStruct((N_CHIPS, half, c), BF16)],
        compiler_params=_params("parallel", "parallel"),
    )(core, grad, got)


def _total(name, chip, core, sum32, got16):
    _, half, c = sum32.shape
    th = _tile(half, 128)
    n = half // th

    def body(chip_ref, core_ref, own_ref, a_ref, b_ref, c_ref, o_ref):
        del chip_ref, core_ref
        o_ref[...] = ((own_ref[...] + a_ref[...].astype(F32)) + b_ref[...].astype(F32)) + c_ref[...].astype(F32)

    def other(step):
        return pl.BlockSpec((None, th, c), lambda i, chip_ref, core_ref: ((chip_ref[0] + step) % N_CHIPS, i, 0))

    return pl.pallas_call(
        body, name=name,
        grid_spec=pltpu.PrefetchScalarGridSpec(
            num_scalar_prefetch=2, grid=(n,),
            in_specs=[other(0), other(1), other(2), other(3)],
            out_specs=pl.BlockSpec((th, c), lambda i, chip_ref, core_ref: (core_ref[0] * n + i, 0))),
        out_shape=jax.ShapeDtypeStruct((2 * half, c), F32),
        compiler_params=_params("parallel"),
    )(chip, core, sum32, got16, got16, got16)


def _join_halves(blocks):
    nw = len(blocks)

    def body(*refs):
        outs = refs[nw:2 * nw]
        send_sems, recv_sems = refs[2 * nw:]
        x, y, c, _ = _place()
        copies = []
        for w in range(nw):
            half = outs[w].shape[0] // 2
            rows = outs[w].at[pl.ds(c * half, half)]
            cp = _remote(rows, rows, send_sems, recv_sems, w, (x, y, 1 - c))
            cp.start()
            copies.append(cp)
        for w, cp in enumerate(copies):
            cp.wait_send()
            half = outs[w].shape[0] // 2
            rows = outs[w].at[pl.ds((1 - c) * half, half)]
            _remote(rows, rows, send_sems, recv_sems, w, (x, y, 1 - c)).wait_recv()

    return pl.pallas_call(
        body, name="join_halves",
        in_specs=[ANY] * nw, out_specs=[ANY] * nw,
        out_shape=[jax.ShapeDtypeStruct(t.shape, t.dtype) for t in blocks],
        scratch_shapes=[pltpu.SemaphoreType.DMA((nw,)), pltpu.SemaphoreType.DMA((nw,))],
        input_output_aliases={i: i for i in range(nw)},
    )(*blocks)


def _gather_small(pack):
    m_per, n = pack.shape

    def body(x_ref, out_ref, send_sems, recv_sems, local_sem):
        x, y, c, chips = _place()
        me, sibling = (x, y, c), (x, y, 1 - c)

        def rows(px, py, pc):
            return out_ref.at[pl.ds((4 * px + 2 * py + pc) * m_per, m_per), :]

        def copy(k, block, to, src=None):
            return _remote(rows(*block) if src is None else src, rows(*block), send_sems, recv_sems, k, to)

        mine = pltpu.make_async_copy(x_ref, rows(*me), local_sem)
        mine.start()
        first = [copy(0, me, sibling, src=x_ref)]
        first += [copy(1 + j, me, (*chip, c), src=x_ref) for j, chip in enumerate(chips)]
        for cp in first:
            cp.start()
        passed = [copy(4 + j, (*chip, c), sibling) for j, chip in enumerate(chips)]
        for j, chip in enumerate(chips):
            copy(1 + j, (*chip, c), me).wait_recv()
            passed[j].start()
        copy(0, sibling, me).wait_recv()
        for j, chip in enumerate(chips):
            copy(4 + j, (*chip, 1 - c), me).wait_recv()
        for cp in first + passed:
            cp.wait_send()
        mine.wait()

    return pl.pallas_call(
        body, name="gather_small",
        out_shape=jax.ShapeDtypeStruct((N_DEV * m_per, n), pack.dtype),
        in_specs=[pl.BlockSpec(memory_space=pltpu.VMEM)],
        out_specs=pl.BlockSpec(memory_space=pltpu.VMEM),
        scratch_shapes=[pltpu.SemaphoreType.DMA((7,)), pltpu.SemaphoreType.DMA((7,)), pltpu.SemaphoreType.DMA],
        compiler_params=_params(),
    )(pack)


def _adamw_math(w, g, m, v):
    m = ADAM_B1 * m + (1.0 - ADAM_B1) * g
    v = ADAM_B2 * v + (1.0 - ADAM_B2) * (g * g)
    m_hat = m / (1.0 - ADAM_B1 ** ADAM_STEP)
    v_hat = v / (1.0 - ADAM_B2 ** ADAM_STEP)
    delta = -ADAM_LR * (m_hat / (jnp.sqrt(v_hat) + ADAM_EPS) + ADAM_WD * w)
    return delta, m, v


def _adamw(name, w, g, m, v):
    r, c = w.shape
    tm = _tile(r, 128)

    def body(w_ref, g_ref, m_ref, v_ref, d_out, m_out, v_out):
        d_out[...], m_out[...], v_out[...] = _adamw_math(w_ref[...], g_ref[...], m_ref[...], v_ref[...])

    spec = pl.BlockSpec((tm, c), lambda i: (i, 0))
    return pl.pallas_call(
        body, name=name, grid=(r // tm,), in_specs=[spec] * 4, out_specs=[spec] * 3,
        out_shape=[jax.ShapeDtypeStruct((r, c), F32)] * 3, compiler_params=_params("parallel"),
    )(w, g, m, v)


def _adamw_small(gathered, w, m, v):
    rows = w.shape[0]

    def body(all_ref, w_ref, m_ref, v_ref, g_out, d_out, m_out, v_out):
        g = all_ref[0:rows, :]
        for dev in range(1, N_DEV):
            g = g + all_ref[dev * rows:(dev + 1) * rows, :]
        g_out[...] = g
        d_out[...], m_out[...], v_out[...] = _adamw_math(w_ref[...], g, m_ref[...], v_ref[...])

    return pl.pallas_call(
        body, name="adamw_small", out_shape=[jax.ShapeDtypeStruct(w.shape, F32)] * 4, compiler_params=_params(),
    )(gathered, w, m, v)


SMALL = ("g_pre", "w_s", "b_s", "ln_v_g", "ln_v_b", "g_q", "g_k", "rel_bias", "g_out_a", "g_out_b", "g_ple")
LARGE = ("w_in", "w_out", "w_ple_gate", "w_ple_up")
WEIGHTS = ("g_pre", "w_in", "w_s", "b_s", "ln_v_g", "ln_v_b", "g_q", "g_k", "rel_bias", "g_out_a", "g_out_b", "w_out",
           "g_ple", "w_ple_gate", "w_ple_up")


def _pack(parts):
    flat = jnp.concatenate([parts[n].reshape(-1).astype(F32) for n in SMALL])
    rows = -(-flat.shape[0] // (8 * LANE)) * 8
    return jnp.pad(flat, (0, rows * LANE - flat.shape[0])).reshape(rows, LANE)


def _unpack(pack, like):
    flat = pack.reshape(-1)
    out, at = {}, 0
    for n in SMALL:
        size = math.prod(like[n].shape)
        out[n] = flat[at:at + size].reshape(like[n].shape)
        at += size
    return out


def kernel(x, p, g_pre, w_in, w_s, b_s, ln_v_g, ln_v_b, g_q, g_k, rel_bias, g_out_a, g_out_b, w_out, g_ple, w_ple_gate, w_ple_up, loss_target, m_g_pre, m_w_in, m_w_s, m_b_s, m_ln_v_g, m_ln_v_b, m_g_q, m_g_k, m_rel_bias, m_g_out_a, m_g_out_b, m_w_out, m_g_ple, m_w_ple_gate, m_w_ple_up, v_g_pre, v_w_in, v_w_s, v_b_s, v_ln_v_g, v_ln_v_b, v_g_q, v_g_k, v_rel_bias, v_g_out_a, v_g_out_b, v_w_out, v_g_ple, v_w_ple_gate, v_w_ple_up):
    given = dict(locals())
    weights = {n: given[n] for n in WEIGHTS}
    mom_m = {n: given["m_" + n] for n in WEIGHTS}
    mom_v = {n: given["v_" + n] for n in WEIGHTS}
    xs, ps, tgt = x[0], p[0, 0], loss_target[0]
    d = xs.shape[1]

    core = lax.axis_index("c").astype(jnp.int32).reshape(1)
    chip = (2 * lax.axis_index("x") + lax.axis_index("y")).astype(jnp.int32).reshape(1)

    win4, wout4, wgate4, wup4 = [_cast_bf16("cast_" + n, chip, weights[n][0]) for n in LARGE]

    sp = {
        "g_pre": g_pre, "w_s": w_s[0], "b_s": b_s[0], "ln_v_g": ln_v_g, "ln_v_b": ln_v_b, "g_q": g_q, "g_k": g_k,
        "rel_bias": rel_bias, "g_out_a": g_out_a, "g_out_b": g_out_b, "g_ple": g_ple,
    }
    loss_local, grad_x, blocks, small = _local_step(xs, ps, tgt, sp, win4, wout4, wgate4, wup4,
                                                    dist={"chip": chip, "core": core})
    loss = lax.psum(loss_local, MESH_AXES)
    grads = dict(zip(LARGE, blocks))

    out_g, out_d, out_m, out_v = {}, {}, {}, {}
    for n in LARGE:
        out_g[n] = grads[n][None]
        dl, mn, vn = _adamw("adamw_" + n, weights[n][0], grads[n], mom_m[n][0], mom_v[n][0])
        out_d[n], out_m[n], out_v[n] = dl[None], mn[None], vn[None]

    gathered = _gather_small(_pack(small))
    pg, pd, pm, pv = _adamw_small(gathered, _pack(weights), _pack(mom_m), _pack(mom_v))
    for packed, out in ((pg, out_g), (pd, out_d), (pm, out_m), (pv, out_v)):
        out.update(_unpack(packed, weights))

    return (loss, grad_x[None], *[out_g[n] for n in WEIGHTS], *[out_d[n] for n in WEIGHTS],
            *[out_m[n] for n in WEIGHTS], *[out_v[n] for n in WEIGHTS])
```

```python
import functools
import math

import numpy as np

import jax
import jax.numpy as jnp
from jax import lax
from jax.experimental import pallas as pl
from jax.experimental.pallas import tpu as pltpu

F32 = jnp.float32
BF16 = jnp.bfloat16

HEAD_DIM = 128
CHUNK = 128
BLK = 128
DILATED = ((128, 1), (512, 4), (2048, 16))
NUM_BUCKETS = 32
MAX_DISTANCE = 2048
ATTN_GROUP = 4
EPS = 1e-6
NEG_INF = -1e30
N_CHIPS = 4
N_DEV = 8
MESH_AXES = ("x", "y", "c")

ADAM_LR = 0.001
ADAM_B1 = 0.9
ADAM_B2 = 0.999
ADAM_EPS = 1e-08
ADAM_WD = 0.01
ADAM_STEP = 10

V7X_VMEM_LIMIT = 56 * 1024 * 1024
LANE = 128
MESH = pl.DeviceIdType.MESH
ANY = pl.BlockSpec(memory_space=pl.ANY)


def _params(*sem):
    return pltpu.CompilerParams(dimension_semantics=sem or None, vmem_limit_bytes=V7X_VMEM_LIMIT)


def _tile(n, target):
    if n <= target:
        return n
    t = (target // LANE) * LANE
    while t > LANE and n % t:
        t -= LANE
    assert n % t == 0, (n, target)
    return t


def _gelu(x):
    return 0.5 * x * (1.0 + lax.erf(x * (1.0 / math.sqrt(2.0))))


def _gelu_grad(x):
    return 0.5 * (1.0 + lax.erf(x * (1.0 / math.sqrt(2.0)))) + x * jnp.exp(-0.5 * x * x) * (1.0 / math.sqrt(2.0 * math.pi))


def _silu_and_grad(x):
    s = jax.nn.sigmoid(x)
    return x * s, s * (1.0 + x * (1.0 - s))


def _dot(a, b, ta=False, tb=False):
    return lax.dot_general(a, b, (((0 if ta else 1,), (1 if tb else 0,)), ((), ())), preferred_element_type=F32)


def _colsum8(v):
    return jnp.sum(v.reshape(v.shape[0] // 8, 8, v.shape[1]), axis=0)


def _matmul(name, a, b, a_spec, b_spec, grid, acc_shape, out_shape, out_specs, ta=False, tb=False,
            extras=(), extra_specs=(), epilogue=None, comm=None):
    nk = grid[2]
    n_extra = len(extras)
    single = not isinstance(out_shape, (tuple, list))
    outs_shape = (out_shape,) if single else tuple(out_shape)
    outs_specs = (out_specs,) if single else tuple(out_specs)
    n_out = len(outs_shape)
    c_ins = list(comm.ins) if comm else []
    c_outs = list(comm.out_shape) if comm else []
    c_scratch = list(comm.scratch) if comm else []
    n_cin, n_cout = len(c_ins), len(c_outs)

    def finish(acc, extra_refs, out_refs):
        if epilogue is None:
            out_refs[0][...] = acc.astype(out_refs[0].dtype)
        else:
            epilogue(acc, extra_refs, out_refs)

    def body(a_ref, b_ref, *rest):
        extra_refs = rest[:n_extra]
        cin_refs = rest[n_extra:n_extra + n_cin]
        out_refs = rest[n_extra + n_cin:n_extra + n_cin + n_out]
        cout_refs = rest[n_extra + n_cin + n_out:n_extra + n_cin + n_out + n_cout]
        scratch_refs = rest[n_extra + n_cin + n_out + n_cout:]
        ids = [pl.program_id(ax) for ax in range(3)]
        if comm:
            sems = scratch_refs[len(scratch_refs) - len(c_scratch):]

            @pl.when((ids[0] == 0) & (ids[1] == 0) & (ids[2] == 0))
            def _():
                comm.start(cin_refs, cout_refs, sems)

        if nk == 1:
            finish(_dot(a_ref[...], b_ref[...], ta, tb), extra_refs, out_refs)
        else:
            acc_ref = scratch_refs[0]

            @pl.when(ids[2] == 0)
            def _():
                acc_ref[...] = jnp.zeros_like(acc_ref)

            acc_ref[...] += _dot(a_ref[...], b_ref[...], ta, tb)

            @pl.when(ids[2] == nk - 1)
            def _():
                finish(acc_ref[...], extra_refs, out_refs)

        if comm:
            @pl.when((ids[0] == grid[0] - 1) & (ids[1] == grid[1] - 1) & (ids[2] == nk - 1))
            def _():
                comm.finish(cin_refs, cout_refs, sems)

    scratch = ([] if nk == 1 else [pltpu.VMEM(acc_shape, F32)]) + c_scratch
    aliases = {2 + n_extra + i: n_out + i for i in range(n_cin)} if (comm and comm.in_place) else {}
    res = pl.pallas_call(
        body, name=name, grid=grid,
        in_specs=[a_spec, b_spec, *extra_specs] + [ANY] * n_cin,
        out_specs=list(outs_specs) + [ANY] * n_cout, out_shape=list(outs_shape) + c_outs, scratch_shapes=scratch,
        input_output_aliases=aliases,
        compiler_params=_params(*(("arbitrary",) * 3 if comm else ("parallel", "parallel", "arbitrary"))),
    )(a, b, *extras, *c_ins)
    if comm:
        main = res[:n_out]
        return (main[0] if single else main), res[n_out:]
    return res[0] if single else res


def _shard_spec(rows, cols, per_shard, row_axis, col_axis):
    def index(i, j, k):
        g = (i, j, k)
        return (g[col_axis] // per_shard, g[row_axis], g[col_axis] % per_shard)
    return pl.BlockSpec((None, rows, cols), index)


def _rms_fwd(name, x, g):
    s, d = x.shape
    tm = _tile(s, 256)

    def body(x_ref, g_ref, y_ref, r_ref):
        xf = x_ref[...]
        r = lax.rsqrt(jnp.mean(xf * xf, axis=-1, keepdims=True) + EPS)
        y_ref[...] = (xf * r * g_ref[...]).astype(BF16)
        r_ref[...] = r

    return pl.pallas_call(
        body, name=name, grid=(s // tm,),
        in_specs=[pl.BlockSpec((tm, d), lambda i: (i, 0)), pl.BlockSpec((1, d), lambda i: (0, 0))],
        out_specs=(pl.BlockSpec((tm, d), lambda i: (i, 0)), pl.BlockSpec((tm, 1), lambda i: (i, 0))),
        out_shape=(jax.ShapeDtypeStruct((s, d), BF16), jax.ShapeDtypeStruct((s, 1), F32)),
        compiler_params=_params("parallel"),
    )(x, g)


def _rms_bwd(name, dy, x, r, g, skip, with_bf16):
    s, d = x.shape
    tm = _tile(s, 256)
    n = s // tm

    def body(dy_ref, x_ref, r_ref, g_ref, skip_ref, *outs):
        dx_ref = outs[0]
        dg_ref = outs[-2]
        acc_ref = outs[-1]
        i = pl.program_id(0)
        dyv, xv, rv = dy_ref[...], x_ref[...], r_ref[...]

        @pl.when(i == 0)
        def _():
            acc_ref[...] = jnp.zeros_like(acc_ref)

        acc_ref[...] += _colsum8(dyv * xv * rv)
        dg = dyv * g_ref[...]
        dx = skip_ref[...] + rv * (dg - xv * (rv * rv) * jnp.mean(dg * xv, axis=-1, keepdims=True))
        dx_ref[...] = dx
        if with_bf16:
            outs[1][...] = dx.astype(BF16)

        @pl.when(i == n - 1)
        def _():
            dg_ref[...] = jnp.sum(acc_ref[...], axis=0, keepdims=True)

    row = pl.BlockSpec((tm, d), lambda i: (i, 0))
    vec = pl.BlockSpec((1, d), lambda i: (0, 0))
    out_specs = [row] + ([row] if with_bf16 else []) + [vec]
    out_shape = [jax.ShapeDtypeStruct((s, d), F32)] + ([jax.ShapeDtypeStruct((s, d), BF16)] if with_bf16 else []) \
        + [jax.ShapeDtypeStruct((1, d), F32)]
    return pl.pallas_call(
        body, name=name, grid=(n,),
        in_specs=[row, row, pl.BlockSpec((tm, 1), lambda i: (i, 0)), vec, row],
        out_specs=out_specs, out_shape=out_shape, scratch_shapes=[pltpu.VMEM((8, d), F32)],
        compiler_params=_params("arbitrary"),
    )(dy, x, r, g, skip)


def _causal(w):
    t = lax.broadcasted_iota(jnp.int32, w.shape, 0)
    s_ = lax.broadcasted_iota(jnp.int32, w.shape, 1)
    return jnp.where(t >= s_, w, 0.0)


def _gmlp_fwd(proj, w_s, b_st, ln_g, ln_b, g_out, d_model):
    s = proj.shape[0]
    wa = d_model // 2
    groups = wa // HEAD_DIM
    tm = _tile(s, 256)
    n_chunks = tm // CHUNK

    def body(au_ref, av_ref, az_ref, ws_ref, bst_ref, lng_ref, lnb_ref, go_ref,
             y_ref, z_ref, vn_ref, mu_ref, rs_ref, ra_ref):
        gv = _gelu(av_ref[...])
        mu = jnp.mean(gv, axis=-1, keepdims=True)
        xc = gv - mu
        rs = lax.rsqrt(jnp.mean(xc * xc, axis=-1, keepdims=True) + EPS)
        vn = (xc * rs * lng_ref[...] + lnb_ref[...]).astype(BF16)
        vn_ref[...] = vn
        mu_ref[...] = mu
        rs_ref[...] = rs
        for g in range(groups):
            wm = _causal(ws_ref[g]).astype(BF16)
            cols = slice(g * HEAD_DIM, (g + 1) * HEAD_DIM)
            for ch in range(n_chunks):
                rows = slice(ch * CHUNK, (ch + 1) * CHUNK)
                z_ref[rows, cols] = _dot(wm, vn_ref[rows, cols]) + bst_ref[:, g:g + 1]
        ya = _gelu(au_ref[...]) * z_ref[...]
        ra = lax.rsqrt(jnp.mean(ya * ya, axis=-1, keepdims=True) + EPS)
        ra_ref[...] = ra
        sz, _ = _silu_and_grad(az_ref[...])
        y_ref[...] = (ya * ra * go_ref[...] * sz).astype(BF16)

    def col(j):
        return pl.BlockSpec((tm, wa), lambda i: (i, j))
    vec = pl.BlockSpec((1, wa), lambda i: (0, 0))
    stat = pl.BlockSpec((tm, 1), lambda i: (i, 0))
    return pl.pallas_call(
        body, name="gmlp_fwd", grid=(s // tm,),
        in_specs=[col(0), col(1), col(2),
                  pl.BlockSpec((groups, CHUNK, CHUNK), lambda i: (0, 0, 0)),
                  pl.BlockSpec((CHUNK, groups), lambda i: (0, 0)), vec, vec, vec],
        out_specs=(col(0), col(0), col(0), stat, stat, stat),
        out_shape=(jax.ShapeDtypeStruct((s, d_model), BF16), jax.ShapeDtypeStruct((s, wa), F32),
                   jax.ShapeDtypeStruct((s, wa), BF16), jax.ShapeDtypeStruct((s, 1), F32),
                   jax.ShapeDtypeStruct((s, 1), F32), jax.ShapeDtypeStruct((s, 1), F32)),
        compiler_params=_params("parallel"),
    )(proj, proj, proj, w_s, b_st, ln_g, ln_b, g_out)


def _gmlp_bwd(dproj, dy, proj, z, vn, mu, rs, ra, w_s, ln_g, g_out, d_model):
    s = proj.shape[0]
    wa = d_model // 2
    groups = wa // HEAD_DIM
    tm = _tile(s, 256)
    n_chunks = tm // CHUNK
    n = s // tm

    def causal_stack(w):
        t = lax.broadcasted_iota(jnp.int32, w.shape, 1)
        s_ = lax.broadcasted_iota(jnp.int32, w.shape, 2)
        return jnp.where(t >= s_, w, 0.0)

    def body(dproj_in, dy_ref, au_ref, av_ref, az_ref, z_ref, vn_ref, mu_ref, rs_ref, ra_ref, ws_ref, lng_ref, go_ref,
             dp_ref, gws_ref, dzs_ref, glg_ref, glb_ref, ggo_ref,
             dz_s, dvn_s, acc_lg, acc_lb, acc_go):
        del dproj_in
        i = pl.program_id(0)

        @pl.when(i == 0)
        def _():
            gws_ref[...] = jnp.zeros_like(gws_ref)
            dzs_ref[...] = jnp.zeros_like(dzs_ref)
            acc_lg[...] = jnp.zeros_like(acc_lg)
            acc_lb[...] = jnp.zeros_like(acc_lb)
            acc_go[...] = jnp.zeros_like(acc_go)

        au, az, zv, rav = au_ref[...], az_ref[...], z_ref[...], ra_ref[...]
        u = _gelu(au)
        ya = u * zv
        sz, dsz = _silu_and_grad(az)
        dyv = dy_ref[...]
        dp_ref[:, 2 * wa:3 * wa] = (dyv * (ya * rav * go_ref[...]) * dsz).astype(BF16)
        dn = dyv * sz
        acc_go[...] += _colsum8(dn * ya * rav)
        dyg = dn * go_ref[...]
        dya = rav * (dyg - ya * (rav * rav) * jnp.mean(dyg * ya, axis=-1, keepdims=True))
        dp_ref[:, 0:wa] = (dya * zv * _gelu_grad(au)).astype(BF16)
        dz_s[...] = dya * u
        for ch in range(n_chunks):
            dzs_ref[...] += dz_s[ch * CHUNK:(ch + 1) * CHUNK, :]
        for g in range(groups):
            wm = _causal(ws_ref[g]).astype(BF16)
            cols = slice(g * HEAD_DIM, (g + 1) * HEAD_DIM)
            for ch in range(n_chunks):
                rows = slice(ch * CHUNK, (ch + 1) * CHUNK)
                dzb = dz_s[rows, cols].astype(BF16)
                gws_ref[g] += _dot(dzb, vn_ref[rows, cols], tb=True)
                dvn_s[rows, cols] = _dot(wm, dzb, ta=True)
        av = av_ref[...]
        xh = (_gelu(av) - mu_ref[...]) * rs_ref[...]
        dvn = dvn_s[...]
        acc_lb[...] += _colsum8(dvn)
        acc_lg[...] += _colsum8(dvn * xh)
        dxh = dvn * lng_ref[...]
        dgv = rs_ref[...] * (dxh - jnp.mean(dxh, axis=-1, keepdims=True) - xh * jnp.mean(dxh * xh, axis=-1, keepdims=True))
        dp_ref[:, wa:2 * wa] = (dgv * _gelu_grad(av)).astype(BF16)

        @pl.when(i == n - 1)
        def _():
            gws_ref[...] = causal_stack(gws_ref[...])
            glg_ref[...] = jnp.sum(acc_lg[...], axis=0, keepdims=True)
            glb_ref[...] = jnp.sum(acc_lb[...], axis=0, keepdims=True)
            ggo_ref[...] = jnp.sum(acc_go[...], axis=0, keepdims=True)

    def col(j):
        return pl.BlockSpec((tm, wa), lambda i: (i, j))
    vec = pl.BlockSpec((1, wa), lambda i: (0, 0))
    stat = pl.BlockSpec((tm, 1), lambda i: (i, 0))
    ws_spec = pl.BlockSpec((groups, CHUNK, CHUNK), lambda i: (0, 0, 0))
    d_in = dproj.shape[1]
    return pl.pallas_call(
        body, name="gmlp_bwd", grid=(n,),
        in_specs=[ANY, col(0), col(0), col(1), col(2), col(0), col(0), stat, stat, stat, ws_spec, vec, vec],
        out_specs=(pl.BlockSpec((tm, 3 * wa), lambda i: (i, 0)), ws_spec,
                   pl.BlockSpec((CHUNK, wa), lambda i: (0, 0)), vec, vec, vec),
        out_shape=(jax.ShapeDtypeStruct((s, d_in), BF16),
                   jax.ShapeDtypeStruct((groups, CHUNK, CHUNK), F32), jax.ShapeDtypeStruct((CHUNK, wa), F32))
        + (jax.ShapeDtypeStruct((1, wa), F32),) * 3,
        scratch_shapes=[pltpu.VMEM((tm, wa), F32), pltpu.VMEM((tm, wa), F32)] + [pltpu.VMEM((8, wa), F32)] * 3,
        input_output_aliases={0: 0},
        compiler_params=_params("arbitrary"),
    )(dproj, dy, proj, proj, proj, z, vn, mu, rs, ra, w_s, ln_g, g_out)


def _gate_fwd(y_in, yb, proj, g_out, d_model):
    s = proj.shape[0]
    wa = d_model // 2
    tm = _tile(s, 256)

    def body(y_any, yb_ref, bz_ref, go_ref, y_ref, rb_ref):
        del y_any
        ybv = yb_ref[...]
        rb = lax.rsqrt(jnp.mean(ybv * ybv, axis=-1, keepdims=True) + EPS)
        rb_ref[...] = rb
        sz, _ = _silu_and_grad(bz_ref[...])
        y_ref[...] = (ybv * rb * go_ref[...] * sz).astype(BF16)

    return pl.pallas_call(
        body, name="gate_b_fwd", grid=(s // tm,),
        in_specs=[ANY, pl.BlockSpec((tm, wa), lambda i: (i, 0)), pl.BlockSpec((tm, wa), lambda i: (i, 6)),
                  pl.BlockSpec((1, wa), lambda i: (0, 0))],
        out_specs=(pl.BlockSpec((tm, wa), lambda i: (i, 1)), pl.BlockSpec((tm, 1), lambda i: (i, 0))),
        out_shape=(jax.ShapeDtypeStruct((s, d_model), BF16), jax.ShapeDtypeStruct((s, 1), F32)),
        input_output_aliases={0: 0},
        compiler_params=_params("parallel"),
    )(y_in, yb, proj, g_out)


def _gate_bwd(dy, yb, proj, rb, g_out, d_model):
    s, d_in = proj.shape
    wa = d_model // 2
    tm = _tile(s, 256)
    n = s // tm

    def body(dy_ref, yb_ref, bz_ref, rb_ref, go_ref, dp_ref, do_ref, ggo_ref, acc):
        i = pl.program_id(0)

        @pl.when(i == 0)
        def _():
            acc[...] = jnp.zeros_like(acc)

        dyv, ybv, rbv = dy_ref[...], yb_ref[...], rb_ref[...]
        sz, dsz = _silu_and_grad(bz_ref[...])
        dp_ref[...] = (dyv * (ybv * rbv * go_ref[...]) * dsz).astype(BF16)
        dn = dyv * sz
        acc[...] += _colsum8(dn * ybv * rbv)
        dyg = dn * go_ref[...]
        do_ref[...] = rbv * (dyg - ybv * (rbv * rbv) * jnp.mean(dyg * ybv, axis=-1, keepdims=True))

        @pl.when(i == n - 1)
        def _():
            ggo_ref[...] = jnp.sum(acc[...], axis=0, keepdims=True)

    vec = pl.BlockSpec((1, wa), lambda i: (0, 0))
    return pl.pallas_call(
        body, name="gate_b_bwd", grid=(n,),
        in_specs=[pl.BlockSpec((tm, wa), lambda i: (i, 1)), pl.BlockSpec((tm, wa), lambda i: (i, 0)),
                  pl.BlockSpec((tm, wa), lambda i: (i, 6)), pl.BlockSpec((tm, 1), lambda i: (i, 0)), vec],
        out_specs=(pl.BlockSpec((tm, wa), lambda i: (i, 6)), pl.BlockSpec((tm, wa), lambda i: (i, 0)), vec),
        out_shape=(jax.ShapeDtypeStruct((s, d_in), BF16), jax.ShapeDtypeStruct((s, wa), F32),
                   jax.ShapeDtypeStruct((1, wa), F32)),
        scratch_shapes=[pltpu.VMEM((8, wa), F32)],
        compiler_params=_params("arbitrary"),
    )(dy, yb, proj, rb, g_out)


def _bucket_tables():
    qi = BLK + np.arange(BLK)
    kj = np.arange(2 * BLK)
    delta = qi[:, None] - kj[None, :]
    max_exact = NUM_BUCKETS // 2
    tabs = []
    for window, dil in DILATED:
        band = (delta >= 0) & (delta <= window // dil)
        dist = np.clip(delta, 0, None) * dil
        d = np.maximum(dist, 1).astype(np.float32)
        large = max_exact + (np.log(d / np.float32(max_exact)) / np.float32(math.log(MAX_DISTANCE / max_exact))
                             * np.float32(NUM_BUCKETS - max_exact)).astype(np.int32)
        large = np.minimum(large, NUM_BUCKETS - 1)
        tabs.append(np.where(band, np.where(dist < max_exact, dist, large), -1).astype(np.int32))
    return np.stack(tabs)


def _bias_tiles(tab_ref, rb_ref, h, bias_s):
    col = lax.broadcasted_iota(jnp.int32, (BLK, 2 * BLK), 1)
    for i in range(len(DILATED)):
        t = tab_ref[i]
        bias = jnp.zeros(t.shape, F32)
        for b in range(NUM_BUCKETS):
            bias = jnp.where(t == b, rb_ref[b, h], bias)
        bias = jnp.where(t >= 0, bias, NEG_INF)
        bias_s[2 * i] = jnp.where(col >= BLK, bias, NEG_INF)
        bias_s[2 * i + 1] = bias


def _block_rows(b, n_blocks, dil):
    nb = n_blocks // dil
    r = b // nb
    n = b % nb
    start = r + dil * BLK * n
    if dil == 1:
        return pl.ds(pl.multiple_of(start, BLK), BLK), n
    return pl.ds(start, BLK, stride=dil), n


def _sub_block(b, rows=BLK, pad=0):
    return pl.ds(pl.multiple_of(b * BLK + pad, BLK), rows)


def _rms_rows(x, gain):
    return x * lax.rsqrt(jnp.mean(x * x, axis=-1, keepdims=True) + EPS) * gain


def _attn_fwd(proj, g_q, g_k, rel_bias, d_model, comm=None):
    s = proj.shape[0]
    heads = d_model // 2 // HEAD_DIM
    n_blocks = s // BLK
    scale = HEAD_DIM ** -0.5
    tables = jnp.asarray(_bucket_tables())
    c_ins = list(comm.ins) if comm else []
    c_outs = list(comm.out_shape) if comm else []
    c_scratch = list(comm.scratch) if comm else []

    def body(q_ref, k_ref, v_ref, gq_ref, gk_ref, rb_ref, tab_ref, *rest):
        cin_refs, rest = rest[:len(c_ins)], rest[len(c_ins):]
        yb_ref, lse_ref = rest[:2]
        cout_refs, rest = rest[2:2 + len(c_outs)], rest[2 + len(c_outs):]
        qs, ks, vs, m_s, l_s, acc_s, bias_s, sc_s, p_s, w_s = rest[:10]
        sems = rest[10:]
        h = pl.program_id(0)
        if comm:
            @pl.when(h == 0)
            def _():
                comm.start(cin_refs, cout_refs, sems)

        _bias_tiles(tab_ref, rb_ref, h, bias_s)
        gq = gq_ref[...] * scale
        gk = gk_ref[...]
        ks[0:BLK, :] = jnp.zeros((BLK, HEAD_DIM), BF16)
        vs[0:BLK, :] = jnp.zeros((BLK, HEAD_DIM), BF16)

        for i, (_, dil) in enumerate(DILATED):
            def prepare(b, carry, dil=dil):
                rows, _ = _block_rows(b, n_blocks, dil)
                qs[_sub_block(b), :] = _rms_rows(q_ref[rows, :], gq).astype(BF16)
                ks[_sub_block(b, pad=BLK), :] = _rms_rows(k_ref[rows, :], gk).astype(BF16)
                vs[_sub_block(b, pad=BLK), :] = v_ref[rows, :].astype(BF16)
                return carry
            lax.fori_loop(0, n_blocks, prepare, 0, unroll=2)

            def group(g, carry, i=i, dil=dil):
                blocks = [g * ATTN_GROUP + t for t in range(ATTN_GROUP)]
                for t, b in enumerate(blocks):
                    sc_s[t] = _dot(qs[_sub_block(b), :], ks[_sub_block(b, 2 * BLK), :], tb=True)
                for t, b in enumerate(blocks):
                    rows, n = _block_rows(b, n_blocks, dil)
                    sc = sc_s[t] + bias_s[2 * i + jnp.minimum(n, 1)]
                    m_i = jnp.max(sc, axis=-1, keepdims=True)
                    pr = jnp.exp(sc - m_i)
                    l_i = jnp.sum(pr, axis=-1, keepdims=True)
                    p_s[t] = pr.astype(BF16)
                    if i == 0:
                        m_s[rows, :] = m_i
                        l_s[rows, :] = l_i
                    else:
                        m_o = m_s[rows, :]
                        m_n = jnp.maximum(m_o, m_i)
                        c_o = jnp.exp(m_o - m_n)
                        c_i = jnp.exp(m_i - m_n)
                        m_s[rows, :] = m_n
                        l_s[rows, :] = l_s[rows, :] * c_o + l_i * c_i
                        w_s[2 * t] = c_o
                        w_s[2 * t + 1] = c_i
                for t, b in enumerate(blocks):
                    rows, _ = _block_rows(b, n_blocks, dil)
                    a_i = _dot(p_s[t], vs[_sub_block(b, 2 * BLK), :])
                    if i == 0:
                        acc_s[rows, :] = a_i
                    else:
                        acc_s[rows, :] = acc_s[rows, :] * w_s[2 * t] + a_i * w_s[2 * t + 1]
                return carry
            lax.fori_loop(0, n_blocks // ATTN_GROUP, group, 0)

        l = l_s[...]
        yb_ref[...] = acc_s[...] / l
        lse_ref[...] = m_s[...] + jnp.log(l)
        if comm:
            @pl.when(h == heads - 1)
            def _():
                comm.finish(cin_refs, cout_refs, sems)

    def head_col(off):
        return pl.BlockSpec((s, HEAD_DIM), lambda h: (0, off * heads + h))
    vec = pl.BlockSpec((1, HEAD_DIM), lambda h: (0, 0))
    res = pl.pallas_call(
        body, name="attn_fwd", grid=(heads,),
        in_specs=[head_col(3), head_col(4), head_col(5), vec, vec,
                  pl.BlockSpec(memory_space=pltpu.SMEM),
                  pl.BlockSpec((len(DILATED), BLK, 2 * BLK), lambda h: (0, 0, 0))] + [ANY] * len(c_ins),
        out_specs=[pl.BlockSpec((s, HEAD_DIM), lambda h: (0, h)), pl.BlockSpec((None, s, 1), lambda h: (h, 0, 0))]
        + [ANY] * len(c_outs),
        out_shape=[jax.ShapeDtypeStruct((s, heads * HEAD_DIM), F32), jax.ShapeDtypeStruct((heads, s, 1), F32)] + c_outs,
        scratch_shapes=[pltpu.VMEM((s, HEAD_DIM), BF16), pltpu.VMEM((s + BLK, HEAD_DIM), BF16),
                        pltpu.VMEM((s + BLK, HEAD_DIM), BF16),
                        pltpu.VMEM((s, 1), F32), pltpu.VMEM((s, 1), F32), pltpu.VMEM((s, HEAD_DIM), F32),
                        pltpu.VMEM((2 * len(DILATED), BLK, 2 * BLK), F32),
                        pltpu.VMEM((ATTN_GROUP, BLK, 2 * BLK), F32), pltpu.VMEM((ATTN_GROUP, BLK, 2 * BLK), BF16),
                        pltpu.VMEM((2 * ATTN_GROUP, BLK, 1), F32)] + c_scratch,
        input_output_aliases={7 + i: 2 + i for i in range(len(c_ins))} if (comm and comm.in_place) else {},
        compiler_params=_params("arbitrary"),
    )(proj, proj, proj, g_q, g_k, rel_bias, tables, *c_ins)
    return res[0], res[1], res[2:]


def _attn_bwd(dproj, d_o, yb, lse, proj, g_q, g_k, rel_bias, d_model):
    s, d_in = proj.shape
    heads = d_model // 2 // HEAD_DIM
    n_blocks = s // BLK
    scale = HEAD_DIM ** -0.5
    tables = jnp.asarray(_bucket_tables())
    n_dil = len(DILATED)

    def body(dp_any, q_ref, k_ref, v_ref, do_ref, o_ref, lse_ref, gq_ref, gk_ref, rb_ref, tab_ref,
             dp_out, ggq_ref, ggk_ref, db_ref,
             qs, ks, vs, dos, lse_s, del_s, dqn, dkn, dvv, dq_u, dk_u, dv_u, bias_s, dbias_s, sc_s, dp_s, p_s, ds_s,
             sems):
        del dp_any
        h = pl.program_id(0)

        @pl.when(h == 0)
        def _():
            ggq_ref[...] = jnp.zeros_like(ggq_ref)
            ggk_ref[...] = jnp.zeros_like(ggk_ref)

        dbias_s[...] = jnp.zeros_like(dbias_s)
        _bias_tiles(tab_ref, rb_ref, h, bias_s)
        gq = gq_ref[...] * scale
        gk = gk_ref[...]
        ks[0:BLK, :] = jnp.zeros((BLK, HEAD_DIM), BF16)
        vs[0:BLK, :] = jnp.zeros((BLK, HEAD_DIM), BF16)

        for i, (_, dil) in enumerate(DILATED):
            def prepare(b, carry, dil=dil):
                rows, _ = _block_rows(b, n_blocks, dil)
                qs[_sub_block(b), :] = _rms_rows(q_ref[rows, :], gq).astype(BF16)
                ks[_sub_block(b, pad=BLK), :] = _rms_rows(k_ref[rows, :], gk).astype(BF16)
                vs[_sub_block(b, pad=BLK), :] = v_ref[rows, :].astype(BF16)
                do = do_ref[rows, :]
                dos[_sub_block(b), :] = do.astype(BF16)
                del_s[_sub_block(b), :] = jnp.sum(do * o_ref[rows, :], axis=-1, keepdims=True)
                lse_s[_sub_block(b), :] = lse_ref[rows, :]
                return carry
            lax.fori_loop(0, n_blocks, prepare, 0, unroll=2)
            dk_u[...] = jnp.zeros_like(dk_u)
            dv_u[...] = jnp.zeros_like(dv_u)

            def group(g, carry, i=i, dil=dil):
                blocks = [g * ATTN_GROUP + t for t in range(ATTN_GROUP)]
                for t, b in enumerate(blocks):
                    sc_s[t] = _dot(qs[_sub_block(b), :], ks[_sub_block(b, 2 * BLK), :], tb=True)
                    dp_s[t] = _dot(dos[_sub_block(b), :], vs[_sub_block(b, 2 * BLK), :], tb=True)
                for t, b in enumerate(blocks):
                    _, n = _block_rows(b, n_blocks, dil)
                    pr = jnp.exp(sc_s[t] + bias_s[2 * i + jnp.minimum(n, 1)] - lse_s[_sub_block(b), :])
                    ds = pr * (dp_s[t] - del_s[_sub_block(b), :])
                    dbias_s[i] += ds
                    p_s[t] = pr.astype(BF16)
                    ds_s[t] = ds.astype(BF16)
                for t, b in enumerate(blocks):
                    dq_u[_sub_block(b), :] = _dot(ds_s[t], ks[_sub_block(b, 2 * BLK), :])
                    dk_u[_sub_block(b, 2 * BLK), :] += _dot(ds_s[t], qs[_sub_block(b), :], ta=True)
                    dv_u[_sub_block(b, 2 * BLK), :] += _dot(p_s[t], dos[_sub_block(b), :], ta=True)
                return carry
            lax.fori_loop(0, n_blocks // ATTN_GROUP, group, 0)

            def scatter(b, carry, i=i, dil=dil):
                rows, _ = _block_rows(b, n_blocks, dil)
                for acc, part, pad in ((dqn, dq_u, 0), (dkn, dk_u, BLK), (dvv, dv_u, BLK)):
                    val = part[_sub_block(b, pad=pad), :]
                    acc[rows, :] = val if i == 0 else acc[rows, :] + val
                return carry
            lax.fori_loop(0, n_blocks, scatter, 0, unroll=2)

        q = q_ref[...]
        rq = lax.rsqrt(jnp.mean(q * q, axis=-1, keepdims=True) + EPS)
        k = k_ref[...]
        rk = lax.rsqrt(jnp.mean(k * k, axis=-1, keepdims=True) + EPS)
        dq_n = dqn[...]
        ggq_ref[...] += jnp.sum(dq_n * (q * rq) * scale, axis=0, keepdims=True)
        dg = dq_n * gq_ref[...] * scale
        qs[...] = (rq * (dg - q * (rq * rq) * jnp.mean(dg * q, axis=-1, keepdims=True))).astype(BF16)
        dk_n = dkn[...]
        ggk_ref[...] += jnp.sum(dk_n * (k * rk), axis=0, keepdims=True)
        dg = dk_n * gk_ref[...]
        dos[...] = (rk * (dg - k * (rk * rk) * jnp.mean(dg * k, axis=-1, keepdims=True))).astype(BF16)
        vs[BLK:, :] = dvv[...].astype(BF16)
        copies = [pltpu.make_async_copy(src, dp_out.at[:, pl.ds(pl.multiple_of(((3 + j) * heads + h) * HEAD_DIM, HEAD_DIM), HEAD_DIM)],
                                        sems.at[j]) for j, src in enumerate((qs, dos, vs.at[pl.ds(BLK, s)]))]
        for cp in copies:
            cp.start()
        for b in range(NUM_BUCKETS):
            tot = jnp.zeros((BLK, 2 * BLK), F32)
            for i in range(n_dil):
                tot = tot + jnp.where(tab_ref[i] == b, dbias_s[i], 0.0)
            db_ref[b:b + 1, :] = jnp.full((1, LANE), jnp.sum(tot), F32)
        for cp in copies:
            cp.wait()

    def head_col(off):
        return pl.BlockSpec((s, HEAD_DIM), lambda h: (0, off * heads + h))
    own_col = pl.BlockSpec((s, HEAD_DIM), lambda h: (0, h))
    vec = pl.BlockSpec((1, HEAD_DIM), lambda h: (0, 0))
    big = pltpu.VMEM((s, HEAD_DIM), F32)
    padded32 = pltpu.VMEM((s + BLK, HEAD_DIM), F32)
    padded16 = pltpu.VMEM((s + BLK, HEAD_DIM), BF16)
    return pl.pallas_call(
        body, name="attn_bwd", grid=(heads,),
        in_specs=[ANY, head_col(3), head_col(4), head_col(5), own_col, own_col,
                  pl.BlockSpec((None, s, 1), lambda h: (h, 0, 0)), vec, vec,
                  pl.BlockSpec(memory_space=pltpu.SMEM),
                  pl.BlockSpec((n_dil, BLK, 2 * BLK), lambda h: (0, 0, 0))],
        out_specs=(ANY, vec, vec, pl.BlockSpec((None, NUM_BUCKETS, LANE), lambda h: (h, 0, 0))),
        out_shape=(jax.ShapeDtypeStruct((s, d_in), BF16), jax.ShapeDtypeStruct((1, HEAD_DIM), F32),
                   jax.ShapeDtypeStruct((1, HEAD_DIM), F32), jax.ShapeDtypeStruct((heads, NUM_BUCKETS, LANE), F32)),
        scratch_shapes=[pltpu.VMEM((s, HEAD_DIM), BF16), padded16, padded16, pltpu.VMEM((s, HEAD_DIM), BF16),
                        pltpu.VMEM((s, 1), F32), pltpu.VMEM((s, 1), F32),
                        big, big, big, big, padded32, padded32,
                        pltpu.VMEM((2 * n_dil, BLK, 2 * BLK), F32), pltpu.VMEM((n_dil, BLK, 2 * BLK), F32),
                        pltpu.VMEM((ATTN_GROUP, BLK, 2 * BLK), F32), pltpu.VMEM((ATTN_GROUP, BLK, 2 * BLK), F32),
                        pltpu.VMEM((ATTN_GROUP, BLK, 2 * BLK), BF16), pltpu.VMEM((ATTN_GROUP, BLK, 2 * BLK), BF16),
                        pltpu.SemaphoreType.DMA((3,))],
        input_output_aliases={0: 0},
        compiler_params=_params("arbitrary"),
    )(dproj, proj, proj, proj, d_o, yb, lse, g_q, g_k, rel_bias, tables)


def _local_step(xs, ps, tgt, sp, win4, wout4, wgate4, wup4, dist=None):
    s, d = xs.shape
    wa = d // 2
    heads = wa // HEAD_DIM
    d_in = 7 * wa
    c4 = d_in // N_CHIPS
    dple = ps.shape[1]

    tm, tn = _tile(s, 512), _tile(d, 512)
    tn_in = _tile(c4, 512)
    per_in = c4 // tn_in
    tn_up = _tile(d // N_CHIPS, 512)
    per_up = (d // N_CHIPS) // tn_up
    gm, gn = s // tm, d // tn

    hn, rx = _rms_fwd("rms_pre", xs, sp["g_pre"])
    if dist:
        proj, win4 = _mm_in_gather(dist["chip"], hn, win4)
    else:
        proj = _matmul(
            "mm_in", hn, win4, pl.BlockSpec((tm, d), lambda i, j, k: (i, 0)), _shard_spec(d, tn_in, per_in, 2, 1),
            (gm, d_in // tn_in, 1), None, jax.ShapeDtypeStruct((s, d_in), F32),
            pl.BlockSpec((tm, tn_in), lambda i, j, k: (i, j)))
    b_st = sp["b_s"].T
    y, z, vn, mu_v, rs_v, ra = _gmlp_fwd(proj, sp["w_s"], b_st, sp["ln_v_g"], sp["ln_v_b"], sp["g_out_a"], d)
    yb, lse, gathered = _attn_fwd(proj, sp["g_q"], sp["g_k"], sp["rel_bias"], d,
                                  comm=_Gather([wout4, wgate4, wup4]) if dist else None)
    if dist:
        wout4, wgate4, wup4 = gathered
    wout, wgate = wout4.reshape(d, d), wgate4.reshape(d, d)
    y, rb = _gate_fwd(y, yb, proj, sp["g_out_b"], d)

    def residual(acc, extra, outs):
        outs[0][...] = extra[0][...] + acc

    tile = pl.BlockSpec((tm, tn), lambda i, j, k: (i, j))
    h = _matmul(
        "mm_out", y, wout, pl.BlockSpec((tm, d), lambda i, j, k: (i, 0)), pl.BlockSpec((d, tn), lambda i, j, k: (0, j)),
        (gm, gn, 1), None, jax.ShapeDtypeStruct((s, d), F32), tile, extras=(xs,), extra_specs=(tile,), epilogue=residual)
    hp, rh = _rms_fwd("rms_ple", h, sp["g_ple"])
    p16 = ps.astype(BF16)

    def head(acc, extra, outs):
        p_ref, wup_ref, h_ref, t_ref = extra
        dout_ref, dgl_ref, dup_ref, loss_ref = outs
        up = _dot(p_ref[...], wup_ref[...])
        gate = jax.nn.sigmoid(acc)
        err = h_ref[...] + gate * up - t_ref[...]
        sq = _colsum8(err * err)
        part = sq[:, 0:LANE]
        for c in range(1, sq.shape[1] // LANE):
            part = part + sq[:, c * LANE:(c + 1) * LANE]
        loss_ref[...] = part
        dout = err * (1.0 / d)
        dout_ref[...] = dout
        dup_ref[...] = (dout * gate).astype(BF16)
        dgl_ref[...] = (dout * up * gate * (1.0 - gate)).astype(BF16)

    tile_up = pl.BlockSpec((tm, tn_up), lambda i, j, k: (i, j))
    dout, dglin, dup, loss_parts = _matmul(
        "mm_gate_loss", hp, wgate, pl.BlockSpec((tm, d), lambda i, j, k: (i, 0)), pl.BlockSpec((d, tn_up), lambda i, j, k: (0, j)),
        (gm, d // tn_up, 1), None,
        (jax.ShapeDtypeStruct((s, d), F32), jax.ShapeDtypeStruct((s, d), BF16), jax.ShapeDtypeStruct((s, d), BF16),
         jax.ShapeDtypeStruct((gm * 8, (d // tn_up) * LANE), F32)),
        (tile_up, tile_up, tile_up, pl.BlockSpec((8, LANE), lambda i, j, k: (i, j))),
        extras=(p16, wup4, h, tgt),
        extra_specs=(pl.BlockSpec((tm, dple), lambda i, j, k: (i, 0)), _shard_spec(dple, tn_up, per_up, 2, 1),
                     tile_up, tile_up),
        epilogue=head)
    loss = 0.5 * jnp.sum(loss_parts) * (1.0 / d)

    dhp = _matmul(
        "mm_dhp", dglin, wgate, pl.BlockSpec((tm, d), lambda i, j, k: (i, 0)), pl.BlockSpec((tn, d), lambda i, j, k: (j, 0)),
        (gm, gn, 1), None, jax.ShapeDtypeStruct((s, d), F32), tile, tb=True)
    dh, dh16, g_g_ple = _rms_bwd("rms_ple_bwd", dhp, h, rh, sp["g_ple"], dout, True)
    tmw = _tile(d, 512)
    g_wgate = _matmul(
        "mm_gwgate", hp, dglin, pl.BlockSpec((s, tmw), lambda i, j, k: (0, i)), pl.BlockSpec((s, tn), lambda i, j, k: (0, j)),
        (d // tmw, gn, 1), None, jax.ShapeDtypeStruct((d, d), F32), pl.BlockSpec((tmw, tn), lambda i, j, k: (i, j)), ta=True)
    g_wup = _matmul(
        "mm_gwup", p16, dup, pl.BlockSpec((s, dple), lambda i, j, k: (0, 0)), pl.BlockSpec((s, tn_up), lambda i, j, k: (0, j)),
        (1, d // tn_up, 1), None, jax.ShapeDtypeStruct((N_CHIPS, dple, d // N_CHIPS), F32),
        _shard_spec(dple, tn_up, per_up, 0, 1), ta=True)
    dy = _matmul(
        "mm_dy", dh16, wout, pl.BlockSpec((tm, d), lambda i, j, k: (i, 0)), pl.BlockSpec((tn, d), lambda i, j, k: (j, 0)),
        (gm, gn, 1), None, jax.ShapeDtypeStruct((s, d), F32), tile, tb=True)
    g_wout = _matmul(
        "mm_gwout", y, dh16, pl.BlockSpec((s, tmw), lambda i, j, k: (0, i)), pl.BlockSpec((s, tn), lambda i, j, k: (0, j)),
        (d // tmw, gn, 1), None, jax.ShapeDtypeStruct((d, d), F32), pl.BlockSpec((tmw, tn), lambda i, j, k: (i, j)), ta=True)

    dproj, d_o, g_g_out_b = _gate_bwd(dy, yb, proj, rb, sp["g_out_b"], d)
    dproj, g_g_q, g_g_k, dbias = _attn_bwd(dproj, d_o, yb, lse, proj, sp["g_q"], sp["g_k"], sp["rel_bias"], d)
    dproj, g_w_s, dz_sum, g_ln_g, g_ln_b, g_g_out_a = _gmlp_bwd(
        dproj, dy, proj, z, vn, mu_v, rs_v, ra, sp["w_s"], sp["ln_v_g"], sp["g_out_a"], d)

    g_wout4 = g_wout.reshape(N_CHIPS, d // N_CHIPS, d)
    g_wgate4 = g_wgate.reshape(N_CHIPS, d // N_CHIPS, d)
    early = (g_wout4, g_wgate4, g_wup)
    if dist:
        early_sums = [_chip_sum("chip_sum_" + n, dist["core"], g, o)
                      for n, g, o in zip(LARGE[1:], early, _swap_halves("swap_early", list(early)))]
    g_win = _matmul(
        "mm_gwin", hn, dproj, pl.BlockSpec((s, tmw), lambda i, j, k: (0, i)), pl.BlockSpec((s, tn_in), lambda i, j, k: (0, j)),
        (d // tmw, d_in // tn_in, 1), None, jax.ShapeDtypeStruct((N_CHIPS, d, c4), F32),
        _shard_spec(tmw, tn_in, per_in, 0, 1), ta=True,
        comm=_ExchangeChips([s16 for _, s16 in early_sums]) if dist else None)
    if dist:
        g_win, early_got = g_win
        win_sum = _chip_sum("chip_sum_w_in", dist["core"], g_win, _swap_halves("swap_w_in", [g_win])[0])
    tk = c4
    tmh, tnh = _tile(s, 1024), _tile(d, 1024)
    dhn = _matmul(
        "mm_dhn", dproj, win4, pl.BlockSpec((tmh, tk), lambda i, j, k: (i, k)), _shard_spec(tnh, tk, 1, 1, 2),
        (s // tmh, d // tnh, d_in // tk), (tmh, tnh), jax.ShapeDtypeStruct((s, d), F32),
        pl.BlockSpec((tmh, tnh), lambda i, j, k: (i, j)), tb=True,
        comm=_ExchangeChips([win_sum[1]]) if dist else None)
    if dist:
        dhn, win_got = dhn
    grad_x, g_g_pre = _rms_bwd("rms_pre_bwd", dhn, xs, rx, sp["g_pre"], dh, False)

    small = {
        "g_pre": g_g_pre,
        "w_s": g_w_s,
        "b_s": jnp.sum(dz_sum.reshape(CHUNK, heads, HEAD_DIM), axis=-1).T,
        "ln_v_g": g_ln_g, "ln_v_b": g_ln_b,
        "g_q": g_g_q, "g_k": g_g_k,
        "rel_bias": dbias[:, :, 0].T,
        "g_out_a": g_g_out_a, "g_out_b": g_g_out_b,
        "g_ple": g_g_ple,
    }
    if not dist:
        return loss, grad_x, (g_win, *early), small
    sums32 = [win_sum[0]] + [s32 for s32, _ in early_sums]
    blocks = [_total("total_" + n, dist["chip"], dist["core"], s32, o)
              for n, s32, o in zip(LARGE, sums32, list(win_got) + list(early_got))]
    return loss, grad_x, _join_halves(blocks), small


def _place():
    x, y, c = lax.axis_index("x"), lax.axis_index("y"), lax.axis_index("c")
    chips = [(1 - x, y), (x, 1 - y), (1 - x, 1 - y)]
    return x, y, c, chips


def _remote(src, dst, send_sems, recv_sems, k, to):
    return pltpu.make_async_remote_copy(src_ref=src, dst_ref=dst, send_sem=send_sems.at[k], recv_sem=recv_sems.at[k],
                                        device_id=to, device_id_type=MESH)


def _cast_bf16(name, chip, w):
    r, c = w.shape
    tm = _tile(r, 256)

    def body(chip_ref, w_ref, o_ref):
        del chip_ref
        o_ref[...] = w_ref[...].astype(BF16)

    return pl.pallas_call(
        body, name=name,
        grid_spec=pltpu.PrefetchScalarGridSpec(
            num_scalar_prefetch=1, grid=(r // tm,),
            in_specs=[pl.BlockSpec((tm, c), lambda i, chip_ref: (i, 0))],
            out_specs=pl.BlockSpec((None, tm, c), lambda i, chip_ref: (chip_ref[0], i, 0))),
        out_shape=jax.ShapeDtypeStruct((N_CHIPS, r, c), BF16), compiler_params=_params("parallel"),
    )(chip, w)


class _Gather:
    in_place = True

    def __init__(self, fulls):
        self.ins = list(fulls)
        self.out_shape = [jax.ShapeDtypeStruct(f.shape, f.dtype) for f in fulls]
        n = len(fulls)
        self.scratch = [pltpu.SemaphoreType.DMA((6 * n,)), pltpu.SemaphoreType.DMA((6 * n,))]

    @staticmethod
    def _sends(outs, sems):
        send_sems, recv_sems = sems
        x, y, c, chips = _place()
        cps = []
        for w, ref in enumerate(outs):
            half = ref.shape[1] // 2
            blk = ref.at[2 * x + y, pl.ds(c * half, half)]
            cps += [_remote(blk, blk, send_sems, recv_sems, 6 * w + q, (*chip, c)) for q, chip in enumerate(chips)]
        return cps

    def start(self, ins, outs, sems):
        for cp in self._sends(outs, sems):
            cp.start()

    def finish(self, ins, outs, sems):
        send_sems, recv_sems = sems
        x, y, c, chips = _place()
        sibling = (x, y, 1 - c)
        forwards = []
        for w, ref in enumerate(outs):
            half = ref.shape[1] // 2
            for q, chip in enumerate(chips):
                blk = ref.at[2 * chip[0] + chip[1], pl.ds(c * half, half)]
                _remote(blk, blk, send_sems, recv_sems, 6 * w + q, sibling).wait_recv()
                fwd = _remote(blk, blk, send_sems, recv_sems, 6 * w + 3 + q, sibling)
                fwd.start()
                forwards.append(fwd)
        for w, ref in enumerate(outs):
            half = ref.shape[1] // 2
            for q, chip in enumerate(chips):
                blk = ref.at[2 * chip[0] + chip[1], pl.ds((1 - c) * half, half)]
                _remote(blk, blk, send_sems, recv_sems, 6 * w + 3 + q, sibling).wait_recv()
        for cp in self._sends(outs, sems) + forwards:
            cp.wait_send()


class _ExchangeChips:
    in_place = False

    def __init__(self, sums16):
        self.ins = list(sums16)
        self.out_shape = [jax.ShapeDtypeStruct(g.shape, g.dtype) for g in sums16]
        n = len(sums16)
        self.scratch = [pltpu.SemaphoreType.DMA((3 * n,)), pltpu.SemaphoreType.DMA((3 * n,))]

    @staticmethod
    def _sends(ins, outs, sems):
        send_sems, recv_sems = sems
        x, y, c, chips = _place()
        return [_remote(ins[w].at[2 * chip[0] + chip[1]], outs[w].at[2 * x + y], send_sems, recv_sems, 3 * w + q, (*chip, c))
                for w in range(len(ins)) for q, chip in enumerate(chips)]

    def start(self, ins, outs, sems):
        for cp in self._sends(ins, outs, sems):
            cp.start()

    def finish(self, ins, outs, sems):
        send_sems, recv_sems = sems
        x, y, c, chips = _place()
        for w in range(len(ins)):
            for q, chip in enumerate(chips):
                blk = outs[w].at[2 * chip[0] + chip[1]]
                _remote(blk, blk, send_sems, recv_sems, 3 * w + q, (*chip, c)).wait_recv()
        for cp in self._sends(ins, outs, sems):
            cp.wait_send()


def _mm_in_gather(chip, hn, win4):
    s, d = hn.shape
    c4 = win4.shape[2]
    tm, tn = _tile(s, 512), _tile(c4, 512)
    per = c4 // tn
    nj, gm = N_CHIPS * per, s // tm
    half = d // 2

    def block_of(j, chip_ref):
        o = j // per
        return chip_ref[0] ^ (((o & 1) << 1) | (o >> 1))

    def body(chip_ref, hn_ref, w_any, proj_ref, w_ref, wbuf, wsem, send_sems, recv_sems):
        del w_any
        j, i = pl.program_id(0), pl.program_id(1)
        x, y, c, chips = _place()
        sibling = (x, y, 1 - c)

        def send_mine(q):
            blk = w_ref.at[2 * x + y, pl.ds(c * half, half)]
            return _remote(blk, blk, send_sems, recv_sems, q, (*chips[q], c))

        def forward(q):
            chip = chips[q]
            blk = w_ref.at[2 * chip[0] + chip[1], pl.ds(c * half, half)]
            return _remote(blk, blk, send_sems, recv_sems, 3 + q, sibling)

        def arrive(q):
            chip = chips[q]
            blk = w_ref.at[2 * chip[0] + chip[1], pl.ds(c * half, half)]
            _remote(blk, blk, send_sems, recv_sems, q, sibling).wait_recv()
            forward(q).start()
            oth = w_ref.at[2 * chip[0] + chip[1], pl.ds((1 - c) * half, half)]
            _remote(oth, oth, send_sems, recv_sems, 3 + q, sibling).wait_recv()

        def tile(jj, slot):
            return pltpu.make_async_copy(w_ref.at[block_of(jj, chip_ref), :, pl.ds(pl.multiple_of((jj % per) * tn, tn), tn)],
                                         wbuf.at[slot], wsem.at[slot])

        @pl.when((j == 0) & (i == 0))
        def _():
            send_mine(0).start()
            send_mine(1).start()
            tile(0, 0).start()

        @pl.when(i == 0)
        def _():
            for o in range(1, N_CHIPS):
                @pl.when(j + 1 == o * per)
                def _():
                    arrive(o - 1)
                    if o == 1:
                        send_mine(2).start()

            @pl.when(j + 1 < nj)
            def _():
                tile(j + 1, (j + 1) % 2).start()

            tile(j, j % 2).wait()

        proj_ref[...] = _dot(hn_ref[...], wbuf[j % 2])

        @pl.when((j == nj - 1) & (i == gm - 1))
        def _():
            for q in range(3):
                send_mine(q).wait_send()
                forward(q).wait_send()

    return pl.pallas_call(
        body, name="mm_in_gather",
        grid_spec=pltpu.PrefetchScalarGridSpec(
            num_scalar_prefetch=1, grid=(nj, gm),
            in_specs=[pl.BlockSpec((tm, d), lambda j, i, chip_ref: (i, 0)), ANY],
            out_specs=[pl.BlockSpec((tm, tn), lambda j, i, chip_ref: (i, block_of(j, chip_ref) * per + j % per)), ANY],
            scratch_shapes=[pltpu.VMEM((2, d, tn), BF16), pltpu.SemaphoreType.DMA((2,)),
                            pltpu.SemaphoreType.DMA((6,)), pltpu.SemaphoreType.DMA((6,))]),
        out_shape=[jax.ShapeDtypeStruct((s, N_CHIPS * c4), F32), jax.ShapeDtypeStruct(win4.shape, win4.dtype)],
        input_output_aliases={2: 1},
        compiler_params=_params("arbitrary", "arbitrary"),
    )(chip, hn, win4)


def _run_comm(name, comm):
    n_in, n_out = len(comm.ins), len(comm.out_shape)

    def body(*refs):
        ins, outs, sems = refs[:n_in], refs[n_in:n_in + n_out], refs[n_in + n_out:]
        comm.start(ins, outs, sems)
        comm.finish(ins, outs, sems)

    return pl.pallas_call(
        body, name=name, in_specs=[ANY] * n_in, out_specs=[ANY] * n_out, out_shape=comm.out_shape,
        scratch_shapes=comm.scratch, input_output_aliases={i: i for i in range(n_in)} if comm.in_place else {},
    )(*comm.ins)


def _swap_halves(name, grads):
    nw = len(grads)

    def body(*refs):
        ins, outs = refs[:nw], refs[nw:2 * nw]
        send_sems, recv_sems = refs[2 * nw:]
        x, y, c, _ = _place()
        copies = []
        for w in range(nw):
            half = ins[w].shape[1] // 2
            cp = _remote(ins[w].at[:, pl.ds((1 - c) * half, half)], outs[w], send_sems, recv_sems, w, (x, y, 1 - c))
            cp.start()
            copies.append(cp)
        for cp in copies:
            cp.wait()

    return pl.pallas_call(
        body, name=name,
        in_specs=[ANY] * nw, out_specs=[ANY] * nw,
        out_shape=[jax.ShapeDtypeStruct((N_CHIPS, g.shape[1] // 2, g.shape[2]), g.dtype) for g in grads],
        scratch_shapes=[pltpu.SemaphoreType.DMA((nw,)), pltpu.SemaphoreType.DMA((nw,))],
    )(*grads)


def _chip_sum(name, core, grad, got):
    _, r, c = grad.shape
    half = r // 2
    th = _tile(half, 128)
    n = half // th

    def body(core_ref, g_ref, o_ref, s32_ref, s16_ref):
        del core_ref
        v = g_ref[...] + o_ref[...]
        s32_ref[...] = v
        s16_ref[...] = v.astype(BF16)

    blk = pl.BlockSpec((None, th, c), lambda k, i, core_ref: (k, i, 0))
    return pl.pallas_call(
        body, name=name,
        grid_spec=pltpu.PrefetchScalarGridSpec(
            num_scalar_prefetch=1, grid=(N_CHIPS, n),
            in_specs=[pl.BlockSpec((None, th, c), lambda k, i, core_ref: (k, core_ref[0] * n + i, 0)), blk],
            out_specs=[blk, blk]),
        out_shape=[jax.ShapeDtypeStruct((N_CHIPS, half, c), F32), jax.ShapeDtypeStruct((N_CHIPS, half, c), BF16)],
        compiler_params=_params("parallel", "parallel"),
    )(core, grad, got)


def _total(name, chip, core, sum32, got16):
    _, half, c = sum32.shape
    th = _tile(half, 128)
    n = half // th

    def body(chip_ref, core_ref, own_ref, a_ref, b_ref, c_ref, o_ref):
        del chip_ref, core_ref
        o_ref[...] = ((own_ref[...] + a_ref[...].astype(F32)) + b_ref[...].astype(F32)) + c_ref[...].astype(F32)

    def other(step):
        return pl.BlockSpec((None, th, c), lambda i, chip_ref, core_ref: ((chip_ref[0] + step) % N_CHIPS, i, 0))

    return pl.pallas_call(
        body, name=name,
        grid_spec=pltpu.PrefetchScalarGridSpec(
            num_scalar_prefetch=2, grid=(n,),
            in_specs=[other(0), other(1), other(2), other(3)],
            out_specs=pl.BlockSpec((th, c), lambda i, chip_ref, core_ref: (core_ref[0] * n + i, 0))),
        out_shape=jax.ShapeDtypeStruct((2 * half, c), F32),
        compiler_params=_params("parallel"),
    )(chip, core, sum32, got16, got16, got16)


def _join_halves(blocks):
    nw = len(blocks)

    def body(*refs):
        outs = refs[nw:2 * nw]
        send_sems, recv_sems = refs[2 * nw:]
        x, y, c, _ = _place()
        copies = []
        for w in range(nw):
            half = outs[w].shape[0] // 2
            rows = outs[w].at[pl.ds(c * half, half)]
            cp = _remote(rows, rows, send_sems, recv_sems, w, (x, y, 1 - c))
            cp.start()
            copies.append(cp)
        for w, cp in enumerate(copies):
            cp.wait_send()
            half = outs[w].shape[0] // 2
            rows = outs[w].at[pl.ds((1 - c) * half, half)]
            _remote(rows, rows, send_sems, recv_sems, w, (x, y, 1 - c)).wait_recv()

    return pl.pallas_call(
        body, name="join_halves",
        in_specs=[ANY] * nw, out_specs=[ANY] * nw,
        out_shape=[jax.ShapeDtypeStruct(t.shape, t.dtype) for t in blocks],
        scratch_shapes=[pltpu.SemaphoreType.DMA((nw,)), pltpu.SemaphoreType.DMA((nw,))],
        input_output_aliases={i: i for i in range(nw)},
    )(*blocks)


def _gather_small(pack):
    m_per, n = pack.shape

    def body(x_ref, out_ref, send_sems, recv_sems, local_sem):
        x, y, c, chips = _place()
        me, sibling = (x, y, c), (x, y, 1 - c)

        def rows(px, py, pc):
            return out_ref.at[pl.ds((4 * px + 2 * py + pc) * m_per, m_per), :]

        def copy(k, block, to, src=None):
            return _remote(rows(*block) if src is None else src, rows(*block), send_sems, recv_sems, k, to)

        mine = pltpu.make_async_copy(x_ref, rows(*me), local_sem)
        mine.start()
        first = [copy(0, me, sibling, src=x_ref)]
        first += [copy(1 + j, me, (*chip, c), src=x_ref) for j, chip in enumerate(chips)]
        for cp in first:
            cp.start()
        passed = [copy(4 + j, (*chip, c), sibling) for j, chip in enumerate(chips)]
        for j, chip in enumerate(chips):
            copy(1 + j, (*chip, c), me).wait_recv()
            passed[j].start()
        copy(0, sibling, me).wait_recv()
        for j, chip in enumerate(chips):
            copy(4 + j, (*chip, 1 - c), me).wait_recv()
        for cp in first + passed:
            cp.wait_send()
        mine.wait()

    return pl.pallas_call(
        body, name="gather_small",
        out_shape=jax.ShapeDtypeStruct((N_DEV * m_per, n), pack.dtype),
        in_specs=[pl.BlockSpec(memory_space=pltpu.VMEM)],
        out_specs=pl.BlockSpec(memory_space=pltpu.VMEM),
        scratch_shapes=[pltpu.SemaphoreType.DMA((7,)), pltpu.SemaphoreType.DMA((7,)), pltpu.SemaphoreType.DMA],
        compiler_params=_params(),
    )(pack)


def _adamw_math(w, g, m, v):
    m = ADAM_B1 * m + (1.0 - ADAM_B1) * g
    v = ADAM_B2 * v + (1.0 - ADAM_B2) * (g * g)
    m_hat = m / (1.0 - ADAM_B1 ** ADAM_STEP)
    v_hat = v / (1.0 - ADAM_B2 ** ADAM_STEP)
    delta = -ADAM_LR * (m_hat / (jnp.sqrt(v_hat) + ADAM_EPS) + ADAM_WD * w)
    return delta, m, v


def _adamw(name, w, g, m, v):
    r, c = w.shape
    tm = _tile(r, 128)

    def body(w_ref, g_ref, m_ref, v_ref, d_out, m_out, v_out):
        d_out[...], m_out[...], v_out[...] = _adamw_math(w_ref[...], g_ref[...], m_ref[...], v_ref[...])

    spec = pl.BlockSpec((tm, c), lambda i: (i, 0))
    return pl.pallas_call(
        body, name=name, grid=(r // tm,), in_specs=[spec] * 4, out_specs=[spec] * 3,
        out_shape=[jax.ShapeDtypeStruct((r, c), F32)] * 3, compiler_params=_params("parallel"),
    )(w, g, m, v)


def _adamw_small(gathered, w, m, v):
    rows = w.shape[0]

    def body(all_ref, w_ref, m_ref, v_ref, g_out, d_out, m_out, v_out):
        g = all_ref[0:rows, :]
        for dev in range(1, N_DEV):
            g = g + all_ref[dev * rows:(dev + 1) * rows, :]
        g_out[...] = g
        d_out[...], m_out[...], v_out[...] = _adamw_math(w_ref[...], g, m_ref[...], v_ref[...])

    return pl.pallas_call(
        body, name="adamw_small", out_shape=[jax.ShapeDtypeStruct(w.shape, F32)] * 4, compiler_params=_params(),
    )(gathered, w, m, v)


SMALL = ("g_pre", "w_s", "b_s", "ln_v_g", "ln_v_b", "g_q", "g_k", "rel_bias", "g_out_a", "g_out_b", "g_ple")
LARGE = ("w_in", "w_out", "w_ple_gate", "w_ple_up")
WEIGHTS = ("g_pre", "w_in", "w_s", "b_s", "ln_v_g", "ln_v_b", "g_q", "g_k", "rel_bias", "g_out_a", "g_out_b", "w_out",
           "g_ple", "w_ple_gate", "w_ple_up")


def _pack(parts):
    flat = jnp.concatenate([parts[n].reshape(-1).astype(F32) for n in SMALL])
    rows = -(-flat.shape[0] // (8 * LANE)) * 8
    return jnp.pad(flat, (0, rows * LANE - flat.shape[0])).reshape(rows, LANE)


def _unpack(pack, like):
    flat = pack.reshape(-1)
    out, at = {}, 0
    for n in SMALL:
        size = math.prod(like[n].shape)
        out[n] = flat[at:at + size].reshape(like[n].shape)
        at += size
    return out


def kernel(x, p, g_pre, w_in, w_s, b_s, ln_v_g, ln_v_b, g_q, g_k, rel_bias, g_out_a, g_out_b, w_out, g_ple, w_ple_gate, w_ple_up, loss_target, m_g_pre, m_w_in, m_w_s, m_b_s, m_ln_v_g, m_ln_v_b, m_g_q, m_g_k, m_rel_bias, m_g_out_a, m_g_out_b, m_w_out, m_g_ple, m_w_ple_gate, m_w_ple_up, v_g_pre, v_w_in, v_w_s, v_b_s, v_ln_v_g, v_ln_v_b, v_g_q, v_g_k, v_rel_bias, v_g_out_a, v_g_out_b, v_w_out, v_g_ple, v_w_ple_gate, v_w_ple_up):
    given = dict(locals())
    weights = {n: given[n] for n in WEIGHTS}
    mom_m = {n: given["m_" + n] for n in WEIGHTS}
    mom_v = {n: given["v_" + n] for n in WEIGHTS}
    xs, ps, tgt = x[0], p[0, 0], loss_target[0]
    d = xs.shape[1]

    core = lax.axis_index("c").astype(jnp.int32).reshape(1)
    chip = (2 * lax.axis_index("x") + lax.axis_index("y")).astype(jnp.int32).reshape(1)

    win4, wout4, wgate4, wup4 = [_cast_bf16("cast_" + n, chip, weights[n][0]) for n in LARGE]

    sp = {
        "g_pre": g_pre, "w_s": w_s[0], "b_s": b_s[0], "ln_v_g": ln_v_g, "ln_v_b": ln_v_b, "g_q": g_q, "g_k": g_k,
        "rel_bias": rel_bias, "g_out_a": g_out_a, "g_out_b": g_out_b, "g_ple": g_ple,
    }
    loss_local, grad_x, blocks, small = _local_step(xs, ps, tgt, sp, win4, wout4, wgate4, wup4,
                                                    dist={"chip": chip, "core": core})
    loss = lax.psum(loss_local, MESH_AXES)
    grads = dict(zip(LARGE, blocks))

    out_g, out_d, out_m, out_v = {}, {}, {}, {}
    for n in LARGE:
        out_g[n] = grads[n][None]
        dl, mn, vn = _adamw("adamw_" + n, weights[n][0], grads[n], mom_m[n][0], mom_v[n][0])
        out_d[n], out_m[n], out_v[n] = dl[None], mn[None], vn[None]

    gathered = _gather_small(_pack(small))
    pg, pd, pm, pv = _adamw_small(gathered, _pack(weights), _pack(mom_m), _pack(mom_v))
    for packed, out in ((pg, out_g), (pd, out_d), (pm, out_m), (pv, out_v)):
        out.update(_unpack(packed, weights))

    return (loss, grad_x[None], *[out_g[n] for n in WEIGHTS], *[out_d[n] for n in WEIGHTS],
            *[out_m[n] for n in WEIGHTS], *[out_v[n] for n in WEIGHTS])
```

```python
import functools
import math

import numpy as np

import jax
import jax.numpy as jnp
from jax import lax
from jax.experimental import pallas as pl
from jax.experimental.pallas import tpu as pltpu

F32 = jnp.float32
BF16 = jnp.bfloat16

HEAD_DIM = 128
CHUNK = 128
BLK = 128
DILATED = ((128, 1), (512, 4), (2048, 16))
NUM_BUCKETS = 32
MAX_DISTANCE = 2048
ATTN_GROUP = 4
EPS = 1e-6
NEG_INF = -1e30
N_CHIPS = 4
N_DEV = 8
MESH_AXES = ("x", "y", "c")

ADAM_LR = 0.001
ADAM_B1 = 0.9
ADAM_B2 = 0.999
ADAM_EPS = 1e-08
ADAM_WD = 0.01
ADAM_STEP = 10

V7X_VMEM_LIMIT = 56 * 1024 * 1024
LANE = 128
MESH = pl.DeviceIdType.MESH
ANY = pl.BlockSpec(memory_space=pl.ANY)


def _params(*sem):
    return pltpu.CompilerParams(dimension_semantics=sem or None, vmem_limit_bytes=V7X_VMEM_LIMIT)


def _tile(n, target):
    if n <= target:
        return n
    t = (target // LANE) * LANE
    while t > LANE and n % t:
        t -= LANE
    assert n % t == 0, (n, target)
    return t


def _gelu(x):
    return 0.5 * x * (1.0 + lax.erf(x * (1.0 / math.sqrt(2.0))))


def _gelu_grad(x):
    return 0.5 * (1.0 + lax.erf(x * (1.0 / math.sqrt(2.0)))) + x * jnp.exp(-0.5 * x * x) * (1.0 / math.sqrt(2.0 * math.pi))


def _silu_and_grad(x):
    s = jax.nn.sigmoid(x)
    return x * s, s * (1.0 + x * (1.0 - s))


def _dot(a, b, ta=False, tb=False):
    return lax.dot_general(a, b, (((0 if ta else 1,), (1 if tb else 0,)), ((), ())), preferred_element_type=F32)


def _colsum8(v):
    return jnp.sum(v.reshape(v.shape[0] // 8, 8, v.shape[1]), axis=0)


def _matmul(name, a, b, a_spec, b_spec, grid, acc_shape, out_shape, out_specs, ta=False, tb=False,
            extras=(), extra_specs=(), epilogue=None, comm=None):
    nk = grid[2]
    n_extra = len(extras)
    single = not isinstance(out_shape, (tuple, list))
    outs_shape = (out_shape,) if single else tuple(out_shape)
    outs_specs = (out_specs,) if single else tuple(out_specs)
    n_out = len(outs_shape)
    c_ins = list(comm.ins) if comm else []
    c_outs = list(comm.out_shape) if comm else []
    c_scratch = list(comm.scratch) if comm else []
    n_cin, n_cout = len(c_ins), len(c_outs)

    def finish(acc, extra_refs, out_refs):
        if epilogue is None:
            out_refs[0][...] = acc.astype(out_refs[0].dtype)
        else:
            epilogue(acc, extra_refs, out_refs)

    def body(a_ref, b_ref, *rest):
        extra_refs = rest[:n_extra]
        cin_refs = rest[n_extra:n_extra + n_cin]
        out_refs = rest[n_extra + n_cin:n_extra + n_cin + n_out]
        cout_refs = rest[n_extra + n_cin + n_out:n_extra + n_cin + n_out + n_cout]
        scratch_refs = rest[n_extra + n_cin + n_out + n_cout:]
        ids = [pl.program_id(ax) for ax in range(3)]
        if comm:
            sems = scratch_refs[len(scratch_refs) - len(c_scratch):]

            @pl.when((ids[0] == 0) & (ids[1] == 0) & (ids[2] == 0))
            def _():
                comm.start(cin_refs, cout_refs, sems)

        if nk == 1:
            finish(_dot(a_ref[...], b_ref[...], ta, tb), extra_refs, out_refs)
        else:
            acc_ref = scratch_refs[0]

            @pl.when(ids[2] == 0)
            def _():
                acc_ref[...] = jnp.zeros_like(acc_ref)

            acc_ref[...] += _dot(a_ref[...], b_ref[...], ta, tb)

            @pl.when(ids[2] == nk - 1)
            def _():
                finish(acc_ref[...], extra_refs, out_refs)

        if comm:
            @pl.when((ids[0] == grid[0] - 1) & (ids[1] == grid[1] - 1) & (ids[2] == nk - 1))
            def _():
                comm.finish(cin_refs, cout_refs, sems)

    scratch = ([] if nk == 1 else [pltpu.VMEM(acc_shape, F32)]) + c_scratch
    aliases = {2 + n_extra + i: n_out + i for i in range(n_cin)} if (comm and comm.in_place) else {}
    res = pl.pallas_call(
        body, name=name, grid=grid,
        in_specs=[a_spec, b_spec, *extra_specs] + [ANY] * n_cin,
        out_specs=list(outs_specs) + [ANY] * n_cout, out_shape=list(outs_shape) + c_outs, scratch_shapes=scratch,
        input_output_aliases=aliases,
        compiler_params=_params(*(("arbitrary",) * 3 if comm else ("parallel", "parallel", "arbitrary"))),
    )(a, b, *extras, *c_ins)
    if comm:
        main = res[:n_out]
        return (main[0] if single else main), res[n_out:]
    return res[0] if single else res


def _shard_spec(rows, cols, per_shard, row_axis, col_axis):
    def index(i, j, k):
        g = (i, j, k)
        return (g[col_axis] // per_shard, g[row_axis], g[col_axis] % per_shard)
    return pl.BlockSpec((None, rows, cols), index)


def _rms_fwd(name, x, g):
    s, d = x.shape
    tm = _tile(s, 256)

    def body(x_ref, g_ref, y_ref, r_ref):
        xf = x_ref[...]
        r = lax.rsqrt(jnp.mean(xf * xf, axis=-1, keepdims=True) + EPS)
        y_ref[...] = (xf * r * g_ref[...]).astype(BF16)
        r_ref[...] = r

    return pl.pallas_call(
        body, name=name, grid=(s // tm,),
        in_specs=[pl.BlockSpec((tm, d), lambda i: (i, 0)), pl.BlockSpec((1, d), lambda i: (0, 0))],
        out_specs=(pl.BlockSpec((tm, d), lambda i: (i, 0)), pl.BlockSpec((tm, 1), lambda i: (i, 0))),
        out_shape=(jax.ShapeDtypeStruct((s, d), BF16), jax.ShapeDtypeStruct((s, 1), F32)),
        compiler_params=_params("parallel"),
    )(x, g)


def _rms_bwd(name, dy, x, r, g, skip, with_bf16):
    s, d = x.shape
    tm = _tile(s, 256)
    n = s // tm

    def body(dy_ref, x_ref, r_ref, g_ref, skip_ref, *outs):
        dx_ref = outs[0]
        dg_ref = outs[-2]
        acc_ref = outs[-1]
        i = pl.program_id(0)
        dyv, xv, rv = dy_ref[...], x_ref[...], r_ref[...]

        @pl.when(i == 0)
        def _():
            acc_ref[...] = jnp.zeros_like(acc_ref)

        acc_ref[...] += _colsum8(dyv * xv * rv)
        dg = dyv * g_ref[...]
        dx = skip_ref[...] + rv * (dg - xv * (rv * rv) * jnp.mean(dg * xv, axis=-1, keepdims=True))
        dx_ref[...] = dx
        if with_bf16:
            outs[1][...] = dx.astype(BF16)

        @pl.when(i == n - 1)
        def _():
            dg_ref[...] = jnp.sum(acc_ref[...], axis=0, keepdims=True)

    row = pl.BlockSpec((tm, d), lambda i: (i, 0))
    vec = pl.BlockSpec((1, d), lambda i: (0, 0))
    out_specs = [row] + ([row] if with_bf16 else []) + [vec]
    out_shape = [jax.ShapeDtypeStruct((s, d), F32)] + ([jax.ShapeDtypeStruct((s, d), BF16)] if with_bf16 else []) \
        + [jax.ShapeDtypeStruct((1, d), F32)]
    return pl.pallas_call(
        body, name=name, grid=(n,),
        in_specs=[row, row, pl.BlockSpec((tm, 1), lambda i: (i, 0)), vec, row],
        out_specs=out_specs, out_shape=out_shape, scratch_shapes=[pltpu.VMEM((8, d), F32)],
        compiler_params=_params("arbitrary"),
    )(dy, x, r, g, skip)


def _causal(w):
    t = lax.broadcasted_iota(jnp.int32, w.shape, 0)
    s_ = lax.broadcasted_iota(jnp.int32, w.shape, 1)
    return jnp.where(t >= s_, w, 0.0)


def _gmlp_fwd(proj, w_s, b_st, ln_g, ln_b, g_out, d_model):
    s = proj.shape[0]
    wa = d_model // 2
    groups = wa // HEAD_DIM
    tm = _tile(s, 256)
    n_chunks = tm // CHUNK

    def body(au_ref, av_ref, az_ref, ws_ref, bst_ref, lng_ref, lnb_ref, go_ref,
             y_ref, z_ref, vn_ref, mu_ref, rs_ref, ra_ref):
        gv = _gelu(av_ref[...])
        mu = jnp.mean(gv, axis=-1, keepdims=True)
        xc = gv - mu
        rs = lax.rsqrt(jnp.mean(xc * xc, axis=-1, keepdims=True) + EPS)
        vn = (xc * rs * lng_ref[...] + lnb_ref[...]).astype(BF16)
        vn_ref[...] = vn
        mu_ref[...] = mu
        rs_ref[...] = rs
        for g in range(groups):
            wm = _causal(ws_ref[g]).astype(BF16)
            cols = slice(g * HEAD_DIM, (g + 1) * HEAD_DIM)
            for ch in range(n_chunks):
                rows = slice(ch * CHUNK, (ch + 1) * CHUNK)
                z_ref[rows, cols] = _dot(wm, vn_ref[rows, cols]) + bst_ref[:, g:g + 1]
        ya = _gelu(au_ref[...]) * z_ref[...]
        ra = lax.rsqrt(jnp.mean(ya * ya, axis=-1, keepdims=True) + EPS)
        ra_ref[...] = ra
        sz, _ = _silu_and_grad(az_ref[...])
        y_ref[...] = (ya * ra * go_ref[...] * sz).astype(BF16)

    def col(j):
        return pl.BlockSpec((tm, wa), lambda i: (i, j))
    vec = pl.BlockSpec((1, wa), lambda i: (0, 0))
    stat = pl.BlockSpec((tm, 1), lambda i: (i, 0))
    return pl.pallas_call(
        body, name="gmlp_fwd", grid=(s // tm,),
        in_specs=[col(0), col(1), col(2),
                  pl.BlockSpec((groups, CHUNK, CHUNK), lambda i: (0, 0, 0)),
                  pl.BlockSpec((CHUNK, groups), lambda i: (0, 0)), vec, vec, vec],
        out_specs=(col(0), col(0), col(0), stat, stat, stat),
        out_shape=(jax.ShapeDtypeStruct((s, d_model), BF16), jax.ShapeDtypeStruct((s, wa), F32),
                   jax.ShapeDtypeStruct((s, wa), BF16), jax.ShapeDtypeStruct((s, 1), F32),
                   jax.ShapeDtypeStruct((s, 1), F32), jax.ShapeDtypeStruct((s, 1), F32)),
        compiler_params=_params("parallel"),
    )(proj, proj, proj, w_s, b_st, ln_g, ln_b, g_out)


def _gmlp_bwd(dproj, dy, proj, z, vn, mu, rs, ra, w_s, ln_g, g_out, d_model):
    s = proj.shape[0]
    wa = d_model // 2
    groups = wa // HEAD_DIM
    tm = _tile(s, 256)
    n_chunks = tm // CHUNK
    n = s // tm

    def causal_stack(w):
        t = lax.broadcasted_iota(jnp.int32, w.shape, 1)
        s_ = lax.broadcasted_iota(jnp.int32, w.shape, 2)
        return jnp.where(t >= s_, w, 0.0)

    def body(dproj_in, dy_ref, au_ref, av_ref, az_ref, z_ref, vn_ref, mu_ref, rs_ref, ra_ref, ws_ref, lng_ref, go_ref,
             dp_ref, gws_ref, dzs_ref, glg_ref, glb_ref, ggo_ref,
             dz_s, dvn_s, acc_lg, acc_lb, acc_go):
        del dproj_in
        i = pl.program_id(0)

        @pl.when(i == 0)
        def _():
            gws_ref[...] = jnp.zeros_like(gws_ref)
            dzs_ref[...] = jnp.zeros_like(dzs_ref)
            acc_lg[...] = jnp.zeros_like(acc_lg)
            acc_lb[...] = jnp.zeros_like(acc_lb)
            acc_go[...] = jnp.zeros_like(acc_go)

        au, az, zv, rav = au_ref[...], az_ref[...], z_ref[...], ra_ref[...]
        u = _gelu(au)
        ya = u * zv
        sz, dsz = _silu_and_grad(az)
        dyv = dy_ref[...]
        dp_ref[:, 2 * wa:3 * wa] = (dyv * (ya * rav * go_ref[...]) * dsz).astype(BF16)
        dn = dyv * sz
        acc_go[...] += _colsum8(dn * ya * rav)
        dyg = dn * go_ref[...]
        dya = rav * (dyg - ya * (rav * rav) * jnp.mean(dyg * ya, axis=-1, keepdims=True))
        dp_ref[:, 0:wa] = (dya * zv * _gelu_grad(au)).astype(BF16)
        dz_s[...] = dya * u
        for ch in range(n_chunks):
            dzs_ref[...] += dz_s[ch * CHUNK:(ch + 1) * CHUNK, :]
        for g in range(groups):
            wm = _causal(ws_ref[g]).astype(BF16)
            cols = slice(g * HEAD_DIM, (g + 1) * HEAD_DIM)
            for ch in range(n_chunks):
                rows = slice(ch * CHUNK, (ch + 1) * CHUNK)
                dzb = dz_s[rows, cols].astype(BF16)
                gws_ref[g] += _dot(dzb, vn_ref[rows, cols], tb=True)
                dvn_s[rows, cols] = _dot(wm, dzb, ta=True)
        av = av_ref[...]
        xh = (_gelu(av) - mu_ref[...]) * rs_ref[...]
        dvn = dvn_s[...]
        acc_lb[...] += _colsum8(dvn)
        acc_lg[...] += _colsum8(dvn * xh)
        dxh = dvn * lng_ref[...]
        dgv = rs_ref[...] * (dxh - jnp.mean(dxh, axis=-1, keepdims=True) - xh * jnp.mean(dxh * xh, axis=-1, keepdims=True))
        dp_ref[:, wa:2 * wa] = (dgv * _gelu_grad(av)).astype(BF16)

        @pl.when(i == n - 1)
        def _():
            gws_ref[...] = causal_stack(gws_ref[...])
            glg_ref[...] = jnp.sum(acc_lg[...], axis=0, keepdims=True)
            glb_ref[...] = jnp.sum(acc_lb[...], axis=0, keepdims=True)
            ggo_ref[...] = jnp.sum(acc_go[...], axis=0, keepdims=True)

    def col(j):
        return pl.BlockSpec((tm, wa), lambda i: (i, j))
    vec = pl.BlockSpec((1, wa), lambda i: (0, 0))
    stat = pl.BlockSpec((tm, 1), lambda i: (i, 0))
    ws_spec = pl.BlockSpec((groups, CHUNK, CHUNK), lambda i: (0, 0, 0))
    d_in = dproj.shape[1]
    return pl.pallas_call(
        body, name="gmlp_bwd", grid=(n,),
        in_specs=[ANY, col(0), col(0), col(1), col(2), col(0), col(0), stat, stat, stat, ws_spec, vec, vec],
        out_specs=(pl.BlockSpec((tm, 3 * wa), lambda i: (i, 0)), ws_spec,
                   pl.BlockSpec((CHUNK, wa), lambda i: (0, 0)), vec, vec, vec),
        out_shape=(jax.ShapeDtypeStruct((s, d_in), BF16),
                   jax.ShapeDtypeStruct((groups, CHUNK, CHUNK), F32), jax.ShapeDtypeStruct((CHUNK, wa), F32))
        + (jax.ShapeDtypeStruct((1, wa), F32),) * 3,
        scratch_shapes=[pltpu.VMEM((tm, wa), F32), pltpu.VMEM((tm, wa), F32)] + [pltpu.VMEM((8, wa), F32)] * 3,
        input_output_aliases={0: 0},
        compiler_params=_params("arbitrary"),
    )(dproj, dy, proj, proj, proj, z, vn, mu, rs, ra, w_s, ln_g, g_out)


def _gate_fwd(y_in, yb, proj, g_out, d_model):
    s = proj.shape[0]
    wa = d_model // 2
    tm = _tile(s, 256)

    def body(y_any, yb_ref, bz_ref, go_ref, y_ref, rb_ref):
        del y_any
        ybv = yb_ref[...]
        rb = lax.rsqrt(jnp.mean(ybv * ybv, axis=-1, keepdims=True) + EPS)
        rb_ref[...] = rb
        sz, _ = _silu_and_grad(bz_ref[...])
        y_ref[...] = (ybv * rb * go_ref[...] * sz).astype(BF16)

    return pl.pallas_call(
        body, name="gate_b_fwd", grid=(s // tm,),
        in_specs=[ANY, pl.BlockSpec((tm, wa), lambda i: (i, 0)), pl.BlockSpec((tm, wa), lambda i: (i, 6)),
                  pl.BlockSpec((1, wa), lambda i: (0, 0))],
        out_specs=(pl.BlockSpec((tm, wa), lambda i: (i, 1)), pl.BlockSpec((tm, 1), lambda i: (i, 0))),
        out_shape=(jax.ShapeDtypeStruct((s, d_model), BF16), jax.ShapeDtypeStruct((s, 1), F32)),
        input_output_aliases={0: 0},
        compiler_params=_params("parallel"),
    )(y_in, yb, proj, g_out)


def _gate_bwd(dy, yb, proj, rb, g_out, d_model):
    s, d_in = proj.shape
    wa = d_model // 2
    tm = _tile(s, 256)
    n = s // tm

    def body(dy_ref, yb_ref, bz_ref, rb_ref, go_ref, dp_ref, do_ref, ggo_ref, acc):
        i = pl.program_id(0)

        @pl.when(i == 0)
        def _():
            acc[...] = jnp.zeros_like(acc)

        dyv, ybv, rbv = dy_ref[...], yb_ref[...], rb_ref[...]
        sz, dsz = _silu_and_grad(bz_ref[...])
        dp_ref[...] = (dyv * (ybv * rbv * go_ref[...]) * dsz).astype(BF16)
        dn = dyv * sz
        acc[...] += _colsum8(dn * ybv * rbv)
        dyg = dn * go_ref[...]
        do_ref[...] = rbv * (dyg - ybv * (rbv * rbv) * jnp.mean(dyg * ybv, axis=-1, keepdims=True))

        @pl.when(i == n - 1)
        def _():
            ggo_ref[...] = jnp.sum(acc[...], axis=0, keepdims=True)

    vec = pl.BlockSpec((1, wa), lambda i: (0, 0))
    return pl.pallas_call(
        body, name="gate_b_bwd", grid=(n,),
        in_specs=[pl.BlockSpec((tm, wa), lambda i: (i, 1)), pl.BlockSpec((tm, wa), lambda i: (i, 0)),
                  pl.BlockSpec((tm, wa), lambda i: (i, 6)), pl.BlockSpec((tm, 1), lambda i: (i, 0)), vec],
        out_specs=(pl.BlockSpec((tm, wa), lambda i: (i, 6)), pl.BlockSpec((tm, wa), lambda i: (i, 0)), vec),
        out_shape=(jax.ShapeDtypeStruct((s, d_in), BF16), jax.ShapeDtypeStruct((s, wa), F32),
                   jax.ShapeDtypeStruct((1, wa), F32)),
        scratch_shapes=[pltpu.VMEM((8, wa), F32)],
        compiler_params=_params("arbitrary"),
    )(dy, yb, proj, rb, g_out)


def _bucket_tables():
    qi = BLK + np.arange(BLK)
    kj = np.arange(2 * BLK)
    delta = qi[:, None] - kj[None, :]
    max_exact = NUM_BUCKETS // 2
    tabs = []
    for window, dil in DILATED:
        band = (delta >= 0) & (delta <= window // dil)
        dist = np.clip(delta, 0, None) * dil
        d = np.maximum(dist, 1).astype(np.float32)
        large = max_exact + (np.log(d / np.float32(max_exact)) / np.float32(math.log(MAX_DISTANCE / max_exact))
                             * np.float32(NUM_BUCKETS - max_exact)).astype(np.int32)
        large = np.minimum(large, NUM_BUCKETS - 1)
        tabs.append(np.where(band, np.where(dist < max_exact, dist, large), -1).astype(np.int32))
    return np.stack(tabs)


def _bias_tiles(tab_ref, rb_ref, h, bias_s):
    col = lax.broadcasted_iota(jnp.int32, (BLK, 2 * BLK), 1)
    for i in range(len(DILATED)):
        t = tab_ref[i]
        bias = jnp.zeros(t.shape, F32)
        for b in range(NUM_BUCKETS):
            bias = jnp.where(t == b, rb_ref[b, h], bias)
        bias = jnp.where(t >= 0, bias, NEG_INF)
        bias_s[2 * i] = jnp.where(col >= BLK, bias, NEG_INF)
        bias_s[2 * i + 1] = bias


def _block_rows(b, n_blocks, dil):
    nb = n_blocks // dil
    r = b // nb
    n = b % nb
    start = r + dil * BLK * n
    if dil == 1:
        return pl.ds(pl.multiple_of(start, BLK), BLK), n
    return pl.ds(start, BLK, stride=dil), n


def _sub_block(b, rows=BLK, pad=0):
    return pl.ds(pl.multiple_of(b * BLK + pad, BLK), rows)


def _rms_rows(x, gain):
    return x * lax.rsqrt(jnp.mean(x * x, axis=-1, keepdims=True) + EPS) * gain


def _attn_fwd(proj, g_q, g_k, rel_bias, d_model, comm=None):
    s = proj.shape[0]
    heads = d_model // 2 // HEAD_DIM
    n_blocks = s // BLK
    scale = HEAD_DIM ** -0.5
    tables = jnp.asarray(_bucket_tables())
    c_ins = list(comm.ins) if comm else []
    c_outs = list(comm.out_shape) if comm else []
    c_scratch = list(comm.scratch) if comm else []

    def body(q_ref, k_ref, v_ref, gq_ref, gk_ref, rb_ref, tab_ref, *rest):
        cin_refs, rest = rest[:len(c_ins)], rest[len(c_ins):]
        yb_ref, lse_ref = rest[:2]
        cout_refs, rest = rest[2:2 + len(c_outs)], rest[2 + len(c_outs):]
        qs, ks, vs, m_s, l_s, acc_s, bias_s, sc_s, p_s, w_s = rest[:10]
        sems = rest[10:]
        h = pl.program_id(0)
        if comm:
            @pl.when(h == 0)
            def _():
                comm.start(cin_refs, cout_refs, sems)

        _bias_tiles(tab_ref, rb_ref, h, bias_s)
        gq = gq_ref[...] * scale
        gk = gk_ref[...]
        ks[0:BLK, :] = jnp.zeros((BLK, HEAD_DIM), BF16)
        vs[0:BLK, :] = jnp.zeros((BLK, HEAD_DIM), BF16)

        for i, (_, dil) in enumerate(DILATED):
            def prepare(b, carry, dil=dil):
                rows, _ = _block_rows(b, n_blocks, dil)
                qs[_sub_block(b), :] = _rms_rows(q_ref[rows, :], gq).astype(BF16)
                ks[_sub_block(b, pad=BLK), :] = _rms_rows(k_ref[rows, :], gk).astype(BF16)
                vs[_sub_block(b, pad=BLK), :] = v_ref[rows, :].astype(BF16)
                return carry
            lax.fori_loop(0, n_blocks, prepare, 0, unroll=2)

            def group(g, carry, i=i, dil=dil):
                blocks = [g * ATTN_GROUP + t for t in range(ATTN_GROUP)]
                for t, b in enumerate(blocks):
                    sc_s[t] = _dot(qs[_sub_block(b), :], ks[_sub_block(b, 2 * BLK), :], tb=True)
                for t, b in enumerate(blocks):
                    rows, n = _block_rows(b, n_blocks, dil)
                    sc = sc_s[t] + bias_s[2 * i + jnp.minimum(n, 1)]
                    m_i = jnp.max(sc, axis=-1, keepdims=True)
                    pr = jnp.exp(sc - m_i)
                    l_i = jnp.sum(pr, axis=-1, keepdims=True)
                    p_s[t] = pr.astype(BF16)
                    if i == 0:
                        m_s[rows, :] = m_i
                        l_s[rows, :] = l_i
                    else:
                        m_o = m_s[rows, :]
                        m_n = jnp.maximum(m_o, m_i)
                        c_o = jnp.exp(m_o - m_n)
                        c_i = jnp.exp(m_i - m_n)
                        m_s[rows, :] = m_n
                        l_s[rows, :] = l_s[rows, :] * c_o + l_i * c_i
                        w_s[2 * t] = c_o
                        w_s[2 * t + 1] = c_i
                for t, b in enumerate(blocks):
                    rows, _ = _block_rows(b, n_blocks, dil)
                    a_i = _dot(p_s[t], vs[_sub_block(b, 2 * BLK), :])
                    if i == 0:
                        acc_s[rows, :] = a_i
                    else:
                        acc_s[rows, :] = acc_s[rows, :] * w_s[2 * t] + a_i * w_s[2 * t + 1]
                return carry
            lax.fori_loop(0, n_blocks // ATTN_GROUP, group, 0)

        l = l_s[...]
        yb_ref[...] = acc_s[...] / l
        lse_ref[...] = m_s[...] + jnp.log(l)
        if comm:
            @pl.when(h == heads - 1)
            def _():
                comm.finish(cin_refs, cout_refs, sems)

    def head_col(off):
        return pl.BlockSpec((s, HEAD_DIM), lambda h: (0, off * heads + h))
    vec = pl.BlockSpec((1, HEAD_DIM), lambda h: (0, 0))
    res = pl.pallas_call(
        body, name="attn_fwd", grid=(heads,),
        in_specs=[head_col(3), head_col(4), head_col(5), vec, vec,
                  pl.BlockSpec(memory_space=pltpu.SMEM),
                  pl.BlockSpec((len(DILATED), BLK, 2 * BLK), lambda h: (0, 0, 0))] + [ANY] * len(c_ins),
        out_specs=[pl.BlockSpec((s, HEAD_DIM), lambda h: (0, h)), pl.BlockSpec((None, s, 1), lambda h: (h, 0, 0))]
        + [ANY] * len(c_outs),
        out_shape=[jax.ShapeDtypeStruct((s, heads * HEAD_DIM), F32), jax.ShapeDtypeStruct((heads, s, 1), F32)] + c_outs,
        scratch_shapes=[pltpu.VMEM((s, HEAD_DIM), BF16), pltpu.VMEM((s + BLK, HEAD_DIM), BF16),
                        pltpu.VMEM((s + BLK, HEAD_DIM), BF16),
                        pltpu.VMEM((s, 1), F32), pltpu.VMEM((s, 1), F32), pltpu.VMEM((s, HEAD_DIM), F32),
                        pltpu.VMEM((2 * len(DILATED), BLK, 2 * BLK), F32),
                        pltpu.VMEM((ATTN_GROUP, BLK, 2 * BLK), F32), pltpu.VMEM((ATTN_GROUP, BLK, 2 * BLK), BF16),
                        pltpu.VMEM((2 * ATTN_GROUP, BLK, 1), F32)] + c_scratch,
        input_output_aliases={7 + i: 2 + i for i in range(len(c_ins))} if (comm and comm.in_place) else {},
        compiler_params=_params("arbitrary"),
    )(proj, proj, proj, g_q, g_k, rel_bias, tables, *c_ins)
    return res[0], res[1], res[2:]


def _attn_bwd(dproj, d_o, yb, lse, proj, g_q, g_k, rel_bias, d_model):
    s, d_in = proj.shape
    heads = d_model // 2 // HEAD_DIM
    n_blocks = s // BLK
    scale = HEAD_DIM ** -0.5
    tables = jnp.asarray(_bucket_tables())
    n_dil = len(DILATED)

    def body(dp_any, q_ref, k_ref, v_ref, do_ref, o_ref, lse_ref, gq_ref, gk_ref, rb_ref, tab_ref,
             dp_out, ggq_ref, ggk_ref, db_ref,
             qs, ks, vs, dos, lse_s, del_s, dqn, dkn, dvv, dq_u, dk_u, dv_u, bias_s, dbias_s, sc_s, dp_s, p_s, ds_s,
             sems):
        del dp_any
        h = pl.program_id(0)

        @pl.when(h == 0)
        def _():
            ggq_ref[...] = jnp.zeros_like(ggq_ref)
            ggk_ref[...] = jnp.zeros_like(ggk_ref)

        dbias_s[...] = jnp.zeros_like(dbias_s)
        _bias_tiles(tab_ref, rb_ref, h, bias_s)
        gq = gq_ref[...] * scale
        gk = gk_ref[...]
        ks[0:BLK, :] = jnp.zeros((BLK, HEAD_DIM), BF16)
        vs[0:BLK, :] = jnp.zeros((BLK, HEAD_DIM), BF16)

        for i, (_, dil) in enumerate(DILATED):
            def prepare(b, carry, dil=dil):
                rows, _ = _block_rows(b, n_blocks, dil)
                qs[_sub_block(b), :] = _rms_rows(q_ref[rows, :], gq).astype(BF16)
                ks[_sub_block(b, pad=BLK), :] = _rms_rows(k_ref[rows, :], gk).astype(BF16)
                vs[_sub_block(b, pad=BLK), :] = v_ref[rows, :].astype(BF16)
                do = do_ref[rows, :]
                dos[_sub_block(b), :] = do.astype(BF16)
                del_s[_sub_block(b), :] = jnp.sum(do * o_ref[rows, :], axis=-1, keepdims=True)
                lse_s[_sub_block(b), :] = lse_ref[rows, :]
                return carry
            lax.fori_loop(0, n_blocks, prepare, 0, unroll=2)
            dk_u[...] = jnp.zeros_like(dk_u)
            dv_u[...] = jnp.zeros_like(dv_u)

            def group(g, carry, i=i, dil=dil):
                blocks = [g * ATTN_GROUP + t for t in range(ATTN_GROUP)]
                for t, b in enumerate(blocks):
                    sc_s[t] = _dot(qs[_sub_block(b), :], ks[_sub_block(b, 2 * BLK), :], tb=True)
                    dp_s[t] = _dot(dos[_sub_block(b), :], vs[_sub_block(b, 2 * BLK), :], tb=True)
                for t, b in enumerate(blocks):
                    _, n = _block_rows(b, n_blocks, dil)
                    pr = jnp.exp(sc_s[t] + bias_s[2 * i + jnp.minimum(n, 1)] - lse_s[_sub_block(b), :])
                    ds = pr * (dp_s[t] - del_s[_sub_block(b), :])
                    dbias_s[i] += ds
                    p_s[t] = pr.astype(BF16)
                    ds_s[t] = ds.astype(BF16)
                for t, b in enumerate(blocks):
                    dq_u[_sub_block(b), :] = _dot(ds_s[t], ks[_sub_block(b, 2 * BLK), :])
                    dk_u[_sub_block(b, 2 * BLK), :] += _dot(ds_s[t], qs[_sub_block(b), :], ta=True)
                    dv_u[_sub_block(b, 2 * BLK), :] += _dot(p_s[t], dos[_sub_block(b), :], ta=True)
                return carry
            lax.fori_loop(0, n_blocks // ATTN_GROUP, group, 0)

            def scatter(b, carry, i=i, dil=dil):
                rows, _ = _block_rows(b, n_blocks, dil)
                for acc, part, pad in ((dqn, dq_u, 0), (dkn, dk_u, BLK), (dvv, dv_u, BLK)):
                    val = part[_sub_block(b, pad=pad), :]
                    acc[rows, :] = val if i == 0 else acc[rows, :] + val
                return carry
            lax.fori_loop(0, n_blocks, scatter, 0, unroll=2)

        q = q_ref[...]
        rq = lax.rsqrt(jnp.mean(q * q, axis=-1, keepdims=True) + EPS)
        k = k_ref[...]
        rk = lax.rsqrt(jnp.mean(k * k, axis=-1, keepdims=True) + EPS)
        dq_n = dqn[...]
        ggq_ref[...] += jnp.sum(dq_n * (q * rq) * scale, axis=0, keepdims=True)
        dg = dq_n * gq_ref[...] * scale
        qs[...] = (rq * (dg - q * (rq * rq) * jnp.mean(dg * q, axis=-1, keepdims=True))).astype(BF16)
        dk_n = dkn[...]
        ggk_ref[...] += jnp.sum(dk_n * (k * rk), axis=0, keepdims=True)
        dg = dk_n * gk_ref[...]
        dos[...] = (rk * (dg - k * (rk * rk) * jnp.mean(dg * k, axis=-1, keepdims=True))).astype(BF16)
        vs[BLK:, :] = dvv[...].astype(BF16)
        copies = [pltpu.make_async_copy(src, dp_out.at[:, pl.ds(pl.multiple_of(((3 + j) * heads + h) * HEAD_DIM, HEAD_DIM), HEAD_DIM)],
                                        sems.at[j]) for j, src in enumerate((qs, dos, vs.at[pl.ds(BLK, s)]))]
        for cp in copies:
            cp.start()
        for b in range(NUM_BUCKETS):
            tot = jnp.zeros((BLK, 2 * BLK), F32)
            for i in range(n_dil):
                tot = tot + jnp.where(tab_ref[i] == b, dbias_s[i], 0.0)
            db_ref[b:b + 1, :] = jnp.full((1, LANE), jnp.sum(tot), F32)
        for cp in copies:
            cp.wait()

    def head_col(off):
        return pl.BlockSpec((s, HEAD_DIM), lambda h: (0, off * heads + h))
    own_col = pl.BlockSpec((s, HEAD_DIM), lambda h: (0, h))
    vec = pl.BlockSpec((1, HEAD_DIM), lambda h: (0, 0))
    big = pltpu.VMEM((s, HEAD_DIM), F32)
    padded32 = pltpu.VMEM((s + BLK, HEAD_DIM), F32)
    padded16 = pltpu.VMEM((s + BLK, HEAD_DIM), BF16)
    return pl.pallas_call(
        body, name="attn_bwd", grid=(heads,),
        in_specs=[ANY, head_col(3), head_col(4), head_col(5), own_col, own_col,
                  pl.BlockSpec((None, s, 1), lambda h: (h, 0, 0)), vec, vec,
                  pl.BlockSpec(memory_space=pltpu.SMEM),
                  pl.BlockSpec((n_dil, BLK, 2 * BLK), lambda h: (0, 0, 0))],
        out_specs=(ANY, vec, vec, pl.BlockSpec((None, NUM_BUCKETS, LANE), lambda h: (h, 0, 0))),
        out_shape=(jax.ShapeDtypeStruct((s, d_in), BF16), jax.ShapeDtypeStruct((1, HEAD_DIM), F32),
                   jax.ShapeDtypeStruct((1, HEAD_DIM), F32), jax.ShapeDtypeStruct((heads, NUM_BUCKETS, LANE), F32)),
        scratch_shapes=[pltpu.VMEM((s, HEAD_DIM), BF16), padded16, padded16, pltpu.VMEM((s, HEAD_DIM), BF16),
                        pltpu.VMEM((s, 1), F32), pltpu.VMEM((s, 1), F32),
                        big, big, big, big, padded32, padded32,
                        pltpu.VMEM((2 * n_dil, BLK, 2 * BLK), F32), pltpu.VMEM((n_dil, BLK, 2 * BLK), F32),
                        pltpu.VMEM((ATTN_GROUP, BLK, 2 * BLK), F32), pltpu.VMEM((ATTN_GROUP, BLK, 2 * BLK), F32),
                        pltpu.VMEM((ATTN_GROUP, BLK, 2 * BLK), BF16), pltpu.VMEM((ATTN_GROUP, BLK, 2 * BLK), BF16),
                        pltpu.SemaphoreType.DMA((3,))],
        input_output_aliases={0: 0},
        compiler_params=_params("arbitrary"),
    )(dproj, proj, proj, proj, d_o, yb, lse, g_q, g_k, rel_bias, tables)


def _local_step(xs, ps, tgt, sp, win4, wout4, wgate4, wup4, dist=None):
    s, d = xs.shape
    wa = d // 2
    heads = wa // HEAD_DIM
    d_in = 7 * wa
    c4 = d_in // N_CHIPS
    dple = ps.shape[1]

    tm, tn = _tile(s, 1024), _tile(d, 1024)
    tn_in = _tile(c4, 1024)
    per_in = c4 // tn_in
    tn_up = _tile(d // N_CHIPS, 512)
    per_up = (d // N_CHIPS) // tn_up
    gm, gn = s // tm, d // tn

    hn, rx = _rms_fwd("rms_pre", xs, sp["g_pre"])
    if dist:
        proj, win4 = _mm_in_gather(dist["chip"], hn, win4)
    else:
        proj = _matmul(
            "mm_in", hn, win4, pl.BlockSpec((tm, d), lambda i, j, k: (i, 0)), _shard_spec(d, tn_in, per_in, 2, 1),
            (gm, d_in // tn_in, 1), None, jax.ShapeDtypeStruct((s, d_in), F32),
            pl.BlockSpec((tm, tn_in), lambda i, j, k: (i, j)))
    b_st = sp["b_s"].T
    y, z, vn, mu_v, rs_v, ra = _gmlp_fwd(proj, sp["w_s"], b_st, sp["ln_v_g"], sp["ln_v_b"], sp["g_out_a"], d)
    yb, lse, gathered = _attn_fwd(proj, sp["g_q"], sp["g_k"], sp["rel_bias"], d,
                                  comm=_Gather([wout4, wgate4, wup4]) if dist else None)
    if dist:
        wout4, wgate4, wup4 = gathered
    wout, wgate = wout4.reshape(d, d), wgate4.reshape(d, d)
    y, rb = _gate_fwd(y, yb, proj, sp["g_out_b"], d)

    def residual(acc, extra, outs):
        outs[0][...] = extra[0][...] + acc

    tile = pl.BlockSpec((tm, tn), lambda i, j, k: (i, j))
    tile_up = pl.BlockSpec((tm, tn_up), lambda i, j, k: (i, j))
    h = _matmul(
        "mm_out", y, wout, pl.BlockSpec((tm, d), lambda i, j, k: (i, 0)), pl.BlockSpec((d, tn_up), lambda i, j, k: (0, j)),
        (gm, d // tn_up, 1), None, jax.ShapeDtypeStruct((s, d), F32), tile_up, extras=(xs,), extra_specs=(tile_up,),
        epilogue=residual)
    hp, rh = _rms_fwd("rms_ple", h, sp["g_ple"])
    p16 = ps.astype(BF16)

    def head(acc, extra, outs):
        p_ref, wup_ref, h_ref, t_ref = extra
        dout_ref, dgl_ref, dup_ref, loss_ref = outs
        up = _dot(p_ref[...], wup_ref[...])
        gate = jax.nn.sigmoid(acc)
        err = h_ref[...] + gate * up - t_ref[...]
        sq = _colsum8(err * err)
        part = sq[:, 0:LANE]
        for c in range(1, sq.shape[1] // LANE):
            part = part + sq[:, c * LANE:(c + 1) * LANE]
        loss_ref[...] = part
        dout = err * (1.0 / d)
        dout_ref[...] = dout
        dup_ref[...] = (dout * gate).astype(BF16)
        dgl_ref[...] = (dout * up * gate * (1.0 - gate)).astype(BF16)

    tile_up = pl.BlockSpec((tm, tn_up), lambda i, j, k: (i, j))
    dout, dglin, dup, loss_parts = _matmul(
        "mm_gate_loss", hp, wgate, pl.BlockSpec((tm, d), lambda i, j, k: (i, 0)), pl.BlockSpec((d, tn_up), lambda i, j, k: (0, j)),
        (gm, d // tn_up, 1), None,
        (jax.ShapeDtypeStruct((s, d), F32), jax.ShapeDtypeStruct((s, d), BF16), jax.ShapeDtypeStruct((s, d), BF16),
         jax.ShapeDtypeStruct((gm * 8, (d // tn_up) * LANE), F32)),
        (tile_up, tile_up, tile_up, pl.BlockSpec((8, LANE), lambda i, j, k: (i, j))),
        extras=(p16, wup4, h, tgt),
        extra_specs=(pl.BlockSpec((tm, dple), lambda i, j, k: (i, 0)), _shard_spec(dple, tn_up, per_up, 2, 1),
                     tile_up, tile_up),
        epilogue=head)
    loss = 0.5 * jnp.sum(loss_parts) * (1.0 / d)

    dhp = _matmul(
        "mm_dhp", dglin, wgate, pl.BlockSpec((tm, d), lambda i, j, k: (i, 0)), pl.BlockSpec((tn, d), lambda i, j, k: (j, 0)),
        (gm, gn, 1), None, jax.ShapeDtypeStruct((s, d), F32), tile, tb=True)
    dh, dh16, g_g_ple = _rms_bwd("rms_ple_bwd", dhp, h, rh, sp["g_ple"], dout, True)
    tmw = _tile(d, 1024)
    g_wgate = _matmul(
        "mm_gwgate", hp, dglin, pl.BlockSpec((s, tmw), lambda i, j, k: (0, i)), pl.BlockSpec((s, tn), lambda i, j, k: (0, j)),
        (d // tmw, gn, 1), None, jax.ShapeDtypeStruct((d, d), F32), pl.BlockSpec((tmw, tn), lambda i, j, k: (i, j)), ta=True)
    g_wup = _matmul(
        "mm_gwup", p16, dup, pl.BlockSpec((s, dple), lambda i, j, k: (0, 0)), pl.BlockSpec((s, tn_up), lambda i, j, k: (0, j)),
        (1, d // tn_up, 1), None, jax.ShapeDtypeStruct((N_CHIPS, dple, d // N_CHIPS), F32),
        _shard_spec(dple, tn_up, per_up, 0, 1), ta=True)
    dy = _matmul(
        "mm_dy", dh16, wout, pl.BlockSpec((tm, d), lambda i, j, k: (i, 0)), pl.BlockSpec((tn, d), lambda i, j, k: (j, 0)),
        (gm, gn, 1), None, jax.ShapeDtypeStruct((s, d), F32), tile, tb=True)
    g_wout = _matmul(
        "mm_gwout", y, dh16, pl.BlockSpec((s, tmw), lambda i, j, k: (0, i)), pl.BlockSpec((s, tn), lambda i, j, k: (0, j)),
        (d // tmw, gn, 1), None, jax.ShapeDtypeStruct((d, d), F32), pl.BlockSpec((tmw, tn), lambda i, j, k: (i, j)), ta=True)

    dproj, d_o, g_g_out_b = _gate_bwd(dy, yb, proj, rb, sp["g_out_b"], d)
    dproj, g_g_q, g_g_k, dbias = _attn_bwd(dproj, d_o, yb, lse, proj, sp["g_q"], sp["g_k"], sp["rel_bias"], d)
    dproj, g_w_s, dz_sum, g_ln_g, g_ln_b, g_g_out_a = _gmlp_bwd(
        dproj, dy, proj, z, vn, mu_v, rs_v, ra, sp["w_s"], sp["ln_v_g"], sp["g_out_a"], d)

    g_wout4 = g_wout.reshape(N_CHIPS, d // N_CHIPS, d)
    g_wgate4 = g_wgate.reshape(N_CHIPS, d // N_CHIPS, d)
    early = (g_wout4, g_wgate4, g_wup)
    if dist:
        early_sums = [_chip_sum("chip_sum_" + n, dist["core"], g, o)
                      for n, g, o in zip(LARGE[1:], early, _swap_halves("swap_early", list(early)))]
    g_win = _matmul(
        "mm_gwin", hn, dproj, pl.BlockSpec((s, tmw), lambda i, j, k: (0, i)), pl.BlockSpec((s, tn_in), lambda i, j, k: (0, j)),
        (d // tmw, d_in // tn_in, 1), None, jax.ShapeDtypeStruct((N_CHIPS, d, c4), F32),
        _shard_spec(tmw, tn_in, per_in, 0, 1), ta=True,
        comm=_ExchangeChips([s16 for _, s16 in early_sums]) if dist else None)
    if dist:
        g_win, early_got = g_win
        win_sum = _chip_sum("chip_sum_w_in", dist["core"], g_win, _swap_halves("swap_w_in", [g_win])[0])
    tk = c4
    tmh, tnh = _tile(s, 1024), _tile(d, 1024)
    dhn = _matmul(
        "mm_dhn", dproj, win4, pl.BlockSpec((tmh, tk), lambda i, j, k: (i, k)), _shard_spec(tnh, tk, 1, 1, 2),
        (s // tmh, d // tnh, d_in // tk), (tmh, tnh), jax.ShapeDtypeStruct((s, d), F32),
        pl.BlockSpec((tmh, tnh), lambda i, j, k: (i, j)), tb=True,
        comm=_ExchangeChips([win_sum[1]]) if dist else None)
    if dist:
        dhn, win_got = dhn
    grad_x, g_g_pre = _rms_bwd("rms_pre_bwd", dhn, xs, rx, sp["g_pre"], dh, False)

    small = {
        "g_pre": g_g_pre,
        "w_s": g_w_s,
        "b_s": jnp.sum(dz_sum.reshape(CHUNK, heads, HEAD_DIM), axis=-1).T,
        "ln_v_g": g_ln_g, "ln_v_b": g_ln_b,
        "g_q": g_g_q, "g_k": g_g_k,
        "rel_bias": dbias[:, :, 0].T,
        "g_out_a": g_g_out_a, "g_out_b": g_g_out_b,
        "g_ple": g_g_ple,
    }
    if not dist:
        return loss, grad_x, (g_win, *early), small
    sums32 = [win_sum[0]] + [s32 for s32, _ in early_sums]
    blocks = [_total("total_" + n, dist["chip"], dist["core"], s32, o)
              for n, s32, o in zip(LARGE, sums32, list(win_got) + list(early_got))]
    return loss, grad_x, _join_halves(blocks), small


def _place():
    x, y, c = lax.axis_index("x"), lax.axis_index("y"), lax.axis_index("c")
    chips = [(1 - x, y), (x, 1 - y), (1 - x, 1 - y)]
    return x, y, c, chips


def _remote(src, dst, send_sems, recv_sems, k, to):
    return pltpu.make_async_remote_copy(src_ref=src, dst_ref=dst, send_sem=send_sems.at[k], recv_sem=recv_sems.at[k],
                                        device_id=to, device_id_type=MESH)


def _cast_bf16(name, chip, w):
    r, c = w.shape
    tm = _tile(r, 256)

    def body(chip_ref, w_ref, o_ref):
        del chip_ref
        o_ref[...] = w_ref[...].astype(BF16)

    return pl.pallas_call(
        body, name=name,
        grid_spec=pltpu.PrefetchScalarGridSpec(
            num_scalar_prefetch=1, grid=(r // tm,),
            in_specs=[pl.BlockSpec((tm, c), lambda i, chip_ref: (i, 0))],
            out_specs=pl.BlockSpec((None, tm, c), lambda i, chip_ref: (chip_ref[0], i, 0))),
        out_shape=jax.ShapeDtypeStruct((N_CHIPS, r, c), BF16), compiler_params=_params("parallel"),
    )(chip, w)


class _Gather:
    in_place = True

    def __init__(self, fulls):
        self.ins = list(fulls)
        self.out_shape = [jax.ShapeDtypeStruct(f.shape, f.dtype) for f in fulls]
        n = len(fulls)
        self.scratch = [pltpu.SemaphoreType.DMA((6 * n,)), pltpu.SemaphoreType.DMA((6 * n,))]

    @staticmethod
    def _sends(outs, sems):
        send_sems, recv_sems = sems
        x, y, c, chips = _place()
        cps = []
        for w, ref in enumerate(outs):
            half = ref.shape[1] // 2
            blk = ref.at[2 * x + y, pl.ds(c * half, half)]
            cps += [_remote(blk, blk, send_sems, recv_sems, 6 * w + q, (*chip, c)) for q, chip in enumerate(chips)]
        return cps

    def start(self, ins, outs, sems):
        for cp in self._sends(outs, sems):
            cp.start()

    def finish(self, ins, outs, sems):
        send_sems, recv_sems = sems
        x, y, c, chips = _place()
        sibling = (x, y, 1 - c)
        forwards = []
        for w, ref in enumerate(outs):
            half = ref.shape[1] // 2
            for q, chip in enumerate(chips):
                blk = ref.at[2 * chip[0] + chip[1], pl.ds(c * half, half)]
                _remote(blk, blk, send_sems, recv_sems, 6 * w + q, sibling).wait_recv()
                fwd = _remote(blk, blk, send_sems, recv_sems, 6 * w + 3 + q, sibling)
                fwd.start()
                forwards.append(fwd)
        for w, ref in enumerate(outs):
            half = ref.shape[1] // 2
            for q, chip in enumerate(chips):
                blk = ref.at[2 * chip[0] + chip[1], pl.ds((1 - c) * half, half)]
                _remote(blk, blk, send_sems, recv_sems, 6 * w + 3 + q, sibling).wait_recv()
        for cp in self._sends(outs, sems) + forwards:
            cp.wait_send()


class _ExchangeChips:
    in_place = False

    def __init__(self, sums16):
        self.ins = list(sums16)
        self.out_shape = [jax.ShapeDtypeStruct(g.shape, g.dtype) for g in sums16]
        n = len(sums16)
        self.scratch = [pltpu.SemaphoreType.DMA((3 * n,)), pltpu.SemaphoreType.DMA((3 * n,))]

    @staticmethod
    def _sends(ins, outs, sems):
        send_sems, recv_sems = sems
        x, y, c, chips = _place()
        return [_remote(ins[w].at[2 * chip[0] + chip[1]], outs[w].at[2 * x + y], send_sems, recv_sems, 3 * w + q, (*chip, c))
                for w in range(len(ins)) for q, chip in enumerate(chips)]

    def start(self, ins, outs, sems):
        for cp in self._sends(ins, outs, sems):
            cp.start()

    def finish(self, ins, outs, sems):
        send_sems, recv_sems = sems
        x, y, c, chips = _place()
        for w in range(len(ins)):
            for q, chip in enumerate(chips):
                blk = outs[w].at[2 * chip[0] + chip[1]]
                _remote(blk, blk, send_sems, recv_sems, 3 * w + q, (*chip, c)).wait_recv()
        for cp in self._sends(ins, outs, sems):
            cp.wait_send()


def _mm_in_gather(chip, hn, win4):
    s, d = hn.shape
    c4 = win4.shape[2]
    tm, tn = _tile(s, 1024), _tile(c4, 1024)
    per = c4 // tn
    nj, gm = N_CHIPS * per, s // tm
    half = d // 2

    def block_of(j, chip_ref):
        o = j // per
        return chip_ref[0] ^ (((o & 1) << 1) | (o >> 1))

    i_late = max(gm - 2, 0)

    def body(chip_ref, hn_ref, w_any, proj_ref, w_ref, wbuf, wsem, send_sems, recv_sems):
        del w_any
        j, i = pl.program_id(0), pl.program_id(1)
        x, y, c, chips = _place()
        sibling = (x, y, 1 - c)

        def region(k, rows_half, t):
            return w_ref.at[k, pl.ds(rows_half * half, half), pl.ds(t * tn, tn)]

        def send_mine(q, t):
            blk = region(2 * x + y, c, t)
            return _remote(blk, blk, send_sems, recv_sems, q * per + t, (*chips[q], c))

        def forward(q, t):
            blk = region(2 * chips[q][0] + chips[q][1], c, t)
            return _remote(blk, blk, send_sems, recv_sems, (3 + q) * per + t, sibling)

        def fetch(jj, mine, slot):
            rows = pl.ds((c if mine else 1 - c) * half, half)
            cols = pl.ds(pl.multiple_of((jj % per) * tn, LANE), tn)
            return pltpu.make_async_copy(w_ref.at[block_of(jj, chip_ref), rows, cols], wbuf.at[slot, rows],
                                         wsem.at[2 * slot + (0 if mine else 1)])

        def foreign_tiles():
            return [(o, t) for o in range(1, N_CHIPS) for t in range(per)]

        @pl.when((j == 0) & (i == 0))
        def _():
            for t in range(per):
                send_mine(0, t).start()
                send_mine(1, t).start()
            fetch(0, True, 0).start()
            fetch(0, False, 0).start()

        nxt = j + 1

        @pl.when((i == 0) & (nxt < nj))
        def _():
            for o, t in foreign_tiles():
                @pl.when(nxt == o * per + t)
                def _():
                    blk = region(2 * chips[o - 1][0] + chips[o - 1][1], c, t)
                    _remote(blk, blk, send_sems, recv_sems, (o - 1) * per + t, sibling).wait_recv()
                    forward(o - 1, t).start()
                    if (o, t) == (1, per - 1):
                        for tt in range(per):
                            send_mine(2, tt).start()

            fetch(nxt, True, nxt % 2).start()

            @pl.when(nxt < per)
            def _():
                fetch(nxt, False, nxt % 2).start()

        @pl.when((i == i_late) & (nxt < nj))
        def _():
            for o, t in foreign_tiles():
                @pl.when(nxt == o * per + t)
                def _():
                    blk = region(2 * chips[o - 1][0] + chips[o - 1][1], 1 - c, t)
                    _remote(blk, blk, send_sems, recv_sems, (3 + o - 1) * per + t, sibling).wait_recv()
                    fetch(nxt, False, nxt % 2).start()

        @pl.when(i == 0)
        def _():
            fetch(j, True, j % 2).wait()
            fetch(j, False, j % 2).wait()

        proj_ref[...] = _dot(hn_ref[...], wbuf[j % 2])

        @pl.when((j == nj - 1) & (i == gm - 1))
        def _():
            for q in range(3):
                for t in range(per):
                    send_mine(q, t).wait_send()
                    forward(q, t).wait_send()

    return pl.pallas_call(
        body, name="mm_in_gather",
        grid_spec=pltpu.PrefetchScalarGridSpec(
            num_scalar_prefetch=1, grid=(nj, gm),
            in_specs=[pl.BlockSpec((tm, d), lambda j, i, chip_ref: (i, 0)), ANY],
            out_specs=[pl.BlockSpec((tm, tn), lambda j, i, chip_ref: (i, block_of(j, chip_ref) * per + j % per)), ANY],
            scratch_shapes=[pltpu.VMEM((2, d, tn), BF16), pltpu.SemaphoreType.DMA((4,)),
                            pltpu.SemaphoreType.DMA((6 * per,)), pltpu.SemaphoreType.DMA((6 * per,))]),
        out_shape=[jax.ShapeDtypeStruct((s, N_CHIPS * c4), F32), jax.ShapeDtypeStruct(win4.shape, win4.dtype)],
        input_output_aliases={2: 1},
        compiler_params=_params("arbitrary", "arbitrary"),
    )(chip, hn, win4)


def _swap_halves(name, grads):
    nw = len(grads)

    def body(*refs):
        ins, outs = refs[:nw], refs[nw:2 * nw]
        send_sems, recv_sems = refs[2 * nw:]
        x, y, c, _ = _place()
        copies = []
        for w in range(nw):
            half = ins[w].shape[1] // 2
            cp = _remote(ins[w].at[:, pl.ds((1 - c) * half, half)], outs[w], send_sems, recv_sems, w, (x, y, 1 - c))
            cp.start()
            copies.append(cp)
        for cp in copies:
            cp.wait()

    return pl.pallas_call(
        body, name=name,
        in_specs=[ANY] * nw, out_specs=[ANY] * nw,
        out_shape=[jax.ShapeDtypeStruct((N_CHIPS, g.shape[1] // 2, g.shape[2]), g.dtype) for g in grads],
        scratch_shapes=[pltpu.SemaphoreType.DMA((nw,)), pltpu.SemaphoreType.DMA((nw,))],
    )(*grads)


def _chip_sum(name, core, grad, got):
    _, r, c = grad.shape
    half = r // 2
    th = _tile(half, 128)
    n = half // th

    def body(core_ref, g_ref, o_ref, s32_ref, s16_ref):
        del core_ref
        v = g_ref[...] + o_ref[...]
        s32_ref[...] = v
        s16_ref[...] = v.astype(BF16)

    blk = pl.BlockSpec((None, th, c), lambda k, i, core_ref: (k, i, 0))
    return pl.pallas_call(
        body, name=name,
        grid_spec=pltpu.PrefetchScalarGridSpec(
            num_scalar_prefetch=1, grid=(N_CHIPS, n),
            in_specs=[pl.BlockSpec((None, th, c), lambda k, i, core_ref: (k, core_ref[0] * n + i, 0)), blk],
            out_specs=[blk, blk]),
        out_shape=[jax.ShapeDtypeStruct((N_CHIPS, half, c), F32), jax.ShapeDtypeStruct((N_CHIPS, half, c), BF16)],
        compiler_params=_params("parallel", "parallel"),
    )(core, grad, got)


def _total(name, chip, core, sum32, got16):
    _, half, c = sum32.shape
    th = _tile(half, 128)
    n = half // th

    def body(chip_ref, core_ref, own_ref, a_ref, b_ref, c_ref, o_ref):
        del chip_ref, core_ref
        o_ref[...] = ((own_ref[...] + a_ref[...].astype(F32)) + b_ref[...].astype(F32)) + c_ref[...].astype(F32)

    def other(step):
        return pl.BlockSpec((None, th, c), lambda i, chip_ref, core_ref: ((chip_ref[0] + step) % N_CHIPS, i, 0))

    return pl.pallas_call(
        body, name=name,
        grid_spec=pltpu.PrefetchScalarGridSpec(
            num_scalar_prefetch=2, grid=(n,),
            in_specs=[other(0), other(1), other(2), other(3)],
            out_specs=pl.BlockSpec((th, c), lambda i, chip_ref, core_ref: (core_ref[0] * n + i, 0))),
        out_shape=jax.ShapeDtypeStruct((2 * half, c), F32),
        compiler_params=_params("parallel"),
    )(chip, core, sum32, got16, got16, got16)


def _join_halves(blocks):
    nw = len(blocks)

    def body(*refs):
        outs = refs[nw:2 * nw]
        send_sems, recv_sems = refs[2 * nw:]
        x, y, c, _ = _place()
        copies = []
        for w in range(nw):
            half = outs[w].shape[0] // 2
            rows = outs[w].at[pl.ds(c * half, half)]
            cp = _remote(rows, rows, send_sems, recv_sems, w, (x, y, 1 - c))
            cp.start()
            copies.append(cp)
        for w, cp in enumerate(copies):
            cp.wait_send()
            half = outs[w].shape[0] // 2
            rows = outs[w].at[pl.ds((1 - c) * half, half)]
            _remote(rows, rows, send_sems, recv_sems, w, (x, y, 1 - c)).wait_recv()

    return pl.pallas_call(
        body, name="join_halves",
        in_specs=[ANY] * nw, out_specs=[ANY] * nw,
        out_shape=[jax.ShapeDtypeStruct(t.shape, t.dtype) for t in blocks],
        scratch_shapes=[pltpu.SemaphoreType.DMA((nw,)), pltpu.SemaphoreType.DMA((nw,))],
        input_output_aliases={i: i for i in range(nw)},
    )(*blocks)


def _gather_small(pack):
    m_per, n = pack.shape

    def body(x_ref, out_ref, send_sems, recv_sems, local_sem):
        x, y, c, chips = _place()
        me, sibling = (x, y, c), (x, y, 1 - c)

        def rows(px, py, pc):
            return out_ref.at[pl.ds((4 * px + 2 * py + pc) * m_per, m_per), :]

        def copy(k, block, to, src=None):
            return _remote(rows(*block) if src is None else src, rows(*block), send_sems, recv_sems, k, to)

        mine = pltpu.make_async_copy(x_ref, rows(*me), local_sem)
        mine.start()
        first = [copy(0, me, sibling, src=x_ref)]
        first += [copy(1 + j, me, (*chip, c), src=x_ref) for j, chip in enumerate(chips)]
        for cp in first:
            cp.start()
        passed = [copy(4 + j, (*chip, c), sibling) for j, chip in enumerate(chips)]
        for j, chip in enumerate(chips):
            copy(1 + j, (*chip, c), me).wait_recv()
            passed[j].start()
        copy(0, sibling, me).wait_recv()
        for j, chip in enumerate(chips):
            copy(4 + j, (*chip, 1 - c), me).wait_recv()
        for cp in first + passed:
            cp.wait_send()
        mine.wait()

    return pl.pallas_call(
        body, name="gather_small",
        out_shape=jax.ShapeDtypeStruct((N_DEV * m_per, n), pack.dtype),
        in_specs=[pl.BlockSpec(memory_space=pltpu.VMEM)],
        out_specs=pl.BlockSpec(memory_space=pltpu.VMEM),
        scratch_shapes=[pltpu.SemaphoreType.DMA((7,)), pltpu.SemaphoreType.DMA((7,)), pltpu.SemaphoreType.DMA],
        compiler_params=_params(),
    )(pack)


def _adamw_math(w, g, m, v):
    m = ADAM_B1 * m + (1.0 - ADAM_B1) * g
    v = ADAM_B2 * v + (1.0 - ADAM_B2) * (g * g)
    m_hat = m / (1.0 - ADAM_B1 ** ADAM_STEP)
    v_hat = v / (1.0 - ADAM_B2 ** ADAM_STEP)
    delta = -ADAM_LR * (m_hat / (jnp.sqrt(v_hat) + ADAM_EPS) + ADAM_WD * w)
    return delta, m, v


def _adamw(name, w, g, m, v):
    r, c = w.shape
    tm = _tile(r, 128)

    def body(w_ref, g_ref, m_ref, v_ref, d_out, m_out, v_out):
        d_out[...], m_out[...], v_out[...] = _adamw_math(w_ref[...], g_ref[...], m_ref[...], v_ref[...])

    spec = pl.BlockSpec((tm, c), lambda i: (i, 0))
    return pl.pallas_call(
        body, name=name, grid=(r // tm,), in_specs=[spec] * 4, out_specs=[spec] * 3,
        out_shape=[jax.ShapeDtypeStruct((r, c), F32)] * 3, compiler_params=_params("parallel"),
    )(w, g, m, v)


def _adamw_small(gathered, w, m, v):
    rows = w.shape[0]

    def body(all_ref, w_ref, m_ref, v_ref, g_out, d_out, m_out, v_out):
        g = all_ref[0:rows, :]
        for dev in range(1, N_DEV):
            g = g + all_ref[dev * rows:(dev + 1) * rows, :]
        g_out[...] = g
        d_out[...], m_out[...], v_out[...] = _adamw_math(w_ref[...], g, m_ref[...], v_ref[...])

    return pl.pallas_call(
        body, name="adamw_small", out_shape=[jax.ShapeDtypeStruct(w.shape, F32)] * 4, compiler_params=_params(),
    )(gathered, w, m, v)


SMALL = ("g_pre", "w_s", "b_s", "ln_v_g", "ln_v_b", "g_q", "g_k", "rel_bias", "g_out_a", "g_out_b", "g_ple")
LARGE = ("w_in", "w_out", "w_ple_gate", "w_ple_up")
WEIGHTS = ("g_pre", "w_in", "w_s", "b_s", "ln_v_g", "ln_v_b", "g_q", "g_k", "rel_bias", "g_out_a", "g_out_b", "w_out",
           "g_ple", "w_ple_gate", "w_ple_up")


def _pack(parts):
    flat = jnp.concatenate([parts[n].reshape(-1).astype(F32) for n in SMALL])
    rows = -(-flat.shape[0] // (8 * LANE)) * 8
    return jnp.pad(flat, (0, rows * LANE - flat.shape[0])).reshape(rows, LANE)


def _unpack(pack, like):
    flat = pack.reshape(-1)
    out, at = {}, 0
    for n in SMALL:
        size = math.prod(like[n].shape)
        out[n] = flat[at:at + size].reshape(like[n].shape)
        at += size
    return out


def kernel(x, p, g_pre, w_in, w_s, b_s, ln_v_g, ln_v_b, g_q, g_k, rel_bias, g_out_a, g_out_b, w_out, g_ple, w_ple_gate, w_ple_up, loss_target, m_g_pre, m_w_in, m_w_s, m_b_s, m_ln_v_g, m_ln_v_b, m_g_q, m_g_k, m_rel_bias, m_g_out_a, m_g_out_b, m_w_out, m_g_ple, m_w_ple_gate, m_w_ple_up, v_g_pre, v_w_in, v_w_s, v_b_s, v_ln_v_g, v_ln_v_b, v_g_q, v_g_k, v_rel_bias, v_g_out_a, v_g_out_b, v_w_out, v_g_ple, v_w_ple_gate, v_w_ple_up):
    given = dict(locals())
    weights = {n: given[n] for n in WEIGHTS}
    mom_m = {n: given["m_" + n] for n in WEIGHTS}
    mom_v = {n: given["v_" + n] for n in WEIGHTS}
    xs, ps, tgt = x[0], p[0, 0], loss_target[0]
    d = xs.shape[1]

    core = lax.axis_index("c").astype(jnp.int32).reshape(1)
    chip = (2 * lax.axis_index("x") + lax.axis_index("y")).astype(jnp.int32).reshape(1)

    win4, wout4, wgate4, wup4 = [_cast_bf16("cast_" + n, chip, weights[n][0]) for n in LARGE]

    sp = {
        "g_pre": g_pre, "w_s": w_s[0], "b_s": b_s[0], "ln_v_g": ln_v_g, "ln_v_b": ln_v_b, "g_q": g_q, "g_k": g_k,
        "rel_bias": rel_bias, "g_out_a": g_out_a, "g_out_b": g_out_b, "g_ple": g_ple,
    }
    loss_local, grad_x, blocks, small = _local_step(xs, ps, tgt, sp, win4, wout4, wgate4, wup4,
                                                    dist={"chip": chip, "core": core})
    loss = lax.psum(loss_local, MESH_AXES)
    grads = dict(zip(LARGE, blocks))

    out_g, out_d, out_m, out_v = {}, {}, {}, {}
    for n in LARGE:
        out_g[n] = grads[n][None]
        dl, mn, vn = _adamw("adamw_" + n, weights[n][0], grads[n], mom_m[n][0], mom_v[n][0])
        out_d[n], out_m[n], out_v[n] = dl[None], mn[None], vn[None]

    gathered = _gather_small(_pack(small))
    pg, pd, pm, pv = _adamw_small(gathered, _pack(weights), _pack(mom_m), _pack(mom_v))
    for packed, out in ((pg, out_g), (pd, out_d), (pm, out_m), (pv, out_v)):
        out.update(_unpack(packed, weights))

    return (loss, grad_x[None], *[out_g[n] for n in WEIGHTS], *[out_d[n] for n in WEIGHTS],
            *[out_m[n] for n in WEIGHTS], *[out_v[n] for n in WEIGHTS])
```

```python
import functools
import math

import numpy as np

import jax
import jax.numpy as jnp
from jax import lax
from jax.experimental import pallas as pl
from jax.experimental.pallas import tpu as pltpu

F32 = jnp.float32
BF16 = jnp.bfloat16

HEAD_DIM = 128
CHUNK = 128
BLK = 128
DILATED = ((128, 1), (512, 4), (2048, 16))
NUM_BUCKETS = 32
MAX_DISTANCE = 2048
ATTN_GROUP = 4
EPS = 1e-6
NEG_INF = -1e30
N_CHIPS = 4
N_DEV = 8
MESH_AXES = ("x", "y", "c")

ADAM_LR = 0.001
ADAM_B1 = 0.9
ADAM_B2 = 0.999
ADAM_EPS = 1e-08
ADAM_WD = 0.01
ADAM_STEP = 10

V7X_VMEM_LIMIT = 56 * 1024 * 1024
LANE = 128
MESH = pl.DeviceIdType.MESH
ANY = pl.BlockSpec(memory_space=pl.ANY)


def _params(*sem):
    return pltpu.CompilerParams(dimension_semantics=sem or None, vmem_limit_bytes=V7X_VMEM_LIMIT)


def _tile(n, target):
    if n <= target:
        return n
    t = (target // LANE) * LANE
    while t > LANE and n % t:
        t -= LANE
    assert n % t == 0, (n, target)
    return t


def _gelu(x):
    return 0.5 * x * (1.0 + lax.erf(x * (1.0 / math.sqrt(2.0))))


def _gelu_grad(x):
    return 0.5 * (1.0 + lax.erf(x * (1.0 / math.sqrt(2.0)))) + x * jnp.exp(-0.5 * x * x) * (1.0 / math.sqrt(2.0 * math.pi))


def _silu_and_grad(x):
    s = jax.nn.sigmoid(x)
    return x * s, s * (1.0 + x * (1.0 - s))


def _dot(a, b, ta=False, tb=False):
    return lax.dot_general(a, b, (((0 if ta else 1,), (1 if tb else 0,)), ((), ())), preferred_element_type=F32)


def _colsum8(v):
    return jnp.sum(v.reshape(v.shape[0] // 8, 8, v.shape[1]), axis=0)


def _matmul(name, a, b, a_spec, b_spec, grid, acc_shape, out_shape, out_specs, ta=False, tb=False,
            extras=(), extra_specs=(), epilogue=None, comm=None, prefetch=()):
    nk = grid[2]
    n_pre = len(prefetch)
    n_extra = len(extras)
    single = not isinstance(out_shape, (tuple, list))
    outs_shape = (out_shape,) if single else tuple(out_shape)
    outs_specs = (out_specs,) if single else tuple(out_specs)
    n_out = len(outs_shape)
    c_ins = list(comm.ins) if comm else []
    c_outs = list(comm.out_shape) if comm else []
    c_scratch = list(comm.scratch) if comm else []
    n_cin, n_cout = len(c_ins), len(c_outs)

    def finish(acc, extra_refs, out_refs):
        if epilogue is None:
            out_refs[0][...] = acc.astype(out_refs[0].dtype)
        else:
            epilogue(acc, extra_refs, out_refs)

    def body(*refs):
        a_ref, b_ref, *rest = refs[n_pre:]
        extra_refs = rest[:n_extra]
        cin_refs = rest[n_extra:n_extra + n_cin]
        out_refs = rest[n_extra + n_cin:n_extra + n_cin + n_out]
        cout_refs = rest[n_extra + n_cin + n_out:n_extra + n_cin + n_out + n_cout]
        scratch_refs = rest[n_extra + n_cin + n_out + n_cout:]
        ids = [pl.program_id(ax) for ax in range(3)]
        if comm:
            sems = scratch_refs[len(scratch_refs) - len(c_scratch):]

            @pl.when((ids[0] == 0) & (ids[1] == 0) & (ids[2] == 0))
            def _():
                comm.start(cin_refs, cout_refs, sems)

        if nk == 1:
            finish(_dot(a_ref[...], b_ref[...], ta, tb), extra_refs, out_refs)
        else:
            acc_ref = scratch_refs[0]

            @pl.when(ids[2] == 0)
            def _():
                acc_ref[...] = jnp.zeros_like(acc_ref)

            acc_ref[...] += _dot(a_ref[...], b_ref[...], ta, tb)

            @pl.when(ids[2] == nk - 1)
            def _():
                finish(acc_ref[...], extra_refs, out_refs)

        if comm:
            @pl.when((ids[0] == grid[0] - 1) & (ids[1] == grid[1] - 1) & (ids[2] == nk - 1))
            def _():
                comm.finish(cin_refs, cout_refs, sems)

    scratch = ([] if nk == 1 else [pltpu.VMEM(acc_shape, F32)]) + c_scratch
    aliases = {n_pre + 2 + n_extra + i: n_out + i for i in range(n_cin)} if (comm and comm.in_place) else {}
    res = pl.pallas_call(
        body, name=name,
        grid_spec=pltpu.PrefetchScalarGridSpec(
            num_scalar_prefetch=n_pre, grid=grid,
            in_specs=[a_spec, b_spec, *extra_specs] + [ANY] * n_cin,
            out_specs=list(outs_specs) + [ANY] * n_cout, scratch_shapes=scratch),
        out_shape=list(outs_shape) + c_outs,
        input_output_aliases=aliases,
        compiler_params=_params(*(("arbitrary",) * 3 if comm else ("parallel", "parallel", "arbitrary"))),
    )(*prefetch, a, b, *extras, *c_ins)
    if comm:
        main = res[:n_out]
        return (main[0] if single else main), res[n_out:]
    return res[0] if single else res


def _shard_spec(rows, cols, per_shard, row_axis, col_axis):
    def index(i, j, k):
        g = (i, j, k)
        return (g[col_axis] // per_shard, g[row_axis], g[col_axis] % per_shard)
    return pl.BlockSpec((None, rows, cols), index)


def _rms_fwd(name, x, g):
    s, d = x.shape
    tm = _tile(s, 256)

    def body(x_ref, g_ref, y_ref, r_ref):
        xf = x_ref[...]
        r = lax.rsqrt(jnp.mean(xf * xf, axis=-1, keepdims=True) + EPS)
        y_ref[...] = (xf * r * g_ref[...]).astype(BF16)
        r_ref[...] = r

    return pl.pallas_call(
        body, name=name, grid=(s // tm,),
        in_specs=[pl.BlockSpec((tm, d), lambda i: (i, 0)), pl.BlockSpec((1, d), lambda i: (0, 0))],
        out_specs=(pl.BlockSpec((tm, d), lambda i: (i, 0)), pl.BlockSpec((tm, 1), lambda i: (i, 0))),
        out_shape=(jax.ShapeDtypeStruct((s, d), BF16), jax.ShapeDtypeStruct((s, 1), F32)),
        compiler_params=_params("parallel"),
    )(x, g)


def _rms_bwd(name, dy, x, r, g, skip, with_bf16):
    s, d = x.shape
    tm = _tile(s, 256)
    n = s // tm

    def body(dy_ref, x_ref, r_ref, g_ref, skip_ref, *outs):
        dx_ref = outs[0]
        dg_ref = outs[-2]
        acc_ref = outs[-1]
        i = pl.program_id(0)
        dyv, xv, rv = dy_ref[...], x_ref[...], r_ref[...]

        @pl.when(i == 0)
        def _():
            acc_ref[...] = jnp.zeros_like(acc_ref)

        acc_ref[...] += _colsum8(dyv * xv * rv)
        dg = dyv * g_ref[...]
        dx = skip_ref[...] + rv * (dg - xv * (rv * rv) * jnp.mean(dg * xv, axis=-1, keepdims=True))
        dx_ref[...] = dx
        if with_bf16:
            outs[1][...] = dx.astype(BF16)

        @pl.when(i == n - 1)
        def _():
            dg_ref[...] = jnp.sum(acc_ref[...], axis=0, keepdims=True)

    row = pl.BlockSpec((tm, d), lambda i: (i, 0))
    vec = pl.BlockSpec((1, d), lambda i: (0, 0))
    out_specs = [row] + ([row] if with_bf16 else []) + [vec]
    out_shape = [jax.ShapeDtypeStruct((s, d), F32)] + ([jax.ShapeDtypeStruct((s, d), BF16)] if with_bf16 else []) \
        + [jax.ShapeDtypeStruct((1, d), F32)]
    return pl.pallas_call(
        body, name=name, grid=(n,),
        in_specs=[row, row, pl.BlockSpec((tm, 1), lambda i: (i, 0)), vec, row],
        out_specs=out_specs, out_shape=out_shape, scratch_shapes=[pltpu.VMEM((8, d), F32)],
        compiler_params=_params("arbitrary"),
    )(dy, x, r, g, skip)


def _causal(w):
    t = lax.broadcasted_iota(jnp.int32, w.shape, 0)
    s_ = lax.broadcasted_iota(jnp.int32, w.shape, 1)
    return jnp.where(t >= s_, w, 0.0)


def _gmlp_fwd(proj, w_s, b_st, ln_g, ln_b, g_out, d_model):
    s = proj.shape[0]
    wa = d_model // 2
    groups = wa // HEAD_DIM
    tm = _tile(s, 256)
    n_chunks = tm // CHUNK

    def body(au_ref, av_ref, az_ref, ws_ref, bst_ref, lng_ref, lnb_ref, go_ref,
             y_ref, z_ref, vn_ref, mu_ref, rs_ref, ra_ref):
        gv = _gelu(av_ref[...])
        mu = jnp.mean(gv, axis=-1, keepdims=True)
        xc = gv - mu
        rs = lax.rsqrt(jnp.mean(xc * xc, axis=-1, keepdims=True) + EPS)
        vn = (xc * rs * lng_ref[...] + lnb_ref[...]).astype(BF16)
        vn_ref[...] = vn
        mu_ref[...] = mu
        rs_ref[...] = rs
        for g in range(groups):
            wm = _causal(ws_ref[g]).astype(BF16)
            cols = slice(g * HEAD_DIM, (g + 1) * HEAD_DIM)
            for ch in range(n_chunks):
                rows = slice(ch * CHUNK, (ch + 1) * CHUNK)
                z_ref[rows, cols] = _dot(wm, vn_ref[rows, cols]) + bst_ref[:, g:g + 1]
        ya = _gelu(au_ref[...]) * z_ref[...]
        ra = lax.rsqrt(jnp.mean(ya * ya, axis=-1, keepdims=True) + EPS)
        ra_ref[...] = ra
        sz, _ = _silu_and_grad(az_ref[...])
        y_ref[...] = (ya * ra * go_ref[...] * sz).astype(BF16)

    def col(j):
        return pl.BlockSpec((tm, wa), lambda i: (i, j))
    vec = pl.BlockSpec((1, wa), lambda i: (0, 0))
    stat = pl.BlockSpec((tm, 1), lambda i: (i, 0))
    return pl.pallas_call(
        body, name="gmlp_fwd", grid=(s // tm,),
        in_specs=[col(0), col(1), col(2),
                  pl.BlockSpec((groups, CHUNK, CHUNK), lambda i: (0, 0, 0)),
                  pl.BlockSpec((CHUNK, groups), lambda i: (0, 0)), vec, vec, vec],
        out_specs=(col(0), col(0), col(0), stat, stat, stat),
        out_shape=(jax.ShapeDtypeStruct((s, d_model), BF16), jax.ShapeDtypeStruct((s, wa), F32),
                   jax.ShapeDtypeStruct((s, wa), BF16), jax.ShapeDtypeStruct((s, 1), F32),
                   jax.ShapeDtypeStruct((s, 1), F32), jax.ShapeDtypeStruct((s, 1), F32)),
        compiler_params=_params("parallel"),
    )(proj, proj, proj, w_s, b_st, ln_g, ln_b, g_out)


def _gmlp_bwd(dproj, dy, proj, z, vn, mu, rs, ra, w_s, ln_g, g_out, d_model):
    s = proj.shape[0]
    wa = d_model // 2
    groups = wa // HEAD_DIM
    tm = _tile(s, 256)
    n_chunks = tm // CHUNK
    n = s // tm

    def causal_stack(w):
        t = lax.broadcasted_iota(jnp.int32, w.shape, 1)
        s_ = lax.broadcasted_iota(jnp.int32, w.shape, 2)
        return jnp.where(t >= s_, w, 0.0)

    def body(dproj_in, dy_ref, au_ref, av_ref, az_ref, z_ref, vn_ref, mu_ref, rs_ref, ra_ref, ws_ref, lng_ref, go_ref,
             dp_ref, gws_ref, dzs_ref, glg_ref, glb_ref, ggo_ref,
             dz_s, dvn_s, acc_lg, acc_lb, acc_go):
        del dproj_in
        i = pl.program_id(0)

        @pl.when(i == 0)
        def _():
            gws_ref[...] = jnp.zeros_like(gws_ref)
            dzs_ref[...] = jnp.zeros_like(dzs_ref)
            acc_lg[...] = jnp.zeros_like(acc_lg)
            acc_lb[...] = jnp.zeros_like(acc_lb)
            acc_go[...] = jnp.zeros_like(acc_go)

        au, az, zv, rav = au_ref[...], az_ref[...], z_ref[...], ra_ref[...]
        u = _gelu(au)
        ya = u * zv
        sz, dsz = _silu_and_grad(az)
        dyv = dy_ref[...]
        dp_ref[:, 2 * wa:3 * wa] = (dyv * (ya * rav * go_ref[...]) * dsz).astype(BF16)
        dn = dyv * sz
        acc_go[...] += _colsum8(dn * ya * rav)
        dyg = dn * go_ref[...]
        dya = rav * (dyg - ya * (rav * rav) * jnp.mean(dyg * ya, axis=-1, keepdims=True))
        dp_ref[:, 0:wa] = (dya * zv * _gelu_grad(au)).astype(BF16)
        dz_s[...] = dya * u
        for ch in range(n_chunks):
            dzs_ref[...] += dz_s[ch * CHUNK:(ch + 1) * CHUNK, :]
        for g in range(groups):
            wm = _causal(ws_ref[g]).astype(BF16)
            cols = slice(g * HEAD_DIM, (g + 1) * HEAD_DIM)
            for ch in range(n_chunks):
                rows = slice(ch * CHUNK, (ch + 1) * CHUNK)
                dzb = dz_s[rows, cols].astype(BF16)
                gws_ref[g] += _dot(dzb, vn_ref[rows, cols], tb=True)
                dvn_s[rows, cols] = _dot(wm, dzb, ta=True)
        av = av_ref[...]
        xh = (_gelu(av) - mu_ref[...]) * rs_ref[...]
        dvn = dvn_s[...]
        acc_lb[...] += _colsum8(dvn)
        acc_lg[...] += _colsum8(dvn * xh)
        dxh = dvn * lng_ref[...]
        dgv = rs_ref[...] * (dxh - jnp.mean(dxh, axis=-1, keepdims=True) - xh * jnp.mean(dxh * xh, axis=-1, keepdims=True))
        dp_ref[:, wa:2 * wa] = (dgv * _gelu_grad(av)).astype(BF16)

        @pl.when(i == n - 1)
        def _():
            gws_ref[...] = causal_stack(gws_ref[...])
            glg_ref[...] = jnp.sum(acc_lg[...], axis=0, keepdims=True)
            glb_ref[...] = jnp.sum(acc_lb[...], axis=0, keepdims=True)
            ggo_ref[...] = jnp.sum(acc_go[...], axis=0, keepdims=True)

    def col(j):
        return pl.BlockSpec((tm, wa), lambda i: (i, j))
    vec = pl.BlockSpec((1, wa), lambda i: (0, 0))
    stat = pl.BlockSpec((tm, 1), lambda i: (i, 0))
    ws_spec = pl.BlockSpec((groups, CHUNK, CHUNK), lambda i: (0, 0, 0))
    d_in = dproj.shape[1]
    return pl.pallas_call(
        body, name="gmlp_bwd", grid=(n,),
        in_specs=[ANY, col(0), col(0), col(1), col(2), col(0), col(0), stat, stat, stat, ws_spec, vec, vec],
        out_specs=(pl.BlockSpec((tm, 3 * wa), lambda i: (i, 0)), ws_spec,
                   pl.BlockSpec((CHUNK, wa), lambda i: (0, 0)), vec, vec, vec),
        out_shape=(jax.ShapeDtypeStruct((s, d_in), BF16),
                   jax.ShapeDtypeStruct((groups, CHUNK, CHUNK), F32), jax.ShapeDtypeStruct((CHUNK, wa), F32))
        + (jax.ShapeDtypeStruct((1, wa), F32),) * 3,
        scratch_shapes=[pltpu.VMEM((tm, wa), F32), pltpu.VMEM((tm, wa), F32)] + [pltpu.VMEM((8, wa), F32)] * 3,
        input_output_aliases={0: 0},
        compiler_params=_params("arbitrary"),
    )(dproj, dy, proj, proj, proj, z, vn, mu, rs, ra, w_s, ln_g, g_out)


def _gate_fwd(y_in, yb, proj, g_out, d_model):
    s = proj.shape[0]
    wa = d_model // 2
    tm = _tile(s, 256)

    def body(y_any, yb_ref, bz_ref, go_ref, y_ref, rb_ref):
        del y_any
        ybv = yb_ref[...]
        rb = lax.rsqrt(jnp.mean(ybv * ybv, axis=-1, keepdims=True) + EPS)
        rb_ref[...] = rb
        sz, _ = _silu_and_grad(bz_ref[...])
        y_ref[...] = (ybv * rb * go_ref[...] * sz).astype(BF16)

    return pl.pallas_call(
        body, name="gate_b_fwd", grid=(s // tm,),
        in_specs=[ANY, pl.BlockSpec((tm, wa), lambda i: (i, 0)), pl.BlockSpec((tm, wa), lambda i: (i, 6)),
                  pl.BlockSpec((1, wa), lambda i: (0, 0))],
        out_specs=(pl.BlockSpec((tm, wa), lambda i: (i, 1)), pl.BlockSpec((tm, 1), lambda i: (i, 0))),
        out_shape=(jax.ShapeDtypeStruct((s, d_model), BF16), jax.ShapeDtypeStruct((s, 1), F32)),
        input_output_aliases={0: 0},
        compiler_params=_params("parallel"),
    )(y_in, yb, proj, g_out)


def _gate_bwd(dy, yb, proj, rb, g_out, d_model):
    s, d_in = proj.shape
    wa = d_model // 2
    tm = _tile(s, 256)
    n = s // tm

    def body(dy_ref, yb_ref, bz_ref, rb_ref, go_ref, dp_ref, do_ref, ggo_ref, acc):
        i = pl.program_id(0)

        @pl.when(i == 0)
        def _():
            acc[...] = jnp.zeros_like(acc)

        dyv, ybv, rbv = dy_ref[...], yb_ref[...], rb_ref[...]
        sz, dsz = _silu_and_grad(bz_ref[...])
        dp_ref[...] = (dyv * (ybv * rbv * go_ref[...]) * dsz).astype(BF16)
        dn = dyv * sz
        acc[...] += _colsum8(dn * ybv * rbv)
        dyg = dn * go_ref[...]
        do_ref[...] = rbv * (dyg - ybv * (rbv * rbv) * jnp.mean(dyg * ybv, axis=-1, keepdims=True))

        @pl.when(i == n - 1)
        def _():
            ggo_ref[...] = jnp.sum(acc[...], axis=0, keepdims=True)

    vec = pl.BlockSpec((1, wa), lambda i: (0, 0))
    return pl.pallas_call(
        body, name="gate_b_bwd", grid=(n,),
        in_specs=[pl.BlockSpec((tm, wa), lambda i: (i, 1)), pl.BlockSpec((tm, wa), lambda i: (i, 0)),
                  pl.BlockSpec((tm, wa), lambda i: (i, 6)), pl.BlockSpec((tm, 1), lambda i: (i, 0)), vec],
        out_specs=(pl.BlockSpec((tm, wa), lambda i: (i, 6)), pl.BlockSpec((tm, wa), lambda i: (i, 0)), vec),
        out_shape=(jax.ShapeDtypeStruct((s, d_in), BF16), jax.ShapeDtypeStruct((s, wa), F32),
                   jax.ShapeDtypeStruct((1, wa), F32)),
        scratch_shapes=[pltpu.VMEM((8, wa), F32)],
        compiler_params=_params("arbitrary"),
    )(dy, yb, proj, rb, g_out)


def _bucket_tables():
    qi = BLK + np.arange(BLK)
    kj = np.arange(2 * BLK)
    delta = qi[:, None] - kj[None, :]
    max_exact = NUM_BUCKETS // 2
    tabs = []
    for window, dil in DILATED:
        band = (delta >= 0) & (delta <= window // dil)
        dist = np.clip(delta, 0, None) * dil
        d = np.maximum(dist, 1).astype(np.float32)
        large = max_exact + (np.log(d / np.float32(max_exact)) / np.float32(math.log(MAX_DISTANCE / max_exact))
                             * np.float32(NUM_BUCKETS - max_exact)).astype(np.int32)
        large = np.minimum(large, NUM_BUCKETS - 1)
        tabs.append(np.where(band, np.where(dist < max_exact, dist, large), -1).astype(np.int32))
    return np.stack(tabs)


def _bias_tiles(tab_ref, rb_ref, h, bias_s):
    col = lax.broadcasted_iota(jnp.int32, (BLK, 2 * BLK), 1)
    for i in range(len(DILATED)):
        t = tab_ref[i]
        bias = jnp.zeros(t.shape, F32)
        for b in range(NUM_BUCKETS):
            bias = jnp.where(t == b, rb_ref[b, h], bias)
        bias = jnp.where(t >= 0, bias, NEG_INF)
        bias_s[2 * i] = jnp.where(col >= BLK, bias, NEG_INF)
        bias_s[2 * i + 1] = bias


def _block_rows(b, n_blocks, dil):
    nb = n_blocks // dil
    r = b // nb
    n = b % nb
    start = r + dil * BLK * n
    if dil == 1:
        return pl.ds(pl.multiple_of(start, BLK), BLK), n
    return pl.ds(start, BLK, stride=dil), n


def _sub_block(b, rows=BLK, pad=0):
    return pl.ds(pl.multiple_of(b * BLK + pad, BLK), rows)


def _rms_rows(x, gain):
    return x * lax.rsqrt(jnp.mean(x * x, axis=-1, keepdims=True) + EPS) * gain


def _attn_fwd(proj, g_q, g_k, rel_bias, d_model, comm=None):
    s = proj.shape[0]
    heads = d_model // 2 // HEAD_DIM
    n_blocks = s // BLK
    scale = HEAD_DIM ** -0.5
    tables = jnp.asarray(_bucket_tables())
    c_ins = list(comm.ins) if comm else []
    c_outs = list(comm.out_shape) if comm else []
    c_scratch = list(comm.scratch) if comm else []

    def body(q_ref, k_ref, v_ref, gq_ref, gk_ref, rb_ref, tab_ref, *rest):
        cin_refs, rest = rest[:len(c_ins)], rest[len(c_ins):]
        yb_ref, lse_ref = rest[:2]
        cout_refs, rest = rest[2:2 + len(c_outs)], rest[2 + len(c_outs):]
        qs, ks, vs, m_s, l_s, acc_s, bias_s, sc_s, p_s = rest[:9]
        sems = rest[9:]
        h = pl.program_id(0)
        if comm:
            @pl.when(h == 0)
            def _():
                comm.start(cin_refs, cout_refs, sems)

        _bias_tiles(tab_ref, rb_ref, h, bias_s)
        gq = gq_ref[...] * scale
        gk = gk_ref[...]
        ks[0:BLK, :] = jnp.zeros((BLK, HEAD_DIM), BF16)
        vs[0:BLK, :] = jnp.zeros((BLK, HEAD_DIM), BF16)

        for i, (_, dil) in enumerate(DILATED):
            def prepare(b, carry, dil=dil):
                rows, _ = _block_rows(b, n_blocks, dil)
                qs[_sub_block(b), :] = _rms_rows(q_ref[rows, :], gq).astype(BF16)
                ks[_sub_block(b, pad=BLK), :] = _rms_rows(k_ref[rows, :], gk).astype(BF16)
                vs[_sub_block(b, pad=BLK), :] = v_ref[rows, :].astype(BF16)
                return carry
            lax.fori_loop(0, n_blocks, prepare, 0, unroll=8)

            def group(g, carry, i=i, dil=dil):
                blocks = [g * ATTN_GROUP + t for t in range(ATTN_GROUP)]
                for t, b in enumerate(blocks):
                    sc_s[t] = _dot(qs[_sub_block(b), :], ks[_sub_block(b, 2 * BLK), :], tb=True)
                for t, b in enumerate(blocks):
                    rows, n = _block_rows(b, n_blocks, dil)
                    sc = sc_s[t] + bias_s[2 * i + jnp.minimum(n, 1)]
                    m_i = jnp.max(sc, axis=-1, keepdims=True)
                    pr = jnp.exp(sc - m_i)
                    l_i = jnp.sum(pr, axis=-1, keepdims=True)
                    p_s[t] = pr.astype(BF16)
                    m_s[i, rows, :] = m_i
                    l_s[i, rows, :] = l_i
                for t, b in enumerate(blocks):
                    rows, _ = _block_rows(b, n_blocks, dil)
                    acc_s[i, rows, :] = _dot(p_s[t], vs[_sub_block(b, 2 * BLK), :])
                return carry
            lax.fori_loop(0, n_blocks // ATTN_GROUP, group, 0)

        m = jnp.maximum(jnp.maximum(m_s[0], m_s[1]), m_s[2])
        l = jnp.zeros_like(m)
        acc = jnp.zeros((s, HEAD_DIM), F32)
        for i in range(len(DILATED)):
            w = jnp.exp(m_s[i] - m)
            l = l + l_s[i] * w
            acc = acc + acc_s[i] * w
        yb_ref[...] = acc / l
        lse_ref[...] = m + jnp.log(l)
        if comm:
            @pl.when(h == heads - 1)
            def _():
                comm.finish(cin_refs, cout_refs, sems)

    def head_col(off):
        return pl.BlockSpec((s, HEAD_DIM), lambda h: (0, off * heads + h))
    vec = pl.BlockSpec((1, HEAD_DIM), lambda h: (0, 0))
    res = pl.pallas_call(
        body, name="attn_fwd", grid=(heads,),
        in_specs=[head_col(3), head_col(4), head_col(5), vec, vec,
                  pl.BlockSpec(memory_space=pltpu.SMEM),
                  pl.BlockSpec((len(DILATED), BLK, 2 * BLK), lambda h: (0, 0, 0))] + [ANY] * len(c_ins),
        out_specs=[pl.BlockSpec((s, HEAD_DIM), lambda h: (0, h)), pl.BlockSpec((None, s, 1), lambda h: (h, 0, 0))]
        + [ANY] * len(c_outs),
        out_shape=[jax.ShapeDtypeStruct((s, heads * HEAD_DIM), F32), jax.ShapeDtypeStruct((heads, s, 1), F32)] + c_outs,
        scratch_shapes=[pltpu.VMEM((s, HEAD_DIM), BF16), pltpu.VMEM((s + BLK, HEAD_DIM), BF16),
                        pltpu.VMEM((s + BLK, HEAD_DIM), BF16),
                        pltpu.VMEM((len(DILATED), s, 1), F32), pltpu.VMEM((len(DILATED), s, 1), F32),
                        pltpu.VMEM((len(DILATED), s, HEAD_DIM), F32),
                        pltpu.VMEM((2 * len(DILATED), BLK, 2 * BLK), F32),
                        pltpu.VMEM((ATTN_GROUP, BLK, 2 * BLK), F32), pltpu.VMEM((ATTN_GROUP, BLK, 2 * BLK), BF16)] + c_scratch,
        input_output_aliases={7 + i: 2 + i for i in range(len(c_ins))} if (comm and comm.in_place) else {},
        compiler_params=_params("arbitrary"),
    )(proj, proj, proj, g_q, g_k, rel_bias, tables, *c_ins)
    return res[0], res[1], res[2:]


def _attn_bwd(dproj, d_o, yb, lse, proj, g_q, g_k, rel_bias, d_model, comm=None):
    s, d_in = proj.shape
    heads = d_model // 2 // HEAD_DIM
    n_blocks = s // BLK
    scale = HEAD_DIM ** -0.5
    tables = jnp.asarray(_bucket_tables())
    n_dil = len(DILATED)
    c_ins = list(comm.ins) if comm else []
    c_outs = list(comm.out_shape) if comm else []
    c_scratch = list(comm.scratch) if comm else []

    def body(dp_any, q_ref, k_ref, v_ref, do_ref, o_ref, lse_ref, gq_ref, gk_ref, rb_ref, tab_ref, *rest):
        cin_refs, rest = rest[:len(c_ins)], rest[len(c_ins):]
        dp_out, ggq_ref, ggk_ref, db_ref = rest[:4]
        cout_refs, rest = rest[4:4 + len(c_outs)], rest[4 + len(c_outs):]
        (qs, ks, vs, dos, lse_s, del_s, dqn, dkn, dvv, dq_u, dk_u, dv_u, bias_s, dbias_s, sc_s, dp_s, p_s, ds_s,
         sems) = rest[:19]
        comm_sems = rest[19:]
        del dp_any
        h = pl.program_id(0)
        if comm:
            @pl.when(h == 0)
            def _():
                comm.start(cin_refs, cout_refs, comm_sems)


        @pl.when(h == 0)
        def _():
            ggq_ref[...] = jnp.zeros_like(ggq_ref)
            ggk_ref[...] = jnp.zeros_like(ggk_ref)

        dbias_s[...] = jnp.zeros_like(dbias_s)
        _bias_tiles(tab_ref, rb_ref, h, bias_s)
        gq = gq_ref[...] * scale
        gk = gk_ref[...]
        ks[0:BLK, :] = jnp.zeros((BLK, HEAD_DIM), BF16)
        vs[0:BLK, :] = jnp.zeros((BLK, HEAD_DIM), BF16)

        for i, (_, dil) in enumerate(DILATED):
            def prepare(b, carry, dil=dil):
                rows, _ = _block_rows(b, n_blocks, dil)
                qs[_sub_block(b), :] = _rms_rows(q_ref[rows, :], gq).astype(BF16)
                ks[_sub_block(b, pad=BLK), :] = _rms_rows(k_ref[rows, :], gk).astype(BF16)
                vs[_sub_block(b, pad=BLK), :] = v_ref[rows, :].astype(BF16)
                do = do_ref[rows, :]
                dos[_sub_block(b), :] = do.astype(BF16)
                del_s[_sub_block(b), :] = jnp.sum(do * o_ref[rows, :], axis=-1, keepdims=True)
                lse_s[_sub_block(b), :] = lse_ref[rows, :]
                return carry
            lax.fori_loop(0, n_blocks, prepare, 0, unroll=8)
            dk_u[...] = jnp.zeros_like(dk_u)
            dv_u[...] = jnp.zeros_like(dv_u)

            def group(g, carry, i=i, dil=dil):
                blocks = [g * ATTN_GROUP + t for t in range(ATTN_GROUP)]
                for t, b in enumerate(blocks):
                    sc_s[t] = _dot(qs[_sub_block(b), :], ks[_sub_block(b, 2 * BLK), :], tb=True)
                    dp_s[t] = _dot(dos[_sub_block(b), :], vs[_sub_block(b, 2 * BLK), :], tb=True)
                for t, b in enumerate(blocks):
                    _, n = _block_rows(b, n_blocks, dil)
                    pr = jnp.exp(sc_s[t] + bias_s[2 * i + jnp.minimum(n, 1)] - lse_s[_sub_block(b), :])
                    ds = pr * (dp_s[t] - del_s[_sub_block(b), :])
                    dbias_s[i] += ds
                    p_s[t] = pr.astype(BF16)
                    ds_s[t] = ds.astype(BF16)
                for t, b in enumerate(blocks):
                    dq_u[_sub_block(b), :] = _dot(ds_s[t], ks[_sub_block(b, 2 * BLK), :])
                    dk_u[_sub_block(b, 2 * BLK), :] += _dot(ds_s[t], qs[_sub_block(b), :], ta=True)
                    dv_u[_sub_block(b, 2 * BLK), :] += _dot(p_s[t], dos[_sub_block(b), :], ta=True)
                return carry
            lax.fori_loop(0, n_blocks // ATTN_GROUP, group, 0)

            def scatter(b, carry, i=i, dil=dil):
                rows, _ = _block_rows(b, n_blocks, dil)
                for acc, part, pad in ((dqn, dq_u, 0), (dkn, dk_u, BLK), (dvv, dv_u, BLK)):
                    val = part[_sub_block(b, pad=pad), :]
                    acc[rows, :] = val if i == 0 else acc[rows, :] + val
                return carry
            lax.fori_loop(0, n_blocks, scatter, 0, unroll=8)

        q = q_ref[...]
        rq = lax.rsqrt(jnp.mean(q * q, axis=-1, keepdims=True) + EPS)
        k = k_ref[...]
        rk = lax.rsqrt(jnp.mean(k * k, axis=-1, keepdims=True) + EPS)
        dq_n = dqn[...]
        ggq_ref[...] += jnp.sum(dq_n * (q * rq) * scale, axis=0, keepdims=True)
        dg = dq_n * gq_ref[...] * scale
        qs[...] = (rq * (dg - q * (rq * rq) * jnp.mean(dg * q, axis=-1, keepdims=True))).astype(BF16)
        dk_n = dkn[...]
        ggk_ref[...] += jnp.sum(dk_n * (k * rk), axis=0, keepdims=True)
        dg = dk_n * gk_ref[...]
        dos[...] = (rk * (dg - k * (rk * rk) * jnp.mean(dg * k, axis=-1, keepdims=True))).astype(BF16)
        vs[BLK:, :] = dvv[...].astype(BF16)
        copies = [pltpu.make_async_copy(src, dp_out.at[:, pl.ds(pl.multiple_of(((3 + j) * heads + h) * HEAD_DIM, HEAD_DIM), HEAD_DIM)],
                                        sems.at[j]) for j, src in enumerate((qs, dos, vs.at[pl.ds(BLK, s)]))]
        for cp in copies:
            cp.start()
        for b in range(NUM_BUCKETS):
            tot = jnp.zeros((BLK, 2 * BLK), F32)
            for i in range(n_dil):
                tot = tot + jnp.where(tab_ref[i] == b, dbias_s[i], 0.0)
            db_ref[b:b + 1, :] = jnp.full((1, LANE), jnp.sum(tot), F32)
        for cp in copies:
            cp.wait()
        if comm:
            @pl.when(h == heads - 1)
            def _():
                comm.finish(cin_refs, cout_refs, comm_sems)

    def head_col(off):
        return pl.BlockSpec((s, HEAD_DIM), lambda h: (0, off * heads + h))
    own_col = pl.BlockSpec((s, HEAD_DIM), lambda h: (0, h))
    vec = pl.BlockSpec((1, HEAD_DIM), lambda h: (0, 0))
    big = pltpu.VMEM((s, HEAD_DIM), F32)
    padded32 = pltpu.VMEM((s + BLK, HEAD_DIM), F32)
    padded16 = pltpu.VMEM((s + BLK, HEAD_DIM), BF16)
    res = pl.pallas_call(
        body, name="attn_bwd", grid=(heads,),
        in_specs=[ANY, head_col(3), head_col(4), head_col(5), own_col, own_col,
                  pl.BlockSpec((None, s, 1), lambda h: (h, 0, 0)), vec, vec,
                  pl.BlockSpec(memory_space=pltpu.SMEM),
                  pl.BlockSpec((n_dil, BLK, 2 * BLK), lambda h: (0, 0, 0))] + [ANY] * len(c_ins),
        out_specs=[ANY, vec, vec, pl.BlockSpec((None, NUM_BUCKETS, LANE), lambda h: (h, 0, 0))] + [ANY] * len(c_outs),
        out_shape=[jax.ShapeDtypeStruct((s, d_in), BF16), jax.ShapeDtypeStruct((1, HEAD_DIM), F32),
                   jax.ShapeDtypeStruct((1, HEAD_DIM), F32), jax.ShapeDtypeStruct((heads, NUM_BUCKETS, LANE), F32)]
        + c_outs,
        scratch_shapes=[pltpu.VMEM((s, HEAD_DIM), BF16), padded16, padded16, pltpu.VMEM((s, HEAD_DIM), BF16),
                        pltpu.VMEM((s, 1), F32), pltpu.VMEM((s, 1), F32),
                        big, big, big, big, padded32, padded32,
                        pltpu.VMEM((2 * n_dil, BLK, 2 * BLK), F32), pltpu.VMEM((n_dil, BLK, 2 * BLK), F32),
                        pltpu.VMEM((ATTN_GROUP, BLK, 2 * BLK), F32), pltpu.VMEM((ATTN_GROUP, BLK, 2 * BLK), F32),
                        pltpu.VMEM((ATTN_GROUP, BLK, 2 * BLK), BF16), pltpu.VMEM((ATTN_GROUP, BLK, 2 * BLK), BF16),
                        pltpu.SemaphoreType.DMA((3,))] + c_scratch,
        input_output_aliases={0: 0},
        compiler_params=_params("arbitrary"),
    )(dproj, proj, proj, proj, d_o, yb, lse, g_q, g_k, rel_bias, tables, *c_ins)
    return res[0], res[1], res[2], res[3], res[4:]


def _local_step(xs, ps, tgt, sp, win4, wout4, wgate4, wup4, dist=None):
    s, d = xs.shape
    wa = d // 2
    heads = wa // HEAD_DIM
    d_in = 7 * wa
    c4 = d_in // N_CHIPS
    dple = ps.shape[1]

    tm, tn = _tile(s, 1024), _tile(d, 1024)
    tn_in = _tile(c4, 1024)
    per_in = c4 // tn_in
    tn_up = _tile(d // N_CHIPS, 512)
    per_up = (d // N_CHIPS) // tn_up
    gm, gn = s // tm, d // tn

    hn, rx = _rms_fwd("rms_pre", xs, sp["g_pre"])
    if dist:
        proj, win4 = _mm_in_gather(dist["chip"], hn, win4)
    else:
        proj = _matmul(
            "mm_in", hn, win4, pl.BlockSpec((tm, d), lambda i, j, k: (i, 0)), _shard_spec(d, tn_in, per_in, 2, 1),
            (gm, d_in // tn_in, 1), None, jax.ShapeDtypeStruct((s, d_in), F32),
            pl.BlockSpec((tm, tn_in), lambda i, j, k: (i, j)))
    b_st = sp["b_s"].T
    y, z, vn, mu_v, rs_v, ra = _gmlp_fwd(proj, sp["w_s"], b_st, sp["ln_v_g"], sp["ln_v_b"], sp["g_out_a"], d)
    yb, lse, gathered = _attn_fwd(proj, sp["g_q"], sp["g_k"], sp["rel_bias"], d,
                                  comm=_Gather([wout4, wgate4, wup4]) if dist else None)
    if dist:
        wout4, wgate4, wup4 = gathered
    wout, wgate = wout4.reshape(d, d), wgate4.reshape(d, d)
    y, rb = _gate_fwd(y, yb, proj, sp["g_out_b"], d)

    def residual(acc, extra, outs):
        outs[0][...] = extra[0][...] + acc

    tile = pl.BlockSpec((tm, tn), lambda i, j, k: (i, j))
    tile_up = pl.BlockSpec((tm, tn_up), lambda i, j, k: (i, j))
    h = _matmul(
        "mm_out", y, wout, pl.BlockSpec((tm, d), lambda i, j, k: (i, 0)), pl.BlockSpec((d, tn_up), lambda i, j, k: (0, j)),
        (gm, d // tn_up, 1), None, jax.ShapeDtypeStruct((s, d), F32), tile_up, extras=(xs,), extra_specs=(tile_up,),
        epilogue=residual)
    hp, rh = _rms_fwd("rms_ple", h, sp["g_ple"])
    p16 = ps.astype(BF16)

    def head(acc, extra, outs):
        p_ref, wup_ref, h_ref, t_ref = extra
        dout_ref, dgl_ref, dup_ref, loss_ref = outs
        up = _dot(p_ref[...], wup_ref[...])
        gate = jax.nn.sigmoid(acc)
        err = h_ref[...] + gate * up - t_ref[...]
        sq = _colsum8(err * err)
        part = sq[:, 0:LANE]
        for c in range(1, sq.shape[1] // LANE):
            part = part + sq[:, c * LANE:(c + 1) * LANE]
        loss_ref[...] = part
        dout = err * (1.0 / d)
        dout_ref[...] = dout
        dup_ref[...] = (dout * gate).astype(BF16)
        dgl_ref[...] = (dout * up * gate * (1.0 - gate)).astype(BF16)

    tile_up = pl.BlockSpec((tm, tn_up), lambda i, j, k: (i, j))
    dout, dglin, dup, loss_parts = _matmul(
        "mm_gate_loss", hp, wgate, pl.BlockSpec((tm, d), lambda i, j, k: (i, 0)), pl.BlockSpec((d, tn_up), lambda i, j, k: (0, j)),
        (gm, d // tn_up, 1), None,
        (jax.ShapeDtypeStruct((s, d), F32), jax.ShapeDtypeStruct((s, d), BF16), jax.ShapeDtypeStruct((s, d), BF16),
         jax.ShapeDtypeStruct((gm * 8, (d // tn_up) * LANE), F32)),
        (tile_up, tile_up, tile_up, pl.BlockSpec((8, LANE), lambda i, j, k: (i, j))),
        extras=(p16, wup4, h, tgt),
        extra_specs=(pl.BlockSpec((tm, dple), lambda i, j, k: (i, 0)), _shard_spec(dple, tn_up, per_up, 2, 1),
                     tile_up, tile_up),
        epilogue=head)
    loss = 0.5 * jnp.sum(loss_parts) * (1.0 / d)

    dhp = _matmul(
        "mm_dhp", dglin, wgate, pl.BlockSpec((tm, d), lambda i, j, k: (i, 0)), pl.BlockSpec((tn, d), lambda i, j, k: (j, 0)),
        (gm, gn, 1), None, jax.ShapeDtypeStruct((s, d), F32), tile, tb=True)
    dh, dh16, g_g_ple = _rms_bwd("rms_ple_bwd", dhp, h, rh, sp["g_ple"], dout, True)
    tmw = _tile(d, 1024)
    g_wgate = _matmul(
        "mm_gwgate", hp, dglin, pl.BlockSpec((s, tmw), lambda i, j, k: (0, i)), pl.BlockSpec((s, tn), lambda i, j, k: (0, j)),
        (d // tmw, gn, 1), None, jax.ShapeDtypeStruct((d, d), F32), pl.BlockSpec((tmw, tn), lambda i, j, k: (i, j)), ta=True)
    g_wup = _matmul(
        "mm_gwup", p16, dup, pl.BlockSpec((s, dple), lambda i, j, k: (0, 0)), pl.BlockSpec((s, tn_up), lambda i, j, k: (0, j)),
        (1, d // tn_up, 1), None, jax.ShapeDtypeStruct((N_CHIPS, dple, d // N_CHIPS), F32),
        _shard_spec(dple, tn_up, per_up, 0, 1), ta=True)
    dy = _matmul(
        "mm_dy", dh16, wout, pl.BlockSpec((tm, d), lambda i, j, k: (i, 0)), pl.BlockSpec((tn, d), lambda i, j, k: (j, 0)),
        (gm, gn, 1), None, jax.ShapeDtypeStruct((s, d), F32), tile, tb=True)
    g_wout = _matmul(
        "mm_gwout", y, dh16, pl.BlockSpec((s, tmw), lambda i, j, k: (0, i)), pl.BlockSpec((s, tn), lambda i, j, k: (0, j)),
        (d // tmw, gn, 1), None, jax.ShapeDtypeStruct((d, d), F32), pl.BlockSpec((tmw, tn), lambda i, j, k: (i, j)), ta=True)

    g_wout4 = g_wout.reshape(N_CHIPS, d // N_CHIPS, d)
    g_wgate4 = g_wgate.reshape(N_CHIPS, d // N_CHIPS, d)
    early = (g_wout4, g_wgate4, g_wup)
    dproj, d_o, g_g_out_b = _gate_bwd(dy, yb, proj, rb, sp["g_out_b"], d)
    dproj, g_g_q, g_g_k, dbias, early_swapped = _attn_bwd(
        dproj, d_o, yb, lse, proj, sp["g_q"], sp["g_k"], sp["rel_bias"], d, comm=_SwapHalves(early) if dist else None)
    dproj, g_w_s, dz_sum, g_ln_g, g_ln_b, g_g_out_a = _gmlp_bwd(
        dproj, dy, proj, z, vn, mu_v, rs_v, ra, sp["w_s"], sp["ln_v_g"], sp["g_out_a"], d)

    if dist:
        core = dist["core"]
        early_sums = [_chip_sum("chip_sum_" + n, core, g, o) for n, g, o in zip(LARGE[1:], early, early_swapped)]
        tmw_half = _tile(d // 2, 1024)
        n_half = (d // 2) // tmw_half

        def half_of_gwin(name, own, comm):
            def rows(i, j, k, core_ref):
                return (0, (core_ref[0] if own else 1 - core_ref[0]) * n_half + i)
            return _matmul(
                name, hn, dproj, pl.BlockSpec((s, tmw_half), rows),
                pl.BlockSpec((s, tn_in), lambda i, j, k, core_ref: (0, j)),
                (n_half, d_in // tn_in, 1), None, jax.ShapeDtypeStruct((N_CHIPS, d // 2, c4), F32),
                pl.BlockSpec((None, tmw_half, tn_in), lambda i, j, k, core_ref: (j // per_in, i, j % per_in)), ta=True,
                comm=comm, prefetch=(core,))

        g_theirs, early_got = half_of_gwin("mm_gwin_theirs", False, _ExchangeChips([s16 for _, s16 in early_sums]))
        g_mine, (g_from_sibling,) = half_of_gwin("mm_gwin_mine", True, _SwapHalves([g_theirs], whole=True))
        win_sum = _chip_sum("chip_sum_w_in", core, g_mine, g_from_sibling)
    else:
        g_win = _matmul(
            "mm_gwin", hn, dproj, pl.BlockSpec((s, tmw), lambda i, j, k: (0, i)),
            pl.BlockSpec((s, tn_in), lambda i, j, k: (0, j)),
            (d // tmw, d_in // tn_in, 1), None, jax.ShapeDtypeStruct((N_CHIPS, d, c4), F32),
            _shard_spec(tmw, tn_in, per_in, 0, 1), ta=True)
    tk = c4
    tmh, tnh = _tile(s, 1024), _tile(d, 1024)
    dhn = _matmul(
        "mm_dhn", dproj, win4, pl.BlockSpec((tmh, tk), lambda i, j, k: (i, k)), _shard_spec(tnh, tk, 1, 1, 2),
        (s // tmh, d // tnh, d_in // tk), (tmh, tnh), jax.ShapeDtypeStruct((s, d), F32),
        pl.BlockSpec((tmh, tnh), lambda i, j, k: (i, j)), tb=True,
        comm=_ExchangeChips([win_sum[1]]) if dist else None)
    if dist:
        dhn, win_got = dhn
    grad_x, g_g_pre = _rms_bwd("rms_pre_bwd", dhn, xs, rx, sp["g_pre"], dh, False)

    small = {
        "g_pre": g_g_pre,
        "w_s": g_w_s,
        "b_s": jnp.sum(dz_sum.reshape(CHUNK, heads, HEAD_DIM), axis=-1).T,
        "ln_v_g": g_ln_g, "ln_v_b": g_ln_b,
        "g_q": g_g_q, "g_k": g_g_k,
        "rel_bias": dbias[:, :, 0].T,
        "g_out_a": g_g_out_a, "g_out_b": g_g_out_b,
        "g_ple": g_g_ple,
    }
    if not dist:
        return loss, grad_x, (g_win, *early), small
    sums32 = [win_sum[0]] + [s32 for s32, _ in early_sums]
    blocks = [_total("total_" + n, dist["chip"], dist["core"], s32, o)
              for n, s32, o in zip(LARGE, sums32, list(win_got) + list(early_got))]
    return loss, grad_x, _join_halves(blocks), small


def _place():
    x, y, c = lax.axis_index("x"), lax.axis_index("y"), lax.axis_index("c")
    chips = [(1 - x, y), (x, 1 - y), (1 - x, 1 - y)]
    return x, y, c, chips


def _remote(src, dst, send_sems, recv_sems, k, to):
    return pltpu.make_async_remote_copy(src_ref=src, dst_ref=dst, send_sem=send_sems.at[k], recv_sem=recv_sems.at[k],
                                        device_id=to, device_id_type=MESH)


def _cast_bf16(name, chip, w):
    r, c = w.shape
    tm = _tile(r, 256)

    def body(chip_ref, w_ref, o_ref):
        del chip_ref
        o_ref[...] = w_ref[...].astype(BF16)

    return pl.pallas_call(
        body, name=name,
        grid_spec=pltpu.PrefetchScalarGridSpec(
            num_scalar_prefetch=1, grid=(r // tm,),
            in_specs=[pl.BlockSpec((tm, c), lambda i, chip_ref: (i, 0))],
            out_specs=pl.BlockSpec((None, tm, c), lambda i, chip_ref: (chip_ref[0], i, 0))),
        out_shape=jax.ShapeDtypeStruct((N_CHIPS, r, c), BF16), compiler_params=_params("parallel"),
    )(chip, w)


class _Gather:
    in_place = True

    def __init__(self, fulls):
        self.ins = list(fulls)
        self.out_shape = [jax.ShapeDtypeStruct(f.shape, f.dtype) for f in fulls]
        n = len(fulls)
        self.scratch = [pltpu.SemaphoreType.DMA((6 * n,)), pltpu.SemaphoreType.DMA((6 * n,))]

    @staticmethod
    def _sends(outs, sems):
        send_sems, recv_sems = sems
        x, y, c, chips = _place()
        cps = []
        for w, ref in enumerate(outs):
            half = ref.shape[1] // 2
            blk = ref.at[2 * x + y, pl.ds(c * half, half)]
            cps += [_remote(blk, blk, send_sems, recv_sems, 6 * w + q, (*chip, c)) for q, chip in enumerate(chips)]
        return cps

    def start(self, ins, outs, sems):
        for cp in self._sends(outs, sems):
            cp.start()

    def finish(self, ins, outs, sems):
        send_sems, recv_sems = sems
        x, y, c, chips = _place()
        sibling = (x, y, 1 - c)
        forwards = []
        for w, ref in enumerate(outs):
            half = ref.shape[1] // 2
            for q, chip in enumerate(chips):
                blk = ref.at[2 * chip[0] + chip[1], pl.ds(c * half, half)]
                _remote(blk, blk, send_sems, recv_sems, 6 * w + q, sibling).wait_recv()
                fwd = _remote(blk, blk, send_sems, recv_sems, 6 * w + 3 + q, sibling)
                fwd.start()
                forwards.append(fwd)
        for w, ref in enumerate(outs):
            half = ref.shape[1] // 2
            for q, chip in enumerate(chips):
                blk = ref.at[2 * chip[0] + chip[1], pl.ds((1 - c) * half, half)]
                _remote(blk, blk, send_sems, recv_sems, 6 * w + 3 + q, sibling).wait_recv()
        for cp in self._sends(outs, sems) + forwards:
            cp.wait_send()


class _ExchangeChips:
    in_place = False

    def __init__(self, sums16):
        self.ins = list(sums16)
        self.out_shape = [jax.ShapeDtypeStruct(g.shape, g.dtype) for g in sums16]
        n = len(sums16)
        self.scratch = [pltpu.SemaphoreType.DMA((3 * n,)), pltpu.SemaphoreType.DMA((3 * n,))]

    @staticmethod
    def _sends(ins, outs, sems):
        send_sems, recv_sems = sems
        x, y, c, chips = _place()
        return [_remote(ins[w].at[2 * chip[0] + chip[1]], outs[w].at[2 * x + y], send_sems, recv_sems, 3 * w + q, (*chip, c))
                for w in range(len(ins)) for q, chip in enumerate(chips)]

    def start(self, ins, outs, sems):
        for cp in self._sends(ins, outs, sems):
            cp.start()

    def finish(self, ins, outs, sems):
        send_sems, recv_sems = sems
        x, y, c, chips = _place()
        for w in range(len(ins)):
            for q, chip in enumerate(chips):
                blk = outs[w].at[2 * chip[0] + chip[1]]
                _remote(blk, blk, send_sems, recv_sems, 3 * w + q, (*chip, c)).wait_recv()
        for cp in self._sends(ins, outs, sems):
            cp.wait_send()


def _mm_in_gather(chip, hn, win4):
    s, d = hn.shape
    c4 = win4.shape[2]
    tm, tn = _tile(s, 1024), _tile(c4, 1024)
    per = c4 // tn
    nj, gm = N_CHIPS * per, s // tm
    half = d // 2

    def block_of(j, chip_ref):
        o = j // per
        return chip_ref[0] ^ (((o & 1) << 1) | (o >> 1))

    i_late = max(gm - 2, 0)

    def body(chip_ref, hn_ref, w_any, proj_ref, w_ref, wbuf, wsem, send_sems, recv_sems):
        del w_any
        j, i = pl.program_id(0), pl.program_id(1)
        x, y, c, chips = _place()
        sibling = (x, y, 1 - c)

        def region(k, rows_half, t):
            return w_ref.at[k, pl.ds(rows_half * half, half), pl.ds(t * tn, tn)]

        def send_mine(q, t):
            blk = region(2 * x + y, c, t)
            return _remote(blk, blk, send_sems, recv_sems, q * per + t, (*chips[q], c))

        def forward(q, t):
            blk = region(2 * chips[q][0] + chips[q][1], c, t)
            return _remote(blk, blk, send_sems, recv_sems, (3 + q) * per + t, sibling)

        def fetch(jj, mine, slot):
            rows = pl.ds((c if mine else 1 - c) * half, half)
            cols = pl.ds(pl.multiple_of((jj % per) * tn, LANE), tn)
            return pltpu.make_async_copy(w_ref.at[block_of(jj, chip_ref), rows, cols], wbuf.at[slot, rows],
                                         wsem.at[2 * slot + (0 if mine else 1)])

        def foreign_tiles():
            return [(o, t) for o in range(1, N_CHIPS) for t in range(per)]

        @pl.when((j == 0) & (i == 0))
        def _():
            for t in range(per):
                send_mine(0, t).start()
                send_mine(1, t).start()
            fetch(0, True, 0).start()
            fetch(0, False, 0).start()

        nxt = j + 1

        @pl.when((i == 0) & (nxt < nj))
        def _():
            for o, t in foreign_tiles():
                @pl.when(nxt == o * per + t)
                def _():
                    blk = region(2 * chips[o - 1][0] + chips[o - 1][1], c, t)
                    _remote(blk, blk, send_sems, recv_sems, (o - 1) * per + t, sibling).wait_recv()
                    forward(o - 1, t).start()
                    if (o, t) == (1, per - 1):
                        for tt in range(per):
                            send_mine(2, tt).start()

            fetch(nxt, True, nxt % 2).start()

            @pl.when(nxt < per)
            def _():
                fetch(nxt, False, nxt % 2).start()

        @pl.when((i == i_late) & (nxt < nj))
        def _():
            for o, t in foreign_tiles():
                @pl.when(nxt == o * per + t)
                def _():
                    blk = region(2 * chips[o - 1][0] + chips[o - 1][1], 1 - c, t)
                    _remote(blk, blk, send_sems, recv_sems, (3 + o - 1) * per + t, sibling).wait_recv()
                    fetch(nxt, False, nxt % 2).start()

        @pl.when(i == 0)
        def _():
            fetch(j, True, j % 2).wait()
            fetch(j, False, j % 2).wait()

        proj_ref[...] = _dot(hn_ref[...], wbuf[j % 2])

        @pl.when((j == nj - 1) & (i == gm - 1))
        def _():
            for q in range(3):
                for t in range(per):
                    send_mine(q, t).wait_send()
                    forward(q, t).wait_send()

    return pl.pallas_call(
        body, name="mm_in_gather",
        grid_spec=pltpu.PrefetchScalarGridSpec(
            num_scalar_prefetch=1, grid=(nj, gm),
            in_specs=[pl.BlockSpec((tm, d), lambda j, i, chip_ref: (i, 0)), ANY],
            out_specs=[pl.BlockSpec((tm, tn), lambda j, i, chip_ref: (i, block_of(j, chip_ref) * per + j % per)), ANY],
            scratch_shapes=[pltpu.VMEM((2, d, tn), BF16), pltpu.SemaphoreType.DMA((4,)),
                            pltpu.SemaphoreType.DMA((6 * per,)), pltpu.SemaphoreType.DMA((6 * per,))]),
        out_shape=[jax.ShapeDtypeStruct((s, N_CHIPS * c4), F32), jax.ShapeDtypeStruct(win4.shape, win4.dtype)],
        input_output_aliases={2: 1},
        compiler_params=_params("arbitrary", "arbitrary"),
    )(chip, hn, win4)


class _SwapHalves:
    in_place = False

    def __init__(self, grads, whole=False):
        self.ins = list(grads)
        self.whole = whole
        self.out_shape = [jax.ShapeDtypeStruct((N_CHIPS, g.shape[1] // (1 if whole else 2), g.shape[2]), g.dtype)
                          for g in grads]
        self.scratch = [pltpu.SemaphoreType.DMA((len(grads),)), pltpu.SemaphoreType.DMA((len(grads),))]

    def _copies(self, ins, outs, sems):
        x, y, c, _ = _place()
        cps = []
        for w in range(len(ins)):
            half = ins[w].shape[1] // 2
            src = ins[w] if self.whole else ins[w].at[:, pl.ds((1 - c) * half, half)]
            cps.append(_remote(src, outs[w], sems[0], sems[1], w, (x, y, 1 - c)))
        return cps

    def start(self, ins, outs, sems):
        for cp in self._copies(ins, outs, sems):
            cp.start()

    def finish(self, ins, outs, sems):
        for cp in self._copies(ins, outs, sems):
            cp.wait()


def _chip_sum(name, core, grad, got):
    _, half, c = got.shape
    th = _tile(half, 128)
    n = half // th
    skip = n if grad.shape[1] != half else 0

    def body(core_ref, g_ref, o_ref, s32_ref, s16_ref):
        del core_ref
        v = g_ref[...] + o_ref[...]
        s32_ref[...] = v
        s16_ref[...] = v.astype(BF16)

    blk = pl.BlockSpec((None, th, c), lambda k, i, core_ref: (k, i, 0))
    return pl.pallas_call(
        body, name=name,
        grid_spec=pltpu.PrefetchScalarGridSpec(
            num_scalar_prefetch=1, grid=(N_CHIPS, n),
            in_specs=[pl.BlockSpec((None, th, c), lambda k, i, core_ref: (k, core_ref[0] * skip + i, 0)), blk],
            out_specs=[blk, blk]),
        out_shape=[jax.ShapeDtypeStruct((N_CHIPS, half, c), F32), jax.ShapeDtypeStruct((N_CHIPS, half, c), BF16)],
        compiler_params=_params("parallel", "parallel"),
    )(core, grad, got)


def _total(name, chip, core, sum32, got16):
    _, half, c = sum32.shape
    th = _tile(half, 128)
    n = half // th

    def body(chip_ref, core_ref, own_ref, a_ref, b_ref, c_ref, o_ref):
        del chip_ref, core_ref
        o_ref[...] = ((own_ref[...] + a_ref[...].astype(F32)) + b_ref[...].astype(F32)) + c_ref[...].astype(F32)

    def other(step):
        return pl.BlockSpec((None, th, c), lambda i, chip_ref, core_ref: ((chip_ref[0] + step) % N_CHIPS, i, 0))

    return pl.pallas_call(
        body, name=name,
        grid_spec=pltpu.PrefetchScalarGridSpec(
            num_scalar_prefetch=2, grid=(n,),
            in_specs=[other(0), other(1), other(2), other(3)],
            out_specs=pl.BlockSpec((th, c), lambda i, chip_ref, core_ref: (core_ref[0] * n + i, 0))),
        out_shape=jax.ShapeDtypeStruct((2 * half, c), F32),
        compiler_params=_params("parallel"),
    )(chip, core, sum32, got16, got16, got16)


def _join_halves(blocks):
    nw = len(blocks)

    def body(*refs):
        outs = refs[nw:2 * nw]
        send_sems, recv_sems = refs[2 * nw:]
        x, y, c, _ = _place()
        copies = []
        for w in range(nw):
            half = outs[w].shape[0] // 2
            rows = outs[w].at[pl.ds(c * half, half)]
            cp = _remote(rows, rows, send_sems, recv_sems, w, (x, y, 1 - c))
            cp.start()
            copies.append(cp)
        for w, cp in enumerate(copies):
            cp.wait_send()
            half = outs[w].shape[0] // 2
            rows = outs[w].at[pl.ds((1 - c) * half, half)]
            _remote(rows, rows, send_sems, recv_sems, w, (x, y, 1 - c)).wait_recv()

    return pl.pallas_call(
        body, name="join_halves",
        in_specs=[ANY] * nw, out_specs=[ANY] * nw,
        out_shape=[jax.ShapeDtypeStruct(t.shape, t.dtype) for t in blocks],
        scratch_shapes=[pltpu.SemaphoreType.DMA((nw,)), pltpu.SemaphoreType.DMA((nw,))],
        input_output_aliases={i: i for i in range(nw)},
    )(*blocks)


def _gather_small(pack):
    m_per, n = pack.shape

    def body(x_ref, out_ref, send_sems, recv_sems, local_sem):
        x, y, c, chips = _place()
        me, sibling = (x, y, c), (x, y, 1 - c)

        def rows(px, py, pc):
            return out_ref.at[pl.ds((4 * px + 2 * py + pc) * m_per, m_per), :]

        def copy(k, block, to, src=None):
            return _remote(rows(*block) if src is None else src, rows(*block), send_sems, recv_sems, k, to)

        mine = pltpu.make_async_copy(x_ref, rows(*me), local_sem)
        mine.start()
        first = [copy(0, me, sibling, src=x_ref)]
        first += [copy(1 + j, me, (*chip, c), src=x_ref) for j, chip in enumerate(chips)]
        for cp in first:
            cp.start()
        passed = [copy(4 + j, (*chip, c), sibling) for j, chip in enumerate(chips)]
        for j, chip in enumerate(chips):
            copy(1 + j, (*chip, c), me).wait_recv()
            passed[j].start()
        copy(0, sibling, me).wait_recv()
        for j, chip in enumerate(chips):
            copy(4 + j, (*chip, 1 - c), me).wait_recv()
        for cp in first + passed:
            cp.wait_send()
        mine.wait()

    return pl.pallas_call(
        body, name="gather_small",
        out_shape=jax.ShapeDtypeStruct((N_DEV * m_per, n), pack.dtype),
        in_specs=[pl.BlockSpec(memory_space=pltpu.VMEM)],
        out_specs=pl.BlockSpec(memory_space=pltpu.VMEM),
        scratch_shapes=[pltpu.SemaphoreType.DMA((7,)), pltpu.SemaphoreType.DMA((7,)), pltpu.SemaphoreType.DMA],
        compiler_params=_params(),
    )(pack)


def _adamw_math(w, g, m, v):
    m = ADAM_B1 * m + (1.0 - ADAM_B1) * g
    v = ADAM_B2 * v + (1.0 - ADAM_B2) * (g * g)
    m_hat = m / (1.0 - ADAM_B1 ** ADAM_STEP)
    v_hat = v / (1.0 - ADAM_B2 ** ADAM_STEP)
    delta = -ADAM_LR * (m_hat / (jnp.sqrt(v_hat) + ADAM_EPS) + ADAM_WD * w)
    return delta, m, v


def _adamw(name, w, g, m, v):
    r, c = w.shape
    tm = _tile(r, 128)

    def body(w_ref, g_ref, m_ref, v_ref, g_out, d_out, m_out, v_out):
        g = g_ref[...]
        g_out[...] = g
        d_out[...], m_out[...], v_out[...] = _adamw_math(w_ref[...], g, m_ref[...], v_ref[...])

    spec = pl.BlockSpec((tm, c), lambda i: (i, 0))
    return pl.pallas_call(
        body, name=name, grid=(r // tm,), in_specs=[spec] * 4, out_specs=[spec] * 4,
        out_shape=[jax.ShapeDtypeStruct((r, c), F32)] * 4, compiler_params=_params("parallel"),
    )(w, g, m, v)


def _adamw_small(gathered, w, m, v):
    rows = w.shape[0]

    def body(all_ref, w_ref, m_ref, v_ref, g_out, d_out, m_out, v_out):
        g = all_ref[0:rows, :]
        for dev in range(1, N_DEV):
            g = g + all_ref[dev * rows:(dev + 1) * rows, :]
        g_out[...] = g
        d_out[...], m_out[...], v_out[...] = _adamw_math(w_ref[...], g, m_ref[...], v_ref[...])

    return pl.pallas_call(
        body, name="adamw_small", out_shape=[jax.ShapeDtypeStruct(w.shape, F32)] * 4, compiler_params=_params(),
    )(gathered, w, m, v)


SMALL = ("g_pre", "w_s", "b_s", "ln_v_g", "ln_v_b", "g_q", "g_k", "rel_bias", "g_out_a", "g_out_b", "g_ple")
LARGE = ("w_in", "w_out", "w_ple_gate", "w_ple_up")
WEIGHTS = ("g_pre", "w_in", "w_s", "b_s", "ln_v_g", "ln_v_b", "g_q", "g_k", "rel_bias", "g_out_a", "g_out_b", "w_out",
           "g_ple", "w_ple_gate", "w_ple_up")


def _pack(parts):
    flat = jnp.concatenate([parts[n].reshape(-1).astype(F32) for n in SMALL])
    rows = -(-flat.shape[0] // (8 * LANE)) * 8
    return jnp.pad(flat, (0, rows * LANE - flat.shape[0])).reshape(rows, LANE)


def _unpack(pack, like):
    flat = pack.reshape(-1)
    out, at = {}, 0
    for n in SMALL:
        size = math.prod(like[n].shape)
        out[n] = flat[at:at + size].reshape(like[n].shape)
        at += size
    return out


def kernel(x, p, g_pre, w_in, w_s, b_s, ln_v_g, ln_v_b, g_q, g_k, rel_bias, g_out_a, g_out_b, w_out, g_ple, w_ple_gate, w_ple_up, loss_target, m_g_pre, m_w_in, m_w_s, m_b_s, m_ln_v_g, m_ln_v_b, m_g_q, m_g_k, m_rel_bias, m_g_out_a, m_g_out_b, m_w_out, m_g_ple, m_w_ple_gate, m_w_ple_up, v_g_pre, v_w_in, v_w_s, v_b_s, v_ln_v_g, v_ln_v_b, v_g_q, v_g_k, v_rel_bias, v_g_out_a, v_g_out_b, v_w_out, v_g_ple, v_w_ple_gate, v_w_ple_up):
    given = dict(locals())
    weights = {n: given[n] for n in WEIGHTS}
    mom_m = {n: given["m_" + n] for n in WEIGHTS}
    mom_v = {n: given["v_" + n] for n in WEIGHTS}
    xs, ps, tgt = x[0], p[0, 0], loss_target[0]
    d = xs.shape[1]

    core = lax.axis_index("c").astype(jnp.int32).reshape(1)
    chip = (2 * lax.axis_index("x") + lax.axis_index("y")).astype(jnp.int32).reshape(1)

    win4, wout4, wgate4, wup4 = [_cast_bf16("cast_" + n, chip, weights[n][0]) for n in LARGE]

    sp = {
        "g_pre": g_pre, "w_s": w_s[0], "b_s": b_s[0], "ln_v_g": ln_v_g, "ln_v_b": ln_v_b, "g_q": g_q, "g_k": g_k,
        "rel_bias": rel_bias, "g_out_a": g_out_a, "g_out_b": g_out_b, "g_ple": g_ple,
    }
    loss_local, grad_x, blocks, small = _local_step(xs, ps, tgt, sp, win4, wout4, wgate4, wup4,
                                                    dist={"chip": chip, "core": core})
    loss = lax.psum(loss_local, MESH_AXES)
    grads = dict(zip(LARGE, blocks))

    out_g, out_d, out_m, out_v = {}, {}, {}, {}
    for n in LARGE:
        gr, dl, mn, vn = _adamw("adamw_" + n, weights[n][0], grads[n], mom_m[n][0], mom_v[n][0])
        out_g[n], out_d[n], out_m[n], out_v[n] = gr[None], dl[None], mn[None], vn[None]

    gathered = _gather_small(_pack(small))
    pg, pd, pm, pv = _adamw_small(gathered, _pack(weights), _pack(mom_m), _pack(mom_v))
    for packed, out in ((pg, out_g), (pd, out_d), (pm, out_m), (pv, out_v)):
        out.update(_unpack(packed, weights))

    return (loss, grad_x[None], *[out_g[n] for n in WEIGHTS], *[out_d[n] for n in WEIGHTS],
            *[out_m[n] for n in WEIGHTS], *[out_v[n] for n in WEIGHTS])
```

```python
import functools
import math

import numpy as np

import jax
import jax.numpy as jnp
from jax import lax
from jax.experimental import pallas as pl
from jax.experimental.pallas import tpu as pltpu

F32 = jnp.float32
BF16 = jnp.bfloat16

HEAD_DIM = 128
CHUNK = 128
BLK = 128
DILATED = ((128, 1), (512, 4), (2048, 16))
NUM_BUCKETS = 32
MAX_DISTANCE = 2048
ATTN_GROUP = 4
EPS = 1e-6
NEG_INF = -1e30
N_CHIPS = 4
N_DEV = 8
MESH_AXES = ("x", "y", "c")

ADAM_LR = 0.001
ADAM_B1 = 0.9
ADAM_B2 = 0.999
ADAM_EPS = 1e-08
ADAM_WD = 0.01
ADAM_STEP = 10

V7X_VMEM_LIMIT = 56 * 1024 * 1024
LANE = 128
MESH = pl.DeviceIdType.MESH
ANY = pl.BlockSpec(memory_space=pl.ANY)


def _params(*sem):
    return pltpu.CompilerParams(dimension_semantics=sem or None, vmem_limit_bytes=V7X_VMEM_LIMIT)


def _tile(n, target):
    if n <= target:
        return n
    t = (target // LANE) * LANE
    while t > LANE and n % t:
        t -= LANE
    assert n % t == 0, (n, target)
    return t


def _gelu(x):
    return 0.5 * x * (1.0 + lax.erf(x * (1.0 / math.sqrt(2.0))))


def _gelu_grad(x):
    return 0.5 * (1.0 + lax.erf(x * (1.0 / math.sqrt(2.0)))) + x * jnp.exp(-0.5 * x * x) * (1.0 / math.sqrt(2.0 * math.pi))


def _silu_and_grad(x):
    s = jax.nn.sigmoid(x)
    return x * s, s * (1.0 + x * (1.0 - s))


def _dot(a, b, ta=False, tb=False):
    return lax.dot_general(a, b, (((0 if ta else 1,), (1 if tb else 0,)), ((), ())), preferred_element_type=F32)


def _colsum8(v):
    return jnp.sum(v.reshape(v.shape[0] // 8, 8, v.shape[1]), axis=0)


def _matmul(name, a, b, a_spec, b_spec, grid, acc_shape, out_shape, out_specs, ta=False, tb=False,
            extras=(), extra_specs=(), epilogue=None, comm=None, prefetch=()):
    nk = grid[2]
    n_pre = len(prefetch)
    n_extra = len(extras)
    single = not isinstance(out_shape, (tuple, list))
    outs_shape = (out_shape,) if single else tuple(out_shape)
    outs_specs = (out_specs,) if single else tuple(out_specs)
    n_out = len(outs_shape)
    c_ins = list(comm.ins) if comm else []
    c_outs = list(comm.out_shape) if comm else []
    c_scratch = list(comm.scratch) if comm else []
    n_cin, n_cout = len(c_ins), len(c_outs)

    def finish(acc, extra_refs, out_refs):
        if epilogue is None:
            out_refs[0][...] = acc.astype(out_refs[0].dtype)
        else:
            epilogue(acc, extra_refs, out_refs)

    def body(*refs):
        a_ref, b_ref, *rest = refs[n_pre:]
        extra_refs = rest[:n_extra]
        cin_refs = rest[n_extra:n_extra + n_cin]
        out_refs = rest[n_extra + n_cin:n_extra + n_cin + n_out]
        cout_refs = rest[n_extra + n_cin + n_out:n_extra + n_cin + n_out + n_cout]
        scratch_refs = rest[n_extra + n_cin + n_out + n_cout:]
        ids = [pl.program_id(ax) for ax in range(3)]
        if comm:
            sems = scratch_refs[len(scratch_refs) - len(c_scratch):]

            @pl.when((ids[0] == 0) & (ids[1] == 0) & (ids[2] == 0))
            def _():
                comm.start(cin_refs, cout_refs, sems)

        if nk == 1:
            finish(_dot(a_ref[...], b_ref[...], ta, tb), extra_refs, out_refs)
        else:
            acc_ref = scratch_refs[0]

            @pl.when(ids[2] == 0)
            def _():
                acc_ref[...] = jnp.zeros_like(acc_ref)

            acc_ref[...] += _dot(a_ref[...], b_ref[...], ta, tb)

            @pl.when(ids[2] == nk - 1)
            def _():
                finish(acc_ref[...], extra_refs, out_refs)

        if comm:
            @pl.when((ids[0] == grid[0] - 1) & (ids[1] == grid[1] - 1) & (ids[2] == nk - 1))
            def _():
                comm.finish(cin_refs, cout_refs, sems)

    scratch = ([] if nk == 1 else [pltpu.VMEM(acc_shape, F32)]) + c_scratch
    aliases = {n_pre + 2 + n_extra + i: n_out + i for i in range(n_cin)} if (comm and comm.in_place) else {}
    res = pl.pallas_call(
        body, name=name,
        grid_spec=pltpu.PrefetchScalarGridSpec(
            num_scalar_prefetch=n_pre, grid=grid,
            in_specs=[a_spec, b_spec, *extra_specs] + [ANY] * n_cin,
            out_specs=list(outs_specs) + [ANY] * n_cout, scratch_shapes=scratch),
        out_shape=list(outs_shape) + c_outs,
        input_output_aliases=aliases,
        compiler_params=_params(*(("arbitrary",) * 3 if comm else ("parallel", "parallel", "arbitrary"))),
    )(*prefetch, a, b, *extras, *c_ins)
    if comm:
        main = res[:n_out]
        return (main[0] if single else main), res[n_out:]
    return res[0] if single else res


def _shard_spec(rows, cols, per_shard, row_axis, col_axis):
    def index(i, j, k):
        g = (i, j, k)
        return (g[col_axis] // per_shard, g[row_axis], g[col_axis] % per_shard)
    return pl.BlockSpec((None, rows, cols), index)


def _rms_fwd(name, x, g):
    s, d = x.shape
    tm = _tile(s, 256)

    def body(x_ref, g_ref, y_ref, r_ref):
        xf = x_ref[...]
        r = lax.rsqrt(jnp.mean(xf * xf, axis=-1, keepdims=True) + EPS)
        y_ref[...] = (xf * r * g_ref[...]).astype(BF16)
        r_ref[...] = r

    return pl.pallas_call(
        body, name=name, grid=(s // tm,),
        in_specs=[pl.BlockSpec((tm, d), lambda i: (i, 0)), pl.BlockSpec((1, d), lambda i: (0, 0))],
        out_specs=(pl.BlockSpec((tm, d), lambda i: (i, 0)), pl.BlockSpec((tm, 1), lambda i: (i, 0))),
        out_shape=(jax.ShapeDtypeStruct((s, d), BF16), jax.ShapeDtypeStruct((s, 1), F32)),
        compiler_params=_params("parallel"),
    )(x, g)


def _rms_bwd(name, dy, x, r, g, skip, with_bf16):
    s, d = x.shape
    tm = _tile(s, 256)
    n = s // tm

    def body(dy_ref, x_ref, r_ref, g_ref, skip_ref, *outs):
        dx_ref = outs[0]
        dg_ref = outs[-2]
        acc_ref = outs[-1]
        i = pl.program_id(0)
        dyv, xv, rv = dy_ref[...], x_ref[...], r_ref[...]

        @pl.when(i == 0)
        def _():
            acc_ref[...] = jnp.zeros_like(acc_ref)

        acc_ref[...] += _colsum8(dyv * xv * rv)
        dg = dyv * g_ref[...]
        dx = skip_ref[...] + rv * (dg - xv * (rv * rv) * jnp.mean(dg * xv, axis=-1, keepdims=True))
        dx_ref[...] = dx
        if with_bf16:
            outs[1][...] = dx.astype(BF16)

        @pl.when(i == n - 1)
        def _():
            dg_ref[...] = jnp.sum(acc_ref[...], axis=0, keepdims=True)

    row = pl.BlockSpec((tm, d), lambda i: (i, 0))
    vec = pl.BlockSpec((1, d), lambda i: (0, 0))
    out_specs = [row] + ([row] if with_bf16 else []) + [vec]
    out_shape = [jax.ShapeDtypeStruct((s, d), F32)] + ([jax.ShapeDtypeStruct((s, d), BF16)] if with_bf16 else []) \
        + [jax.ShapeDtypeStruct((1, d), F32)]
    return pl.pallas_call(
        body, name=name, grid=(n,),
        in_specs=[row, row, pl.BlockSpec((tm, 1), lambda i: (i, 0)), vec, row],
        out_specs=out_specs, out_shape=out_shape, scratch_shapes=[pltpu.VMEM((8, d), F32)],
        compiler_params=_params("arbitrary"),
    )(dy, x, r, g, skip)


def _causal(w):
    t = lax.broadcasted_iota(jnp.int32, w.shape, 0)
    s_ = lax.broadcasted_iota(jnp.int32, w.shape, 1)
    return jnp.where(t >= s_, w, 0.0)


def _gmlp_fwd(proj, w_s, b_st, ln_g, ln_b, g_out, d_model):
    s = proj.shape[0]
    wa = d_model // 2
    groups = wa // HEAD_DIM
    tm = _tile(s, 256)
    n_chunks = tm // CHUNK

    def body(au_ref, av_ref, az_ref, ws_ref, bst_ref, lng_ref, lnb_ref, go_ref,
             y_ref, z_ref, vn_ref, mu_ref, rs_ref, ra_ref):
        gv = _gelu(av_ref[...])
        mu = jnp.mean(gv, axis=-1, keepdims=True)
        xc = gv - mu
        rs = lax.rsqrt(jnp.mean(xc * xc, axis=-1, keepdims=True) + EPS)
        vn = (xc * rs * lng_ref[...] + lnb_ref[...]).astype(BF16)
        vn_ref[...] = vn
        mu_ref[...] = mu
        rs_ref[...] = rs
        for g in range(groups):
            wm = _causal(ws_ref[g]).astype(BF16)
            cols = slice(g * HEAD_DIM, (g + 1) * HEAD_DIM)
            for ch in range(n_chunks):
                rows = slice(ch * CHUNK, (ch + 1) * CHUNK)
                z_ref[rows, cols] = _dot(wm, vn_ref[rows, cols]) + bst_ref[:, g:g + 1]
        ya = _gelu(au_ref[...]) * z_ref[...]
        ra = lax.rsqrt(jnp.mean(ya * ya, axis=-1, keepdims=True) + EPS)
        ra_ref[...] = ra
        sz, _ = _silu_and_grad(az_ref[...])
        y_ref[...] = (ya * ra * go_ref[...] * sz).astype(BF16)

    def col(j):
        return pl.BlockSpec((tm, wa), lambda i: (i, j))
    vec = pl.BlockSpec((1, wa), lambda i: (0, 0))
    stat = pl.BlockSpec((tm, 1), lambda i: (i, 0))
    return pl.pallas_call(
        body, name="gmlp_fwd", grid=(s // tm,),
        in_specs=[col(0), col(1), col(2),
                  pl.BlockSpec((groups, CHUNK, CHUNK), lambda i: (0, 0, 0)),
                  pl.BlockSpec((CHUNK, groups), lambda i: (0, 0)), vec, vec, vec],
        out_specs=(col(0), col(0), col(0), stat, stat, stat),
        out_shape=(jax.ShapeDtypeStruct((s, d_model), BF16), jax.ShapeDtypeStruct((s, wa), F32),
                   jax.ShapeDtypeStruct((s, wa), BF16), jax.ShapeDtypeStruct((s, 1), F32),
                   jax.ShapeDtypeStruct((s, 1), F32), jax.ShapeDtypeStruct((s, 1), F32)),
        compiler_params=_params("parallel"),
    )(proj, proj, proj, w_s, b_st, ln_g, ln_b, g_out)


def _gmlp_bwd(dproj, dy, proj, z, vn, mu, rs, ra, w_s, ln_g, g_out, d_model):
    s = proj.shape[0]
    wa = d_model // 2
    groups = wa // HEAD_DIM
    tm = _tile(s, 256)
    n_chunks = tm // CHUNK
    n = s // tm

    def causal_stack(w):
        t = lax.broadcasted_iota(jnp.int32, w.shape, 1)
        s_ = lax.broadcasted_iota(jnp.int32, w.shape, 2)
        return jnp.where(t >= s_, w, 0.0)

    def body(dproj_in, dy_ref, au_ref, av_ref, az_ref, z_ref, vn_ref, mu_ref, rs_ref, ra_ref, ws_ref, lng_ref, go_ref,
             dp_ref, gws_ref, dzs_ref, glg_ref, glb_ref, ggo_ref,
             dz_s, dvn_s, acc_lg, acc_lb, acc_go):
        del dproj_in
        i = pl.program_id(0)

        @pl.when(i == 0)
        def _():
            gws_ref[...] = jnp.zeros_like(gws_ref)
            dzs_ref[...] = jnp.zeros_like(dzs_ref)
            acc_lg[...] = jnp.zeros_like(acc_lg)
            acc_lb[...] = jnp.zeros_like(acc_lb)
            acc_go[...] = jnp.zeros_like(acc_go)

        au, az, zv, rav = au_ref[...], az_ref[...], z_ref[...], ra_ref[...]
        u = _gelu(au)
        ya = u * zv
        sz, dsz = _silu_and_grad(az)
        dyv = dy_ref[...]
        dp_ref[:, 2 * wa:3 * wa] = (dyv * (ya * rav * go_ref[...]) * dsz).astype(BF16)
        dn = dyv * sz
        acc_go[...] += _colsum8(dn * ya * rav)
        dyg = dn * go_ref[...]
        dya = rav * (dyg - ya * (rav * rav) * jnp.mean(dyg * ya, axis=-1, keepdims=True))
        dp_ref[:, 0:wa] = (dya * zv * _gelu_grad(au)).astype(BF16)
        dz_s[...] = dya * u
        for ch in range(n_chunks):
            dzs_ref[...] += dz_s[ch * CHUNK:(ch + 1) * CHUNK, :]
        for g in range(groups):
            wm = _causal(ws_ref[g]).astype(BF16)
            cols = slice(g * HEAD_DIM, (g + 1) * HEAD_DIM)
            for ch in range(n_chunks):
                rows = slice(ch * CHUNK, (ch + 1) * CHUNK)
                dzb = dz_s[rows, cols].astype(BF16)
                gws_ref[g] += _dot(dzb, vn_ref[rows, cols], tb=True)
                dvn_s[rows, cols] = _dot(wm, dzb, ta=True)
        av = av_ref[...]
        xh = (_gelu(av) - mu_ref[...]) * rs_ref[...]
        dvn = dvn_s[...]
        acc_lb[...] += _colsum8(dvn)
        acc_lg[...] += _colsum8(dvn * xh)
        dxh = dvn * lng_ref[...]
        dgv = rs_ref[...] * (dxh - jnp.mean(dxh, axis=-1, keepdims=True) - xh * jnp.mean(dxh * xh, axis=-1, keepdims=True))
        dp_ref[:, wa:2 * wa] = (dgv * _gelu_grad(av)).astype(BF16)

        @pl.when(i == n - 1)
        def _():
            gws_ref[...] = causal_stack(gws_ref[...])
            glg_ref[...] = jnp.sum(acc_lg[...], axis=0, keepdims=True)
            glb_ref[...] = jnp.sum(acc_lb[...], axis=0, keepdims=True)
            ggo_ref[...] = jnp.sum(acc_go[...], axis=0, keepdims=True)

    def col(j):
        return pl.BlockSpec((tm, wa), lambda i: (i, j))
    vec = pl.BlockSpec((1, wa), lambda i: (0, 0))
    stat = pl.BlockSpec((tm, 1), lambda i: (i, 0))
    ws_spec = pl.BlockSpec((groups, CHUNK, CHUNK), lambda i: (0, 0, 0))
    d_in = dproj.shape[1]
    return pl.pallas_call(
        body, name="gmlp_bwd", grid=(n,),
        in_specs=[ANY, col(0), col(0), col(1), col(2), col(0), col(0), stat, stat, stat, ws_spec, vec, vec],
        out_specs=(pl.BlockSpec((tm, 3 * wa), lambda i: (i, 0)), ws_spec,
                   pl.BlockSpec((CHUNK, wa), lambda i: (0, 0)), vec, vec, vec),
        out_shape=(jax.ShapeDtypeStruct((s, d_in), BF16),
                   jax.ShapeDtypeStruct((groups, CHUNK, CHUNK), F32), jax.ShapeDtypeStruct((CHUNK, wa), F32))
        + (jax.ShapeDtypeStruct((1, wa), F32),) * 3,
        scratch_shapes=[pltpu.VMEM((tm, wa), F32), pltpu.VMEM((tm, wa), F32)] + [pltpu.VMEM((8, wa), F32)] * 3,
        input_output_aliases={0: 0},
        compiler_params=_params("arbitrary"),
    )(dproj, dy, proj, proj, proj, z, vn, mu, rs, ra, w_s, ln_g, g_out)


def _gate_fwd(y_in, yb, proj, g_out, d_model):
    s = proj.shape[0]
    wa = d_model // 2
    tm = _tile(s, 256)

    def body(y_any, yb_ref, bz_ref, go_ref, y_ref, rb_ref):
        del y_any
        ybv = yb_ref[...]
        rb = lax.rsqrt(jnp.mean(ybv * ybv, axis=-1, keepdims=True) + EPS)
        rb_ref[...] = rb
        sz, _ = _silu_and_grad(bz_ref[...])
        y_ref[...] = (ybv * rb * go_ref[...] * sz).astype(BF16)

    return pl.pallas_call(
        body, name="gate_b_fwd", grid=(s // tm,),
        in_specs=[ANY, pl.BlockSpec((tm, wa), lambda i: (i, 0)), pl.BlockSpec((tm, wa), lambda i: (i, 6)),
                  pl.BlockSpec((1, wa), lambda i: (0, 0))],
        out_specs=(pl.BlockSpec((tm, wa), lambda i: (i, 1)), pl.BlockSpec((tm, 1), lambda i: (i, 0))),
        out_shape=(jax.ShapeDtypeStruct((s, d_model), BF16), jax.ShapeDtypeStruct((s, 1), F32)),
        input_output_aliases={0: 0},
        compiler_params=_params("parallel"),
    )(y_in, yb, proj, g_out)


def _gate_bwd(dy, yb, proj, rb, g_out, d_model):
    s, d_in = proj.shape
    wa = d_model // 2
    tm = _tile(s, 256)
    n = s // tm

    def body(dy_ref, yb_ref, bz_ref, rb_ref, go_ref, dp_ref, do_ref, ggo_ref, acc):
        i = pl.program_id(0)

        @pl.when(i == 0)
        def _():
            acc[...] = jnp.zeros_like(acc)

        dyv, ybv, rbv = dy_ref[...], yb_ref[...], rb_ref[...]
        sz, dsz = _silu_and_grad(bz_ref[...])
        dp_ref[...] = (dyv * (ybv * rbv * go_ref[...]) * dsz).astype(BF16)
        dn = dyv * sz
        acc[...] += _colsum8(dn * ybv * rbv)
        dyg = dn * go_ref[...]
        do_ref[...] = rbv * (dyg - ybv * (rbv * rbv) * jnp.mean(dyg * ybv, axis=-1, keepdims=True))

        @pl.when(i == n - 1)
        def _():
            ggo_ref[...] = jnp.sum(acc[...], axis=0, keepdims=True)

    vec = pl.BlockSpec((1, wa), lambda i: (0, 0))
    return pl.pallas_call(
        body, name="gate_b_bwd", grid=(n,),
        in_specs=[pl.BlockSpec((tm, wa), lambda i: (i, 1)), pl.BlockSpec((tm, wa), lambda i: (i, 0)),
                  pl.BlockSpec((tm, wa), lambda i: (i, 6)), pl.BlockSpec((tm, 1), lambda i: (i, 0)), vec],
        out_specs=(pl.BlockSpec((tm, wa), lambda i: (i, 6)), pl.BlockSpec((tm, wa), lambda i: (i, 0)), vec),
        out_shape=(jax.ShapeDtypeStruct((s, d_in), BF16), jax.ShapeDtypeStruct((s, wa), F32),
                   jax.ShapeDtypeStruct((1, wa), F32)),
        scratch_shapes=[pltpu.VMEM((8, wa), F32)],
        compiler_params=_params("arbitrary"),
    )(dy, yb, proj, rb, g_out)


def _bucket_tables():
    qi = BLK + np.arange(BLK)
    kj = np.arange(2 * BLK)
    delta = qi[:, None] - kj[None, :]
    max_exact = NUM_BUCKETS // 2
    tabs = []
    for window, dil in DILATED:
        band = (delta >= 0) & (delta <= window // dil)
        dist = np.clip(delta, 0, None) * dil
        d = np.maximum(dist, 1).astype(np.float32)
        large = max_exact + (np.log(d / np.float32(max_exact)) / np.float32(math.log(MAX_DISTANCE / max_exact))
                             * np.float32(NUM_BUCKETS - max_exact)).astype(np.int32)
        large = np.minimum(large, NUM_BUCKETS - 1)
        tabs.append(np.where(band, np.where(dist < max_exact, dist, large), -1).astype(np.int32))
    return np.stack(tabs)


def _bias_tiles(tab_ref, rb_ref, h, bias_s):
    col = lax.broadcasted_iota(jnp.int32, (BLK, 2 * BLK), 1)
    for i in range(len(DILATED)):
        t = tab_ref[i]
        bias = jnp.zeros(t.shape, F32)
        for b in range(NUM_BUCKETS):
            bias = jnp.where(t == b, rb_ref[b, h], bias)
        bias = jnp.where(t >= 0, bias, NEG_INF)
        bias_s[2 * i] = jnp.where(col >= BLK, bias, NEG_INF)
        bias_s[2 * i + 1] = bias


def _block_rows(b, n_blocks, dil):
    nb = n_blocks // dil
    r = b // nb
    n = b % nb
    start = r + dil * BLK * n
    if dil == 1:
        return pl.ds(pl.multiple_of(start, BLK), BLK), n
    return pl.ds(start, BLK, stride=dil), n


def _sub_block(b, rows=BLK, pad=0):
    return pl.ds(pl.multiple_of(b * BLK + pad, BLK), rows)


def _rms_rows(x, gain):
    return x * lax.rsqrt(jnp.mean(x * x, axis=-1, keepdims=True) + EPS) * gain


def _attn_fwd(proj, g_q, g_k, rel_bias, d_model, comm=None):
    s = proj.shape[0]
    heads = d_model // 2 // HEAD_DIM
    n_blocks = s // BLK
    scale = HEAD_DIM ** -0.5
    tables = jnp.asarray(_bucket_tables())
    c_ins = list(comm.ins) if comm else []
    c_outs = list(comm.out_shape) if comm else []
    c_scratch = list(comm.scratch) if comm else []

    def body(q_ref, k_ref, v_ref, gq_ref, gk_ref, rb_ref, tab_ref, *rest):
        cin_refs, rest = rest[:len(c_ins)], rest[len(c_ins):]
        yb_ref, lse_ref = rest[:2]
        cout_refs, rest = rest[2:2 + len(c_outs)], rest[2 + len(c_outs):]
        qs, ks, vs, m_s, l_s, acc_s, bias_s, sc_s, p_s = rest[:9]
        sems = rest[9:]
        h = pl.program_id(0)
        if comm:
            @pl.when(h == 0)
            def _():
                comm.start(cin_refs, cout_refs, sems)

        _bias_tiles(tab_ref, rb_ref, h, bias_s)
        gq = gq_ref[...] * scale
        gk = gk_ref[...]
        ks[0:BLK, :] = jnp.zeros((BLK, HEAD_DIM), BF16)
        vs[0:BLK, :] = jnp.zeros((BLK, HEAD_DIM), BF16)

        for i, (_, dil) in enumerate(DILATED):
            def prepare(b, carry, dil=dil):
                rows, _ = _block_rows(b, n_blocks, dil)
                qs[_sub_block(b), :] = _rms_rows(q_ref[rows, :], gq).astype(BF16)
                ks[_sub_block(b, pad=BLK), :] = _rms_rows(k_ref[rows, :], gk).astype(BF16)
                vs[_sub_block(b, pad=BLK), :] = v_ref[rows, :].astype(BF16)
                return carry
            lax.fori_loop(0, n_blocks, prepare, 0, unroll=8)

            def group(g, carry, i=i, dil=dil):
                blocks = [g * ATTN_GROUP + t for t in range(ATTN_GROUP)]
                for t, b in enumerate(blocks):
                    sc_s[t] = _dot(qs[_sub_block(b), :], ks[_sub_block(b, 2 * BLK), :], tb=True)
                for t, b in enumerate(blocks):
                    rows, n = _block_rows(b, n_blocks, dil)
                    sc = sc_s[t] + bias_s[2 * i + jnp.minimum(n, 1)]
                    m_i = jnp.max(sc, axis=-1, keepdims=True)
                    pr = jnp.exp(sc - m_i)
                    l_i = jnp.sum(pr, axis=-1, keepdims=True)
                    p_s[t] = pr.astype(BF16)
                    m_s[i, rows, :] = m_i
                    l_s[i, rows, :] = l_i
                for t, b in enumerate(blocks):
                    rows, _ = _block_rows(b, n_blocks, dil)
                    acc_s[i, rows, :] = _dot(p_s[t], vs[_sub_block(b, 2 * BLK), :])
                return carry
            lax.fori_loop(0, n_blocks // ATTN_GROUP, group, 0)

        m = jnp.maximum(jnp.maximum(m_s[0], m_s[1]), m_s[2])
        l = jnp.zeros_like(m)
        acc = jnp.zeros((s, HEAD_DIM), F32)
        for i in range(len(DILATED)):
            w = jnp.exp(m_s[i] - m)
            l = l + l_s[i] * w
            acc = acc + acc_s[i] * w
        yb_ref[...] = acc / l
        lse_ref[...] = m + jnp.log(l)
        if comm:
            @pl.when(h == heads - 1)
            def _():
                comm.finish(cin_refs, cout_refs, sems)

    def head_col(off):
        return pl.BlockSpec((s, HEAD_DIM), lambda h: (0, off * heads + h))
    vec = pl.BlockSpec((1, HEAD_DIM), lambda h: (0, 0))
    res = pl.pallas_call(
        body, name="attn_fwd", grid=(heads,),
        in_specs=[head_col(3), head_col(4), head_col(5), vec, vec,
                  pl.BlockSpec(memory_space=pltpu.SMEM),
                  pl.BlockSpec((len(DILATED), BLK, 2 * BLK), lambda h: (0, 0, 0))] + [ANY] * len(c_ins),
        out_specs=[pl.BlockSpec((s, HEAD_DIM), lambda h: (0, h)), pl.BlockSpec((None, s, 1), lambda h: (h, 0, 0))]
        + [ANY] * len(c_outs),
        out_shape=[jax.ShapeDtypeStruct((s, heads * HEAD_DIM), F32), jax.ShapeDtypeStruct((heads, s, 1), F32)] + c_outs,
        scratch_shapes=[pltpu.VMEM((s, HEAD_DIM), BF16), pltpu.VMEM((s + BLK, HEAD_DIM), BF16),
                        pltpu.VMEM((s + BLK, HEAD_DIM), BF16),
                        pltpu.VMEM((len(DILATED), s, 1), F32), pltpu.VMEM((len(DILATED), s, 1), F32),
                        pltpu.VMEM((len(DILATED), s, HEAD_DIM), F32),
                        pltpu.VMEM((2 * len(DILATED), BLK, 2 * BLK), F32),
                        pltpu.VMEM((ATTN_GROUP, BLK, 2 * BLK), F32), pltpu.VMEM((ATTN_GROUP, BLK, 2 * BLK), BF16)] + c_scratch,
        input_output_aliases={7 + i: 2 + i for i in range(len(c_ins))} if (comm and comm.in_place) else {},
        compiler_params=_params("arbitrary"),
    )(proj, proj, proj, g_q, g_k, rel_bias, tables, *c_ins)
    return res[0], res[1], res[2:]


def _attn_bwd(dproj, d_o, yb, lse, proj, g_q, g_k, rel_bias, d_model, comm=None):
    s, d_in = proj.shape
    heads = d_model // 2 // HEAD_DIM
    n_blocks = s // BLK
    scale = HEAD_DIM ** -0.5
    tables = jnp.asarray(_bucket_tables())
    n_dil = len(DILATED)
    c_ins = list(comm.ins) if comm else []
    c_outs = list(comm.out_shape) if comm else []
    c_scratch = list(comm.scratch) if comm else []

    def body(dp_any, q_ref, k_ref, v_ref, do_ref, o_ref, lse_ref, gq_ref, gk_ref, rb_ref, tab_ref, *rest):
        cin_refs, rest = rest[:len(c_ins)], rest[len(c_ins):]
        dp_out, ggq_ref, ggk_ref, db_ref = rest[:4]
        cout_refs, rest = rest[4:4 + len(c_outs)], rest[4 + len(c_outs):]
        (qs, ks, vs, dos, lse_s, del_s, dqn, dkn, dvv, dq_u, dk_u, dv_u, bias_s, dbias_s, sc_s, dp_s, p_s, ds_s,
         sems) = rest[:19]
        comm_sems = rest[19:]
        del dp_any
        h = pl.program_id(0)
        if comm:
            @pl.when(h == 0)
            def _():
                comm.start(cin_refs, cout_refs, comm_sems)


        @pl.when(h == 0)
        def _():
            ggq_ref[...] = jnp.zeros_like(ggq_ref)
            ggk_ref[...] = jnp.zeros_like(ggk_ref)

        dbias_s[...] = jnp.zeros_like(dbias_s)
        _bias_tiles(tab_ref, rb_ref, h, bias_s)
        gq = gq_ref[...] * scale
        gk = gk_ref[...]
        ks[0:BLK, :] = jnp.zeros((BLK, HEAD_DIM), BF16)
        vs[0:BLK, :] = jnp.zeros((BLK, HEAD_DIM), BF16)

        for i, (_, dil) in enumerate(DILATED):
            def prepare(b, carry, dil=dil):
                rows, _ = _block_rows(b, n_blocks, dil)
                qs[_sub_block(b), :] = _rms_rows(q_ref[rows, :], gq).astype(BF16)
                ks[_sub_block(b, pad=BLK), :] = _rms_rows(k_ref[rows, :], gk).astype(BF16)
                vs[_sub_block(b, pad=BLK), :] = v_ref[rows, :].astype(BF16)
                do = do_ref[rows, :]
                dos[_sub_block(b), :] = do.astype(BF16)
                del_s[_sub_block(b), :] = jnp.sum(do * o_ref[rows, :], axis=-1, keepdims=True)
                lse_s[_sub_block(b), :] = lse_ref[rows, :]
                return carry
            lax.fori_loop(0, n_blocks, prepare, 0, unroll=8)
            dk_u[...] = jnp.zeros_like(dk_u)
            dv_u[...] = jnp.zeros_like(dv_u)

            def group(g, carry, i=i, dil=dil):
                blocks = [g * ATTN_GROUP + t for t in range(ATTN_GROUP)]
                for t, b in enumerate(blocks):
                    sc_s[t] = _dot(qs[_sub_block(b), :], ks[_sub_block(b, 2 * BLK), :], tb=True)
                    dp_s[t] = _dot(dos[_sub_block(b), :], vs[_sub_block(b, 2 * BLK), :], tb=True)
                for t, b in enumerate(blocks):
                    _, n = _block_rows(b, n_blocks, dil)
                    pr = jnp.exp(sc_s[t] + bias_s[2 * i + jnp.minimum(n, 1)] - lse_s[_sub_block(b), :])
                    ds = pr * (dp_s[t] - del_s[_sub_block(b), :])
                    dbias_s[i] += ds
                    p_s[t] = pr.astype(BF16)
                    ds_s[t] = ds.astype(BF16)
                for t, b in enumerate(blocks):
                    dq_u[_sub_block(b), :] = _dot(ds_s[t], ks[_sub_block(b, 2 * BLK), :])
                    dk_u[_sub_block(b, 2 * BLK), :] += _dot(ds_s[t], qs[_sub_block(b), :], ta=True)
                    dv_u[_sub_block(b, 2 * BLK), :] += _dot(p_s[t], dos[_sub_block(b), :], ta=True)
                return carry
            lax.fori_loop(0, n_blocks // ATTN_GROUP, group, 0)

            def scatter(b, carry, i=i, dil=dil):
                rows, _ = _block_rows(b, n_blocks, dil)
                for acc, part, pad in ((dqn, dq_u, 0), (dkn, dk_u, BLK), (dvv, dv_u, BLK)):
                    val = part[_sub_block(b, pad=pad), :]
                    acc[rows, :] = val if i == 0 else acc[rows, :] + val
                return carry
            lax.fori_loop(0, n_blocks, scatter, 0, unroll=8)

        q = q_ref[...]
        rq = lax.rsqrt(jnp.mean(q * q, axis=-1, keepdims=True) + EPS)
        k = k_ref[...]
        rk = lax.rsqrt(jnp.mean(k * k, axis=-1, keepdims=True) + EPS)
        dq_n = dqn[...]
        ggq_ref[...] += jnp.sum(dq_n * (q * rq) * scale, axis=0, keepdims=True)
        dg = dq_n * gq_ref[...] * scale
        qs[...] = (rq * (dg - q * (rq * rq) * jnp.mean(dg * q, axis=-1, keepdims=True))).astype(BF16)
        dk_n = dkn[...]
        ggk_ref[...] += jnp.sum(dk_n * (k * rk), axis=0, keepdims=True)
        dg = dk_n * gk_ref[...]
        dos[...] = (rk * (dg - k * (rk * rk) * jnp.mean(dg * k, axis=-1, keepdims=True))).astype(BF16)
        vs[BLK:, :] = dvv[...].astype(BF16)
        copies = [pltpu.make_async_copy(src, dp_out.at[:, pl.ds(pl.multiple_of(((3 + j) * heads + h) * HEAD_DIM, HEAD_DIM), HEAD_DIM)],
                                        sems.at[j]) for j, src in enumerate((qs, dos, vs.at[pl.ds(BLK, s)]))]
        for cp in copies:
            cp.start()
        for b in range(NUM_BUCKETS):
            tot = jnp.zeros((BLK, 2 * BLK), F32)
            for i in range(n_dil):
                tot = tot + jnp.where(tab_ref[i] == b, dbias_s[i], 0.0)
            db_ref[b:b + 1, :] = jnp.full((1, LANE), jnp.sum(tot), F32)
        for cp in copies:
            cp.wait()
        if comm:
            @pl.when(h == heads - 1)
            def _():
                comm.finish(cin_refs, cout_refs, comm_sems)

    def head_col(off):
        return pl.BlockSpec((s, HEAD_DIM), lambda h: (0, off * heads + h))
    own_col = pl.BlockSpec((s, HEAD_DIM), lambda h: (0, h))
    vec = pl.BlockSpec((1, HEAD_DIM), lambda h: (0, 0))
    big = pltpu.VMEM((s, HEAD_DIM), F32)
    padded32 = pltpu.VMEM((s + BLK, HEAD_DIM), F32)
    padded16 = pltpu.VMEM((s + BLK, HEAD_DIM), BF16)
    res = pl.pallas_call(
        body, name="attn_bwd", grid=(heads,),
        in_specs=[ANY, head_col(3), head_col(4), head_col(5), own_col, own_col,
                  pl.BlockSpec((None, s, 1), lambda h: (h, 0, 0)), vec, vec,
                  pl.BlockSpec(memory_space=pltpu.SMEM),
                  pl.BlockSpec((n_dil, BLK, 2 * BLK), lambda h: (0, 0, 0))] + [ANY] * len(c_ins),
        out_specs=[ANY, vec, vec, pl.BlockSpec((None, NUM_BUCKETS, LANE), lambda h: (h, 0, 0))] + [ANY] * len(c_outs),
        out_shape=[jax.ShapeDtypeStruct((s, d_in), BF16), jax.ShapeDtypeStruct((1, HEAD_DIM), F32),
                   jax.ShapeDtypeStruct((1, HEAD_DIM), F32), jax.ShapeDtypeStruct((heads, NUM_BUCKETS, LANE), F32)]
        + c_outs,
        scratch_shapes=[pltpu.VMEM((s, HEAD_DIM), BF16), padded16, padded16, pltpu.VMEM((s, HEAD_DIM), BF16),
                        pltpu.VMEM((s, 1), F32), pltpu.VMEM((s, 1), F32),
                        big, big, big, big, padded32, padded32,
                        pltpu.VMEM((2 * n_dil, BLK, 2 * BLK), F32), pltpu.VMEM((n_dil, BLK, 2 * BLK), F32),
                        pltpu.VMEM((ATTN_GROUP, BLK, 2 * BLK), F32), pltpu.VMEM((ATTN_GROUP, BLK, 2 * BLK), F32),
                        pltpu.VMEM((ATTN_GROUP, BLK, 2 * BLK), BF16), pltpu.VMEM((ATTN_GROUP, BLK, 2 * BLK), BF16),
                        pltpu.SemaphoreType.DMA((3,))] + c_scratch,
        input_output_aliases={0: 0},
        compiler_params=_params("arbitrary"),
    )(dproj, proj, proj, proj, d_o, yb, lse, g_q, g_k, rel_bias, tables, *c_ins)
    return res[0], res[1], res[2], res[3], res[4:]


def _local_step(xs, ps, tgt, sp, win4, wout4, wgate4, wup4, dist=None):
    s, d = xs.shape
    wa = d // 2
    heads = wa // HEAD_DIM
    d_in = 7 * wa
    c4 = d_in // N_CHIPS
    dple = ps.shape[1]

    tm, tn = _tile(s, 1024), _tile(d, 1024)
    tn_in = _tile(c4, 1024)
    per_in = c4 // tn_in
    tn_up = _tile(d // N_CHIPS, 512)
    per_up = (d // N_CHIPS) // tn_up
    gm, gn = s // tm, d // tn

    hn, rx = _rms_fwd("rms_pre", xs, sp["g_pre"])
    if dist:
        proj, win4 = _mm_in_gather(dist["chip"], hn, win4)
    else:
        proj = _matmul(
            "mm_in", hn, win4, pl.BlockSpec((tm, d), lambda i, j, k: (i, 0)), _shard_spec(d, tn_in, per_in, 2, 1),
            (gm, d_in // tn_in, 1), None, jax.ShapeDtypeStruct((s, d_in), F32),
            pl.BlockSpec((tm, tn_in), lambda i, j, k: (i, j)))
    b_st = sp["b_s"].T
    y, z, vn, mu_v, rs_v, ra = _gmlp_fwd(proj, sp["w_s"], b_st, sp["ln_v_g"], sp["ln_v_b"], sp["g_out_a"], d)
    yb, lse, gathered = _attn_fwd(proj, sp["g_q"], sp["g_k"], sp["rel_bias"], d,
                                  comm=_Gather([wout4, wgate4, wup4]) if dist else None)
    if dist:
        wout4, wgate4, wup4 = gathered
    wout, wgate = wout4.reshape(d, d), wgate4.reshape(d, d)
    y, rb = _gate_fwd(y, yb, proj, sp["g_out_b"], d)

    def residual(acc, extra, outs):
        outs[0][...] = extra[0][...] + acc

    tile = pl.BlockSpec((tm, tn), lambda i, j, k: (i, j))
    tile_up = pl.BlockSpec((tm, tn_up), lambda i, j, k: (i, j))
    h = _matmul(
        "mm_out", y, wout, pl.BlockSpec((tm, d), lambda i, j, k: (i, 0)), pl.BlockSpec((d, tn_up), lambda i, j, k: (0, j)),
        (gm, d // tn_up, 1), None, jax.ShapeDtypeStruct((s, d), F32), tile_up, extras=(xs,), extra_specs=(tile_up,),
        epilogue=residual)
    hp, rh = _rms_fwd("rms_ple", h, sp["g_ple"])
    p16 = ps.astype(BF16)

    def head(acc, extra, outs):
        p_ref, wup_ref, h_ref, t_ref = extra
        dout_ref, dgl_ref, dup_ref, loss_ref = outs
        up = _dot(p_ref[...], wup_ref[...])
        gate = jax.nn.sigmoid(acc)
        err = h_ref[...] + gate * up - t_ref[...]
        sq = _colsum8(err * err)
        part = sq[:, 0:LANE]
        for c in range(1, sq.shape[1] // LANE):
            part = part + sq[:, c * LANE:(c + 1) * LANE]
        loss_ref[...] = part
        dout = err * (1.0 / d)
        dout_ref[...] = dout
        dup_ref[...] = (dout * gate).astype(BF16)
        dgl_ref[...] = (dout * up * gate * (1.0 - gate)).astype(BF16)

    tile_up = pl.BlockSpec((tm, tn_up), lambda i, j, k: (i, j))
    dout, dglin, dup, loss_parts = _matmul(
        "mm_gate_loss", hp, wgate, pl.BlockSpec((tm, d), lambda i, j, k: (i, 0)), pl.BlockSpec((d, tn_up), lambda i, j, k: (0, j)),
        (gm, d // tn_up, 1), None,
        (jax.ShapeDtypeStruct((s, d), F32), jax.ShapeDtypeStruct((s, d), BF16), jax.ShapeDtypeStruct((s, d), BF16),
         jax.ShapeDtypeStruct((gm * 8, (d // tn_up) * LANE), F32)),
        (tile_up, tile_up, tile_up, pl.BlockSpec((8, LANE), lambda i, j, k: (i, j))),
        extras=(p16, wup4, h, tgt),
        extra_specs=(pl.BlockSpec((tm, dple), lambda i, j, k: (i, 0)), _shard_spec(dple, tn_up, per_up, 2, 1),
                     tile_up, tile_up),
        epilogue=head)
    loss = 0.5 * jnp.sum(loss_parts) * (1.0 / d)

    dhp = _matmul(
        "mm_dhp", dglin, wgate, pl.BlockSpec((tm, d), lambda i, j, k: (i, 0)), pl.BlockSpec((tn, d), lambda i, j, k: (j, 0)),
        (gm, gn, 1), None, jax.ShapeDtypeStruct((s, d), F32), tile, tb=True)
    dh, dh16, g_g_ple = _rms_bwd("rms_ple_bwd", dhp, h, rh, sp["g_ple"], dout, True)
    tmw = _tile(d, 1024)
    g_wout = _matmul(
        "mm_gwout", y, dh16, pl.BlockSpec((s, tmw), lambda i, j, k: (0, i)), pl.BlockSpec((s, tn), lambda i, j, k: (0, j)),
        (d // tmw, gn, 1), None, jax.ShapeDtypeStruct((d, d), F32), pl.BlockSpec((tmw, tn), lambda i, j, k: (i, j)), ta=True)
    g_wout4 = g_wout.reshape(N_CHIPS, d // N_CHIPS, d)
    g_wgate = _matmul(
        "mm_gwgate", hp, dglin, pl.BlockSpec((s, tmw), lambda i, j, k: (0, i)), pl.BlockSpec((s, tn), lambda i, j, k: (0, j)),
        (d // tmw, gn, 1), None, jax.ShapeDtypeStruct((d, d), F32), pl.BlockSpec((tmw, tn), lambda i, j, k: (i, j)), ta=True,
        comm=_SwapHalves([g_wout4]) if dist else None)
    if dist:
        g_wgate, swapped_out = g_wgate
    g_wgate4 = g_wgate.reshape(N_CHIPS, d // N_CHIPS, d)
    g_wup = _matmul(
        "mm_gwup", p16, dup, pl.BlockSpec((s, dple), lambda i, j, k: (0, 0)), pl.BlockSpec((s, tn_up), lambda i, j, k: (0, j)),
        (1, d // tn_up, 1), None, jax.ShapeDtypeStruct((N_CHIPS, dple, d // N_CHIPS), F32),
        _shard_spec(dple, tn_up, per_up, 0, 1), ta=True)
    dy = _matmul(
        "mm_dy", dh16, wout, pl.BlockSpec((tm, d), lambda i, j, k: (i, 0)), pl.BlockSpec((tn, d), lambda i, j, k: (j, 0)),
        (gm, gn, 1), None, jax.ShapeDtypeStruct((s, d), F32), tile, tb=True,
        comm=_SwapHalves([g_wgate4, g_wup]) if dist else None)
    early = (g_wout4, g_wgate4, g_wup)
    if dist:
        core = dist["core"]
        dy, swapped_rest = dy
        early_sums = [_chip_sum("chip_sum_" + n, core, g, o)
                      for n, g, o in zip(LARGE[1:], early, list(swapped_out) + list(swapped_rest))]

    dproj, d_o, g_g_out_b = _gate_bwd(dy, yb, proj, rb, sp["g_out_b"], d)
    dproj, g_g_q, g_g_k, dbias, early_got = _attn_bwd(
        dproj, d_o, yb, lse, proj, sp["g_q"], sp["g_k"], sp["rel_bias"], d,
        comm=_ExchangeChips([s16 for _, s16 in early_sums]) if dist else None)
    dproj, g_w_s, dz_sum, g_ln_g, g_ln_b, g_g_out_a = _gmlp_bwd(
        dproj, dy, proj, z, vn, mu_v, rs_v, ra, sp["w_s"], sp["ln_v_g"], sp["g_out_a"], d)

    if dist:
        tmw_half = _tile(d // 2, 1024)
        n_half = (d // 2) // tmw_half

        def half_of_gwin(name, own, comm):
            def rows(i, j, k, core_ref):
                return (0, (core_ref[0] if own else 1 - core_ref[0]) * n_half + i)
            return _matmul(
                name, hn, dproj, pl.BlockSpec((s, tmw_half), rows),
                pl.BlockSpec((s, tn_in), lambda i, j, k, core_ref: (0, j)),
                (n_half, d_in // tn_in, 1), None, jax.ShapeDtypeStruct((N_CHIPS, d // 2, c4), F32),
                pl.BlockSpec((None, tmw_half, tn_in), lambda i, j, k, core_ref: (j // per_in, i, j % per_in)), ta=True,
                comm=comm, prefetch=(core,))

        g_theirs = half_of_gwin("mm_gwin_theirs", False, None)
        g_mine, (g_from_sibling,) = half_of_gwin("mm_gwin_mine", True, _SwapHalves([g_theirs], whole=True))
        win_sum = _chip_sum("chip_sum_w_in", core, g_mine, g_from_sibling)
    else:
        g_win = _matmul(
            "mm_gwin", hn, dproj, pl.BlockSpec((s, tmw), lambda i, j, k: (0, i)),
            pl.BlockSpec((s, tn_in), lambda i, j, k: (0, j)),
            (d // tmw, d_in // tn_in, 1), None, jax.ShapeDtypeStruct((N_CHIPS, d, c4), F32),
            _shard_spec(tmw, tn_in, per_in, 0, 1), ta=True)
    tk = c4
    tmh, tnh = _tile(s, 1024), _tile(d, 1024)
    dhn = _matmul(
        "mm_dhn", dproj, win4, pl.BlockSpec((tmh, tk), lambda i, j, k: (i, k)), _shard_spec(tnh, tk, 1, 1, 2),
        (s // tmh, d // tnh, d_in // tk), (tmh, tnh), jax.ShapeDtypeStruct((s, d), F32),
        pl.BlockSpec((tmh, tnh), lambda i, j, k: (i, j)), tb=True,
        comm=_ExchangeChips([win_sum[1]]) if dist else None)
    if dist:
        dhn, win_got = dhn
    grad_x, g_g_pre = _rms_bwd("rms_pre_bwd", dhn, xs, rx, sp["g_pre"], dh, False)

    small = {
        "g_pre": g_g_pre,
        "w_s": g_w_s,
        "b_s": jnp.sum(dz_sum.reshape(CHUNK, heads, HEAD_DIM), axis=-1).T,
        "ln_v_g": g_ln_g, "ln_v_b": g_ln_b,
        "g_q": g_g_q, "g_k": g_g_k,
        "rel_bias": dbias[:, :, 0].T,
        "g_out_a": g_g_out_a, "g_out_b": g_g_out_b,
        "g_ple": g_g_ple,
    }
    if not dist:
        return loss, grad_x, (g_win, *early), small
    sums32 = [win_sum[0]] + [s32 for s32, _ in early_sums]
    return loss, grad_x, (sums32, list(win_got) + list(early_got)), small


def _place():
    x, y, c = lax.axis_index("x"), lax.axis_index("y"), lax.axis_index("c")
    chips = [(1 - x, y), (x, 1 - y), (1 - x, 1 - y)]
    return x, y, c, chips


def _remote(src, dst, send_sems, recv_sems, k, to):
    return pltpu.make_async_remote_copy(src_ref=src, dst_ref=dst, send_sem=send_sems.at[k], recv_sem=recv_sems.at[k],
                                        device_id=to, device_id_type=MESH)


def _cast_bf16(name, chip, w):
    r, c = w.shape
    tm = _tile(r, 256)

    def body(chip_ref, w_ref, o_ref):
        del chip_ref
        o_ref[...] = w_ref[...].astype(BF16)

    return pl.pallas_call(
        body, name=name,
        grid_spec=pltpu.PrefetchScalarGridSpec(
            num_scalar_prefetch=1, grid=(r // tm,),
            in_specs=[pl.BlockSpec((tm, c), lambda i, chip_ref: (i, 0))],
            out_specs=pl.BlockSpec((None, tm, c), lambda i, chip_ref: (chip_ref[0], i, 0))),
        out_shape=jax.ShapeDtypeStruct((N_CHIPS, r, c), BF16), compiler_params=_params("parallel"),
    )(chip, w)


class _Gather:
    in_place = True

    def __init__(self, fulls):
        self.ins = list(fulls)
        self.out_shape = [jax.ShapeDtypeStruct(f.shape, f.dtype) for f in fulls]
        n = len(fulls)
        self.scratch = [pltpu.SemaphoreType.DMA((6 * n,)), pltpu.SemaphoreType.DMA((6 * n,))]

    @staticmethod
    def _sends(outs, sems):
        send_sems, recv_sems = sems
        x, y, c, chips = _place()
        cps = []
        for w, ref in enumerate(outs):
            half = ref.shape[1] // 2
            blk = ref.at[2 * x + y, pl.ds(c * half, half)]
            cps += [_remote(blk, blk, send_sems, recv_sems, 6 * w + q, (*chip, c)) for q, chip in enumerate(chips)]
        return cps

    def start(self, ins, outs, sems):
        for cp in self._sends(outs, sems):
            cp.start()

    def finish(self, ins, outs, sems):
        send_sems, recv_sems = sems
        x, y, c, chips = _place()
        sibling = (x, y, 1 - c)
        forwards = []
        for w, ref in enumerate(outs):
            half = ref.shape[1] // 2
            for q, chip in enumerate(chips):
                blk = ref.at[2 * chip[0] + chip[1], pl.ds(c * half, half)]
                _remote(blk, blk, send_sems, recv_sems, 6 * w + q, sibling).wait_recv()
                fwd = _remote(blk, blk, send_sems, recv_sems, 6 * w + 3 + q, sibling)
                fwd.start()
                forwards.append(fwd)
        for w, ref in enumerate(outs):
            half = ref.shape[1] // 2
            for q, chip in enumerate(chips):
                blk = ref.at[2 * chip[0] + chip[1], pl.ds((1 - c) * half, half)]
                _remote(blk, blk, send_sems, recv_sems, 6 * w + 3 + q, sibling).wait_recv()
        for cp in self._sends(outs, sems) + forwards:
            cp.wait_send()


class _ExchangeChips:
    in_place = False

    def __init__(self, sums16):
        self.ins = list(sums16)
        self.out_shape = [jax.ShapeDtypeStruct(g.shape, g.dtype) for g in sums16]
        n = len(sums16)
        self.scratch = [pltpu.SemaphoreType.DMA((3 * n,)), pltpu.SemaphoreType.DMA((3 * n,))]

    @staticmethod
    def _sends(ins, outs, sems):
        send_sems, recv_sems = sems
        x, y, c, chips = _place()
        return [_remote(ins[w].at[2 * chip[0] + chip[1]], outs[w].at[2 * x + y], send_sems, recv_sems, 3 * w + q, (*chip, c))
                for w in range(len(ins)) for q, chip in enumerate(chips)]

    def start(self, ins, outs, sems):
        for cp in self._sends(ins, outs, sems):
            cp.start()

    def finish(self, ins, outs, sems):
        send_sems, recv_sems = sems
        x, y, c, chips = _place()
        for w in range(len(ins)):
            for q, chip in enumerate(chips):
                blk = outs[w].at[2 * chip[0] + chip[1]]
                _remote(blk, blk, send_sems, recv_sems, 3 * w + q, (*chip, c)).wait_recv()
        for cp in self._sends(ins, outs, sems):
            cp.wait_send()


def _mm_in_gather(chip, hn, win4):
    s, d = hn.shape
    c4 = win4.shape[2]
    tm, tn = _tile(s, 1024), _tile(c4, 1024)
    per = c4 // tn
    nj, gm = N_CHIPS * per, s // tm
    half = d // 2

    def block_of(j, chip_ref):
        o = j // per
        return chip_ref[0] ^ (((o & 1) << 1) | (o >> 1))

    i_late = max(gm - 2, 0)

    def body(chip_ref, hn_ref, w_any, proj_ref, w_ref, wbuf, wsem, send_sems, recv_sems):
        del w_any
        j, i = pl.program_id(0), pl.program_id(1)
        x, y, c, chips = _place()
        sibling = (x, y, 1 - c)

        def region(k, rows_half, t):
            return w_ref.at[k, pl.ds(rows_half * half, half), pl.ds(t * tn, tn)]

        def send_mine(q, t):
            blk = region(2 * x + y, c, t)
            return _remote(blk, blk, send_sems, recv_sems, q * per + t, (*chips[q], c))

        def forward(q, t):
            blk = region(2 * chips[q][0] + chips[q][1], c, t)
            return _remote(blk, blk, send_sems, recv_sems, (3 + q) * per + t, sibling)

        def fetch(jj, mine, slot):
            rows = pl.ds((c if mine else 1 - c) * half, half)
            cols = pl.ds(pl.multiple_of((jj % per) * tn, LANE), tn)
            return pltpu.make_async_copy(w_ref.at[block_of(jj, chip_ref), rows, cols], wbuf.at[slot, rows],
                                         wsem.at[2 * slot + (0 if mine else 1)])

        def foreign_tiles():
            return [(o, t) for o in range(1, N_CHIPS) for t in range(per)]

        @pl.when((j == 0) & (i == 0))
        def _():
            for t in range(per):
                send_mine(0, t).start()
                send_mine(1, t).start()
            fetch(0, True, 0).start()
            fetch(0, False, 0).start()

        nxt = j + 1

        @pl.when((i == 0) & (nxt < nj))
        def _():
            for o, t in foreign_tiles():
                @pl.when(nxt == o * per + t)
                def _():
                    blk = region(2 * chips[o - 1][0] + chips[o - 1][1], c, t)
                    _remote(blk, blk, send_sems, recv_sems, (o - 1) * per + t, sibling).wait_recv()
                    forward(o - 1, t).start()
                    if (o, t) == (1, per - 1):
                        for tt in range(per):
                            send_mine(2, tt).start()

            fetch(nxt, True, nxt % 2).start()

            @pl.when(nxt < per)
            def _():
                fetch(nxt, False, nxt % 2).start()

        @pl.when((i == i_late) & (nxt < nj))
        def _():
            for o, t in foreign_tiles():
                @pl.when(nxt == o * per + t)
                def _():
                    blk = region(2 * chips[o - 1][0] + chips[o - 1][1], 1 - c, t)
                    _remote(blk, blk, send_sems, recv_sems, (3 + o - 1) * per + t, sibling).wait_recv()
                    fetch(nxt, False, nxt % 2).start()

        @pl.when(i == 0)
        def _():
            fetch(j, True, j % 2).wait()
            fetch(j, False, j % 2).wait()

        proj_ref[...] = _dot(hn_ref[...], wbuf[j % 2])

        @pl.when((j == nj - 1) & (i == gm - 1))
        def _():
            for q in range(3):
                for t in range(per):
                    send_mine(q, t).wait_send()
                    forward(q, t).wait_send()

    return pl.pallas_call(
        body, name="mm_in_gather",
        grid_spec=pltpu.PrefetchScalarGridSpec(
            num_scalar_prefetch=1, grid=(nj, gm),
            in_specs=[pl.BlockSpec((tm, d), lambda j, i, chip_ref: (i, 0)), ANY],
            out_specs=[pl.BlockSpec((tm, tn), lambda j, i, chip_ref: (i, block_of(j, chip_ref) * per + j % per)), ANY],
            scratch_shapes=[pltpu.VMEM((2, d, tn), BF16), pltpu.SemaphoreType.DMA((4,)),
                            pltpu.SemaphoreType.DMA((6 * per,)), pltpu.SemaphoreType.DMA((6 * per,))]),
        out_shape=[jax.ShapeDtypeStruct((s, N_CHIPS * c4), F32), jax.ShapeDtypeStruct(win4.shape, win4.dtype)],
        input_output_aliases={2: 1},
        compiler_params=_params("arbitrary", "arbitrary"),
    )(chip, hn, win4)


class _SwapHalves:
    in_place = False

    def __init__(self, grads, whole=False):
        self.ins = list(grads)
        self.whole = whole
        self.out_shape = [jax.ShapeDtypeStruct((N_CHIPS, g.shape[1] // (1 if whole else 2), g.shape[2]), g.dtype)
                          for g in grads]
        self.scratch = [pltpu.SemaphoreType.DMA((len(grads),)), pltpu.SemaphoreType.DMA((len(grads),))]

    def _copies(self, ins, outs, sems):
        x, y, c, _ = _place()
        cps = []
        for w in range(len(ins)):
            half = ins[w].shape[1] // 2
            src = ins[w] if self.whole else ins[w].at[:, pl.ds((1 - c) * half, half)]
            cps.append(_remote(src, outs[w], sems[0], sems[1], w, (x, y, 1 - c)))
        return cps

    def start(self, ins, outs, sems):
        for cp in self._copies(ins, outs, sems):
            cp.start()

    def finish(self, ins, outs, sems):
        for cp in self._copies(ins, outs, sems):
            cp.wait()


def _chip_sum(name, core, grad, got):
    _, half, c = got.shape
    th = _tile(half, 128)
    n = half // th
    skip = n if grad.shape[1] != half else 0

    def body(core_ref, g_ref, o_ref, s32_ref, s16_ref):
        del core_ref
        v = g_ref[...] + o_ref[...]
        s32_ref[...] = v
        s16_ref[...] = v.astype(BF16)

    blk = pl.BlockSpec((None, th, c), lambda k, i, core_ref: (k, i, 0))
    return pl.pallas_call(
        body, name=name,
        grid_spec=pltpu.PrefetchScalarGridSpec(
            num_scalar_prefetch=1, grid=(N_CHIPS, n),
            in_specs=[pl.BlockSpec((None, th, c), lambda k, i, core_ref: (k, core_ref[0] * skip + i, 0)), blk],
            out_specs=[blk, blk]),
        out_shape=[jax.ShapeDtypeStruct((N_CHIPS, half, c), F32), jax.ShapeDtypeStruct((N_CHIPS, half, c), BF16)],
        compiler_params=_params("parallel", "parallel"),
    )(core, grad, got)


def _riding(comm, grid, body, n_pre, n_in, n_out):
    if comm is None:
        return body
    c_in, c_out, c_scr = len(comm.ins), len(comm.out_shape), len(comm.scratch)

    def wrapped(*refs):
        pre, rest = refs[:n_pre], refs[n_pre:]
        ins, rest = rest[:n_in], rest[n_in:]
        cin, rest = rest[:c_in], rest[c_in:]
        outs, rest = rest[:n_out], rest[n_out:]
        cout, rest = rest[:c_out], rest[c_out:]
        own, sems = rest[:len(rest) - c_scr], rest[len(rest) - c_scr:]
        ids = [pl.program_id(ax) for ax in range(len(grid))]
        first, last = ids[0] == 0, ids[0] == grid[0] - 1
        for ax in range(1, len(grid)):
            first, last = first & (ids[ax] == 0), last & (ids[ax] == grid[ax] - 1)

        @pl.when(first)
        def _():
            comm.start(cin, cout, sems)

        body(*pre, *ins, *outs, *own)

        @pl.when(last)
        def _():
            comm.finish(cin, cout, sems)

    return wrapped


def _total(name, chip, core, sum32, got16, comm=None):
    _, half, c = sum32.shape
    th = _tile(half, 128)
    n = half // th

    def body(chip_ref, core_ref, own_ref, a_ref, b_ref, c_ref, o_ref):
        del chip_ref, core_ref
        o_ref[...] = ((own_ref[...] + a_ref[...].astype(F32)) + b_ref[...].astype(F32)) + c_ref[...].astype(F32)

    def other(step):
        return pl.BlockSpec((None, th, c), lambda i, chip_ref, core_ref: ((chip_ref[0] + step) % N_CHIPS, i, 0))

    c_ins = list(comm.ins) if comm else []
    res = pl.pallas_call(
        _riding(comm, (n,), body, 2, 4, 1), name=name,
        grid_spec=pltpu.PrefetchScalarGridSpec(
            num_scalar_prefetch=2, grid=(n,),
            in_specs=[other(0), other(1), other(2), other(3)] + [ANY] * len(c_ins),
            out_specs=[pl.BlockSpec((th, c), lambda i, chip_ref, core_ref: (core_ref[0] * n + i, 0))]
            + [ANY] * len(c_ins),
            scratch_shapes=list(comm.scratch) if comm else []),
        out_shape=[jax.ShapeDtypeStruct((2 * half, c), F32)] + (list(comm.out_shape) if comm else []),
        input_output_aliases={6 + i: 1 + i for i in range(len(c_ins))} if comm else {},
        compiler_params=_params("arbitrary" if comm else "parallel"),
    )(chip, core, sum32, got16, got16, got16, *c_ins)
    return (res[0], res[1:]) if comm else res[0]


class _JoinHalves:
    in_place = True

    def __init__(self, blocks):
        self.ins = list(blocks)
        self.out_shape = [jax.ShapeDtypeStruct(t.shape, t.dtype) for t in blocks]
        self.scratch = [pltpu.SemaphoreType.DMA((len(blocks),)), pltpu.SemaphoreType.DMA((len(blocks),))]

    @staticmethod
    def _copy(ref, w, mine, sems):
        x, y, c, _ = _place()
        half = ref.shape[0] // 2
        rows = ref.at[pl.ds((c if mine else 1 - c) * half, half)]
        return _remote(rows, rows, sems[0], sems[1], w, (x, y, 1 - c))

    def start(self, ins, outs, sems):
        for w, ref in enumerate(outs):
            self._copy(ref, w, True, sems).start()

    def finish(self, ins, outs, sems):
        for w, ref in enumerate(outs):
            self._copy(ref, w, True, sems).wait_send()
            self._copy(ref, w, False, sems).wait_recv()


class _GatherSmall:
    in_place = True

    def __init__(self, full):
        self.ins = [full]
        self.out_shape = [jax.ShapeDtypeStruct(full.shape, full.dtype)]
        self.scratch = [pltpu.SemaphoreType.DMA((7,)), pltpu.SemaphoreType.DMA((7,))]

    @staticmethod
    def _copy(ref, sems, k, block, to):
        m_per = ref.shape[0] // N_DEV
        px, py, pc = block
        rows = ref.at[pl.ds((4 * px + 2 * py + pc) * m_per, m_per), :]
        return _remote(rows, rows, sems[0], sems[1], k, to)

    def _first(self, ref, sems):
        x, y, c, chips = _place()
        me = (x, y, c)
        return [self._copy(ref, sems, 0, me, (x, y, 1 - c))] + \
               [self._copy(ref, sems, 1 + j, me, (*chip, c)) for j, chip in enumerate(chips)]

    def start(self, ins, outs, sems):
        for cp in self._first(outs[0], sems):
            cp.start()

    def finish(self, ins, outs, sems):
        ref = outs[0]
        x, y, c, chips = _place()
        me, sibling = (x, y, c), (x, y, 1 - c)
        passed = [self._copy(ref, sems, 4 + j, (*chip, c), sibling) for j, chip in enumerate(chips)]
        for j, chip in enumerate(chips):
            self._copy(ref, sems, 1 + j, (*chip, c), me).wait_recv()
            passed[j].start()
        self._copy(ref, sems, 0, sibling, me).wait_recv()
        for j, chip in enumerate(chips):
            self._copy(ref, sems, 4 + j, (*chip, 1 - c), me).wait_recv()
        for cp in self._first(ref, sems) + passed:
            cp.wait_send()


def _adamw_math(w, g, m, v):
    m = ADAM_B1 * m + (1.0 - ADAM_B1) * g
    v = ADAM_B2 * v + (1.0 - ADAM_B2) * (g * g)
    m_hat = m / (1.0 - ADAM_B1 ** ADAM_STEP)
    v_hat = v / (1.0 - ADAM_B2 ** ADAM_STEP)
    delta = -ADAM_LR * (m_hat / (jnp.sqrt(v_hat) + ADAM_EPS) + ADAM_WD * w)
    return delta, m, v


def _adamw(name, w, g, m, v, comm=None):
    r, c = w.shape
    tm = _tile(r, 128)
    c_ins = list(comm.ins) if comm else []

    def body(w_ref, g_ref, m_ref, v_ref, g_out, d_out, m_out, v_out):
        g = g_ref[...]
        g_out[...] = g
        d_out[...], m_out[...], v_out[...] = _adamw_math(w_ref[...], g, m_ref[...], v_ref[...])

    spec = pl.BlockSpec((tm, c), lambda i: (i, 0))
    res = pl.pallas_call(
        _riding(comm, (r // tm,), body, 0, 4, 4), name=name, grid=(r // tm,),
        in_specs=[spec] * 4 + [ANY] * len(c_ins), out_specs=[spec] * 4 + [ANY] * len(c_ins),
        out_shape=[jax.ShapeDtypeStruct((r, c), F32)] * 4 + (list(comm.out_shape) if comm else []),
        scratch_shapes=list(comm.scratch) if comm else [],
        input_output_aliases={4 + i: 4 + i for i in range(len(c_ins))} if comm else {},
        compiler_params=_params("arbitrary" if comm else "parallel"),
    )(w, g, m, v, *c_ins)
    return (res[:4], res[4:]) if comm else res


def _adamw_small(gathered, w, m, v):
    rows = w.shape[0]

    def body(all_ref, w_ref, m_ref, v_ref, g_out, d_out, m_out, v_out):
        g = all_ref[0:rows, :]
        for dev in range(1, N_DEV):
            g = g + all_ref[dev * rows:(dev + 1) * rows, :]
        g_out[...] = g
        d_out[...], m_out[...], v_out[...] = _adamw_math(w_ref[...], g, m_ref[...], v_ref[...])

    return pl.pallas_call(
        body, name="adamw_small", out_shape=[jax.ShapeDtypeStruct(w.shape, F32)] * 4, compiler_params=_params(),
    )(gathered, w, m, v)


SMALL = ("g_pre", "w_s", "b_s", "ln_v_g", "ln_v_b", "g_q", "g_k", "rel_bias", "g_out_a", "g_out_b", "g_ple")
LARGE = ("w_in", "w_out", "w_ple_gate", "w_ple_up")
WEIGHTS = ("g_pre", "w_in", "w_s", "b_s", "ln_v_g", "ln_v_b", "g_q", "g_k", "rel_bias", "g_out_a", "g_out_b", "w_out",
           "g_ple", "w_ple_gate", "w_ple_up")


def _pack(parts):
    flat = jnp.concatenate([parts[n].reshape(-1).astype(F32) for n in SMALL])
    rows = -(-flat.shape[0] // (8 * LANE)) * 8
    return jnp.pad(flat, (0, rows * LANE - flat.shape[0])).reshape(rows, LANE)


def _unpack(pack, like):
    flat = pack.reshape(-1)
    out, at = {}, 0
    for n in SMALL:
        size = math.prod(like[n].shape)
        out[n] = flat[at:at + size].reshape(like[n].shape)
        at += size
    return out


def kernel(x, p, g_pre, w_in, w_s, b_s, ln_v_g, ln_v_b, g_q, g_k, rel_bias, g_out_a, g_out_b, w_out, g_ple, w_ple_gate, w_ple_up, loss_target, m_g_pre, m_w_in, m_w_s, m_b_s, m_ln_v_g, m_ln_v_b, m_g_q, m_g_k, m_rel_bias, m_g_out_a, m_g_out_b, m_w_out, m_g_ple, m_w_ple_gate, m_w_ple_up, v_g_pre, v_w_in, v_w_s, v_b_s, v_ln_v_g, v_ln_v_b, v_g_q, v_g_k, v_rel_bias, v_g_out_a, v_g_out_b, v_w_out, v_g_ple, v_w_ple_gate, v_w_ple_up):
    given = dict(locals())
    weights = {n: given[n] for n in WEIGHTS}
    mom_m = {n: given["m_" + n] for n in WEIGHTS}
    mom_v = {n: given["v_" + n] for n in WEIGHTS}
    xs, ps, tgt = x[0], p[0, 0], loss_target[0]
    d = xs.shape[1]

    core = lax.axis_index("c").astype(jnp.int32).reshape(1)
    chip = (2 * lax.axis_index("x") + lax.axis_index("y")).astype(jnp.int32).reshape(1)

    win4, wout4, wgate4, wup4 = [_cast_bf16("cast_" + n, chip, weights[n][0]) for n in LARGE]

    sp = {
        "g_pre": g_pre, "w_s": w_s[0], "b_s": b_s[0], "ln_v_g": ln_v_g, "ln_v_b": ln_v_b, "g_q": g_q, "g_k": g_k,
        "rel_bias": rel_bias, "g_out_a": g_out_a, "g_out_b": g_out_b, "g_ple": g_ple,
    }
    loss_local, grad_x, (sums32, got16), small = _local_step(xs, ps, tgt, sp, win4, wout4, wgate4, wup4,
                                                             dist={"chip": chip, "core": core})
    loss = lax.psum(loss_local, MESH_AXES)

    own = dict(zip(LARGE, sums32))
    got = dict(zip(LARGE, got16))
    halves = [_total("total_" + n, chip, core, own[n], got[n]) for n in LARGE[1:]]
    half_in, joined = _total("total_w_in", chip, core, own["w_in"], got["w_in"], comm=_JoinHalves(halves))
    grads = dict(zip(LARGE[1:], joined))

    my_pack = _pack(small)
    device = 2 * chip[0] + core[0]
    everyone = lax.dynamic_update_slice(jnp.zeros((N_DEV * my_pack.shape[0], LANE), F32), my_pack,
                                        (device * my_pack.shape[0], 0))

    def update(n, comm=None):
        return _adamw("adamw_" + n, weights[n][0], grads[n], mom_m[n][0], mom_v[n][0], comm=comm)

    results = {}
    results["w_out"], (grads["w_in"],) = update("w_out", _JoinHalves([half_in]))
    results["w_ple_gate"] = update("w_ple_gate")
    results["w_ple_up"] = update("w_ple_up")
    results["w_in"], (gathered,) = update("w_in", _GatherSmall(everyone))
    out_g, out_d, out_m, out_v = {}, {}, {}, {}
    for n in LARGE:
        out_g[n], out_d[n], out_m[n], out_v[n] = [r[None] for r in results[n]]

    pg, pd, pm, pv = _adamw_small(gathered, _pack(weights), _pack(mom_m), _pack(mom_v))
    for packed, out in ((pg, out_g), (pd, out_d), (pm, out_m), (pv, out_v)):
        out.update(_unpack(packed, weights))

    return (loss, grad_x[None], *[out_g[n] for n in WEIGHTS], *[out_d[n] for n in WEIGHTS],
            *[out_m[n] for n in WEIGHTS], *[out_v[n] for n in WEIGHTS])
```

```python
import functools
import math

import numpy as np

import jax
import jax.numpy as jnp
from jax import lax
from jax.experimental import pallas as pl
from jax.experimental.pallas import tpu as pltpu

F32 = jnp.float32
BF16 = jnp.bfloat16

HEAD_DIM = 128
CHUNK = 128
BLK = 128
DILATED = ((128, 1), (512, 4), (2048, 16))
NUM_BUCKETS = 32
MAX_DISTANCE = 2048
ATTN_GROUP = 4
EPS = 1e-6
NEG_INF = -1e30
N_CHIPS = 4
N_DEV = 8
MESH_AXES = ("x", "y", "c")

ADAM_LR = 0.001
ADAM_B1 = 0.9
ADAM_B2 = 0.999
ADAM_EPS = 1e-08
ADAM_WD = 0.01
ADAM_STEP = 10

V7X_VMEM_LIMIT = 56 * 1024 * 1024
LANE = 128
MESH = pl.DeviceIdType.MESH
ANY = pl.BlockSpec(memory_space=pl.ANY)


def _params(*sem):
    return pltpu.CompilerParams(dimension_semantics=sem or None, vmem_limit_bytes=V7X_VMEM_LIMIT)


def _tile(n, target):
    if n <= target:
        return n
    t = (target // LANE) * LANE
    while t > LANE and n % t:
        t -= LANE
    assert n % t == 0, (n, target)
    return t


def _gelu(x):
    return 0.5 * x * (1.0 + lax.erf(x * (1.0 / math.sqrt(2.0))))


def _gelu_grad(x):
    return 0.5 * (1.0 + lax.erf(x * (1.0 / math.sqrt(2.0)))) + x * jnp.exp(-0.5 * x * x) * (1.0 / math.sqrt(2.0 * math.pi))


def _silu_and_grad(x):
    s = jax.nn.sigmoid(x)
    return x * s, s * (1.0 + x * (1.0 - s))


def _dot(a, b, ta=False, tb=False):
    return lax.dot_general(a, b, (((0 if ta else 1,), (1 if tb else 0,)), ((), ())), preferred_element_type=F32)


def _colsum8(v):
    return jnp.sum(v.reshape(v.shape[0] // 8, 8, v.shape[1]), axis=0)


def _matmul(name, a, b, a_spec, b_spec, grid, acc_shape, out_shape, out_specs, ta=False, tb=False,
            extras=(), extra_specs=(), epilogue=None, comm=None, prefetch=()):
    nk = grid[2]
    n_pre = len(prefetch)
    n_extra = len(extras)
    single = not isinstance(out_shape, (tuple, list))
    outs_shape = (out_shape,) if single else tuple(out_shape)
    outs_specs = (out_specs,) if single else tuple(out_specs)
    n_out = len(outs_shape)
    c_ins = list(comm.ins) if comm else []
    c_outs = list(comm.out_shape) if comm else []
    c_scratch = list(comm.scratch) if comm else []
    n_cin, n_cout = len(c_ins), len(c_outs)

    def finish(acc, extra_refs, out_refs):
        if epilogue is None:
            out_refs[0][...] = acc.astype(out_refs[0].dtype)
        else:
            epilogue(acc, extra_refs, out_refs)

    def body(*refs):
        a_ref, b_ref, *rest = refs[n_pre:]
        extra_refs = rest[:n_extra]
        cin_refs = rest[n_extra:n_extra + n_cin]
        out_refs = rest[n_extra + n_cin:n_extra + n_cin + n_out]
        cout_refs = rest[n_extra + n_cin + n_out:n_extra + n_cin + n_out + n_cout]
        scratch_refs = rest[n_extra + n_cin + n_out + n_cout:]
        ids = [pl.program_id(ax) for ax in range(3)]
        if comm:
            sems = scratch_refs[len(scratch_refs) - len(c_scratch):]

            @pl.when((ids[0] == 0) & (ids[1] == 0) & (ids[2] == 0))
            def _():
                comm.start(cin_refs, cout_refs, sems)

        if nk == 1:
            finish(_dot(a_ref[...], b_ref[...], ta, tb), extra_refs, out_refs)
        else:
            acc_ref = scratch_refs[0]

            @pl.when(ids[2] == 0)
            def _():
                acc_ref[...] = jnp.zeros_like(acc_ref)

            acc_ref[...] += _dot(a_ref[...], b_ref[...], ta, tb)

            @pl.when(ids[2] == nk - 1)
            def _():
                finish(acc_ref[...], extra_refs, out_refs)

        if comm:
            @pl.when((ids[0] == grid[0] - 1) & (ids[1] == grid[1] - 1) & (ids[2] == nk - 1))
            def _():
                comm.finish(cin_refs, cout_refs, sems)

    scratch = ([] if nk == 1 else [pltpu.VMEM(acc_shape, F32)]) + c_scratch
    aliases = {n_pre + 2 + n_extra + i: n_out + i for i in range(n_cin)} if (comm and comm.in_place) else {}
    res = pl.pallas_call(
        body, name=name,
        grid_spec=pltpu.PrefetchScalarGridSpec(
            num_scalar_prefetch=n_pre, grid=grid,
            in_specs=[a_spec, b_spec, *extra_specs] + [ANY] * n_cin,
            out_specs=list(outs_specs) + [ANY] * n_cout, scratch_shapes=scratch),
        out_shape=list(outs_shape) + c_outs,
        input_output_aliases=aliases,
        compiler_params=_params(*(("arbitrary",) * 3 if comm else ("parallel", "parallel", "arbitrary"))),
    )(*prefetch, a, b, *extras, *c_ins)
    if comm:
        main = res[:n_out]
        return (main[0] if single else main), res[n_out:]
    return res[0] if single else res


def _shard_spec(rows, cols, per_shard, row_axis, col_axis):
    def index(i, j, k):
        g = (i, j, k)
        return (g[col_axis] // per_shard, g[row_axis], g[col_axis] % per_shard)
    return pl.BlockSpec((None, rows, cols), index)


def _rms_fwd(name, x, g):
    s, d = x.shape
    tm = _tile(s, 256)

    def body(x_ref, g_ref, y_ref, r_ref):
        xf = x_ref[...]
        r = lax.rsqrt(jnp.mean(xf * xf, axis=-1, keepdims=True) + EPS)
        y_ref[...] = (xf * r * g_ref[...]).astype(BF16)
        r_ref[...] = r

    return pl.pallas_call(
        body, name=name, grid=(s // tm,),
        in_specs=[pl.BlockSpec((tm, d), lambda i: (i, 0)), pl.BlockSpec((1, d), lambda i: (0, 0))],
        out_specs=(pl.BlockSpec((tm, d), lambda i: (i, 0)), pl.BlockSpec((tm, 1), lambda i: (i, 0))),
        out_shape=(jax.ShapeDtypeStruct((s, d), BF16), jax.ShapeDtypeStruct((s, 1), F32)),
        compiler_params=_params("parallel"),
    )(x, g)


def _rms_bwd(name, dy, x, r, g, skip, with_bf16):
    s, d = x.shape
    tm = _tile(s, 256)
    n = s // tm

    def body(dy_ref, x_ref, r_ref, g_ref, skip_ref, *outs):
        dx_ref = outs[0]
        dg_ref = outs[-2]
        acc_ref = outs[-1]
        i = pl.program_id(0)
        dyv, xv, rv = dy_ref[...], x_ref[...], r_ref[...]

        @pl.when(i == 0)
        def _():
            acc_ref[...] = jnp.zeros_like(acc_ref)

        acc_ref[...] += _colsum8(dyv * xv * rv)
        dg = dyv * g_ref[...]
        dx = skip_ref[...] + rv * (dg - xv * (rv * rv) * jnp.mean(dg * xv, axis=-1, keepdims=True))
        dx_ref[...] = dx
        if with_bf16:
            outs[1][...] = dx.astype(BF16)

        @pl.when(i == n - 1)
        def _():
            dg_ref[...] = jnp.sum(acc_ref[...], axis=0, keepdims=True)

    row = pl.BlockSpec((tm, d), lambda i: (i, 0))
    vec = pl.BlockSpec((1, d), lambda i: (0, 0))
    out_specs = [row] + ([row] if with_bf16 else []) + [vec]
    out_shape = [jax.ShapeDtypeStruct((s, d), F32)] + ([jax.ShapeDtypeStruct((s, d), BF16)] if with_bf16 else []) \
        + [jax.ShapeDtypeStruct((1, d), F32)]
    return pl.pallas_call(
        body, name=name, grid=(n,),
        in_specs=[row, row, pl.BlockSpec((tm, 1), lambda i: (i, 0)), vec, row],
        out_specs=out_specs, out_shape=out_shape, scratch_shapes=[pltpu.VMEM((8, d), F32)],
        compiler_params=_params("arbitrary"),
    )(dy, x, r, g, skip)


def _causal(w):
    t = lax.broadcasted_iota(jnp.int32, w.shape, 0)
    s_ = lax.broadcasted_iota(jnp.int32, w.shape, 1)
    return jnp.where(t >= s_, w, 0.0)


def _gmlp_fwd(proj, w_s, b_st, ln_g, ln_b, g_out, d_model):
    s = proj.shape[0]
    wa = d_model // 2
    groups = wa // HEAD_DIM
    tm = _tile(s, 256)
    n_chunks = tm // CHUNK

    def body(au_ref, av_ref, az_ref, ws_ref, bst_ref, lng_ref, lnb_ref, go_ref,
             y_ref, z_ref, vn_ref, mu_ref, rs_ref, ra_ref):
        gv = _gelu(av_ref[...])
        mu = jnp.mean(gv, axis=-1, keepdims=True)
        xc = gv - mu
        rs = lax.rsqrt(jnp.mean(xc * xc, axis=-1, keepdims=True) + EPS)
        vn = (xc * rs * lng_ref[...] + lnb_ref[...]).astype(BF16)
        vn_ref[...] = vn
        mu_ref[...] = mu
        rs_ref[...] = rs
        for g in range(groups):
            wm = _causal(ws_ref[g]).astype(BF16)
            cols = slice(g * HEAD_DIM, (g + 1) * HEAD_DIM)
            for ch in range(n_chunks):
                rows = slice(ch * CHUNK, (ch + 1) * CHUNK)
                z_ref[rows, cols] = _dot(wm, vn_ref[rows, cols]) + bst_ref[:, g:g + 1]
        ya = _gelu(au_ref[...]) * z_ref[...]
        ra = lax.rsqrt(jnp.mean(ya * ya, axis=-1, keepdims=True) + EPS)
        ra_ref[...] = ra
        sz, _ = _silu_and_grad(az_ref[...])
        y_ref[...] = (ya * ra * go_ref[...] * sz).astype(BF16)

    def col(j):
        return pl.BlockSpec((tm, wa), lambda i: (i, j))
    vec = pl.BlockSpec((1, wa), lambda i: (0, 0))
    stat = pl.BlockSpec((tm, 1), lambda i: (i, 0))
    return pl.pallas_call(
        body, name="gmlp_fwd", grid=(s // tm,),
        in_specs=[col(0), col(1), col(2),
                  pl.BlockSpec((groups, CHUNK, CHUNK), lambda i: (0, 0, 0)),
                  pl.BlockSpec((CHUNK, groups), lambda i: (0, 0)), vec, vec, vec],
        out_specs=(col(0), col(0), col(0), stat, stat, stat),
        out_shape=(jax.ShapeDtypeStruct((s, d_model), BF16), jax.ShapeDtypeStruct((s, wa), F32),
                   jax.ShapeDtypeStruct((s, wa), BF16), jax.ShapeDtypeStruct((s, 1), F32),
                   jax.ShapeDtypeStruct((s, 1), F32), jax.ShapeDtypeStruct((s, 1), F32)),
        compiler_params=_params("parallel"),
    )(proj, proj, proj, w_s, b_st, ln_g, ln_b, g_out)


def _gmlp_bwd(dproj, dy, proj, z, vn, mu, rs, ra, w_s, ln_g, g_out, d_model):
    s = proj.shape[0]
    wa = d_model // 2
    groups = wa // HEAD_DIM
    tm = _tile(s, 256)
    n_chunks = tm // CHUNK
    n = s // tm

    def causal_stack(w):
        t = lax.broadcasted_iota(jnp.int32, w.shape, 1)
        s_ = lax.broadcasted_iota(jnp.int32, w.shape, 2)
        return jnp.where(t >= s_, w, 0.0)

    def body(dproj_in, dy_ref, au_ref, av_ref, az_ref, z_ref, vn_ref, mu_ref, rs_ref, ra_ref, ws_ref, lng_ref, go_ref,
             dp_ref, gws_ref, dzs_ref, glg_ref, glb_ref, ggo_ref,
             dz_s, dvn_s, acc_lg, acc_lb, acc_go):
        del dproj_in
        i = pl.program_id(0)

        @pl.when(i == 0)
        def _():
            gws_ref[...] = jnp.zeros_like(gws_ref)
            dzs_ref[...] = jnp.zeros_like(dzs_ref)
            acc_lg[...] = jnp.zeros_like(acc_lg)
            acc_lb[...] = jnp.zeros_like(acc_lb)
            acc_go[...] = jnp.zeros_like(acc_go)

        au, az, zv, rav = au_ref[...], az_ref[...], z_ref[...], ra_ref[...]
        u = _gelu(au)
        ya = u * zv
        sz, dsz = _silu_and_grad(az)
        dyv = dy_ref[...]
        dp_ref[:, 2 * wa:3 * wa] = (dyv * (ya * rav * go_ref[...]) * dsz).astype(BF16)
        dn = dyv * sz
        acc_go[...] += _colsum8(dn * ya * rav)
        dyg = dn * go_ref[...]
        dya = rav * (dyg - ya * (rav * rav) * jnp.mean(dyg * ya, axis=-1, keepdims=True))
        dp_ref[:, 0:wa] = (dya * zv * _gelu_grad(au)).astype(BF16)
        dz_s[...] = dya * u
        for ch in range(n_chunks):
            dzs_ref[...] += dz_s[ch * CHUNK:(ch + 1) * CHUNK, :]
        for g in range(groups):
            wm = _causal(ws_ref[g]).astype(BF16)
            cols = slice(g * HEAD_DIM, (g + 1) * HEAD_DIM)
            for ch in range(n_chunks):
                rows = slice(ch * CHUNK, (ch + 1) * CHUNK)
                dzb = dz_s[rows, cols].astype(BF16)
                gws_ref[g] += _dot(dzb, vn_ref[rows, cols], tb=True)
                dvn_s[rows, cols] = _dot(wm, dzb, ta=True)
        av = av_ref[...]
        xh = (_gelu(av) - mu_ref[...]) * rs_ref[...]
        dvn = dvn_s[...]
        acc_lb[...] += _colsum8(dvn)
        acc_lg[...] += _colsum8(dvn * xh)
        dxh = dvn * lng_ref[...]
        dgv = rs_ref[...] * (dxh - jnp.mean(dxh, axis=-1, keepdims=True) - xh * jnp.mean(dxh * xh, axis=-1, keepdims=True))
        dp_ref[:, wa:2 * wa] = (dgv * _gelu_grad(av)).astype(BF16)

        @pl.when(i == n - 1)
        def _():
            gws_ref[...] = causal_stack(gws_ref[...])
            glg_ref[...] = jnp.sum(acc_lg[...], axis=0, keepdims=True)
            glb_ref[...] = jnp.sum(acc_lb[...], axis=0, keepdims=True)
            ggo_ref[...] = jnp.sum(acc_go[...], axis=0, keepdims=True)

    def col(j):
        return pl.BlockSpec((tm, wa), lambda i: (i, j))
    vec = pl.BlockSpec((1, wa), lambda i: (0, 0))
    stat = pl.BlockSpec((tm, 1), lambda i: (i, 0))
    ws_spec = pl.BlockSpec((groups, CHUNK, CHUNK), lambda i: (0, 0, 0))
    d_in = dproj.shape[1]
    return pl.pallas_call(
        body, name="gmlp_bwd", grid=(n,),
        in_specs=[ANY, col(0), col(0), col(1), col(2), col(0), col(0), stat, stat, stat, ws_spec, vec, vec],
        out_specs=(pl.BlockSpec((tm, 3 * wa), lambda i: (i, 0)), ws_spec,
                   pl.BlockSpec((CHUNK, wa), lambda i: (0, 0)), vec, vec, vec),
        out_shape=(jax.ShapeDtypeStruct((s, d_in), BF16),
                   jax.ShapeDtypeStruct((groups, CHUNK, CHUNK), F32), jax.ShapeDtypeStruct((CHUNK, wa), F32))
        + (jax.ShapeDtypeStruct((1, wa), F32),) * 3,
        scratch_shapes=[pltpu.VMEM((tm, wa), F32), pltpu.VMEM((tm, wa), F32)] + [pltpu.VMEM((8, wa), F32)] * 3,
        input_output_aliases={0: 0},
        compiler_params=_params("arbitrary"),
    )(dproj, dy, proj, proj, proj, z, vn, mu, rs, ra, w_s, ln_g, g_out)


def _gate_fwd(y_in, yb, proj, g_out, d_model):
    s = proj.shape[0]
    wa = d_model // 2
    tm = _tile(s, 256)

    def body(y_any, yb_ref, bz_ref, go_ref, y_ref, rb_ref):
        del y_any
        ybv = yb_ref[...]
        rb = lax.rsqrt(jnp.mean(ybv * ybv, axis=-1, keepdims=True) + EPS)
        rb_ref[...] = rb
        sz, _ = _silu_and_grad(bz_ref[...])
        y_ref[...] = (ybv * rb * go_ref[...] * sz).astype(BF16)

    return pl.pallas_call(
        body, name="gate_b_fwd", grid=(s // tm,),
        in_specs=[ANY, pl.BlockSpec((tm, wa), lambda i: (i, 0)), pl.BlockSpec((tm, wa), lambda i: (i, 6)),
                  pl.BlockSpec((1, wa), lambda i: (0, 0))],
        out_specs=(pl.BlockSpec((tm, wa), lambda i: (i, 1)), pl.BlockSpec((tm, 1), lambda i: (i, 0))),
        out_shape=(jax.ShapeDtypeStruct((s, d_model), BF16), jax.ShapeDtypeStruct((s, 1), F32)),
        input_output_aliases={0: 0},
        compiler_params=_params("parallel"),
    )(y_in, yb, proj, g_out)


def _gate_bwd(dy, yb, proj, rb, g_out, d_model):
    s, d_in = proj.shape
    wa = d_model // 2
    tm = _tile(s, 256)
    n = s // tm

    def body(dy_ref, yb_ref, bz_ref, rb_ref, go_ref, dp_ref, do_ref, ggo_ref, acc):
        i = pl.program_id(0)

        @pl.when(i == 0)
        def _():
            acc[...] = jnp.zeros_like(acc)

        dyv, ybv, rbv = dy_ref[...], yb_ref[...], rb_ref[...]
        sz, dsz = _silu_and_grad(bz_ref[...])
        dp_ref[...] = (dyv * (ybv * rbv * go_ref[...]) * dsz).astype(BF16)
        dn = dyv * sz
        acc[...] += _colsum8(dn * ybv * rbv)
        dyg = dn * go_ref[...]
        do_ref[...] = rbv * (dyg - ybv * (rbv * rbv) * jnp.mean(dyg * ybv, axis=-1, keepdims=True))

        @pl.when(i == n - 1)
        def _():
            ggo_ref[...] = jnp.sum(acc[...], axis=0, keepdims=True)

    vec = pl.BlockSpec((1, wa), lambda i: (0, 0))
    return pl.pallas_call(
        body, name="gate_b_bwd", grid=(n,),
        in_specs=[pl.BlockSpec((tm, wa), lambda i: (i, 1)), pl.BlockSpec((tm, wa), lambda i: (i, 0)),
                  pl.BlockSpec((tm, wa), lambda i: (i, 6)), pl.BlockSpec((tm, 1), lambda i: (i, 0)), vec],
        out_specs=(pl.BlockSpec((tm, wa), lambda i: (i, 6)), pl.BlockSpec((tm, wa), lambda i: (i, 0)), vec),
        out_shape=(jax.ShapeDtypeStruct((s, d_in), BF16), jax.ShapeDtypeStruct((s, wa), F32),
                   jax.ShapeDtypeStruct((1, wa), F32)),
        scratch_shapes=[pltpu.VMEM((8, wa), F32)],
        compiler_params=_params("arbitrary"),
    )(dy, yb, proj, rb, g_out)


def _bucket_tables():
    qi = BLK + np.arange(BLK)
    kj = np.arange(2 * BLK)
    delta = qi[:, None] - kj[None, :]
    max_exact = NUM_BUCKETS // 2
    tabs = []
    for window, dil in DILATED:
        band = (delta >= 0) & (delta <= window // dil)
        dist = np.clip(delta, 0, None) * dil
        d = np.maximum(dist, 1).astype(np.float32)
        large = max_exact + (np.log(d / np.float32(max_exact)) / np.float32(math.log(MAX_DISTANCE / max_exact))
                             * np.float32(NUM_BUCKETS - max_exact)).astype(np.int32)
        large = np.minimum(large, NUM_BUCKETS - 1)
        tabs.append(np.where(band, np.where(dist < max_exact, dist, large), -1).astype(np.int32))
    return np.stack(tabs)


def _bias_tiles(tab_ref, rb_ref, h, bias_s):
    col = lax.broadcasted_iota(jnp.int32, (BLK, 2 * BLK), 1)
    for i in range(len(DILATED)):
        t = tab_ref[i]
        bias = jnp.zeros(t.shape, F32)
        for b in range(NUM_BUCKETS):
            bias = jnp.where(t == b, rb_ref[b, h], bias)
        bias = jnp.where(t >= 0, bias, NEG_INF)
        bias_s[2 * i] = jnp.where(col >= BLK, bias, NEG_INF)
        bias_s[2 * i + 1] = bias


def _block_rows(b, n_blocks, dil):
    nb = n_blocks // dil
    r = b // nb
    n = b % nb
    start = r + dil * BLK * n
    if dil == 1:
        return pl.ds(pl.multiple_of(start, BLK), BLK), n
    return pl.ds(start, BLK, stride=dil), n


def _sub_block(b, rows=BLK, pad=0):
    return pl.ds(pl.multiple_of(b * BLK + pad, BLK), rows)


def _rms_rows(x, gain):
    return x * lax.rsqrt(jnp.mean(x * x, axis=-1, keepdims=True) + EPS) * gain


def _attn_fwd(proj, g_q, g_k, rel_bias, d_model, comm=None):
    s = proj.shape[0]
    heads = d_model // 2 // HEAD_DIM
    n_blocks = s // BLK
    scale = HEAD_DIM ** -0.5
    tables = jnp.asarray(_bucket_tables())
    c_ins = list(comm.ins) if comm else []
    c_outs = list(comm.out_shape) if comm else []
    c_scratch = list(comm.scratch) if comm else []

    def body(q_ref, k_ref, v_ref, gq_ref, gk_ref, rb_ref, tab_ref, *rest):
        cin_refs, rest = rest[:len(c_ins)], rest[len(c_ins):]
        yb_ref, lse_ref = rest[:2]
        cout_refs, rest = rest[2:2 + len(c_outs)], rest[2 + len(c_outs):]
        qs, ks, vs, m_s, l_s, acc_s, bias_s, sc_s, p_s = rest[:9]
        sems = rest[9:]
        h = pl.program_id(0)
        if comm:
            @pl.when(h == 0)
            def _():
                comm.start(cin_refs, cout_refs, sems)

        _bias_tiles(tab_ref, rb_ref, h, bias_s)
        gq = gq_ref[...] * scale
        gk = gk_ref[...]
        ks[0:BLK, :] = jnp.zeros((BLK, HEAD_DIM), BF16)
        vs[0:BLK, :] = jnp.zeros((BLK, HEAD_DIM), BF16)

        for i, (_, dil) in enumerate(DILATED):
            def prepare(b, carry, dil=dil):
                rows, _ = _block_rows(b, n_blocks, dil)
                qs[_sub_block(b), :] = _rms_rows(q_ref[rows, :], gq).astype(BF16)
                ks[_sub_block(b, pad=BLK), :] = _rms_rows(k_ref[rows, :], gk).astype(BF16)
                vs[_sub_block(b, pad=BLK), :] = v_ref[rows, :].astype(BF16)
                return carry
            lax.fori_loop(0, n_blocks, prepare, 0, unroll=8)

            def group(g, carry, i=i, dil=dil):
                blocks = [g * ATTN_GROUP + t for t in range(ATTN_GROUP)]
                for t, b in enumerate(blocks):
                    sc_s[t] = _dot(qs[_sub_block(b), :], ks[_sub_block(b, 2 * BLK), :], tb=True)
                for t, b in enumerate(blocks):
                    rows, n = _block_rows(b, n_blocks, dil)
                    sc = sc_s[t] + bias_s[2 * i + jnp.minimum(n, 1)]
                    m_i = jnp.max(sc, axis=-1, keepdims=True)
                    pr = jnp.exp(sc - m_i)
                    l_i = jnp.sum(pr, axis=-1, keepdims=True)
                    p_s[t] = pr.astype(BF16)
                    m_s[i, rows, :] = m_i
                    l_s[i, rows, :] = l_i
                for t, b in enumerate(blocks):
                    rows, _ = _block_rows(b, n_blocks, dil)
                    acc_s[i, rows, :] = _dot(p_s[t], vs[_sub_block(b, 2 * BLK), :])
                return carry
            lax.fori_loop(0, n_blocks // ATTN_GROUP, group, 0)

        m = jnp.maximum(jnp.maximum(m_s[0], m_s[1]), m_s[2])
        l = jnp.zeros_like(m)
        acc = jnp.zeros((s, HEAD_DIM), F32)
        for i in range(len(DILATED)):
            w = jnp.exp(m_s[i] - m)
            l = l + l_s[i] * w
            acc = acc + acc_s[i] * w
        yb_ref[...] = acc / l
        lse_ref[...] = m + jnp.log(l)
        if comm:
            @pl.when(h == heads - 1)
            def _():
                comm.finish(cin_refs, cout_refs, sems)

    def head_col(off):
        return pl.BlockSpec((s, HEAD_DIM), lambda h: (0, off * heads + h))
    vec = pl.BlockSpec((1, HEAD_DIM), lambda h: (0, 0))
    res = pl.pallas_call(
        body, name="attn_fwd", grid=(heads,),
        in_specs=[head_col(3), head_col(4), head_col(5), vec, vec,
                  pl.BlockSpec(memory_space=pltpu.SMEM),
                  pl.BlockSpec((len(DILATED), BLK, 2 * BLK), lambda h: (0, 0, 0))] + [ANY] * len(c_ins),
        out_specs=[pl.BlockSpec((s, HEAD_DIM), lambda h: (0, h)), pl.BlockSpec((None, s, 1), lambda h: (h, 0, 0))]
        + [ANY] * len(c_outs),
        out_shape=[jax.ShapeDtypeStruct((s, heads * HEAD_DIM), F32), jax.ShapeDtypeStruct((heads, s, 1), F32)] + c_outs,
        scratch_shapes=[pltpu.VMEM((s, HEAD_DIM), BF16), pltpu.VMEM((s + BLK, HEAD_DIM), BF16),
                        pltpu.VMEM((s + BLK, HEAD_DIM), BF16),
                        pltpu.VMEM((len(DILATED), s, 1), F32), pltpu.VMEM((len(DILATED), s, 1), F32),
                        pltpu.VMEM((len(DILATED), s, HEAD_DIM), F32),
                        pltpu.VMEM((2 * len(DILATED), BLK, 2 * BLK), F32),
                        pltpu.VMEM((ATTN_GROUP, BLK, 2 * BLK), F32), pltpu.VMEM((ATTN_GROUP, BLK, 2 * BLK), BF16)] + c_scratch,
        input_output_aliases={7 + i: 2 + i for i in range(len(c_ins))} if (comm and comm.in_place) else {},
        compiler_params=_params("arbitrary"),
    )(proj, proj, proj, g_q, g_k, rel_bias, tables, *c_ins)
    return res[0], res[1], res[2:]


def _attn_bwd(dproj, d_o, yb, lse, proj, g_q, g_k, rel_bias, d_model, comm=None):
    s, d_in = proj.shape
    heads = d_model // 2 // HEAD_DIM
    n_blocks = s // BLK
    scale = HEAD_DIM ** -0.5
    tables = jnp.asarray(_bucket_tables())
    n_dil = len(DILATED)
    c_ins = list(comm.ins) if comm else []
    c_outs = list(comm.out_shape) if comm else []
    c_scratch = list(comm.scratch) if comm else []

    def body(dp_any, q_ref, k_ref, v_ref, do_ref, o_ref, lse_ref, gq_ref, gk_ref, rb_ref, tab_ref, *rest):
        cin_refs, rest = rest[:len(c_ins)], rest[len(c_ins):]
        dp_out, ggq_ref, ggk_ref, db_ref = rest[:4]
        cout_refs, rest = rest[4:4 + len(c_outs)], rest[4 + len(c_outs):]
        (qs, ks, vs, dos, lse_s, del_s, dqn, dkn, dvv, dq_u, dk_u, dv_u, bias_s, dbias_s, sc_s, dp_s, p_s, ds_s,
         sems) = rest[:19]
        comm_sems = rest[19:]
        del dp_any
        h = pl.program_id(0)
        if comm:
            @pl.when(h == 0)
            def _():
                comm.start(cin_refs, cout_refs, comm_sems)


        @pl.when(h == 0)
        def _():
            ggq_ref[...] = jnp.zeros_like(ggq_ref)
            ggk_ref[...] = jnp.zeros_like(ggk_ref)

        dbias_s[...] = jnp.zeros_like(dbias_s)
        _bias_tiles(tab_ref, rb_ref, h, bias_s)
        gq = gq_ref[...] * scale
        gk = gk_ref[...]
        ks[0:BLK, :] = jnp.zeros((BLK, HEAD_DIM), BF16)
        vs[0:BLK, :] = jnp.zeros((BLK, HEAD_DIM), BF16)

        for i, (_, dil) in enumerate(DILATED):
            def prepare(b, carry, dil=dil):
                rows, _ = _block_rows(b, n_blocks, dil)
                qs[_sub_block(b), :] = _rms_rows(q_ref[rows, :], gq).astype(BF16)
                ks[_sub_block(b, pad=BLK), :] = _rms_rows(k_ref[rows, :], gk).astype(BF16)
                vs[_sub_block(b, pad=BLK), :] = v_ref[rows, :].astype(BF16)
                do = do_ref[rows, :]
                dos[_sub_block(b), :] = do.astype(BF16)
                del_s[_sub_block(b), :] = jnp.sum(do * o_ref[rows, :], axis=-1, keepdims=True)
                lse_s[_sub_block(b), :] = lse_ref[rows, :]
                return carry
            lax.fori_loop(0, n_blocks, prepare, 0, unroll=8)
            dk_u[...] = jnp.zeros_like(dk_u)
            dv_u[...] = jnp.zeros_like(dv_u)

            def group(g, carry, i=i, dil=dil):
                blocks = [g * ATTN_GROUP + t for t in range(ATTN_GROUP)]
                for t, b in enumerate(blocks):
                    sc_s[t] = _dot(qs[_sub_block(b), :], ks[_sub_block(b, 2 * BLK), :], tb=True)
                    dp_s[t] = _dot(dos[_sub_block(b), :], vs[_sub_block(b, 2 * BLK), :], tb=True)
                for t, b in enumerate(blocks):
                    _, n = _block_rows(b, n_blocks, dil)
                    pr = jnp.exp(sc_s[t] + bias_s[2 * i + jnp.minimum(n, 1)] - lse_s[_sub_block(b), :])
                    ds = pr * (dp_s[t] - del_s[_sub_block(b), :])
                    dbias_s[i] += ds
                    p_s[t] = pr.astype(BF16)
                    ds_s[t] = ds.astype(BF16)
                for t, b in enumerate(blocks):
                    dq_u[_sub_block(b), :] = _dot(ds_s[t], ks[_sub_block(b, 2 * BLK), :])
                    dk_u[_sub_block(b, 2 * BLK), :] += _dot(ds_s[t], qs[_sub_block(b), :], ta=True)
                    dv_u[_sub_block(b, 2 * BLK), :] += _dot(p_s[t], dos[_sub_block(b), :], ta=True)
                return carry
            lax.fori_loop(0, n_blocks // ATTN_GROUP, group, 0)

            def scatter(b, carry, i=i, dil=dil):
                rows, _ = _block_rows(b, n_blocks, dil)
                for acc, part, pad in ((dqn, dq_u, 0), (dkn, dk_u, BLK), (dvv, dv_u, BLK)):
                    val = part[_sub_block(b, pad=pad), :]
                    acc[rows, :] = val if i == 0 else acc[rows, :] + val
                return carry
            lax.fori_loop(0, n_blocks, scatter, 0, unroll=8)

        q = q_ref[...]
        rq = lax.rsqrt(jnp.mean(q * q, axis=-1, keepdims=True) + EPS)
        k = k_ref[...]
        rk = lax.rsqrt(jnp.mean(k * k, axis=-1, keepdims=True) + EPS)
        dq_n = dqn[...]
        ggq_ref[...] += jnp.sum(dq_n * (q * rq) * scale, axis=0, keepdims=True)
        dg = dq_n * gq_ref[...] * scale
        qs[...] = (rq * (dg - q * (rq * rq) * jnp.mean(dg * q, axis=-1, keepdims=True))).astype(BF16)
        dk_n = dkn[...]
        ggk_ref[...] += jnp.sum(dk_n * (k * rk), axis=0, keepdims=True)
        dg = dk_n * gk_ref[...]
        dos[...] = (rk * (dg - k * (rk * rk) * jnp.mean(dg * k, axis=-1, keepdims=True))).astype(BF16)
        vs[BLK:, :] = dvv[...].astype(BF16)
        copies = [pltpu.make_async_copy(src, dp_out.at[:, pl.ds(pl.multiple_of(((3 + j) * heads + h) * HEAD_DIM, HEAD_DIM), HEAD_DIM)],
                                        sems.at[j]) for j, src in enumerate((qs, dos, vs.at[pl.ds(BLK, s)]))]
        for cp in copies:
            cp.start()
        for b in range(NUM_BUCKETS):
            tot = jnp.zeros((BLK, 2 * BLK), F32)
            for i in range(n_dil):
                tot = tot + jnp.where(tab_ref[i] == b, dbias_s[i], 0.0)
            db_ref[b:b + 1, :] = jnp.full((1, LANE), jnp.sum(tot), F32)
        for cp in copies:
            cp.wait()
        if comm:
            @pl.when(h == heads - 1)
            def _():
                comm.finish(cin_refs, cout_refs, comm_sems)

    def head_col(off):
        return pl.BlockSpec((s, HEAD_DIM), lambda h: (0, off * heads + h))
    own_col = pl.BlockSpec((s, HEAD_DIM), lambda h: (0, h))
    vec = pl.BlockSpec((1, HEAD_DIM), lambda h: (0, 0))
    big = pltpu.VMEM((s, HEAD_DIM), F32)
    padded32 = pltpu.VMEM((s + BLK, HEAD_DIM), F32)
    padded16 = pltpu.VMEM((s + BLK, HEAD_DIM), BF16)
    res = pl.pallas_call(
        body, name="attn_bwd", grid=(heads,),
        in_specs=[ANY, head_col(3), head_col(4), head_col(5), own_col, own_col,
                  pl.BlockSpec((None, s, 1), lambda h: (h, 0, 0)), vec, vec,
                  pl.BlockSpec(memory_space=pltpu.SMEM),
                  pl.BlockSpec((n_dil, BLK, 2 * BLK), lambda h: (0, 0, 0))] + [ANY] * len(c_ins),
        out_specs=[ANY, vec, vec, pl.BlockSpec((None, NUM_BUCKETS, LANE), lambda h: (h, 0, 0))] + [ANY] * len(c_outs),
        out_shape=[jax.ShapeDtypeStruct((s, d_in), BF16), jax.ShapeDtypeStruct((1, HEAD_DIM), F32),
                   jax.ShapeDtypeStruct((1, HEAD_DIM), F32), jax.ShapeDtypeStruct((heads, NUM_BUCKETS, LANE), F32)]
        + c_outs,
        scratch_shapes=[pltpu.VMEM((s, HEAD_DIM), BF16), padded16, padded16, pltpu.VMEM((s, HEAD_DIM), BF16),
                        pltpu.VMEM((s, 1), F32), pltpu.VMEM((s, 1), F32),
                        big, big, big, big, padded32, padded32,
                        pltpu.VMEM((2 * n_dil, BLK, 2 * BLK), F32), pltpu.VMEM((n_dil, BLK, 2 * BLK), F32),
                        pltpu.VMEM((ATTN_GROUP, BLK, 2 * BLK), F32), pltpu.VMEM((ATTN_GROUP, BLK, 2 * BLK), F32),
                        pltpu.VMEM((ATTN_GROUP, BLK, 2 * BLK), BF16), pltpu.VMEM((ATTN_GROUP, BLK, 2 * BLK), BF16),
                        pltpu.SemaphoreType.DMA((3,))] + c_scratch,
        input_output_aliases={0: 0},
        compiler_params=_params("arbitrary"),
    )(dproj, proj, proj, proj, d_o, yb, lse, g_q, g_k, rel_bias, tables, *c_ins)
    return res[0], res[1], res[2], res[3], res[4:]


def _local_step(xs, ps, tgt, sp, win4, wout4, wgate4, wup4, dist=None):
    s, d = xs.shape
    wa = d // 2
    heads = wa // HEAD_DIM
    d_in = 7 * wa
    c4 = d_in // N_CHIPS
    dple = ps.shape[1]

    tm, tn = _tile(s, 1024), _tile(d, 1024)
    tn_in = _tile(c4, 1024)
    per_in = c4 // tn_in
    tn_up = _tile(d // N_CHIPS, 512)
    per_up = (d // N_CHIPS) // tn_up
    gm, gn = s // tm, d // tn

    hn, rx = _rms_fwd("rms_pre", xs, sp["g_pre"])
    if dist:
        proj, win4 = _mm_in_gather(dist["chip"], hn, win4)
    else:
        proj = _matmul(
            "mm_in", hn, win4, pl.BlockSpec((tm, d), lambda i, j, k: (i, 0)), _shard_spec(d, tn_in, per_in, 2, 1),
            (gm, d_in // tn_in, 1), None, jax.ShapeDtypeStruct((s, d_in), F32),
            pl.BlockSpec((tm, tn_in), lambda i, j, k: (i, j)))
    b_st = sp["b_s"].T
    y, z, vn, mu_v, rs_v, ra = _gmlp_fwd(proj, sp["w_s"], b_st, sp["ln_v_g"], sp["ln_v_b"], sp["g_out_a"], d)
    yb, lse, gathered = _attn_fwd(proj, sp["g_q"], sp["g_k"], sp["rel_bias"], d,
                                  comm=_Gather([wout4, wgate4, wup4]) if dist else None)
    if dist:
        wout4, wgate4, wup4 = gathered
    wout, wgate = wout4.reshape(d, d), wgate4.reshape(d, d)
    y, rb = _gate_fwd(y, yb, proj, sp["g_out_b"], d)

    def residual(acc, extra, outs):
        outs[0][...] = extra[0][...] + acc

    tile = pl.BlockSpec((tm, tn), lambda i, j, k: (i, j))
    tile_up = pl.BlockSpec((tm, tn_up), lambda i, j, k: (i, j))
    h = _matmul(
        "mm_out", y, wout, pl.BlockSpec((tm, d), lambda i, j, k: (i, 0)), pl.BlockSpec((d, tn_up), lambda i, j, k: (0, j)),
        (gm, d // tn_up, 1), None, jax.ShapeDtypeStruct((s, d), F32), tile_up, extras=(xs,), extra_specs=(tile_up,),
        epilogue=residual)
    hp, rh = _rms_fwd("rms_ple", h, sp["g_ple"])
    p16 = ps.astype(BF16)

    def head(acc, extra, outs):
        p_ref, wup_ref, h_ref, t_ref = extra
        dout_ref, dgl_ref, dup_ref, loss_ref = outs
        up = _dot(p_ref[...], wup_ref[...])
        gate = jax.nn.sigmoid(acc)
        err = h_ref[...] + gate * up - t_ref[...]
        sq = _colsum8(err * err)
        part = sq[:, 0:LANE]
        for c in range(1, sq.shape[1] // LANE):
            part = part + sq[:, c * LANE:(c + 1) * LANE]
        loss_ref[...] = part
        dout = err * (1.0 / d)
        dout_ref[...] = dout
        dup_ref[...] = (dout * gate).astype(BF16)
        dgl_ref[...] = (dout * up * gate * (1.0 - gate)).astype(BF16)

    tile_up = pl.BlockSpec((tm, tn_up), lambda i, j, k: (i, j))
    dout, dglin, dup, loss_parts = _matmul(
        "mm_gate_loss", hp, wgate, pl.BlockSpec((tm, d), lambda i, j, k: (i, 0)), pl.BlockSpec((d, tn_up), lambda i, j, k: (0, j)),
        (gm, d // tn_up, 1), None,
        (jax.ShapeDtypeStruct((s, d), F32), jax.ShapeDtypeStruct((s, d), BF16), jax.ShapeDtypeStruct((s, d), BF16),
         jax.ShapeDtypeStruct((gm * 8, (d // tn_up) * LANE), F32)),
        (tile_up, tile_up, tile_up, pl.BlockSpec((8, LANE), lambda i, j, k: (i, j))),
        extras=(p16, wup4, h, tgt),
        extra_specs=(pl.BlockSpec((tm, dple), lambda i, j, k: (i, 0)), _shard_spec(dple, tn_up, per_up, 2, 1),
                     tile_up, tile_up),
        epilogue=head)
    loss = 0.5 * jnp.sum(loss_parts) * (1.0 / d)

    dhp = _matmul(
        "mm_dhp", dglin, wgate, pl.BlockSpec((tm, d), lambda i, j, k: (i, 0)), pl.BlockSpec((tn, d), lambda i, j, k: (j, 0)),
        (gm, gn, 1), None, jax.ShapeDtypeStruct((s, d), F32), tile, tb=True)
    dh, dh16, g_g_ple = _rms_bwd("rms_ple_bwd", dhp, h, rh, sp["g_ple"], dout, True)
    tmw = _tile(d, 1024)
    g_wout = _matmul(
        "mm_gwout", y, dh16, pl.BlockSpec((s, tmw), lambda i, j, k: (0, i)), pl.BlockSpec((s, tn), lambda i, j, k: (0, j)),
        (d // tmw, gn, 1), None, jax.ShapeDtypeStruct((d, d), F32), pl.BlockSpec((tmw, tn), lambda i, j, k: (i, j)), ta=True)
    g_wout4 = g_wout.reshape(N_CHIPS, d // N_CHIPS, d)
    g_wgate = _matmul(
        "mm_gwgate", hp, dglin, pl.BlockSpec((s, tmw), lambda i, j, k: (0, i)), pl.BlockSpec((s, tn), lambda i, j, k: (0, j)),
        (d // tmw, gn, 1), None, jax.ShapeDtypeStruct((d, d), F32), pl.BlockSpec((tmw, tn), lambda i, j, k: (i, j)), ta=True,
        comm=_SwapHalves([g_wout4]) if dist else None)
    if dist:
        g_wgate, swapped_out = g_wgate
    g_wgate4 = g_wgate.reshape(N_CHIPS, d // N_CHIPS, d)
    g_wup = _matmul(
        "mm_gwup", p16, dup, pl.BlockSpec((s, dple), lambda i, j, k: (0, 0)), pl.BlockSpec((s, tn_up), lambda i, j, k: (0, j)),
        (1, d // tn_up, 1), None, jax.ShapeDtypeStruct((N_CHIPS, dple, d // N_CHIPS), F32),
        _shard_spec(dple, tn_up, per_up, 0, 1), ta=True)
    dy = _matmul(
        "mm_dy", dh16, wout, pl.BlockSpec((tm, d), lambda i, j, k: (i, 0)), pl.BlockSpec((tn, d), lambda i, j, k: (j, 0)),
        (gm, gn, 1), None, jax.ShapeDtypeStruct((s, d), F32), tile, tb=True,
        comm=_SwapHalves([g_wgate4, g_wup]) if dist else None)
    early = (g_wout4, g_wgate4, g_wup)
    if dist:
        core = dist["core"]
        dy, swapped_rest = dy
        early_sums = [_chip_sum("chip_sum_" + n, core, g, o)
                      for n, g, o in zip(LARGE[1:], early, list(swapped_out) + list(swapped_rest))]

    dproj, d_o, g_g_out_b = _gate_bwd(dy, yb, proj, rb, sp["g_out_b"], d)
    dproj, g_g_q, g_g_k, dbias, early_got = _attn_bwd(
        dproj, d_o, yb, lse, proj, sp["g_q"], sp["g_k"], sp["rel_bias"], d,
        comm=_ExchangeChips([s16 for _, s16 in early_sums]) if dist else None)
    dproj, g_w_s, dz_sum, g_ln_g, g_ln_b, g_g_out_a = _gmlp_bwd(
        dproj, dy, proj, z, vn, mu_v, rs_v, ra, sp["w_s"], sp["ln_v_g"], sp["g_out_a"], d)

    if dist:
        tmw_half = _tile(d // 2, 1024)
        n_half = (d // 2) // tmw_half

        def half_of_gwin(name, own, comm):
            def rows(i, j, k, core_ref):
                return (0, (core_ref[0] if own else 1 - core_ref[0]) * n_half + i)
            return _matmul(
                name, hn, dproj, pl.BlockSpec((s, tmw_half), rows),
                pl.BlockSpec((s, tn_in), lambda i, j, k, core_ref: (0, j)),
                (n_half, d_in // tn_in, 1), None, jax.ShapeDtypeStruct((N_CHIPS, d // 2, c4), F32),
                pl.BlockSpec((None, tmw_half, tn_in), lambda i, j, k, core_ref: (j // per_in, i, j % per_in)), ta=True,
                comm=comm, prefetch=(core,))

        g_theirs = half_of_gwin("mm_gwin_theirs", False, None)
        g_mine, (g_from_sibling,) = half_of_gwin("mm_gwin_mine", True, _SwapHalves([g_theirs], whole=True))
        win_sum = _chip_sum("chip_sum_w_in", core, g_mine, g_from_sibling)
    else:
        g_win = _matmul(
            "mm_gwin", hn, dproj, pl.BlockSpec((s, tmw), lambda i, j, k: (0, i)),
            pl.BlockSpec((s, tn_in), lambda i, j, k: (0, j)),
            (d // tmw, d_in // tn_in, 1), None, jax.ShapeDtypeStruct((N_CHIPS, d, c4), F32),
            _shard_spec(tmw, tn_in, per_in, 0, 1), ta=True)
    tk = c4
    tmh, tnh = _tile(s, 1024), _tile(d, 1024)
    dhn = _matmul(
        "mm_dhn", dproj, win4, pl.BlockSpec((tmh, tk), lambda i, j, k: (i, k)), _shard_spec(tnh, tk, 1, 1, 2),
        (s // tmh, d // tnh, d_in // tk), (tmh, tnh), jax.ShapeDtypeStruct((s, d), F32),
        pl.BlockSpec((tmh, tnh), lambda i, j, k: (i, j)), tb=True,
        comm=_ExchangeChips([win_sum[1]]) if dist else None)
    if dist:
        dhn, win_got = dhn
    grad_x, g_g_pre = _rms_bwd("rms_pre_bwd", dhn, xs, rx, sp["g_pre"], dh, False)

    small = {
        "g_pre": g_g_pre,
        "w_s": g_w_s,
        "b_s": jnp.sum(dz_sum.reshape(CHUNK, heads, HEAD_DIM), axis=-1).T,
        "ln_v_g": g_ln_g, "ln_v_b": g_ln_b,
        "g_q": g_g_q, "g_k": g_g_k,
        "rel_bias": dbias[:, :, 0].T,
        "g_out_a": g_g_out_a, "g_out_b": g_g_out_b,
        "g_ple": g_g_ple,
    }
    if not dist:
        return loss, grad_x, (g_win, *early), small
    sums32 = [win_sum[0]] + [s32 for s32, _ in early_sums]
    return loss, grad_x, (sums32, list(win_got) + list(early_got)), small


def _place():
    x, y, c = lax.axis_index("x"), lax.axis_index("y"), lax.axis_index("c")
    chips = [(1 - x, y), (x, 1 - y), (1 - x, 1 - y)]
    return x, y, c, chips


def _remote(src, dst, send_sems, recv_sems, k, to):
    return pltpu.make_async_remote_copy(src_ref=src, dst_ref=dst, send_sem=send_sems.at[k], recv_sem=recv_sems.at[k],
                                        device_id=to, device_id_type=MESH)


def _cast_bf16(name, chip, w):
    r, c = w.shape
    tm = _tile(r, 256)

    def body(chip_ref, w_ref, o_ref):
        del chip_ref
        o_ref[...] = w_ref[...].astype(BF16)

    return pl.pallas_call(
        body, name=name,
        grid_spec=pltpu.PrefetchScalarGridSpec(
            num_scalar_prefetch=1, grid=(r // tm,),
            in_specs=[pl.BlockSpec((tm, c), lambda i, chip_ref: (i, 0))],
            out_specs=pl.BlockSpec((None, tm, c), lambda i, chip_ref: (chip_ref[0], i, 0))),
        out_shape=jax.ShapeDtypeStruct((N_CHIPS, r, c), BF16), compiler_params=_params("parallel"),
    )(chip, w)


class _Gather:
    in_place = True

    def __init__(self, fulls):
        self.ins = list(fulls)
        self.out_shape = [jax.ShapeDtypeStruct(f.shape, f.dtype) for f in fulls]
        n = len(fulls)
        self.scratch = [pltpu.SemaphoreType.DMA((6 * n,)), pltpu.SemaphoreType.DMA((6 * n,))]

    @staticmethod
    def _sends(outs, sems):
        send_sems, recv_sems = sems
        x, y, c, chips = _place()
        cps = []
        for w, ref in enumerate(outs):
            half = ref.shape[1] // 2
            blk = ref.at[2 * x + y, pl.ds(c * half, half)]
            cps += [_remote(blk, blk, send_sems, recv_sems, 6 * w + q, (*chip, c)) for q, chip in enumerate(chips)]
        return cps

    def start(self, ins, outs, sems):
        for cp in self._sends(outs, sems):
            cp.start()

    def finish(self, ins, outs, sems):
        send_sems, recv_sems = sems
        x, y, c, chips = _place()
        sibling = (x, y, 1 - c)
        forwards = []
        for w, ref in enumerate(outs):
            half = ref.shape[1] // 2
            for q, chip in enumerate(chips):
                blk = ref.at[2 * chip[0] + chip[1], pl.ds(c * half, half)]
                _remote(blk, blk, send_sems, recv_sems, 6 * w + q, sibling).wait_recv()
                fwd = _remote(blk, blk, send_sems, recv_sems, 6 * w + 3 + q, sibling)
                fwd.start()
                forwards.append(fwd)
        for w, ref in enumerate(outs):
            half = ref.shape[1] // 2
            for q, chip in enumerate(chips):
                blk = ref.at[2 * chip[0] + chip[1], pl.ds((1 - c) * half, half)]
                _remote(blk, blk, send_sems, recv_sems, 6 * w + 3 + q, sibling).wait_recv()
        for cp in self._sends(outs, sems) + forwards:
            cp.wait_send()


class _ExchangeChips:
    in_place = False

    def __init__(self, sums16):
        self.ins = list(sums16)
        self.out_shape = [jax.ShapeDtypeStruct(g.shape, g.dtype) for g in sums16]
        n = len(sums16)
        self.scratch = [pltpu.SemaphoreType.DMA((3 * n,)), pltpu.SemaphoreType.DMA((3 * n,))]

    @staticmethod
    def _sends(ins, outs, sems):
        send_sems, recv_sems = sems
        x, y, c, chips = _place()
        return [_remote(ins[w].at[2 * chip[0] + chip[1]], outs[w].at[2 * x + y], send_sems, recv_sems, 3 * w + q, (*chip, c))
                for w in range(len(ins)) for q, chip in enumerate(chips)]

    def start(self, ins, outs, sems):
        for cp in self._sends(ins, outs, sems):
            cp.start()

    def finish(self, ins, outs, sems):
        send_sems, recv_sems = sems
        x, y, c, chips = _place()
        for w in range(len(ins)):
            for q, chip in enumerate(chips):
                blk = outs[w].at[2 * chip[0] + chip[1]]
                _remote(blk, blk, send_sems, recv_sems, 3 * w + q, (*chip, c)).wait_recv()
        for cp in self._sends(ins, outs, sems):
            cp.wait_send()


def _mm_in_gather(chip, hn, win4):
    s, d = hn.shape
    c4 = win4.shape[2]
    tm, tn = _tile(s, 1024), _tile(c4, 1024)
    per = c4 // tn
    nj, gm = N_CHIPS * per, s // tm
    half = d // 2

    def block_of(j, chip_ref):
        o = j // per
        return chip_ref[0] ^ (((o & 1) << 1) | (o >> 1))

    i_late = max(gm - 2, 0)

    def body(chip_ref, hn_ref, w_any, proj_ref, w_ref, wbuf, wsem, send_sems, recv_sems):
        del w_any
        j, i = pl.program_id(0), pl.program_id(1)
        x, y, c, chips = _place()
        sibling = (x, y, 1 - c)

        def region(k, rows_half, t):
            return w_ref.at[k, pl.ds(rows_half * half, half), pl.ds(t * tn, tn)]

        def send_mine(q, t):
            blk = region(2 * x + y, c, t)
            return _remote(blk, blk, send_sems, recv_sems, q * per + t, (*chips[q], c))

        def forward(q, t):
            blk = region(2 * chips[q][0] + chips[q][1], c, t)
            return _remote(blk, blk, send_sems, recv_sems, (3 + q) * per + t, sibling)

        def fetch(jj, mine, slot):
            rows = pl.ds((c if mine else 1 - c) * half, half)
            cols = pl.ds(pl.multiple_of((jj % per) * tn, LANE), tn)
            return pltpu.make_async_copy(w_ref.at[block_of(jj, chip_ref), rows, cols], wbuf.at[slot, rows],
                                         wsem.at[2 * slot + (0 if mine else 1)])

        def foreign_tiles():
            return [(o, t) for o in range(1, N_CHIPS) for t in range(per)]

        @pl.when((j == 0) & (i == 0))
        def _():
            for t in range(per):
                send_mine(0, t).start()
                send_mine(1, t).start()
            fetch(0, True, 0).start()
            fetch(0, False, 0).start()

        nxt = j + 1

        @pl.when((i == 0) & (nxt < nj))
        def _():
            for o, t in foreign_tiles():
                @pl.when(nxt == o * per + t)
                def _():
                    blk = region(2 * chips[o - 1][0] + chips[o - 1][1], c, t)
                    _remote(blk, blk, send_sems, recv_sems, (o - 1) * per + t, sibling).wait_recv()
                    forward(o - 1, t).start()
                    if (o, t) == (1, per - 1):
                        for tt in range(per):
                            send_mine(2, tt).start()

            fetch(nxt, True, nxt % 2).start()

            @pl.when(nxt < per)
            def _():
                fetch(nxt, False, nxt % 2).start()

        @pl.when((i == i_late) & (nxt < nj))
        def _():
            for o, t in foreign_tiles():
                @pl.when(nxt == o * per + t)
                def _():
                    blk = region(2 * chips[o - 1][0] + chips[o - 1][1], 1 - c, t)
                    _remote(blk, blk, send_sems, recv_sems, (3 + o - 1) * per + t, sibling).wait_recv()
                    fetch(nxt, False, nxt % 2).start()

        @pl.when(i == 0)
        def _():
            fetch(j, True, j % 2).wait()
            fetch(j, False, j % 2).wait()

        proj_ref[...] = _dot(hn_ref[...], wbuf[j % 2])

        @pl.when((j == nj - 1) & (i == gm - 1))
        def _():
            for q in range(3):
                for t in range(per):
                    send_mine(q, t).wait_send()
                    forward(q, t).wait_send()

    return pl.pallas_call(
        body, name="mm_in_gather",
        grid_spec=pltpu.PrefetchScalarGridSpec(
            num_scalar_prefetch=1, grid=(nj, gm),
            in_specs=[pl.BlockSpec((tm, d), lambda j, i, chip_ref: (i, 0)), ANY],
            out_specs=[pl.BlockSpec((tm, tn), lambda j, i, chip_ref: (i, block_of(j, chip_ref) * per + j % per)), ANY],
            scratch_shapes=[pltpu.VMEM((2, d, tn), BF16), pltpu.SemaphoreType.DMA((4,)),
                            pltpu.SemaphoreType.DMA((6 * per,)), pltpu.SemaphoreType.DMA((6 * per,))]),
        out_shape=[jax.ShapeDtypeStruct((s, N_CHIPS * c4), F32), jax.ShapeDtypeStruct(win4.shape, win4.dtype)],
        input_output_aliases={2: 1},
        compiler_params=_params("arbitrary", "arbitrary"),
    )(chip, hn, win4)


class _SwapHalves:
    in_place = False

    def __init__(self, grads, whole=False):
        self.ins = list(grads)
        self.whole = whole
        self.out_shape = [jax.ShapeDtypeStruct((N_CHIPS, g.shape[1] // (1 if whole else 2), g.shape[2]), g.dtype)
                          for g in grads]
        self.scratch = [pltpu.SemaphoreType.DMA((len(grads),)), pltpu.SemaphoreType.DMA((len(grads),))]

    def _copies(self, ins, outs, sems):
        x, y, c, _ = _place()
        cps = []
        for w in range(len(ins)):
            half = ins[w].shape[1] // 2
            src = ins[w] if self.whole else ins[w].at[:, pl.ds((1 - c) * half, half)]
            cps.append(_remote(src, outs[w], sems[0], sems[1], w, (x, y, 1 - c)))
        return cps

    def start(self, ins, outs, sems):
        for cp in self._copies(ins, outs, sems):
            cp.start()

    def finish(self, ins, outs, sems):
        for cp in self._copies(ins, outs, sems):
            cp.wait()


def _chip_sum(name, core, grad, got):
    _, half, c = got.shape
    th = _tile(half, 128)
    n = half // th
    skip = n if grad.shape[1] != half else 0

    def body(core_ref, g_ref, o_ref, s32_ref, s16_ref):
        del core_ref
        v = g_ref[...] + o_ref[...]
        s32_ref[...] = v
        s16_ref[...] = v.astype(BF16)

    blk = pl.BlockSpec((None, th, c), lambda k, i, core_ref: (k, i, 0))
    return pl.pallas_call(
        body, name=name,
        grid_spec=pltpu.PrefetchScalarGridSpec(
            num_scalar_prefetch=1, grid=(N_CHIPS, n),
            in_specs=[pl.BlockSpec((None, th, c), lambda k, i, core_ref: (k, core_ref[0] * skip + i, 0)), blk],
            out_specs=[blk, blk]),
        out_shape=[jax.ShapeDtypeStruct((N_CHIPS, half, c), F32), jax.ShapeDtypeStruct((N_CHIPS, half, c), BF16)],
        compiler_params=_params("parallel", "parallel"),
    )(core, grad, got)


def _riding(comm, grid, body, n_pre, n_in, n_out):
    if comm is None:
        return body
    c_in, c_out, c_scr = len(comm.ins), len(comm.out_shape), len(comm.scratch)

    def wrapped(*refs):
        pre, rest = refs[:n_pre], refs[n_pre:]
        ins, rest = rest[:n_in], rest[n_in:]
        cin, rest = rest[:c_in], rest[c_in:]
        outs, rest = rest[:n_out], rest[n_out:]
        cout, rest = rest[:c_out], rest[c_out:]
        own, sems = rest[:len(rest) - c_scr], rest[len(rest) - c_scr:]
        ids = [pl.program_id(ax) for ax in range(len(grid))]
        first, last = ids[0] == 0, ids[0] == grid[0] - 1
        for ax in range(1, len(grid)):
            first, last = first & (ids[ax] == 0), last & (ids[ax] == grid[ax] - 1)

        @pl.when(first)
        def _():
            comm.start(cin, cout, sems)

        body(*pre, *ins, *outs, *own)

        @pl.when(last)
        def _():
            comm.finish(cin, cout, sems)

    return wrapped


def _total(name, chip, core, sum32, got16, comm=None):
    _, half, c = sum32.shape
    th = _tile(half, 128)
    n = half // th

    def body(chip_ref, core_ref, own_ref, a_ref, b_ref, c_ref, o_ref):
        del chip_ref, core_ref
        o_ref[...] = ((own_ref[...] + a_ref[...].astype(F32)) + b_ref[...].astype(F32)) + c_ref[...].astype(F32)

    def other(step):
        return pl.BlockSpec((None, th, c), lambda i, chip_ref, core_ref: ((chip_ref[0] + step) % N_CHIPS, i, 0))

    c_ins = list(comm.ins) if comm else []
    res = pl.pallas_call(
        _riding(comm, (n,), body, 2, 4, 1), name=name,
        grid_spec=pltpu.PrefetchScalarGridSpec(
            num_scalar_prefetch=2, grid=(n,),
            in_specs=[other(0), other(1), other(2), other(3)] + [ANY] * len(c_ins),
            out_specs=[pl.BlockSpec((th, c), lambda i, chip_ref, core_ref: (core_ref[0] * n + i, 0))]
            + [ANY] * len(c_ins),
            scratch_shapes=list(comm.scratch) if comm else []),
        out_shape=[jax.ShapeDtypeStruct((2 * half, c), F32)] + (list(comm.out_shape) if comm else []),
        input_output_aliases={6 + i: 1 + i for i in range(len(c_ins))} if comm else {},
        compiler_params=_params("arbitrary" if comm else "parallel"),
    )(chip, core, sum32, got16, got16, got16, *c_ins)
    return (res[0], res[1:]) if comm else res[0]


class _JoinHalves:
    in_place = True

    def __init__(self, blocks):
        self.ins = list(blocks)
        self.out_shape = [jax.ShapeDtypeStruct(t.shape, t.dtype) for t in blocks]
        self.scratch = [pltpu.SemaphoreType.DMA((len(blocks),)), pltpu.SemaphoreType.DMA((len(blocks),))]

    @staticmethod
    def _copy(ref, w, mine, sems):
        x, y, c, _ = _place()
        half = ref.shape[0] // 2
        rows = ref.at[pl.ds((c if mine else 1 - c) * half, half)]
        return _remote(rows, rows, sems[0], sems[1], w, (x, y, 1 - c))

    def start(self, ins, outs, sems):
        for w, ref in enumerate(outs):
            self._copy(ref, w, True, sems).start()

    def finish(self, ins, outs, sems):
        for w, ref in enumerate(outs):
            self._copy(ref, w, True, sems).wait_send()
            self._copy(ref, w, False, sems).wait_recv()


class _Both:
    in_place = True

    def __init__(self, a, b):
        assert a.in_place and b.in_place
        self.parts = (a, b)
        self.ins = list(a.ins) + list(b.ins)
        self.out_shape = list(a.out_shape) + list(b.out_shape)
        self.scratch = list(a.scratch) + list(b.scratch)

    def _split(self, ins, outs, sems):
        a = self.parts[0]
        n, k = len(a.ins), len(a.scratch)
        return (a, ins[:n], outs[:n], sems[:k]), (self.parts[1], ins[n:], outs[n:], sems[k:])

    def start(self, ins, outs, sems):
        for part, i, o, s in self._split(ins, outs, sems):
            part.start(i, o, s)

    def finish(self, ins, outs, sems):
        for part, i, o, s in self._split(ins, outs, sems):
            part.finish(i, o, s)


def _alone(name, comm):
    n = len(comm.ins)

    def body(*refs):
        comm.start(refs[:n], refs[n:2 * n], refs[2 * n:])
        comm.finish(refs[:n], refs[n:2 * n], refs[2 * n:])

    return pl.pallas_call(
        body, name=name, in_specs=[ANY] * n, out_specs=[ANY] * n, out_shape=comm.out_shape,
        scratch_shapes=comm.scratch, input_output_aliases={i: i for i in range(n)},
    )(*comm.ins)


class _GatherSmall:
    in_place = True

    def __init__(self, full):
        self.ins = [full]
        self.out_shape = [jax.ShapeDtypeStruct(full.shape, full.dtype)]
        self.scratch = [pltpu.SemaphoreType.DMA((7,)), pltpu.SemaphoreType.DMA((7,))]

    @staticmethod
    def _copy(ref, sems, k, block, to):
        m_per = ref.shape[0] // N_DEV
        px, py, pc = block
        rows = ref.at[pl.ds((4 * px + 2 * py + pc) * m_per, m_per), :]
        return _remote(rows, rows, sems[0], sems[1], k, to)

    def _first(self, ref, sems):
        x, y, c, chips = _place()
        me = (x, y, c)
        return [self._copy(ref, sems, 0, me, (x, y, 1 - c))] + \
               [self._copy(ref, sems, 1 + j, me, (*chip, c)) for j, chip in enumerate(chips)]

    def start(self, ins, outs, sems):
        for cp in self._first(outs[0], sems):
            cp.start()

    def finish(self, ins, outs, sems):
        ref = outs[0]
        x, y, c, chips = _place()
        me, sibling = (x, y, c), (x, y, 1 - c)
        passed = [self._copy(ref, sems, 4 + j, (*chip, c), sibling) for j, chip in enumerate(chips)]
        for j, chip in enumerate(chips):
            self._copy(ref, sems, 1 + j, (*chip, c), me).wait_recv()
            passed[j].start()
        self._copy(ref, sems, 0, sibling, me).wait_recv()
        for j, chip in enumerate(chips):
            self._copy(ref, sems, 4 + j, (*chip, 1 - c), me).wait_recv()
        for cp in self._first(ref, sems) + passed:
            cp.wait_send()


def _adamw_math(w, g, m, v):
    m = ADAM_B1 * m + (1.0 - ADAM_B1) * g
    v = ADAM_B2 * v + (1.0 - ADAM_B2) * (g * g)
    m_hat = m / (1.0 - ADAM_B1 ** ADAM_STEP)
    v_hat = v / (1.0 - ADAM_B2 ** ADAM_STEP)
    delta = -ADAM_LR * (m_hat / (jnp.sqrt(v_hat) + ADAM_EPS) + ADAM_WD * w)
    return delta, m, v


def _adamw(name, w, g, m, v):
    r, c = w.shape
    tm = _tile(r, 128)

    def body(w_ref, g_ref, m_ref, v_ref, g_out, d_out, m_out, v_out):
        g = g_ref[...]
        g_out[...] = g
        d_out[...], m_out[...], v_out[...] = _adamw_math(w_ref[...], g, m_ref[...], v_ref[...])

    spec = pl.BlockSpec((tm, c), lambda i: (i, 0))
    return pl.pallas_call(
        body, name=name, grid=(r // tm,), in_specs=[spec] * 4, out_specs=[spec] * 4,
        out_shape=[jax.ShapeDtypeStruct((r, c), F32)] * 4, compiler_params=_params("parallel"),
    )(w, g, m, v)


def _adamw_small(gathered, w, m, v):
    rows = w.shape[0]

    def body(all_ref, w_ref, m_ref, v_ref, g_out, d_out, m_out, v_out):
        g = all_ref[0:rows, :]
        for dev in range(1, N_DEV):
            g = g + all_ref[dev * rows:(dev + 1) * rows, :]
        g_out[...] = g
        d_out[...], m_out[...], v_out[...] = _adamw_math(w_ref[...], g, m_ref[...], v_ref[...])

    return pl.pallas_call(
        body, name="adamw_small", out_shape=[jax.ShapeDtypeStruct(w.shape, F32)] * 4, compiler_params=_params(),
    )(gathered, w, m, v)


SMALL = ("g_pre", "w_s", "b_s", "ln_v_g", "ln_v_b", "g_q", "g_k", "rel_bias", "g_out_a", "g_out_b", "g_ple")
LARGE = ("w_in", "w_out", "w_ple_gate", "w_ple_up")
WEIGHTS = ("g_pre", "w_in", "w_s", "b_s", "ln_v_g", "ln_v_b", "g_q", "g_k", "rel_bias", "g_out_a", "g_out_b", "w_out",
           "g_ple", "w_ple_gate", "w_ple_up")


def _pack(parts):
    flat = jnp.concatenate([parts[n].reshape(-1).astype(F32) for n in SMALL])
    rows = -(-flat.shape[0] // (8 * LANE)) * 8
    return jnp.pad(flat, (0, rows * LANE - flat.shape[0])).reshape(rows, LANE)


def _unpack(pack, like):
    flat = pack.reshape(-1)
    out, at = {}, 0
    for n in SMALL:
        size = math.prod(like[n].shape)
        out[n] = flat[at:at + size].reshape(like[n].shape)
        at += size
    return out


def kernel(x, p, g_pre, w_in, w_s, b_s, ln_v_g, ln_v_b, g_q, g_k, rel_bias, g_out_a, g_out_b, w_out, g_ple, w_ple_gate, w_ple_up, loss_target, m_g_pre, m_w_in, m_w_s, m_b_s, m_ln_v_g, m_ln_v_b, m_g_q, m_g_k, m_rel_bias, m_g_out_a, m_g_out_b, m_w_out, m_g_ple, m_w_ple_gate, m_w_ple_up, v_g_pre, v_w_in, v_w_s, v_b_s, v_ln_v_g, v_ln_v_b, v_g_q, v_g_k, v_rel_bias, v_g_out_a, v_g_out_b, v_w_out, v_g_ple, v_w_ple_gate, v_w_ple_up):
    given = dict(locals())
    weights = {n: given[n] for n in WEIGHTS}
    mom_m = {n: given["m_" + n] for n in WEIGHTS}
    mom_v = {n: given["v_" + n] for n in WEIGHTS}
    xs, ps, tgt = x[0], p[0, 0], loss_target[0]
    d = xs.shape[1]

    core = lax.axis_index("c").astype(jnp.int32).reshape(1)
    chip = (2 * lax.axis_index("x") + lax.axis_index("y")).astype(jnp.int32).reshape(1)

    win4, wout4, wgate4, wup4 = [_cast_bf16("cast_" + n, chip, weights[n][0]) for n in LARGE]

    sp = {
        "g_pre": g_pre, "w_s": w_s[0], "b_s": b_s[0], "ln_v_g": ln_v_g, "ln_v_b": ln_v_b, "g_q": g_q, "g_k": g_k,
        "rel_bias": rel_bias, "g_out_a": g_out_a, "g_out_b": g_out_b, "g_ple": g_ple,
    }
    loss_local, grad_x, (sums32, got16), small = _local_step(xs, ps, tgt, sp, win4, wout4, wgate4, wup4,
                                                             dist={"chip": chip, "core": core})
    loss = lax.psum(loss_local, MESH_AXES)

    own = dict(zip(LARGE, sums32))
    got = dict(zip(LARGE, got16))
    halves = [_total("total_" + n, chip, core, own[n], got[n]) for n in LARGE[1:]]
    my_pack = _pack(small)
    device = 2 * chip[0] + core[0]
    everyone = lax.dynamic_update_slice(jnp.zeros((N_DEV * my_pack.shape[0], LANE), F32), my_pack,
                                        (device * my_pack.shape[0], 0))
    half_in, (*joined, gathered) = _total("total_w_in", chip, core, own["w_in"], got["w_in"],
                                          comm=_Both(_JoinHalves(halves), _GatherSmall(everyone)))
    grads = dict(zip(LARGE[1:], joined))
    grads["w_in"], = _alone("join_w_in", _JoinHalves([half_in]))

    out_g, out_d, out_m, out_v = {}, {}, {}, {}
    for n in LARGE:
        results = _adamw("adamw_" + n, weights[n][0], grads[n], mom_m[n][0], mom_v[n][0])
        out_g[n], out_d[n], out_m[n], out_v[n] = [r[None] for r in results]

    pg, pd, pm, pv = _adamw_small(gathered, _pack(weights), _pack(mom_m), _pack(mom_v))
    for packed, out in ((pg, out_g), (pd, out_d), (pm, out_m), (pv, out_v)):
        out.update(_unpack(packed, weights))

    return (loss, grad_x[None], *[out_g[n] for n in WEIGHTS], *[out_d[n] for n in WEIGHTS],
            *[out_m[n] for n in WEIGHTS], *[out_v[n] for n in WEIGHTS])
```

```python
import functools
import math

import numpy as np

import jax
import jax.numpy as jnp
from jax import lax
from jax.experimental import pallas as pl
from jax.experimental.pallas import tpu as pltpu

F32 = jnp.float32
BF16 = jnp.bfloat16

HEAD_DIM = 128
CHUNK = 128
BLK = 128
DILATED = ((128, 1), (512, 4), (2048, 16))
NUM_BUCKETS = 32
MAX_DISTANCE = 2048
ATTN_GROUP = 4
EPS = 1e-6
NEG_INF = -1e30
N_CHIPS = 4
N_DEV = 8
MESH_AXES = ("x", "y", "c")

ADAM_LR = 0.001
ADAM_B1 = 0.9
ADAM_B2 = 0.999
ADAM_EPS = 1e-08
ADAM_WD = 0.01
ADAM_STEP = 10

V7X_VMEM_LIMIT = 56 * 1024 * 1024
LANE = 128
MESH = pl.DeviceIdType.MESH
ANY = pl.BlockSpec(memory_space=pl.ANY)


def _params(*sem):
    return pltpu.CompilerParams(dimension_semantics=sem or None, vmem_limit_bytes=V7X_VMEM_LIMIT)


def _tile(n, target):
    if n <= target:
        return n
    t = (target // LANE) * LANE
    while t > LANE and n % t:
        t -= LANE
    assert n % t == 0, (n, target)
    return t


def _gelu(x):
    return 0.5 * x * (1.0 + lax.erf(x * (1.0 / math.sqrt(2.0))))


def _gelu_grad(x):
    return 0.5 * (1.0 + lax.erf(x * (1.0 / math.sqrt(2.0)))) + x * jnp.exp(-0.5 * x * x) * (1.0 / math.sqrt(2.0 * math.pi))


def _silu_and_grad(x):
    s = jax.nn.sigmoid(x)
    return x * s, s * (1.0 + x * (1.0 - s))


def _dot(a, b, ta=False, tb=False):
    return lax.dot_general(a, b, (((0 if ta else 1,), (1 if tb else 0,)), ((), ())), preferred_element_type=F32)


def _colsum8(v):
    return jnp.sum(v.reshape(v.shape[0] // 8, 8, v.shape[1]), axis=0)


def _matmul(name, a, b, a_spec, b_spec, grid, acc_shape, out_shape, out_specs, ta=False, tb=False,
            extras=(), extra_specs=(), epilogue=None, comm=None, prefetch=()):
    nk = grid[2]
    n_pre = len(prefetch)
    n_extra = len(extras)
    single = not isinstance(out_shape, (tuple, list))
    outs_shape = (out_shape,) if single else tuple(out_shape)
    outs_specs = (out_specs,) if single else tuple(out_specs)
    n_out = len(outs_shape)
    c_ins = list(comm.ins) if comm else []
    c_outs = list(comm.out_shape) if comm else []
    c_scratch = list(comm.scratch) if comm else []
    n_cin, n_cout = len(c_ins), len(c_outs)

    def finish(acc, extra_refs, out_refs):
        if epilogue is None:
            out_refs[0][...] = acc.astype(out_refs[0].dtype)
        else:
            epilogue(acc, extra_refs, out_refs)

    def body(*refs):
        a_ref, b_ref, *rest = refs[n_pre:]
        extra_refs = rest[:n_extra]
        cin_refs = rest[n_extra:n_extra + n_cin]
        out_refs = rest[n_extra + n_cin:n_extra + n_cin + n_out]
        cout_refs = rest[n_extra + n_cin + n_out:n_extra + n_cin + n_out + n_cout]
        scratch_refs = rest[n_extra + n_cin + n_out + n_cout:]
        ids = [pl.program_id(ax) for ax in range(3)]
        if comm:
            sems = scratch_refs[len(scratch_refs) - len(c_scratch):]

            @pl.when((ids[0] == 0) & (ids[1] == 0) & (ids[2] == 0))
            def _():
                comm.start(cin_refs, cout_refs, sems)

        if nk == 1:
            finish(_dot(a_ref[...], b_ref[...], ta, tb), extra_refs, out_refs)
        else:
            acc_ref = scratch_refs[0]

            @pl.when(ids[2] == 0)
            def _():
                acc_ref[...] = jnp.zeros_like(acc_ref)

            acc_ref[...] += _dot(a_ref[...], b_ref[...], ta, tb)

            @pl.when(ids[2] == nk - 1)
            def _():
                finish(acc_ref[...], extra_refs, out_refs)

        if comm:
            @pl.when((ids[0] == grid[0] - 1) & (ids[1] == grid[1] - 1) & (ids[2] == nk - 1))
            def _():
                comm.finish(cin_refs, cout_refs, sems)

    scratch = ([] if nk == 1 else [pltpu.VMEM(acc_shape, F32)]) + c_scratch
    aliases = {n_pre + 2 + n_extra + i: n_out + i for i in range(n_cin)} if (comm and comm.in_place) else {}
    res = pl.pallas_call(
        body, name=name,
        grid_spec=pltpu.PrefetchScalarGridSpec(
            num_scalar_prefetch=n_pre, grid=grid,
            in_specs=[a_spec, b_spec, *extra_specs] + [ANY] * n_cin,
            out_specs=list(outs_specs) + [ANY] * n_cout, scratch_shapes=scratch),
        out_shape=list(outs_shape) + c_outs,
        input_output_aliases=aliases,
        compiler_params=_params(*(("arbitrary",) * 3 if comm else ("parallel", "parallel", "arbitrary"))),
    )(*prefetch, a, b, *extras, *c_ins)
    if comm:
        main = res[:n_out]
        return (main[0] if single else main), res[n_out:]
    return res[0] if single else res


def _shard_spec(rows, cols, per_shard, row_axis, col_axis):
    def index(i, j, k):
        g = (i, j, k)
        return (g[col_axis] // per_shard, g[row_axis], g[col_axis] % per_shard)
    return pl.BlockSpec((None, rows, cols), index)


def _rms_fwd(name, x, g):
    s, d = x.shape
    tm = _tile(s, 256)

    def body(x_ref, g_ref, y_ref, r_ref):
        xf = x_ref[...]
        r = lax.rsqrt(jnp.mean(xf * xf, axis=-1, keepdims=True) + EPS)
        y_ref[...] = (xf * r * g_ref[...]).astype(BF16)
        r_ref[...] = r

    return pl.pallas_call(
        body, name=name, grid=(s // tm,),
        in_specs=[pl.BlockSpec((tm, d), lambda i: (i, 0)), pl.BlockSpec((1, d), lambda i: (0, 0))],
        out_specs=(pl.BlockSpec((tm, d), lambda i: (i, 0)), pl.BlockSpec((tm, 1), lambda i: (i, 0))),
        out_shape=(jax.ShapeDtypeStruct((s, d), BF16), jax.ShapeDtypeStruct((s, 1), F32)),
        compiler_params=_params("parallel"),
    )(x, g)


def _rms_bwd(name, dy, x, r, g, skip, with_bf16):
    s, d = x.shape
    tm = _tile(s, 256)
    n = s // tm

    def body(dy_ref, x_ref, r_ref, g_ref, skip_ref, *outs):
        dx_ref = outs[0]
        dg_ref = outs[-2]
        acc_ref = outs[-1]
        i = pl.program_id(0)
        dyv, xv, rv = dy_ref[...], x_ref[...], r_ref[...]

        @pl.when(i == 0)
        def _():
            acc_ref[...] = jnp.zeros_like(acc_ref)

        acc_ref[...] += _colsum8(dyv * xv * rv)
        dg = dyv * g_ref[...]
        dx = skip_ref[...] + rv * (dg - xv * (rv * rv) * jnp.mean(dg * xv, axis=-1, keepdims=True))
        dx_ref[...] = dx
        if with_bf16:
            outs[1][...] = dx.astype(BF16)

        @pl.when(i == n - 1)
        def _():
            dg_ref[...] = jnp.sum(acc_ref[...], axis=0, keepdims=True)

    row = pl.BlockSpec((tm, d), lambda i: (i, 0))
    vec = pl.BlockSpec((1, d), lambda i: (0, 0))
    out_specs = [row] + ([row] if with_bf16 else []) + [vec]
    out_shape = [jax.ShapeDtypeStruct((s, d), F32)] + ([jax.ShapeDtypeStruct((s, d), BF16)] if with_bf16 else []) \
        + [jax.ShapeDtypeStruct((1, d), F32)]
    return pl.pallas_call(
        body, name=name, grid=(n,),
        in_specs=[row, row, pl.BlockSpec((tm, 1), lambda i: (i, 0)), vec, row],
        out_specs=out_specs, out_shape=out_shape, scratch_shapes=[pltpu.VMEM((8, d), F32)],
        compiler_params=_params("arbitrary"),
    )(dy, x, r, g, skip)


def _causal(w):
    t = lax.broadcasted_iota(jnp.int32, w.shape, 0)
    s_ = lax.broadcasted_iota(jnp.int32, w.shape, 1)
    return jnp.where(t >= s_, w, 0.0)


def _gmlp_fwd(proj, w_s, b_st, ln_g, ln_b, g_out, d_model):
    s = proj.shape[0]
    wa = d_model // 2
    groups = wa // HEAD_DIM
    tm = _tile(s, 256)
    n_chunks = tm // CHUNK

    def body(au_ref, av_ref, az_ref, ws_ref, bst_ref, lng_ref, lnb_ref, go_ref,
             y_ref, z_ref, vn_ref, mu_ref, rs_ref, ra_ref):
        gv = _gelu(av_ref[...])
        mu = jnp.mean(gv, axis=-1, keepdims=True)
        xc = gv - mu
        rs = lax.rsqrt(jnp.mean(xc * xc, axis=-1, keepdims=True) + EPS)
        vn = (xc * rs * lng_ref[...] + lnb_ref[...]).astype(BF16)
        vn_ref[...] = vn
        mu_ref[...] = mu
        rs_ref[...] = rs
        for g in range(groups):
            wm = _causal(ws_ref[g]).astype(BF16)
            cols = slice(g * HEAD_DIM, (g + 1) * HEAD_DIM)
            for ch in range(n_chunks):
                rows = slice(ch * CHUNK, (ch + 1) * CHUNK)
                z_ref[rows, cols] = _dot(wm, vn_ref[rows, cols]) + bst_ref[:, g:g + 1]
        ya = _gelu(au_ref[...]) * z_ref[...]
        ra = lax.rsqrt(jnp.mean(ya * ya, axis=-1, keepdims=True) + EPS)
        ra_ref[...] = ra
        sz, _ = _silu_and_grad(az_ref[...])
        y_ref[...] = (ya * ra * go_ref[...] * sz).astype(BF16)

    def col(j):
        return pl.BlockSpec((tm, wa), lambda i: (i, j))
    vec = pl.BlockSpec((1, wa), lambda i: (0, 0))
    stat = pl.BlockSpec((tm, 1), lambda i: (i, 0))
    return pl.pallas_call(
        body, name="gmlp_fwd", grid=(s // tm,),
        in_specs=[col(0), col(1), col(2),
                  pl.BlockSpec((groups, CHUNK, CHUNK), lambda i: (0, 0, 0)),
                  pl.BlockSpec((CHUNK, groups), lambda i: (0, 0)), vec, vec, vec],
        out_specs=(col(0), col(0), col(0), stat, stat, stat),
        out_shape=(jax.ShapeDtypeStruct((s, d_model), BF16), jax.ShapeDtypeStruct((s, wa), F32),
                   jax.ShapeDtypeStruct((s, wa), BF16), jax.ShapeDtypeStruct((s, 1), F32),
                   jax.ShapeDtypeStruct((s, 1), F32), jax.ShapeDtypeStruct((s, 1), F32)),
        compiler_params=_params("parallel"),
    )(proj, proj, proj, w_s, b_st, ln_g, ln_b, g_out)


def _gmlp_bwd(dproj, dy, proj, z, vn, mu, rs, ra, w_s, ln_g, g_out, d_model):
    s = proj.shape[0]
    wa = d_model // 2
    groups = wa // HEAD_DIM
    tm = _tile(s, 256)
    n_chunks = tm // CHUNK
    n = s // tm

    def causal_stack(w):
        t = lax.broadcasted_iota(jnp.int32, w.shape, 1)
        s_ = lax.broadcasted_iota(jnp.int32, w.shape, 2)
        return jnp.where(t >= s_, w, 0.0)

    def body(dproj_in, dy_ref, au_ref, av_ref, az_ref, z_ref, vn_ref, mu_ref, rs_ref, ra_ref, ws_ref, lng_ref, go_ref,
             dp_ref, gws_ref, dzs_ref, glg_ref, glb_ref, ggo_ref,
             dz_s, dvn_s, acc_lg, acc_lb, acc_go):
        del dproj_in
        i = pl.program_id(0)

        @pl.when(i == 0)
        def _():
            gws_ref[...] = jnp.zeros_like(gws_ref)
            dzs_ref[...] = jnp.zeros_like(dzs_ref)
            acc_lg[...] = jnp.zeros_like(acc_lg)
            acc_lb[...] = jnp.zeros_like(acc_lb)
            acc_go[...] = jnp.zeros_like(acc_go)

        au, az, zv, rav = au_ref[...], az_ref[...], z_ref[...], ra_ref[...]
        u = _gelu(au)
        ya = u * zv
        sz, dsz = _silu_and_grad(az)
        dyv = dy_ref[...]
        dp_ref[:, 2 * wa:3 * wa] = (dyv * (ya * rav * go_ref[...]) * dsz).astype(BF16)
        dn = dyv * sz
        acc_go[...] += _colsum8(dn * ya * rav)
        dyg = dn * go_ref[...]
        dya = rav * (dyg - ya * (rav * rav) * jnp.mean(dyg * ya, axis=-1, keepdims=True))
        dp_ref[:, 0:wa] = (dya * zv * _gelu_grad(au)).astype(BF16)
        dz_s[...] = dya * u
        for ch in range(n_chunks):
            dzs_ref[...] += dz_s[ch * CHUNK:(ch + 1) * CHUNK, :]
        for g in range(groups):
            wm = _causal(ws_ref[g]).astype(BF16)
            cols = slice(g * HEAD_DIM, (g + 1) * HEAD_DIM)
            for ch in range(n_chunks):
                rows = slice(ch * CHUNK, (ch + 1) * CHUNK)
                dzb = dz_s[rows, cols].astype(BF16)
                gws_ref[g] += _dot(dzb, vn_ref[rows, cols], tb=True)
                dvn_s[rows, cols] = _dot(wm, dzb, ta=True)
        av = av_ref[...]
        xh = (_gelu(av) - mu_ref[...]) * rs_ref[...]
        dvn = dvn_s[...]
        acc_lb[...] += _colsum8(dvn)
        acc_lg[...] += _colsum8(dvn * xh)
        dxh = dvn * lng_ref[...]
        dgv = rs_ref[...] * (dxh - jnp.mean(dxh, axis=-1, keepdims=True) - xh * jnp.mean(dxh * xh, axis=-1, keepdims=True))
        dp_ref[:, wa:2 * wa] = (dgv * _gelu_grad(av)).astype(BF16)

        @pl.when(i == n - 1)
        def _():
            gws_ref[...] = causal_stack(gws_ref[...])
            glg_ref[...] = jnp.sum(acc_lg[...], axis=0, keepdims=True)
            glb_ref[...] = jnp.sum(acc_lb[...], axis=0, keepdims=True)
            ggo_ref[...] = jnp.sum(acc_go[...], axis=0, keepdims=True)

    def col(j):
        return pl.BlockSpec((tm, wa), lambda i: (i, j))
    vec = pl.BlockSpec((1, wa), lambda i: (0, 0))
    stat = pl.BlockSpec((tm, 1), lambda i: (i, 0))
    ws_spec = pl.BlockSpec((groups, CHUNK, CHUNK), lambda i: (0, 0, 0))
    d_in = dproj.shape[1]
    return pl.pallas_call(
        body, name="gmlp_bwd", grid=(n,),
        in_specs=[ANY, col(0), col(0), col(1), col(2), col(0), col(0), stat, stat, stat, ws_spec, vec, vec],
        out_specs=(pl.BlockSpec((tm, 3 * wa), lambda i: (i, 0)), ws_spec,
                   pl.BlockSpec((CHUNK, wa), lambda i: (0, 0)), vec, vec, vec),
        out_shape=(jax.ShapeDtypeStruct((s, d_in), BF16),
                   jax.ShapeDtypeStruct((groups, CHUNK, CHUNK), F32), jax.ShapeDtypeStruct((CHUNK, wa), F32))
        + (jax.ShapeDtypeStruct((1, wa), F32),) * 3,
        scratch_shapes=[pltpu.VMEM((tm, wa), F32), pltpu.VMEM((tm, wa), F32)] + [pltpu.VMEM((8, wa), F32)] * 3,
        input_output_aliases={0: 0},
        compiler_params=_params("arbitrary"),
    )(dproj, dy, proj, proj, proj, z, vn, mu, rs, ra, w_s, ln_g, g_out)


def _gate_fwd(y_in, yb, proj, g_out, d_model):
    s = proj.shape[0]
    wa = d_model // 2
    tm = _tile(s, 256)

    def body(y_any, yb_ref, bz_ref, go_ref, y_ref, rb_ref):
        del y_any
        ybv = yb_ref[...]
        rb = lax.rsqrt(jnp.mean(ybv * ybv, axis=-1, keepdims=True) + EPS)
        rb_ref[...] = rb
        sz, _ = _silu_and_grad(bz_ref[...])
        y_ref[...] = (ybv * rb * go_ref[...] * sz).astype(BF16)

    return pl.pallas_call(
        body, name="gate_b_fwd", grid=(s // tm,),
        in_specs=[ANY, pl.BlockSpec((tm, wa), lambda i: (i, 0)), pl.BlockSpec((tm, wa), lambda i: (i, 6)),
                  pl.BlockSpec((1, wa), lambda i: (0, 0))],
        out_specs=(pl.BlockSpec((tm, wa), lambda i: (i, 1)), pl.BlockSpec((tm, 1), lambda i: (i, 0))),
        out_shape=(jax.ShapeDtypeStruct((s, d_model), BF16), jax.ShapeDtypeStruct((s, 1), F32)),
        input_output_aliases={0: 0},
        compiler_params=_params("parallel"),
    )(y_in, yb, proj, g_out)


def _gate_bwd(dy, yb, proj, rb, g_out, d_model):
    s, d_in = proj.shape
    wa = d_model // 2
    tm = _tile(s, 256)
    n = s // tm

    def body(dy_ref, yb_ref, bz_ref, rb_ref, go_ref, dp_ref, do_ref, ggo_ref, acc):
        i = pl.program_id(0)

        @pl.when(i == 0)
        def _():
            acc[...] = jnp.zeros_like(acc)

        dyv, ybv, rbv = dy_ref[...], yb_ref[...], rb_ref[...]
        sz, dsz = _silu_and_grad(bz_ref[...])
        dp_ref[...] = (dyv * (ybv * rbv * go_ref[...]) * dsz).astype(BF16)
        dn = dyv * sz
        acc[...] += _colsum8(dn * ybv * rbv)
        dyg = dn * go_ref[...]
        do_ref[...] = rbv * (dyg - ybv * (rbv * rbv) * jnp.mean(dyg * ybv, axis=-1, keepdims=True))

        @pl.when(i == n - 1)
        def _():
            ggo_ref[...] = jnp.sum(acc[...], axis=0, keepdims=True)

    vec = pl.BlockSpec((1, wa), lambda i: (0, 0))
    return pl.pallas_call(
        body, name="gate_b_bwd", grid=(n,),
        in_specs=[pl.BlockSpec((tm, wa), lambda i: (i, 1)), pl.BlockSpec((tm, wa), lambda i: (i, 0)),
                  pl.BlockSpec((tm, wa), lambda i: (i, 6)), pl.BlockSpec((tm, 1), lambda i: (i, 0)), vec],
        out_specs=(pl.BlockSpec((tm, wa), lambda i: (i, 6)), pl.BlockSpec((tm, wa), lambda i: (i, 0)), vec),
        out_shape=(jax.ShapeDtypeStruct((s, d_in), BF16), jax.ShapeDtypeStruct((s, wa), F32),
                   jax.ShapeDtypeStruct((1, wa), F32)),
        scratch_shapes=[pltpu.VMEM((8, wa), F32)],
        compiler_params=_params("arbitrary"),
    )(dy, yb, proj, rb, g_out)


def _bucket_tables():
    qi = BLK + np.arange(BLK)
    kj = np.arange(2 * BLK)
    delta = qi[:, None] - kj[None, :]
    max_exact = NUM_BUCKETS // 2
    tabs = []
    for window, dil in DILATED:
        band = (delta >= 0) & (delta <= window // dil)
        dist = np.clip(delta, 0, None) * dil
        d = np.maximum(dist, 1).astype(np.float32)
        large = max_exact + (np.log(d / np.float32(max_exact)) / np.float32(math.log(MAX_DISTANCE / max_exact))
                             * np.float32(NUM_BUCKETS - max_exact)).astype(np.int32)
        large = np.minimum(large, NUM_BUCKETS - 1)
        tabs.append(np.where(band, np.where(dist < max_exact, dist, large), -1).astype(np.int32))
    return np.stack(tabs)


def _bias_tiles(tab_ref, rb_ref, h, bias_s):
    col = lax.broadcasted_iota(jnp.int32, (BLK, 2 * BLK), 1)
    for i in range(len(DILATED)):
        t = tab_ref[i]
        bias = jnp.zeros(t.shape, F32)
        for b in range(NUM_BUCKETS):
            bias = jnp.where(t == b, rb_ref[b, h], bias)
        bias = jnp.where(t >= 0, bias, NEG_INF)
        bias_s[2 * i] = jnp.where(col >= BLK, bias, NEG_INF)
        bias_s[2 * i + 1] = bias


def _block_rows(b, n_blocks, dil):
    nb = n_blocks // dil
    r = b // nb
    n = b % nb
    start = r + dil * BLK * n
    if dil == 1:
        return pl.ds(pl.multiple_of(start, BLK), BLK), n
    return pl.ds(start, BLK, stride=dil), n


def _sub_block(b, rows=BLK, pad=0):
    return pl.ds(pl.multiple_of(b * BLK + pad, BLK), rows)


def _rms_rows(x, gain):
    return x * lax.rsqrt(jnp.mean(x * x, axis=-1, keepdims=True) + EPS) * gain


def _attn_fwd(proj, g_q, g_k, rel_bias, d_model, comm=None):
    s = proj.shape[0]
    heads = d_model // 2 // HEAD_DIM
    n_blocks = s // BLK
    scale = HEAD_DIM ** -0.5
    tables = jnp.asarray(_bucket_tables())
    c_ins = list(comm.ins) if comm else []
    c_outs = list(comm.out_shape) if comm else []
    c_scratch = list(comm.scratch) if comm else []

    def body(q_ref, k_ref, v_ref, gq_ref, gk_ref, rb_ref, tab_ref, *rest):
        cin_refs, rest = rest[:len(c_ins)], rest[len(c_ins):]
        yb_ref, lse_ref = rest[:2]
        cout_refs, rest = rest[2:2 + len(c_outs)], rest[2 + len(c_outs):]
        qs, ks, vs, m_s, l_s, acc_s, bias_s, sc_s, p_s = rest[:9]
        sems = rest[9:]
        h = pl.program_id(0)
        if comm:
            @pl.when(h == 0)
            def _():
                comm.start(cin_refs, cout_refs, sems)

        _bias_tiles(tab_ref, rb_ref, h, bias_s)
        gq = gq_ref[...] * scale
        gk = gk_ref[...]
        ks[0:BLK, :] = jnp.zeros((BLK, HEAD_DIM), BF16)
        vs[0:BLK, :] = jnp.zeros((BLK, HEAD_DIM), BF16)

        for i, (_, dil) in enumerate(DILATED):
            def prepare(b, carry, dil=dil):
                rows, _ = _block_rows(b, n_blocks, dil)
                qs[_sub_block(b), :] = _rms_rows(q_ref[rows, :], gq).astype(BF16)
                ks[_sub_block(b, pad=BLK), :] = _rms_rows(k_ref[rows, :], gk).astype(BF16)
                vs[_sub_block(b, pad=BLK), :] = v_ref[rows, :].astype(BF16)
                return carry
            lax.fori_loop(0, n_blocks, prepare, 0, unroll=8)

            def group(g, carry, i=i, dil=dil):
                blocks = [g * ATTN_GROUP + t for t in range(ATTN_GROUP)]
                for t, b in enumerate(blocks):
                    sc_s[t] = _dot(qs[_sub_block(b), :], ks[_sub_block(b, 2 * BLK), :], tb=True)
                for t, b in enumerate(blocks):
                    rows, n = _block_rows(b, n_blocks, dil)
                    sc = sc_s[t] + bias_s[2 * i + jnp.minimum(n, 1)]
                    m_i = jnp.max(sc, axis=-1, keepdims=True)
                    pr = jnp.exp(sc - m_i)
                    l_i = jnp.sum(pr, axis=-1, keepdims=True)
                    p_s[t] = pr.astype(BF16)
                    m_s[i, rows, :] = m_i
                    l_s[i, rows, :] = l_i
                for t, b in enumerate(blocks):
                    rows, _ = _block_rows(b, n_blocks, dil)
                    acc_s[i, rows, :] = _dot(p_s[t], vs[_sub_block(b, 2 * BLK), :])
                return carry
            lax.fori_loop(0, n_blocks // ATTN_GROUP, group, 0)

        m = jnp.maximum(jnp.maximum(m_s[0], m_s[1]), m_s[2])
        l = jnp.zeros_like(m)
        acc = jnp.zeros((s, HEAD_DIM), F32)
        for i in range(len(DILATED)):
            w = jnp.exp(m_s[i] - m)
            l = l + l_s[i] * w
            acc = acc + acc_s[i] * w
        yb_ref[...] = acc / l
        lse_ref[...] = m + jnp.log(l)
        if comm:
            @pl.when(h == heads - 1)
            def _():
                comm.finish(cin_refs, cout_refs, sems)

    def head_col(off):
        return pl.BlockSpec((s, HEAD_DIM), lambda h: (0, off * heads + h))
    vec = pl.BlockSpec((1, HEAD_DIM), lambda h: (0, 0))
    res = pl.pallas_call(
        body, name="attn_fwd", grid=(heads,),
        in_specs=[head_col(3), head_col(4), head_col(5), vec, vec,
                  pl.BlockSpec(memory_space=pltpu.SMEM),
                  pl.BlockSpec((len(DILATED), BLK, 2 * BLK), lambda h: (0, 0, 0))] + [ANY] * len(c_ins),
        out_specs=[pl.BlockSpec((s, HEAD_DIM), lambda h: (0, h)), pl.BlockSpec((None, s, 1), lambda h: (h, 0, 0))]
        + [ANY] * len(c_outs),
        out_shape=[jax.ShapeDtypeStruct((s, heads * HEAD_DIM), F32), jax.ShapeDtypeStruct((heads, s, 1), F32)] + c_outs,
        scratch_shapes=[pltpu.VMEM((s, HEAD_DIM), BF16), pltpu.VMEM((s + BLK, HEAD_DIM), BF16),
                        pltpu.VMEM((s + BLK, HEAD_DIM), BF16),
                        pltpu.VMEM((len(DILATED), s, 1), F32), pltpu.VMEM((len(DILATED), s, 1), F32),
                        pltpu.VMEM((len(DILATED), s, HEAD_DIM), F32),
                        pltpu.VMEM((2 * len(DILATED), BLK, 2 * BLK), F32),
                        pltpu.VMEM((ATTN_GROUP, BLK, 2 * BLK), F32), pltpu.VMEM((ATTN_GROUP, BLK, 2 * BLK), BF16)] + c_scratch,
        input_output_aliases={7 + i: 2 + i for i in range(len(c_ins))} if (comm and comm.in_place) else {},
        compiler_params=_params("arbitrary"),
    )(proj, proj, proj, g_q, g_k, rel_bias, tables, *c_ins)
    return res[0], res[1], res[2:]


def _attn_bwd(dproj, d_o, yb, lse, proj, g_q, g_k, rel_bias, d_model, comm=None):
    s, d_in = proj.shape
    heads = d_model // 2 // HEAD_DIM
    n_blocks = s // BLK
    scale = HEAD_DIM ** -0.5
    tables = jnp.asarray(_bucket_tables())
    n_dil = len(DILATED)
    c_ins = list(comm.ins) if comm else []
    c_outs = list(comm.out_shape) if comm else []
    c_scratch = list(comm.scratch) if comm else []

    def body(dp_any, q_ref, k_ref, v_ref, do_ref, o_ref, lse_ref, gq_ref, gk_ref, rb_ref, tab_ref, *rest):
        cin_refs, rest = rest[:len(c_ins)], rest[len(c_ins):]
        dp_out, ggq_ref, ggk_ref, db_ref = rest[:4]
        cout_refs, rest = rest[4:4 + len(c_outs)], rest[4 + len(c_outs):]
        (qs, ks, vs, dos, lse_s, del_s, dqn, dkn, dvv, dq_u, dk_u, dv_u, bias_s, dbias_s, sc_s, dp_s, p_s, ds_s,
         sems) = rest[:19]
        comm_sems = rest[19:]
        del dp_any
        h = pl.program_id(0)
        if comm:
            @pl.when(h == 0)
            def _():
                comm.start(cin_refs, cout_refs, comm_sems)


        @pl.when(h == 0)
        def _():
            ggq_ref[...] = jnp.zeros_like(ggq_ref)
            ggk_ref[...] = jnp.zeros_like(ggk_ref)

        dbias_s[...] = jnp.zeros_like(dbias_s)
        _bias_tiles(tab_ref, rb_ref, h, bias_s)
        gq = gq_ref[...] * scale
        gk = gk_ref[...]
        ks[0:BLK, :] = jnp.zeros((BLK, HEAD_DIM), BF16)
        vs[0:BLK, :] = jnp.zeros((BLK, HEAD_DIM), BF16)

        for i, (_, dil) in enumerate(DILATED):
            def prepare(b, carry, dil=dil):
                rows, _ = _block_rows(b, n_blocks, dil)
                qs[_sub_block(b), :] = _rms_rows(q_ref[rows, :], gq).astype(BF16)
                ks[_sub_block(b, pad=BLK), :] = _rms_rows(k_ref[rows, :], gk).astype(BF16)
                vs[_sub_block(b, pad=BLK), :] = v_ref[rows, :].astype(BF16)
                do = do_ref[rows, :]
                dos[_sub_block(b), :] = do.astype(BF16)
                del_s[_sub_block(b), :] = jnp.sum(do * o_ref[rows, :], axis=-1, keepdims=True)
                lse_s[_sub_block(b), :] = lse_ref[rows, :]
                return carry
            lax.fori_loop(0, n_blocks, prepare, 0, unroll=8)
            dk_u[...] = jnp.zeros_like(dk_u)
            dv_u[...] = jnp.zeros_like(dv_u)

            def group(g, carry, i=i, dil=dil):
                blocks = [g * ATTN_GROUP + t for t in range(ATTN_GROUP)]
                for t, b in enumerate(blocks):
                    sc_s[t] = _dot(qs[_sub_block(b), :], ks[_sub_block(b, 2 * BLK), :], tb=True)
                    dp_s[t] = _dot(dos[_sub_block(b), :], vs[_sub_block(b, 2 * BLK), :], tb=True)
                for t, b in enumerate(blocks):
                    _, n = _block_rows(b, n_blocks, dil)
                    pr = jnp.exp(sc_s[t] + bias_s[2 * i + jnp.minimum(n, 1)] - lse_s[_sub_block(b), :])
                    ds = pr * (dp_s[t] - del_s[_sub_block(b), :])
                    dbias_s[i] += ds
                    p_s[t] = pr.astype(BF16)
                    ds_s[t] = ds.astype(BF16)
                for t, b in enumerate(blocks):
                    dq_u[_sub_block(b), :] = _dot(ds_s[t], ks[_sub_block(b, 2 * BLK), :])
                    dk_u[_sub_block(b, 2 * BLK), :] += _dot(ds_s[t], qs[_sub_block(b), :], ta=True)
                    dv_u[_sub_block(b, 2 * BLK), :] += _dot(p_s[t], dos[_sub_block(b), :], ta=True)
                return carry
            lax.fori_loop(0, n_blocks // ATTN_GROUP, group, 0)

            def scatter(b, carry, i=i, dil=dil):
                rows, _ = _block_rows(b, n_blocks, dil)
                for acc, part, pad in ((dqn, dq_u, 0), (dkn, dk_u, BLK), (dvv, dv_u, BLK)):
                    val = part[_sub_block(b, pad=pad), :]
                    acc[rows, :] = val if i == 0 else acc[rows, :] + val
                return carry
            lax.fori_loop(0, n_blocks, scatter, 0, unroll=8)

        q = q_ref[...]
        rq = lax.rsqrt(jnp.mean(q * q, axis=-1, keepdims=True) + EPS)
        k = k_ref[...]
        rk = lax.rsqrt(jnp.mean(k * k, axis=-1, keepdims=True) + EPS)
        dq_n = dqn[...]
        ggq_ref[...] += jnp.sum(dq_n * (q * rq) * scale, axis=0, keepdims=True)
        dg = dq_n * gq_ref[...] * scale
        qs[...] = (rq * (dg - q * (rq * rq) * jnp.mean(dg * q, axis=-1, keepdims=True))).astype(BF16)
        dk_n = dkn[...]
        ggk_ref[...] += jnp.sum(dk_n * (k * rk), axis=0, keepdims=True)
        dg = dk_n * gk_ref[...]
        dos[...] = (rk * (dg - k * (rk * rk) * jnp.mean(dg * k, axis=-1, keepdims=True))).astype(BF16)
        vs[BLK:, :] = dvv[...].astype(BF16)
        copies = [pltpu.make_async_copy(src, dp_out.at[:, pl.ds(pl.multiple_of(((3 + j) * heads + h) * HEAD_DIM, HEAD_DIM), HEAD_DIM)],
                                        sems.at[j]) for j, src in enumerate((qs, dos, vs.at[pl.ds(BLK, s)]))]
        for cp in copies:
            cp.start()
        for b in range(NUM_BUCKETS):
            tot = jnp.zeros((BLK, 2 * BLK), F32)
            for i in range(n_dil):
                tot = tot + jnp.where(tab_ref[i] == b, dbias_s[i], 0.0)
            db_ref[b:b + 1, :] = jnp.full((1, LANE), jnp.sum(tot), F32)
        for cp in copies:
            cp.wait()
        if comm:
            @pl.when(h == heads - 1)
            def _():
                comm.finish(cin_refs, cout_refs, comm_sems)

    def head_col(off):
        return pl.BlockSpec((s, HEAD_DIM), lambda h: (0, off * heads + h))
    own_col = pl.BlockSpec((s, HEAD_DIM), lambda h: (0, h))
    vec = pl.BlockSpec((1, HEAD_DIM), lambda h: (0, 0))
    big = pltpu.VMEM((s, HEAD_DIM), F32)
    padded32 = pltpu.VMEM((s + BLK, HEAD_DIM), F32)
    padded16 = pltpu.VMEM((s + BLK, HEAD_DIM), BF16)
    res = pl.pallas_call(
        body, name="attn_bwd", grid=(heads,),
        in_specs=[ANY, head_col(3), head_col(4), head_col(5), own_col, own_col,
                  pl.BlockSpec((None, s, 1), lambda h: (h, 0, 0)), vec, vec,
                  pl.BlockSpec(memory_space=pltpu.SMEM),
                  pl.BlockSpec((n_dil, BLK, 2 * BLK), lambda h: (0, 0, 0))] + [ANY] * len(c_ins),
        out_specs=[ANY, vec, vec, pl.BlockSpec((None, NUM_BUCKETS, LANE), lambda h: (h, 0, 0))] + [ANY] * len(c_outs),
        out_shape=[jax.ShapeDtypeStruct((s, d_in), BF16), jax.ShapeDtypeStruct((1, HEAD_DIM), F32),
                   jax.ShapeDtypeStruct((1, HEAD_DIM), F32), jax.ShapeDtypeStruct((heads, NUM_BUCKETS, LANE), F32)]
        + c_outs,
        scratch_shapes=[pltpu.VMEM((s, HEAD_DIM), BF16), padded16, padded16, pltpu.VMEM((s, HEAD_DIM), BF16),
                        pltpu.VMEM((s, 1), F32), pltpu.VMEM((s, 1), F32),
                        big, big, big, big, padded32, padded32,
                        pltpu.VMEM((2 * n_dil, BLK, 2 * BLK), F32), pltpu.VMEM((n_dil, BLK, 2 * BLK), F32),
                        pltpu.VMEM((ATTN_GROUP, BLK, 2 * BLK), F32), pltpu.VMEM((ATTN_GROUP, BLK, 2 * BLK), F32),
                        pltpu.VMEM((ATTN_GROUP, BLK, 2 * BLK), BF16), pltpu.VMEM((ATTN_GROUP, BLK, 2 * BLK), BF16),
                        pltpu.SemaphoreType.DMA((3,))] + c_scratch,
        input_output_aliases={0: 0},
        compiler_params=_params("arbitrary"),
    )(dproj, proj, proj, proj, d_o, yb, lse, g_q, g_k, rel_bias, tables, *c_ins)
    return res[0], res[1], res[2], res[3], res[4:]


def _local_step(xs, ps, tgt, sp, win4, wout4, wgate4, wup4, dist=None):
    s, d = xs.shape
    wa = d // 2
    heads = wa // HEAD_DIM
    d_in = 7 * wa
    c4 = d_in // N_CHIPS
    dple = ps.shape[1]

    tm, tn = _tile(s, 1024), _tile(d, 1024)
    tn_in = _tile(c4, 1024)
    per_in = c4 // tn_in
    tn_up = _tile(d // N_CHIPS, 512)
    per_up = (d // N_CHIPS) // tn_up
    gm, gn = s // tm, d // tn

    hn, rx = _rms_fwd("rms_pre", xs, sp["g_pre"])
    if dist:
        proj, win4 = _mm_in_gather(dist["chip"], hn, win4)
    else:
        proj = _matmul(
            "mm_in", hn, win4, pl.BlockSpec((tm, d), lambda i, j, k: (i, 0)), _shard_spec(d, tn_in, per_in, 2, 1),
            (gm, d_in // tn_in, 1), None, jax.ShapeDtypeStruct((s, d_in), F32),
            pl.BlockSpec((tm, tn_in), lambda i, j, k: (i, j)))
    b_st = sp["b_s"].T
    y, z, vn, mu_v, rs_v, ra = _gmlp_fwd(proj, sp["w_s"], b_st, sp["ln_v_g"], sp["ln_v_b"], sp["g_out_a"], d)
    yb, lse, gathered = _attn_fwd(proj, sp["g_q"], sp["g_k"], sp["rel_bias"], d,
                                  comm=_Gather([wout4, wgate4, wup4]) if dist else None)
    if dist:
        wout4, wgate4, wup4 = gathered
    wout, wgate = wout4.reshape(d, d), wgate4.reshape(d, d)
    y, rb = _gate_fwd(y, yb, proj, sp["g_out_b"], d)

    def residual(acc, extra, outs):
        outs[0][...] = extra[0][...] + acc

    tile = pl.BlockSpec((tm, tn), lambda i, j, k: (i, j))
    tile_up = pl.BlockSpec((tm, tn_up), lambda i, j, k: (i, j))
    h = _matmul(
        "mm_out", y, wout, pl.BlockSpec((tm, d), lambda i, j, k: (i, 0)), pl.BlockSpec((d, tn_up), lambda i, j, k: (0, j)),
        (gm, d // tn_up, 1), None, jax.ShapeDtypeStruct((s, d), F32), tile_up, extras=(xs,), extra_specs=(tile_up,),
        epilogue=residual)
    hp, rh = _rms_fwd("rms_ple", h, sp["g_ple"])
    p16 = ps.astype(BF16)

    def head(acc, extra, outs):
        p_ref, wup_ref, h_ref, t_ref = extra
        dout_ref, dgl_ref, dup_ref, loss_ref = outs
        up = _dot(p_ref[...], wup_ref[...])
        gate = jax.nn.sigmoid(acc)
        err = h_ref[...] + gate * up - t_ref[...]
        sq = _colsum8(err * err)
        part = sq[:, 0:LANE]
        for c in range(1, sq.shape[1] // LANE):
            part = part + sq[:, c * LANE:(c + 1) * LANE]
        loss_ref[...] = part
        dout = err * (1.0 / d)
        dout_ref[...] = dout
        dup_ref[...] = (dout * gate).astype(BF16)
        dgl_ref[...] = (dout * up * gate * (1.0 - gate)).astype(BF16)

    tile_up = pl.BlockSpec((tm, tn_up), lambda i, j, k: (i, j))
    dout, dglin, dup, loss_parts = _matmul(
        "mm_gate_loss", hp, wgate, pl.BlockSpec((tm, d), lambda i, j, k: (i, 0)), pl.BlockSpec((d, tn_up), lambda i, j, k: (0, j)),
        (gm, d // tn_up, 1), None,
        (jax.ShapeDtypeStruct((s, d), F32), jax.ShapeDtypeStruct((s, d), BF16), jax.ShapeDtypeStruct((s, d), BF16),
         jax.ShapeDtypeStruct((gm * 8, (d // tn_up) * LANE), F32)),
        (tile_up, tile_up, tile_up, pl.BlockSpec((8, LANE), lambda i, j, k: (i, j))),
        extras=(p16, wup4, h, tgt),
        extra_specs=(pl.BlockSpec((tm, dple), lambda i, j, k: (i, 0)), _shard_spec(dple, tn_up, per_up, 2, 1),
                     tile_up, tile_up),
        epilogue=head)
    loss = 0.5 * jnp.sum(loss_parts) * (1.0 / d)

    dhp = _matmul(
        "mm_dhp", dglin, wgate, pl.BlockSpec((tm, d), lambda i, j, k: (i, 0)), pl.BlockSpec((tn, d), lambda i, j, k: (j, 0)),
        (gm, gn, 1), None, jax.ShapeDtypeStruct((s, d), F32), tile, tb=True)
    dh, dh16, g_g_ple = _rms_bwd("rms_ple_bwd", dhp, h, rh, sp["g_ple"], dout, True)
    tmw = _tile(d, 1024)
    g_wout = _matmul(
        "mm_gwout", y, dh16, pl.BlockSpec((s, tmw), lambda i, j, k: (0, i)), pl.BlockSpec((s, tn), lambda i, j, k: (0, j)),
        (d // tmw, gn, 1), None, jax.ShapeDtypeStruct((d, d), F32), pl.BlockSpec((tmw, tn), lambda i, j, k: (i, j)), ta=True)
    g_wout4 = g_wout.reshape(N_CHIPS, d // N_CHIPS, d)
    g_wgate = _matmul(
        "mm_gwgate", hp, dglin, pl.BlockSpec((s, tmw), lambda i, j, k: (0, i)), pl.BlockSpec((s, tn), lambda i, j, k: (0, j)),
        (d // tmw, gn, 1), None, jax.ShapeDtypeStruct((d, d), F32), pl.BlockSpec((tmw, tn), lambda i, j, k: (i, j)), ta=True,
        comm=_SwapHalves([g_wout4]) if dist else None)
    if dist:
        g_wgate, swapped_out = g_wgate
    g_wgate4 = g_wgate.reshape(N_CHIPS, d // N_CHIPS, d)
    g_wup = _matmul(
        "mm_gwup", p16, dup, pl.BlockSpec((s, dple), lambda i, j, k: (0, 0)), pl.BlockSpec((s, tn_up), lambda i, j, k: (0, j)),
        (1, d // tn_up, 1), None, jax.ShapeDtypeStruct((N_CHIPS, dple, d // N_CHIPS), F32),
        _shard_spec(dple, tn_up, per_up, 0, 1), ta=True)
    dy = _matmul(
        "mm_dy", dh16, wout, pl.BlockSpec((tm, d), lambda i, j, k: (i, 0)), pl.BlockSpec((tn, d), lambda i, j, k: (j, 0)),
        (gm, gn, 1), None, jax.ShapeDtypeStruct((s, d), F32), tile, tb=True,
        comm=_SwapHalves([g_wgate4, g_wup]) if dist else None)
    early = (g_wout4, g_wgate4, g_wup)
    if dist:
        core = dist["core"]
        dy, swapped_rest = dy
        early_sums = [_chip_sum("chip_sum_" + n, dist["chip"], core, g, o)
                      for n, g, o in zip(LARGE[1:], early, list(swapped_out) + list(swapped_rest))]

    dproj, d_o, g_g_out_b = _gate_bwd(dy, yb, proj, rb, sp["g_out_b"], d)
    dproj, g_g_q, g_g_k, dbias, early_got = _attn_bwd(
        dproj, d_o, yb, lse, proj, sp["g_q"], sp["g_k"], sp["rel_bias"], d,
        comm=_ExchangeChips([s16 for _, s16 in early_sums]) if dist else None)
    dproj, g_w_s, dz_sum, g_ln_g, g_ln_b, g_g_out_a = _gmlp_bwd(
        dproj, dy, proj, z, vn, mu_v, rs_v, ra, sp["w_s"], sp["ln_v_g"], sp["g_out_a"], d)

    if dist:
        tmw_half = _tile(d // 2, 1024)
        n_half = (d // 2) // tmw_half

        def half_of_gwin(name, own, comm):
            def rows(i, j, k, core_ref):
                return (0, (core_ref[0] if own else 1 - core_ref[0]) * n_half + i)
            return _matmul(
                name, hn, dproj, pl.BlockSpec((s, tmw_half), rows),
                pl.BlockSpec((s, tn_in), lambda i, j, k, core_ref: (0, j)),
                (n_half, d_in // tn_in, 1), None, jax.ShapeDtypeStruct((N_CHIPS, d // 2, c4), F32),
                pl.BlockSpec((None, tmw_half, tn_in), lambda i, j, k, core_ref: (j // per_in, i, j % per_in)), ta=True,
                comm=comm, prefetch=(core,))

        g_theirs = half_of_gwin("mm_gwin_theirs", False, None)
        g_mine, (g_from_sibling,) = half_of_gwin("mm_gwin_mine", True, _SwapHalves([g_theirs], whole=True))
        win_sum = _chip_sum("chip_sum_w_in", dist["chip"], core, g_mine, g_from_sibling)
    else:
        g_win = _matmul(
            "mm_gwin", hn, dproj, pl.BlockSpec((s, tmw), lambda i, j, k: (0, i)),
            pl.BlockSpec((s, tn_in), lambda i, j, k: (0, j)),
            (d // tmw, d_in // tn_in, 1), None, jax.ShapeDtypeStruct((N_CHIPS, d, c4), F32),
            _shard_spec(tmw, tn_in, per_in, 0, 1), ta=True)
    tk = c4
    tmh, tnh = _tile(s, 1024), _tile(d, 1024)
    dhn = _matmul(
        "mm_dhn", dproj, win4, pl.BlockSpec((tmh, tk), lambda i, j, k: (i, k)), _shard_spec(tnh, tk, 1, 1, 2),
        (s // tmh, d // tnh, d_in // tk), (tmh, tnh), jax.ShapeDtypeStruct((s, d), F32),
        pl.BlockSpec((tmh, tnh), lambda i, j, k: (i, j)), tb=True,
        comm=_ExchangeChips([win_sum[1]]) if dist else None)
    if dist:
        dhn, win_got = dhn
    grad_x, g_g_pre = _rms_bwd("rms_pre_bwd", dhn, xs, rx, sp["g_pre"], dh, False)

    small = {
        "g_pre": g_g_pre,
        "w_s": g_w_s,
        "b_s": jnp.sum(dz_sum.reshape(CHUNK, heads, HEAD_DIM), axis=-1).T,
        "ln_v_g": g_ln_g, "ln_v_b": g_ln_b,
        "g_q": g_g_q, "g_k": g_g_k,
        "rel_bias": dbias[:, :, 0].T,
        "g_out_a": g_g_out_a, "g_out_b": g_g_out_b,
        "g_ple": g_g_ple,
    }
    if not dist:
        return loss, grad_x, (g_win, *early), small
    sums32 = [win_sum[0]] + [s32 for s32, _ in early_sums]
    return loss, grad_x, (sums32, list(win_got) + list(early_got)), small


def _place():
    x, y, c = lax.axis_index("x"), lax.axis_index("y"), lax.axis_index("c")
    chips = [(1 - x, y), (x, 1 - y), (1 - x, 1 - y)]
    return x, y, c, chips


def _remote(src, dst, send_sems, recv_sems, k, to):
    return pltpu.make_async_remote_copy(src_ref=src, dst_ref=dst, send_sem=send_sems.at[k], recv_sem=recv_sems.at[k],
                                        device_id=to, device_id_type=MESH)


def _cast_bf16(name, chip, w):
    r, c = w.shape
    tm = _tile(r, 256)

    def body(chip_ref, w_ref, o_ref):
        del chip_ref
        o_ref[...] = w_ref[...].astype(BF16)

    return pl.pallas_call(
        body, name=name,
        grid_spec=pltpu.PrefetchScalarGridSpec(
            num_scalar_prefetch=1, grid=(r // tm,),
            in_specs=[pl.BlockSpec((tm, c), lambda i, chip_ref: (i, 0))],
            out_specs=pl.BlockSpec((None, tm, c), lambda i, chip_ref: (chip_ref[0], i, 0))),
        out_shape=jax.ShapeDtypeStruct((N_CHIPS, r, c), BF16), compiler_params=_params("parallel"),
    )(chip, w)


class _Gather:
    in_place = True

    def __init__(self, fulls):
        self.ins = list(fulls)
        self.out_shape = [jax.ShapeDtypeStruct(f.shape, f.dtype) for f in fulls]
        n = len(fulls)
        self.scratch = [pltpu.SemaphoreType.DMA((6 * n,)), pltpu.SemaphoreType.DMA((6 * n,))]

    @staticmethod
    def _sends(outs, sems):
        send_sems, recv_sems = sems
        x, y, c, chips = _place()
        cps = []
        for w, ref in enumerate(outs):
            half = ref.shape[1] // 2
            blk = ref.at[2 * x + y, pl.ds(c * half, half)]
            cps += [_remote(blk, blk, send_sems, recv_sems, 6 * w + q, (*chip, c)) for q, chip in enumerate(chips)]
        return cps

    def start(self, ins, outs, sems):
        for cp in self._sends(outs, sems):
            cp.start()

    def finish(self, ins, outs, sems):
        send_sems, recv_sems = sems
        x, y, c, chips = _place()
        sibling = (x, y, 1 - c)
        forwards = []
        for w, ref in enumerate(outs):
            half = ref.shape[1] // 2
            for q, chip in enumerate(chips):
                blk = ref.at[2 * chip[0] + chip[1], pl.ds(c * half, half)]
                _remote(blk, blk, send_sems, recv_sems, 6 * w + q, sibling).wait_recv()
                fwd = _remote(blk, blk, send_sems, recv_sems, 6 * w + 3 + q, sibling)
                fwd.start()
                forwards.append(fwd)
        for w, ref in enumerate(outs):
            half = ref.shape[1] // 2
            for q, chip in enumerate(chips):
                blk = ref.at[2 * chip[0] + chip[1], pl.ds((1 - c) * half, half)]
                _remote(blk, blk, send_sems, recv_sems, 6 * w + 3 + q, sibling).wait_recv()
        for cp in self._sends(outs, sems) + forwards:
            cp.wait_send()


class _ExchangeChips:
    in_place = False

    def __init__(self, sums16):
        self.ins = list(sums16)
        self.out_shape = [jax.ShapeDtypeStruct(g.shape, g.dtype) for g in sums16]
        n = len(sums16)
        self.scratch = [pltpu.SemaphoreType.DMA((3 * n,)), pltpu.SemaphoreType.DMA((3 * n,))]

    @staticmethod
    def _sends(ins, outs, sems):
        send_sems, recv_sems = sems
        x, y, c, chips = _place()
        return [_remote(ins[w].at[2 * chip[0] + chip[1]], outs[w].at[2 * x + y], send_sems, recv_sems, 3 * w + q, (*chip, c))
                for w in range(len(ins)) for q, chip in enumerate(chips)]

    def start(self, ins, outs, sems):
        for cp in self._sends(ins, outs, sems):
            cp.start()

    def finish(self, ins, outs, sems):
        send_sems, recv_sems = sems
        x, y, c, chips = _place()
        for w in range(len(ins)):
            for q, chip in enumerate(chips):
                blk = outs[w].at[2 * chip[0] + chip[1]]
                _remote(blk, blk, send_sems, recv_sems, 3 * w + q, (*chip, c)).wait_recv()
        for cp in self._sends(ins, outs, sems):
            cp.wait_send()


def _mm_in_gather(chip, hn, win4):
    s, d = hn.shape
    c4 = win4.shape[2]
    tm, tn = _tile(s, 1024), _tile(c4, 1024)
    per = c4 // tn
    nj, gm = N_CHIPS * per, s // tm
    half = d // 2

    def block_of(j, chip_ref):
        o = j // per
        return chip_ref[0] ^ (((o & 1) << 1) | (o >> 1))

    i_late = max(gm - 2, 0)

    def body(chip_ref, hn_ref, w_any, proj_ref, w_ref, wbuf, wsem, send_sems, recv_sems):
        del w_any
        j, i = pl.program_id(0), pl.program_id(1)
        x, y, c, chips = _place()
        sibling = (x, y, 1 - c)

        def region(k, rows_half, t):
            return w_ref.at[k, pl.ds(rows_half * half, half), pl.ds(t * tn, tn)]

        def send_mine(q, t):
            blk = region(2 * x + y, c, t)
            return _remote(blk, blk, send_sems, recv_sems, q * per + t, (*chips[q], c))

        def forward(q, t):
            blk = region(2 * chips[q][0] + chips[q][1], c, t)
            return _remote(blk, blk, send_sems, recv_sems, (3 + q) * per + t, sibling)

        def fetch(jj, mine, slot):
            rows = pl.ds((c if mine else 1 - c) * half, half)
            cols = pl.ds(pl.multiple_of((jj % per) * tn, LANE), tn)
            return pltpu.make_async_copy(w_ref.at[block_of(jj, chip_ref), rows, cols], wbuf.at[slot, rows],
                                         wsem.at[2 * slot + (0 if mine else 1)])

        def foreign_tiles():
            return [(o, t) for o in range(1, N_CHIPS) for t in range(per)]

        relayed = [(0, t) for t in range(1, per, 2)] + [(1, t) for t in range(0, per, 2)]

        def relay(q, t):
            blk = region(2 * chips[q][0] + chips[q][1], c, t)
            return _remote(blk, blk, send_sems, recv_sems, 2 * per + t, (*chips[1 - q], c))

        def landed(q, t):
            blk = region(2 * chips[q][0] + chips[q][1], c, t)
            _remote(blk, blk, send_sems, recv_sems, q * per + t, sibling).wait_recv()
            forward(q, t).start()
            if (q, t) in relayed:
                relay(q, t).start()

        @pl.when((j == 0) & (i == 0))
        def _():
            for t in range(per):
                send_mine(0, t).start()
                send_mine(1, t).start()
            fetch(0, True, 0).start()
            fetch(0, False, 0).start()

        nxt = j + 1

        @pl.when((i == 0) & (nxt < nj))
        def _():
            for o, t in foreign_tiles():
                @pl.when(nxt == o * per + t)
                def _():
                    if o == 1:
                        landed(0, t)
                        landed(1, t)
                    elif o == 3:
                        landed(2, t)

            fetch(nxt, True, nxt % 2).start()

            @pl.when(nxt < per)
            def _():
                fetch(nxt, False, nxt % 2).start()

        @pl.when((i == i_late) & (nxt < nj))
        def _():
            for o, t in foreign_tiles():
                @pl.when(nxt == o * per + t)
                def _():
                    blk = region(2 * chips[o - 1][0] + chips[o - 1][1], 1 - c, t)
                    _remote(blk, blk, send_sems, recv_sems, (3 + o - 1) * per + t, sibling).wait_recv()
                    fetch(nxt, False, nxt % 2).start()

        @pl.when(i == 0)
        def _():
            fetch(j, True, j % 2).wait()
            fetch(j, False, j % 2).wait()

        proj_ref[...] = _dot(hn_ref[...], wbuf[j % 2])

        @pl.when((j == nj - 1) & (i == gm - 1))
        def _():
            for t in range(per):
                send_mine(0, t).wait_send()
                send_mine(1, t).wait_send()
                for q in range(3):
                    forward(q, t).wait_send()
            for q, t in relayed:
                relay(q, t).wait_send()

    return pl.pallas_call(
        body, name="mm_in_gather",
        grid_spec=pltpu.PrefetchScalarGridSpec(
            num_scalar_prefetch=1, grid=(nj, gm),
            in_specs=[pl.BlockSpec((tm, d), lambda j, i, chip_ref: (i, 0)), ANY],
            out_specs=[pl.BlockSpec((tm, tn), lambda j, i, chip_ref: (i, block_of(j, chip_ref) * per + j % per)), ANY],
            scratch_shapes=[pltpu.VMEM((2, d, tn), BF16), pltpu.SemaphoreType.DMA((4,)),
                            pltpu.SemaphoreType.DMA((6 * per,)), pltpu.SemaphoreType.DMA((6 * per,))]),
        out_shape=[jax.ShapeDtypeStruct((s, N_CHIPS * c4), F32), jax.ShapeDtypeStruct(win4.shape, win4.dtype)],
        input_output_aliases={2: 1},
        compiler_params=_params("arbitrary", "arbitrary"),
    )(chip, hn, win4)


class _SwapHalves:
    in_place = False

    def __init__(self, grads, whole=False):
        self.ins = list(grads)
        self.whole = whole
        self.out_shape = [jax.ShapeDtypeStruct((N_CHIPS, g.shape[1] // (1 if whole else 2), g.shape[2]), g.dtype)
                          for g in grads]
        self.scratch = [pltpu.SemaphoreType.DMA((len(grads),)), pltpu.SemaphoreType.DMA((len(grads),))]

    def _copies(self, ins, outs, sems):
        x, y, c, _ = _place()
        cps = []
        for w in range(len(ins)):
            half = ins[w].shape[1] // 2
            src = ins[w] if self.whole else ins[w].at[:, pl.ds((1 - c) * half, half)]
            cps.append(_remote(src, outs[w], sems[0], sems[1], w, (x, y, 1 - c)))
        return cps

    def start(self, ins, outs, sems):
        for cp in self._copies(ins, outs, sems):
            cp.start()

    def finish(self, ins, outs, sems):
        for cp in self._copies(ins, outs, sems):
            cp.wait()


def _chip_sum(name, chip, core, grad, got):
    _, half, c = got.shape
    th = _tile(half, 128)
    n = half // th
    skip = n if grad.shape[1] != half else 0

    def body(chip_ref, core_ref, g_ref, o_ref, s_ref):
        del chip_ref, core_ref
        s_ref[...] = (g_ref[...] + o_ref[...]).astype(s_ref.dtype)

    def specs(first, out_block):
        def block(k, i, chip_ref):
            return (chip_ref[0] + first + k) % N_CHIPS

        return dict(
            in_specs=[pl.BlockSpec((None, th, c), lambda k, i, chip_ref, core_ref: (block(k, i, chip_ref), core_ref[0] * skip + i, 0)),
                      pl.BlockSpec((None, th, c), lambda k, i, chip_ref, core_ref: (block(k, i, chip_ref), i, 0))],
            out_specs=out_block(block))

    own = pl.pallas_call(
        body, name=name + "_own",
        grid_spec=pltpu.PrefetchScalarGridSpec(
            num_scalar_prefetch=2, grid=(1, n),
            **specs(0, lambda block: pl.BlockSpec((th, c), lambda k, i, chip_ref, core_ref: (i, 0)))),
        out_shape=jax.ShapeDtypeStruct((half, c), F32), compiler_params=_params("parallel", "parallel"),
    )(chip, core, grad, got)
    others = pl.pallas_call(
        functools.partial(body), name=name + "_others",
        grid_spec=pltpu.PrefetchScalarGridSpec(
            num_scalar_prefetch=2, grid=(N_CHIPS - 1, n),
            **specs(1, lambda block: pl.BlockSpec(
                (None, th, c), lambda k, i, chip_ref, core_ref: (block(k, i, chip_ref), i, 0)))),
        out_shape=jax.ShapeDtypeStruct((N_CHIPS, half, c), BF16), compiler_params=_params("parallel", "parallel"),
    )(chip, core, grad, got)
    return own, others


def _riding(comm, grid, body, n_pre, n_in, n_out):
    if comm is None:
        return body
    c_in, c_out, c_scr = len(comm.ins), len(comm.out_shape), len(comm.scratch)

    def wrapped(*refs):
        pre, rest = refs[:n_pre], refs[n_pre:]
        ins, rest = rest[:n_in], rest[n_in:]
        cin, rest = rest[:c_in], rest[c_in:]
        outs, rest = rest[:n_out], rest[n_out:]
        cout, rest = rest[:c_out], rest[c_out:]
        own, sems = rest[:len(rest) - c_scr], rest[len(rest) - c_scr:]
        ids = [pl.program_id(ax) for ax in range(len(grid))]
        first, last = ids[0] == 0, ids[0] == grid[0] - 1
        for ax in range(1, len(grid)):
            first, last = first & (ids[ax] == 0), last & (ids[ax] == grid[ax] - 1)

        @pl.when(first)
        def _():
            comm.start(cin, cout, sems)

        body(*pre, *ins, *outs, *own)

        @pl.when(last)
        def _():
            comm.finish(cin, cout, sems)

    return wrapped


def _total(name, chip, core, sum32, got16, comm=None):
    half, c = sum32.shape
    th = _tile(half, 128)
    n = half // th

    def body(chip_ref, core_ref, own_ref, a_ref, b_ref, c_ref, o_ref):
        del chip_ref, core_ref
        o_ref[...] = ((own_ref[...] + a_ref[...].astype(F32)) + b_ref[...].astype(F32)) + c_ref[...].astype(F32)

    def other(step):
        return pl.BlockSpec((None, th, c), lambda i, chip_ref, core_ref: ((chip_ref[0] + step) % N_CHIPS, i, 0))

    c_ins = list(comm.ins) if comm else []
    res = pl.pallas_call(
        _riding(comm, (n,), body, 2, 4, 1), name=name,
        grid_spec=pltpu.PrefetchScalarGridSpec(
            num_scalar_prefetch=2, grid=(n,),
            in_specs=[pl.BlockSpec((th, c), lambda i, chip_ref, core_ref: (i, 0)), other(1), other(2), other(3)]
            + [ANY] * len(c_ins),
            out_specs=[pl.BlockSpec((th, c), lambda i, chip_ref, core_ref: (core_ref[0] * n + i, 0))]
            + [ANY] * len(c_ins),
            scratch_shapes=list(comm.scratch) if comm else []),
        out_shape=[jax.ShapeDtypeStruct((2 * half, c), F32)] + (list(comm.out_shape) if comm else []),
        input_output_aliases={6 + i: 1 + i for i in range(len(c_ins))} if comm else {},
        compiler_params=_params("arbitrary" if comm else "parallel"),
    )(chip, core, sum32, got16, got16, got16, *c_ins)
    return (res[0], res[1:]) if comm else res[0]


class _JoinHalves:
    in_place = True

    def __init__(self, blocks):
        self.ins = list(blocks)
        self.out_shape = [jax.ShapeDtypeStruct(t.shape, t.dtype) for t in blocks]
        self.scratch = [pltpu.SemaphoreType.DMA((len(blocks),)), pltpu.SemaphoreType.DMA((len(blocks),))]

    @staticmethod
    def _copy(ref, w, mine, sems):
        x, y, c, _ = _place()
        half = ref.shape[0] // 2
        rows = ref.at[pl.ds((c if mine else 1 - c) * half, half)]
        return _remote(rows, rows, sems[0], sems[1], w, (x, y, 1 - c))

    def start(self, ins, outs, sems):
        for w, ref in enumerate(outs):
            self._copy(ref, w, True, sems).start()

    def finish(self, ins, outs, sems):
        for w, ref in enumerate(outs):
            self._copy(ref, w, True, sems).wait_send()
            self._copy(ref, w, False, sems).wait_recv()


class _Both:
    in_place = True

    def __init__(self, a, b):
        assert a.in_place and b.in_place
        self.parts = (a, b)
        self.ins = list(a.ins) + list(b.ins)
        self.out_shape = list(a.out_shape) + list(b.out_shape)
        self.scratch = list(a.scratch) + list(b.scratch)

    def _split(self, ins, outs, sems):
        a = self.parts[0]
        n, k = len(a.ins), len(a.scratch)
        return (a, ins[:n], outs[:n], sems[:k]), (self.parts[1], ins[n:], outs[n:], sems[k:])

    def start(self, ins, outs, sems):
        for part, i, o, s in self._split(ins, outs, sems):
            part.start(i, o, s)

    def finish(self, ins, outs, sems):
        for part, i, o, s in self._split(ins, outs, sems):
            part.finish(i, o, s)


def _alone(name, comm):
    n = len(comm.ins)

    def body(*refs):
        comm.start(refs[:n], refs[n:2 * n], refs[2 * n:])
        comm.finish(refs[:n], refs[n:2 * n], refs[2 * n:])

    return pl.pallas_call(
        body, name=name, in_specs=[ANY] * n, out_specs=[ANY] * n, out_shape=comm.out_shape,
        scratch_shapes=comm.scratch, input_output_aliases={i: i for i in range(n)},
    )(*comm.ins)


class _GatherSmall:
    in_place = True

    def __init__(self, full):
        self.ins = [full]
        self.out_shape = [jax.ShapeDtypeStruct(full.shape, full.dtype)]
        self.scratch = [pltpu.SemaphoreType.DMA((7,)), pltpu.SemaphoreType.DMA((7,))]

    @staticmethod
    def _copy(ref, sems, k, block, to):
        m_per = ref.shape[0] // N_DEV
        px, py, pc = block
        rows = ref.at[pl.ds((4 * px + 2 * py + pc) * m_per, m_per), :]
        return _remote(rows, rows, sems[0], sems[1], k, to)

    def _first(self, ref, sems):
        x, y, c, chips = _place()
        me = (x, y, c)
        return [self._copy(ref, sems, 0, me, (x, y, 1 - c))] + \
               [self._copy(ref, sems, 1 + j, me, (*chip, c)) for j, chip in enumerate(chips)]

    def start(self, ins, outs, sems):
        for cp in self._first(outs[0], sems):
            cp.start()

    def finish(self, ins, outs, sems):
        ref = outs[0]
        x, y, c, chips = _place()
        me, sibling = (x, y, c), (x, y, 1 - c)
        passed = [self._copy(ref, sems, 4 + j, (*chip, c), sibling) for j, chip in enumerate(chips)]
        for j, chip in enumerate(chips):
            self._copy(ref, sems, 1 + j, (*chip, c), me).wait_recv()
            passed[j].start()
        self._copy(ref, sems, 0, sibling, me).wait_recv()
        for j, chip in enumerate(chips):
            self._copy(ref, sems, 4 + j, (*chip, 1 - c), me).wait_recv()
        for cp in self._first(ref, sems) + passed:
            cp.wait_send()


def _adamw_math(w, g, m, v):
    m = ADAM_B1 * m + (1.0 - ADAM_B1) * g
    v = ADAM_B2 * v + (1.0 - ADAM_B2) * (g * g)
    m_hat = m / (1.0 - ADAM_B1 ** ADAM_STEP)
    v_hat = v / (1.0 - ADAM_B2 ** ADAM_STEP)
    delta = -ADAM_LR * (m_hat / (jnp.sqrt(v_hat) + ADAM_EPS) + ADAM_WD * w)
    return delta, m, v


def _adamw(name, w, g, m, v):
    r, c = w.shape
    tm = _tile(r, 128)

    def body(w_ref, g_ref, m_ref, v_ref, g_out, d_out, m_out, v_out):
        g = g_ref[...]
        g_out[...] = g
        d_out[...], m_out[...], v_out[...] = _adamw_math(w_ref[...], g, m_ref[...], v_ref[...])

    spec = pl.BlockSpec((tm, c), lambda i: (i, 0))
    return pl.pallas_call(
        body, name=name, grid=(r // tm,), in_specs=[spec] * 4, out_specs=[spec] * 4,
        out_shape=[jax.ShapeDtypeStruct((r, c), F32)] * 4, compiler_params=_params("parallel"),
    )(w, g, m, v)


def _adamw_small(gathered, w, m, v):
    rows = w.shape[0]

    def body(all_ref, w_ref, m_ref, v_ref, g_out, d_out, m_out, v_out):
        g = all_ref[0:rows, :]
        for dev in range(1, N_DEV):
            g = g + all_ref[dev * rows:(dev + 1) * rows, :]
        g_out[...] = g
        d_out[...], m_out[...], v_out[...] = _adamw_math(w_ref[...], g, m_ref[...], v_ref[...])

    return pl.pallas_call(
        body, name="adamw_small", out_shape=[jax.ShapeDtypeStruct(w.shape, F32)] * 4, compiler_params=_params(),
    )(gathered, w, m, v)


SMALL = ("g_pre", "w_s", "b_s", "ln_v_g", "ln_v_b", "g_q", "g_k", "rel_bias", "g_out_a", "g_out_b", "g_ple")
LARGE = ("w_in", "w_out", "w_ple_gate", "w_ple_up")
WEIGHTS = ("g_pre", "w_in", "w_s", "b_s", "ln_v_g", "ln_v_b", "g_q", "g_k", "rel_bias", "g_out_a", "g_out_b", "w_out",
           "g_ple", "w_ple_gate", "w_ple_up")


def _pack(parts):
    flat = jnp.concatenate([parts[n].reshape(-1).astype(F32) for n in SMALL])
    rows = -(-flat.shape[0] // (8 * LANE)) * 8
    return jnp.pad(flat, (0, rows * LANE - flat.shape[0])).reshape(rows, LANE)


def _unpack(pack, like):
    flat = pack.reshape(-1)
    out, at = {}, 0
    for n in SMALL:
        size = math.prod(like[n].shape)
        out[n] = flat[at:at + size].reshape(like[n].shape)
        at += size
    return out


def kernel(x, p, g_pre, w_in, w_s, b_s, ln_v_g, ln_v_b, g_q, g_k, rel_bias, g_out_a, g_out_b, w_out, g_ple, w_ple_gate, w_ple_up, loss_target, m_g_pre, m_w_in, m_w_s, m_b_s, m_ln_v_g, m_ln_v_b, m_g_q, m_g_k, m_rel_bias, m_g_out_a, m_g_out_b, m_w_out, m_g_ple, m_w_ple_gate, m_w_ple_up, v_g_pre, v_w_in, v_w_s, v_b_s, v_ln_v_g, v_ln_v_b, v_g_q, v_g_k, v_rel_bias, v_g_out_a, v_g_out_b, v_w_out, v_g_ple, v_w_ple_gate, v_w_ple_up):
    given = dict(locals())
    weights = {n: given[n] for n in WEIGHTS}
    mom_m = {n: given["m_" + n] for n in WEIGHTS}
    mom_v = {n: given["v_" + n] for n in WEIGHTS}
    xs, ps, tgt = x[0], p[0, 0], loss_target[0]
    d = xs.shape[1]

    core = lax.axis_index("c").astype(jnp.int32).reshape(1)
    chip = (2 * lax.axis_index("x") + lax.axis_index("y")).astype(jnp.int32).reshape(1)

    win4, wout4, wgate4, wup4 = [_cast_bf16("cast_" + n, chip, weights[n][0]) for n in LARGE]

    sp = {
        "g_pre": g_pre, "w_s": w_s[0], "b_s": b_s[0], "ln_v_g": ln_v_g, "ln_v_b": ln_v_b, "g_q": g_q, "g_k": g_k,
        "rel_bias": rel_bias, "g_out_a": g_out_a, "g_out_b": g_out_b, "g_ple": g_ple,
    }
    loss_local, grad_x, (sums32, got16), small = _local_step(xs, ps, tgt, sp, win4, wout4, wgate4, wup4,
                                                             dist={"chip": chip, "core": core})
    loss = lax.psum(loss_local, MESH_AXES)

    own = dict(zip(LARGE, sums32))
    got = dict(zip(LARGE, got16))
    halves = [_total("total_" + n, chip, core, own[n], got[n]) for n in LARGE[1:]]
    my_pack = _pack(small)
    device = 2 * chip[0] + core[0]
    everyone = lax.dynamic_update_slice(jnp.zeros((N_DEV * my_pack.shape[0], LANE), F32), my_pack,
                                        (device * my_pack.shape[0], 0))
    half_in, (*joined, gathered) = _total("total_w_in", chip, core, own["w_in"], got["w_in"],
                                          comm=_Both(_JoinHalves(halves), _GatherSmall(everyone)))
    grads = dict(zip(LARGE[1:], joined))
    grads["w_in"], = _alone("join_w_in", _JoinHalves([half_in]))

    out_g, out_d, out_m, out_v = {}, {}, {}, {}
    for n in LARGE:
        results = _adamw("adamw_" + n, weights[n][0], grads[n], mom_m[n][0], mom_v[n][0])
        out_g[n], out_d[n], out_m[n], out_v[n] = [r[None] for r in results]

    pg, pd, pm, pv = _adamw_small(gathered, _pack(weights), _pack(mom_m), _pack(mom_v))
    for packed, out in ((pg, out_g), (pd, out_d), (pm, out_m), (pv, out_v)):
        out.update(_unpack(packed, weights))

    return (loss, grad_x[None], *[out_g[n] for n in WEIGHTS], *[out_d[n] for n in WEIGHTS],
            *[out_m[n] for n in WEIGHTS], *[out_v[n] for n in WEIGHTS])
```

```python
import functools
import math

import numpy as np

import jax
import jax.numpy as jnp
from jax import lax
from jax.experimental import pallas as pl
from jax.experimental.pallas import tpu as pltpu

F32 = jnp.float32
BF16 = jnp.bfloat16

HEAD_DIM = 128
CHUNK = 128
BLK = 128
DILATED = ((128, 1), (512, 4), (2048, 16))
NUM_BUCKETS = 32
MAX_DISTANCE = 2048
ATTN_GROUP = 4
EPS = 1e-6
NEG_INF = -1e30
N_CHIPS = 4
N_DEV = 8
MESH_AXES = ("x", "y", "c")

ADAM_LR = 0.001
ADAM_B1 = 0.9
ADAM_B2 = 0.999
ADAM_EPS = 1e-08
ADAM_WD = 0.01
ADAM_STEP = 10

V7X_VMEM_LIMIT = 56 * 1024 * 1024
LANE = 128
MESH = pl.DeviceIdType.MESH
ANY = pl.BlockSpec(memory_space=pl.ANY)


def _params(*sem):
    return pltpu.CompilerParams(dimension_semantics=sem or None, vmem_limit_bytes=V7X_VMEM_LIMIT)


def _tile(n, target):
    if n <= target:
        return n
    t = (target // LANE) * LANE
    while t > LANE and n % t:
        t -= LANE
    assert n % t == 0, (n, target)
    return t


def _gelu(x):
    return 0.5 * x * (1.0 + lax.erf(x * (1.0 / math.sqrt(2.0))))


def _gelu_grad(x):
    return 0.5 * (1.0 + lax.erf(x * (1.0 / math.sqrt(2.0)))) + x * jnp.exp(-0.5 * x * x) * (1.0 / math.sqrt(2.0 * math.pi))


def _silu_and_grad(x):
    s = jax.nn.sigmoid(x)
    return x * s, s * (1.0 + x * (1.0 - s))


def _dot(a, b, ta=False, tb=False):
    return lax.dot_general(a, b, (((0 if ta else 1,), (1 if tb else 0,)), ((), ())), preferred_element_type=F32)


def _colsum8(v):
    return jnp.sum(v.reshape(v.shape[0] // 8, 8, v.shape[1]), axis=0)


def _matmul(name, a, b, a_spec, b_spec, grid, acc_shape, out_shape, out_specs, ta=False, tb=False,
            extras=(), extra_specs=(), epilogue=None, comm=None, prefetch=()):
    nk = grid[2]
    n_pre = len(prefetch)
    n_extra = len(extras)
    single = not isinstance(out_shape, (tuple, list))
    outs_shape = (out_shape,) if single else tuple(out_shape)
    outs_specs = (out_specs,) if single else tuple(out_specs)
    n_out = len(outs_shape)
    c_ins = list(comm.ins) if comm else []
    c_outs = list(comm.out_shape) if comm else []
    c_scratch = list(comm.scratch) if comm else []
    n_cin, n_cout = len(c_ins), len(c_outs)

    def finish(acc, extra_refs, out_refs):
        if epilogue is None:
            out_refs[0][...] = acc.astype(out_refs[0].dtype)
        else:
            epilogue(acc, extra_refs, out_refs)

    def body(*refs):
        a_ref, b_ref, *rest = refs[n_pre:]
        extra_refs = rest[:n_extra]
        cin_refs = rest[n_extra:n_extra + n_cin]
        out_refs = rest[n_extra + n_cin:n_extra + n_cin + n_out]
        cout_refs = rest[n_extra + n_cin + n_out:n_extra + n_cin + n_out + n_cout]
        scratch_refs = rest[n_extra + n_cin + n_out + n_cout:]
        ids = [pl.program_id(ax) for ax in range(3)]
        if comm:
            sems = scratch_refs[len(scratch_refs) - len(c_scratch):]

            @pl.when((ids[0] == 0) & (ids[1] == 0) & (ids[2] == 0))
            def _():
                comm.start(cin_refs, cout_refs, sems)

        if nk == 1:
            finish(_dot(a_ref[...], b_ref[...], ta, tb), extra_refs, out_refs)
        else:
            acc_ref = scratch_refs[0]

            @pl.when(ids[2] == 0)
            def _():
                acc_ref[...] = jnp.zeros_like(acc_ref)

            acc_ref[...] += _dot(a_ref[...], b_ref[...], ta, tb)

            @pl.when(ids[2] == nk - 1)
            def _():
                finish(acc_ref[...], extra_refs, out_refs)

        if comm:
            @pl.when((ids[0] == grid[0] - 1) & (ids[1] == grid[1] - 1) & (ids[2] == nk - 1))
            def _():
                comm.finish(cin_refs, cout_refs, sems)

    scratch = ([] if nk == 1 else [pltpu.VMEM(acc_shape, F32)]) + c_scratch
    aliases = {n_pre + 2 + n_extra + i: n_out + i for i in range(n_cin)} if (comm and comm.in_place) else {}
    res = pl.pallas_call(
        body, name=name,
        grid_spec=pltpu.PrefetchScalarGridSpec(
            num_scalar_prefetch=n_pre, grid=grid,
            in_specs=[a_spec, b_spec, *extra_specs] + [ANY] * n_cin,
            out_specs=list(outs_specs) + [ANY] * n_cout, scratch_shapes=scratch),
        out_shape=list(outs_shape) + c_outs,
        input_output_aliases=aliases,
        compiler_params=_params(*(("arbitrary",) * 3 if comm else ("parallel", "parallel", "arbitrary"))),
    )(*prefetch, a, b, *extras, *c_ins)
    if comm:
        main = res[:n_out]
        return (main[0] if single else main), res[n_out:]
    return res[0] if single else res


def _shard_spec(rows, cols, per_shard, row_axis, col_axis):
    def index(i, j, k):
        g = (i, j, k)
        return (g[col_axis] // per_shard, g[row_axis], g[col_axis] % per_shard)
    return pl.BlockSpec((None, rows, cols), index)


def _rms_fwd(name, x, g):
    s, d = x.shape
    tm = _tile(s, 256)

    def body(x_ref, g_ref, y_ref, r_ref):
        xf = x_ref[...]
        r = lax.rsqrt(jnp.mean(xf * xf, axis=-1, keepdims=True) + EPS)
        y_ref[...] = (xf * r * g_ref[...]).astype(BF16)
        r_ref[...] = r

    return pl.pallas_call(
        body, name=name, grid=(s // tm,),
        in_specs=[pl.BlockSpec((tm, d), lambda i: (i, 0)), pl.BlockSpec((1, d), lambda i: (0, 0))],
        out_specs=(pl.BlockSpec((tm, d), lambda i: (i, 0)), pl.BlockSpec((tm, 1), lambda i: (i, 0))),
        out_shape=(jax.ShapeDtypeStruct((s, d), BF16), jax.ShapeDtypeStruct((s, 1), F32)),
        compiler_params=_params("parallel"),
    )(x, g)


def _rms_bwd(name, dy, x, r, g, skip, with_bf16):
    s, d = x.shape
    tm = _tile(s, 256)
    n = s // tm

    def body(dy_ref, x_ref, r_ref, g_ref, skip_ref, *outs):
        dx_ref = outs[0]
        dg_ref = outs[-2]
        acc_ref = outs[-1]
        i = pl.program_id(0)
        dyv, xv, rv = dy_ref[...], x_ref[...], r_ref[...]

        @pl.when(i == 0)
        def _():
            acc_ref[...] = jnp.zeros_like(acc_ref)

        acc_ref[...] += _colsum8(dyv * xv * rv)
        dg = dyv * g_ref[...]
        dx = skip_ref[...] + rv * (dg - xv * (rv * rv) * jnp.mean(dg * xv, axis=-1, keepdims=True))
        dx_ref[...] = dx
        if with_bf16:
            outs[1][...] = dx.astype(BF16)

        @pl.when(i == n - 1)
        def _():
            dg_ref[...] = jnp.sum(acc_ref[...], axis=0, keepdims=True)

    row = pl.BlockSpec((tm, d), lambda i: (i, 0))
    vec = pl.BlockSpec((1, d), lambda i: (0, 0))
    out_specs = [row] + ([row] if with_bf16 else []) + [vec]
    out_shape = [jax.ShapeDtypeStruct((s, d), F32)] + ([jax.ShapeDtypeStruct((s, d), BF16)] if with_bf16 else []) \
        + [jax.ShapeDtypeStruct((1, d), F32)]
    return pl.pallas_call(
        body, name=name, grid=(n,),
        in_specs=[row, row, pl.BlockSpec((tm, 1), lambda i: (i, 0)), vec, row],
        out_specs=out_specs, out_shape=out_shape, scratch_shapes=[pltpu.VMEM((8, d), F32)],
        compiler_params=_params("arbitrary"),
    )(dy, x, r, g, skip)


def _causal(w):
    t = lax.broadcasted_iota(jnp.int32, w.shape, 0)
    s_ = lax.broadcasted_iota(jnp.int32, w.shape, 1)
    return jnp.where(t >= s_, w, 0.0)


def _gmlp_fwd(proj, w_s, b_st, ln_g, ln_b, g_out, d_model):
    s = proj.shape[0]
    wa = d_model // 2
    groups = wa // HEAD_DIM
    tm = _tile(s, 256)
    n_chunks = tm // CHUNK

    def body(au_ref, av_ref, az_ref, ws_ref, bst_ref, lng_ref, lnb_ref, go_ref,
             y_ref, z_ref, vn_ref, mu_ref, rs_ref, ra_ref):
        gv = _gelu(av_ref[...])
        mu = jnp.mean(gv, axis=-1, keepdims=True)
        xc = gv - mu
        rs = lax.rsqrt(jnp.mean(xc * xc, axis=-1, keepdims=True) + EPS)
        vn = (xc * rs * lng_ref[...] + lnb_ref[...]).astype(BF16)
        vn_ref[...] = vn
        mu_ref[...] = mu
        rs_ref[...] = rs
        for g in range(groups):
            wm = _causal(ws_ref[g]).astype(BF16)
            cols = slice(g * HEAD_DIM, (g + 1) * HEAD_DIM)
            for ch in range(n_chunks):
                rows = slice(ch * CHUNK, (ch + 1) * CHUNK)
                z_ref[rows, cols] = _dot(wm, vn_ref[rows, cols]) + bst_ref[:, g:g + 1]
        ya = _gelu(au_ref[...]) * z_ref[...]
        ra = lax.rsqrt(jnp.mean(ya * ya, axis=-1, keepdims=True) + EPS)
        ra_ref[...] = ra
        sz, _ = _silu_and_grad(az_ref[...])
        y_ref[...] = (ya * ra * go_ref[...] * sz).astype(BF16)

    def col(j):
        return pl.BlockSpec((tm, wa), lambda i: (i, j))
    vec = pl.BlockSpec((1, wa), lambda i: (0, 0))
    stat = pl.BlockSpec((tm, 1), lambda i: (i, 0))
    return pl.pallas_call(
        body, name="gmlp_fwd", grid=(s // tm,),
        in_specs=[col(0), col(1), col(2),
                  pl.BlockSpec((groups, CHUNK, CHUNK), lambda i: (0, 0, 0)),
                  pl.BlockSpec((CHUNK, groups), lambda i: (0, 0)), vec, vec, vec],
        out_specs=(col(0), col(0), col(0), stat, stat, stat),
        out_shape=(jax.ShapeDtypeStruct((s, d_model), BF16), jax.ShapeDtypeStruct((s, wa), F32),
                   jax.ShapeDtypeStruct((s, wa), BF16), jax.ShapeDtypeStruct((s, 1), F32),
                   jax.ShapeDtypeStruct((s, 1), F32), jax.ShapeDtypeStruct((s, 1), F32)),
        compiler_params=_params("parallel"),
    )(proj, proj, proj, w_s, b_st, ln_g, ln_b, g_out)


def _gmlp_bwd(dproj, dy, proj, z, vn, mu, rs, ra, w_s, ln_g, g_out, d_model):
    s = proj.shape[0]
    wa = d_model // 2
    groups = wa // HEAD_DIM
    tm = _tile(s, 256)
    n_chunks = tm // CHUNK
    n = s // tm

    def causal_stack(w):
        t = lax.broadcasted_iota(jnp.int32, w.shape, 1)
        s_ = lax.broadcasted_iota(jnp.int32, w.shape, 2)
        return jnp.where(t >= s_, w, 0.0)

    def body(dproj_in, dy_ref, au_ref, av_ref, az_ref, z_ref, vn_ref, mu_ref, rs_ref, ra_ref, ws_ref, lng_ref, go_ref,
             dp_ref, gws_ref, dzs_ref, glg_ref, glb_ref, ggo_ref,
             dz_s, dvn_s, acc_lg, acc_lb, acc_go):
        del dproj_in
        i = pl.program_id(0)

        @pl.when(i == 0)
        def _():
            gws_ref[...] = jnp.zeros_like(gws_ref)
            dzs_ref[...] = jnp.zeros_like(dzs_ref)
            acc_lg[...] = jnp.zeros_like(acc_lg)
            acc_lb[...] = jnp.zeros_like(acc_lb)
            acc_go[...] = jnp.zeros_like(acc_go)

        au, az, zv, rav = au_ref[...], az_ref[...], z_ref[...], ra_ref[...]
        u = _gelu(au)
        ya = u * zv
        sz, dsz = _silu_and_grad(az)
        dyv = dy_ref[...]
        dp_ref[:, 2 * wa:3 * wa] = (dyv * (ya * rav * go_ref[...]) * dsz).astype(BF16)
        dn = dyv * sz
        acc_go[...] += _colsum8(dn * ya * rav)
        dyg = dn * go_ref[...]
        dya = rav * (dyg - ya * (rav * rav) * jnp.mean(dyg * ya, axis=-1, keepdims=True))
        dp_ref[:, 0:wa] = (dya * zv * _gelu_grad(au)).astype(BF16)
        dz_s[...] = dya * u
        for ch in range(n_chunks):
            dzs_ref[...] += dz_s[ch * CHUNK:(ch + 1) * CHUNK, :]
        for g in range(groups):
            wm = _causal(ws_ref[g]).astype(BF16)
            cols = slice(g * HEAD_DIM, (g + 1) * HEAD_DIM)
            for ch in range(n_chunks):
                rows = slice(ch * CHUNK, (ch + 1) * CHUNK)
                dzb = dz_s[rows, cols].astype(BF16)
                gws_ref[g] += _dot(dzb, vn_ref[rows, cols], tb=True)
                dvn_s[rows, cols] = _dot(wm, dzb, ta=True)
        av = av_ref[...]
        xh = (_gelu(av) - mu_ref[...]) * rs_ref[...]
        dvn = dvn_s[...]
        acc_lb[...] += _colsum8(dvn)
        acc_lg[...] += _colsum8(dvn * xh)
        dxh = dvn * lng_ref[...]
        dgv = rs_ref[...] * (dxh - jnp.mean(dxh, axis=-1, keepdims=True) - xh * jnp.mean(dxh * xh, axis=-1, keepdims=True))
        dp_ref[:, wa:2 * wa] = (dgv * _gelu_grad(av)).astype(BF16)

        @pl.when(i == n - 1)
        def _():
            gws_ref[...] = causal_stack(gws_ref[...])
            glg_ref[...] = jnp.sum(acc_lg[...], axis=0, keepdims=True)
            glb_ref[...] = jnp.sum(acc_lb[...], axis=0, keepdims=True)
            ggo_ref[...] = jnp.sum(acc_go[...], axis=0, keepdims=True)

    def col(j):
        return pl.BlockSpec((tm, wa), lambda i: (i, j))
    vec = pl.BlockSpec((1, wa), lambda i: (0, 0))
    stat = pl.BlockSpec((tm, 1), lambda i: (i, 0))
    ws_spec = pl.BlockSpec((groups, CHUNK, CHUNK), lambda i: (0, 0, 0))
    d_in = dproj.shape[1]
    return pl.pallas_call(
        body, name="gmlp_bwd", grid=(n,),
        in_specs=[ANY, col(0), col(0), col(1), col(2), col(0), col(0), stat, stat, stat, ws_spec, vec, vec],
        out_specs=(pl.BlockSpec((tm, 3 * wa), lambda i: (i, 0)), ws_spec,
                   pl.BlockSpec((CHUNK, wa), lambda i: (0, 0)), vec, vec, vec),
        out_shape=(jax.ShapeDtypeStruct((s, d_in), BF16),
                   jax.ShapeDtypeStruct((groups, CHUNK, CHUNK), F32), jax.ShapeDtypeStruct((CHUNK, wa), F32))
        + (jax.ShapeDtypeStruct((1, wa), F32),) * 3,
        scratch_shapes=[pltpu.VMEM((tm, wa), F32), pltpu.VMEM((tm, wa), F32)] + [pltpu.VMEM((8, wa), F32)] * 3,
        input_output_aliases={0: 0},
        compiler_params=_params("arbitrary"),
    )(dproj, dy, proj, proj, proj, z, vn, mu, rs, ra, w_s, ln_g, g_out)


def _gate_fwd(y_in, yb, proj, g_out, d_model):
    s = proj.shape[0]
    wa = d_model // 2
    tm = _tile(s, 256)

    def body(y_any, yb_ref, bz_ref, go_ref, y_ref, rb_ref):
        del y_any
        ybv = yb_ref[...]
        rb = lax.rsqrt(jnp.mean(ybv * ybv, axis=-1, keepdims=True) + EPS)
        rb_ref[...] = rb
        sz, _ = _silu_and_grad(bz_ref[...])
        y_ref[...] = (ybv * rb * go_ref[...] * sz).astype(BF16)

    return pl.pallas_call(
        body, name="gate_b_fwd", grid=(s // tm,),
        in_specs=[ANY, pl.BlockSpec((tm, wa), lambda i: (i, 0)), pl.BlockSpec((tm, wa), lambda i: (i, 6)),
                  pl.BlockSpec((1, wa), lambda i: (0, 0))],
        out_specs=(pl.BlockSpec((tm, wa), lambda i: (i, 1)), pl.BlockSpec((tm, 1), lambda i: (i, 0))),
        out_shape=(jax.ShapeDtypeStruct((s, d_model), BF16), jax.ShapeDtypeStruct((s, 1), F32)),
        input_output_aliases={0: 0},
        compiler_params=_params("parallel"),
    )(y_in, yb, proj, g_out)


def _gate_bwd(dy, yb, lse, proj, rb, g_out, d_model):
    s, d_in = proj.shape
    wa = d_model // 2
    heads = wa // HEAD_DIM
    tm = _tile(s, 256)
    n = s // tm

    def body(dy_ref, yb_ref, lse_ref, bz_ref, rb_ref, go_ref, dp_ref, do_ref, st_ref, ggo_ref, acc):
        i = pl.program_id(0)

        @pl.when(i == 0)
        def _():
            acc[...] = jnp.zeros_like(acc)

        dyv, ybv, rbv = dy_ref[...], yb_ref[...], rb_ref[...]
        sz, dsz = _silu_and_grad(bz_ref[...])
        dp_ref[...] = (dyv * (ybv * rbv * go_ref[...]) * dsz).astype(BF16)
        dn = dyv * sz
        acc[...] += _colsum8(dn * ybv * rbv)
        dyg = dn * go_ref[...]
        do = rbv * (dyg - ybv * (rbv * rbv) * jnp.mean(dyg * ybv, axis=-1, keepdims=True))
        do_ref[...] = do
        prod = do * ybv
        first = lax.broadcasted_iota(jnp.int32, (tm, 2), 1) == 0
        for h in range(heads):
            delta = jnp.sum(prod[:, h * HEAD_DIM:(h + 1) * HEAD_DIM], axis=-1, keepdims=True)
            st_ref[h] = jnp.where(first, lse_ref[h], delta)

        @pl.when(i == n - 1)
        def _():
            ggo_ref[...] = jnp.sum(acc[...], axis=0, keepdims=True)

    vec = pl.BlockSpec((1, wa), lambda i: (0, 0))
    return pl.pallas_call(
        body, name="gate_b_bwd", grid=(n,),
        in_specs=[pl.BlockSpec((tm, wa), lambda i: (i, 1)), pl.BlockSpec((tm, wa), lambda i: (i, 0)),
                  pl.BlockSpec((heads, tm, 1), lambda i: (0, i, 0)),
                  pl.BlockSpec((tm, wa), lambda i: (i, 6)), pl.BlockSpec((tm, 1), lambda i: (i, 0)), vec],
        out_specs=(pl.BlockSpec((tm, wa), lambda i: (i, 6)), pl.BlockSpec((tm, wa), lambda i: (i, 0)),
                   pl.BlockSpec((heads, tm, 2), lambda i: (0, i, 0)), vec),
        out_shape=(jax.ShapeDtypeStruct((s, d_in), BF16), jax.ShapeDtypeStruct((s, wa), F32),
                   jax.ShapeDtypeStruct((heads, s, 2), F32), jax.ShapeDtypeStruct((1, wa), F32)),
        scratch_shapes=[pltpu.VMEM((8, wa), F32)],
        compiler_params=_params("arbitrary"),
    )(dy, yb, lse, proj, rb, g_out)


def _bucket_tables():
    qi = BLK + np.arange(BLK)
    kj = np.arange(2 * BLK)
    delta = qi[:, None] - kj[None, :]
    max_exact = NUM_BUCKETS // 2
    tabs = []
    for window, dil in DILATED:
        band = (delta >= 0) & (delta <= window // dil)
        dist = np.clip(delta, 0, None) * dil
        d = np.maximum(dist, 1).astype(np.float32)
        large = max_exact + (np.log(d / np.float32(max_exact)) / np.float32(math.log(MAX_DISTANCE / max_exact))
                             * np.float32(NUM_BUCKETS - max_exact)).astype(np.int32)
        large = np.minimum(large, NUM_BUCKETS - 1)
        tabs.append(np.where(band, np.where(dist < max_exact, dist, large), -1).astype(np.int32))
    return np.stack(tabs)


def _bias_tiles(tab_ref, rb_ref, h, bias_s):
    col = lax.broadcasted_iota(jnp.int32, (BLK, 2 * BLK), 1)
    for i in range(len(DILATED)):
        t = tab_ref[i]
        bias = jnp.zeros(t.shape, F32)
        for b in range(NUM_BUCKETS):
            bias = jnp.where(t == b, rb_ref[b, h], bias)
        bias = jnp.where(t >= 0, bias, NEG_INF)
        bias_s[2 * i] = jnp.where(col >= BLK, bias, NEG_INF)
        bias_s[2 * i + 1] = bias


def _block_rows(b, n_blocks, dil):
    nb = n_blocks // dil
    r = b // nb
    n = b % nb
    start = r + dil * BLK * n
    if dil == 1:
        return pl.ds(pl.multiple_of(start, BLK), BLK), n
    return pl.ds(start, BLK, stride=dil), n


def _sub_block(b, rows=BLK, pad=0):
    return pl.ds(pl.multiple_of(b * BLK + pad, BLK), rows)


def _rms_rows(x, gain):
    return x * lax.rsqrt(jnp.mean(x * x, axis=-1, keepdims=True) + EPS) * gain


def _attn_fwd(proj, g_q, g_k, rel_bias, d_model, comm=None):
    s = proj.shape[0]
    heads = d_model // 2 // HEAD_DIM
    n_blocks = s // BLK
    scale = HEAD_DIM ** -0.5
    tables = jnp.asarray(_bucket_tables())
    c_ins = list(comm.ins) if comm else []
    c_outs = list(comm.out_shape) if comm else []
    c_scratch = list(comm.scratch) if comm else []

    def body(q_ref, k_ref, v_ref, gq_ref, gk_ref, rb_ref, tab_ref, *rest):
        cin_refs, rest = rest[:len(c_ins)], rest[len(c_ins):]
        yb_ref, lse_ref = rest[:2]
        cout_refs, rest = rest[2:2 + len(c_outs)], rest[2 + len(c_outs):]
        qs, ks, vs, m_s, l_s, acc_s, bias_s, sc_s, p_s = rest[:9]
        sems = rest[9:]
        h = pl.program_id(0)
        if comm:
            @pl.when(h == 0)
            def _():
                comm.start(cin_refs, cout_refs, sems)

        _bias_tiles(tab_ref, rb_ref, h, bias_s)
        gq = gq_ref[...] * scale
        gk = gk_ref[...]
        ks[0:BLK, :] = jnp.zeros((BLK, HEAD_DIM), BF16)
        vs[0:BLK, :] = jnp.zeros((BLK, HEAD_DIM), BF16)

        for i, (_, dil) in enumerate(DILATED):
            def prepare(b, carry, dil=dil):
                rows, _ = _block_rows(b, n_blocks, dil)
                qs[_sub_block(b), :] = _rms_rows(q_ref[rows, :], gq).astype(BF16)
                ks[_sub_block(b, pad=BLK), :] = _rms_rows(k_ref[rows, :], gk).astype(BF16)
                vs[_sub_block(b, pad=BLK), :] = v_ref[rows, :].astype(BF16)
                return carry
            lax.fori_loop(0, n_blocks, prepare, 0, unroll=8)

            def group(g, carry, i=i, dil=dil):
                blocks = [g * ATTN_GROUP + t for t in range(ATTN_GROUP)]
                for t, b in enumerate(blocks):
                    sc_s[t] = _dot(qs[_sub_block(b), :], ks[_sub_block(b, 2 * BLK), :], tb=True)
                for t, b in enumerate(blocks):
                    rows, n = _block_rows(b, n_blocks, dil)
                    sc = sc_s[t] + bias_s[2 * i + jnp.minimum(n, 1)]
                    m_i = jnp.max(sc, axis=-1, keepdims=True)
                    pr = jnp.exp(sc - m_i)
                    l_i = jnp.sum(pr, axis=-1, keepdims=True)
                    p_s[t] = pr.astype(BF16)
                    m_s[i, rows, :] = m_i
                    l_s[i, rows, :] = l_i
                for t, b in enumerate(blocks):
                    rows, _ = _block_rows(b, n_blocks, dil)
                    acc_s[i, rows, :] = _dot(p_s[t], vs[_sub_block(b, 2 * BLK), :])
                return carry
            lax.fori_loop(0, n_blocks // ATTN_GROUP, group, 0)

        m = jnp.maximum(jnp.maximum(m_s[0], m_s[1]), m_s[2])
        l = jnp.zeros_like(m)
        acc = jnp.zeros((s, HEAD_DIM), F32)
        for i in range(len(DILATED)):
            w = jnp.exp(m_s[i] - m)
            l = l + l_s[i] * w
            acc = acc + acc_s[i] * w
        yb_ref[...] = acc / l
        lse_ref[...] = m + jnp.log(l)
        if comm:
            @pl.when(h == heads - 1)
            def _():
                comm.finish(cin_refs, cout_refs, sems)

    def head_col(off):
        return pl.BlockSpec((s, HEAD_DIM), lambda h: (0, off * heads + h))
    vec = pl.BlockSpec((1, HEAD_DIM), lambda h: (0, 0))
    res = pl.pallas_call(
        body, name="attn_fwd", grid=(heads,),
        in_specs=[head_col(3), head_col(4), head_col(5), vec, vec,
                  pl.BlockSpec(memory_space=pltpu.SMEM),
                  pl.BlockSpec((len(DILATED), BLK, 2 * BLK), lambda h: (0, 0, 0))] + [ANY] * len(c_ins),
        out_specs=[pl.BlockSpec((s, HEAD_DIM), lambda h: (0, h)), pl.BlockSpec((None, s, 1), lambda h: (h, 0, 0))]
        + [ANY] * len(c_outs),
        out_shape=[jax.ShapeDtypeStruct((s, heads * HEAD_DIM), F32), jax.ShapeDtypeStruct((heads, s, 1), F32)] + c_outs,
        scratch_shapes=[pltpu.VMEM((s, HEAD_DIM), BF16), pltpu.VMEM((s + BLK, HEAD_DIM), BF16),
                        pltpu.VMEM((s + BLK, HEAD_DIM), BF16),
                        pltpu.VMEM((len(DILATED), s, 1), F32), pltpu.VMEM((len(DILATED), s, 1), F32),
                        pltpu.VMEM((len(DILATED), s, HEAD_DIM), F32),
                        pltpu.VMEM((2 * len(DILATED), BLK, 2 * BLK), F32),
                        pltpu.VMEM((ATTN_GROUP, BLK, 2 * BLK), F32), pltpu.VMEM((ATTN_GROUP, BLK, 2 * BLK), BF16)] + c_scratch,
        input_output_aliases={7 + i: 2 + i for i in range(len(c_ins))} if (comm and comm.in_place) else {},
        compiler_params=_params("arbitrary"),
    )(proj, proj, proj, g_q, g_k, rel_bias, tables, *c_ins)
    return res[0], res[1], res[2:]


def _attn_bwd(dproj, d_o, stats, proj, g_q, g_k, rel_bias, d_model, comm=None):
    s, d_in = proj.shape
    heads = d_model // 2 // HEAD_DIM
    n_blocks = s // BLK
    scale = HEAD_DIM ** -0.5
    tables = jnp.asarray(_bucket_tables())
    n_dil = len(DILATED)
    c_ins = list(comm.ins) if comm else []
    c_outs = list(comm.out_shape) if comm else []
    c_scratch = list(comm.scratch) if comm else []

    def body(dp_any, q_ref, k_ref, v_ref, do_ref, st_ref, gq_ref, gk_ref, rb_ref, tab_ref, *rest):
        cin_refs, rest = rest[:len(c_ins)], rest[len(c_ins):]
        dp_out, ggq_ref, ggk_ref, db_ref = rest[:4]
        cout_refs, rest = rest[4:4 + len(c_outs)], rest[4 + len(c_outs):]
        (qs, ks, vs, dos, st_s, dqn, dkn, dvv, dq_u, dk_u, dv_u, bias_s, dbias_s, sc_s, dp_s, p_s, ds_s,
         sems) = rest[:18]
        comm_sems = rest[18:]
        del dp_any
        h = pl.program_id(0)
        if comm:
            @pl.when(h == 0)
            def _():
                comm.start(cin_refs, cout_refs, comm_sems)


        @pl.when(h == 0)
        def _():
            ggq_ref[...] = jnp.zeros_like(ggq_ref)
            ggk_ref[...] = jnp.zeros_like(ggk_ref)

        dbias_s[...] = jnp.zeros_like(dbias_s)
        _bias_tiles(tab_ref, rb_ref, h, bias_s)
        gq = gq_ref[...] * scale
        gk = gk_ref[...]
        ks[0:BLK, :] = jnp.zeros((BLK, HEAD_DIM), BF16)
        vs[0:BLK, :] = jnp.zeros((BLK, HEAD_DIM), BF16)

        for i, (_, dil) in enumerate(DILATED):
            def prepare(b, carry, dil=dil):
                rows, _ = _block_rows(b, n_blocks, dil)
                qs[_sub_block(b), :] = _rms_rows(q_ref[rows, :], gq).astype(BF16)
                ks[_sub_block(b, pad=BLK), :] = _rms_rows(k_ref[rows, :], gk).astype(BF16)
                vs[_sub_block(b, pad=BLK), :] = v_ref[rows, :].astype(BF16)
                dos[_sub_block(b), :] = do_ref[rows, :].astype(BF16)
                st_s[_sub_block(b), :] = st_ref[rows, :]
                return carry
            lax.fori_loop(0, n_blocks, prepare, 0, unroll=8)
            dk_u[...] = jnp.zeros_like(dk_u)
            dv_u[...] = jnp.zeros_like(dv_u)

            def group(g, carry, i=i, dil=dil):
                blocks = [g * ATTN_GROUP + t for t in range(ATTN_GROUP)]
                for t, b in enumerate(blocks):
                    sc_s[t] = _dot(qs[_sub_block(b), :], ks[_sub_block(b, 2 * BLK), :], tb=True)
                    dp_s[t] = _dot(dos[_sub_block(b), :], vs[_sub_block(b, 2 * BLK), :], tb=True)
                for t, b in enumerate(blocks):
                    _, n = _block_rows(b, n_blocks, dil)
                    st = st_s[_sub_block(b), :]
                    pr = jnp.exp(sc_s[t] + bias_s[2 * i + jnp.minimum(n, 1)] - st[:, 0:1])
                    ds = pr * (dp_s[t] - st[:, 1:2])
                    dbias_s[i] += ds
                    p_s[t] = pr.astype(BF16)
                    ds_s[t] = ds.astype(BF16)
                for t, b in enumerate(blocks):
                    dq_u[_sub_block(b), :] = _dot(ds_s[t], ks[_sub_block(b, 2 * BLK), :])
                    dk_u[_sub_block(b, 2 * BLK), :] += _dot(ds_s[t], qs[_sub_block(b), :], ta=True)
                    dv_u[_sub_block(b, 2 * BLK), :] += _dot(p_s[t], dos[_sub_block(b), :], ta=True)
                return carry
            lax.fori_loop(0, n_blocks // ATTN_GROUP, group, 0)

            def scatter(b, carry, i=i, dil=dil):
                rows, _ = _block_rows(b, n_blocks, dil)
                for acc, part, pad in ((dqn, dq_u, 0), (dkn, dk_u, BLK), (dvv, dv_u, BLK)):
                    val = part[_sub_block(b, pad=pad), :]
                    acc[rows, :] = val if i == 0 else acc[rows, :] + val
                return carry
            lax.fori_loop(0, n_blocks, scatter, 0, unroll=8)

        q = q_ref[...]
        rq = lax.rsqrt(jnp.mean(q * q, axis=-1, keepdims=True) + EPS)
        k = k_ref[...]
        rk = lax.rsqrt(jnp.mean(k * k, axis=-1, keepdims=True) + EPS)
        dq_n = dqn[...]
        ggq_ref[...] += jnp.sum(dq_n * (q * rq) * scale, axis=0, keepdims=True)
        dg = dq_n * gq_ref[...] * scale
        qs[...] = (rq * (dg - q * (rq * rq) * jnp.mean(dg * q, axis=-1, keepdims=True))).astype(BF16)
        dk_n = dkn[...]
        ggk_ref[...] += jnp.sum(dk_n * (k * rk), axis=0, keepdims=True)
        dg = dk_n * gk_ref[...]
        dos[...] = (rk * (dg - k * (rk * rk) * jnp.mean(dg * k, axis=-1, keepdims=True))).astype(BF16)
        vs[BLK:, :] = dvv[...].astype(BF16)
        copies = [pltpu.make_async_copy(src, dp_out.at[:, pl.ds(pl.multiple_of(((3 + j) * heads + h) * HEAD_DIM, HEAD_DIM), HEAD_DIM)],
                                        sems.at[j]) for j, src in enumerate((qs, dos, vs.at[pl.ds(BLK, s)]))]
        for cp in copies:
            cp.start()
        for b in range(NUM_BUCKETS):
            tot = jnp.zeros((BLK, 2 * BLK), F32)
            for i in range(n_dil):
                tot = tot + jnp.where(tab_ref[i] == b, dbias_s[i], 0.0)
            db_ref[b:b + 1, :] = jnp.full((1, LANE), jnp.sum(tot), F32)
        for cp in copies:
            cp.wait()
        if comm:
            @pl.when(h == heads - 1)
            def _():
                comm.finish(cin_refs, cout_refs, comm_sems)

    def head_col(off):
        return pl.BlockSpec((s, HEAD_DIM), lambda h: (0, off * heads + h))
    own_col = pl.BlockSpec((s, HEAD_DIM), lambda h: (0, h))
    vec = pl.BlockSpec((1, HEAD_DIM), lambda h: (0, 0))
    big = pltpu.VMEM((s, HEAD_DIM), F32)
    padded32 = pltpu.VMEM((s + BLK, HEAD_DIM), F32)
    padded16 = pltpu.VMEM((s + BLK, HEAD_DIM), BF16)
    res = pl.pallas_call(
        body, name="attn_bwd", grid=(heads,),
        in_specs=[ANY, head_col(3), head_col(4), head_col(5), own_col,
                  pl.BlockSpec((None, s, 2), lambda h: (h, 0, 0)), vec, vec,
                  pl.BlockSpec(memory_space=pltpu.SMEM),
                  pl.BlockSpec((n_dil, BLK, 2 * BLK), lambda h: (0, 0, 0))] + [ANY] * len(c_ins),
        out_specs=[ANY, vec, vec, pl.BlockSpec((None, NUM_BUCKETS, LANE), lambda h: (h, 0, 0))] + [ANY] * len(c_outs),
        out_shape=[jax.ShapeDtypeStruct((s, d_in), BF16), jax.ShapeDtypeStruct((1, HEAD_DIM), F32),
                   jax.ShapeDtypeStruct((1, HEAD_DIM), F32), jax.ShapeDtypeStruct((heads, NUM_BUCKETS, LANE), F32)]
        + c_outs,
        scratch_shapes=[pltpu.VMEM((s, HEAD_DIM), BF16), padded16, padded16, pltpu.VMEM((s, HEAD_DIM), BF16),
                        pltpu.VMEM((s, 2), F32),
                        big, big, big, big, padded32, padded32,
                        pltpu.VMEM((2 * n_dil, BLK, 2 * BLK), F32), pltpu.VMEM((n_dil, BLK, 2 * BLK), F32),
                        pltpu.VMEM((ATTN_GROUP, BLK, 2 * BLK), F32), pltpu.VMEM((ATTN_GROUP, BLK, 2 * BLK), F32),
                        pltpu.VMEM((ATTN_GROUP, BLK, 2 * BLK), BF16), pltpu.VMEM((ATTN_GROUP, BLK, 2 * BLK), BF16),
                        pltpu.SemaphoreType.DMA((3,))] + c_scratch,
        input_output_aliases={0: 0},
        compiler_params=_params("arbitrary"),
    )(dproj, proj, proj, proj, d_o, stats, g_q, g_k, rel_bias, tables, *c_ins)
    return res[0], res[1], res[2], res[3], res[4:]


def _local_step(xs, ps, tgt, sp, win4, wout4, wgate4, wup4, dist=None):
    s, d = xs.shape
    wa = d // 2
    heads = wa // HEAD_DIM
    d_in = 7 * wa
    c4 = d_in // N_CHIPS
    dple = ps.shape[1]

    tm, tn = _tile(s, 1024), _tile(d, 1024)
    tn_in = _tile(c4, 1024)
    per_in = c4 // tn_in
    tn_up = _tile(d // N_CHIPS, 512)
    per_up = (d // N_CHIPS) // tn_up
    gm, gn = s // tm, d // tn

    hn, rx = _rms_fwd("rms_pre", xs, sp["g_pre"])
    if dist:
        proj, win4 = _mm_in_gather(dist["chip"], hn, win4)
    else:
        proj = _matmul(
            "mm_in", hn, win4, pl.BlockSpec((tm, d), lambda i, j, k: (i, 0)), _shard_spec(d, tn_in, per_in, 2, 1),
            (gm, d_in // tn_in, 1), None, jax.ShapeDtypeStruct((s, d_in), F32),
            pl.BlockSpec((tm, tn_in), lambda i, j, k: (i, j)))
    b_st = sp["b_s"].T
    y, z, vn, mu_v, rs_v, ra = _gmlp_fwd(proj, sp["w_s"], b_st, sp["ln_v_g"], sp["ln_v_b"], sp["g_out_a"], d)
    yb, lse, gathered = _attn_fwd(proj, sp["g_q"], sp["g_k"], sp["rel_bias"], d,
                                  comm=_Gather([wout4, wgate4, wup4]) if dist else None)
    if dist:
        wout4, wgate4, wup4 = gathered
    wout, wgate = wout4.reshape(d, d), wgate4.reshape(d, d)
    y, rb = _gate_fwd(y, yb, proj, sp["g_out_b"], d)

    def residual(acc, extra, outs):
        outs[0][...] = extra[0][...] + acc

    tile = pl.BlockSpec((tm, tn), lambda i, j, k: (i, j))
    tile_up = pl.BlockSpec((tm, tn_up), lambda i, j, k: (i, j))
    h = _matmul(
        "mm_out", y, wout, pl.BlockSpec((tm, d), lambda i, j, k: (i, 0)), pl.BlockSpec((d, tn_up), lambda i, j, k: (0, j)),
        (gm, d // tn_up, 1), None, jax.ShapeDtypeStruct((s, d), F32), tile_up, extras=(xs,), extra_specs=(tile_up,),
        epilogue=residual)
    hp, rh = _rms_fwd("rms_ple", h, sp["g_ple"])
    p16 = ps.astype(BF16)

    def head(acc, extra, outs):
        p_ref, wup_ref, h_ref, t_ref = extra
        dout_ref, dgl_ref, dup_ref, loss_ref = outs
        up = _dot(p_ref[...], wup_ref[...])
        gate = jax.nn.sigmoid(acc)
        err = h_ref[...] + gate * up - t_ref[...]
        sq = _colsum8(err * err)
        part = sq[:, 0:LANE]
        for c in range(1, sq.shape[1] // LANE):
            part = part + sq[:, c * LANE:(c + 1) * LANE]
        loss_ref[...] = part
        dout = err * (1.0 / d)
        dout_ref[...] = dout
        dup_ref[...] = (dout * gate).astype(BF16)
        dgl_ref[...] = (dout * up * gate * (1.0 - gate)).astype(BF16)

    tile_up = pl.BlockSpec((tm, tn_up), lambda i, j, k: (i, j))
    dout, dglin, dup, loss_parts = _matmul(
        "mm_gate_loss", hp, wgate, pl.BlockSpec((tm, d), lambda i, j, k: (i, 0)), pl.BlockSpec((d, tn_up), lambda i, j, k: (0, j)),
        (gm, d // tn_up, 1), None,
        (jax.ShapeDtypeStruct((s, d), F32), jax.ShapeDtypeStruct((s, d), BF16), jax.ShapeDtypeStruct((s, d), BF16),
         jax.ShapeDtypeStruct((gm * 8, (d // tn_up) * LANE), F32)),
        (tile_up, tile_up, tile_up, pl.BlockSpec((8, LANE), lambda i, j, k: (i, j))),
        extras=(p16, wup4, h, tgt),
        extra_specs=(pl.BlockSpec((tm, dple), lambda i, j, k: (i, 0)), _shard_spec(dple, tn_up, per_up, 2, 1),
                     tile_up, tile_up),
        epilogue=head)
    loss = 0.5 * jnp.sum(loss_parts) * (1.0 / d)

    dhp = _matmul(
        "mm_dhp", dglin, wgate, pl.BlockSpec((tm, d), lambda i, j, k: (i, 0)), pl.BlockSpec((tn, d), lambda i, j, k: (j, 0)),
        (gm, gn, 1), None, jax.ShapeDtypeStruct((s, d), F32), tile, tb=True)
    dh, dh16, g_g_ple = _rms_bwd("rms_ple_bwd", dhp, h, rh, sp["g_ple"], dout, True)
    tmw = _tile(d, 1024)
    g_wout = _matmul(
        "mm_gwout", y, dh16, pl.BlockSpec((s, tmw), lambda i, j, k: (0, i)), pl.BlockSpec((s, tn), lambda i, j, k: (0, j)),
        (d // tmw, gn, 1), None, jax.ShapeDtypeStruct((d, d), F32), pl.BlockSpec((tmw, tn), lambda i, j, k: (i, j)), ta=True)
    g_wout4 = g_wout.reshape(N_CHIPS, d // N_CHIPS, d)
    g_wgate = _matmul(
        "mm_gwgate", hp, dglin, pl.BlockSpec((s, tmw), lambda i, j, k: (0, i)), pl.BlockSpec((s, tn), lambda i, j, k: (0, j)),
        (d // tmw, gn, 1), None, jax.ShapeDtypeStruct((d, d), F32), pl.BlockSpec((tmw, tn), lambda i, j, k: (i, j)), ta=True,
        comm=_SwapHalves([g_wout4]) if dist else None)
    if dist:
        g_wgate, swapped_out = g_wgate
    g_wgate4 = g_wgate.reshape(N_CHIPS, d // N_CHIPS, d)
    g_wup = _matmul(
        "mm_gwup", p16, dup, pl.BlockSpec((s, dple), lambda i, j, k: (0, 0)), pl.BlockSpec((s, tn_up), lambda i, j, k: (0, j)),
        (1, d // tn_up, 1), None, jax.ShapeDtypeStruct((N_CHIPS, dple, d // N_CHIPS), F32),
        _shard_spec(dple, tn_up, per_up, 0, 1), ta=True)
    dy = _matmul(
        "mm_dy", dh16, wout, pl.BlockSpec((tm, d), lambda i, j, k: (i, 0)), pl.BlockSpec((tn, d), lambda i, j, k: (j, 0)),
        (gm, gn, 1), None, jax.ShapeDtypeStruct((s, d), F32), tile, tb=True,
        comm=_SwapHalves([g_wgate4, g_wup]) if dist else None)
    early = (g_wout4, g_wgate4, g_wup)
    if dist:
        core = dist["core"]
        dy, swapped_rest = dy
        early_sums = [_chip_sum("chip_sum_" + n, dist["chip"], core, g, o)
                      for n, g, o in zip(LARGE[1:], early, list(swapped_out) + list(swapped_rest))]

    dproj, d_o, stats, g_g_out_b = _gate_bwd(dy, yb, lse, proj, rb, sp["g_out_b"], d)
    dproj, g_g_q, g_g_k, dbias, early_got = _attn_bwd(
        dproj, d_o, stats, proj, sp["g_q"], sp["g_k"], sp["rel_bias"], d,
        comm=_ExchangeChips([s16 for _, s16 in early_sums]) if dist else None)
    dproj, g_w_s, dz_sum, g_ln_g, g_ln_b, g_g_out_a = _gmlp_bwd(
        dproj, dy, proj, z, vn, mu_v, rs_v, ra, sp["w_s"], sp["ln_v_g"], sp["g_out_a"], d)

    if dist:
        tmw_half = _tile(d // 2, 1024)
        n_half = (d // 2) // tmw_half

        def half_of_gwin(name, own, comm):
            def rows(i, j, k, core_ref):
                return (0, (core_ref[0] if own else 1 - core_ref[0]) * n_half + i)
            return _matmul(
                name, hn, dproj, pl.BlockSpec((s, tmw_half), rows),
                pl.BlockSpec((s, tn_in), lambda i, j, k, core_ref: (0, j)),
                (n_half, d_in // tn_in, 1), None, jax.ShapeDtypeStruct((N_CHIPS, d // 2, c4), F32),
                pl.BlockSpec((None, tmw_half, tn_in), lambda i, j, k, core_ref: (j // per_in, i, j % per_in)), ta=True,
                comm=comm, prefetch=(core,))

        g_theirs = half_of_gwin("mm_gwin_theirs", False, None)
        g_mine, (g_from_sibling,) = half_of_gwin("mm_gwin_mine", True, _SwapHalves([g_theirs], whole=True))
        win_sum = _chip_sum("chip_sum_w_in", dist["chip"], core, g_mine, g_from_sibling)
    else:
        g_win = _matmul(
            "mm_gwin", hn, dproj, pl.BlockSpec((s, tmw), lambda i, j, k: (0, i)),
            pl.BlockSpec((s, tn_in), lambda i, j, k: (0, j)),
            (d // tmw, d_in // tn_in, 1), None, jax.ShapeDtypeStruct((N_CHIPS, d, c4), F32),
            _shard_spec(tmw, tn_in, per_in, 0, 1), ta=True)
    tk = c4
    tmh, tnh = _tile(s, 1024), _tile(d, 1024)
    dhn = _matmul(
        "mm_dhn", dproj, win4, pl.BlockSpec((tmh, tk), lambda i, j, k: (i, k)), _shard_spec(tnh, tk, 1, 1, 2),
        (s // tmh, d // tnh, d_in // tk), (tmh, tnh), jax.ShapeDtypeStruct((s, d), F32),
        pl.BlockSpec((tmh, tnh), lambda i, j, k: (i, j)), tb=True,
        comm=_ExchangeChips([win_sum[1]]) if dist else None)
    if dist:
        dhn, win_got = dhn
    grad_x, g_g_pre = _rms_bwd("rms_pre_bwd", dhn, xs, rx, sp["g_pre"], dh, False)

    small = {
        "g_pre": g_g_pre,
        "w_s": g_w_s,
        "b_s": jnp.sum(dz_sum.reshape(CHUNK, heads, HEAD_DIM), axis=-1).T,
        "ln_v_g": g_ln_g, "ln_v_b": g_ln_b,
        "g_q": g_g_q, "g_k": g_g_k,
        "rel_bias": dbias[:, :, 0].T,
        "g_out_a": g_g_out_a, "g_out_b": g_g_out_b,
        "g_ple": g_g_ple,
    }
    if not dist:
        return loss, grad_x, (g_win, *early), small
    sums32 = [win_sum[0]] + [s32 for s32, _ in early_sums]
    return loss, grad_x, (sums32, list(win_got) + list(early_got)), small


def _place():
    x, y, c = lax.axis_index("x"), lax.axis_index("y"), lax.axis_index("c")
    chips = [(1 - x, y), (x, 1 - y), (1 - x, 1 - y)]
    return x, y, c, chips


def _remote(src, dst, send_sems, recv_sems, k, to):
    return pltpu.make_async_remote_copy(src_ref=src, dst_ref=dst, send_sem=send_sems.at[k], recv_sem=recv_sems.at[k],
                                        device_id=to, device_id_type=MESH)


def _cast_bf16(name, chip, w):
    r, c = w.shape
    tm = _tile(r, 256)

    def body(chip_ref, w_ref, o_ref):
        del chip_ref
        o_ref[...] = w_ref[...].astype(BF16)

    return pl.pallas_call(
        body, name=name,
        grid_spec=pltpu.PrefetchScalarGridSpec(
            num_scalar_prefetch=1, grid=(r // tm,),
            in_specs=[pl.BlockSpec((tm, c), lambda i, chip_ref: (i, 0))],
            out_specs=pl.BlockSpec((None, tm, c), lambda i, chip_ref: (chip_ref[0], i, 0))),
        out_shape=jax.ShapeDtypeStruct((N_CHIPS, r, c), BF16), compiler_params=_params("parallel"),
    )(chip, w)


class _Gather:
    in_place = True

    def __init__(self, fulls):
        self.ins = list(fulls)
        self.out_shape = [jax.ShapeDtypeStruct(f.shape, f.dtype) for f in fulls]
        n = len(fulls)
        self.scratch = [pltpu.SemaphoreType.DMA((6 * n,)), pltpu.SemaphoreType.DMA((6 * n,))]

    @staticmethod
    def _sends(outs, sems):
        send_sems, recv_sems = sems
        x, y, c, chips = _place()
        cps = []
        for w, ref in enumerate(outs):
            half = ref.shape[1] // 2
            blk = ref.at[2 * x + y, pl.ds(c * half, half)]
            cps += [_remote(blk, blk, send_sems, recv_sems, 6 * w + q, (*chip, c)) for q, chip in enumerate(chips)]
        return cps

    def start(self, ins, outs, sems):
        for cp in self._sends(outs, sems):
            cp.start()

    def finish(self, ins, outs, sems):
        send_sems, recv_sems = sems
        x, y, c, chips = _place()
        sibling = (x, y, 1 - c)
        forwards = []
        for w, ref in enumerate(outs):
            half = ref.shape[1] // 2
            for q, chip in enumerate(chips):
                blk = ref.at[2 * chip[0] + chip[1], pl.ds(c * half, half)]
                _remote(blk, blk, send_sems, recv_sems, 6 * w + q, sibling).wait_recv()
                fwd = _remote(blk, blk, send_sems, recv_sems, 6 * w + 3 + q, sibling)
                fwd.start()
                forwards.append(fwd)
        for w, ref in enumerate(outs):
            half = ref.shape[1] // 2
            for q, chip in enumerate(chips):
                blk = ref.at[2 * chip[0] + chip[1], pl.ds((1 - c) * half, half)]
                _remote(blk, blk, send_sems, recv_sems, 6 * w + 3 + q, sibling).wait_recv()
        for cp in self._sends(outs, sems) + forwards:
            cp.wait_send()


class _ExchangeChips:
    in_place = False

    def __init__(self, sums16):
        self.ins = list(sums16)
        self.out_shape = [jax.ShapeDtypeStruct(g.shape, g.dtype) for g in sums16]
        n = len(sums16)
        self.scratch = [pltpu.SemaphoreType.DMA((3 * n,)), pltpu.SemaphoreType.DMA((3 * n,))]

    @staticmethod
    def _sends(ins, outs, sems):
        send_sems, recv_sems = sems
        x, y, c, chips = _place()
        return [_remote(ins[w].at[2 * chip[0] + chip[1]], outs[w].at[2 * x + y], send_sems, recv_sems, 3 * w + q, (*chip, c))
                for w in range(len(ins)) for q, chip in enumerate(chips)]

    def start(self, ins, outs, sems):
        for cp in self._sends(ins, outs, sems):
            cp.start()

    def finish(self, ins, outs, sems):
        send_sems, recv_sems = sems
        x, y, c, chips = _place()
        for w in range(len(ins)):
            for q, chip in enumerate(chips):
                blk = outs[w].at[2 * chip[0] + chip[1]]
                _remote(blk, blk, send_sems, recv_sems, 3 * w + q, (*chip, c)).wait_recv()
        for cp in self._sends(ins, outs, sems):
            cp.wait_send()


def _mm_in_gather(chip, hn, win4):
    s, d = hn.shape
    c4 = win4.shape[2]
    tm, tn = _tile(s, 1024), _tile(c4, 1024)
    per = c4 // tn
    nj, gm = N_CHIPS * per, s // tm
    half = d // 2

    def tile_of(j):
        jj = j - per
        o = jnp.where(j < per, 0, jnp.where(j < 3 * per, 1 + jj % 2, 3))
        t = jnp.where(j < per, j, jnp.where(j < 3 * per, jj // 2, j - 3 * per))
        return o, t

    def step_of(o, t):
        return per + 2 * t + o - 1 if o in (1, 2) else o * per + t

    def block_of(j, chip_ref):
        o, _ = tile_of(j)
        return chip_ref[0] ^ (((o & 1) << 1) | (o >> 1))

    i_late = max(gm - 2, 0)

    def body(chip_ref, hn_ref, w_any, proj_ref, w_ref, wbuf, wsem, send_sems, recv_sems):
        del w_any
        j, i = pl.program_id(0), pl.program_id(1)
        x, y, c, chips = _place()
        sibling = (x, y, 1 - c)

        def region(k, rows_half, t):
            return w_ref.at[k, pl.ds(rows_half * half, half), pl.ds(t * tn, tn)]

        def send_mine(q, t):
            blk = region(2 * x + y, c, t)
            return _remote(blk, blk, send_sems, recv_sems, q * per + t, (*chips[q], c))

        def forward(q, t):
            blk = region(2 * chips[q][0] + chips[q][1], c, t)
            return _remote(blk, blk, send_sems, recv_sems, (3 + q) * per + t, sibling)

        def fetch(jj, mine, slot):
            rows = pl.ds((c if mine else 1 - c) * half, half)
            cols = pl.ds(pl.multiple_of(tile_of(jj)[1] * tn, LANE), tn)
            return pltpu.make_async_copy(w_ref.at[block_of(jj, chip_ref), rows, cols], wbuf.at[slot, rows],
                                         wsem.at[2 * slot + (0 if mine else 1)])

        def foreign_tiles():
            return [(o, t) for o in range(1, N_CHIPS) for t in range(per)]

        relayed = [(0, t) for t in range(1, per, 2)] + [(1, t) for t in range(0, per, 2)]

        def relay(q, t):
            blk = region(2 * chips[q][0] + chips[q][1], c, t)
            return _remote(blk, blk, send_sems, recv_sems, 2 * per + t, (*chips[1 - q], c))

        def landed(q, t):
            blk = region(2 * chips[q][0] + chips[q][1], c, t)
            _remote(blk, blk, send_sems, recv_sems, q * per + t, sibling).wait_recv()
            forward(q, t).start()
            if (q, t) in relayed:
                relay(q, t).start()

        @pl.when((j == 0) & (i == 0))
        def _():
            for t in range(per):
                send_mine(0, t).start()
                send_mine(1, t).start()
            fetch(0, True, 0).start()
            fetch(0, False, 0).start()

        nxt = j + 1

        @pl.when((i == 0) & (nxt < nj))
        def _():
            for o, t in foreign_tiles():
                @pl.when(nxt == step_of(o, t))
                def _():
                    if o == 1:
                        landed(0, t)
                        landed(1, t)
                    elif o == 3:
                        landed(2, t)

            fetch(nxt, True, nxt % 2).start()

            @pl.when(nxt < per)
            def _():
                fetch(nxt, False, nxt % 2).start()

        @pl.when((i == i_late) & (nxt < nj))
        def _():
            for o, t in foreign_tiles():
                @pl.when(nxt == step_of(o, t))
                def _():
                    blk = region(2 * chips[o - 1][0] + chips[o - 1][1], 1 - c, t)
                    _remote(blk, blk, send_sems, recv_sems, (3 + o - 1) * per + t, sibling).wait_recv()
                    fetch(nxt, False, nxt % 2).start()

        @pl.when(i == 0)
        def _():
            fetch(j, True, j % 2).wait()
            fetch(j, False, j % 2).wait()

        proj_ref[...] = _dot(hn_ref[...], wbuf[j % 2])

        @pl.when((j == nj - 1) & (i == gm - 1))
        def _():
            for t in range(per):
                send_mine(0, t).wait_send()
                send_mine(1, t).wait_send()
                for q in range(3):
                    forward(q, t).wait_send()
            for q, t in relayed:
                relay(q, t).wait_send()

    return pl.pallas_call(
        body, name="mm_in_gather",
        grid_spec=pltpu.PrefetchScalarGridSpec(
            num_scalar_prefetch=1, grid=(nj, gm),
            in_specs=[pl.BlockSpec((tm, d), lambda j, i, chip_ref: (i, 0)), ANY],
            out_specs=[pl.BlockSpec((tm, tn), lambda j, i, chip_ref: (i, block_of(j, chip_ref) * per + tile_of(j)[1])), ANY],
            scratch_shapes=[pltpu.VMEM((2, d, tn), BF16), pltpu.SemaphoreType.DMA((4,)),
                            pltpu.SemaphoreType.DMA((6 * per,)), pltpu.SemaphoreType.DMA((6 * per,))]),
        out_shape=[jax.ShapeDtypeStruct((s, N_CHIPS * c4), F32), jax.ShapeDtypeStruct(win4.shape, win4.dtype)],
        input_output_aliases={2: 1},
        compiler_params=_params("arbitrary", "arbitrary"),
    )(chip, hn, win4)


class _SwapHalves:
    in_place = False

    def __init__(self, grads, whole=False):
        self.ins = list(grads)
        self.whole = whole
        self.out_shape = [jax.ShapeDtypeStruct((N_CHIPS, g.shape[1] // (1 if whole else 2), g.shape[2]), g.dtype)
                          for g in grads]
        self.scratch = [pltpu.SemaphoreType.DMA((len(grads),)), pltpu.SemaphoreType.DMA((len(grads),))]

    def _copies(self, ins, outs, sems):
        x, y, c, _ = _place()
        cps = []
        for w in range(len(ins)):
            half = ins[w].shape[1] // 2
            src = ins[w] if self.whole else ins[w].at[:, pl.ds((1 - c) * half, half)]
            cps.append(_remote(src, outs[w], sems[0], sems[1], w, (x, y, 1 - c)))
        return cps

    def start(self, ins, outs, sems):
        for cp in self._copies(ins, outs, sems):
            cp.start()

    def finish(self, ins, outs, sems):
        for cp in self._copies(ins, outs, sems):
            cp.wait()


def _chip_sum(name, chip, core, grad, got):
    _, half, c = got.shape
    th = _tile(half, 128)
    n = half // th
    skip = n if grad.shape[1] != half else 0

    def body(chip_ref, core_ref, g_ref, o_ref, s_ref):
        del chip_ref, core_ref
        s_ref[...] = (g_ref[...] + o_ref[...]).astype(s_ref.dtype)

    def specs(first, out_block):
        def block(k, i, chip_ref):
            return (chip_ref[0] + first + k) % N_CHIPS

        return dict(
            in_specs=[pl.BlockSpec((None, th, c), lambda k, i, chip_ref, core_ref: (block(k, i, chip_ref), core_ref[0] * skip + i, 0)),
                      pl.BlockSpec((None, th, c), lambda k, i, chip_ref, core_ref: (block(k, i, chip_ref), i, 0))],
            out_specs=out_block(block))

    own = pl.pallas_call(
        body, name=name + "_own",
        grid_spec=pltpu.PrefetchScalarGridSpec(
            num_scalar_prefetch=2, grid=(1, n),
            **specs(0, lambda block: pl.BlockSpec((th, c), lambda k, i, chip_ref, core_ref: (i, 0)))),
        out_shape=jax.ShapeDtypeStruct((half, c), F32), compiler_params=_params("parallel", "parallel"),
    )(chip, core, grad, got)
    others = pl.pallas_call(
        functools.partial(body), name=name + "_others",
        grid_spec=pltpu.PrefetchScalarGridSpec(
            num_scalar_prefetch=2, grid=(N_CHIPS - 1, n),
            **specs(1, lambda block: pl.BlockSpec(
                (None, th, c), lambda k, i, chip_ref, core_ref: (block(k, i, chip_ref), i, 0)))),
        out_shape=jax.ShapeDtypeStruct((N_CHIPS, half, c), BF16), compiler_params=_params("parallel", "parallel"),
    )(chip, core, grad, got)
    return own, others


def _riding(comm, grid, body, n_pre, n_in, n_out):
    if comm is None:
        return body
    c_in, c_out, c_scr = len(comm.ins), len(comm.out_shape), len(comm.scratch)

    def wrapped(*refs):
        pre, rest = refs[:n_pre], refs[n_pre:]
        ins, rest = rest[:n_in], rest[n_in:]
        cin, rest = rest[:c_in], rest[c_in:]
        outs, rest = rest[:n_out], rest[n_out:]
        cout, rest = rest[:c_out], rest[c_out:]
        own, sems = rest[:len(rest) - c_scr], rest[len(rest) - c_scr:]
        ids = [pl.program_id(ax) for ax in range(len(grid))]
        first, last = ids[0] == 0, ids[0] == grid[0] - 1
        for ax in range(1, len(grid)):
            first, last = first & (ids[ax] == 0), last & (ids[ax] == grid[ax] - 1)

        @pl.when(first)
        def _():
            comm.start(cin, cout, sems)

        body(*pre, *ins, *outs, *own)

        @pl.when(last)
        def _():
            comm.finish(cin, cout, sems)

    return wrapped


def _total(name, chip, core, sum32, got16, comm=None):
    half, c = sum32.shape
    th = _tile(half, 128)
    n = half // th

    def body(chip_ref, core_ref, own_ref, a_ref, b_ref, c_ref, o_ref):
        del chip_ref, core_ref
        o_ref[...] = ((own_ref[...] + a_ref[...].astype(F32)) + b_ref[...].astype(F32)) + c_ref[...].astype(F32)

    def other(step):
        return pl.BlockSpec((None, th, c), lambda i, chip_ref, core_ref: ((chip_ref[0] + step) % N_CHIPS, i, 0))

    c_ins = list(comm.ins) if comm else []
    res = pl.pallas_call(
        _riding(comm, (n,), body, 2, 4, 1), name=name,
        grid_spec=pltpu.PrefetchScalarGridSpec(
            num_scalar_prefetch=2, grid=(n,),
            in_specs=[pl.BlockSpec((th, c), lambda i, chip_ref, core_ref: (i, 0)), other(1), other(2), other(3)]
            + [ANY] * len(c_ins),
            out_specs=[pl.BlockSpec((th, c), lambda i, chip_ref, core_ref: (core_ref[0] * n + i, 0))]
            + [ANY] * len(c_ins),
            scratch_shapes=list(comm.scratch) if comm else []),
        out_shape=[jax.ShapeDtypeStruct((2 * half, c), F32)] + (list(comm.out_shape) if comm else []),
        input_output_aliases={6 + i: 1 + i for i in range(len(c_ins))} if comm else {},
        compiler_params=_params("arbitrary" if comm else "parallel"),
    )(chip, core, sum32, got16, got16, got16, *c_ins)
    return (res[0], res[1:]) if comm else res[0]


class _JoinHalves:
    in_place = True

    def __init__(self, blocks):
        self.ins = list(blocks)
        self.out_shape = [jax.ShapeDtypeStruct(t.shape, t.dtype) for t in blocks]
        self.scratch = [pltpu.SemaphoreType.DMA((len(blocks),)), pltpu.SemaphoreType.DMA((len(blocks),))]

    @staticmethod
    def _copy(ref, w, mine, sems):
        x, y, c, _ = _place()
        half = ref.shape[0] // 2
        rows = ref.at[pl.ds((c if mine else 1 - c) * half, half)]
        return _remote(rows, rows, sems[0], sems[1], w, (x, y, 1 - c))

    def start(self, ins, outs, sems):
        for w, ref in enumerate(outs):
            self._copy(ref, w, True, sems).start()

    def finish(self, ins, outs, sems):
        for w, ref in enumerate(outs):
            self._copy(ref, w, True, sems).wait_send()
            self._copy(ref, w, False, sems).wait_recv()


class _Both:
    in_place = True

    def __init__(self, a, b):
        assert a.in_place and b.in_place
        self.parts = (a, b)
        self.ins = list(a.ins) + list(b.ins)
        self.out_shape = list(a.out_shape) + list(b.out_shape)
        self.scratch = list(a.scratch) + list(b.scratch)

    def _split(self, ins, outs, sems):
        a = self.parts[0]
        n, k = len(a.ins), len(a.scratch)
        return (a, ins[:n], outs[:n], sems[:k]), (self.parts[1], ins[n:], outs[n:], sems[k:])

    def start(self, ins, outs, sems):
        for part, i, o, s in self._split(ins, outs, sems):
            part.start(i, o, s)

    def finish(self, ins, outs, sems):
        for part, i, o, s in self._split(ins, outs, sems):
            part.finish(i, o, s)


def _alone(name, comm):
    n = len(comm.ins)

    def body(*refs):
        comm.start(refs[:n], refs[n:2 * n], refs[2 * n:])
        comm.finish(refs[:n], refs[n:2 * n], refs[2 * n:])

    return pl.pallas_call(
        body, name=name, in_specs=[ANY] * n, out_specs=[ANY] * n, out_shape=comm.out_shape,
        scratch_shapes=comm.scratch, input_output_aliases={i: i for i in range(n)},
    )(*comm.ins)


class _GatherSmall:
    in_place = True

    def __init__(self, full):
        self.ins = [full]
        self.out_shape = [jax.ShapeDtypeStruct(full.shape, full.dtype)]
        self.scratch = [pltpu.SemaphoreType.DMA((7,)), pltpu.SemaphoreType.DMA((7,))]

    @staticmethod
    def _copy(ref, sems, k, block, to):
        m_per = ref.shape[0] // N_DEV
        px, py, pc = block
        rows = ref.at[pl.ds((4 * px + 2 * py + pc) * m_per, m_per), :]
        return _remote(rows, rows, sems[0], sems[1], k, to)

    def _first(self, ref, sems):
        x, y, c, chips = _place()
        me = (x, y, c)
        return [self._copy(ref, sems, 0, me, (x, y, 1 - c))] + \
               [self._copy(ref, sems, 1 + j, me, (*chip, c)) for j, chip in enumerate(chips)]

    def start(self, ins, outs, sems):
        for cp in self._first(outs[0], sems):
            cp.start()

    def finish(self, ins, outs, sems):
        ref = outs[0]
        x, y, c, chips = _place()
        me, sibling = (x, y, c), (x, y, 1 - c)
        passed = [self._copy(ref, sems, 4 + j, (*chip, c), sibling) for j, chip in enumerate(chips)]
        for j, chip in enumerate(chips):
            self._copy(ref, sems, 1 + j, (*chip, c), me).wait_recv()
            passed[j].start()
        self._copy(ref, sems, 0, sibling, me).wait_recv()
        for j, chip in enumerate(chips):
            self._copy(ref, sems, 4 + j, (*chip, 1 - c), me).wait_recv()
        for cp in self._first(ref, sems) + passed:
            cp.wait_send()


def _adamw_math(w, g, m, v):
    m = ADAM_B1 * m + (1.0 - ADAM_B1) * g
    v = ADAM_B2 * v + (1.0 - ADAM_B2) * (g * g)
    m_hat = m / (1.0 - ADAM_B1 ** ADAM_STEP)
    v_hat = v / (1.0 - ADAM_B2 ** ADAM_STEP)
    delta = -ADAM_LR * (m_hat / (jnp.sqrt(v_hat) + ADAM_EPS) + ADAM_WD * w)
    return delta, m, v


def _adamw(name, w, g, m, v):
    r, c = w.shape
    tm = _tile(r, 128)

    def body(w_ref, g_ref, m_ref, v_ref, g_out, d_out, m_out, v_out):
        g = g_ref[...]
        g_out[...] = g
        d_out[...], m_out[...], v_out[...] = _adamw_math(w_ref[...], g, m_ref[...], v_ref[...])

    spec = pl.BlockSpec((tm, c), lambda i: (i, 0))
    return pl.pallas_call(
        body, name=name, grid=(r // tm,), in_specs=[spec] * 4, out_specs=[spec] * 4,
        out_shape=[jax.ShapeDtypeStruct((r, c), F32)] * 4, compiler_params=_params("parallel"),
    )(w, g, m, v)


def _adamw_small(gathered, w, m, v):
    rows = w.shape[0]

    def body(all_ref, w_ref, m_ref, v_ref, g_out, d_out, m_out, v_out):
        g = all_ref[0:rows, :]
        for dev in range(1, N_DEV):
            g = g + all_ref[dev * rows:(dev + 1) * rows, :]
        g_out[...] = g
        d_out[...], m_out[...], v_out[...] = _adamw_math(w_ref[...], g, m_ref[...], v_ref[...])

    return pl.pallas_call(
        body, name="adamw_small", out_shape=[jax.ShapeDtypeStruct(w.shape, F32)] * 4, compiler_params=_params(),
    )(gathered, w, m, v)


SMALL = ("g_pre", "w_s", "b_s", "ln_v_g", "ln_v_b", "g_q", "g_k", "rel_bias", "g_out_a", "g_out_b", "g_ple")
LARGE = ("w_in", "w_out", "w_ple_gate", "w_ple_up")
WEIGHTS = ("g_pre", "w_in", "w_s", "b_s", "ln_v_g", "ln_v_b", "g_q", "g_k", "rel_bias", "g_out_a", "g_out_b", "w_out",
           "g_ple", "w_ple_gate", "w_ple_up")


def _pack(parts):
    flat = jnp.concatenate([parts[n].reshape(-1).astype(F32) for n in SMALL])
    rows = -(-flat.shape[0] // (8 * LANE)) * 8
    return jnp.pad(flat, (0, rows * LANE - flat.shape[0])).reshape(rows, LANE)


def _unpack(pack, like):
    flat = pack.reshape(-1)
    out, at = {}, 0
    for n in SMALL:
        size = math.prod(like[n].shape)
        out[n] = flat[at:at + size].reshape(like[n].shape)
        at += size
    return out


def kernel(x, p, g_pre, w_in, w_s, b_s, ln_v_g, ln_v_b, g_q, g_k, rel_bias, g_out_a, g_out_b, w_out, g_ple, w_ple_gate, w_ple_up, loss_target, m_g_pre, m_w_in, m_w_s, m_b_s, m_ln_v_g, m_ln_v_b, m_g_q, m_g_k, m_rel_bias, m_g_out_a, m_g_out_b, m_w_out, m_g_ple, m_w_ple_gate, m_w_ple_up, v_g_pre, v_w_in, v_w_s, v_b_s, v_ln_v_g, v_ln_v_b, v_g_q, v_g_k, v_rel_bias, v_g_out_a, v_g_out_b, v_w_out, v_g_ple, v_w_ple_gate, v_w_ple_up):
    given = dict(locals())
    weights = {n: given[n] for n in WEIGHTS}
    mom_m = {n: given["m_" + n] for n in WEIGHTS}
    mom_v = {n: given["v_" + n] for n in WEIGHTS}
    xs, ps, tgt = x[0], p[0, 0], loss_target[0]
    d = xs.shape[1]

    core = lax.axis_index("c").astype(jnp.int32).reshape(1)
    chip = (2 * lax.axis_index("x") + lax.axis_index("y")).astype(jnp.int32).reshape(1)

    win4, wout4, wgate4, wup4 = [_cast_bf16("cast_" + n, chip, weights[n][0]) for n in LARGE]

    sp = {
        "g_pre": g_pre, "w_s": w_s[0], "b_s": b_s[0], "ln_v_g": ln_v_g, "ln_v_b": ln_v_b, "g_q": g_q, "g_k": g_k,
        "rel_bias": rel_bias, "g_out_a": g_out_a, "g_out_b": g_out_b, "g_ple": g_ple,
    }
    loss_local, grad_x, (sums32, got16), small = _local_step(xs, ps, tgt, sp, win4, wout4, wgate4, wup4,
                                                             dist={"chip": chip, "core": core})
    loss = lax.psum(loss_local, MESH_AXES)

    own = dict(zip(LARGE, sums32))
    got = dict(zip(LARGE, got16))
    halves = [_total("total_" + n, chip, core, own[n], got[n]) for n in LARGE[1:]]
    my_pack = _pack(small)
    device = 2 * chip[0] + core[0]
    everyone = lax.dynamic_update_slice(jnp.zeros((N_DEV * my_pack.shape[0], LANE), F32), my_pack,
                                        (device * my_pack.shape[0], 0))
    half_in, (*joined, gathered) = _total("total_w_in", chip, core, own["w_in"], got["w_in"],
                                          comm=_Both(_JoinHalves(halves), _GatherSmall(everyone)))
    grads = dict(zip(LARGE[1:], joined))
    grads["w_in"], = _alone("join_w_in", _JoinHalves([half_in]))

    out_g, out_d, out_m, out_v = {}, {}, {}, {}
    for n in LARGE:
        results = _adamw("adamw_" + n, weights[n][0], grads[n], mom_m[n][0], mom_v[n][0])
        out_g[n], out_d[n], out_m[n], out_v[n] = [r[None] for r in results]

    pg, pd, pm, pv = _adamw_small(gathered, _pack(weights), _pack(mom_m), _pack(mom_v))
    for packed, out in ((pg, out_g), (pd, out_d), (pm, out_m), (pv, out_v)):
        out.update(_unpack(packed, weights))

    return (loss, grad_x[None], *[out_g[n] for n in WEIGHTS], *[out_d[n] for n in WEIGHTS],
            *[out_m[n] for n in WEIGHTS], *[out_v[n] for n in WEIGHTS])
```

```python
import functools
import math

import numpy as np

import jax
import jax.numpy as jnp
from jax import lax
from jax.experimental import pallas as pl
from jax.experimental.pallas import tpu as pltpu

F32 = jnp.float32
BF16 = jnp.bfloat16

HEAD_DIM = 128
CHUNK = 128
BLK = 128
DILATED = ((128, 1), (512, 4), (2048, 16))
NUM_BUCKETS = 32
MAX_DISTANCE = 2048
ATTN_GROUP = 8
EPS = 1e-6
NEG_INF = -1e30
N_CHIPS = 4
N_DEV = 8
MESH_AXES = ("x", "y", "c")

ADAM_LR = 0.001
ADAM_B1 = 0.9
ADAM_B2 = 0.999
ADAM_EPS = 1e-08
ADAM_WD = 0.01
ADAM_STEP = 10

V7X_VMEM_LIMIT = 58 * 1024 * 1024
LANE = 128
MESH = pl.DeviceIdType.MESH
ANY = pl.BlockSpec(memory_space=pl.ANY)


def _params(*sem):
    return pltpu.CompilerParams(dimension_semantics=sem or None, vmem_limit_bytes=V7X_VMEM_LIMIT)


def _tile(n, target):
    if n <= target:
        return n
    t = (target // LANE) * LANE
    while t > LANE and n % t:
        t -= LANE
    assert n % t == 0, (n, target)
    return t


def _gelu(x):
    return 0.5 * x * (1.0 + lax.erf(x * (1.0 / math.sqrt(2.0))))


def _gelu_grad(x):
    return 0.5 * (1.0 + lax.erf(x * (1.0 / math.sqrt(2.0)))) + x * jnp.exp(-0.5 * x * x) * (1.0 / math.sqrt(2.0 * math.pi))


def _silu_and_grad(x):
    s = jax.nn.sigmoid(x)
    return x * s, s * (1.0 + x * (1.0 - s))


def _dot(a, b, ta=False, tb=False):
    return lax.dot_general(a, b, (((0 if ta else 1,), (1 if tb else 0,)), ((), ())), preferred_element_type=F32)


def _colsum8(v):
    return jnp.sum(v.reshape(v.shape[0] // 8, 8, v.shape[1]), axis=0)


def _matmul(name, a, b, a_spec, b_spec, grid, acc_shape, out_shape, out_specs, ta=False, tb=False,
            extras=(), extra_specs=(), epilogue=None, comm=None, prefetch=()):
    nk = grid[2]
    n_pre = len(prefetch)
    n_extra = len(extras)
    single = not isinstance(out_shape, (tuple, list))
    outs_shape = (out_shape,) if single else tuple(out_shape)
    outs_specs = (out_specs,) if single else tuple(out_specs)
    n_out = len(outs_shape)
    c_ins = list(comm.ins) if comm else []
    c_outs = list(comm.out_shape) if comm else []
    c_scratch = list(comm.scratch) if comm else []
    n_cin, n_cout = len(c_ins), len(c_outs)

    def finish(acc, extra_refs, out_refs):
        if epilogue is None:
            out_refs[0][...] = acc.astype(out_refs[0].dtype)
        else:
            epilogue(acc, extra_refs, out_refs)

    def body(*refs):
        a_ref, b_ref, *rest = refs[n_pre:]
        extra_refs = rest[:n_extra]
        cin_refs = rest[n_extra:n_extra + n_cin]
        out_refs = rest[n_extra + n_cin:n_extra + n_cin + n_out]
        cout_refs = rest[n_extra + n_cin + n_out:n_extra + n_cin + n_out + n_cout]
        scratch_refs = rest[n_extra + n_cin + n_out + n_cout:]
        ids = [pl.program_id(ax) for ax in range(3)]
        if comm:
            sems = scratch_refs[len(scratch_refs) - len(c_scratch):]

            @pl.when((ids[0] == 0) & (ids[1] == 0) & (ids[2] == 0))
            def _():
                comm.start(cin_refs, cout_refs, sems)

        if nk == 1:
            finish(_dot(a_ref[...], b_ref[...], ta, tb), extra_refs, out_refs)
        else:
            acc_ref = scratch_refs[0]

            @pl.when(ids[2] == 0)
            def _():
                acc_ref[...] = jnp.zeros_like(acc_ref)

            acc_ref[...] += _dot(a_ref[...], b_ref[...], ta, tb)

            @pl.when(ids[2] == nk - 1)
            def _():
                finish(acc_ref[...], extra_refs, out_refs)

        if comm:
            @pl.when((ids[0] == grid[0] - 1) & (ids[1] == grid[1] - 1) & (ids[2] == nk - 1))
            def _():
                comm.finish(cin_refs, cout_refs, sems)

    scratch = ([] if nk == 1 else [pltpu.VMEM(acc_shape, F32)]) + c_scratch
    aliases = {n_pre + 2 + n_extra + i: n_out + i for i in range(n_cin)} if (comm and comm.in_place) else {}
    res = pl.pallas_call(
        body, name=name,
        grid_spec=pltpu.PrefetchScalarGridSpec(
            num_scalar_prefetch=n_pre, grid=grid,
            in_specs=[a_spec, b_spec, *extra_specs] + [ANY] * n_cin,
            out_specs=list(outs_specs) + [ANY] * n_cout, scratch_shapes=scratch),
        out_shape=list(outs_shape) + c_outs,
        input_output_aliases=aliases,
        compiler_params=_params(*(("arbitrary",) * 3 if comm else ("parallel", "parallel", "arbitrary"))),
    )(*prefetch, a, b, *extras, *c_ins)
    if comm:
        main = res[:n_out]
        return (main[0] if single else main), res[n_out:]
    return res[0] if single else res


def _shard_spec(rows, cols, per_shard, row_axis, col_axis):
    def index(i, j, k):
        g = (i, j, k)
        return (g[col_axis] // per_shard, g[row_axis], g[col_axis] % per_shard)
    return pl.BlockSpec((None, rows, cols), index)


def _rms_fwd(name, x, g):
    s, d = x.shape
    tm = _tile(s, 256)

    def body(x_ref, g_ref, y_ref, r_ref):
        xf = x_ref[...]
        r = lax.rsqrt(jnp.mean(xf * xf, axis=-1, keepdims=True) + EPS)
        y_ref[...] = (xf * r * g_ref[...]).astype(BF16)
        r_ref[...] = r

    return pl.pallas_call(
        body, name=name, grid=(s // tm,),
        in_specs=[pl.BlockSpec((tm, d), lambda i: (i, 0)), pl.BlockSpec((1, d), lambda i: (0, 0))],
        out_specs=(pl.BlockSpec((tm, d), lambda i: (i, 0)), pl.BlockSpec((tm, 1), lambda i: (i, 0))),
        out_shape=(jax.ShapeDtypeStruct((s, d), BF16), jax.ShapeDtypeStruct((s, 1), F32)),
        compiler_params=_params("parallel"),
    )(x, g)


def _rms_bwd(name, dy, x, r, g, skip, with_bf16):
    s, d = x.shape
    tm = _tile(s, 256)
    n = s // tm

    def body(dy_ref, x_ref, r_ref, g_ref, skip_ref, *outs):
        dx_ref = outs[0]
        dg_ref = outs[-2]
        acc_ref = outs[-1]
        i = pl.program_id(0)
        dyv, xv, rv = dy_ref[...], x_ref[...], r_ref[...]

        @pl.when(i == 0)
        def _():
            acc_ref[...] = jnp.zeros_like(acc_ref)

        acc_ref[...] += _colsum8(dyv * xv * rv)
        dg = dyv * g_ref[...]
        dx = skip_ref[...] + rv * (dg - xv * (rv * rv) * jnp.mean(dg * xv, axis=-1, keepdims=True))
        dx_ref[...] = dx
        if with_bf16:
            outs[1][...] = dx.astype(BF16)

        @pl.when(i == n - 1)
        def _():
            dg_ref[...] = jnp.sum(acc_ref[...], axis=0, keepdims=True)

    row = pl.BlockSpec((tm, d), lambda i: (i, 0))
    vec = pl.BlockSpec((1, d), lambda i: (0, 0))
    out_specs = [row] + ([row] if with_bf16 else []) + [vec]
    out_shape = [jax.ShapeDtypeStruct((s, d), F32)] + ([jax.ShapeDtypeStruct((s, d), BF16)] if with_bf16 else []) \
        + [jax.ShapeDtypeStruct((1, d), F32)]
    return pl.pallas_call(
        body, name=name, grid=(n,),
        in_specs=[row, row, pl.BlockSpec((tm, 1), lambda i: (i, 0)), vec, row],
        out_specs=out_specs, out_shape=out_shape, scratch_shapes=[pltpu.VMEM((8, d), F32)],
        compiler_params=_params("arbitrary"),
    )(dy, x, r, g, skip)


def _causal(w):
    t = lax.broadcasted_iota(jnp.int32, w.shape, 0)
    s_ = lax.broadcasted_iota(jnp.int32, w.shape, 1)
    return jnp.where(t >= s_, w, 0.0)


def _gmlp_fwd(proj, w_s, b_st, ln_g, ln_b, g_out, d_model):
    s = proj.shape[0]
    wa = d_model // 2
    groups = wa // HEAD_DIM
    tm = _tile(s, 256)
    n_chunks = tm // CHUNK

    def body(au_ref, av_ref, az_ref, ws_ref, bst_ref, lng_ref, lnb_ref, go_ref,
             y_ref, z_ref, vn_ref, mu_ref, rs_ref, ra_ref):
        gv = _gelu(av_ref[...])
        mu = jnp.mean(gv, axis=-1, keepdims=True)
        xc = gv - mu
        rs = lax.rsqrt(jnp.mean(xc * xc, axis=-1, keepdims=True) + EPS)
        vn = (xc * rs * lng_ref[...] + lnb_ref[...]).astype(BF16)
        vn_ref[...] = vn
        mu_ref[...] = mu
        rs_ref[...] = rs
        for g in range(groups):
            wm = _causal(ws_ref[g]).astype(BF16)
            cols = slice(g * HEAD_DIM, (g + 1) * HEAD_DIM)
            for ch in range(n_chunks):
                rows = slice(ch * CHUNK, (ch + 1) * CHUNK)
                z_ref[rows, cols] = _dot(wm, vn_ref[rows, cols]) + bst_ref[:, g:g + 1]
        ya = _gelu(au_ref[...]) * z_ref[...]
        ra = lax.rsqrt(jnp.mean(ya * ya, axis=-1, keepdims=True) + EPS)
        ra_ref[...] = ra
        sz, _ = _silu_and_grad(az_ref[...])
        y_ref[...] = (ya * ra * go_ref[...] * sz).astype(BF16)

    def col(j):
        return pl.BlockSpec((tm, wa), lambda i: (i, j))
    vec = pl.BlockSpec((1, wa), lambda i: (0, 0))
    stat = pl.BlockSpec((tm, 1), lambda i: (i, 0))
    return pl.pallas_call(
        body, name="gmlp_fwd", grid=(s // tm,),
        in_specs=[col(0), col(1), col(2),
                  pl.BlockSpec((groups, CHUNK, CHUNK), lambda i: (0, 0, 0)),
                  pl.BlockSpec((CHUNK, groups), lambda i: (0, 0)), vec, vec, vec],
        out_specs=(col(0), col(0), col(0), stat, stat, stat),
        out_shape=(jax.ShapeDtypeStruct((s, d_model), BF16), jax.ShapeDtypeStruct((s, wa), F32),
                   jax.ShapeDtypeStruct((s, wa), BF16), jax.ShapeDtypeStruct((s, 1), F32),
                   jax.ShapeDtypeStruct((s, 1), F32), jax.ShapeDtypeStruct((s, 1), F32)),
        compiler_params=_params("parallel"),
    )(proj, proj, proj, w_s, b_st, ln_g, ln_b, g_out)


def _gmlp_bwd(dproj, dy, proj, z, vn, mu, rs, ra, w_s, ln_g, g_out, d_model):
    s = proj.shape[0]
    wa = d_model // 2
    groups = wa // HEAD_DIM
    tm = _tile(s, 256)
    n_chunks = tm // CHUNK
    n = s // tm

    def causal_stack(w):
        t = lax.broadcasted_iota(jnp.int32, w.shape, 1)
        s_ = lax.broadcasted_iota(jnp.int32, w.shape, 2)
        return jnp.where(t >= s_, w, 0.0)

    def body(dproj_in, dy_ref, au_ref, av_ref, az_ref, z_ref, vn_ref, mu_ref, rs_ref, ra_ref, ws_ref, lng_ref, go_ref,
             dp_ref, gws_ref, dzs_ref, glg_ref, glb_ref, ggo_ref,
             dz_s, dvn_s, acc_lg, acc_lb, acc_go):
        del dproj_in
        i = pl.program_id(0)

        @pl.when(i == 0)
        def _():
            gws_ref[...] = jnp.zeros_like(gws_ref)
            dzs_ref[...] = jnp.zeros_like(dzs_ref)
            acc_lg[...] = jnp.zeros_like(acc_lg)
            acc_lb[...] = jnp.zeros_like(acc_lb)
            acc_go[...] = jnp.zeros_like(acc_go)

        au, az, zv, rav = au_ref[...], az_ref[...], z_ref[...], ra_ref[...]
        u = _gelu(au)
        ya = u * zv
        sz, dsz = _silu_and_grad(az)
        dyv = dy_ref[...]
        dp_ref[:, 2 * wa:3 * wa] = (dyv * (ya * rav * go_ref[...]) * dsz).astype(BF16)
        dn = dyv * sz
        acc_go[...] += _colsum8(dn * ya * rav)
        dyg = dn * go_ref[...]
        dya = rav * (dyg - ya * (rav * rav) * jnp.mean(dyg * ya, axis=-1, keepdims=True))
        dp_ref[:, 0:wa] = (dya * zv * _gelu_grad(au)).astype(BF16)
        dz_s[...] = dya * u
        for ch in range(n_chunks):
            dzs_ref[...] += dz_s[ch * CHUNK:(ch + 1) * CHUNK, :]
        for g in range(groups):
            wm = _causal(ws_ref[g]).astype(BF16)
            cols = slice(g * HEAD_DIM, (g + 1) * HEAD_DIM)
            for ch in range(n_chunks):
                rows = slice(ch * CHUNK, (ch + 1) * CHUNK)
                dzb = dz_s[rows, cols].astype(BF16)
                gws_ref[g] += _dot(dzb, vn_ref[rows, cols], tb=True)
                dvn_s[rows, cols] = _dot(wm, dzb, ta=True)
        av = av_ref[...]
        xh = (_gelu(av) - mu_ref[...]) * rs_ref[...]
        dvn = dvn_s[...]
        acc_lb[...] += _colsum8(dvn)
        acc_lg[...] += _colsum8(dvn * xh)
        dxh = dvn * lng_ref[...]
        dgv = rs_ref[...] * (dxh - jnp.mean(dxh, axis=-1, keepdims=True) - xh * jnp.mean(dxh * xh, axis=-1, keepdims=True))
        dp_ref[:, wa:2 * wa] = (dgv * _gelu_grad(av)).astype(BF16)

        @pl.when(i == n - 1)
        def _():
            gws_ref[...] = causal_stack(gws_ref[...])
            glg_ref[...] = jnp.sum(acc_lg[...], axis=0, keepdims=True)
            glb_ref[...] = jnp.sum(acc_lb[...], axis=0, keepdims=True)
            ggo_ref[...] = jnp.sum(acc_go[...], axis=0, keepdims=True)

    def col(j):
        return pl.BlockSpec((tm, wa), lambda i: (i, j))
    vec = pl.BlockSpec((1, wa), lambda i: (0, 0))
    stat = pl.BlockSpec((tm, 1), lambda i: (i, 0))
    ws_spec = pl.BlockSpec((groups, CHUNK, CHUNK), lambda i: (0, 0, 0))
    d_in = dproj.shape[1]
    return pl.pallas_call(
        body, name="gmlp_bwd", grid=(n,),
        in_specs=[ANY, col(0), col(0), col(1), col(2), col(0), col(0), stat, stat, stat, ws_spec, vec, vec],
        out_specs=(pl.BlockSpec((tm, 3 * wa), lambda i: (i, 0)), ws_spec,
                   pl.BlockSpec((CHUNK, wa), lambda i: (0, 0)), vec, vec, vec),
        out_shape=(jax.ShapeDtypeStruct((s, d_in), BF16),
                   jax.ShapeDtypeStruct((groups, CHUNK, CHUNK), F32), jax.ShapeDtypeStruct((CHUNK, wa), F32))
        + (jax.ShapeDtypeStruct((1, wa), F32),) * 3,
        scratch_shapes=[pltpu.VMEM((tm, wa), F32), pltpu.VMEM((tm, wa), F32)] + [pltpu.VMEM((8, wa), F32)] * 3,
        input_output_aliases={0: 0},
        compiler_params=_params("arbitrary"),
    )(dproj, dy, proj, proj, proj, z, vn, mu, rs, ra, w_s, ln_g, g_out)


def _gate_fwd(y_in, yb, proj, g_out, d_model):
    s = proj.shape[0]
    wa = d_model // 2
    tm = _tile(s, 256)

    def body(y_any, yb_ref, bz_ref, go_ref, y_ref, rb_ref):
        del y_any
        ybv = yb_ref[...]
        rb = lax.rsqrt(jnp.mean(ybv * ybv, axis=-1, keepdims=True) + EPS)
        rb_ref[...] = rb
        sz, _ = _silu_and_grad(bz_ref[...])
        y_ref[...] = (ybv * rb * go_ref[...] * sz).astype(BF16)

    return pl.pallas_call(
        body, name="gate_b_fwd", grid=(s // tm,),
        in_specs=[ANY, pl.BlockSpec((tm, wa), lambda i: (i, 0)), pl.BlockSpec((tm, wa), lambda i: (i, 6)),
                  pl.BlockSpec((1, wa), lambda i: (0, 0))],
        out_specs=(pl.BlockSpec((tm, wa), lambda i: (i, 1)), pl.BlockSpec((tm, 1), lambda i: (i, 0))),
        out_shape=(jax.ShapeDtypeStruct((s, d_model), BF16), jax.ShapeDtypeStruct((s, 1), F32)),
        input_output_aliases={0: 0},
        compiler_params=_params("parallel"),
    )(y_in, yb, proj, g_out)


def _gate_bwd(dy, yb, lse, proj, rb, g_out, d_model):
    s, d_in = proj.shape
    wa = d_model // 2
    heads = wa // HEAD_DIM
    tm = _tile(s, 256)
    n = s // tm

    def body(dy_ref, yb_ref, lse_ref, bz_ref, rb_ref, go_ref, dp_ref, do_ref, st_ref, ggo_ref, acc):
        i = pl.program_id(0)

        @pl.when(i == 0)
        def _():
            acc[...] = jnp.zeros_like(acc)

        dyv, ybv, rbv = dy_ref[...], yb_ref[...], rb_ref[...]
        sz, dsz = _silu_and_grad(bz_ref[...])
        dp_ref[...] = (dyv * (ybv * rbv * go_ref[...]) * dsz).astype(BF16)
        dn = dyv * sz
        acc[...] += _colsum8(dn * ybv * rbv)
        dyg = dn * go_ref[...]
        do = rbv * (dyg - ybv * (rbv * rbv) * jnp.mean(dyg * ybv, axis=-1, keepdims=True))
        do_ref[...] = do
        prod = do * ybv
        first = lax.broadcasted_iota(jnp.int32, (tm, 2), 1) == 0
        for h in range(heads):
            delta = jnp.sum(prod[:, h * HEAD_DIM:(h + 1) * HEAD_DIM], axis=-1, keepdims=True)
            st_ref[h] = jnp.where(first, lse_ref[h], delta)

        @pl.when(i == n - 1)
        def _():
            ggo_ref[...] = jnp.sum(acc[...], axis=0, keepdims=True)

    vec = pl.BlockSpec((1, wa), lambda i: (0, 0))
    return pl.pallas_call(
        body, name="gate_b_bwd", grid=(n,),
        in_specs=[pl.BlockSpec((tm, wa), lambda i: (i, 1)), pl.BlockSpec((tm, wa), lambda i: (i, 0)),
                  pl.BlockSpec((heads, tm, 1), lambda i: (0, i, 0)),
                  pl.BlockSpec((tm, wa), lambda i: (i, 6)), pl.BlockSpec((tm, 1), lambda i: (i, 0)), vec],
        out_specs=(pl.BlockSpec((tm, wa), lambda i: (i, 6)), pl.BlockSpec((tm, wa), lambda i: (i, 0)),
                   pl.BlockSpec((heads, tm, 2), lambda i: (0, i, 0)), vec),
        out_shape=(jax.ShapeDtypeStruct((s, d_in), BF16), jax.ShapeDtypeStruct((s, wa), F32),
                   jax.ShapeDtypeStruct((heads, s, 2), F32), jax.ShapeDtypeStruct((1, wa), F32)),
        scratch_shapes=[pltpu.VMEM((8, wa), F32)],
        compiler_params=_params("arbitrary"),
    )(dy, yb, lse, proj, rb, g_out)


def _bucket_tables():
    qi = BLK + np.arange(BLK)
    kj = np.arange(2 * BLK)
    delta = qi[:, None] - kj[None, :]
    max_exact = NUM_BUCKETS // 2
    tabs = []
    for window, dil in DILATED:
        band = (delta >= 0) & (delta <= window // dil)
        dist = np.clip(delta, 0, None) * dil
        d = np.maximum(dist, 1).astype(np.float32)
        large = max_exact + (np.log(d / np.float32(max_exact)) / np.float32(math.log(MAX_DISTANCE / max_exact))
                             * np.float32(NUM_BUCKETS - max_exact)).astype(np.int32)
        large = np.minimum(large, NUM_BUCKETS - 1)
        tabs.append(np.where(band, np.where(dist < max_exact, dist, large), -1).astype(np.int32))
    return np.stack(tabs)


def _bias_tiles(tab_ref, rb_ref, h, bias_s):
    col = lax.broadcasted_iota(jnp.int32, (BLK, 2 * BLK), 1)
    for i in range(len(DILATED)):
        t = tab_ref[i]
        bias = jnp.zeros(t.shape, F32)
        for b in range(NUM_BUCKETS):
            bias = jnp.where(t == b, rb_ref[b, h], bias)
        bias = jnp.where(t >= 0, bias, NEG_INF)
        bias_s[2 * i] = jnp.where(col >= BLK, bias, NEG_INF)
        bias_s[2 * i + 1] = bias


def _block_rows(b, n_blocks, dil):
    nb = n_blocks // dil
    r = b // nb
    n = b % nb
    start = r + dil * BLK * n
    if dil == 1:
        return pl.ds(pl.multiple_of(start, BLK), BLK), n
    return pl.ds(start, BLK, stride=dil), n


def _sub_block(b, rows=BLK, pad=0):
    return pl.ds(pl.multiple_of(b * BLK + pad, BLK), rows)


def _regroup(b, n_blocks, small, big):
    ratio = big // small
    piece = BLK // ratio
    nb_small, nb_big = n_blocks // small, n_blocks // big
    r, n = b // nb_small, b % nb_small
    for q in range(ratio):
        block = (r + small * q) * nb_big + n // ratio
        yield q, ratio, piece, pl.multiple_of(block * BLK + piece * (n % ratio), piece)


def _rms_rows(x, gain):
    return x * lax.rsqrt(jnp.mean(x * x, axis=-1, keepdims=True) + EPS) * gain


def _attn_fwd(proj, g_q, g_k, rel_bias, d_model, comm=None):
    s = proj.shape[0]
    heads = d_model // 2 // HEAD_DIM
    n_blocks = s // BLK
    scale = HEAD_DIM ** -0.5
    tables = jnp.asarray(_bucket_tables())
    c_ins = list(comm.ins) if comm else []
    c_outs = list(comm.out_shape) if comm else []
    c_scratch = list(comm.scratch) if comm else []

    def body(q_ref, k_ref, v_ref, gq_ref, gk_ref, rb_ref, tab_ref, *rest):
        cin_refs, rest = rest[:len(c_ins)], rest[len(c_ins):]
        yb_ref, lse_ref = rest[:2]
        cout_refs, rest = rest[2:2 + len(c_outs)], rest[2 + len(c_outs):]
        qs, ks, vs, qs_w, ks_w, vs_w, tmp, m_s, l_s, acc_s, bias_s, sc_s, p_s = rest[:13]
        sems = rest[13:]
        h = pl.program_id(0)
        if comm:
            @pl.when(h == 0)
            def _():
                comm.start(cin_refs, cout_refs, sems)

        _bias_tiles(tab_ref, rb_ref, h, bias_s)
        gq = gq_ref[...] * scale
        gk = gk_ref[...]
        for buf in (ks, vs, ks_w, vs_w):
            buf[0:BLK, :] = jnp.zeros((BLK, HEAD_DIM), BF16)

        for i, (_, dil) in enumerate(DILATED):
            def prepare(b, carry, i=i, dil=dil):
                rows, _ = _block_rows(b, n_blocks, dil)
                vals = (_rms_rows(q_ref[rows, :], gq), _rms_rows(k_ref[rows, :], gk), v_ref[rows, :])
                for val, buf, pad in zip(vals, (qs, ks, vs), (0, BLK, BLK)):
                    buf[_sub_block(b, pad=pad), :] = val.astype(BF16)
                if i == 1:
                    slot = b % 2
                    for a, val in enumerate(vals):
                        tmp[slot, a] = val
                    for q, ratio, piece, dst in _regroup(b, n_blocks, dil, DILATED[2][1]):
                        for a, (buf, pad) in enumerate(((qs_w, 0), (ks_w, BLK), (vs_w, BLK))):
                            buf[pl.ds(dst + pad, piece), :] = tmp[slot, a, pl.ds(q, piece, stride=ratio), :].astype(BF16)
                return carry
            if i < 2:
                lax.fori_loop(0, n_blocks, prepare, 0, unroll=8)
            qs, ks, vs = (qs, ks, vs) if i < 2 else (qs_w, ks_w, vs_w)

            def group(g, carry, i=i, dil=dil, qs=qs, ks=ks, vs=vs):
                blocks = [g * ATTN_GROUP + t for t in range(ATTN_GROUP)]
                for t, b in enumerate(blocks):
                    sc_s[t] = _dot(qs[_sub_block(b), :], ks[_sub_block(b, 2 * BLK), :], tb=True)
                for t, b in enumerate(blocks):
                    rows, n = _block_rows(b, n_blocks, dil)
                    sc = sc_s[t] + bias_s[2 * i + jnp.minimum(n, 1)]
                    m_i = jnp.max(sc, axis=-1, keepdims=True)
                    pr = jnp.exp(sc - m_i)
                    l_i = jnp.sum(pr, axis=-1, keepdims=True)
                    p_s[t] = pr.astype(BF16)
                    m_s[new, rows, :] = m_i
                    l_s[new, rows, :] = l_i
                for t, b in enumerate(blocks):
                    rows, _ = _block_rows(b, n_blocks, dil)
                    acc_s[new, rows, :] = _dot(p_s[t], vs[_sub_block(b, 2 * BLK), :])
                return carry
            new = min(i, 1)
            lax.fori_loop(0, n_blocks // ATTN_GROUP, group, 0)
            if i > 0:
                m = jnp.maximum(m_s[0], m_s[1])
                w_old, w_new = jnp.exp(m_s[0] - m), jnp.exp(m_s[1] - m)
                m_s[0] = m
                l_s[0] = l_s[0] * w_old + l_s[1] * w_new
                acc_s[0] = acc_s[0] * w_old + acc_s[1] * w_new

        l = l_s[0]
        yb_ref[...] = acc_s[0] / l
        lse_ref[...] = m_s[0] + jnp.log(l)
        if comm:
            @pl.when(h == heads - 1)
            def _():
                comm.finish(cin_refs, cout_refs, sems)

    def head_col(off):
        return pl.BlockSpec((s, HEAD_DIM), lambda h: (0, off * heads + h))
    vec = pl.BlockSpec((1, HEAD_DIM), lambda h: (0, 0))
    res = pl.pallas_call(
        body, name="attn_fwd", grid=(heads,),
        in_specs=[head_col(3), head_col(4), head_col(5), vec, vec,
                  pl.BlockSpec(memory_space=pltpu.SMEM),
                  pl.BlockSpec((len(DILATED), BLK, 2 * BLK), lambda h: (0, 0, 0))] + [ANY] * len(c_ins),
        out_specs=[pl.BlockSpec((s, HEAD_DIM), lambda h: (0, h)), pl.BlockSpec((None, s, 1), lambda h: (h, 0, 0))]
        + [ANY] * len(c_outs),
        out_shape=[jax.ShapeDtypeStruct((s, heads * HEAD_DIM), F32), jax.ShapeDtypeStruct((heads, s, 1), F32)] + c_outs,
        scratch_shapes=[pltpu.VMEM((s, HEAD_DIM), BF16), pltpu.VMEM((s + BLK, HEAD_DIM), BF16),
                        pltpu.VMEM((s + BLK, HEAD_DIM), BF16),
                        pltpu.VMEM((s, HEAD_DIM), BF16), pltpu.VMEM((s + BLK, HEAD_DIM), BF16),
                        pltpu.VMEM((s + BLK, HEAD_DIM), BF16), pltpu.VMEM((2, 3, BLK, HEAD_DIM), F32),
                        pltpu.VMEM((2, s, 1), F32), pltpu.VMEM((2, s, 1), F32), pltpu.VMEM((2, s, HEAD_DIM), F32),
                        pltpu.VMEM((2 * len(DILATED), BLK, 2 * BLK), F32),
                        pltpu.VMEM((ATTN_GROUP, BLK, 2 * BLK), F32), pltpu.VMEM((ATTN_GROUP, BLK, 2 * BLK), BF16)] + c_scratch,
        input_output_aliases={7 + i: 2 + i for i in range(len(c_ins))} if (comm and comm.in_place) else {},
        compiler_params=_params("arbitrary"),
    )(proj, proj, proj, g_q, g_k, rel_bias, tables, *c_ins)
    return res[0], res[1], res[2:]


def _attn_bwd(dproj, d_o, stats, proj, g_q, g_k, rel_bias, d_model, comm=None):
    s, d_in = proj.shape
    heads = d_model // 2 // HEAD_DIM
    n_blocks = s // BLK
    scale = HEAD_DIM ** -0.5
    tables = jnp.asarray(_bucket_tables())
    n_dil = len(DILATED)
    c_ins = list(comm.ins) if comm else []
    c_outs = list(comm.out_shape) if comm else []
    c_scratch = list(comm.scratch) if comm else []

    def body(dp_any, q_ref, k_ref, v_ref, do_ref, st_ref, gq_ref, gk_ref, rb_ref, tab_ref, *rest):
        cin_refs, rest = rest[:len(c_ins)], rest[len(c_ins):]
        dp_out, ggq_ref, ggk_ref, db_ref = rest[:4]
        cout_refs, rest = rest[4:4 + len(c_outs)], rest[4 + len(c_outs):]
        (qs, ks, vs, dos, st_s, qs_w, ks_w, vs_w, dos_w, st_w, tmp, tmp_st,
         dqn, dkn, dvv, dq_u, dk_u, dv_u, bias_s, dbias_s, sc_s, dp_s, p_s, ds_s, sems) = rest[:25]
        comm_sems = rest[25:]
        del dp_any
        h = pl.program_id(0)
        if comm:
            @pl.when(h == 0)
            def _():
                comm.start(cin_refs, cout_refs, comm_sems)


        @pl.when(h == 0)
        def _():
            ggq_ref[...] = jnp.zeros_like(ggq_ref)
            ggk_ref[...] = jnp.zeros_like(ggk_ref)

        dbias_s[...] = jnp.zeros_like(dbias_s)
        _bias_tiles(tab_ref, rb_ref, h, bias_s)
        gq = gq_ref[...] * scale
        gk = gk_ref[...]
        for buf in (ks, vs, ks_w, vs_w):
            buf[0:BLK, :] = jnp.zeros((BLK, HEAD_DIM), BF16)

        for i, (_, dil) in enumerate(DILATED):
            def prepare(b, carry, i=i, dil=dil):
                rows, _ = _block_rows(b, n_blocks, dil)
                vals = (_rms_rows(q_ref[rows, :], gq), _rms_rows(k_ref[rows, :], gk), v_ref[rows, :], do_ref[rows, :])
                for val, buf, pad in zip(vals, (qs, ks, vs, dos), (0, BLK, BLK, 0)):
                    buf[_sub_block(b, pad=pad), :] = val.astype(BF16)
                st = st_ref[rows, :]
                st_s[_sub_block(b), :] = st
                if i == 1:
                    slot = b % 2
                    for a, val in enumerate(vals):
                        tmp[slot, a] = val
                    tmp_st[slot] = st
                    for q, ratio, piece, dst in _regroup(b, n_blocks, dil, DILATED[2][1]):
                        for a, (buf, pad) in enumerate(((qs_w, 0), (ks_w, BLK), (vs_w, BLK), (dos_w, 0))):
                            buf[pl.ds(dst + pad, piece), :] = tmp[slot, a, pl.ds(q, piece, stride=ratio), :].astype(BF16)
                        st_w[pl.ds(dst, piece), :] = tmp_st[slot, pl.ds(q, piece, stride=ratio), :]
                return carry
            if i < 2:
                lax.fori_loop(0, n_blocks, prepare, 0, unroll=8)
            dk_u[...] = jnp.zeros_like(dk_u)
            dv_u[...] = jnp.zeros_like(dv_u)
            operands = (qs, ks, vs, dos, st_s) if i < 2 else (qs_w, ks_w, vs_w, dos_w, st_w)

            def group(g, carry, i=i, dil=dil, operands=operands):
                qs, ks, vs, dos, st_s = operands
                blocks = [g * ATTN_GROUP + t for t in range(ATTN_GROUP)]
                for t, b in enumerate(blocks):
                    sc_s[t] = _dot(qs[_sub_block(b), :], ks[_sub_block(b, 2 * BLK), :], tb=True)
                    dp_s[t] = _dot(dos[_sub_block(b), :], vs[_sub_block(b, 2 * BLK), :], tb=True)
                for t, b in enumerate(blocks):
                    _, n = _block_rows(b, n_blocks, dil)
                    st = st_s[_sub_block(b), :]
                    pr = jnp.exp(sc_s[t] + bias_s[2 * i + jnp.minimum(n, 1)] - st[:, 0:1])
                    ds = pr * (dp_s[t] - st[:, 1:2])
                    dbias_s[i] += ds
                    p_s[t] = pr.astype(BF16)
                    ds_s[t] = ds.astype(BF16)
                for t, b in enumerate(blocks):
                    dq_u[_sub_block(b), :] = _dot(ds_s[t], ks[_sub_block(b, 2 * BLK), :])
                    dk_u[_sub_block(b, 2 * BLK), :] += _dot(ds_s[t], qs[_sub_block(b), :], ta=True)
                    dv_u[_sub_block(b, 2 * BLK), :] += _dot(p_s[t], dos[_sub_block(b), :], ta=True)
                return carry
            lax.fori_loop(0, n_blocks // ATTN_GROUP, group, 0)

            def scatter(b, carry, i=i, dil=dil):
                rows, _ = _block_rows(b, n_blocks, dil)
                for acc, part, pad in ((dqn, dq_u, 0), (dkn, dk_u, BLK), (dvv, dv_u, BLK)):
                    val = part[_sub_block(b, pad=pad), :]
                    acc[rows, :] = val if i == 0 else acc[rows, :] + val
                return carry
            lax.fori_loop(0, n_blocks, scatter, 0, unroll=8)

        q = q_ref[...]
        rq = lax.rsqrt(jnp.mean(q * q, axis=-1, keepdims=True) + EPS)
        k = k_ref[...]
        rk = lax.rsqrt(jnp.mean(k * k, axis=-1, keepdims=True) + EPS)
        dq_n = dqn[...]
        ggq_ref[...] += jnp.sum(dq_n * (q * rq) * scale, axis=0, keepdims=True)
        dg = dq_n * gq_ref[...] * scale
        qs[...] = (rq * (dg - q * (rq * rq) * jnp.mean(dg * q, axis=-1, keepdims=True))).astype(BF16)
        dk_n = dkn[...]
        ggk_ref[...] += jnp.sum(dk_n * (k * rk), axis=0, keepdims=True)
        dg = dk_n * gk_ref[...]
        dos[...] = (rk * (dg - k * (rk * rk) * jnp.mean(dg * k, axis=-1, keepdims=True))).astype(BF16)
        vs[BLK:, :] = dvv[...].astype(BF16)
        copies = [pltpu.make_async_copy(src, dp_out.at[:, pl.ds(pl.multiple_of(((3 + j) * heads + h) * HEAD_DIM, HEAD_DIM), HEAD_DIM)],
                                        sems.at[j]) for j, src in enumerate((qs, dos, vs.at[pl.ds(BLK, s)]))]
        for cp in copies:
            cp.start()
        for b in range(NUM_BUCKETS):
            tot = jnp.zeros((BLK, 2 * BLK), F32)
            for i in range(n_dil):
                tot = tot + jnp.where(tab_ref[i] == b, dbias_s[i], 0.0)
            db_ref[b:b + 1, :] = jnp.full((1, LANE), jnp.sum(tot), F32)
        for cp in copies:
            cp.wait()
        if comm:
            @pl.when(h == heads - 1)
            def _():
                comm.finish(cin_refs, cout_refs, comm_sems)

    def head_col(off):
        return pl.BlockSpec((s, HEAD_DIM), lambda h: (0, off * heads + h))
    own_col = pl.BlockSpec((s, HEAD_DIM), lambda h: (0, h))
    vec = pl.BlockSpec((1, HEAD_DIM), lambda h: (0, 0))
    big = pltpu.VMEM((s, HEAD_DIM), F32)
    padded32 = pltpu.VMEM((s + BLK, HEAD_DIM), F32)
    padded16 = pltpu.VMEM((s + BLK, HEAD_DIM), BF16)
    res = pl.pallas_call(
        body, name="attn_bwd", grid=(heads,),
        in_specs=[ANY, head_col(3), head_col(4), head_col(5), own_col,
                  pl.BlockSpec((None, s, 2), lambda h: (h, 0, 0)), vec, vec,
                  pl.BlockSpec(memory_space=pltpu.SMEM),
                  pl.BlockSpec((n_dil, BLK, 2 * BLK), lambda h: (0, 0, 0))] + [ANY] * len(c_ins),
        out_specs=[ANY, vec, vec, pl.BlockSpec((None, NUM_BUCKETS, LANE), lambda h: (h, 0, 0))] + [ANY] * len(c_outs),
        out_shape=[jax.ShapeDtypeStruct((s, d_in), BF16), jax.ShapeDtypeStruct((1, HEAD_DIM), F32),
                   jax.ShapeDtypeStruct((1, HEAD_DIM), F32), jax.ShapeDtypeStruct((heads, NUM_BUCKETS, LANE), F32)]
        + c_outs,
        scratch_shapes=[pltpu.VMEM((s, HEAD_DIM), BF16), padded16, padded16, pltpu.VMEM((s, HEAD_DIM), BF16),
                        pltpu.VMEM((s, 2), F32),
                        pltpu.VMEM((s, HEAD_DIM), BF16), padded16, padded16, pltpu.VMEM((s, HEAD_DIM), BF16),
                        pltpu.VMEM((s, 2), F32),
                        pltpu.VMEM((2, 4, BLK, HEAD_DIM), F32), pltpu.VMEM((2, BLK, 2), F32),
                        big, big, big, big, padded32, padded32,
                        pltpu.VMEM((2 * n_dil, BLK, 2 * BLK), F32), pltpu.VMEM((n_dil, BLK, 2 * BLK), F32),
                        pltpu.VMEM((ATTN_GROUP, BLK, 2 * BLK), F32), pltpu.VMEM((ATTN_GROUP, BLK, 2 * BLK), F32),
                        pltpu.VMEM((ATTN_GROUP, BLK, 2 * BLK), BF16), pltpu.VMEM((ATTN_GROUP, BLK, 2 * BLK), BF16),
                        pltpu.SemaphoreType.DMA((3,))] + c_scratch,
        input_output_aliases={0: 0},
        compiler_params=_params("arbitrary"),
    )(dproj, proj, proj, proj, d_o, stats, g_q, g_k, rel_bias, tables, *c_ins)
    return res[0], res[1], res[2], res[3], res[4:]


def _local_step(xs, ps, tgt, sp, win4, wout4, wgate4, wup4, dist=None):
    s, d = xs.shape
    wa = d // 2
    heads = wa // HEAD_DIM
    d_in = 7 * wa
    c4 = d_in // N_CHIPS
    dple = ps.shape[1]

    tm, tn = _tile(s, 1024), _tile(d, 1024)
    tn_in = _tile(c4, 1024)
    per_in = c4 // tn_in
    tn_up = _tile(d // N_CHIPS, 512)
    per_up = (d // N_CHIPS) // tn_up
    gm, gn = s // tm, d // tn

    hn, rx = _rms_fwd("rms_pre", xs, sp["g_pre"])
    if dist:
        proj, win4 = _mm_in_gather(dist["chip"], hn, win4)
    else:
        proj = _matmul(
            "mm_in", hn, win4, pl.BlockSpec((tm, d), lambda i, j, k: (i, 0)), _shard_spec(d, tn_in, per_in, 2, 1),
            (gm, d_in // tn_in, 1), None, jax.ShapeDtypeStruct((s, d_in), F32),
            pl.BlockSpec((tm, tn_in), lambda i, j, k: (i, j)))
    b_st = sp["b_s"].T
    y, z, vn, mu_v, rs_v, ra = _gmlp_fwd(proj, sp["w_s"], b_st, sp["ln_v_g"], sp["ln_v_b"], sp["g_out_a"], d)
    yb, lse, gathered = _attn_fwd(proj, sp["g_q"], sp["g_k"], sp["rel_bias"], d,
                                  comm=_Gather([wout4, wgate4, wup4]) if dist else None)
    if dist:
        wout4, wgate4, wup4 = gathered
    wout, wgate = wout4.reshape(d, d), wgate4.reshape(d, d)
    y, rb = _gate_fwd(y, yb, proj, sp["g_out_b"], d)

    def residual(acc, extra, outs):
        outs[0][...] = extra[0][...] + acc

    tile = pl.BlockSpec((tm, tn), lambda i, j, k: (i, j))
    tile_up = pl.BlockSpec((tm, tn_up), lambda i, j, k: (i, j))
    h = _matmul(
        "mm_out", y, wout, pl.BlockSpec((tm, d), lambda i, j, k: (i, 0)), pl.BlockSpec((d, tn_up), lambda i, j, k: (0, j)),
        (gm, d // tn_up, 1), None, jax.ShapeDtypeStruct((s, d), F32), tile_up, extras=(xs,), extra_specs=(tile_up,),
        epilogue=residual)
    hp, rh = _rms_fwd("rms_ple", h, sp["g_ple"])
    p16 = ps.astype(BF16)

    def head(acc, extra, outs):
        p_ref, wup_ref, h_ref, t_ref = extra
        dout_ref, dgl_ref, dup_ref, loss_ref = outs
        up = _dot(p_ref[...], wup_ref[...])
        gate = jax.nn.sigmoid(acc)
        err = h_ref[...] + gate * up - t_ref[...]
        sq = _colsum8(err * err)
        part = sq[:, 0:LANE]
        for c in range(1, sq.shape[1] // LANE):
            part = part + sq[:, c * LANE:(c + 1) * LANE]
        loss_ref[...] = part
        dout = err * (1.0 / d)
        dout_ref[...] = dout
        dup_ref[...] = (dout * gate).astype(BF16)
        dgl_ref[...] = (dout * up * gate * (1.0 - gate)).astype(BF16)

    tile_up = pl.BlockSpec((tm, tn_up), lambda i, j, k: (i, j))
    dout, dglin, dup, loss_parts = _matmul(
        "mm_gate_loss", hp, wgate, pl.BlockSpec((tm, d), lambda i, j, k: (i, 0)), pl.BlockSpec((d, tn_up), lambda i, j, k: (0, j)),
        (gm, d // tn_up, 1), None,
        (jax.ShapeDtypeStruct((s, d), F32), jax.ShapeDtypeStruct((s, d), BF16), jax.ShapeDtypeStruct((s, d), BF16),
         jax.ShapeDtypeStruct((gm * 8, (d // tn_up) * LANE), F32)),
        (tile_up, tile_up, tile_up, pl.BlockSpec((8, LANE), lambda i, j, k: (i, j))),
        extras=(p16, wup4, h, tgt),
        extra_specs=(pl.BlockSpec((tm, dple), lambda i, j, k: (i, 0)), _shard_spec(dple, tn_up, per_up, 2, 1),
                     tile_up, tile_up),
        epilogue=head)
    loss = 0.5 * jnp.sum(loss_parts) * (1.0 / d)

    dhp = _matmul(
        "mm_dhp", dglin, wgate, pl.BlockSpec((tm, d), lambda i, j, k: (i, 0)), pl.BlockSpec((tn, d), lambda i, j, k: (j, 0)),
        (gm, gn, 1), None, jax.ShapeDtypeStruct((s, d), F32), tile, tb=True)
    dh, dh16, g_g_ple = _rms_bwd("rms_ple_bwd", dhp, h, rh, sp["g_ple"], dout, True)
    tmw = _tile(d, 1024)
    g_wout = _matmul(
        "mm_gwout", y, dh16, pl.BlockSpec((s, tmw), lambda i, j, k: (0, i)), pl.BlockSpec((s, tn), lambda i, j, k: (0, j)),
        (d // tmw, gn, 1), None, jax.ShapeDtypeStruct((d, d), F32), pl.BlockSpec((tmw, tn), lambda i, j, k: (i, j)), ta=True)
    g_wout4 = g_wout.reshape(N_CHIPS, d // N_CHIPS, d)
    g_wgate = _matmul(
        "mm_gwgate", hp, dglin, pl.BlockSpec((s, tmw), lambda i, j, k: (0, i)), pl.BlockSpec((s, tn), lambda i, j, k: (0, j)),
        (d // tmw, gn, 1), None, jax.ShapeDtypeStruct((d, d), F32), pl.BlockSpec((tmw, tn), lambda i, j, k: (i, j)), ta=True,
        comm=_SwapHalves([g_wout4]) if dist else None)
    if dist:
        g_wgate, swapped_out = g_wgate
    g_wgate4 = g_wgate.reshape(N_CHIPS, d // N_CHIPS, d)
    g_wup = _matmul(
        "mm_gwup", p16, dup, pl.BlockSpec((s, dple), lambda i, j, k: (0, 0)), pl.BlockSpec((s, tn_up), lambda i, j, k: (0, j)),
        (1, d // tn_up, 1), None, jax.ShapeDtypeStruct((N_CHIPS, dple, d // N_CHIPS), F32),
        _shard_spec(dple, tn_up, per_up, 0, 1), ta=True)
    dy = _matmul(
        "mm_dy", dh16, wout, pl.BlockSpec((tm, d), lambda i, j, k: (i, 0)), pl.BlockSpec((tn, d), lambda i, j, k: (j, 0)),
        (gm, gn, 1), None, jax.ShapeDtypeStruct((s, d), F32), tile, tb=True,
        comm=_SwapHalves([g_wgate4, g_wup]) if dist else None)
    early = (g_wout4, g_wgate4, g_wup)
    if dist:
        core = dist["core"]
        dy, swapped_rest = dy
        early_sums = [_chip_sum("chip_sum_" + n, dist["chip"], core, g, o)
                      for n, g, o in zip(LARGE[1:], early, list(swapped_out) + list(swapped_rest))]

    dproj, d_o, stats, g_g_out_b = _gate_bwd(dy, yb, lse, proj, rb, sp["g_out_b"], d)
    dproj, g_g_q, g_g_k, dbias, early_got = _attn_bwd(
        dproj, d_o, stats, proj, sp["g_q"], sp["g_k"], sp["rel_bias"], d,
        comm=_ExchangeChips([s16 for _, s16 in early_sums]) if dist else None)
    dproj, g_w_s, dz_sum, g_ln_g, g_ln_b, g_g_out_a = _gmlp_bwd(
        dproj, dy, proj, z, vn, mu_v, rs_v, ra, sp["w_s"], sp["ln_v_g"], sp["g_out_a"], d)

    if dist:
        tmw_half = _tile(d // 2, 1024)
        n_half = (d // 2) // tmw_half

        def half_of_gwin(name, own, comm):
            def rows(i, j, k, core_ref):
                return (0, (core_ref[0] if own else 1 - core_ref[0]) * n_half + i)
            return _matmul(
                name, hn, dproj, pl.BlockSpec((s, tmw_half), rows),
                pl.BlockSpec((s, tn_in), lambda i, j, k, core_ref: (0, j)),
                (n_half, d_in // tn_in, 1), None, jax.ShapeDtypeStruct((N_CHIPS, d // 2, c4), F32),
                pl.BlockSpec((None, tmw_half, tn_in), lambda i, j, k, core_ref: (j // per_in, i, j % per_in)), ta=True,
                comm=comm, prefetch=(core,))

        g_theirs = half_of_gwin("mm_gwin_theirs", False, None)
        g_mine, (g_from_sibling,) = half_of_gwin("mm_gwin_mine", True, _SwapHalves([g_theirs], whole=True))
        win_sum = _chip_sum("chip_sum_w_in", dist["chip"], core, g_mine, g_from_sibling)
    else:
        g_win = _matmul(
            "mm_gwin", hn, dproj, pl.BlockSpec((s, tmw), lambda i, j, k: (0, i)),
            pl.BlockSpec((s, tn_in), lambda i, j, k: (0, j)),
            (d // tmw, d_in // tn_in, 1), None, jax.ShapeDtypeStruct((N_CHIPS, d, c4), F32),
            _shard_spec(tmw, tn_in, per_in, 0, 1), ta=True)
    tk = c4
    tmh, tnh = _tile(s, 1024), _tile(d, 1024)
    dhn = _matmul(
        "mm_dhn", dproj, win4, pl.BlockSpec((tmh, tk), lambda i, j, k: (i, k)), _shard_spec(tnh, tk, 1, 1, 2),
        (s // tmh, d // tnh, d_in // tk), (tmh, tnh), jax.ShapeDtypeStruct((s, d), F32),
        pl.BlockSpec((tmh, tnh), lambda i, j, k: (i, j)), tb=True,
        comm=_ExchangeChips([win_sum[1]]) if dist else None)
    if dist:
        dhn, win_got = dhn
    grad_x, g_g_pre = _rms_bwd("rms_pre_bwd", dhn, xs, rx, sp["g_pre"], dh, False)

    small = {
        "g_pre": g_g_pre,
        "w_s": g_w_s,
        "b_s": jnp.sum(dz_sum.reshape(CHUNK, heads, HEAD_DIM), axis=-1).T,
        "ln_v_g": g_ln_g, "ln_v_b": g_ln_b,
        "g_q": g_g_q, "g_k": g_g_k,
        "rel_bias": dbias[:, :, 0].T,
        "g_out_a": g_g_out_a, "g_out_b": g_g_out_b,
        "g_ple": g_g_ple,
    }
    if not dist:
        return loss, grad_x, (g_win, *early), small
    sums32 = [win_sum[0]] + [s32 for s32, _ in early_sums]
    return loss, grad_x, (sums32, list(win_got) + list(early_got)), small


def _place():
    x, y, c = lax.axis_index("x"), lax.axis_index("y"), lax.axis_index("c")
    chips = [(1 - x, y), (x, 1 - y), (1 - x, 1 - y)]
    return x, y, c, chips


def _remote(src, dst, send_sems, recv_sems, k, to):
    return pltpu.make_async_remote_copy(src_ref=src, dst_ref=dst, send_sem=send_sems.at[k], recv_sem=recv_sems.at[k],
                                        device_id=to, device_id_type=MESH)


def _cast_bf16(name, chip, w):
    r, c = w.shape
    tm = _tile(r, 256)

    def body(chip_ref, w_ref, o_ref):
        del chip_ref
        o_ref[...] = w_ref[...].astype(BF16)

    return pl.pallas_call(
        body, name=name,
        grid_spec=pltpu.PrefetchScalarGridSpec(
            num_scalar_prefetch=1, grid=(r // tm,),
            in_specs=[pl.BlockSpec((tm, c), lambda i, chip_ref: (i, 0))],
            out_specs=pl.BlockSpec((None, tm, c), lambda i, chip_ref: (chip_ref[0], i, 0))),
        out_shape=jax.ShapeDtypeStruct((N_CHIPS, r, c), BF16), compiler_params=_params("parallel"),
    )(chip, w)


class _Gather:
    in_place = True

    def __init__(self, fulls):
        self.ins = list(fulls)
        self.out_shape = [jax.ShapeDtypeStruct(f.shape, f.dtype) for f in fulls]
        n = len(fulls)
        self.scratch = [pltpu.SemaphoreType.DMA((6 * n,)), pltpu.SemaphoreType.DMA((6 * n,))]

    @staticmethod
    def _sends(outs, sems):
        send_sems, recv_sems = sems
        x, y, c, chips = _place()
        cps = []
        for w, ref in enumerate(outs):
            half = ref.shape[1] // 2
            blk = ref.at[2 * x + y, pl.ds(c * half, half)]
            cps += [_remote(blk, blk, send_sems, recv_sems, 6 * w + q, (*chip, c)) for q, chip in enumerate(chips)]
        return cps

    def start(self, ins, outs, sems):
        for cp in self._sends(outs, sems):
            cp.start()

    def finish(self, ins, outs, sems):
        send_sems, recv_sems = sems
        x, y, c, chips = _place()
        sibling = (x, y, 1 - c)
        forwards = []
        for w, ref in enumerate(outs):
            half = ref.shape[1] // 2
            for q, chip in enumerate(chips):
                blk = ref.at[2 * chip[0] + chip[1], pl.ds(c * half, half)]
                _remote(blk, blk, send_sems, recv_sems, 6 * w + q, sibling).wait_recv()
                fwd = _remote(blk, blk, send_sems, recv_sems, 6 * w + 3 + q, sibling)
                fwd.start()
                forwards.append(fwd)
        for w, ref in enumerate(outs):
            half = ref.shape[1] // 2
            for q, chip in enumerate(chips):
                blk = ref.at[2 * chip[0] + chip[1], pl.ds((1 - c) * half, half)]
                _remote(blk, blk, send_sems, recv_sems, 6 * w + 3 + q, sibling).wait_recv()
        for cp in self._sends(outs, sems) + forwards:
            cp.wait_send()


class _ExchangeChips:
    in_place = False

    def __init__(self, sums16):
        self.ins = list(sums16)
        self.out_shape = [jax.ShapeDtypeStruct(g.shape, g.dtype) for g in sums16]
        n = len(sums16)
        self.scratch = [pltpu.SemaphoreType.DMA((3 * n,)), pltpu.SemaphoreType.DMA((3 * n,))]

    @staticmethod
    def _sends(ins, outs, sems):
        send_sems, recv_sems = sems
        x, y, c, chips = _place()
        return [_remote(ins[w].at[2 * chip[0] + chip[1]], outs[w].at[2 * x + y], send_sems, recv_sems, 3 * w + q, (*chip, c))
                for w in range(len(ins)) for q, chip in enumerate(chips)]

    def start(self, ins, outs, sems):
        for cp in self._sends(ins, outs, sems):
            cp.start()

    def finish(self, ins, outs, sems):
        send_sems, recv_sems = sems
        x, y, c, chips = _place()
        for w in range(len(ins)):
            for q, chip in enumerate(chips):
                blk = outs[w].at[2 * chip[0] + chip[1]]
                _remote(blk, blk, send_sems, recv_sems, 3 * w + q, (*chip, c)).wait_recv()
        for cp in self._sends(ins, outs, sems):
            cp.wait_send()


def _mm_in_gather(chip, hn, win4):
    s, d = hn.shape
    c4 = win4.shape[2]
    tm, tn = _tile(s, 1024), _tile(c4, 1024)
    per = c4 // tn
    nj, gm = N_CHIPS * per, s // tm
    half = d // 2

    def tile_of(j):
        jj = j - per
        o = jnp.where(j < per, 0, jnp.where(j < 3 * per, 1 + jj % 2, 3))
        t = jnp.where(j < per, j, jnp.where(j < 3 * per, jj // 2, j - 3 * per))
        return o, t

    def step_of(o, t):
        return per + 2 * t + o - 1 if o in (1, 2) else o * per + t

    def block_of(j, chip_ref):
        o, _ = tile_of(j)
        return chip_ref[0] ^ (((o & 1) << 1) | (o >> 1))

    i_late = max(gm - 2, 0)

    def body(chip_ref, hn_ref, w_any, proj_ref, w_ref, wbuf, wsem, send_sems, recv_sems):
        del w_any
        j, i = pl.program_id(0), pl.program_id(1)
        x, y, c, chips = _place()
        sibling = (x, y, 1 - c)

        def region(k, rows_half, t):
            return w_ref.at[k, pl.ds(rows_half * half, half), pl.ds(t * tn, tn)]

        def send_mine(q, t):
            blk = region(2 * x + y, c, t)
            return _remote(blk, blk, send_sems, recv_sems, q * per + t, (*chips[q], c))

        def forward(q, t):
            blk = region(2 * chips[q][0] + chips[q][1], c, t)
            return _remote(blk, blk, send_sems, recv_sems, (3 + q) * per + t, sibling)

        def fetch(jj, mine, slot):
            rows = pl.ds((c if mine else 1 - c) * half, half)
            cols = pl.ds(pl.multiple_of(tile_of(jj)[1] * tn, LANE), tn)
            return pltpu.make_async_copy(w_ref.at[block_of(jj, chip_ref), rows, cols], wbuf.at[slot, rows],
                                         wsem.at[2 * slot + (0 if mine else 1)])

        def foreign_tiles():
            return [(o, t) for o in range(1, N_CHIPS) for t in range(per)]

        relayed = [(0, t) for t in range(1, per, 2)] + [(1, t) for t in range(0, per, 2)]

        def relay(q, t):
            blk = region(2 * chips[q][0] + chips[q][1], c, t)
            return _remote(blk, blk, send_sems, recv_sems, 2 * per + t, (*chips[1 - q], c))

        def landed(q, t):
            blk = region(2 * chips[q][0] + chips[q][1], c, t)
            _remote(blk, blk, send_sems, recv_sems, q * per + t, sibling).wait_recv()
            forward(q, t).start()
            if (q, t) in relayed:
                relay(q, t).start()

        @pl.when((j == 0) & (i == 0))
        def _():
            for t in range(per):
                send_mine(0, t).start()
                send_mine(1, t).start()
            fetch(0, True, 0).start()
            fetch(0, False, 0).start()

        nxt = j + 1

        @pl.when((i == 0) & (nxt < nj))
        def _():
            for o, t in foreign_tiles():
                @pl.when(nxt == step_of(o, t))
                def _():
                    if o == 1:
                        landed(0, t)
                        landed(1, t)
                    elif o == 3:
                        landed(2, t)

            fetch(nxt, True, nxt % 2).start()

            @pl.when(nxt < per)
            def _():
                fetch(nxt, False, nxt % 2).start()

        @pl.when((i == i_late) & (nxt < nj))
        def _():
            for o, t in foreign_tiles():
                @pl.when(nxt == step_of(o, t))
                def _():
                    blk = region(2 * chips[o - 1][0] + chips[o - 1][1], 1 - c, t)
                    _remote(blk, blk, send_sems, recv_sems, (3 + o - 1) * per + t, sibling).wait_recv()
                    fetch(nxt, False, nxt % 2).start()

        @pl.when(i == 0)
        def _():
            fetch(j, True, j % 2).wait()
            fetch(j, False, j % 2).wait()

        proj_ref[...] = _dot(hn_ref[...], wbuf[j % 2])

        @pl.when((j == nj - 1) & (i == gm - 1))
        def _():
            for t in range(per):
                send_mine(0, t).wait_send()
                send_mine(1, t).wait_send()
                for q in range(3):
                    forward(q, t).wait_send()
            for q, t in relayed:
                relay(q, t).wait_send()

    return pl.pallas_call(
        body, name="mm_in_gather",
        grid_spec=pltpu.PrefetchScalarGridSpec(
            num_scalar_prefetch=1, grid=(nj, gm),
            in_specs=[pl.BlockSpec((tm, d), lambda j, i, chip_ref: (i, 0)), ANY],
            out_specs=[pl.BlockSpec((tm, tn), lambda j, i, chip_ref: (i, block_of(j, chip_ref) * per + tile_of(j)[1])), ANY],
            scratch_shapes=[pltpu.VMEM((2, d, tn), BF16), pltpu.SemaphoreType.DMA((4,)),
                            pltpu.SemaphoreType.DMA((6 * per,)), pltpu.SemaphoreType.DMA((6 * per,))]),
        out_shape=[jax.ShapeDtypeStruct((s, N_CHIPS * c4), F32), jax.ShapeDtypeStruct(win4.shape, win4.dtype)],
        input_output_aliases={2: 1},
        compiler_params=_params("arbitrary", "arbitrary"),
    )(chip, hn, win4)


class _SwapHalves:
    in_place = False

    def __init__(self, grads, whole=False):
        self.ins = list(grads)
        self.whole = whole
        self.out_shape = [jax.ShapeDtypeStruct((N_CHIPS, g.shape[1] // (1 if whole else 2), g.shape[2]), g.dtype)
                          for g in grads]
        self.scratch = [pltpu.SemaphoreType.DMA((len(grads),)), pltpu.SemaphoreType.DMA((len(grads),))]

    def _copies(self, ins, outs, sems):
        x, y, c, _ = _place()
        cps = []
        for w in range(len(ins)):
            half = ins[w].shape[1] // 2
            src = ins[w] if self.whole else ins[w].at[:, pl.ds((1 - c) * half, half)]
            cps.append(_remote(src, outs[w], sems[0], sems[1], w, (x, y, 1 - c)))
        return cps

    def start(self, ins, outs, sems):
        for cp in self._copies(ins, outs, sems):
            cp.start()

    def finish(self, ins, outs, sems):
        for cp in self._copies(ins, outs, sems):
            cp.wait()


def _chip_sum(name, chip, core, grad, got):
    _, half, c = got.shape
    th = _tile(half, 128)
    n = half // th
    skip = n if grad.shape[1] != half else 0

    def body(chip_ref, core_ref, g_ref, o_ref, s_ref):
        del chip_ref, core_ref
        s_ref[...] = (g_ref[...] + o_ref[...]).astype(s_ref.dtype)

    def specs(first, out_block):
        def block(k, i, chip_ref):
            return (chip_ref[0] + first + k) % N_CHIPS

        return dict(
            in_specs=[pl.BlockSpec((None, th, c), lambda k, i, chip_ref, core_ref: (block(k, i, chip_ref), core_ref[0] * skip + i, 0)),
                      pl.BlockSpec((None, th, c), lambda k, i, chip_ref, core_ref: (block(k, i, chip_ref), i, 0))],
            out_specs=out_block(block))

    own = pl.pallas_call(
        body, name=name + "_own",
        grid_spec=pltpu.PrefetchScalarGridSpec(
            num_scalar_prefetch=2, grid=(1, n),
            **specs(0, lambda block: pl.BlockSpec((th, c), lambda k, i, chip_ref, core_ref: (i, 0)))),
        out_shape=jax.ShapeDtypeStruct((half, c), F32), compiler_params=_params("parallel", "parallel"),
    )(chip, core, grad, got)
    others = pl.pallas_call(
        functools.partial(body), name=name + "_others",
        grid_spec=pltpu.PrefetchScalarGridSpec(
            num_scalar_prefetch=2, grid=(N_CHIPS - 1, n),
            **specs(1, lambda block: pl.BlockSpec(
                (None, th, c), lambda k, i, chip_ref, core_ref: (block(k, i, chip_ref), i, 0)))),
        out_shape=jax.ShapeDtypeStruct((N_CHIPS, half, c), BF16), compiler_params=_params("parallel", "parallel"),
    )(chip, core, grad, got)
    return own, others


def _riding(comm, grid, body, n_pre, n_in, n_out):
    if comm is None:
        return body
    c_in, c_out, c_scr = len(comm.ins), len(comm.out_shape), len(comm.scratch)

    def wrapped(*refs):
        pre, rest = refs[:n_pre], refs[n_pre:]
        ins, rest = rest[:n_in], rest[n_in:]
        cin, rest = rest[:c_in], rest[c_in:]
        outs, rest = rest[:n_out], rest[n_out:]
        cout, rest = rest[:c_out], rest[c_out:]
        own, sems = rest[:len(rest) - c_scr], rest[len(rest) - c_scr:]
        ids = [pl.program_id(ax) for ax in range(len(grid))]
        first, last = ids[0] == 0, ids[0] == grid[0] - 1
        for ax in range(1, len(grid)):
            first, last = first & (ids[ax] == 0), last & (ids[ax] == grid[ax] - 1)

        @pl.when(first)
        def _():
            comm.start(cin, cout, sems)

        body(*pre, *ins, *outs, *own)

        @pl.when(last)
        def _():
            comm.finish(cin, cout, sems)

    return wrapped


def _total(name, chip, core, sum32, got16, comm=None):
    half, c = sum32.shape
    th = _tile(half, 128)
    n = half // th

    def body(chip_ref, core_ref, own_ref, a_ref, b_ref, c_ref, o_ref):
        del chip_ref, core_ref
        o_ref[...] = ((own_ref[...] + a_ref[...].astype(F32)) + b_ref[...].astype(F32)) + c_ref[...].astype(F32)

    def other(step):
        return pl.BlockSpec((None, th, c), lambda i, chip_ref, core_ref: ((chip_ref[0] + step) % N_CHIPS, i, 0))

    c_ins = list(comm.ins) if comm else []
    res = pl.pallas_call(
        _riding(comm, (n,), body, 2, 4, 1), name=name,
        grid_spec=pltpu.PrefetchScalarGridSpec(
            num_scalar_prefetch=2, grid=(n,),
            in_specs=[pl.BlockSpec((th, c), lambda i, chip_ref, core_ref: (i, 0)), other(1), other(2), other(3)]
            + [ANY] * len(c_ins),
            out_specs=[pl.BlockSpec((th, c), lambda i, chip_ref, core_ref: (core_ref[0] * n + i, 0))]
            + [ANY] * len(c_ins),
            scratch_shapes=list(comm.scratch) if comm else []),
        out_shape=[jax.ShapeDtypeStruct((2 * half, c), F32)] + (list(comm.out_shape) if comm else []),
        input_output_aliases={6 + i: 1 + i for i in range(len(c_ins))} if comm else {},
        compiler_params=_params("arbitrary" if comm else "parallel"),
    )(chip, core, sum32, got16, got16, got16, *c_ins)
    return (res[0], res[1:]) if comm else res[0]


class _JoinHalves:
    in_place = True

    def __init__(self, blocks):
        self.ins = list(blocks)
        self.out_shape = [jax.ShapeDtypeStruct(t.shape, t.dtype) for t in blocks]
        self.scratch = [pltpu.SemaphoreType.DMA((len(blocks),)), pltpu.SemaphoreType.DMA((len(blocks),))]

    @staticmethod
    def _copy(ref, w, mine, sems):
        x, y, c, _ = _place()
        half = ref.shape[0] // 2
        rows = ref.at[pl.ds((c if mine else 1 - c) * half, half)]
        return _remote(rows, rows, sems[0], sems[1], w, (x, y, 1 - c))

    def start(self, ins, outs, sems):
        for w, ref in enumerate(outs):
            self._copy(ref, w, True, sems).start()

    def finish(self, ins, outs, sems):
        for w, ref in enumerate(outs):
            self._copy(ref, w, True, sems).wait_send()
            self._copy(ref, w, False, sems).wait_recv()


class _Both:
    in_place = True

    def __init__(self, a, b):
        assert a.in_place and b.in_place
        self.parts = (a, b)
        self.ins = list(a.ins) + list(b.ins)
        self.out_shape = list(a.out_shape) + list(b.out_shape)
        self.scratch = list(a.scratch) + list(b.scratch)

    def _split(self, ins, outs, sems):
        a = self.parts[0]
        n, k = len(a.ins), len(a.scratch)
        return (a, ins[:n], outs[:n], sems[:k]), (self.parts[1], ins[n:], outs[n:], sems[k:])

    def start(self, ins, outs, sems):
        for part, i, o, s in self._split(ins, outs, sems):
            part.start(i, o, s)

    def finish(self, ins, outs, sems):
        for part, i, o, s in self._split(ins, outs, sems):
            part.finish(i, o, s)


def _alone(name, comm):
    n = len(comm.ins)

    def body(*refs):
        comm.start(refs[:n], refs[n:2 * n], refs[2 * n:])
        comm.finish(refs[:n], refs[n:2 * n], refs[2 * n:])

    return pl.pallas_call(
        body, name=name, in_specs=[ANY] * n, out_specs=[ANY] * n, out_shape=comm.out_shape,
        scratch_shapes=comm.scratch, input_output_aliases={i: i for i in range(n)},
    )(*comm.ins)


class _GatherSmall:
    in_place = True

    def __init__(self, full):
        self.ins = [full]
        self.out_shape = [jax.ShapeDtypeStruct(full.shape, full.dtype)]
        self.scratch = [pltpu.SemaphoreType.DMA((7,)), pltpu.SemaphoreType.DMA((7,))]

    @staticmethod
    def _copy(ref, sems, k, block, to):
        m_per = ref.shape[0] // N_DEV
        px, py, pc = block
        rows = ref.at[pl.ds((4 * px + 2 * py + pc) * m_per, m_per), :]
        return _remote(rows, rows, sems[0], sems[1], k, to)

    def _first(self, ref, sems):
        x, y, c, chips = _place()
        me = (x, y, c)
        return [self._copy(ref, sems, 0, me, (x, y, 1 - c))] + \
               [self._copy(ref, sems, 1 + j, me, (*chip, c)) for j, chip in enumerate(chips)]

    def start(self, ins, outs, sems):
        for cp in self._first(outs[0], sems):
            cp.start()

    def finish(self, ins, outs, sems):
        ref = outs[0]
        x, y, c, chips = _place()
        me, sibling = (x, y, c), (x, y, 1 - c)
        passed = [self._copy(ref, sems, 4 + j, (*chip, c), sibling) for j, chip in enumerate(chips)]
        for j, chip in enumerate(chips):
            self._copy(ref, sems, 1 + j, (*chip, c), me).wait_recv()
            passed[j].start()
        self._copy(ref, sems, 0, sibling, me).wait_recv()
        for j, chip in enumerate(chips):
            self._copy(ref, sems, 4 + j, (*chip, 1 - c), me).wait_recv()
        for cp in self._first(ref, sems) + passed:
            cp.wait_send()


def _adamw_math(w, g, m, v):
    m = ADAM_B1 * m + (1.0 - ADAM_B1) * g
    v = ADAM_B2 * v + (1.0 - ADAM_B2) * (g * g)
    m_hat = m / (1.0 - ADAM_B1 ** ADAM_STEP)
    v_hat = v / (1.0 - ADAM_B2 ** ADAM_STEP)
    delta = -ADAM_LR * (m_hat / (jnp.sqrt(v_hat) + ADAM_EPS) + ADAM_WD * w)
    return delta, m, v


def _adamw(name, w, g, m, v):
    r, c = w.shape
    tm = _tile(r, 128)

    def body(w_ref, g_ref, m_ref, v_ref, g_out, d_out, m_out, v_out):
        g = g_ref[...]
        g_out[...] = g
        d_out[...], m_out[...], v_out[...] = _adamw_math(w_ref[...], g, m_ref[...], v_ref[...])

    spec = pl.BlockSpec((tm, c), lambda i: (i, 0))
    return pl.pallas_call(
        body, name=name, grid=(r // tm,), in_specs=[spec] * 4, out_specs=[spec] * 4,
        out_shape=[jax.ShapeDtypeStruct((r, c), F32)] * 4, compiler_params=_params("parallel"),
    )(w, g, m, v)


def _adamw_small(gathered, w, m, v):
    rows = w.shape[0]

    def body(all_ref, w_ref, m_ref, v_ref, g_out, d_out, m_out, v_out):
        g = all_ref[0:rows, :]
        for dev in range(1, N_DEV):
            g = g + all_ref[dev * rows:(dev + 1) * rows, :]
        g_out[...] = g
        d_out[...], m_out[...], v_out[...] = _adamw_math(w_ref[...], g, m_ref[...], v_ref[...])

    return pl.pallas_call(
        body, name="adamw_small", out_shape=[jax.ShapeDtypeStruct(w.shape, F32)] * 4, compiler_params=_params(),
    )(gathered, w, m, v)


SMALL = ("g_pre", "w_s", "b_s", "ln_v_g", "ln_v_b", "g_q", "g_k", "rel_bias", "g_out_a", "g_out_b", "g_ple")
LARGE = ("w_in", "w_out", "w_ple_gate", "w_ple_up")
WEIGHTS = ("g_pre", "w_in", "w_s", "b_s", "ln_v_g", "ln_v_b", "g_q", "g_k", "rel_bias", "g_out_a", "g_out_b", "w_out",
           "g_ple", "w_ple_gate", "w_ple_up")


def _pack(parts):
    flat = jnp.concatenate([parts[n].reshape(-1).astype(F32) for n in SMALL])
    rows = -(-flat.shape[0] // (8 * LANE)) * 8
    return jnp.pad(flat, (0, rows * LANE - flat.shape[0])).reshape(rows, LANE)


def _unpack(pack, like):
    flat = pack.reshape(-1)
    out, at = {}, 0
    for n in SMALL:
        size = math.prod(like[n].shape)
        out[n] = flat[at:at + size].reshape(like[n].shape)
        at += size
    return out


def kernel(x, p, g_pre, w_in, w_s, b_s, ln_v_g, ln_v_b, g_q, g_k, rel_bias, g_out_a, g_out_b, w_out, g_ple, w_ple_gate, w_ple_up, loss_target, m_g_pre, m_w_in, m_w_s, m_b_s, m_ln_v_g, m_ln_v_b, m_g_q, m_g_k, m_rel_bias, m_g_out_a, m_g_out_b, m_w_out, m_g_ple, m_w_ple_gate, m_w_ple_up, v_g_pre, v_w_in, v_w_s, v_b_s, v_ln_v_g, v_ln_v_b, v_g_q, v_g_k, v_rel_bias, v_g_out_a, v_g_out_b, v_w_out, v_g_ple, v_w_ple_gate, v_w_ple_up):
    given = dict(locals())
    weights = {n: given[n] for n in WEIGHTS}
    mom_m = {n: given["m_" + n] for n in WEIGHTS}
    mom_v = {n: given["v_" + n] for n in WEIGHTS}
    xs, ps, tgt = x[0], p[0, 0], loss_target[0]
    d = xs.shape[1]

    core = lax.axis_index("c").astype(jnp.int32).reshape(1)
    chip = (2 * lax.axis_index("x") + lax.axis_index("y")).astype(jnp.int32).reshape(1)

    win4, wout4, wgate4, wup4 = [_cast_bf16("cast_" + n, chip, weights[n][0]) for n in LARGE]

    sp = {
        "g_pre": g_pre, "w_s": w_s[0], "b_s": b_s[0], "ln_v_g": ln_v_g, "ln_v_b": ln_v_b, "g_q": g_q, "g_k": g_k,
        "rel_bias": rel_bias, "g_out_a": g_out_a, "g_out_b": g_out_b, "g_ple": g_ple,
    }
    loss_local, grad_x, (sums32, got16), small = _local_step(xs, ps, tgt, sp, win4, wout4, wgate4, wup4,
                                                             dist={"chip": chip, "core": core})
    loss = lax.psum(loss_local, MESH_AXES)

    own = dict(zip(LARGE, sums32))
    got = dict(zip(LARGE, got16))
    halves = [_total("total_" + n, chip, core, own[n], got[n]) for n in LARGE[1:]]
    my_pack = _pack(small)
    device = 2 * chip[0] + core[0]
    everyone = lax.dynamic_update_slice(jnp.zeros((N_DEV * my_pack.shape[0], LANE), F32), my_pack,
                                        (device * my_pack.shape[0], 0))
    half_in, (*joined, gathered) = _total("total_w_in", chip, core, own["w_in"], got["w_in"],
                                          comm=_Both(_JoinHalves(halves), _GatherSmall(everyone)))
    grads = dict(zip(LARGE[1:], joined))
    grads["w_in"], = _alone("join_w_in", _JoinHalves([half_in]))

    out_g, out_d, out_m, out_v = {}, {}, {}, {}
    for n in LARGE:
        results = _adamw("adamw_" + n, weights[n][0], grads[n], mom_m[n][0], mom_v[n][0])
        out_g[n], out_d[n], out_m[n], out_v[n] = [r[None] for r in results]

    pg, pd, pm, pv = _adamw_small(gathered, _pack(weights), _pack(mom_m), _pack(mom_v))
    for packed, out in ((pg, out_g), (pd, out_d), (pm, out_m), (pv, out_v)):
        out.update(_unpack(packed, weights))

    return (loss, grad_x[None], *[out_g[n] for n in WEIGHTS], *[out_d[n] for n in WEIGHTS],
            *[out_m[n] for n in WEIGHTS], *[out_v[n] for n in WEIGHTS])
```

```python
import functools
import math

import numpy as np

import jax
import jax.numpy as jnp
from jax import lax
from jax.experimental import pallas as pl
from jax.experimental.pallas import tpu as pltpu

F32 = jnp.float32
BF16 = jnp.bfloat16

HEAD_DIM = 128
CHUNK = 128
BLK = 128
DILATED = ((128, 1), (512, 4), (2048, 16))
NUM_BUCKETS = 32
MAX_DISTANCE = 2048
ATTN_GROUP = 8
EPS = 1e-6
NEG_INF = -1e30
N_CHIPS = 4
N_DEV = 8
MESH_AXES = ("x", "y", "c")

ADAM_LR = 0.001
ADAM_B1 = 0.9
ADAM_B2 = 0.999
ADAM_EPS = 1e-08
ADAM_WD = 0.01
ADAM_STEP = 10

V7X_VMEM_LIMIT = 58 * 1024 * 1024
LANE = 128
MESH = pl.DeviceIdType.MESH
ANY = pl.BlockSpec(memory_space=pl.ANY)


def _params(*sem):
    return pltpu.CompilerParams(dimension_semantics=sem or None, vmem_limit_bytes=V7X_VMEM_LIMIT)


def _tile(n, target):
    if n <= target:
        return n
    t = (target // LANE) * LANE
    while t > LANE and n % t:
        t -= LANE
    assert n % t == 0, (n, target)
    return t


def _gelu(x):
    return 0.5 * x * (1.0 + lax.erf(x * (1.0 / math.sqrt(2.0))))


def _gelu_grad(x):
    return 0.5 * (1.0 + lax.erf(x * (1.0 / math.sqrt(2.0)))) + x * jnp.exp(-0.5 * x * x) * (1.0 / math.sqrt(2.0 * math.pi))


def _silu_and_grad(x):
    s = jax.nn.sigmoid(x)
    return x * s, s * (1.0 + x * (1.0 - s))


def _dot(a, b, ta=False, tb=False):
    return lax.dot_general(a, b, (((0 if ta else 1,), (1 if tb else 0,)), ((), ())), preferred_element_type=F32)


def _colsum8(v):
    return jnp.sum(v.reshape(v.shape[0] // 8, 8, v.shape[1]), axis=0)


def _matmul(name, a, b, a_spec, b_spec, grid, acc_shape, out_shape, out_specs, ta=False, tb=False,
            extras=(), extra_specs=(), epilogue=None, comm=None, prefetch=()):
    nk = grid[2]
    n_pre = len(prefetch)
    n_extra = len(extras)
    single = not isinstance(out_shape, (tuple, list))
    outs_shape = (out_shape,) if single else tuple(out_shape)
    outs_specs = (out_specs,) if single else tuple(out_specs)
    n_out = len(outs_shape)
    c_ins = list(comm.ins) if comm else []
    c_outs = list(comm.out_shape) if comm else []
    c_scratch = list(comm.scratch) if comm else []
    n_cin, n_cout = len(c_ins), len(c_outs)

    def finish(acc, extra_refs, out_refs):
        if epilogue is None:
            out_refs[0][...] = acc.astype(out_refs[0].dtype)
        else:
            epilogue(acc, extra_refs, out_refs)

    def body(*refs):
        a_ref, b_ref, *rest = refs[n_pre:]
        extra_refs = rest[:n_extra]
        cin_refs = rest[n_extra:n_extra + n_cin]
        out_refs = rest[n_extra + n_cin:n_extra + n_cin + n_out]
        cout_refs = rest[n_extra + n_cin + n_out:n_extra + n_cin + n_out + n_cout]
        scratch_refs = rest[n_extra + n_cin + n_out + n_cout:]
        ids = [pl.program_id(ax) for ax in range(3)]
        if comm:
            sems = scratch_refs[len(scratch_refs) - len(c_scratch):]

            @pl.when((ids[0] == 0) & (ids[1] == 0) & (ids[2] == 0))
            def _():
                comm.start(cin_refs, cout_refs, sems)

        if nk == 1:
            finish(_dot(a_ref[...], b_ref[...], ta, tb), extra_refs, out_refs)
        else:
            acc_ref = scratch_refs[0]

            @pl.when(ids[2] == 0)
            def _():
                acc_ref[...] = jnp.zeros_like(acc_ref)

            acc_ref[...] += _dot(a_ref[...], b_ref[...], ta, tb)

            @pl.when(ids[2] == nk - 1)
            def _():
                finish(acc_ref[...], extra_refs, out_refs)

        if comm:
            @pl.when((ids[0] == grid[0] - 1) & (ids[1] == grid[1] - 1) & (ids[2] == nk - 1))
            def _():
                comm.finish(cin_refs, cout_refs, sems)

    scratch = ([] if nk == 1 else [pltpu.VMEM(acc_shape, F32)]) + c_scratch
    aliases = {n_pre + 2 + n_extra + i: n_out + i for i in range(n_cin)} if (comm and comm.in_place) else {}
    res = pl.pallas_call(
        body, name=name,
        grid_spec=pltpu.PrefetchScalarGridSpec(
            num_scalar_prefetch=n_pre, grid=grid,
            in_specs=[a_spec, b_spec, *extra_specs] + [ANY] * n_cin,
            out_specs=list(outs_specs) + [ANY] * n_cout, scratch_shapes=scratch),
        out_shape=list(outs_shape) + c_outs,
        input_output_aliases=aliases,
        compiler_params=_params(*(("arbitrary",) * 3 if comm else ("parallel", "parallel", "arbitrary"))),
    )(*prefetch, a, b, *extras, *c_ins)
    if comm:
        main = res[:n_out]
        return (main[0] if single else main), res[n_out:]
    return res[0] if single else res


def _shard_spec(rows, cols, per_shard, row_axis, col_axis):
    def index(i, j, k):
        g = (i, j, k)
        return (g[col_axis] // per_shard, g[row_axis], g[col_axis] % per_shard)
    return pl.BlockSpec((None, rows, cols), index)


def _rms_fwd(name, x, g):
    s, d = x.shape
    tm = _tile(s, 256)

    def body(x_ref, g_ref, y_ref, r_ref):
        xf = x_ref[...]
        r = lax.rsqrt(jnp.mean(xf * xf, axis=-1, keepdims=True) + EPS)
        y_ref[...] = (xf * r * g_ref[...]).astype(BF16)
        r_ref[...] = r

    return pl.pallas_call(
        body, name=name, grid=(s // tm,),
        in_specs=[pl.BlockSpec((tm, d), lambda i: (i, 0)), pl.BlockSpec((1, d), lambda i: (0, 0))],
        out_specs=(pl.BlockSpec((tm, d), lambda i: (i, 0)), pl.BlockSpec((tm, 1), lambda i: (i, 0))),
        out_shape=(jax.ShapeDtypeStruct((s, d), BF16), jax.ShapeDtypeStruct((s, 1), F32)),
        compiler_params=_params("parallel"),
    )(x, g)


def _rms_bwd(name, dy, x, r, g, skip, with_bf16):
    s, d = x.shape
    tm = _tile(s, 256)
    n = s // tm

    def body(dy_ref, x_ref, r_ref, g_ref, skip_ref, *outs):
        dx_ref = outs[0]
        dg_ref = outs[-2]
        acc_ref = outs[-1]
        i = pl.program_id(0)
        dyv, xv, rv = dy_ref[...].astype(F32), x_ref[...], r_ref[...]

        @pl.when(i == 0)
        def _():
            acc_ref[...] = jnp.zeros_like(acc_ref)

        acc_ref[...] += _colsum8(dyv * xv * rv)
        dg = dyv * g_ref[...]
        dx = skip_ref[...] + rv * (dg - xv * (rv * rv) * jnp.mean(dg * xv, axis=-1, keepdims=True))
        dx_ref[...] = dx
        if with_bf16:
            outs[1][...] = dx.astype(BF16)

        @pl.when(i == n - 1)
        def _():
            dg_ref[...] = jnp.sum(acc_ref[...], axis=0, keepdims=True)

    row = pl.BlockSpec((tm, d), lambda i: (i, 0))
    vec = pl.BlockSpec((1, d), lambda i: (0, 0))
    out_specs = [row] + ([row] if with_bf16 else []) + [vec]
    out_shape = [jax.ShapeDtypeStruct((s, d), F32)] + ([jax.ShapeDtypeStruct((s, d), BF16)] if with_bf16 else []) \
        + [jax.ShapeDtypeStruct((1, d), F32)]
    return pl.pallas_call(
        body, name=name, grid=(n,),
        in_specs=[row, row, pl.BlockSpec((tm, 1), lambda i: (i, 0)), vec, row],
        out_specs=out_specs, out_shape=out_shape, scratch_shapes=[pltpu.VMEM((8, d), F32)],
        compiler_params=_params("arbitrary"),
    )(dy, x, r, g, skip)


def _causal(w):
    t = lax.broadcasted_iota(jnp.int32, w.shape, 0)
    s_ = lax.broadcasted_iota(jnp.int32, w.shape, 1)
    return jnp.where(t >= s_, w, 0.0)


def _gmlp_fwd(proj, w_s, b_st, ln_g, ln_b, g_out, d_model):
    s = proj.shape[0]
    wa = d_model // 2
    groups = wa // HEAD_DIM
    tm = _tile(s, 256)
    n_chunks = tm // CHUNK

    def body(au_ref, av_ref, az_ref, ws_ref, bst_ref, lng_ref, lnb_ref, go_ref,
             y_ref, z_ref, vn_ref, mu_ref, rs_ref, ra_ref):
        gv = _gelu(av_ref[...])
        mu = jnp.mean(gv, axis=-1, keepdims=True)
        xc = gv - mu
        rs = lax.rsqrt(jnp.mean(xc * xc, axis=-1, keepdims=True) + EPS)
        vn = (xc * rs * lng_ref[...] + lnb_ref[...]).astype(BF16)
        vn_ref[...] = vn
        mu_ref[...] = mu
        rs_ref[...] = rs
        for g in range(groups):
            wm = _causal(ws_ref[g]).astype(BF16)
            cols = slice(g * HEAD_DIM, (g + 1) * HEAD_DIM)
            for ch in range(n_chunks):
                rows = slice(ch * CHUNK, (ch + 1) * CHUNK)
                z_ref[rows, cols] = _dot(wm, vn_ref[rows, cols]) + bst_ref[:, g:g + 1]
        ya = _gelu(au_ref[...]) * z_ref[...]
        ra = lax.rsqrt(jnp.mean(ya * ya, axis=-1, keepdims=True) + EPS)
        ra_ref[...] = ra
        sz, _ = _silu_and_grad(az_ref[...])
        y_ref[...] = (ya * ra * go_ref[...] * sz).astype(BF16)

    def col(j):
        return pl.BlockSpec((tm, wa), lambda i: (i, j))
    vec = pl.BlockSpec((1, wa), lambda i: (0, 0))
    stat = pl.BlockSpec((tm, 1), lambda i: (i, 0))
    return pl.pallas_call(
        body, name="gmlp_fwd", grid=(s // tm,),
        in_specs=[col(0), col(1), col(2),
                  pl.BlockSpec((groups, CHUNK, CHUNK), lambda i: (0, 0, 0)),
                  pl.BlockSpec((CHUNK, groups), lambda i: (0, 0)), vec, vec, vec],
        out_specs=(col(0), col(0), col(0), stat, stat, stat),
        out_shape=(jax.ShapeDtypeStruct((s, d_model), BF16), jax.ShapeDtypeStruct((s, wa), F32),
                   jax.ShapeDtypeStruct((s, wa), BF16), jax.ShapeDtypeStruct((s, 1), F32),
                   jax.ShapeDtypeStruct((s, 1), F32), jax.ShapeDtypeStruct((s, 1), F32)),
        compiler_params=_params("parallel"),
    )(proj, proj, proj, w_s, b_st, ln_g, ln_b, g_out)


def _gmlp_bwd(dproj, dy, proj, z, vn, mu, rs, ra, w_s, ln_g, g_out, d_model):
    s = proj.shape[0]
    wa = d_model // 2
    groups = wa // HEAD_DIM
    tm = _tile(s, 256)
    n_chunks = tm // CHUNK
    n = s // tm

    def causal_stack(w):
        t = lax.broadcasted_iota(jnp.int32, w.shape, 1)
        s_ = lax.broadcasted_iota(jnp.int32, w.shape, 2)
        return jnp.where(t >= s_, w, 0.0)

    def body(dproj_in, dy_ref, au_ref, av_ref, az_ref, z_ref, vn_ref, mu_ref, rs_ref, ra_ref, ws_ref, lng_ref, go_ref,
             dp_ref, gws_ref, dzs_ref, glg_ref, glb_ref, ggo_ref,
             dz_s, dvn_s, acc_lg, acc_lb, acc_go):
        del dproj_in
        i = pl.program_id(0)

        @pl.when(i == 0)
        def _():
            gws_ref[...] = jnp.zeros_like(gws_ref)
            dzs_ref[...] = jnp.zeros_like(dzs_ref)
            acc_lg[...] = jnp.zeros_like(acc_lg)
            acc_lb[...] = jnp.zeros_like(acc_lb)
            acc_go[...] = jnp.zeros_like(acc_go)

        au, az, zv, rav = au_ref[...], az_ref[...], z_ref[...], ra_ref[...]
        u = _gelu(au)
        ya = u * zv
        sz, dsz = _silu_and_grad(az)
        dyv = dy_ref[...].astype(F32)
        dp_ref[:, 2 * wa:3 * wa] = (dyv * (ya * rav * go_ref[...]) * dsz).astype(BF16)
        dn = dyv * sz
        acc_go[...] += _colsum8(dn * ya * rav)
        dyg = dn * go_ref[...]
        dya = rav * (dyg - ya * (rav * rav) * jnp.mean(dyg * ya, axis=-1, keepdims=True))
        dp_ref[:, 0:wa] = (dya * zv * _gelu_grad(au)).astype(BF16)
        dz_s[...] = dya * u
        for ch in range(n_chunks):
            dzs_ref[...] += dz_s[ch * CHUNK:(ch + 1) * CHUNK, :]
        for g in range(groups):
            wm = _causal(ws_ref[g]).astype(BF16)
            cols = slice(g * HEAD_DIM, (g + 1) * HEAD_DIM)
            for ch in range(n_chunks):
                rows = slice(ch * CHUNK, (ch + 1) * CHUNK)
                dzb = dz_s[rows, cols].astype(BF16)
                gws_ref[g] += _dot(dzb, vn_ref[rows, cols], tb=True)
                dvn_s[rows, cols] = _dot(wm, dzb, ta=True)
        av = av_ref[...]
        xh = (_gelu(av) - mu_ref[...]) * rs_ref[...]
        dvn = dvn_s[...]
        acc_lb[...] += _colsum8(dvn)
        acc_lg[...] += _colsum8(dvn * xh)
        dxh = dvn * lng_ref[...]
        dgv = rs_ref[...] * (dxh - jnp.mean(dxh, axis=-1, keepdims=True) - xh * jnp.mean(dxh * xh, axis=-1, keepdims=True))
        dp_ref[:, wa:2 * wa] = (dgv * _gelu_grad(av)).astype(BF16)

        @pl.when(i == n - 1)
        def _():
            gws_ref[...] = causal_stack(gws_ref[...])
            glg_ref[...] = jnp.sum(acc_lg[...], axis=0, keepdims=True)
            glb_ref[...] = jnp.sum(acc_lb[...], axis=0, keepdims=True)
            ggo_ref[...] = jnp.sum(acc_go[...], axis=0, keepdims=True)

    def col(j):
        return pl.BlockSpec((tm, wa), lambda i: (i, j))
    vec = pl.BlockSpec((1, wa), lambda i: (0, 0))
    stat = pl.BlockSpec((tm, 1), lambda i: (i, 0))
    ws_spec = pl.BlockSpec((groups, CHUNK, CHUNK), lambda i: (0, 0, 0))
    d_in = dproj.shape[1]
    return pl.pallas_call(
        body, name="gmlp_bwd", grid=(n,),
        in_specs=[ANY, col(0), col(0), col(1), col(2), col(0), col(0), stat, stat, stat, ws_spec, vec, vec],
        out_specs=(pl.BlockSpec((tm, 3 * wa), lambda i: (i, 0)), ws_spec,
                   pl.BlockSpec((CHUNK, wa), lambda i: (0, 0)), vec, vec, vec),
        out_shape=(jax.ShapeDtypeStruct((s, d_in), BF16),
                   jax.ShapeDtypeStruct((groups, CHUNK, CHUNK), F32), jax.ShapeDtypeStruct((CHUNK, wa), F32))
        + (jax.ShapeDtypeStruct((1, wa), F32),) * 3,
        scratch_shapes=[pltpu.VMEM((tm, wa), F32), pltpu.VMEM((tm, wa), F32)] + [pltpu.VMEM((8, wa), F32)] * 3,
        input_output_aliases={0: 0},
        compiler_params=_params("arbitrary"),
    )(dproj, dy, proj, proj, proj, z, vn, mu, rs, ra, w_s, ln_g, g_out)


def _gate_fwd(y_in, yb, proj, g_out, d_model):
    s = proj.shape[0]
    wa = d_model // 2
    tm = _tile(s, 256)

    def body(y_any, yb_ref, bz_ref, go_ref, y_ref, rb_ref):
        del y_any
        ybv = yb_ref[...]
        rb = lax.rsqrt(jnp.mean(ybv * ybv, axis=-1, keepdims=True) + EPS)
        rb_ref[...] = rb
        sz, _ = _silu_and_grad(bz_ref[...])
        y_ref[...] = (ybv * rb * go_ref[...] * sz).astype(BF16)

    return pl.pallas_call(
        body, name="gate_b_fwd", grid=(s // tm,),
        in_specs=[ANY, pl.BlockSpec((tm, wa), lambda i: (i, 0)), pl.BlockSpec((tm, wa), lambda i: (i, 6)),
                  pl.BlockSpec((1, wa), lambda i: (0, 0))],
        out_specs=(pl.BlockSpec((tm, wa), lambda i: (i, 1)), pl.BlockSpec((tm, 1), lambda i: (i, 0))),
        out_shape=(jax.ShapeDtypeStruct((s, d_model), BF16), jax.ShapeDtypeStruct((s, 1), F32)),
        input_output_aliases={0: 0},
        compiler_params=_params("parallel"),
    )(y_in, yb, proj, g_out)


def _gate_bwd(dy, yb, lse, proj, rb, g_out, d_model):
    s, d_in = proj.shape
    wa = d_model // 2
    heads = wa // HEAD_DIM
    tm = _tile(s, 256)
    n = s // tm

    def body(dy_ref, yb_ref, lse_ref, bz_ref, rb_ref, go_ref, dp_ref, do_ref, st_ref, ggo_ref, acc):
        i = pl.program_id(0)

        @pl.when(i == 0)
        def _():
            acc[...] = jnp.zeros_like(acc)

        dyv, ybv, rbv = dy_ref[...].astype(F32), yb_ref[...], rb_ref[...]
        sz, dsz = _silu_and_grad(bz_ref[...])
        dp_ref[...] = (dyv * (ybv * rbv * go_ref[...]) * dsz).astype(BF16)
        dn = dyv * sz
        acc[...] += _colsum8(dn * ybv * rbv)
        dyg = dn * go_ref[...]
        do = rbv * (dyg - ybv * (rbv * rbv) * jnp.mean(dyg * ybv, axis=-1, keepdims=True))
        do_ref[...] = do
        prod = do * ybv
        first = lax.broadcasted_iota(jnp.int32, (tm, 2), 1) == 0
        for h in range(heads):
            delta = jnp.sum(prod[:, h * HEAD_DIM:(h + 1) * HEAD_DIM], axis=-1, keepdims=True)
            st_ref[h] = jnp.where(first, lse_ref[h], delta)

        @pl.when(i == n - 1)
        def _():
            ggo_ref[...] = jnp.sum(acc[...], axis=0, keepdims=True)

    vec = pl.BlockSpec((1, wa), lambda i: (0, 0))
    return pl.pallas_call(
        body, name="gate_b_bwd", grid=(n,),
        in_specs=[pl.BlockSpec((tm, wa), lambda i: (i, 1)), pl.BlockSpec((tm, wa), lambda i: (i, 0)),
                  pl.BlockSpec((heads, tm, 1), lambda i: (0, i, 0)),
                  pl.BlockSpec((tm, wa), lambda i: (i, 6)), pl.BlockSpec((tm, 1), lambda i: (i, 0)), vec],
        out_specs=(pl.BlockSpec((tm, wa), lambda i: (i, 6)), pl.BlockSpec((tm, wa), lambda i: (i, 0)),
                   pl.BlockSpec((heads, tm, 2), lambda i: (0, i, 0)), vec),
        out_shape=(jax.ShapeDtypeStruct((s, d_in), BF16), jax.ShapeDtypeStruct((s, wa), F32),
                   jax.ShapeDtypeStruct((heads, s, 2), F32), jax.ShapeDtypeStruct((1, wa), F32)),
        scratch_shapes=[pltpu.VMEM((8, wa), F32)],
        compiler_params=_params("arbitrary"),
    )(dy, yb, lse, proj, rb, g_out)


def _bucket_tables():
    qi = BLK + np.arange(BLK)
    kj = np.arange(2 * BLK)
    delta = qi[:, None] - kj[None, :]
    max_exact = NUM_BUCKETS // 2
    tabs = []
    for window, dil in DILATED:
        band = (delta >= 0) & (delta <= window // dil)
        dist = np.clip(delta, 0, None) * dil
        d = np.maximum(dist, 1).astype(np.float32)
        large = max_exact + (np.log(d / np.float32(max_exact)) / np.float32(math.log(MAX_DISTANCE / max_exact))
                             * np.float32(NUM_BUCKETS - max_exact)).astype(np.int32)
        large = np.minimum(large, NUM_BUCKETS - 1)
        tabs.append(np.where(band, np.where(dist < max_exact, dist, large), -1).astype(np.int32))
    return np.stack(tabs)


def _bias_tiles(tab_ref, rb_ref, h, bias_s):
    col = lax.broadcasted_iota(jnp.int32, (BLK, 2 * BLK), 1)
    for i in range(len(DILATED)):
        t = tab_ref[i]
        bias = jnp.zeros(t.shape, F32)
        for b in range(NUM_BUCKETS):
            bias = jnp.where(t == b, rb_ref[b, h], bias)
        bias = jnp.where(t >= 0, bias, NEG_INF)
        bias_s[2 * i] = jnp.where(col >= BLK, bias, NEG_INF)
        bias_s[2 * i + 1] = bias


def _block_rows(b, n_blocks, dil):
    nb = n_blocks // dil
    r = b // nb
    n = b % nb
    start = r + dil * BLK * n
    if dil == 1:
        return pl.ds(pl.multiple_of(start, BLK), BLK), n
    return pl.ds(start, BLK, stride=dil), n


def _sub_block(b, rows=BLK, pad=0):
    return pl.ds(pl.multiple_of(b * BLK + pad, BLK), rows)


def _regroup(b, n_blocks, small, big):
    ratio = big // small
    piece = BLK // ratio
    nb_small, nb_big = n_blocks // small, n_blocks // big
    r, n = b // nb_small, b % nb_small
    for q in range(ratio):
        block = (r + small * q) * nb_big + n // ratio
        yield q, ratio, piece, pl.multiple_of(block * BLK + piece * (n % ratio), piece)


def _rms_rows(x, gain):
    return x * lax.rsqrt(jnp.mean(x * x, axis=-1, keepdims=True) + EPS) * gain


def _attn_fwd(proj, g_q, g_k, rel_bias, d_model, comm=None):
    s = proj.shape[0]
    heads = d_model // 2 // HEAD_DIM
    n_blocks = s // BLK
    scale = HEAD_DIM ** -0.5
    tables = jnp.asarray(_bucket_tables())
    c_ins = list(comm.ins) if comm else []
    c_outs = list(comm.out_shape) if comm else []
    c_scratch = list(comm.scratch) if comm else []

    def body(q_ref, k_ref, v_ref, gq_ref, gk_ref, rb_ref, tab_ref, *rest):
        cin_refs, rest = rest[:len(c_ins)], rest[len(c_ins):]
        yb_ref, lse_ref = rest[:2]
        cout_refs, rest = rest[2:2 + len(c_outs)], rest[2 + len(c_outs):]
        qs, ks, vs, qs_w, ks_w, vs_w, tmp, m_s, l_s, acc_s, bias_s, sc_s, p_s = rest[:13]
        sems = rest[13:]
        h = pl.program_id(0)
        if comm:
            @pl.when(h == 0)
            def _():
                comm.start(cin_refs, cout_refs, sems)

        _bias_tiles(tab_ref, rb_ref, h, bias_s)
        gq = gq_ref[...] * scale
        gk = gk_ref[...]
        for buf in (ks, vs, ks_w, vs_w):
            buf[0:BLK, :] = jnp.zeros((BLK, HEAD_DIM), BF16)

        for i, (_, dil) in enumerate(DILATED):
            def prepare(b, carry, i=i, dil=dil):
                rows, _ = _block_rows(b, n_blocks, dil)
                vals = (_rms_rows(q_ref[rows, :], gq), _rms_rows(k_ref[rows, :], gk), v_ref[rows, :])
                for val, buf, pad in zip(vals, (qs, ks, vs), (0, BLK, BLK)):
                    buf[_sub_block(b, pad=pad), :] = val.astype(BF16)
                if i == 1:
                    slot = b % 2
                    for a, val in enumerate(vals):
                        tmp[slot, a] = val
                    for q, ratio, piece, dst in _regroup(b, n_blocks, dil, DILATED[2][1]):
                        for a, (buf, pad) in enumerate(((qs_w, 0), (ks_w, BLK), (vs_w, BLK))):
                            buf[pl.ds(dst + pad, piece), :] = tmp[slot, a, pl.ds(q, piece, stride=ratio), :].astype(BF16)
                return carry
            if i < 2:
                lax.fori_loop(0, n_blocks, prepare, 0, unroll=8)
            qs, ks, vs = (qs, ks, vs) if i < 2 else (qs_w, ks_w, vs_w)

            def group(g, carry, i=i, dil=dil, qs=qs, ks=ks, vs=vs):
                blocks = [g * ATTN_GROUP + t for t in range(ATTN_GROUP)]
                for t, b in enumerate(blocks):
                    sc_s[t] = _dot(qs[_sub_block(b), :], ks[_sub_block(b, 2 * BLK), :], tb=True)
                for t, b in enumerate(blocks):
                    rows, n = _block_rows(b, n_blocks, dil)
                    sc = sc_s[t] + bias_s[2 * i + jnp.minimum(n, 1)]
                    m_i = jnp.max(sc, axis=-1, keepdims=True)
                    pr = jnp.exp(sc - m_i)
                    l_i = jnp.sum(pr, axis=-1, keepdims=True)
                    p_s[t] = pr.astype(BF16)
                    m_s[new, rows, :] = m_i
                    l_s[new, rows, :] = l_i
                for t, b in enumerate(blocks):
                    rows, _ = _block_rows(b, n_blocks, dil)
                    acc_s[new, rows, :] = _dot(p_s[t], vs[_sub_block(b, 2 * BLK), :])
                return carry
            new = min(i, 1)
            lax.fori_loop(0, n_blocks // ATTN_GROUP, group, 0)
            if i > 0:
                m = jnp.maximum(m_s[0], m_s[1])
                w_old, w_new = jnp.exp(m_s[0] - m), jnp.exp(m_s[1] - m)
                m_s[0] = m
                l_s[0] = l_s[0] * w_old + l_s[1] * w_new
                acc_s[0] = acc_s[0] * w_old + acc_s[1] * w_new

        l = l_s[0]
        yb_ref[...] = acc_s[0] / l
        lse_ref[...] = m_s[0] + jnp.log(l)
        if comm:
            @pl.when(h == heads - 1)
            def _():
                comm.finish(cin_refs, cout_refs, sems)

    def head_col(off):
        return pl.BlockSpec((s, HEAD_DIM), lambda h: (0, off * heads + h))
    vec = pl.BlockSpec((1, HEAD_DIM), lambda h: (0, 0))
    res = pl.pallas_call(
        body, name="attn_fwd", grid=(heads,),
        in_specs=[head_col(3), head_col(4), head_col(5), vec, vec,
                  pl.BlockSpec(memory_space=pltpu.SMEM),
                  pl.BlockSpec((len(DILATED), BLK, 2 * BLK), lambda h: (0, 0, 0))] + [ANY] * len(c_ins),
        out_specs=[pl.BlockSpec((s, HEAD_DIM), lambda h: (0, h)), pl.BlockSpec((None, s, 1), lambda h: (h, 0, 0))]
        + [ANY] * len(c_outs),
        out_shape=[jax.ShapeDtypeStruct((s, heads * HEAD_DIM), F32), jax.ShapeDtypeStruct((heads, s, 1), F32)] + c_outs,
        scratch_shapes=[pltpu.VMEM((s, HEAD_DIM), BF16), pltpu.VMEM((s + BLK, HEAD_DIM), BF16),
                        pltpu.VMEM((s + BLK, HEAD_DIM), BF16),
                        pltpu.VMEM((s, HEAD_DIM), BF16), pltpu.VMEM((s + BLK, HEAD_DIM), BF16),
                        pltpu.VMEM((s + BLK, HEAD_DIM), BF16), pltpu.VMEM((2, 3, BLK, HEAD_DIM), F32),
                        pltpu.VMEM((2, s, 1), F32), pltpu.VMEM((2, s, 1), F32), pltpu.VMEM((2, s, HEAD_DIM), F32),
                        pltpu.VMEM((2 * len(DILATED), BLK, 2 * BLK), F32),
                        pltpu.VMEM((ATTN_GROUP, BLK, 2 * BLK), F32), pltpu.VMEM((ATTN_GROUP, BLK, 2 * BLK), BF16)] + c_scratch,
        input_output_aliases={7 + i: 2 + i for i in range(len(c_ins))} if (comm and comm.in_place) else {},
        compiler_params=_params("arbitrary"),
    )(proj, proj, proj, g_q, g_k, rel_bias, tables, *c_ins)
    return res[0], res[1], res[2:]


def _attn_bwd(dproj, d_o, stats, proj, g_q, g_k, rel_bias, d_model, comm=None):
    s, d_in = proj.shape
    heads = d_model // 2 // HEAD_DIM
    n_blocks = s // BLK
    scale = HEAD_DIM ** -0.5
    tables = jnp.asarray(_bucket_tables())
    n_dil = len(DILATED)
    c_ins = list(comm.ins) if comm else []
    c_outs = list(comm.out_shape) if comm else []
    c_scratch = list(comm.scratch) if comm else []

    def body(dp_any, q_ref, k_ref, v_ref, do_ref, st_ref, gq_ref, gk_ref, rb_ref, tab_ref, *rest):
        cin_refs, rest = rest[:len(c_ins)], rest[len(c_ins):]
        dp_out, ggq_ref, ggk_ref, db_ref = rest[:4]
        cout_refs, rest = rest[4:4 + len(c_outs)], rest[4 + len(c_outs):]
        (qs, ks, vs, dos, st_s, qs_w, ks_w, vs_w, dos_w, st_w, tmp, tmp_st,
         dqn, dkn, dvv, dq_u, dk_u, dv_u, bias_s, dbias_s, sc_s, dp_s, p_s, ds_s, sems) = rest[:25]
        comm_sems = rest[25:]
        del dp_any
        h = pl.program_id(0)
        if comm:
            @pl.when(h == 0)
            def _():
                comm.start(cin_refs, cout_refs, comm_sems)


        @pl.when(h == 0)
        def _():
            ggq_ref[...] = jnp.zeros_like(ggq_ref)
            ggk_ref[...] = jnp.zeros_like(ggk_ref)

        dbias_s[...] = jnp.zeros_like(dbias_s)
        _bias_tiles(tab_ref, rb_ref, h, bias_s)
        gq = gq_ref[...] * scale
        gk = gk_ref[...]
        for buf in (ks, vs, ks_w, vs_w):
            buf[0:BLK, :] = jnp.zeros((BLK, HEAD_DIM), BF16)

        for i, (_, dil) in enumerate(DILATED):
            def prepare(b, carry, i=i, dil=dil):
                rows, _ = _block_rows(b, n_blocks, dil)
                vals = (_rms_rows(q_ref[rows, :], gq), _rms_rows(k_ref[rows, :], gk), v_ref[rows, :], do_ref[rows, :])
                for val, buf, pad in zip(vals, (qs, ks, vs, dos), (0, BLK, BLK, 0)):
                    buf[_sub_block(b, pad=pad), :] = val.astype(BF16)
                st = st_ref[rows, :]
                st_s[_sub_block(b), :] = st
                if i == 1:
                    slot = b % 2
                    for a, val in enumerate(vals):
                        tmp[slot, a] = val
                    tmp_st[slot] = st
                    for q, ratio, piece, dst in _regroup(b, n_blocks, dil, DILATED[2][1]):
                        for a, (buf, pad) in enumerate(((qs_w, 0), (ks_w, BLK), (vs_w, BLK), (dos_w, 0))):
                            buf[pl.ds(dst + pad, piece), :] = tmp[slot, a, pl.ds(q, piece, stride=ratio), :].astype(BF16)
                        st_w[pl.ds(dst, piece), :] = tmp_st[slot, pl.ds(q, piece, stride=ratio), :]
                return carry
            if i < 2:
                lax.fori_loop(0, n_blocks, prepare, 0, unroll=8)
            dk_u[...] = jnp.zeros_like(dk_u)
            dv_u[...] = jnp.zeros_like(dv_u)
            operands = (qs, ks, vs, dos, st_s) if i < 2 else (qs_w, ks_w, vs_w, dos_w, st_w)

            def group(g, carry, i=i, dil=dil, operands=operands):
                qs, ks, vs, dos, st_s = operands
                blocks = [g * ATTN_GROUP + t for t in range(ATTN_GROUP)]
                for t, b in enumerate(blocks):
                    sc_s[t] = _dot(qs[_sub_block(b), :], ks[_sub_block(b, 2 * BLK), :], tb=True)
                    dp_s[t] = _dot(dos[_sub_block(b), :], vs[_sub_block(b, 2 * BLK), :], tb=True)
                for t, b in enumerate(blocks):
                    _, n = _block_rows(b, n_blocks, dil)
                    st = st_s[_sub_block(b), :]
                    pr = jnp.exp(sc_s[t] + bias_s[2 * i + jnp.minimum(n, 1)] - st[:, 0:1])
                    ds = pr * (dp_s[t] - st[:, 1:2])
                    dbias_s[i] += ds
                    p_s[t] = pr.astype(BF16)
                    ds_s[t] = ds.astype(BF16)
                for t, b in enumerate(blocks):
                    dq_u[_sub_block(b), :] = _dot(ds_s[t], ks[_sub_block(b, 2 * BLK), :])
                    dk_u[_sub_block(b, 2 * BLK), :] += _dot(ds_s[t], qs[_sub_block(b), :], ta=True)
                    dv_u[_sub_block(b, 2 * BLK), :] += _dot(p_s[t], dos[_sub_block(b), :], ta=True)
                return carry
            lax.fori_loop(0, n_blocks // ATTN_GROUP, group, 0)

            def scatter(b, carry, i=i, dil=dil):
                if i < 2:
                    rows, _ = _block_rows(b, n_blocks, dil)
                    for acc, part, pad in ((dqn, dq_u, 0), (dkn, dk_u, BLK), (dvv, dv_u, BLK)):
                        val = part[_sub_block(b, pad=pad), :]
                        acc[rows, :] = val if i == 0 else acc[rows, :] + val
                else:
                    middle = DILATED[1][1]
                    rows, _ = _block_rows(b, n_blocks, middle)
                    slot = b % 2
                    for a, (acc, part, pad) in enumerate(((dqn, dq_u, 0), (dkn, dk_u, BLK), (dvv, dv_u, BLK))):
                        for q, ratio, piece, src in _regroup(b, n_blocks, middle, dil):
                            tmp[slot, a, pl.ds(q, piece, stride=ratio), :] = part[pl.ds(src + pad, piece), :]
                        acc[rows, :] += tmp[slot, a]
                return carry
            lax.fori_loop(0, n_blocks, scatter, 0, unroll=8)

        q = q_ref[...]
        rq = lax.rsqrt(jnp.mean(q * q, axis=-1, keepdims=True) + EPS)
        k = k_ref[...]
        rk = lax.rsqrt(jnp.mean(k * k, axis=-1, keepdims=True) + EPS)
        dq_n = dqn[...]
        ggq_ref[...] += jnp.sum(dq_n * (q * rq) * scale, axis=0, keepdims=True)
        dg = dq_n * gq_ref[...] * scale
        qs[...] = (rq * (dg - q * (rq * rq) * jnp.mean(dg * q, axis=-1, keepdims=True))).astype(BF16)
        dk_n = dkn[...]
        ggk_ref[...] += jnp.sum(dk_n * (k * rk), axis=0, keepdims=True)
        dg = dk_n * gk_ref[...]
        dos[...] = (rk * (dg - k * (rk * rk) * jnp.mean(dg * k, axis=-1, keepdims=True))).astype(BF16)
        vs[BLK:, :] = dvv[...].astype(BF16)
        copies = [pltpu.make_async_copy(src, dp_out.at[:, pl.ds(pl.multiple_of(((3 + j) * heads + h) * HEAD_DIM, HEAD_DIM), HEAD_DIM)],
                                        sems.at[j]) for j, src in enumerate((qs, dos, vs.at[pl.ds(BLK, s)]))]
        for cp in copies:
            cp.start()
        for b in range(NUM_BUCKETS):
            tot = jnp.zeros((BLK, 2 * BLK), F32)
            for i in range(n_dil):
                tot = tot + jnp.where(tab_ref[i] == b, dbias_s[i], 0.0)
            db_ref[b:b + 1, :] = jnp.full((1, LANE), jnp.sum(tot), F32)
        for cp in copies:
            cp.wait()
        if comm:
            @pl.when(h == heads - 1)
            def _():
                comm.finish(cin_refs, cout_refs, comm_sems)

    def head_col(off):
        return pl.BlockSpec((s, HEAD_DIM), lambda h: (0, off * heads + h))
    own_col = pl.BlockSpec((s, HEAD_DIM), lambda h: (0, h))
    vec = pl.BlockSpec((1, HEAD_DIM), lambda h: (0, 0))
    big = pltpu.VMEM((s, HEAD_DIM), F32)
    padded32 = pltpu.VMEM((s + BLK, HEAD_DIM), F32)
    padded16 = pltpu.VMEM((s + BLK, HEAD_DIM), BF16)
    res = pl.pallas_call(
        body, name="attn_bwd", grid=(heads,),
        in_specs=[ANY, head_col(3), head_col(4), head_col(5), own_col,
                  pl.BlockSpec((None, s, 2), lambda h: (h, 0, 0)), vec, vec,
                  pl.BlockSpec(memory_space=pltpu.SMEM),
                  pl.BlockSpec((n_dil, BLK, 2 * BLK), lambda h: (0, 0, 0))] + [ANY] * len(c_ins),
        out_specs=[ANY, vec, vec, pl.BlockSpec((None, NUM_BUCKETS, LANE), lambda h: (h, 0, 0))] + [ANY] * len(c_outs),
        out_shape=[jax.ShapeDtypeStruct((s, d_in), BF16), jax.ShapeDtypeStruct((1, HEAD_DIM), F32),
                   jax.ShapeDtypeStruct((1, HEAD_DIM), F32), jax.ShapeDtypeStruct((heads, NUM_BUCKETS, LANE), F32)]
        + c_outs,
        scratch_shapes=[pltpu.VMEM((s, HEAD_DIM), BF16), padded16, padded16, pltpu.VMEM((s, HEAD_DIM), BF16),
                        pltpu.VMEM((s, 2), F32),
                        pltpu.VMEM((s, HEAD_DIM), BF16), padded16, padded16, pltpu.VMEM((s, HEAD_DIM), BF16),
                        pltpu.VMEM((s, 2), F32),
                        pltpu.VMEM((2, 4, BLK, HEAD_DIM), F32), pltpu.VMEM((2, BLK, 2), F32),
                        big, big, big, big, padded32, padded32,
                        pltpu.VMEM((2 * n_dil, BLK, 2 * BLK), F32), pltpu.VMEM((n_dil, BLK, 2 * BLK), F32),
                        pltpu.VMEM((ATTN_GROUP, BLK, 2 * BLK), F32), pltpu.VMEM((ATTN_GROUP, BLK, 2 * BLK), F32),
                        pltpu.VMEM((ATTN_GROUP, BLK, 2 * BLK), BF16), pltpu.VMEM((ATTN_GROUP, BLK, 2 * BLK), BF16),
                        pltpu.SemaphoreType.DMA((3,))] + c_scratch,
        input_output_aliases={0: 0},
        compiler_params=_params("arbitrary"),
    )(dproj, proj, proj, proj, d_o, stats, g_q, g_k, rel_bias, tables, *c_ins)
    return res[0], res[1], res[2], res[3], res[4:]


def _local_step(xs, ps, tgt, sp, win4, wout4, wgate4, wup4, dist=None):
    s, d = xs.shape
    wa = d // 2
    heads = wa // HEAD_DIM
    d_in = 7 * wa
    c4 = d_in // N_CHIPS
    dple = ps.shape[1]

    tm, tn = _tile(s, 1024), _tile(d, 1024)
    tn_in = _tile(c4, 1024)
    per_in = c4 // tn_in
    tn_up = _tile(d // N_CHIPS, 512)
    per_up = (d // N_CHIPS) // tn_up
    gm, gn = s // tm, d // tn

    hn, rx = _rms_fwd("rms_pre", xs, sp["g_pre"])
    if dist:
        proj, win4 = _mm_in_gather(dist["chip"], hn, win4)
    else:
        proj = _matmul(
            "mm_in", hn, win4, pl.BlockSpec((tm, d), lambda i, j, k: (i, 0)), _shard_spec(d, tn_in, per_in, 2, 1),
            (gm, d_in // tn_in, 1), None, jax.ShapeDtypeStruct((s, d_in), F32),
            pl.BlockSpec((tm, tn_in), lambda i, j, k: (i, j)))
    b_st = sp["b_s"].T
    y, z, vn, mu_v, rs_v, ra = _gmlp_fwd(proj, sp["w_s"], b_st, sp["ln_v_g"], sp["ln_v_b"], sp["g_out_a"], d)
    yb, lse, gathered = _attn_fwd(proj, sp["g_q"], sp["g_k"], sp["rel_bias"], d,
                                  comm=_Gather([wout4, wgate4, wup4]) if dist else None)
    if dist:
        wout4, wgate4, wup4 = gathered
    wout, wgate = wout4.reshape(d, d), wgate4.reshape(d, d)
    y, rb = _gate_fwd(y, yb, proj, sp["g_out_b"], d)

    def residual(acc, extra, outs):
        outs[0][...] = extra[0][...] + acc

    tile = pl.BlockSpec((tm, tn), lambda i, j, k: (i, j))
    tile_up = pl.BlockSpec((tm, tn_up), lambda i, j, k: (i, j))
    h = _matmul(
        "mm_out", y, wout, pl.BlockSpec((tm, d), lambda i, j, k: (i, 0)), pl.BlockSpec((d, tn_up), lambda i, j, k: (0, j)),
        (gm, d // tn_up, 1), None, jax.ShapeDtypeStruct((s, d), F32), tile_up, extras=(xs,), extra_specs=(tile_up,),
        epilogue=residual)
    hp, rh = _rms_fwd("rms_ple", h, sp["g_ple"])
    p16 = ps.astype(BF16)

    def head(acc, extra, outs):
        p_ref, wup_ref, h_ref, t_ref = extra
        dout_ref, dgl_ref, dup_ref, loss_ref = outs
        up = _dot(p_ref[...], wup_ref[...])
        gate = jax.nn.sigmoid(acc)
        err = h_ref[...] + gate * up - t_ref[...]
        sq = _colsum8(err * err)
        part = sq[:, 0:LANE]
        for c in range(1, sq.shape[1] // LANE):
            part = part + sq[:, c * LANE:(c + 1) * LANE]
        loss_ref[...] = part
        dout = err * (1.0 / d)
        dout_ref[...] = dout
        dup_ref[...] = (dout * gate).astype(BF16)
        dgl_ref[...] = (dout * up * gate * (1.0 - gate)).astype(BF16)

    tile_up = pl.BlockSpec((tm, tn_up), lambda i, j, k: (i, j))
    dout, dglin, dup, loss_parts = _matmul(
        "mm_gate_loss", hp, wgate, pl.BlockSpec((tm, d), lambda i, j, k: (i, 0)), pl.BlockSpec((d, tn_up), lambda i, j, k: (0, j)),
        (gm, d // tn_up, 1), None,
        (jax.ShapeDtypeStruct((s, d), F32), jax.ShapeDtypeStruct((s, d), BF16), jax.ShapeDtypeStruct((s, d), BF16),
         jax.ShapeDtypeStruct((gm * 8, (d // tn_up) * LANE), F32)),
        (tile_up, tile_up, tile_up, pl.BlockSpec((8, LANE), lambda i, j, k: (i, j))),
        extras=(p16, wup4, h, tgt),
        extra_specs=(pl.BlockSpec((tm, dple), lambda i, j, k: (i, 0)), _shard_spec(dple, tn_up, per_up, 2, 1),
                     tile_up, tile_up),
        epilogue=head)
    loss = 0.5 * jnp.sum(loss_parts) * (1.0 / d)

    dhp = _matmul(
        "mm_dhp", dglin, wgate, pl.BlockSpec((tm, d), lambda i, j, k: (i, 0)), pl.BlockSpec((tn, d), lambda i, j, k: (j, 0)),
        (gm, gn, 1), None, jax.ShapeDtypeStruct((s, d), BF16), tile, tb=True)
    dh, dh16, g_g_ple = _rms_bwd("rms_ple_bwd", dhp, h, rh, sp["g_ple"], dout, True)
    tmw = _tile(d, 1024)
    g_wout = _matmul(
        "mm_gwout", y, dh16, pl.BlockSpec((s, tmw), lambda i, j, k: (0, i)), pl.BlockSpec((s, tn), lambda i, j, k: (0, j)),
        (d // tmw, gn, 1), None, jax.ShapeDtypeStruct((d, d), F32), pl.BlockSpec((tmw, tn), lambda i, j, k: (i, j)), ta=True)
    g_wout4 = g_wout.reshape(N_CHIPS, d // N_CHIPS, d)
    g_wgate = _matmul(
        "mm_gwgate", hp, dglin, pl.BlockSpec((s, tmw), lambda i, j, k: (0, i)), pl.BlockSpec((s, tn), lambda i, j, k: (0, j)),
        (d // tmw, gn, 1), None, jax.ShapeDtypeStruct((d, d), F32), pl.BlockSpec((tmw, tn), lambda i, j, k: (i, j)), ta=True,
        comm=_SwapHalves([g_wout4]) if dist else None)
    if dist:
        g_wgate, swapped_out = g_wgate
    g_wgate4 = g_wgate.reshape(N_CHIPS, d // N_CHIPS, d)
    g_wup = _matmul(
        "mm_gwup", p16, dup, pl.BlockSpec((s, dple), lambda i, j, k: (0, 0)), pl.BlockSpec((s, tn_up), lambda i, j, k: (0, j)),
        (1, d // tn_up, 1), None, jax.ShapeDtypeStruct((N_CHIPS, dple, d // N_CHIPS), F32),
        _shard_spec(dple, tn_up, per_up, 0, 1), ta=True)
    dy = _matmul(
        "mm_dy", dh16, wout, pl.BlockSpec((tm, d), lambda i, j, k: (i, 0)), pl.BlockSpec((tn, d), lambda i, j, k: (j, 0)),
        (gm, gn, 1), None, jax.ShapeDtypeStruct((s, d), BF16), tile, tb=True,
        comm=_SwapHalves([g_wgate4, g_wup]) if dist else None)
    early = (g_wout4, g_wgate4, g_wup)
    if dist:
        core = dist["core"]
        dy, swapped_rest = dy
        early_sums = [_chip_sum("chip_sum_" + n, dist["chip"], core, g, o)
                      for n, g, o in zip(LARGE[1:], early, list(swapped_out) + list(swapped_rest))]

    dproj, d_o, stats, g_g_out_b = _gate_bwd(dy, yb, lse, proj, rb, sp["g_out_b"], d)
    dproj, g_g_q, g_g_k, dbias, early_got = _attn_bwd(
        dproj, d_o, stats, proj, sp["g_q"], sp["g_k"], sp["rel_bias"], d,
        comm=_ExchangeChips([s16 for _, s16 in early_sums]) if dist else None)
    dproj, g_w_s, dz_sum, g_ln_g, g_ln_b, g_g_out_a = _gmlp_bwd(
        dproj, dy, proj, z, vn, mu_v, rs_v, ra, sp["w_s"], sp["ln_v_g"], sp["g_out_a"], d)

    if dist:
        tmw_half = _tile(d // 2, 1024)
        n_half = (d // 2) // tmw_half

        def half_of_gwin(name, own, comm):
            def rows(i, j, k, core_ref):
                return (0, (core_ref[0] if own else 1 - core_ref[0]) * n_half + i)
            return _matmul(
                name, hn, dproj, pl.BlockSpec((s, tmw_half), rows),
                pl.BlockSpec((s, tn_in), lambda i, j, k, core_ref: (0, j)),
                (n_half, d_in // tn_in, 1), None, jax.ShapeDtypeStruct((N_CHIPS, d // 2, c4), F32),
                pl.BlockSpec((None, tmw_half, tn_in), lambda i, j, k, core_ref: (j // per_in, i, j % per_in)), ta=True,
                comm=comm, prefetch=(core,))

        g_theirs = half_of_gwin("mm_gwin_theirs", False, None)
        g_mine, (g_from_sibling,) = half_of_gwin("mm_gwin_mine", True, _SwapHalves([g_theirs], whole=True))
        win_sum = _chip_sum("chip_sum_w_in", dist["chip"], core, g_mine, g_from_sibling)
    else:
        g_win = _matmul(
            "mm_gwin", hn, dproj, pl.BlockSpec((s, tmw), lambda i, j, k: (0, i)),
            pl.BlockSpec((s, tn_in), lambda i, j, k: (0, j)),
            (d // tmw, d_in // tn_in, 1), None, jax.ShapeDtypeStruct((N_CHIPS, d, c4), F32),
            _shard_spec(tmw, tn_in, per_in, 0, 1), ta=True)
    tk = c4
    tmh, tnh = _tile(s, 1024), _tile(d, 1024)
    dhn = _matmul(
        "mm_dhn", dproj, win4, pl.BlockSpec((tmh, tk), lambda i, j, k: (i, k)), _shard_spec(tnh, tk, 1, 1, 2),
        (s // tmh, d // tnh, d_in // tk), (tmh, tnh), jax.ShapeDtypeStruct((s, d), BF16),
        pl.BlockSpec((tmh, tnh), lambda i, j, k: (i, j)), tb=True,
        comm=_ExchangeChips([win_sum[1]]) if dist else None)
    if dist:
        dhn, win_got = dhn
    grad_x, g_g_pre = _rms_bwd("rms_pre_bwd", dhn, xs, rx, sp["g_pre"], dh, False)

    small = {
        "g_pre": g_g_pre,
        "w_s": g_w_s,
        "b_s": jnp.sum(dz_sum.reshape(CHUNK, heads, HEAD_DIM), axis=-1).T,
        "ln_v_g": g_ln_g, "ln_v_b": g_ln_b,
        "g_q": g_g_q, "g_k": g_g_k,
        "rel_bias": dbias[:, :, 0].T,
        "g_out_a": g_g_out_a, "g_out_b": g_g_out_b,
        "g_ple": g_g_ple,
    }
    if not dist:
        return loss, grad_x, (g_win, *early), small
    sums32 = [win_sum[0]] + [s32 for s32, _ in early_sums]
    return loss, grad_x, (sums32, list(win_got) + list(early_got)), small


def _place():
    x, y, c = lax.axis_index("x"), lax.axis_index("y"), lax.axis_index("c")
    chips = [(1 - x, y), (x, 1 - y), (1 - x, 1 - y)]
    return x, y, c, chips


def _remote(src, dst, send_sems, recv_sems, k, to):
    return pltpu.make_async_remote_copy(src_ref=src, dst_ref=dst, send_sem=send_sems.at[k], recv_sem=recv_sems.at[k],
                                        device_id=to, device_id_type=MESH)


def _cast_bf16(name, chip, w):
    r, c = w.shape
    tm = _tile(r, 256)

    def body(chip_ref, w_ref, o_ref):
        del chip_ref
        o_ref[...] = w_ref[...].astype(BF16)

    return pl.pallas_call(
        body, name=name,
        grid_spec=pltpu.PrefetchScalarGridSpec(
            num_scalar_prefetch=1, grid=(r // tm,),
            in_specs=[pl.BlockSpec((tm, c), lambda i, chip_ref: (i, 0))],
            out_specs=pl.BlockSpec((None, tm, c), lambda i, chip_ref: (chip_ref[0], i, 0))),
        out_shape=jax.ShapeDtypeStruct((N_CHIPS, r, c), BF16), compiler_params=_params("parallel"),
    )(chip, w)


class _Gather:
    in_place = True

    def __init__(self, fulls):
        self.ins = list(fulls)
        self.out_shape = [jax.ShapeDtypeStruct(f.shape, f.dtype) for f in fulls]
        n = len(fulls)
        self.scratch = [pltpu.SemaphoreType.DMA((6 * n,)), pltpu.SemaphoreType.DMA((6 * n,))]

    @staticmethod
    def _sends(outs, sems):
        send_sems, recv_sems = sems
        x, y, c, chips = _place()
        cps = []
        for w, ref in enumerate(outs):
            half = ref.shape[1] // 2
            blk = ref.at[2 * x + y, pl.ds(c * half, half)]
            cps += [_remote(blk, blk, send_sems, recv_sems, 6 * w + q, (*chip, c)) for q, chip in enumerate(chips)]
        return cps

    def start(self, ins, outs, sems):
        for cp in self._sends(outs, sems):
            cp.start()

    def finish(self, ins, outs, sems):
        send_sems, recv_sems = sems
        x, y, c, chips = _place()
        sibling = (x, y, 1 - c)
        forwards = []
        for w, ref in enumerate(outs):
            half = ref.shape[1] // 2
            for q, chip in enumerate(chips):
                blk = ref.at[2 * chip[0] + chip[1], pl.ds(c * half, half)]
                _remote(blk, blk, send_sems, recv_sems, 6 * w + q, sibling).wait_recv()
                fwd = _remote(blk, blk, send_sems, recv_sems, 6 * w + 3 + q, sibling)
                fwd.start()
                forwards.append(fwd)
        for w, ref in enumerate(outs):
            half = ref.shape[1] // 2
            for q, chip in enumerate(chips):
                blk = ref.at[2 * chip[0] + chip[1], pl.ds((1 - c) * half, half)]
                _remote(blk, blk, send_sems, recv_sems, 6 * w + 3 + q, sibling).wait_recv()
        for cp in self._sends(outs, sems) + forwards:
            cp.wait_send()


class _ExchangeChips:
    in_place = False

    def __init__(self, sums16):
        self.ins = list(sums16)
        self.out_shape = [jax.ShapeDtypeStruct(g.shape, g.dtype) for g in sums16]
        n = len(sums16)
        self.scratch = [pltpu.SemaphoreType.DMA((3 * n,)), pltpu.SemaphoreType.DMA((3 * n,))]

    @staticmethod
    def _sends(ins, outs, sems):
        send_sems, recv_sems = sems
        x, y, c, chips = _place()
        return [_remote(ins[w].at[2 * chip[0] + chip[1]], outs[w].at[2 * x + y], send_sems, recv_sems, 3 * w + q, (*chip, c))
                for w in range(len(ins)) for q, chip in enumerate(chips)]

    def start(self, ins, outs, sems):
        for cp in self._sends(ins, outs, sems):
            cp.start()

    def finish(self, ins, outs, sems):
        send_sems, recv_sems = sems
        x, y, c, chips = _place()
        for w in range(len(ins)):
            for q, chip in enumerate(chips):
                blk = outs[w].at[2 * chip[0] + chip[1]]
                _remote(blk, blk, send_sems, recv_sems, 3 * w + q, (*chip, c)).wait_recv()
        for cp in self._sends(ins, outs, sems):
            cp.wait_send()


def _mm_in_gather(chip, hn, win4):
    s, d = hn.shape
    c4 = win4.shape[2]
    tm, tn = _tile(s, 1024), _tile(c4, 1024)
    per = c4 // tn
    nj, gm = N_CHIPS * per, s // tm
    half = d // 2

    def tile_of(j):
        jj = j - per
        o = jnp.where(j < per, 0, jnp.where(j < 3 * per, 1 + jj % 2, 3))
        t = jnp.where(j < per, j, jnp.where(j < 3 * per, jj // 2, j - 3 * per))
        return o, t

    def step_of(o, t):
        return per + 2 * t + o - 1 if o in (1, 2) else o * per + t

    def block_of(j, chip_ref):
        o, _ = tile_of(j)
        return chip_ref[0] ^ (((o & 1) << 1) | (o >> 1))

    i_late = max(gm - 2, 0)

    def body(chip_ref, hn_ref, w_any, proj_ref, w_ref, wbuf, wsem, send_sems, recv_sems):
        del w_any
        j, i = pl.program_id(0), pl.program_id(1)
        x, y, c, chips = _place()
        sibling = (x, y, 1 - c)

        def region(k, rows_half, t):
            return w_ref.at[k, pl.ds(rows_half * half, half), pl.ds(t * tn, tn)]

        def send_mine(q, t):
            blk = region(2 * x + y, c, t)
            return _remote(blk, blk, send_sems, recv_sems, q * per + t, (*chips[q], c))

        def forward(q, t):
            blk = region(2 * chips[q][0] + chips[q][1], c, t)
            return _remote(blk, blk, send_sems, recv_sems, (3 + q) * per + t, sibling)

        def fetch(jj, mine, slot):
            rows = pl.ds((c if mine else 1 - c) * half, half)
            cols = pl.ds(pl.multiple_of(tile_of(jj)[1] * tn, LANE), tn)
            return pltpu.make_async_copy(w_ref.at[block_of(jj, chip_ref), rows, cols], wbuf.at[slot, rows],
                                         wsem.at[2 * slot + (0 if mine else 1)])

        def foreign_tiles():
            return [(o, t) for o in range(1, N_CHIPS) for t in range(per)]

        relayed = [(0, t) for t in range(1, per, 2)] + [(1, t) for t in range(0, per, 2)]

        def relay(q, t):
            blk = region(2 * chips[q][0] + chips[q][1], c, t)
            return _remote(blk, blk, send_sems, recv_sems, 2 * per + t, (*chips[1 - q], c))

        def landed(q, t):
            blk = region(2 * chips[q][0] + chips[q][1], c, t)
            _remote(blk, blk, send_sems, recv_sems, q * per + t, sibling).wait_recv()
            forward(q, t).start()
            if (q, t) in relayed:
                relay(q, t).start()

        @pl.when((j == 0) & (i == 0))
        def _():
            for t in range(per):
                send_mine(0, t).start()
                send_mine(1, t).start()
            fetch(0, True, 0).start()
            fetch(0, False, 0).start()

        nxt = j + 1

        @pl.when((i == 0) & (nxt < nj))
        def _():
            for o, t in foreign_tiles():
                @pl.when(nxt == step_of(o, t))
                def _():
                    if o == 1:
                        landed(0, t)
                        landed(1, t)
                    elif o == 3:
                        landed(2, t)

            fetch(nxt, True, nxt % 2).start()

            @pl.when(nxt < per)
            def _():
                fetch(nxt, False, nxt % 2).start()

        @pl.when((i == i_late) & (nxt < nj))
        def _():
            for o, t in foreign_tiles():
                @pl.when(nxt == step_of(o, t))
                def _():
                    blk = region(2 * chips[o - 1][0] + chips[o - 1][1], 1 - c, t)
                    _remote(blk, blk, send_sems, recv_sems, (3 + o - 1) * per + t, sibling).wait_recv()
                    fetch(nxt, False, nxt % 2).start()

        @pl.when(i == 0)
        def _():
            fetch(j, True, j % 2).wait()
            fetch(j, False, j % 2).wait()

        proj_ref[...] = _dot(hn_ref[...], wbuf[j % 2])

        @pl.when((j == nj - 1) & (i == gm - 1))
        def _():
            for t in range(per):
                send_mine(0, t).wait_send()
                send_mine(1, t).wait_send()
                for q in range(3):
                    forward(q, t).wait_send()
            for q, t in relayed:
                relay(q, t).wait_send()

    return pl.pallas_call(
        body, name="mm_in_gather",
        grid_spec=pltpu.PrefetchScalarGridSpec(
            num_scalar_prefetch=1, grid=(nj, gm),
            in_specs=[pl.BlockSpec((tm, d), lambda j, i, chip_ref: (i, 0)), ANY],
            out_specs=[pl.BlockSpec((tm, tn), lambda j, i, chip_ref: (i, block_of(j, chip_ref) * per + tile_of(j)[1])), ANY],
            scratch_shapes=[pltpu.VMEM((2, d, tn), BF16), pltpu.SemaphoreType.DMA((4,)),
                            pltpu.SemaphoreType.DMA((6 * per,)), pltpu.SemaphoreType.DMA((6 * per,))]),
        out_shape=[jax.ShapeDtypeStruct((s, N_CHIPS * c4), F32), jax.ShapeDtypeStruct(win4.shape, win4.dtype)],
        input_output_aliases={2: 1},
        compiler_params=_params("arbitrary", "arbitrary"),
    )(chip, hn, win4)


class _SwapHalves:
    in_place = False

    def __init__(self, grads, whole=False):
        self.ins = list(grads)
        self.whole = whole
        self.out_shape = [jax.ShapeDtypeStruct((N_CHIPS, g.shape[1] // (1 if whole else 2), g.shape[2]), g.dtype)
                          for g in grads]
        self.scratch = [pltpu.SemaphoreType.DMA((len(grads),)), pltpu.SemaphoreType.DMA((len(grads),))]

    def _copies(self, ins, outs, sems):
        x, y, c, _ = _place()
        cps = []
        for w in range(len(ins)):
            half = ins[w].shape[1] // 2
            src = ins[w] if self.whole else ins[w].at[:, pl.ds((1 - c) * half, half)]
            cps.append(_remote(src, outs[w], sems[0], sems[1], w, (x, y, 1 - c)))
        return cps

    def start(self, ins, outs, sems):
        for cp in self._copies(ins, outs, sems):
            cp.start()

    def finish(self, ins, outs, sems):
        for cp in self._copies(ins, outs, sems):
            cp.wait()


def _chip_sum(name, chip, core, grad, got):
    _, half, c = got.shape
    th = _tile(half, 128)
    n = half // th
    skip = n if grad.shape[1] != half else 0

    def body(chip_ref, core_ref, g_ref, o_ref, s_ref):
        del chip_ref, core_ref
        s_ref[...] = (g_ref[...] + o_ref[...]).astype(s_ref.dtype)

    def specs(first, out_block):
        def block(k, i, chip_ref):
            return (chip_ref[0] + first + k) % N_CHIPS

        return dict(
            in_specs=[pl.BlockSpec((None, th, c), lambda k, i, chip_ref, core_ref: (block(k, i, chip_ref), core_ref[0] * skip + i, 0)),
                      pl.BlockSpec((None, th, c), lambda k, i, chip_ref, core_ref: (block(k, i, chip_ref), i, 0))],
            out_specs=out_block(block))

    own = pl.pallas_call(
        body, name=name + "_own",
        grid_spec=pltpu.PrefetchScalarGridSpec(
            num_scalar_prefetch=2, grid=(1, n),
            **specs(0, lambda block: pl.BlockSpec((th, c), lambda k, i, chip_ref, core_ref: (i, 0)))),
        out_shape=jax.ShapeDtypeStruct((half, c), F32), compiler_params=_params("parallel", "parallel"),
    )(chip, core, grad, got)
    others = pl.pallas_call(
        functools.partial(body), name=name + "_others",
        grid_spec=pltpu.PrefetchScalarGridSpec(
            num_scalar_prefetch=2, grid=(N_CHIPS - 1, n),
            **specs(1, lambda block: pl.BlockSpec(
                (None, th, c), lambda k, i, chip_ref, core_ref: (block(k, i, chip_ref), i, 0)))),
        out_shape=jax.ShapeDtypeStruct((N_CHIPS, half, c), BF16), compiler_params=_params("parallel", "parallel"),
    )(chip, core, grad, got)
    return own, others


def _riding(comm, grid, body, n_pre, n_in, n_out):
    if comm is None:
        return body
    c_in, c_out, c_scr = len(comm.ins), len(comm.out_shape), len(comm.scratch)

    def wrapped(*refs):
        pre, rest = refs[:n_pre], refs[n_pre:]
        ins, rest = rest[:n_in], rest[n_in:]
        cin, rest = rest[:c_in], rest[c_in:]
        outs, rest = rest[:n_out], rest[n_out:]
        cout, rest = rest[:c_out], rest[c_out:]
        own, sems = rest[:len(rest) - c_scr], rest[len(rest) - c_scr:]
        ids = [pl.program_id(ax) for ax in range(len(grid))]
        first, last = ids[0] == 0, ids[0] == grid[0] - 1
        for ax in range(1, len(grid)):
            first, last = first & (ids[ax] == 0), last & (ids[ax] == grid[ax] - 1)

        @pl.when(first)
        def _():
            comm.start(cin, cout, sems)

        body(*pre, *ins, *outs, *own)

        @pl.when(last)
        def _():
            comm.finish(cin, cout, sems)

    return wrapped


def _total(name, chip, core, sum32, got16, comm=None):
    half, c = sum32.shape
    th = _tile(half, 128)
    n = half // th

    def body(chip_ref, core_ref, own_ref, a_ref, b_ref, c_ref, o_ref):
        del chip_ref, core_ref
        o_ref[...] = ((own_ref[...] + a_ref[...].astype(F32)) + b_ref[...].astype(F32)) + c_ref[...].astype(F32)

    def other(step):
        return pl.BlockSpec((None, th, c), lambda i, chip_ref, core_ref: ((chip_ref[0] + step) % N_CHIPS, i, 0))

    c_ins = list(comm.ins) if comm else []
    res = pl.pallas_call(
        _riding(comm, (n,), body, 2, 4, 1), name=name,
        grid_spec=pltpu.PrefetchScalarGridSpec(
            num_scalar_prefetch=2, grid=(n,),
            in_specs=[pl.BlockSpec((th, c), lambda i, chip_ref, core_ref: (i, 0)), other(1), other(2), other(3)]
            + [ANY] * len(c_ins),
            out_specs=[pl.BlockSpec((th, c), lambda i, chip_ref, core_ref: (core_ref[0] * n + i, 0))]
            + [ANY] * len(c_ins),
            scratch_shapes=list(comm.scratch) if comm else []),
        out_shape=[jax.ShapeDtypeStruct((2 * half, c), F32)] + (list(comm.out_shape) if comm else []),
        input_output_aliases={6 + i: 1 + i for i in range(len(c_ins))} if comm else {},
        compiler_params=_params("arbitrary" if comm else "parallel"),
    )(chip, core, sum32, got16, got16, got16, *c_ins)
    return (res[0], res[1:]) if comm else res[0]


class _JoinHalves:
    in_place = True

    def __init__(self, blocks):
        self.ins = list(blocks)
        self.out_shape = [jax.ShapeDtypeStruct(t.shape, t.dtype) for t in blocks]
        self.scratch = [pltpu.SemaphoreType.DMA((len(blocks),)), pltpu.SemaphoreType.DMA((len(blocks),))]

    @staticmethod
    def _copy(ref, w, mine, sems):
        x, y, c, _ = _place()
        half = ref.shape[0] // 2
        rows = ref.at[pl.ds((c if mine else 1 - c) * half, half)]
        return _remote(rows, rows, sems[0], sems[1], w, (x, y, 1 - c))

    def start(self, ins, outs, sems):
        for w, ref in enumerate(outs):
            self._copy(ref, w, True, sems).start()

    def finish(self, ins, outs, sems):
        for w, ref in enumerate(outs):
            self._copy(ref, w, True, sems).wait_send()
            self._copy(ref, w, False, sems).wait_recv()


class _Both:
    in_place = True

    def __init__(self, a, b):
        assert a.in_place and b.in_place
        self.parts = (a, b)
        self.ins = list(a.ins) + list(b.ins)
        self.out_shape = list(a.out_shape) + list(b.out_shape)
        self.scratch = list(a.scratch) + list(b.scratch)

    def _split(self, ins, outs, sems):
        a = self.parts[0]
        n, k = len(a.ins), len(a.scratch)
        return (a, ins[:n], outs[:n], sems[:k]), (self.parts[1], ins[n:], outs[n:], sems[k:])

    def start(self, ins, outs, sems):
        for part, i, o, s in self._split(ins, outs, sems):
            part.start(i, o, s)

    def finish(self, ins, outs, sems):
        for part, i, o, s in self._split(ins, outs, sems):
            part.finish(i, o, s)


def _alone(name, comm):
    n = len(comm.ins)

    def body(*refs):
        comm.start(refs[:n], refs[n:2 * n], refs[2 * n:])
        comm.finish(refs[:n], refs[n:2 * n], refs[2 * n:])

    return pl.pallas_call(
        body, name=name, in_specs=[ANY] * n, out_specs=[ANY] * n, out_shape=comm.out_shape,
        scratch_shapes=comm.scratch, input_output_aliases={i: i for i in range(n)},
    )(*comm.ins)


class _GatherSmall:
    in_place = True

    def __init__(self, full):
        self.ins = [full]
        self.out_shape = [jax.ShapeDtypeStruct(full.shape, full.dtype)]
        self.scratch = [pltpu.SemaphoreType.DMA((7,)), pltpu.SemaphoreType.DMA((7,))]

    @staticmethod
    def _copy(ref, sems, k, block, to):
        m_per = ref.shape[0] // N_DEV
        px, py, pc = block
        rows = ref.at[pl.ds((4 * px + 2 * py + pc) * m_per, m_per), :]
        return _remote(rows, rows, sems[0], sems[1], k, to)

    def _first(self, ref, sems):
        x, y, c, chips = _place()
        me = (x, y, c)
        return [self._copy(ref, sems, 0, me, (x, y, 1 - c))] + \
               [self._copy(ref, sems, 1 + j, me, (*chip, c)) for j, chip in enumerate(chips)]

    def start(self, ins, outs, sems):
        for cp in self._first(outs[0], sems):
            cp.start()

    def finish(self, ins, outs, sems):
        ref = outs[0]
        x, y, c, chips = _place()
        me, sibling = (x, y, c), (x, y, 1 - c)
        passed = [self._copy(ref, sems, 4 + j, (*chip, c), sibling) for j, chip in enumerate(chips)]
        for j, chip in enumerate(chips):
            self._copy(ref, sems, 1 + j, (*chip, c), me).wait_recv()
            passed[j].start()
        self._copy(ref, sems, 0, sibling, me).wait_recv()
        for j, chip in enumerate(chips):
            self._copy(ref, sems, 4 + j, (*chip, 1 - c), me).wait_recv()
        for cp in self._first(ref, sems) + passed:
            cp.wait_send()


def _adamw_math(w, g, m, v):
    m = ADAM_B1 * m + (1.0 - ADAM_B1) * g
    v = ADAM_B2 * v + (1.0 - ADAM_B2) * (g * g)
    m_hat = m / (1.0 - ADAM_B1 ** ADAM_STEP)
    v_hat = v / (1.0 - ADAM_B2 ** ADAM_STEP)
    delta = -ADAM_LR * (m_hat / (jnp.sqrt(v_hat) + ADAM_EPS) + ADAM_WD * w)
    return delta, m, v


def _adamw(name, w, g, m, v):
    r, c = w.shape
    tm = _tile(r, 128)

    def body(w_ref, g_ref, m_ref, v_ref, g_out, d_out, m_out, v_out):
        g = g_ref[...]
        g_out[...] = g
        d_out[...], m_out[...], v_out[...] = _adamw_math(w_ref[...], g, m_ref[...], v_ref[...])

    spec = pl.BlockSpec((tm, c), lambda i: (i, 0))
    return pl.pallas_call(
        body, name=name, grid=(r // tm,), in_specs=[spec] * 4, out_specs=[spec] * 4,
        out_shape=[jax.ShapeDtypeStruct((r, c), F32)] * 4, compiler_params=_params("parallel"),
    )(w, g, m, v)


def _adamw_small(gathered, w, m, v):
    rows = w.shape[0]

    def body(all_ref, w_ref, m_ref, v_ref, g_out, d_out, m_out, v_out):
        g = all_ref[0:rows, :]
        for dev in range(1, N_DEV):
            g = g + all_ref[dev * rows:(dev + 1) * rows, :]
        g_out[...] = g
        d_out[...], m_out[...], v_out[...] = _adamw_math(w_ref[...], g, m_ref[...], v_ref[...])

    return pl.pallas_call(
        body, name="adamw_small", out_shape=[jax.ShapeDtypeStruct(w.shape, F32)] * 4, compiler_params=_params(),
    )(gathered, w, m, v)


SMALL = ("g_pre", "w_s", "b_s", "ln_v_g", "ln_v_b", "g_q", "g_k", "rel_bias", "g_out_a", "g_out_b", "g_ple")
LARGE = ("w_in", "w_out", "w_ple_gate", "w_ple_up")
WEIGHTS = ("g_pre", "w_in", "w_s", "b_s", "ln_v_g", "ln_v_b", "g_q", "g_k", "rel_bias", "g_out_a", "g_out_b", "w_out",
           "g_ple", "w_ple_gate", "w_ple_up")


def _pack(parts):
    flat = jnp.concatenate([parts[n].reshape(-1).astype(F32) for n in SMALL])
    rows = -(-flat.shape[0] // (8 * LANE)) * 8
    return jnp.pad(flat, (0, rows * LANE - flat.shape[0])).reshape(rows, LANE)


def _unpack(pack, like):
    flat = pack.reshape(-1)
    out, at = {}, 0
    for n in SMALL:
        size = math.prod(like[n].shape)
        out[n] = flat[at:at + size].reshape(like[n].shape)
        at += size
    return out


def kernel(x, p, g_pre, w_in, w_s, b_s, ln_v_g, ln_v_b, g_q, g_k, rel_bias, g_out_a, g_out_b, w_out, g_ple, w_ple_gate, w_ple_up, loss_target, m_g_pre, m_w_in, m_w_s, m_b_s, m_ln_v_g, m_ln_v_b, m_g_q, m_g_k, m_rel_bias, m_g_out_a, m_g_out_b, m_w_out, m_g_ple, m_w_ple_gate, m_w_ple_up, v_g_pre, v_w_in, v_w_s, v_b_s, v_ln_v_g, v_ln_v_b, v_g_q, v_g_k, v_rel_bias, v_g_out_a, v_g_out_b, v_w_out, v_g_ple, v_w_ple_gate, v_w_ple_up):
    given = dict(locals())
    weights = {n: given[n] for n in WEIGHTS}
    mom_m = {n: given["m_" + n] for n in WEIGHTS}
    mom_v = {n: given["v_" + n] for n in WEIGHTS}
    xs, ps, tgt = x[0], p[0, 0], loss_target[0]
    d = xs.shape[1]

    core = lax.axis_index("c").astype(jnp.int32).reshape(1)
    chip = (2 * lax.axis_index("x") + lax.axis_index("y")).astype(jnp.int32).reshape(1)

    win4, wout4, wgate4, wup4 = [_cast_bf16("cast_" + n, chip, weights[n][0]) for n in LARGE]

    sp = {
        "g_pre": g_pre, "w_s": w_s[0], "b_s": b_s[0], "ln_v_g": ln_v_g, "ln_v_b": ln_v_b, "g_q": g_q, "g_k": g_k,
        "rel_bias": rel_bias, "g_out_a": g_out_a, "g_out_b": g_out_b, "g_ple": g_ple,
    }
    loss_local, grad_x, (sums32, got16), small = _local_step(xs, ps, tgt, sp, win4, wout4, wgate4, wup4,
                                                             dist={"chip": chip, "core": core})
    loss = lax.psum(loss_local, MESH_AXES)

    own = dict(zip(LARGE, sums32))
    got = dict(zip(LARGE, got16))
    halves = [_total("total_" + n, chip, core, own[n], got[n]) for n in LARGE[1:]]
    my_pack = _pack(small)
    device = 2 * chip[0] + core[0]
    everyone = lax.dynamic_update_slice(jnp.zeros((N_DEV * my_pack.shape[0], LANE), F32), my_pack,
                                        (device * my_pack.shape[0], 0))
    half_in, (*joined, gathered) = _total("total_w_in", chip, core, own["w_in"], got["w_in"],
                                          comm=_Both(_JoinHalves(halves), _GatherSmall(everyone)))
    grads = dict(zip(LARGE[1:], joined))
    grads["w_in"], = _alone("join_w_in", _JoinHalves([half_in]))

    out_g, out_d, out_m, out_v = {}, {}, {}, {}
    for n in LARGE:
        results = _adamw("adamw_" + n, weights[n][0], grads[n], mom_m[n][0], mom_v[n][0])
        out_g[n], out_d[n], out_m[n], out_v[n] = [r[None] for r in results]

    pg, pd, pm, pv = _adamw_small(gathered, _pack(weights), _pack(mom_m), _pack(mom_v))
    for packed, out in ((pg, out_g), (pd, out_d), (pm, out_m), (pv, out_v)):
        out.update(_unpack(packed, weights))

    return (loss, grad_x[None], *[out_g[n] for n in WEIGHTS], *[out_d[n] for n in WEIGHTS],
            *[out_m[n] for n in WEIGHTS], *[out_v[n] for n in WEIGHTS])
```

```python
import functools
import math

import numpy as np

import jax
import jax.numpy as jnp
from jax import lax
from jax.experimental import pallas as pl
from jax.experimental.pallas import tpu as pltpu

F32 = jnp.float32
BF16 = jnp.bfloat16

HEAD_DIM = 128
CHUNK = 128
BLK = 128
DILATED = ((128, 1), (512, 4), (2048, 16))
NUM_BUCKETS = 32
MAX_DISTANCE = 2048
ATTN_GROUP = 8
ATTN_GROUP_FWD = 16
EPS = 1e-6
NEG_INF = -1e30
N_CHIPS = 4
N_DEV = 8
MESH_AXES = ("x", "y", "c")

ADAM_LR = 0.001
ADAM_B1 = 0.9
ADAM_B2 = 0.999
ADAM_EPS = 1e-08
ADAM_WD = 0.01
ADAM_STEP = 10

V7X_VMEM_LIMIT = 58 * 1024 * 1024
LANE = 128
MESH = pl.DeviceIdType.MESH
ANY = pl.BlockSpec(memory_space=pl.ANY)


def _params(*sem):
    return pltpu.CompilerParams(dimension_semantics=sem or None, vmem_limit_bytes=V7X_VMEM_LIMIT)


def _tile(n, target):
    if n <= target:
        return n
    t = (target // LANE) * LANE
    while t > LANE and n % t:
        t -= LANE
    assert n % t == 0, (n, target)
    return t


def _gelu(x):
    return 0.5 * x * (1.0 + lax.erf(x * (1.0 / math.sqrt(2.0))))


def _gelu_and_grad(x):
    cdf = 0.5 * (1.0 + lax.erf(x * (1.0 / math.sqrt(2.0))))
    return x * cdf, cdf + x * jnp.exp(-0.5 * x * x) * (1.0 / math.sqrt(2.0 * math.pi))


def _silu_and_grad(x):
    s = jax.nn.sigmoid(x)
    return x * s, s * (1.0 + x * (1.0 - s))


def _dot(a, b, ta=False, tb=False):
    return lax.dot_general(a, b, (((0 if ta else 1,), (1 if tb else 0,)), ((), ())), preferred_element_type=F32)


def _colsum8(v):
    return jnp.sum(v.reshape(v.shape[0] // 8, 8, v.shape[1]), axis=0)


def _matmul(name, a, b, a_spec, b_spec, grid, acc_shape, out_shape, out_specs, ta=False, tb=False,
            extras=(), extra_specs=(), epilogue=None, comm=None, prefetch=()):
    nk = grid[2]
    n_pre = len(prefetch)
    n_extra = len(extras)
    single = not isinstance(out_shape, (tuple, list))
    outs_shape = (out_shape,) if single else tuple(out_shape)
    outs_specs = (out_specs,) if single else tuple(out_specs)
    n_out = len(outs_shape)
    c_ins = list(comm.ins) if comm else []
    c_outs = list(comm.out_shape) if comm else []
    c_scratch = list(comm.scratch) if comm else []
    n_cin, n_cout = len(c_ins), len(c_outs)

    def finish(acc, extra_refs, out_refs):
        if epilogue is None:
            out_refs[0][...] = acc.astype(out_refs[0].dtype)
        else:
            epilogue(acc, extra_refs, out_refs)

    def body(*refs):
        a_ref, b_ref, *rest = refs[n_pre:]
        extra_refs = rest[:n_extra]
        cin_refs = rest[n_extra:n_extra + n_cin]
        out_refs = rest[n_extra + n_cin:n_extra + n_cin + n_out]
        cout_refs = rest[n_extra + n_cin + n_out:n_extra + n_cin + n_out + n_cout]
        scratch_refs = rest[n_extra + n_cin + n_out + n_cout:]
        ids = [pl.program_id(ax) for ax in range(3)]
        if comm:
            sems = scratch_refs[len(scratch_refs) - len(c_scratch):]

            @pl.when((ids[0] == 0) & (ids[1] == 0) & (ids[2] == 0))
            def _():
                comm.start(cin_refs, cout_refs, sems)

        if nk == 1:
            finish(_dot(a_ref[...], b_ref[...], ta, tb), extra_refs, out_refs)
        else:
            acc_ref = scratch_refs[0]

            @pl.when(ids[2] == 0)
            def _():
                acc_ref[...] = jnp.zeros_like(acc_ref)

            acc_ref[...] += _dot(a_ref[...], b_ref[...], ta, tb)

            @pl.when(ids[2] == nk - 1)
            def _():
                finish(acc_ref[...], extra_refs, out_refs)

        if comm:
            @pl.when((ids[0] == grid[0] - 1) & (ids[1] == grid[1] - 1) & (ids[2] == nk - 1))
            def _():
                comm.finish(cin_refs, cout_refs, sems)

    scratch = ([] if nk == 1 else [pltpu.VMEM(acc_shape, F32)]) + c_scratch
    aliases = {n_pre + 2 + n_extra + i: n_out + i for i in range(n_cin)} if (comm and comm.in_place) else {}
    res = pl.pallas_call(
        body, name=name,
        grid_spec=pltpu.PrefetchScalarGridSpec(
            num_scalar_prefetch=n_pre, grid=grid,
            in_specs=[a_spec, b_spec, *extra_specs] + [ANY] * n_cin,
            out_specs=list(outs_specs) + [ANY] * n_cout, scratch_shapes=scratch),
        out_shape=list(outs_shape) + c_outs,
        input_output_aliases=aliases,
        compiler_params=_params(*(("arbitrary",) * 3 if comm else ("parallel", "parallel", "arbitrary"))),
    )(*prefetch, a, b, *extras, *c_ins)
    if comm:
        main = res[:n_out]
        return (main[0] if single else main), res[n_out:]
    return res[0] if single else res


def _shard_spec(rows, cols, per_shard, row_axis, col_axis):
    def index(i, j, k):
        g = (i, j, k)
        return (g[col_axis] // per_shard, g[row_axis], g[col_axis] % per_shard)
    return pl.BlockSpec((None, rows, cols), index)


def _rms_fwd(name, x, g):
    s, d = x.shape
    tm = _tile(s, 256)

    def body(x_ref, g_ref, y_ref, r_ref):
        xf = x_ref[...]
        r = lax.rsqrt(jnp.mean(xf * xf, axis=-1, keepdims=True) + EPS)
        y_ref[...] = (xf * r * g_ref[...]).astype(BF16)
        r_ref[...] = r

    return pl.pallas_call(
        body, name=name, grid=(s // tm,),
        in_specs=[pl.BlockSpec((tm, d), lambda i: (i, 0)), pl.BlockSpec((1, d), lambda i: (0, 0))],
        out_specs=(pl.BlockSpec((tm, d), lambda i: (i, 0)), pl.BlockSpec((tm, 1), lambda i: (i, 0))),
        out_shape=(jax.ShapeDtypeStruct((s, d), BF16), jax.ShapeDtypeStruct((s, 1), F32)),
        compiler_params=_params("parallel"),
    )(x, g)


def _rms_bwd(name, dy, x, r, g, skip, with_bf16):
    s, d = x.shape
    tm = _tile(s, 256)
    n = s // tm

    def body(dy_ref, x_ref, r_ref, g_ref, skip_ref, *outs):
        dx_ref = outs[0]
        dg_ref = outs[-2]
        acc_ref = outs[-1]
        i = pl.program_id(0)
        dyv, xv, rv = dy_ref[...].astype(F32), x_ref[...], r_ref[...]

        @pl.when(i == 0)
        def _():
            acc_ref[...] = jnp.zeros_like(acc_ref)

        acc_ref[...] += _colsum8(dyv * xv * rv)
        dg = dyv * g_ref[...]
        dx = skip_ref[...].astype(F32) + rv * (dg - xv * (rv * rv) * jnp.mean(dg * xv, axis=-1, keepdims=True))
        dx_ref[...] = dx
        if with_bf16:
            outs[1][...] = dx.astype(BF16)

        @pl.when(i == n - 1)
        def _():
            dg_ref[...] = jnp.sum(acc_ref[...], axis=0, keepdims=True)

    row = pl.BlockSpec((tm, d), lambda i: (i, 0))
    vec = pl.BlockSpec((1, d), lambda i: (0, 0))
    out_specs = [row] + ([row] if with_bf16 else []) + [vec]
    out_shape = [jax.ShapeDtypeStruct((s, d), F32)] + ([jax.ShapeDtypeStruct((s, d), BF16)] if with_bf16 else []) \
        + [jax.ShapeDtypeStruct((1, d), F32)]
    return pl.pallas_call(
        body, name=name, grid=(n,),
        in_specs=[row, row, pl.BlockSpec((tm, 1), lambda i: (i, 0)), vec, row],
        out_specs=out_specs, out_shape=out_shape, scratch_shapes=[pltpu.VMEM((8, d), F32)],
        compiler_params=_params("arbitrary"),
    )(dy, x, r, g, skip)


def _causal(w):
    t = lax.broadcasted_iota(jnp.int32, w.shape, 0)
    s_ = lax.broadcasted_iota(jnp.int32, w.shape, 1)
    return jnp.where(t >= s_, w, 0.0)


def _gmlp_fwd(proj, w_s, b_st, ln_g, ln_b, g_out, d_model):
    s = proj.shape[0]
    wa = d_model // 2
    groups = wa // HEAD_DIM
    tm = _tile(s, 256)
    n_chunks = tm // CHUNK

    def body(au_ref, av_ref, az_ref, ws_ref, bst_ref, lng_ref, lnb_ref, go_ref,
             y_ref, z_ref, vn_ref, mu_ref, rs_ref, ra_ref):
        gv = _gelu(av_ref[...])
        mu = jnp.mean(gv, axis=-1, keepdims=True)
        xc = gv - mu
        rs = lax.rsqrt(jnp.mean(xc * xc, axis=-1, keepdims=True) + EPS)
        vn = (xc * rs * lng_ref[...] + lnb_ref[...]).astype(BF16)
        vn_ref[...] = vn
        mu_ref[...] = mu
        rs_ref[...] = rs
        for g in range(groups):
            wm = _causal(ws_ref[g]).astype(BF16)
            cols = slice(g * HEAD_DIM, (g + 1) * HEAD_DIM)
            for ch in range(n_chunks):
                rows = slice(ch * CHUNK, (ch + 1) * CHUNK)
                z_ref[rows, cols] = _dot(wm, vn_ref[rows, cols]) + bst_ref[:, g:g + 1]
        ya = _gelu(au_ref[...]) * z_ref[...]
        ra = lax.rsqrt(jnp.mean(ya * ya, axis=-1, keepdims=True) + EPS)
        ra_ref[...] = ra
        sz, _ = _silu_and_grad(az_ref[...])
        y_ref[...] = (ya * ra * go_ref[...] * sz).astype(BF16)

    def col(j):
        return pl.BlockSpec((tm, wa), lambda i: (i, j))
    vec = pl.BlockSpec((1, wa), lambda i: (0, 0))
    stat = pl.BlockSpec((tm, 1), lambda i: (i, 0))
    return pl.pallas_call(
        body, name="gmlp_fwd", grid=(s // tm,),
        in_specs=[col(0), col(1), col(2),
                  pl.BlockSpec((groups, CHUNK, CHUNK), lambda i: (0, 0, 0)),
                  pl.BlockSpec((CHUNK, groups), lambda i: (0, 0)), vec, vec, vec],
        out_specs=(col(0), col(0), col(0), stat, stat, stat),
        out_shape=(jax.ShapeDtypeStruct((s, d_model), BF16), jax.ShapeDtypeStruct((s, wa), F32),
                   jax.ShapeDtypeStruct((s, wa), BF16), jax.ShapeDtypeStruct((s, 1), F32),
                   jax.ShapeDtypeStruct((s, 1), F32), jax.ShapeDtypeStruct((s, 1), F32)),
        compiler_params=_params("parallel"),
    )(proj, proj, proj, w_s, b_st, ln_g, ln_b, g_out)


def _gmlp_bwd(dproj, dy, proj, z, vn, mu, rs, ra, w_s, ln_g, g_out, d_model):
    s = proj.shape[0]
    wa = d_model // 2
    groups = wa // HEAD_DIM
    tm = _tile(s, 256)
    n_chunks = tm // CHUNK
    n = s // tm

    def causal_stack(w):
        t = lax.broadcasted_iota(jnp.int32, w.shape, 1)
        s_ = lax.broadcasted_iota(jnp.int32, w.shape, 2)
        return jnp.where(t >= s_, w, 0.0)

    def body(dproj_in, dy_ref, au_ref, av_ref, az_ref, z_ref, vn_ref, mu_ref, rs_ref, ra_ref, ws_ref, lng_ref, go_ref,
             dp_ref, gws_ref, dzs_ref, glg_ref, glb_ref, ggo_ref,
             dz_s, dvn_s, acc_lg, acc_lb, acc_go):
        del dproj_in
        i = pl.program_id(0)

        @pl.when(i == 0)
        def _():
            gws_ref[...] = jnp.zeros_like(gws_ref)
            dzs_ref[...] = jnp.zeros_like(dzs_ref)
            acc_lg[...] = jnp.zeros_like(acc_lg)
            acc_lb[...] = jnp.zeros_like(acc_lb)
            acc_go[...] = jnp.zeros_like(acc_go)

        au, az, zv, rav = au_ref[...], az_ref[...], z_ref[...], ra_ref[...]
        u, du = _gelu_and_grad(au)
        ya = u * zv
        sz, dsz = _silu_and_grad(az)
        dyv = dy_ref[...].astype(F32)
        dp_ref[:, 2 * wa:3 * wa] = (dyv * (ya * rav * go_ref[...]) * dsz).astype(BF16)
        dn = dyv * sz
        acc_go[...] += _colsum8(dn * ya * rav)
        dyg = dn * go_ref[...]
        dya = rav * (dyg - ya * (rav * rav) * jnp.mean(dyg * ya, axis=-1, keepdims=True))
        dp_ref[:, 0:wa] = (dya * zv * du).astype(BF16)
        dz_s[...] = dya * u
        for ch in range(n_chunks):
            dzs_ref[...] += dz_s[ch * CHUNK:(ch + 1) * CHUNK, :]
        for g in range(groups):
            wm = _causal(ws_ref[g]).astype(BF16)
            cols = slice(g * HEAD_DIM, (g + 1) * HEAD_DIM)
            for ch in range(n_chunks):
                rows = slice(ch * CHUNK, (ch + 1) * CHUNK)
                dzb = dz_s[rows, cols].astype(BF16)
                gws_ref[g] += _dot(dzb, vn_ref[rows, cols], tb=True)
                dvn_s[rows, cols] = _dot(wm, dzb, ta=True)
        av = av_ref[...]
        gv, dgelu_v = _gelu_and_grad(av)
        xh = (gv - mu_ref[...]) * rs_ref[...]
        dvn = dvn_s[...]
        acc_lb[...] += _colsum8(dvn)
        acc_lg[...] += _colsum8(dvn * xh)
        dxh = dvn * lng_ref[...]
        dgv = rs_ref[...] * (dxh - jnp.mean(dxh, axis=-1, keepdims=True) - xh * jnp.mean(dxh * xh, axis=-1, keepdims=True))
        dp_ref[:, wa:2 * wa] = (dgv * dgelu_v).astype(BF16)

        @pl.when(i == n - 1)
        def _():
            gws_ref[...] = causal_stack(gws_ref[...])
            glg_ref[...] = jnp.sum(acc_lg[...], axis=0, keepdims=True)
            glb_ref[...] = jnp.sum(acc_lb[...], axis=0, keepdims=True)
            ggo_ref[...] = jnp.sum(acc_go[...], axis=0, keepdims=True)

    def col(j):
        return pl.BlockSpec((tm, wa), lambda i: (i, j))
    vec = pl.BlockSpec((1, wa), lambda i: (0, 0))
    stat = pl.BlockSpec((tm, 1), lambda i: (i, 0))
    ws_spec = pl.BlockSpec((groups, CHUNK, CHUNK), lambda i: (0, 0, 0))
    d_in = dproj.shape[1]
    return pl.pallas_call(
        body, name="gmlp_bwd", grid=(n,),
        in_specs=[ANY, col(0), col(0), col(1), col(2), col(0), col(0), stat, stat, stat, ws_spec, vec, vec],
        out_specs=(pl.BlockSpec((tm, 3 * wa), lambda i: (i, 0)), ws_spec,
                   pl.BlockSpec((CHUNK, wa), lambda i: (0, 0)), vec, vec, vec),
        out_shape=(jax.ShapeDtypeStruct((s, d_in), BF16),
                   jax.ShapeDtypeStruct((groups, CHUNK, CHUNK), F32), jax.ShapeDtypeStruct((CHUNK, wa), F32))
        + (jax.ShapeDtypeStruct((1, wa), F32),) * 3,
        scratch_shapes=[pltpu.VMEM((tm, wa), F32), pltpu.VMEM((tm, wa), F32)] + [pltpu.VMEM((8, wa), F32)] * 3,
        input_output_aliases={0: 0},
        compiler_params=_params("arbitrary"),
    )(dproj, dy, proj, proj, proj, z, vn, mu, rs, ra, w_s, ln_g, g_out)


def _gate_fwd(y_in, yb, proj, g_out, d_model):
    s = proj.shape[0]
    wa = d_model // 2
    tm = _tile(s, 256)

    def body(y_any, yb_ref, bz_ref, go_ref, y_ref, rb_ref):
        del y_any
        ybv = yb_ref[...]
        rb = lax.rsqrt(jnp.mean(ybv * ybv, axis=-1, keepdims=True) + EPS)
        rb_ref[...] = rb
        sz, _ = _silu_and_grad(bz_ref[...])
        y_ref[...] = (ybv * rb * go_ref[...] * sz).astype(BF16)

    return pl.pallas_call(
        body, name="gate_b_fwd", grid=(s // tm,),
        in_specs=[ANY, pl.BlockSpec((tm, wa), lambda i: (i, 0)), pl.BlockSpec((tm, wa), lambda i: (i, 6)),
                  pl.BlockSpec((1, wa), lambda i: (0, 0))],
        out_specs=(pl.BlockSpec((tm, wa), lambda i: (i, 1)), pl.BlockSpec((tm, 1), lambda i: (i, 0))),
        out_shape=(jax.ShapeDtypeStruct((s, d_model), BF16), jax.ShapeDtypeStruct((s, 1), F32)),
        input_output_aliases={0: 0},
        compiler_params=_params("parallel"),
    )(y_in, yb, proj, g_out)


def _gate_bwd(dy, yb, lse, proj, rb, g_out, d_model):
    s, d_in = proj.shape
    wa = d_model // 2
    heads = wa // HEAD_DIM
    tm = _tile(s, 256)
    n = s // tm

    def body(dy_ref, yb_ref, lse_ref, bz_ref, rb_ref, go_ref, dp_ref, do_ref, st_ref, ggo_ref, acc):
        i = pl.program_id(0)

        @pl.when(i == 0)
        def _():
            acc[...] = jnp.zeros_like(acc)

        dyv, ybv, rbv = dy_ref[...].astype(F32), yb_ref[...], rb_ref[...]
        sz, dsz = _silu_and_grad(bz_ref[...])
        dp_ref[...] = (dyv * (ybv * rbv * go_ref[...]) * dsz).astype(BF16)
        dn = dyv * sz
        acc[...] += _colsum8(dn * ybv * rbv)
        dyg = dn * go_ref[...]
        do = rbv * (dyg - ybv * (rbv * rbv) * jnp.mean(dyg * ybv, axis=-1, keepdims=True))
        do_ref[...] = do
        prod = do * ybv
        first = lax.broadcasted_iota(jnp.int32, (tm, 2), 1) == 0
        for h in range(heads):
            delta = jnp.sum(prod[:, h * HEAD_DIM:(h + 1) * HEAD_DIM], axis=-1, keepdims=True)
            st_ref[h] = jnp.where(first, lse_ref[h], delta)

        @pl.when(i == n - 1)
        def _():
            ggo_ref[...] = jnp.sum(acc[...], axis=0, keepdims=True)

    vec = pl.BlockSpec((1, wa), lambda i: (0, 0))
    return pl.pallas_call(
        body, name="gate_b_bwd", grid=(n,),
        in_specs=[pl.BlockSpec((tm, wa), lambda i: (i, 1)), pl.BlockSpec((tm, wa), lambda i: (i, 0)),
                  pl.BlockSpec((heads, tm, 1), lambda i: (0, i, 0)),
                  pl.BlockSpec((tm, wa), lambda i: (i, 6)), pl.BlockSpec((tm, 1), lambda i: (i, 0)), vec],
        out_specs=(pl.BlockSpec((tm, wa), lambda i: (i, 6)), pl.BlockSpec((tm, wa), lambda i: (i, 0)),
                   pl.BlockSpec((heads, tm, 2), lambda i: (0, i, 0)), vec),
        out_shape=(jax.ShapeDtypeStruct((s, d_in), BF16), jax.ShapeDtypeStruct((s, wa), F32),
                   jax.ShapeDtypeStruct((heads, s, 2), F32), jax.ShapeDtypeStruct((1, wa), F32)),
        scratch_shapes=[pltpu.VMEM((8, wa), F32)],
        compiler_params=_params("arbitrary"),
    )(dy, yb, lse, proj, rb, g_out)


def _bucket_tables():
    qi = BLK + np.arange(BLK)
    kj = np.arange(2 * BLK)
    delta = qi[:, None] - kj[None, :]
    max_exact = NUM_BUCKETS // 2
    tabs = []
    for window, dil in DILATED:
        band = (delta >= 0) & (delta <= window // dil)
        dist = np.clip(delta, 0, None) * dil
        d = np.maximum(dist, 1).astype(np.float32)
        large = max_exact + (np.log(d / np.float32(max_exact)) / np.float32(math.log(MAX_DISTANCE / max_exact))
                             * np.float32(NUM_BUCKETS - max_exact)).astype(np.int32)
        large = np.minimum(large, NUM_BUCKETS - 1)
        tabs.append(np.where(band, np.where(dist < max_exact, dist, large), -1).astype(np.int32))
    return np.stack(tabs)


def _bias_tiles(tab_ref, rb_ref, h, bias_s):
    col = lax.broadcasted_iota(jnp.int32, (BLK, 2 * BLK), 1)
    for i in range(len(DILATED)):
        t = tab_ref[i]
        bias = jnp.zeros(t.shape, F32)
        for b in range(NUM_BUCKETS):
            bias = jnp.where(t == b, rb_ref[b, h], bias)
        bias = jnp.where(t >= 0, bias, NEG_INF)
        bias_s[2 * i] = jnp.where(col >= BLK, bias, NEG_INF)
        bias_s[2 * i + 1] = bias


def _block_rows(b, n_blocks, dil):
    nb = n_blocks // dil
    r = b // nb
    n = b % nb
    start = r + dil * BLK * n
    if dil == 1:
        return pl.ds(pl.multiple_of(start, BLK), BLK), n
    return pl.ds(start, BLK, stride=dil), n


def _sub_block(b, rows=BLK, pad=0):
    return pl.ds(pl.multiple_of(b * BLK + pad, BLK), rows)


def _regroup(b, n_blocks, small, big):
    ratio = big // small
    piece = BLK // ratio
    nb_small, nb_big = n_blocks // small, n_blocks // big
    r, n = b // nb_small, b % nb_small
    for q in range(ratio):
        block = (r + small * q) * nb_big + n // ratio
        yield q, ratio, piece, pl.multiple_of(block * BLK + piece * (n % ratio), piece)


def _rms_rows(x, gain):
    return x * lax.rsqrt(jnp.mean(x * x, axis=-1, keepdims=True) + EPS) * gain


def _attn_fwd(proj, g_q, g_k, rel_bias, d_model, comm=None):
    s = proj.shape[0]
    heads = d_model // 2 // HEAD_DIM
    n_blocks = s // BLK
    scale = HEAD_DIM ** -0.5
    tables = jnp.asarray(_bucket_tables())
    c_ins = list(comm.ins) if comm else []
    c_outs = list(comm.out_shape) if comm else []
    c_scratch = list(comm.scratch) if comm else []

    def body(q_ref, k_ref, v_ref, gq_ref, gk_ref, rb_ref, tab_ref, *rest):
        cin_refs, rest = rest[:len(c_ins)], rest[len(c_ins):]
        yb_ref, lse_ref = rest[:2]
        cout_refs, rest = rest[2:2 + len(c_outs)], rest[2 + len(c_outs):]
        qs, ks, vs, qs_w, ks_w, vs_w, tmp, m_s, l_s, acc_s, bias_s, sc_s, p_s = rest[:13]
        sems = rest[13:]
        h = pl.program_id(0)
        if comm:
            @pl.when(h == 0)
            def _():
                comm.start(cin_refs, cout_refs, sems)

        _bias_tiles(tab_ref, rb_ref, h, bias_s)
        gq = gq_ref[...] * scale
        gk = gk_ref[...]
        for buf in (ks, vs, ks_w, vs_w):
            buf[0:BLK, :] = jnp.zeros((BLK, HEAD_DIM), BF16)

        for i, (_, dil) in enumerate(DILATED):
            def prepare(b, carry, i=i, dil=dil):
                rows, _ = _block_rows(b, n_blocks, dil)
                vals = (_rms_rows(q_ref[rows, :], gq), _rms_rows(k_ref[rows, :], gk), v_ref[rows, :])
                for val, buf, pad in zip(vals, (qs, ks, vs), (0, BLK, BLK)):
                    buf[_sub_block(b, pad=pad), :] = val.astype(BF16)
                if i == 1:
                    slot = b % 2
                    for a, val in enumerate(vals):
                        tmp[slot, a] = val
                    for q, ratio, piece, dst in _regroup(b, n_blocks, dil, DILATED[2][1]):
                        for a, (buf, pad) in enumerate(((qs_w, 0), (ks_w, BLK), (vs_w, BLK))):
                            buf[pl.ds(dst + pad, piece), :] = tmp[slot, a, pl.ds(q, piece, stride=ratio), :].astype(BF16)
                return carry
            if i < 2:
                lax.fori_loop(0, n_blocks, prepare, 0, unroll=8)
            qs, ks, vs = (qs, ks, vs) if i < 2 else (qs_w, ks_w, vs_w)

            def group(g, carry, i=i, dil=dil, qs=qs, ks=ks, vs=vs):
                blocks = [g * ATTN_GROUP_FWD + t for t in range(ATTN_GROUP_FWD)]
                for t, b in enumerate(blocks):
                    sc_s[t] = _dot(qs[_sub_block(b), :], ks[_sub_block(b, 2 * BLK), :], tb=True)
                for t, b in enumerate(blocks):
                    rows, n = _block_rows(b, n_blocks, dil)
                    sc = sc_s[t] + bias_s[2 * i + jnp.minimum(n, 1)]
                    m_i = jnp.max(sc, axis=-1, keepdims=True)
                    pr = jnp.exp(sc - m_i)
                    l_i = jnp.sum(pr, axis=-1, keepdims=True)
                    p_s[t] = pr.astype(BF16)
                    m_s[new, rows, :] = m_i
                    l_s[new, rows, :] = l_i
                for t, b in enumerate(blocks):
                    rows, _ = _block_rows(b, n_blocks, dil)
                    acc_s[new, rows, :] = _dot(p_s[t], vs[_sub_block(b, 2 * BLK), :])
                return carry
            new = min(i, 1)
            lax.fori_loop(0, n_blocks // ATTN_GROUP_FWD, group, 0)
            if i > 0:
                m = jnp.maximum(m_s[0], m_s[1])
                w_old, w_new = jnp.exp(m_s[0] - m), jnp.exp(m_s[1] - m)
                m_s[0] = m
                l_s[0] = l_s[0] * w_old + l_s[1] * w_new
                acc_s[0] = acc_s[0] * w_old + acc_s[1] * w_new

        l = l_s[0]
        yb_ref[...] = acc_s[0] / l
        lse_ref[...] = m_s[0] + jnp.log(l)
        if comm:
            @pl.when(h == heads - 1)
            def _():
                comm.finish(cin_refs, cout_refs, sems)

    def head_col(off):
        return pl.BlockSpec((s, HEAD_DIM), lambda h: (0, off * heads + h))
    vec = pl.BlockSpec((1, HEAD_DIM), lambda h: (0, 0))
    res = pl.pallas_call(
        body, name="attn_fwd", grid=(heads,),
        in_specs=[head_col(3), head_col(4), head_col(5), vec, vec,
                  pl.BlockSpec(memory_space=pltpu.SMEM),
                  pl.BlockSpec((len(DILATED), BLK, 2 * BLK), lambda h: (0, 0, 0))] + [ANY] * len(c_ins),
        out_specs=[pl.BlockSpec((s, HEAD_DIM), lambda h: (0, h)), pl.BlockSpec((None, s, 1), lambda h: (h, 0, 0))]
        + [ANY] * len(c_outs),
        out_shape=[jax.ShapeDtypeStruct((s, heads * HEAD_DIM), F32), jax.ShapeDtypeStruct((heads, s, 1), F32)] + c_outs,
        scratch_shapes=[pltpu.VMEM((s, HEAD_DIM), BF16), pltpu.VMEM((s + BLK, HEAD_DIM), BF16),
                        pltpu.VMEM((s + BLK, HEAD_DIM), BF16),
                        pltpu.VMEM((s, HEAD_DIM), BF16), pltpu.VMEM((s + BLK, HEAD_DIM), BF16),
                        pltpu.VMEM((s + BLK, HEAD_DIM), BF16), pltpu.VMEM((2, 3, BLK, HEAD_DIM), F32),
                        pltpu.VMEM((2, s, 1), F32), pltpu.VMEM((2, s, 1), F32), pltpu.VMEM((2, s, HEAD_DIM), F32),
                        pltpu.VMEM((2 * len(DILATED), BLK, 2 * BLK), F32),
                        pltpu.VMEM((ATTN_GROUP_FWD, BLK, 2 * BLK), F32),
                        pltpu.VMEM((ATTN_GROUP_FWD, BLK, 2 * BLK), BF16)] + c_scratch,
        input_output_aliases={7 + i: 2 + i for i in range(len(c_ins))} if (comm and comm.in_place) else {},
        compiler_params=_params("arbitrary"),
    )(proj, proj, proj, g_q, g_k, rel_bias, tables, *c_ins)
    return res[0], res[1], res[2:]


def _attn_bwd(dproj, d_o, stats, proj, g_q, g_k, rel_bias, d_model, comm=None):
    s, d_in = proj.shape
    heads = d_model // 2 // HEAD_DIM
    n_blocks = s // BLK
    scale = HEAD_DIM ** -0.5
    tables = jnp.asarray(_bucket_tables())
    n_dil = len(DILATED)
    c_ins = list(comm.ins) if comm else []
    c_outs = list(comm.out_shape) if comm else []
    c_scratch = list(comm.scratch) if comm else []

    def body(dp_any, q_ref, k_ref, v_ref, do_ref, st_ref, gq_ref, gk_ref, rb_ref, tab_ref, *rest):
        cin_refs, rest = rest[:len(c_ins)], rest[len(c_ins):]
        dp_out, ggq_ref, ggk_ref, db_ref = rest[:4]
        cout_refs, rest = rest[4:4 + len(c_outs)], rest[4 + len(c_outs):]
        (qs, ks, vs, dos, st_s, qs_w, ks_w, vs_w, dos_w, st_w, tmp, tmp_st,
         dqn, dkn, dvv, dq_u, dk_u, dv_u, bias_s, dbias_s, sc_s, dp_s, p_s, ds_s, sems) = rest[:25]
        comm_sems = rest[25:]
        del dp_any
        h = pl.program_id(0)
        if comm:
            @pl.when(h == 0)
            def _():
                comm.start(cin_refs, cout_refs, comm_sems)


        @pl.when(h == 0)
        def _():
            ggq_ref[...] = jnp.zeros_like(ggq_ref)
            ggk_ref[...] = jnp.zeros_like(ggk_ref)

        dbias_s[...] = jnp.zeros_like(dbias_s)
        _bias_tiles(tab_ref, rb_ref, h, bias_s)
        gq = gq_ref[...] * scale
        gk = gk_ref[...]
        for buf in (ks, vs, ks_w, vs_w):
            buf[0:BLK, :] = jnp.zeros((BLK, HEAD_DIM), BF16)

        for i, (_, dil) in enumerate(DILATED):
            def prepare(b, carry, i=i, dil=dil):
                rows, _ = _block_rows(b, n_blocks, dil)
                vals = (_rms_rows(q_ref[rows, :], gq), _rms_rows(k_ref[rows, :], gk), v_ref[rows, :], do_ref[rows, :])
                for val, buf, pad in zip(vals, (qs, ks, vs, dos), (0, BLK, BLK, 0)):
                    buf[_sub_block(b, pad=pad), :] = val.astype(BF16)
                st = st_ref[rows, :]
                st_s[_sub_block(b), :] = st
                if i == 1:
                    slot = b % 2
                    for a, val in enumerate(vals):
                        tmp[slot, a] = val
                    tmp_st[slot] = st
                    for q, ratio, piece, dst in _regroup(b, n_blocks, dil, DILATED[2][1]):
                        for a, (buf, pad) in enumerate(((qs_w, 0), (ks_w, BLK), (vs_w, BLK), (dos_w, 0))):
                            buf[pl.ds(dst + pad, piece), :] = tmp[slot, a, pl.ds(q, piece, stride=ratio), :].astype(BF16)
                        st_w[pl.ds(dst, piece), :] = tmp_st[slot, pl.ds(q, piece, stride=ratio), :]
                return carry
            if i < 2:
                lax.fori_loop(0, n_blocks, prepare, 0, unroll=8)
            dk_u[...] = jnp.zeros_like(dk_u)
            dv_u[...] = jnp.zeros_like(dv_u)
            operands = (qs, ks, vs, dos, st_s) if i < 2 else (qs_w, ks_w, vs_w, dos_w, st_w)

            def group(g, carry, i=i, dil=dil, operands=operands):
                qs, ks, vs, dos, st_s = operands
                blocks = [g * ATTN_GROUP + t for t in range(ATTN_GROUP)]
                for t, b in enumerate(blocks):
                    sc_s[t] = _dot(qs[_sub_block(b), :], ks[_sub_block(b, 2 * BLK), :], tb=True)
                    dp_s[t] = _dot(dos[_sub_block(b), :], vs[_sub_block(b, 2 * BLK), :], tb=True)
                for t, b in enumerate(blocks):
                    _, n = _block_rows(b, n_blocks, dil)
                    st = st_s[_sub_block(b), :]
                    pr = jnp.exp(sc_s[t] + bias_s[2 * i + jnp.minimum(n, 1)] - st[:, 0:1])
                    ds = pr * (dp_s[t] - st[:, 1:2])
                    dbias_s[i] += ds
                    p_s[t] = pr.astype(BF16)
                    ds_s[t] = ds.astype(BF16)
                for t, b in enumerate(blocks):
                    dq_u[_sub_block(b), :] = _dot(ds_s[t], ks[_sub_block(b, 2 * BLK), :])
                    dk_u[_sub_block(b, 2 * BLK), :] += _dot(ds_s[t], qs[_sub_block(b), :], ta=True)
                    dv_u[_sub_block(b, 2 * BLK), :] += _dot(p_s[t], dos[_sub_block(b), :], ta=True)
                return carry
            lax.fori_loop(0, n_blocks // ATTN_GROUP, group, 0)

            def scatter(b, carry, i=i, dil=dil):
                if i < 2:
                    rows, _ = _block_rows(b, n_blocks, dil)
                    for acc, part, pad in ((dqn, dq_u, 0), (dkn, dk_u, BLK), (dvv, dv_u, BLK)):
                        val = part[_sub_block(b, pad=pad), :]
                        acc[rows, :] = val if i == 0 else acc[rows, :] + val
                else:
                    middle = DILATED[1][1]
                    rows, _ = _block_rows(b, n_blocks, middle)
                    slot = b % 2
                    for a, (acc, part, pad) in enumerate(((dqn, dq_u, 0), (dkn, dk_u, BLK), (dvv, dv_u, BLK))):
                        for q, ratio, piece, src in _regroup(b, n_blocks, middle, dil):
                            tmp[slot, a, pl.ds(q, piece, stride=ratio), :] = part[pl.ds(src + pad, piece), :]
                        acc[rows, :] += tmp[slot, a]
                return carry
            lax.fori_loop(0, n_blocks, scatter, 0, unroll=8)

        q = q_ref[...]
        rq = lax.rsqrt(jnp.mean(q * q, axis=-1, keepdims=True) + EPS)
        k = k_ref[...]
        rk = lax.rsqrt(jnp.mean(k * k, axis=-1, keepdims=True) + EPS)
        dq_n = dqn[...]
        ggq_ref[...] += jnp.sum(dq_n * (q * rq) * scale, axis=0, keepdims=True)
        dg = dq_n * gq_ref[...] * scale
        qs[...] = (rq * (dg - q * (rq * rq) * jnp.mean(dg * q, axis=-1, keepdims=True))).astype(BF16)
        dk_n = dkn[...]
        ggk_ref[...] += jnp.sum(dk_n * (k * rk), axis=0, keepdims=True)
        dg = dk_n * gk_ref[...]
        dos[...] = (rk * (dg - k * (rk * rk) * jnp.mean(dg * k, axis=-1, keepdims=True))).astype(BF16)
        vs[BLK:, :] = dvv[...].astype(BF16)
        copies = [pltpu.make_async_copy(src, dp_out.at[:, pl.ds(pl.multiple_of(((3 + j) * heads + h) * HEAD_DIM, HEAD_DIM), HEAD_DIM)],
                                        sems.at[j]) for j, src in enumerate((qs, dos, vs.at[pl.ds(BLK, s)]))]
        for cp in copies:
            cp.start()
        for b in range(NUM_BUCKETS):
            tot = jnp.zeros((BLK, 2 * BLK), F32)
            for i in range(n_dil):
                tot = tot + jnp.where(tab_ref[i] == b, dbias_s[i], 0.0)
            db_ref[b:b + 1, :] = jnp.full((1, LANE), jnp.sum(tot), F32)
        for cp in copies:
            cp.wait()
        if comm:
            @pl.when(h == heads - 1)
            def _():
                comm.finish(cin_refs, cout_refs, comm_sems)

    def head_col(off):
        return pl.BlockSpec((s, HEAD_DIM), lambda h: (0, off * heads + h))
    own_col = pl.BlockSpec((s, HEAD_DIM), lambda h: (0, h))
    vec = pl.BlockSpec((1, HEAD_DIM), lambda h: (0, 0))
    big = pltpu.VMEM((s, HEAD_DIM), F32)
    padded32 = pltpu.VMEM((s + BLK, HEAD_DIM), F32)
    padded16 = pltpu.VMEM((s + BLK, HEAD_DIM), BF16)
    res = pl.pallas_call(
        body, name="attn_bwd", grid=(heads,),
        in_specs=[ANY, head_col(3), head_col(4), head_col(5), own_col,
                  pl.BlockSpec((None, s, 2), lambda h: (h, 0, 0)), vec, vec,
                  pl.BlockSpec(memory_space=pltpu.SMEM),
                  pl.BlockSpec((n_dil, BLK, 2 * BLK), lambda h: (0, 0, 0))] + [ANY] * len(c_ins),
        out_specs=[ANY, vec, vec, pl.BlockSpec((None, NUM_BUCKETS, LANE), lambda h: (h, 0, 0))] + [ANY] * len(c_outs),
        out_shape=[jax.ShapeDtypeStruct((s, d_in), BF16), jax.ShapeDtypeStruct((1, HEAD_DIM), F32),
                   jax.ShapeDtypeStruct((1, HEAD_DIM), F32), jax.ShapeDtypeStruct((heads, NUM_BUCKETS, LANE), F32)]
        + c_outs,
        scratch_shapes=[pltpu.VMEM((s, HEAD_DIM), BF16), padded16, padded16, pltpu.VMEM((s, HEAD_DIM), BF16),
                        pltpu.VMEM((s, 2), F32),
                        pltpu.VMEM((s, HEAD_DIM), BF16), padded16, padded16, pltpu.VMEM((s, HEAD_DIM), BF16),
                        pltpu.VMEM((s, 2), F32),
                        pltpu.VMEM((2, 4, BLK, HEAD_DIM), F32), pltpu.VMEM((2, BLK, 2), F32),
                        big, big, big, big, padded32, padded32,
                        pltpu.VMEM((2 * n_dil, BLK, 2 * BLK), F32), pltpu.VMEM((n_dil, BLK, 2 * BLK), F32),
                        pltpu.VMEM((ATTN_GROUP, BLK, 2 * BLK), F32), pltpu.VMEM((ATTN_GROUP, BLK, 2 * BLK), F32),
                        pltpu.VMEM((ATTN_GROUP, BLK, 2 * BLK), BF16), pltpu.VMEM((ATTN_GROUP, BLK, 2 * BLK), BF16),
                        pltpu.SemaphoreType.DMA((3,))] + c_scratch,
        input_output_aliases={0: 0},
        compiler_params=_params("arbitrary"),
    )(dproj, proj, proj, proj, d_o, stats, g_q, g_k, rel_bias, tables, *c_ins)
    return res[0], res[1], res[2], res[3], res[4:]


def _local_step(xs, ps, tgt, sp, win4, wout4, wgate4, wup4, dist=None):
    s, d = xs.shape
    wa = d // 2
    heads = wa // HEAD_DIM
    d_in = 7 * wa
    c4 = d_in // N_CHIPS
    dple = ps.shape[1]

    tm, tn = _tile(s, 1024), _tile(d, 1024)
    tn_in = _tile(c4, 1024)
    per_in = c4 // tn_in
    tn_up = _tile(d // N_CHIPS, 512)
    per_up = (d // N_CHIPS) // tn_up
    gm, gn = s // tm, d // tn

    hn, rx = _rms_fwd("rms_pre", xs, sp["g_pre"])
    if dist:
        proj, win4 = _mm_in_gather(dist["chip"], hn, win4)
    else:
        proj = _matmul(
            "mm_in", hn, win4, pl.BlockSpec((tm, d), lambda i, j, k: (i, 0)), _shard_spec(d, tn_in, per_in, 2, 1),
            (gm, d_in // tn_in, 1), None, jax.ShapeDtypeStruct((s, d_in), F32),
            pl.BlockSpec((tm, tn_in), lambda i, j, k: (i, j)))
    b_st = sp["b_s"].T
    y, z, vn, mu_v, rs_v, ra = _gmlp_fwd(proj, sp["w_s"], b_st, sp["ln_v_g"], sp["ln_v_b"], sp["g_out_a"], d)
    yb, lse, gathered = _attn_fwd(proj, sp["g_q"], sp["g_k"], sp["rel_bias"], d,
                                  comm=_Gather([wout4, wgate4, wup4]) if dist else None)
    if dist:
        wout4, wgate4, wup4 = gathered
    wout, wgate = wout4.reshape(d, d), wgate4.reshape(d, d)
    y, rb = _gate_fwd(y, yb, proj, sp["g_out_b"], d)

    def residual(acc, extra, outs):
        outs[0][...] = extra[0][...] + acc

    tile = pl.BlockSpec((tm, tn), lambda i, j, k: (i, j))
    tile_up = pl.BlockSpec((tm, tn_up), lambda i, j, k: (i, j))
    h = _matmul(
        "mm_out", y, wout, pl.BlockSpec((tm, d), lambda i, j, k: (i, 0)), pl.BlockSpec((d, tn_up), lambda i, j, k: (0, j)),
        (gm, d // tn_up, 1), None, jax.ShapeDtypeStruct((s, d), F32), tile_up, extras=(xs,), extra_specs=(tile_up,),
        epilogue=residual)
    hp, rh = _rms_fwd("rms_ple", h, sp["g_ple"])
    p16 = ps.astype(BF16)

    def head(acc, extra, outs):
        p_ref, wup_ref, h_ref, t_ref = extra
        dout_ref, dgl_ref, dup_ref, loss_ref = outs
        up = _dot(p_ref[...], wup_ref[...])
        gate = jax.nn.sigmoid(acc)
        err = h_ref[...] + gate * up - t_ref[...]
        sq = _colsum8(err * err)
        part = sq[:, 0:LANE]
        for c in range(1, sq.shape[1] // LANE):
            part = part + sq[:, c * LANE:(c + 1) * LANE]
        loss_ref[...] = part
        dout = err * (1.0 / d)
        dout_ref[...] = dout.astype(BF16)
        dup_ref[...] = (dout * gate).astype(BF16)
        dgl_ref[...] = (dout * up * gate * (1.0 - gate)).astype(BF16)

    tile_up = pl.BlockSpec((tm, tn_up), lambda i, j, k: (i, j))
    dout, dglin, dup, loss_parts = _matmul(
        "mm_gate_loss", hp, wgate, pl.BlockSpec((tm, d), lambda i, j, k: (i, 0)), pl.BlockSpec((d, tn_up), lambda i, j, k: (0, j)),
        (gm, d // tn_up, 1), None,
        (jax.ShapeDtypeStruct((s, d), BF16), jax.ShapeDtypeStruct((s, d), BF16), jax.ShapeDtypeStruct((s, d), BF16),
         jax.ShapeDtypeStruct((gm * 8, (d // tn_up) * LANE), F32)),
        (tile_up, tile_up, tile_up, pl.BlockSpec((8, LANE), lambda i, j, k: (i, j))),
        extras=(p16, wup4, h, tgt),
        extra_specs=(pl.BlockSpec((tm, dple), lambda i, j, k: (i, 0)), _shard_spec(dple, tn_up, per_up, 2, 1),
                     tile_up, tile_up),
        epilogue=head)
    loss = 0.5 * jnp.sum(loss_parts) * (1.0 / d)

    dhp = _matmul(
        "mm_dhp", dglin, wgate, pl.BlockSpec((tm, d), lambda i, j, k: (i, 0)), pl.BlockSpec((tn, d), lambda i, j, k: (j, 0)),
        (gm, gn, 1), None, jax.ShapeDtypeStruct((s, d), BF16), tile, tb=True)
    dh, dh16, g_g_ple = _rms_bwd("rms_ple_bwd", dhp, h, rh, sp["g_ple"], dout, True)
    tmw = _tile(d, 1024)
    g_wout = _matmul(
        "mm_gwout", y, dh16, pl.BlockSpec((s, tmw), lambda i, j, k: (0, i)), pl.BlockSpec((s, tn), lambda i, j, k: (0, j)),
        (d // tmw, gn, 1), None, jax.ShapeDtypeStruct((d, d), F32), pl.BlockSpec((tmw, tn), lambda i, j, k: (i, j)), ta=True)
    g_wout4 = g_wout.reshape(N_CHIPS, d // N_CHIPS, d)
    g_wgate = _matmul(
        "mm_gwgate", hp, dglin, pl.BlockSpec((s, tmw), lambda i, j, k: (0, i)), pl.BlockSpec((s, tn), lambda i, j, k: (0, j)),
        (d // tmw, gn, 1), None, jax.ShapeDtypeStruct((d, d), F32), pl.BlockSpec((tmw, tn), lambda i, j, k: (i, j)), ta=True,
        comm=_SwapHalves([g_wout4]) if dist else None)
    if dist:
        g_wgate, swapped_out = g_wgate
    g_wgate4 = g_wgate.reshape(N_CHIPS, d // N_CHIPS, d)
    g_wup = _matmul(
        "mm_gwup", p16, dup, pl.BlockSpec((s, dple), lambda i, j, k: (0, 0)), pl.BlockSpec((s, tn_up), lambda i, j, k: (0, j)),
        (1, d // tn_up, 1), None, jax.ShapeDtypeStruct((N_CHIPS, dple, d // N_CHIPS), F32),
        _shard_spec(dple, tn_up, per_up, 0, 1), ta=True)
    dy = _matmul(
        "mm_dy", dh16, wout, pl.BlockSpec((tm, d), lambda i, j, k: (i, 0)), pl.BlockSpec((tn, d), lambda i, j, k: (j, 0)),
        (gm, gn, 1), None, jax.ShapeDtypeStruct((s, d), BF16), tile, tb=True,
        comm=_SwapHalves([g_wgate4, g_wup]) if dist else None)
    early = (g_wout4, g_wgate4, g_wup)
    if dist:
        core = dist["core"]
        dy, swapped_rest = dy
        early_sums = [_chip_sum("chip_sum_" + n, dist["chip"], core, g, o)
                      for n, g, o in zip(LARGE[1:], early, list(swapped_out) + list(swapped_rest))]

    dproj, d_o, stats, g_g_out_b = _gate_bwd(dy, yb, lse, proj, rb, sp["g_out_b"], d)
    dproj, g_g_q, g_g_k, dbias, early_got = _attn_bwd(
        dproj, d_o, stats, proj, sp["g_q"], sp["g_k"], sp["rel_bias"], d,
        comm=_ExchangeChips([s16 for _, s16 in early_sums]) if dist else None)
    dproj, g_w_s, dz_sum, g_ln_g, g_ln_b, g_g_out_a = _gmlp_bwd(
        dproj, dy, proj, z, vn, mu_v, rs_v, ra, sp["w_s"], sp["ln_v_g"], sp["g_out_a"], d)

    if dist:
        tmw_half = _tile(d // 2, 1024)
        n_half = (d // 2) // tmw_half

        def half_of_gwin(name, own, comm):
            def rows(i, j, k, core_ref):
                return (0, (core_ref[0] if own else 1 - core_ref[0]) * n_half + i)
            return _matmul(
                name, hn, dproj, pl.BlockSpec((s, tmw_half), rows),
                pl.BlockSpec((s, tn_in), lambda i, j, k, core_ref: (0, j)),
                (n_half, d_in // tn_in, 1), None, jax.ShapeDtypeStruct((N_CHIPS, d // 2, c4), F32),
                pl.BlockSpec((None, tmw_half, tn_in), lambda i, j, k, core_ref: (j // per_in, i, j % per_in)), ta=True,
                comm=comm, prefetch=(core,))

        g_theirs = half_of_gwin("mm_gwin_theirs", False, None)
        g_mine, (g_from_sibling,) = half_of_gwin("mm_gwin_mine", True, _SwapHalves([g_theirs], whole=True))
        win_sum = _chip_sum("chip_sum_w_in", dist["chip"], core, g_mine, g_from_sibling)
    else:
        g_win = _matmul(
            "mm_gwin", hn, dproj, pl.BlockSpec((s, tmw), lambda i, j, k: (0, i)),
            pl.BlockSpec((s, tn_in), lambda i, j, k: (0, j)),
            (d // tmw, d_in // tn_in, 1), None, jax.ShapeDtypeStruct((N_CHIPS, d, c4), F32),
            _shard_spec(tmw, tn_in, per_in, 0, 1), ta=True)
    tk = c4
    tmh, tnh = _tile(s, 1024), _tile(d, 1024)
    dhn = _matmul(
        "mm_dhn", dproj, win4, pl.BlockSpec((tmh, tk), lambda i, j, k: (i, k)), _shard_spec(tnh, tk, 1, 1, 2),
        (s // tmh, d // tnh, d_in // tk), (tmh, tnh), jax.ShapeDtypeStruct((s, d), BF16),
        pl.BlockSpec((tmh, tnh), lambda i, j, k: (i, j)), tb=True,
        comm=_ExchangeChips([win_sum[1]]) if dist else None)
    if dist:
        dhn, win_got = dhn
    grad_x, g_g_pre = _rms_bwd("rms_pre_bwd", dhn, xs, rx, sp["g_pre"], dh, False)

    small = {
        "g_pre": g_g_pre,
        "w_s": g_w_s,
        "b_s": jnp.sum(dz_sum.reshape(CHUNK, heads, HEAD_DIM), axis=-1).T,
        "ln_v_g": g_ln_g, "ln_v_b": g_ln_b,
        "g_q": g_g_q, "g_k": g_g_k,
        "rel_bias": dbias[:, :, 0].T,
        "g_out_a": g_g_out_a, "g_out_b": g_g_out_b,
        "g_ple": g_g_ple,
    }
    if not dist:
        return loss, grad_x, (g_win, *early), small
    sums32 = [win_sum[0]] + [s32 for s32, _ in early_sums]
    return loss, grad_x, (sums32, list(win_got) + list(early_got)), small


def _place():
    x, y, c = lax.axis_index("x"), lax.axis_index("y"), lax.axis_index("c")
    chips = [(1 - x, y), (x, 1 - y), (1 - x, 1 - y)]
    return x, y, c, chips


def _remote(src, dst, send_sems, recv_sems, k, to):
    return pltpu.make_async_remote_copy(src_ref=src, dst_ref=dst, send_sem=send_sems.at[k], recv_sem=recv_sems.at[k],
                                        device_id=to, device_id_type=MESH)


def _cast_bf16(name, chip, w):
    r, c = w.shape
    tm = _tile(r, 256)

    def body(chip_ref, w_ref, o_ref):
        del chip_ref
        o_ref[...] = w_ref[...].astype(BF16)

    return pl.pallas_call(
        body, name=name,
        grid_spec=pltpu.PrefetchScalarGridSpec(
            num_scalar_prefetch=1, grid=(r // tm,),
            in_specs=[pl.BlockSpec((tm, c), lambda i, chip_ref: (i, 0))],
            out_specs=pl.BlockSpec((None, tm, c), lambda i, chip_ref: (chip_ref[0], i, 0))),
        out_shape=jax.ShapeDtypeStruct((N_CHIPS, r, c), BF16), compiler_params=_params("parallel"),
    )(chip, w)


class _Gather:
    in_place = True

    def __init__(self, fulls):
        self.ins = list(fulls)
        self.out_shape = [jax.ShapeDtypeStruct(f.shape, f.dtype) for f in fulls]
        n = len(fulls)
        self.scratch = [pltpu.SemaphoreType.DMA((6 * n,)), pltpu.SemaphoreType.DMA((6 * n,))]

    @staticmethod
    def _sends(outs, sems):
        send_sems, recv_sems = sems
        x, y, c, chips = _place()
        cps = []
        for w, ref in enumerate(outs):
            half = ref.shape[1] // 2
            blk = ref.at[2 * x + y, pl.ds(c * half, half)]
            cps += [_remote(blk, blk, send_sems, recv_sems, 6 * w + q, (*chip, c)) for q, chip in enumerate(chips)]
        return cps

    def start(self, ins, outs, sems):
        for cp in self._sends(outs, sems):
            cp.start()

    def finish(self, ins, outs, sems):
        send_sems, recv_sems = sems
        x, y, c, chips = _place()
        sibling = (x, y, 1 - c)
        forwards = []
        for w, ref in enumerate(outs):
            half = ref.shape[1] // 2
            for q, chip in enumerate(chips):
                blk = ref.at[2 * chip[0] + chip[1], pl.ds(c * half, half)]
                _remote(blk, blk, send_sems, recv_sems, 6 * w + q, sibling).wait_recv()
                fwd = _remote(blk, blk, send_sems, recv_sems, 6 * w + 3 + q, sibling)
                fwd.start()
                forwards.append(fwd)
        for w, ref in enumerate(outs):
            half = ref.shape[1] // 2
            for q, chip in enumerate(chips):
                blk = ref.at[2 * chip[0] + chip[1], pl.ds((1 - c) * half, half)]
                _remote(blk, blk, send_sems, recv_sems, 6 * w + 3 + q, sibling).wait_recv()
        for cp in self._sends(outs, sems) + forwards:
            cp.wait_send()


class _ExchangeChips:
    in_place = False

    def __init__(self, sums16):
        self.ins = list(sums16)
        self.out_shape = [jax.ShapeDtypeStruct(g.shape, g.dtype) for g in sums16]
        n = len(sums16)
        self.scratch = [pltpu.SemaphoreType.DMA((3 * n,)), pltpu.SemaphoreType.DMA((3 * n,))]

    @staticmethod
    def _sends(ins, outs, sems):
        send_sems, recv_sems = sems
        x, y, c, chips = _place()
        return [_remote(ins[w].at[2 * chip[0] + chip[1]], outs[w].at[2 * x + y], send_sems, recv_sems, 3 * w + q, (*chip, c))
                for w in range(len(ins)) for q, chip in enumerate(chips)]

    def start(self, ins, outs, sems):
        for cp in self._sends(ins, outs, sems):
            cp.start()

    def finish(self, ins, outs, sems):
        send_sems, recv_sems = sems
        x, y, c, chips = _place()
        for w in range(len(ins)):
            for q, chip in enumerate(chips):
                blk = outs[w].at[2 * chip[0] + chip[1]]
                _remote(blk, blk, send_sems, recv_sems, 3 * w + q, (*chip, c)).wait_recv()
        for cp in self._sends(ins, outs, sems):
            cp.wait_send()


def _mm_in_gather(chip, hn, win4):
    s, d = hn.shape
    c4 = win4.shape[2]
    tm, tn = _tile(s, 1024), _tile(c4, 1024)
    per = c4 // tn
    nj, gm = N_CHIPS * per, s // tm
    half = d // 2

    def tile_of(j):
        jj = j - per
        o = jnp.where(j < per, 0, jnp.where(j < 3 * per, 1 + jj % 2, 3))
        t = jnp.where(j < per, j, jnp.where(j < 3 * per, jj // 2, j - 3 * per))
        return o, t

    def step_of(o, t):
        return per + 2 * t + o - 1 if o in (1, 2) else o * per + t

    def block_of(j, chip_ref):
        o, _ = tile_of(j)
        return chip_ref[0] ^ (((o & 1) << 1) | (o >> 1))

    i_late = max(gm - 2, 0)

    def body(chip_ref, hn_ref, w_any, proj_ref, w_ref, wbuf, wsem, send_sems, recv_sems):
        del w_any
        j, i = pl.program_id(0), pl.program_id(1)
        x, y, c, chips = _place()
        sibling = (x, y, 1 - c)

        def region(k, rows_half, t):
            return w_ref.at[k, pl.ds(rows_half * half, half), pl.ds(t * tn, tn)]

        def send_mine(q, t):
            blk = region(2 * x + y, c, t)
            return _remote(blk, blk, send_sems, recv_sems, q * per + t, (*chips[q], c))

        def forward(q, t):
            blk = region(2 * chips[q][0] + chips[q][1], c, t)
            return _remote(blk, blk, send_sems, recv_sems, (3 + q) * per + t, sibling)

        def fetch(jj, mine, slot):
            rows = pl.ds((c if mine else 1 - c) * half, half)
            cols = pl.ds(pl.multiple_of(tile_of(jj)[1] * tn, LANE), tn)
            return pltpu.make_async_copy(w_ref.at[block_of(jj, chip_ref), rows, cols], wbuf.at[slot, rows],
                                         wsem.at[2 * slot + (0 if mine else 1)])

        def foreign_tiles():
            return [(o, t) for o in range(1, N_CHIPS) for t in range(per)]

        relayed = [(0, t) for t in range(1, per, 2)] + [(1, t) for t in range(0, per, 2)]

        def relay(q, t):
            blk = region(2 * chips[q][0] + chips[q][1], c, t)
            return _remote(blk, blk, send_sems, recv_sems, 2 * per + t, (*chips[1 - q], c))

        def landed(q, t):
            blk = region(2 * chips[q][0] + chips[q][1], c, t)
            _remote(blk, blk, send_sems, recv_sems, q * per + t, sibling).wait_recv()
            forward(q, t).start()
            if (q, t) in relayed:
                relay(q, t).start()

        @pl.when((j == 0) & (i == 0))
        def _():
            for t in range(per):
                send_mine(0, t).start()
                send_mine(1, t).start()
            fetch(0, True, 0).start()
            fetch(0, False, 0).start()

        nxt = j + 1

        @pl.when((i == 0) & (nxt < nj))
        def _():
            for o, t in foreign_tiles():
                @pl.when(nxt == step_of(o, t))
                def _():
                    if o == 1:
                        landed(0, t)
                        landed(1, t)
                    elif o == 3:
                        landed(2, t)

            fetch(nxt, True, nxt % 2).start()

            @pl.when(nxt < per)
            def _():
                fetch(nxt, False, nxt % 2).start()

        @pl.when((i == i_late) & (nxt < nj))
        def _():
            for o, t in foreign_tiles():
                @pl.when(nxt == step_of(o, t))
                def _():
                    blk = region(2 * chips[o - 1][0] + chips[o - 1][1], 1 - c, t)
                    _remote(blk, blk, send_sems, recv_sems, (3 + o - 1) * per + t, sibling).wait_recv()
                    fetch(nxt, False, nxt % 2).start()

        @pl.when(i == 0)
        def _():
            fetch(j, True, j % 2).wait()
            fetch(j, False, j % 2).wait()

        proj_ref[...] = _dot(hn_ref[...], wbuf[j % 2])

        @pl.when((j == nj - 1) & (i == gm - 1))
        def _():
            for t in range(per):
                send_mine(0, t).wait_send()
                send_mine(1, t).wait_send()
                for q in range(3):
                    forward(q, t).wait_send()
            for q, t in relayed:
                relay(q, t).wait_send()

    return pl.pallas_call(
        body, name="mm_in_gather",
        grid_spec=pltpu.PrefetchScalarGridSpec(
            num_scalar_prefetch=1, grid=(nj, gm),
            in_specs=[pl.BlockSpec((tm, d), lambda j, i, chip_ref: (i, 0)), ANY],
            out_specs=[pl.BlockSpec((tm, tn), lambda j, i, chip_ref: (i, block_of(j, chip_ref) * per + tile_of(j)[1])), ANY],
            scratch_shapes=[pltpu.VMEM((2, d, tn), BF16), pltpu.SemaphoreType.DMA((4,)),
                            pltpu.SemaphoreType.DMA((6 * per,)), pltpu.SemaphoreType.DMA((6 * per,))]),
        out_shape=[jax.ShapeDtypeStruct((s, N_CHIPS * c4), F32), jax.ShapeDtypeStruct(win4.shape, win4.dtype)],
        input_output_aliases={2: 1},
        compiler_params=_params("arbitrary", "arbitrary"),
    )(chip, hn, win4)


class _SwapHalves:
    in_place = False

    def __init__(self, grads, whole=False):
        self.ins = list(grads)
        self.whole = whole
        self.out_shape = [jax.ShapeDtypeStruct((N_CHIPS, g.shape[1] // (1 if whole else 2), g.shape[2]), g.dtype)
                          for g in grads]
        self.scratch = [pltpu.SemaphoreType.DMA((len(grads),)), pltpu.SemaphoreType.DMA((len(grads),))]

    def _copies(self, ins, outs, sems):
        x, y, c, _ = _place()
        cps = []
        for w in range(len(ins)):
            half = ins[w].shape[1] // 2
            src = ins[w] if self.whole else ins[w].at[:, pl.ds((1 - c) * half, half)]
            cps.append(_remote(src, outs[w], sems[0], sems[1], w, (x, y, 1 - c)))
        return cps

    def start(self, ins, outs, sems):
        for cp in self._copies(ins, outs, sems):
            cp.start()

    def finish(self, ins, outs, sems):
        for cp in self._copies(ins, outs, sems):
            cp.wait()


def _chip_sum(name, chip, core, grad, got):
    _, half, c = got.shape
    th = _tile(half, 128)
    n = half // th
    skip = n if grad.shape[1] != half else 0

    def body(chip_ref, core_ref, g_ref, o_ref, s_ref):
        del chip_ref, core_ref
        s_ref[...] = (g_ref[...] + o_ref[...]).astype(s_ref.dtype)

    def specs(first, out_block):
        def block(k, i, chip_ref):
            return (chip_ref[0] + first + k) % N_CHIPS

        return dict(
            in_specs=[pl.BlockSpec((None, th, c), lambda k, i, chip_ref, core_ref: (block(k, i, chip_ref), core_ref[0] * skip + i, 0)),
                      pl.BlockSpec((None, th, c), lambda k, i, chip_ref, core_ref: (block(k, i, chip_ref), i, 0))],
            out_specs=out_block(block))

    own = pl.pallas_call(
        body, name=name + "_own",
        grid_spec=pltpu.PrefetchScalarGridSpec(
            num_scalar_prefetch=2, grid=(1, n),
            **specs(0, lambda block: pl.BlockSpec((th, c), lambda k, i, chip_ref, core_ref: (i, 0)))),
        out_shape=jax.ShapeDtypeStruct((half, c), F32), compiler_params=_params("parallel", "parallel"),
    )(chip, core, grad, got)
    others = pl.pallas_call(
        functools.partial(body), name=name + "_others",
        grid_spec=pltpu.PrefetchScalarGridSpec(
            num_scalar_prefetch=2, grid=(N_CHIPS - 1, n),
            **specs(1, lambda block: pl.BlockSpec(
                (None, th, c), lambda k, i, chip_ref, core_ref: (block(k, i, chip_ref), i, 0)))),
        out_shape=jax.ShapeDtypeStruct((N_CHIPS, half, c), BF16), compiler_params=_params("parallel", "parallel"),
    )(chip, core, grad, got)
    return own, others


def _riding(comm, grid, body, n_pre, n_in, n_out):
    if comm is None:
        return body
    c_in, c_out, c_scr = len(comm.ins), len(comm.out_shape), len(comm.scratch)

    def wrapped(*refs):
        pre, rest = refs[:n_pre], refs[n_pre:]
        ins, rest = rest[:n_in], rest[n_in:]
        cin, rest = rest[:c_in], rest[c_in:]
        outs, rest = rest[:n_out], rest[n_out:]
        cout, rest = rest[:c_out], rest[c_out:]
        own, sems = rest[:len(rest) - c_scr], rest[len(rest) - c_scr:]
        ids = [pl.program_id(ax) for ax in range(len(grid))]
        first, last = ids[0] == 0, ids[0] == grid[0] - 1
        for ax in range(1, len(grid)):
            first, last = first & (ids[ax] == 0), last & (ids[ax] == grid[ax] - 1)

        @pl.when(first)
        def _():
            comm.start(cin, cout, sems)

        body(*pre, *ins, *outs, *own)

        @pl.when(last)
        def _():
            comm.finish(cin, cout, sems)

    return wrapped


def _total(name, chip, core, sum32, got16, comm=None):
    half, c = sum32.shape
    th = _tile(half, 128)
    n = half // th

    def body(chip_ref, core_ref, own_ref, a_ref, b_ref, c_ref, o_ref):
        del chip_ref, core_ref
        o_ref[...] = ((own_ref[...] + a_ref[...].astype(F32)) + b_ref[...].astype(F32)) + c_ref[...].astype(F32)

    def other(step):
        return pl.BlockSpec((None, th, c), lambda i, chip_ref, core_ref: ((chip_ref[0] + step) % N_CHIPS, i, 0))

    c_ins = list(comm.ins) if comm else []
    res = pl.pallas_call(
        _riding(comm, (n,), body, 2, 4, 1), name=name,
        grid_spec=pltpu.PrefetchScalarGridSpec(
            num_scalar_prefetch=2, grid=(n,),
            in_specs=[pl.BlockSpec((th, c), lambda i, chip_ref, core_ref: (i, 0)), other(1), other(2), other(3)]
            + [ANY] * len(c_ins),
            out_specs=[pl.BlockSpec((th, c), lambda i, chip_ref, core_ref: (core_ref[0] * n + i, 0))]
            + [ANY] * len(c_ins),
            scratch_shapes=list(comm.scratch) if comm else []),
        out_shape=[jax.ShapeDtypeStruct((2 * half, c), F32)] + (list(comm.out_shape) if comm else []),
        input_output_aliases={6 + i: 1 + i for i in range(len(c_ins))} if comm else {},
        compiler_params=_params("arbitrary" if comm else "parallel"),
    )(chip, core, sum32, got16, got16, got16, *c_ins)
    return (res[0], res[1:]) if comm else res[0]


class _JoinHalves:
    in_place = True

    def __init__(self, blocks):
        self.ins = list(blocks)
        self.out_shape = [jax.ShapeDtypeStruct(t.shape, t.dtype) for t in blocks]
        self.scratch = [pltpu.SemaphoreType.DMA((len(blocks),)), pltpu.SemaphoreType.DMA((len(blocks),))]

    @staticmethod
    def _copy(ref, w, mine, sems):
        x, y, c, _ = _place()
        half = ref.shape[0] // 2
        rows = ref.at[pl.ds((c if mine else 1 - c) * half, half)]
        return _remote(rows, rows, sems[0], sems[1], w, (x, y, 1 - c))

    def start(self, ins, outs, sems):
        for w, ref in enumerate(outs):
            self._copy(ref, w, True, sems).start()

    def finish(self, ins, outs, sems):
        for w, ref in enumerate(outs):
            self._copy(ref, w, True, sems).wait_send()
            self._copy(ref, w, False, sems).wait_recv()


class _Both:
    in_place = True

    def __init__(self, a, b):
        assert a.in_place and b.in_place
        self.parts = (a, b)
        self.ins = list(a.ins) + list(b.ins)
        self.out_shape = list(a.out_shape) + list(b.out_shape)
        self.scratch = list(a.scratch) + list(b.scratch)

    def _split(self, ins, outs, sems):
        a = self.parts[0]
        n, k = len(a.ins), len(a.scratch)
        return (a, ins[:n], outs[:n], sems[:k]), (self.parts[1], ins[n:], outs[n:], sems[k:])

    def start(self, ins, outs, sems):
        for part, i, o, s in self._split(ins, outs, sems):
            part.start(i, o, s)

    def finish(self, ins, outs, sems):
        for part, i, o, s in self._split(ins, outs, sems):
            part.finish(i, o, s)


def _alone(name, comm):
    n = len(comm.ins)

    def body(*refs):
        comm.start(refs[:n], refs[n:2 * n], refs[2 * n:])
        comm.finish(refs[:n], refs[n:2 * n], refs[2 * n:])

    return pl.pallas_call(
        body, name=name, in_specs=[ANY] * n, out_specs=[ANY] * n, out_shape=comm.out_shape,
        scratch_shapes=comm.scratch, input_output_aliases={i: i for i in range(n)},
    )(*comm.ins)


class _GatherSmall:
    in_place = True

    def __init__(self, full):
        self.ins = [full]
        self.out_shape = [jax.ShapeDtypeStruct(full.shape, full.dtype)]
        self.scratch = [pltpu.SemaphoreType.DMA((7,)), pltpu.SemaphoreType.DMA((7,))]

    @staticmethod
    def _copy(ref, sems, k, block, to):
        m_per = ref.shape[0] // N_DEV
        px, py, pc = block
        rows = ref.at[pl.ds((4 * px + 2 * py + pc) * m_per, m_per), :]
        return _remote(rows, rows, sems[0], sems[1], k, to)

    def _first(self, ref, sems):
        x, y, c, chips = _place()
        me = (x, y, c)
        return [self._copy(ref, sems, 0, me, (x, y, 1 - c))] + \
               [self._copy(ref, sems, 1 + j, me, (*chip, c)) for j, chip in enumerate(chips)]

    def start(self, ins, outs, sems):
        for cp in self._first(outs[0], sems):
            cp.start()

    def finish(self, ins, outs, sems):
        ref = outs[0]
        x, y, c, chips = _place()
        me, sibling = (x, y, c), (x, y, 1 - c)
        passed = [self._copy(ref, sems, 4 + j, (*chip, c), sibling) for j, chip in enumerate(chips)]
        for j, chip in enumerate(chips):
            self._copy(ref, sems, 1 + j, (*chip, c), me).wait_recv()
            passed[j].start()
        self._copy(ref, sems, 0, sibling, me).wait_recv()
        for j, chip in enumerate(chips):
            self._copy(ref, sems, 4 + j, (*chip, 1 - c), me).wait_recv()
        for cp in self._first(ref, sems) + passed:
            cp.wait_send()


def _adamw_math(w, g, m, v):
    m = ADAM_B1 * m + (1.0 - ADAM_B1) * g
    v = ADAM_B2 * v + (1.0 - ADAM_B2) * (g * g)
    m_hat = m / (1.0 - ADAM_B1 ** ADAM_STEP)
    v_hat = v / (1.0 - ADAM_B2 ** ADAM_STEP)
    delta = -ADAM_LR * (m_hat / (jnp.sqrt(v_hat) + ADAM_EPS) + ADAM_WD * w)
    return delta, m, v


def _adamw(name, w, g, m, v):
    r, c = w.shape
    tm = _tile(r, 128)

    def body(w_ref, g_ref, m_ref, v_ref, g_out, d_out, m_out, v_out):
        g = g_ref[...]
        g_out[...] = g
        d_out[...], m_out[...], v_out[...] = _adamw_math(w_ref[...], g, m_ref[...], v_ref[...])

    spec = pl.BlockSpec((tm, c), lambda i: (i, 0))
    return pl.pallas_call(
        body, name=name, grid=(r // tm,), in_specs=[spec] * 4, out_specs=[spec] * 4,
        out_shape=[jax.ShapeDtypeStruct((r, c), F32)] * 4, compiler_params=_params("parallel"),
    )(w, g, m, v)


def _adamw_small(gathered, w, m, v):
    rows = w.shape[0]

    def body(all_ref, w_ref, m_ref, v_ref, g_out, d_out, m_out, v_out):
        g = all_ref[0:rows, :]
        for dev in range(1, N_DEV):
            g = g + all_ref[dev * rows:(dev + 1) * rows, :]
        g_out[...] = g
        d_out[...], m_out[...], v_out[...] = _adamw_math(w_ref[...], g, m_ref[...], v_ref[...])

    return pl.pallas_call(
        body, name="adamw_small", out_shape=[jax.ShapeDtypeStruct(w.shape, F32)] * 4, compiler_params=_params(),
    )(gathered, w, m, v)


SMALL = ("g_pre", "w_s", "b_s", "ln_v_g", "ln_v_b", "g_q", "g_k", "rel_bias", "g_out_a", "g_out_b", "g_ple")
LARGE = ("w_in", "w_out", "w_ple_gate", "w_ple_up")
WEIGHTS = ("g_pre", "w_in", "w_s", "b_s", "ln_v_g", "ln_v_b", "g_q", "g_k", "rel_bias", "g_out_a", "g_out_b", "w_out",
           "g_ple", "w_ple_gate", "w_ple_up")


def _pack(parts):
    flat = jnp.concatenate([parts[n].reshape(-1).astype(F32) for n in SMALL])
    rows = -(-flat.shape[0] // (8 * LANE)) * 8
    return jnp.pad(flat, (0, rows * LANE - flat.shape[0])).reshape(rows, LANE)


def _unpack(pack, like):
    flat = pack.reshape(-1)
    out, at = {}, 0
    for n in SMALL:
        size = math.prod(like[n].shape)
        out[n] = flat[at:at + size].reshape(like[n].shape)
        at += size
    return out


def kernel(x, p, g_pre, w_in, w_s, b_s, ln_v_g, ln_v_b, g_q, g_k, rel_bias, g_out_a, g_out_b, w_out, g_ple, w_ple_gate, w_ple_up, loss_target, m_g_pre, m_w_in, m_w_s, m_b_s, m_ln_v_g, m_ln_v_b, m_g_q, m_g_k, m_rel_bias, m_g_out_a, m_g_out_b, m_w_out, m_g_ple, m_w_ple_gate, m_w_ple_up, v_g_pre, v_w_in, v_w_s, v_b_s, v_ln_v_g, v_ln_v_b, v_g_q, v_g_k, v_rel_bias, v_g_out_a, v_g_out_b, v_w_out, v_g_ple, v_w_ple_gate, v_w_ple_up):
    given = dict(locals())
    weights = {n: given[n] for n in WEIGHTS}
    mom_m = {n: given["m_" + n] for n in WEIGHTS}
    mom_v = {n: given["v_" + n] for n in WEIGHTS}
    xs, ps, tgt = x[0], p[0, 0], loss_target[0]
    d = xs.shape[1]

    core = lax.axis_index("c").astype(jnp.int32).reshape(1)
    chip = (2 * lax.axis_index("x") + lax.axis_index("y")).astype(jnp.int32).reshape(1)

    win4, wout4, wgate4, wup4 = [_cast_bf16("cast_" + n, chip, weights[n][0]) for n in LARGE]

    sp = {
        "g_pre": g_pre, "w_s": w_s[0], "b_s": b_s[0], "ln_v_g": ln_v_g, "ln_v_b": ln_v_b, "g_q": g_q, "g_k": g_k,
        "rel_bias": rel_bias, "g_out_a": g_out_a, "g_out_b": g_out_b, "g_ple": g_ple,
    }
    loss_local, grad_x, (sums32, got16), small = _local_step(xs, ps, tgt, sp, win4, wout4, wgate4, wup4,
                                                             dist={"chip": chip, "core": core})
    loss = lax.psum(loss_local, MESH_AXES)

    own = dict(zip(LARGE, sums32))
    got = dict(zip(LARGE, got16))
    halves = [_total("total_" + n, chip, core, own[n], got[n]) for n in LARGE[1:]]
    my_pack = _pack(small)
    device = 2 * chip[0] + core[0]
    everyone = lax.dynamic_update_slice(jnp.zeros((N_DEV * my_pack.shape[0], LANE), F32), my_pack,
                                        (device * my_pack.shape[0], 0))
    half_in, (*joined, gathered) = _total("total_w_in", chip, core, own["w_in"], got["w_in"],
                                          comm=_Both(_JoinHalves(halves), _GatherSmall(everyone)))
    grads = dict(zip(LARGE[1:], joined))
    grads["w_in"], = _alone("join_w_in", _JoinHalves([half_in]))

    out_g, out_d, out_m, out_v = {}, {}, {}, {}
    for n in LARGE:
        results = _adamw("adamw_" + n, weights[n][0], grads[n], mom_m[n][0], mom_v[n][0])
        out_g[n], out_d[n], out_m[n], out_v[n] = [r[None] for r in results]

    pg, pd, pm, pv = _adamw_small(gathered, _pack(weights), _pack(mom_m), _pack(mom_v))
    for packed, out in ((pg, out_g), (pd, out_d), (pm, out_m), (pv, out_v)):
        out.update(_unpack(packed, weights))

    return (loss, grad_x[None], *[out_g[n] for n in WEIGHTS], *[out_d[n] for n in WEIGHTS],
            *[out_m[n] for n in WEIGHTS], *[out_v[n] for n in WEIGHTS])
```

```python
import functools
import math

import numpy as np

import jax
import jax.numpy as jnp
from jax import lax
from jax.experimental import pallas as pl
from jax.experimental.pallas import tpu as pltpu

F32 = jnp.float32
BF16 = jnp.bfloat16

HEAD_DIM = 128
CHUNK = 128
BLK = 128
DILATED = ((128, 1), (512, 4), (2048, 16))
NUM_BUCKETS = 32
MAX_DISTANCE = 2048
ATTN_GROUP = 8
ATTN_GROUP_FWD = 16
EPS = 1e-6
NEG_INF = -1e30
N_CHIPS = 4
N_DEV = 8
MESH_AXES = ("x", "y", "c")

ADAM_LR = 0.001
ADAM_B1 = 0.9
ADAM_B2 = 0.999
ADAM_EPS = 1e-08
ADAM_WD = 0.01
ADAM_STEP = 10

V7X_VMEM_LIMIT = 58 * 1024 * 1024
LANE = 128
MESH = pl.DeviceIdType.MESH
ANY = pl.BlockSpec(memory_space=pl.ANY)


def _params(*sem):
    return pltpu.CompilerParams(dimension_semantics=sem or None, vmem_limit_bytes=V7X_VMEM_LIMIT)


def _tile(n, target):
    if n <= target:
        return n
    t = (target // LANE) * LANE
    while t > LANE and n % t:
        t -= LANE
    assert n % t == 0, (n, target)
    return t


def _gelu(x):
    return 0.5 * x * (1.0 + lax.erf(x * (1.0 / math.sqrt(2.0))))


def _gelu_and_grad(x):
    cdf = 0.5 * (1.0 + lax.erf(x * (1.0 / math.sqrt(2.0))))
    return x * cdf, cdf + x * jnp.exp(-0.5 * x * x) * (1.0 / math.sqrt(2.0 * math.pi))


def _silu_and_grad(x):
    s = jax.nn.sigmoid(x)
    return x * s, s * (1.0 + x * (1.0 - s))


def _dot(a, b, ta=False, tb=False):
    return lax.dot_general(a, b, (((0 if ta else 1,), (1 if tb else 0,)), ((), ())), preferred_element_type=F32)


def _colsum8(v):
    return jnp.sum(v.reshape(v.shape[0] // 8, 8, v.shape[1]), axis=0)


def _matmul(name, a, b, a_spec, b_spec, grid, acc_shape, out_shape, out_specs, ta=False, tb=False,
            extras=(), extra_specs=(), epilogue=None, comm=None, prefetch=()):
    nk = grid[2]
    n_pre = len(prefetch)
    n_extra = len(extras)
    single = not isinstance(out_shape, (tuple, list))
    outs_shape = (out_shape,) if single else tuple(out_shape)
    outs_specs = (out_specs,) if single else tuple(out_specs)
    n_out = len(outs_shape)
    c_ins = list(comm.ins) if comm else []
    c_outs = list(comm.out_shape) if comm else []
    c_scratch = list(comm.scratch) if comm else []
    n_cin, n_cout = len(c_ins), len(c_outs)

    def finish(acc, extra_refs, out_refs):
        if epilogue is None:
            out_refs[0][...] = acc.astype(out_refs[0].dtype)
        else:
            epilogue(acc, extra_refs, out_refs)

    def body(*refs):
        a_ref, b_ref, *rest = refs[n_pre:]
        extra_refs = rest[:n_extra]
        cin_refs = rest[n_extra:n_extra + n_cin]
        out_refs = rest[n_extra + n_cin:n_extra + n_cin + n_out]
        cout_refs = rest[n_extra + n_cin + n_out:n_extra + n_cin + n_out + n_cout]
        scratch_refs = rest[n_extra + n_cin + n_out + n_cout:]
        ids = [pl.program_id(ax) for ax in range(3)]
        if comm:
            sems = scratch_refs[len(scratch_refs) - len(c_scratch):]

            @pl.when((ids[0] == 0) & (ids[1] == 0) & (ids[2] == 0))
            def _():
                comm.start(cin_refs, cout_refs, sems)

        if nk == 1:
            finish(_dot(a_ref[...], b_ref[...], ta, tb), extra_refs, out_refs)
        else:
            acc_ref = scratch_refs[0]

            @pl.when(ids[2] == 0)
            def _():
                acc_ref[...] = jnp.zeros_like(acc_ref)

            acc_ref[...] += _dot(a_ref[...], b_ref[...], ta, tb)

            @pl.when(ids[2] == nk - 1)
            def _():
                finish(acc_ref[...], extra_refs, out_refs)

        if comm:
            @pl.when((ids[0] == grid[0] - 1) & (ids[1] == grid[1] - 1) & (ids[2] == nk - 1))
            def _():
                comm.finish(cin_refs, cout_refs, sems)

    scratch = ([] if nk == 1 else [pltpu.VMEM(acc_shape, F32)]) + c_scratch
    aliases = {n_pre + 2 + n_extra + i: n_out + i for i in range(n_cin)} if (comm and comm.in_place) else {}
    res = pl.pallas_call(
        body, name=name,
        grid_spec=pltpu.PrefetchScalarGridSpec(
            num_scalar_prefetch=n_pre, grid=grid,
            in_specs=[a_spec, b_spec, *extra_specs] + [ANY] * n_cin,
            out_specs=list(outs_specs) + [ANY] * n_cout, scratch_shapes=scratch),
        out_shape=list(outs_shape) + c_outs,
        input_output_aliases=aliases,
        compiler_params=_params(*(("arbitrary",) * 3 if comm else ("parallel", "parallel", "arbitrary"))),
    )(*prefetch, a, b, *extras, *c_ins)
    if comm:
        main = res[:n_out]
        return (main[0] if single else main), res[n_out:]
    return res[0] if single else res


def _shard_spec(rows, cols, per_shard, row_axis, col_axis):
    def index(i, j, k):
        g = (i, j, k)
        return (g[col_axis] // per_shard, g[row_axis], g[col_axis] % per_shard)
    return pl.BlockSpec((None, rows, cols), index)


def _rms_fwd(name, x, g):
    s, d = x.shape
    tm = _tile(s, 256)

    def body(x_ref, g_ref, y_ref, r_ref):
        xf = x_ref[...]
        r = lax.rsqrt(jnp.mean(xf * xf, axis=-1, keepdims=True) + EPS)
        y_ref[...] = (xf * r * g_ref[...]).astype(BF16)
        r_ref[...] = r

    return pl.pallas_call(
        body, name=name, grid=(s // tm,),
        in_specs=[pl.BlockSpec((tm, d), lambda i: (i, 0)), pl.BlockSpec((1, d), lambda i: (0, 0))],
        out_specs=(pl.BlockSpec((tm, d), lambda i: (i, 0)), pl.BlockSpec((tm, 1), lambda i: (i, 0))),
        out_shape=(jax.ShapeDtypeStruct((s, d), BF16), jax.ShapeDtypeStruct((s, 1), F32)),
        compiler_params=_params("parallel"),
    )(x, g)


def _rms_bwd(name, dy, x, r, g, skip, with_bf16):
    s, d = x.shape
    tm = _tile(s, 256)
    n = s // tm

    def body(dy_ref, x_ref, r_ref, g_ref, skip_ref, *outs):
        dx_ref = outs[0]
        dg_ref = outs[-2]
        acc_ref = outs[-1]
        i = pl.program_id(0)
        dyv, xv, rv = dy_ref[...].astype(F32), x_ref[...], r_ref[...]

        @pl.when(i == 0)
        def _():
            acc_ref[...] = jnp.zeros_like(acc_ref)

        acc_ref[...] += _colsum8(dyv * xv * rv)
        dg = dyv * g_ref[...]
        dx = skip_ref[...].astype(F32) + rv * (dg - xv * (rv * rv) * jnp.mean(dg * xv, axis=-1, keepdims=True))
        dx_ref[...] = dx
        if with_bf16:
            outs[1][...] = dx.astype(BF16)

        @pl.when(i == n - 1)
        def _():
            dg_ref[...] = jnp.sum(acc_ref[...], axis=0, keepdims=True)

    row = pl.BlockSpec((tm, d), lambda i: (i, 0))
    vec = pl.BlockSpec((1, d), lambda i: (0, 0))
    out_specs = [row] + ([row] if with_bf16 else []) + [vec]
    out_shape = [jax.ShapeDtypeStruct((s, d), F32)] + ([jax.ShapeDtypeStruct((s, d), BF16)] if with_bf16 else []) \
        + [jax.ShapeDtypeStruct((1, d), F32)]
    return pl.pallas_call(
        body, name=name, grid=(n,),
        in_specs=[row, row, pl.BlockSpec((tm, 1), lambda i: (i, 0)), vec, row],
        out_specs=out_specs, out_shape=out_shape, scratch_shapes=[pltpu.VMEM((8, d), F32)],
        compiler_params=_params("arbitrary"),
    )(dy, x, r, g, skip)


def _causal(w):
    t = lax.broadcasted_iota(jnp.int32, w.shape, 0)
    s_ = lax.broadcasted_iota(jnp.int32, w.shape, 1)
    return jnp.where(t >= s_, w, 0.0)


def _gmlp_fwd(proj, w_s, b_st, ln_g, ln_b, g_out, d_model):
    s = proj.shape[0]
    wa = d_model // 2
    groups = wa // HEAD_DIM
    tm = _tile(s, 256)
    n_chunks = tm // CHUNK

    def body(au_ref, av_ref, az_ref, ws_ref, bst_ref, lng_ref, lnb_ref, go_ref,
             y_ref, z_ref, vn_ref, mu_ref, rs_ref, ra_ref):
        gv = _gelu(av_ref[...])
        mu = jnp.mean(gv, axis=-1, keepdims=True)
        xc = gv - mu
        rs = lax.rsqrt(jnp.mean(xc * xc, axis=-1, keepdims=True) + EPS)
        vn = (xc * rs * lng_ref[...] + lnb_ref[...]).astype(BF16)
        vn_ref[...] = vn
        mu_ref[...] = mu
        rs_ref[...] = rs
        for g in range(groups):
            wm = _causal(ws_ref[g]).astype(BF16)
            cols = slice(g * HEAD_DIM, (g + 1) * HEAD_DIM)
            for ch in range(n_chunks):
                rows = slice(ch * CHUNK, (ch + 1) * CHUNK)
                z_ref[rows, cols] = _dot(wm, vn_ref[rows, cols]) + bst_ref[:, g:g + 1]
        ya = _gelu(au_ref[...]) * z_ref[...]
        ra = lax.rsqrt(jnp.mean(ya * ya, axis=-1, keepdims=True) + EPS)
        ra_ref[...] = ra
        sz, _ = _silu_and_grad(az_ref[...])
        y_ref[...] = (ya * ra * go_ref[...] * sz).astype(BF16)

    def col(j):
        return pl.BlockSpec((tm, wa), lambda i: (i, j))
    vec = pl.BlockSpec((1, wa), lambda i: (0, 0))
    stat = pl.BlockSpec((tm, 1), lambda i: (i, 0))
    return pl.pallas_call(
        body, name="gmlp_fwd", grid=(s // tm,),
        in_specs=[col(0), col(1), col(2),
                  pl.BlockSpec((groups, CHUNK, CHUNK), lambda i: (0, 0, 0)),
                  pl.BlockSpec((CHUNK, groups), lambda i: (0, 0)), vec, vec, vec],
        out_specs=(col(0), col(0), col(0), stat, stat, stat),
        out_shape=(jax.ShapeDtypeStruct((s, d_model), BF16), jax.ShapeDtypeStruct((s, wa), F32),
                   jax.ShapeDtypeStruct((s, wa), BF16), jax.ShapeDtypeStruct((s, 1), F32),
                   jax.ShapeDtypeStruct((s, 1), F32), jax.ShapeDtypeStruct((s, 1), F32)),
        compiler_params=_params("parallel"),
    )(proj, proj, proj, w_s, b_st, ln_g, ln_b, g_out)


def _gmlp_bwd(dproj, dy, proj, z, vn, mu, rs, ra, w_s, ln_g, g_out, d_model):
    s = proj.shape[0]
    wa = d_model // 2
    groups = wa // HEAD_DIM
    tm = _tile(s, 256)
    n_chunks = tm // CHUNK
    n = s // tm

    def causal_stack(w):
        t = lax.broadcasted_iota(jnp.int32, w.shape, 1)
        s_ = lax.broadcasted_iota(jnp.int32, w.shape, 2)
        return jnp.where(t >= s_, w, 0.0)

    def body(dproj_in, dy_ref, au_ref, av_ref, az_ref, z_ref, vn_ref, mu_ref, rs_ref, ra_ref, ws_ref, lng_ref, go_ref,
             dp_ref, gws_ref, dzs_ref, glg_ref, glb_ref, ggo_ref,
             dz_s, dvn_s, acc_lg, acc_lb, acc_go):
        del dproj_in
        i = pl.program_id(0)

        @pl.when(i == 0)
        def _():
            gws_ref[...] = jnp.zeros_like(gws_ref)
            dzs_ref[...] = jnp.zeros_like(dzs_ref)
            acc_lg[...] = jnp.zeros_like(acc_lg)
            acc_lb[...] = jnp.zeros_like(acc_lb)
            acc_go[...] = jnp.zeros_like(acc_go)

        au, az, zv, rav = au_ref[...], az_ref[...], z_ref[...], ra_ref[...]
        u, du = _gelu_and_grad(au)
        ya = u * zv
        sz, dsz = _silu_and_grad(az)
        dyv = dy_ref[...].astype(F32)
        dp_ref[:, 2 * wa:3 * wa] = (dyv * (ya * rav * go_ref[...]) * dsz).astype(BF16)
        dn = dyv * sz
        acc_go[...] += _colsum8(dn * ya * rav)
        dyg = dn * go_ref[...]
        dya = rav * (dyg - ya * (rav * rav) * jnp.mean(dyg * ya, axis=-1, keepdims=True))
        dp_ref[:, 0:wa] = (dya * zv * du).astype(BF16)
        dz_s[...] = dya * u
        for ch in range(n_chunks):
            dzs_ref[...] += dz_s[ch * CHUNK:(ch + 1) * CHUNK, :]
        for g in range(groups):
            wm = _causal(ws_ref[g]).astype(BF16)
            cols = slice(g * HEAD_DIM, (g + 1) * HEAD_DIM)
            for ch in range(n_chunks):
                rows = slice(ch * CHUNK, (ch + 1) * CHUNK)
                dzb = dz_s[rows, cols].astype(BF16)
                gws_ref[g] += _dot(dzb, vn_ref[rows, cols], tb=True)
                dvn_s[rows, cols] = _dot(wm, dzb, ta=True)
        av = av_ref[...]
        gv, dgelu_v = _gelu_and_grad(av)
        xh = (gv - mu_ref[...]) * rs_ref[...]
        dvn = dvn_s[...]
        acc_lb[...] += _colsum8(dvn)
        acc_lg[...] += _colsum8(dvn * xh)
        dxh = dvn * lng_ref[...]
        dgv = rs_ref[...] * (dxh - jnp.mean(dxh, axis=-1, keepdims=True) - xh * jnp.mean(dxh * xh, axis=-1, keepdims=True))
        dp_ref[:, wa:2 * wa] = (dgv * dgelu_v).astype(BF16)

        @pl.when(i == n - 1)
        def _():
            gws_ref[...] = causal_stack(gws_ref[...])
            glg_ref[...] = jnp.sum(acc_lg[...], axis=0, keepdims=True)
            glb_ref[...] = jnp.sum(acc_lb[...], axis=0, keepdims=True)
            ggo_ref[...] = jnp.sum(acc_go[...], axis=0, keepdims=True)

    def col(j):
        return pl.BlockSpec((tm, wa), lambda i: (i, j))
    vec = pl.BlockSpec((1, wa), lambda i: (0, 0))
    stat = pl.BlockSpec((tm, 1), lambda i: (i, 0))
    ws_spec = pl.BlockSpec((groups, CHUNK, CHUNK), lambda i: (0, 0, 0))
    d_in = dproj.shape[1]
    return pl.pallas_call(
        body, name="gmlp_bwd", grid=(n,),
        in_specs=[ANY, col(0), col(0), col(1), col(2), col(0), col(0), stat, stat, stat, ws_spec, vec, vec],
        out_specs=(pl.BlockSpec((tm, 3 * wa), lambda i: (i, 0)), ws_spec,
                   pl.BlockSpec((CHUNK, wa), lambda i: (0, 0)), vec, vec, vec),
        out_shape=(jax.ShapeDtypeStruct((s, d_in), BF16),
                   jax.ShapeDtypeStruct((groups, CHUNK, CHUNK), F32), jax.ShapeDtypeStruct((CHUNK, wa), F32))
        + (jax.ShapeDtypeStruct((1, wa), F32),) * 3,
        scratch_shapes=[pltpu.VMEM((tm, wa), F32), pltpu.VMEM((tm, wa), F32)] + [pltpu.VMEM((8, wa), F32)] * 3,
        input_output_aliases={0: 0},
        compiler_params=_params("arbitrary"),
    )(dproj, dy, proj, proj, proj, z, vn, mu, rs, ra, w_s, ln_g, g_out)


def _gate_fwd(y_in, yb, proj, g_out, d_model):
    s = proj.shape[0]
    wa = d_model // 2
    tm = _tile(s, 256)

    def body(y_any, yb_ref, bz_ref, go_ref, y_ref, rb_ref):
        del y_any
        ybv = yb_ref[...]
        rb = lax.rsqrt(jnp.mean(ybv * ybv, axis=-1, keepdims=True) + EPS)
        rb_ref[...] = rb
        sz, _ = _silu_and_grad(bz_ref[...])
        y_ref[...] = (ybv * rb * go_ref[...] * sz).astype(BF16)

    return pl.pallas_call(
        body, name="gate_b_fwd", grid=(s // tm,),
        in_specs=[ANY, pl.BlockSpec((tm, wa), lambda i: (i, 0)), pl.BlockSpec((tm, wa), lambda i: (i, 6)),
                  pl.BlockSpec((1, wa), lambda i: (0, 0))],
        out_specs=(pl.BlockSpec((tm, wa), lambda i: (i, 1)), pl.BlockSpec((tm, 1), lambda i: (i, 0))),
        out_shape=(jax.ShapeDtypeStruct((s, d_model), BF16), jax.ShapeDtypeStruct((s, 1), F32)),
        input_output_aliases={0: 0},
        compiler_params=_params("parallel"),
    )(y_in, yb, proj, g_out)


def _gate_bwd(dy, yb, lse, proj, rb, g_out, d_model):
    s, d_in = proj.shape
    wa = d_model // 2
    heads = wa // HEAD_DIM
    tm = _tile(s, 256)
    n = s // tm

    def body(dy_ref, yb_ref, lse_ref, bz_ref, rb_ref, go_ref, dp_ref, do_ref, st_ref, ggo_ref, acc):
        i = pl.program_id(0)

        @pl.when(i == 0)
        def _():
            acc[...] = jnp.zeros_like(acc)

        dyv, ybv, rbv = dy_ref[...].astype(F32), yb_ref[...], rb_ref[...]
        sz, dsz = _silu_and_grad(bz_ref[...])
        dp_ref[...] = (dyv * (ybv * rbv * go_ref[...]) * dsz).astype(BF16)
        dn = dyv * sz
        acc[...] += _colsum8(dn * ybv * rbv)
        dyg = dn * go_ref[...]
        do = rbv * (dyg - ybv * (rbv * rbv) * jnp.mean(dyg * ybv, axis=-1, keepdims=True))
        do_ref[...] = do
        prod = do * ybv
        first = lax.broadcasted_iota(jnp.int32, (tm, 2), 1) == 0
        for h in range(heads):
            delta = jnp.sum(prod[:, h * HEAD_DIM:(h + 1) * HEAD_DIM], axis=-1, keepdims=True)
            st_ref[h] = jnp.where(first, lse_ref[h], delta)

        @pl.when(i == n - 1)
        def _():
            ggo_ref[...] = jnp.sum(acc[...], axis=0, keepdims=True)

    vec = pl.BlockSpec((1, wa), lambda i: (0, 0))
    return pl.pallas_call(
        body, name="gate_b_bwd", grid=(n,),
        in_specs=[pl.BlockSpec((tm, wa), lambda i: (i, 1)), pl.BlockSpec((tm, wa), lambda i: (i, 0)),
                  pl.BlockSpec((heads, tm, 1), lambda i: (0, i, 0)),
                  pl.BlockSpec((tm, wa), lambda i: (i, 6)), pl.BlockSpec((tm, 1), lambda i: (i, 0)), vec],
        out_specs=(pl.BlockSpec((tm, wa), lambda i: (i, 6)), pl.BlockSpec((tm, wa), lambda i: (i, 0)),
                   pl.BlockSpec((heads, tm, 2), lambda i: (0, i, 0)), vec),
        out_shape=(jax.ShapeDtypeStruct((s, d_in), BF16), jax.ShapeDtypeStruct((s, wa), F32),
                   jax.ShapeDtypeStruct((heads, s, 2), F32), jax.ShapeDtypeStruct((1, wa), F32)),
        scratch_shapes=[pltpu.VMEM((8, wa), F32)],
        compiler_params=_params("arbitrary"),
    )(dy, yb, lse, proj, rb, g_out)


def _bucket_tables():
    qi = BLK + np.arange(BLK)
    kj = np.arange(2 * BLK)
    delta = qi[:, None] - kj[None, :]
    max_exact = NUM_BUCKETS // 2
    tabs = []
    for window, dil in DILATED:
        band = (delta >= 0) & (delta <= window // dil)
        dist = np.clip(delta, 0, None) * dil
        d = np.maximum(dist, 1).astype(np.float32)
        large = max_exact + (np.log(d / np.float32(max_exact)) / np.float32(math.log(MAX_DISTANCE / max_exact))
                             * np.float32(NUM_BUCKETS - max_exact)).astype(np.int32)
        large = np.minimum(large, NUM_BUCKETS - 1)
        tabs.append(np.where(band, np.where(dist < max_exact, dist, large), -1).astype(np.int32))
    return np.stack(tabs)


def _bias_tiles(tab_ref, rb_ref, h, bias_s):
    col = lax.broadcasted_iota(jnp.int32, (BLK, 2 * BLK), 1)
    for i in range(len(DILATED)):
        t = tab_ref[i]
        bias = jnp.zeros(t.shape, F32)
        for b in range(NUM_BUCKETS):
            bias = jnp.where(t == b, rb_ref[b, h], bias)
        bias = jnp.where(t >= 0, bias, NEG_INF)
        bias_s[2 * i] = jnp.where(col >= BLK, bias, NEG_INF)
        bias_s[2 * i + 1] = bias


def _block_rows(b, n_blocks, dil):
    nb = n_blocks // dil
    r = b // nb
    n = b % nb
    start = r + dil * BLK * n
    if dil == 1:
        return pl.ds(pl.multiple_of(start, BLK), BLK), n
    return pl.ds(start, BLK, stride=dil), n


def _sub_block(b, rows=BLK, pad=0):
    return pl.ds(pl.multiple_of(b * BLK + pad, BLK), rows)


def _regroup(b, n_blocks, small, big):
    ratio = big // small
    piece = BLK // ratio
    nb_small, nb_big = n_blocks // small, n_blocks // big
    r, n = b // nb_small, b % nb_small
    for q in range(ratio):
        block = (r + small * q) * nb_big + n // ratio
        yield q, ratio, piece, pl.multiple_of(block * BLK + piece * (n % ratio), piece)


def _rms_rows(x, gain):
    return x * lax.rsqrt(jnp.mean(x * x, axis=-1, keepdims=True) + EPS) * gain


def _attn_fwd(proj, g_q, g_k, rel_bias, d_model, comm=None):
    s = proj.shape[0]
    heads = d_model // 2 // HEAD_DIM
    n_blocks = s // BLK
    scale = HEAD_DIM ** -0.5
    tables = jnp.asarray(_bucket_tables())
    c_ins = list(comm.ins) if comm else []
    c_outs = list(comm.out_shape) if comm else []
    c_scratch = list(comm.scratch) if comm else []

    def body(q_ref, k_ref, v_ref, gq_ref, gk_ref, rb_ref, tab_ref, *rest):
        cin_refs, rest = rest[:len(c_ins)], rest[len(c_ins):]
        yb_ref, lse_ref = rest[:2]
        cout_refs, rest = rest[2:2 + len(c_outs)], rest[2 + len(c_outs):]
        qs, ks, vs, qs_w, ks_w, vs_w, tmp, m_s, l_s, acc_s, bias_s, sc_s, p_s = rest[:13]
        sems = rest[13:]
        h = pl.program_id(0)
        if comm:
            @pl.when(h == 0)
            def _():
                comm.start(cin_refs, cout_refs, sems)

        _bias_tiles(tab_ref, rb_ref, h, bias_s)
        gq = gq_ref[...] * scale
        gk = gk_ref[...]
        for buf in (ks, vs, ks_w, vs_w):
            buf[0:BLK, :] = jnp.zeros((BLK, HEAD_DIM), BF16)

        for i, (_, dil) in enumerate(DILATED):
            def prepare(b, carry, i=i, dil=dil):
                rows, _ = _block_rows(b, n_blocks, dil)
                vals = (_rms_rows(q_ref[rows, :], gq), _rms_rows(k_ref[rows, :], gk), v_ref[rows, :])
                for val, buf, pad in zip(vals, (qs, ks, vs), (0, BLK, BLK)):
                    buf[_sub_block(b, pad=pad), :] = val.astype(BF16)
                if i == 1:
                    slot = b % 2
                    for a, val in enumerate(vals):
                        tmp[slot, a] = val
                    for q, ratio, piece, dst in _regroup(b, n_blocks, dil, DILATED[2][1]):
                        for a, (buf, pad) in enumerate(((qs_w, 0), (ks_w, BLK), (vs_w, BLK))):
                            buf[pl.ds(dst + pad, piece), :] = tmp[slot, a, pl.ds(q, piece, stride=ratio), :].astype(BF16)
                return carry
            if i < 2:
                lax.fori_loop(0, n_blocks, prepare, 0, unroll=8)
            qs, ks, vs = (qs, ks, vs) if i < 2 else (qs_w, ks_w, vs_w)

            def group(g, carry, i=i, dil=dil, qs=qs, ks=ks, vs=vs):
                blocks = [g * ATTN_GROUP_FWD + t for t in range(ATTN_GROUP_FWD)]
                for t, b in enumerate(blocks):
                    sc_s[t] = _dot(qs[_sub_block(b), :], ks[_sub_block(b, 2 * BLK), :], tb=True)
                for t, b in enumerate(blocks):
                    rows, n = _block_rows(b, n_blocks, dil)
                    sc = sc_s[t] + bias_s[2 * i + jnp.minimum(n, 1)]
                    m_i = jnp.max(sc, axis=-1, keepdims=True)
                    pr = jnp.exp(sc - m_i)
                    l_i = jnp.sum(pr, axis=-1, keepdims=True)
                    p_s[t] = pr.astype(BF16)
                    m_s[new, rows, :] = m_i
                    l_s[new, rows, :] = l_i
                for t, b in enumerate(blocks):
                    rows, _ = _block_rows(b, n_blocks, dil)
                    acc_s[new, rows, :] = _dot(p_s[t], vs[_sub_block(b, 2 * BLK), :])
                return carry
            new = min(i, 1)
            lax.fori_loop(0, n_blocks // ATTN_GROUP_FWD, group, 0)
            if i > 0:
                m = jnp.maximum(m_s[0], m_s[1])
                w_old, w_new = jnp.exp(m_s[0] - m), jnp.exp(m_s[1] - m)
                m_s[0] = m
                l_s[0] = l_s[0] * w_old + l_s[1] * w_new
                acc_s[0] = acc_s[0] * w_old + acc_s[1] * w_new

        l = l_s[0]
        yb_ref[...] = acc_s[0] / l
        lse_ref[...] = m_s[0] + jnp.log(l)
        if comm:
            @pl.when(h == heads - 1)
            def _():
                comm.finish(cin_refs, cout_refs, sems)

    def head_col(off):
        return pl.BlockSpec((s, HEAD_DIM), lambda h: (0, off * heads + h))
    vec = pl.BlockSpec((1, HEAD_DIM), lambda h: (0, 0))
    res = pl.pallas_call(
        body, name="attn_fwd", grid=(heads,),
        in_specs=[head_col(3), head_col(4), head_col(5), vec, vec,
                  pl.BlockSpec(memory_space=pltpu.SMEM),
                  pl.BlockSpec((len(DILATED), BLK, 2 * BLK), lambda h: (0, 0, 0))] + [ANY] * len(c_ins),
        out_specs=[pl.BlockSpec((s, HEAD_DIM), lambda h: (0, h)), pl.BlockSpec((None, s, 1), lambda h: (h, 0, 0))]
        + [ANY] * len(c_outs),
        out_shape=[jax.ShapeDtypeStruct((s, heads * HEAD_DIM), F32), jax.ShapeDtypeStruct((heads, s, 1), F32)] + c_outs,
        scratch_shapes=[pltpu.VMEM((s, HEAD_DIM), BF16), pltpu.VMEM((s + BLK, HEAD_DIM), BF16),
                        pltpu.VMEM((s + BLK, HEAD_DIM), BF16),
                        pltpu.VMEM((s, HEAD_DIM), BF16), pltpu.VMEM((s + BLK, HEAD_DIM), BF16),
                        pltpu.VMEM((s + BLK, HEAD_DIM), BF16), pltpu.VMEM((2, 3, BLK, HEAD_DIM), F32),
                        pltpu.VMEM((2, s, 1), F32), pltpu.VMEM((2, s, 1), F32), pltpu.VMEM((2, s, HEAD_DIM), F32),
                        pltpu.VMEM((2 * len(DILATED), BLK, 2 * BLK), F32),
                        pltpu.VMEM((ATTN_GROUP_FWD, BLK, 2 * BLK), F32),
                        pltpu.VMEM((ATTN_GROUP_FWD, BLK, 2 * BLK), BF16)] + c_scratch,
        input_output_aliases={7 + i: 2 + i for i in range(len(c_ins))} if (comm and comm.in_place) else {},
        compiler_params=_params("arbitrary"),
    )(proj, proj, proj, g_q, g_k, rel_bias, tables, *c_ins)
    return res[0], res[1], res[2:]


def _attn_bwd(dproj, d_o, stats, proj, g_q, g_k, rel_bias, d_model, comm=None):
    s, d_in = proj.shape
    heads = d_model // 2 // HEAD_DIM
    n_blocks = s // BLK
    scale = HEAD_DIM ** -0.5
    tables = jnp.asarray(_bucket_tables())
    n_dil = len(DILATED)
    c_ins = list(comm.ins) if comm else []
    c_outs = list(comm.out_shape) if comm else []
    c_scratch = list(comm.scratch) if comm else []

    def body(dp_any, q_ref, k_ref, v_ref, do_ref, st_ref, gq_ref, gk_ref, rb_ref, tab_ref, *rest):
        cin_refs, rest = rest[:len(c_ins)], rest[len(c_ins):]
        dp_out, ggq_ref, ggk_ref, db_ref = rest[:4]
        cout_refs, rest = rest[4:4 + len(c_outs)], rest[4 + len(c_outs):]
        (qs, ks, vs, dos, st_s, qs_w, ks_w, vs_w, dos_w, st_w, tmp, tmp_st,
         dqn, dkn, dvv, dq_u, dk_u, dv_u, bias_s, dbias_s, sc_s, dp_s, p_s, ds_s, sems) = rest[:25]
        comm_sems = rest[25:]
        del dp_any
        h = pl.program_id(0)
        if comm:
            @pl.when(h == 0)
            def _():
                comm.start(cin_refs, cout_refs, comm_sems)


        @pl.when(h == 0)
        def _():
            ggq_ref[...] = jnp.zeros_like(ggq_ref)
            ggk_ref[...] = jnp.zeros_like(ggk_ref)

        dbias_s[...] = jnp.zeros_like(dbias_s)
        _bias_tiles(tab_ref, rb_ref, h, bias_s)
        gq = gq_ref[...] * scale
        gk = gk_ref[...]
        for buf in (ks, vs, ks_w, vs_w):
            buf[0:BLK, :] = jnp.zeros((BLK, HEAD_DIM), BF16)

        for i, (_, dil) in enumerate(DILATED):
            def prepare(b, carry, i=i, dil=dil):
                rows, _ = _block_rows(b, n_blocks, dil)
                vals = (_rms_rows(q_ref[rows, :], gq), _rms_rows(k_ref[rows, :], gk), v_ref[rows, :], do_ref[rows, :])
                for val, buf, pad in zip(vals, (qs, ks, vs, dos), (0, BLK, BLK, 0)):
                    buf[_sub_block(b, pad=pad), :] = val.astype(BF16)
                st = st_ref[rows, :]
                st_s[_sub_block(b), :] = st
                if i == 1:
                    slot = b % 2
                    for a, val in enumerate(vals):
                        tmp[slot, a] = val
                    tmp_st[slot] = st
                    for q, ratio, piece, dst in _regroup(b, n_blocks, dil, DILATED[2][1]):
                        for a, (buf, pad) in enumerate(((qs_w, 0), (ks_w, BLK), (vs_w, BLK), (dos_w, 0))):
                            buf[pl.ds(dst + pad, piece), :] = tmp[slot, a, pl.ds(q, piece, stride=ratio), :].astype(BF16)
                        st_w[pl.ds(dst, piece), :] = tmp_st[slot, pl.ds(q, piece, stride=ratio), :]
                return carry
            if i < 2:
                lax.fori_loop(0, n_blocks, prepare, 0, unroll=8)
            dk_u[0:BLK, :] = jnp.zeros((BLK, HEAD_DIM), F32)
            dv_u[0:BLK, :] = jnp.zeros((BLK, HEAD_DIM), F32)
            operands = (qs, ks, vs, dos, st_s) if i < 2 else (qs_w, ks_w, vs_w, dos_w, st_w)

            def group(g, carry, i=i, dil=dil, operands=operands):
                qs, ks, vs, dos, st_s = operands
                blocks = [g * ATTN_GROUP + t for t in range(ATTN_GROUP)]
                for t, b in enumerate(blocks):
                    sc_s[t] = _dot(qs[_sub_block(b), :], ks[_sub_block(b, 2 * BLK), :], tb=True)
                    dp_s[t] = _dot(dos[_sub_block(b), :], vs[_sub_block(b, 2 * BLK), :], tb=True)
                for t, b in enumerate(blocks):
                    _, n = _block_rows(b, n_blocks, dil)
                    st = st_s[_sub_block(b), :]
                    pr = jnp.exp(sc_s[t] + bias_s[2 * i + jnp.minimum(n, 1)] - st[:, 0:1])
                    ds = pr * (dp_s[t] - st[:, 1:2])
                    dbias_s[i] += ds
                    p_s[t] = pr.astype(BF16)
                    ds_s[t] = ds.astype(BF16)
                for t, b in enumerate(blocks):
                    dq_u[_sub_block(b), :] = _dot(ds_s[t], ks[_sub_block(b, 2 * BLK), :])
                    for part, lhs, rhs in ((dk_u, ds_s, qs), (dv_u, p_s, dos)):
                        both = _dot(lhs[t], rhs[_sub_block(b), :], ta=True)
                        part[_sub_block(b), :] += both[:BLK]
                        part[_sub_block(b, pad=BLK), :] = both[BLK:]
                return carry
            lax.fori_loop(0, n_blocks // ATTN_GROUP, group, 0)

            def scatter(b, carry, i=i, dil=dil):
                if i < 2:
                    rows, _ = _block_rows(b, n_blocks, dil)
                    for acc, part, pad in ((dqn, dq_u, 0), (dkn, dk_u, BLK), (dvv, dv_u, BLK)):
                        val = part[_sub_block(b, pad=pad), :]
                        acc[rows, :] = val if i == 0 else acc[rows, :] + val
                else:
                    middle = DILATED[1][1]
                    rows, _ = _block_rows(b, n_blocks, middle)
                    slot = b % 2
                    for a, (acc, part, pad) in enumerate(((dqn, dq_u, 0), (dkn, dk_u, BLK), (dvv, dv_u, BLK))):
                        for q, ratio, piece, src in _regroup(b, n_blocks, middle, dil):
                            tmp[slot, a, pl.ds(q, piece, stride=ratio), :] = part[pl.ds(src + pad, piece), :]
                        acc[rows, :] += tmp[slot, a]
                return carry
            lax.fori_loop(0, n_blocks, scatter, 0, unroll=8)

        q = q_ref[...]
        rq = lax.rsqrt(jnp.mean(q * q, axis=-1, keepdims=True) + EPS)
        k = k_ref[...]
        rk = lax.rsqrt(jnp.mean(k * k, axis=-1, keepdims=True) + EPS)
        dq_n = dqn[...]
        ggq_ref[...] += jnp.sum(dq_n * (q * rq) * scale, axis=0, keepdims=True)
        dg = dq_n * gq_ref[...] * scale
        qs[...] = (rq * (dg - q * (rq * rq) * jnp.mean(dg * q, axis=-1, keepdims=True))).astype(BF16)
        dk_n = dkn[...]
        ggk_ref[...] += jnp.sum(dk_n * (k * rk), axis=0, keepdims=True)
        dg = dk_n * gk_ref[...]
        dos[...] = (rk * (dg - k * (rk * rk) * jnp.mean(dg * k, axis=-1, keepdims=True))).astype(BF16)
        vs[BLK:, :] = dvv[...].astype(BF16)
        copies = [pltpu.make_async_copy(src, dp_out.at[:, pl.ds(pl.multiple_of(((3 + j) * heads + h) * HEAD_DIM, HEAD_DIM), HEAD_DIM)],
                                        sems.at[j]) for j, src in enumerate((qs, dos, vs.at[pl.ds(BLK, s)]))]
        for cp in copies:
            cp.start()
        for b in range(NUM_BUCKETS):
            tot = jnp.zeros((BLK, 2 * BLK), F32)
            for i in range(n_dil):
                tot = tot + jnp.where(tab_ref[i] == b, dbias_s[i], 0.0)
            db_ref[b:b + 1, :] = jnp.full((1, LANE), jnp.sum(tot), F32)
        for cp in copies:
            cp.wait()
        if comm:
            @pl.when(h == heads - 1)
            def _():
                comm.finish(cin_refs, cout_refs, comm_sems)

    def head_col(off):
        return pl.BlockSpec((s, HEAD_DIM), lambda h: (0, off * heads + h))
    own_col = pl.BlockSpec((s, HEAD_DIM), lambda h: (0, h))
    vec = pl.BlockSpec((1, HEAD_DIM), lambda h: (0, 0))
    big = pltpu.VMEM((s, HEAD_DIM), F32)
    padded32 = pltpu.VMEM((s + BLK, HEAD_DIM), F32)
    padded16 = pltpu.VMEM((s + BLK, HEAD_DIM), BF16)
    res = pl.pallas_call(
        body, name="attn_bwd", grid=(heads,),
        in_specs=[ANY, head_col(3), head_col(4), head_col(5), own_col,
                  pl.BlockSpec((None, s, 2), lambda h: (h, 0, 0)), vec, vec,
                  pl.BlockSpec(memory_space=pltpu.SMEM),
                  pl.BlockSpec((n_dil, BLK, 2 * BLK), lambda h: (0, 0, 0))] + [ANY] * len(c_ins),
        out_specs=[ANY, vec, vec, pl.BlockSpec((None, NUM_BUCKETS, LANE), lambda h: (h, 0, 0))] + [ANY] * len(c_outs),
        out_shape=[jax.ShapeDtypeStruct((s, d_in), BF16), jax.ShapeDtypeStruct((1, HEAD_DIM), F32),
                   jax.ShapeDtypeStruct((1, HEAD_DIM), F32), jax.ShapeDtypeStruct((heads, NUM_BUCKETS, LANE), F32)]
        + c_outs,
        scratch_shapes=[pltpu.VMEM((s, HEAD_DIM), BF16), padded16, padded16, pltpu.VMEM((s, HEAD_DIM), BF16),
                        pltpu.VMEM((s, 2), F32),
                        pltpu.VMEM((s, HEAD_DIM), BF16), padded16, padded16, pltpu.VMEM((s, HEAD_DIM), BF16),
                        pltpu.VMEM((s, 2), F32),
                        pltpu.VMEM((2, 4, BLK, HEAD_DIM), F32), pltpu.VMEM((2, BLK, 2), F32),
                        big, big, big, big, padded32, padded32,
                        pltpu.VMEM((2 * n_dil, BLK, 2 * BLK), F32), pltpu.VMEM((n_dil, BLK, 2 * BLK), F32),
                        pltpu.VMEM((ATTN_GROUP, BLK, 2 * BLK), F32), pltpu.VMEM((ATTN_GROUP, BLK, 2 * BLK), F32),
                        pltpu.VMEM((ATTN_GROUP, BLK, 2 * BLK), BF16), pltpu.VMEM((ATTN_GROUP, BLK, 2 * BLK), BF16),
                        pltpu.SemaphoreType.DMA((3,))] + c_scratch,
        input_output_aliases={0: 0},
        compiler_params=_params("arbitrary"),
    )(dproj, proj, proj, proj, d_o, stats, g_q, g_k, rel_bias, tables, *c_ins)
    return res[0], res[1], res[2], res[3], res[4:]


def _local_step(xs, ps, tgt, sp, win4, wout4, wgate4, wup4, dist=None):
    s, d = xs.shape
    wa = d // 2
    heads = wa // HEAD_DIM
    d_in = 7 * wa
    c4 = d_in // N_CHIPS
    dple = ps.shape[1]

    tm, tn = _tile(s, 1024), _tile(d, 1024)
    tn_in = _tile(c4, 1024)
    per_in = c4 // tn_in
    tn_up = _tile(d // N_CHIPS, 512)
    per_up = (d // N_CHIPS) // tn_up
    gm, gn = s // tm, d // tn

    hn, rx = _rms_fwd("rms_pre", xs, sp["g_pre"])
    if dist:
        proj, win4 = _mm_in_gather(dist["chip"], hn, win4)
    else:
        proj = _matmul(
            "mm_in", hn, win4, pl.BlockSpec((tm, d), lambda i, j, k: (i, 0)), _shard_spec(d, tn_in, per_in, 2, 1),
            (gm, d_in // tn_in, 1), None, jax.ShapeDtypeStruct((s, d_in), F32),
            pl.BlockSpec((tm, tn_in), lambda i, j, k: (i, j)))
    b_st = sp["b_s"].T
    y, z, vn, mu_v, rs_v, ra = _gmlp_fwd(proj, sp["w_s"], b_st, sp["ln_v_g"], sp["ln_v_b"], sp["g_out_a"], d)
    yb, lse, gathered = _attn_fwd(proj, sp["g_q"], sp["g_k"], sp["rel_bias"], d,
                                  comm=_Gather([wout4, wgate4, wup4]) if dist else None)
    if dist:
        wout4, wgate4, wup4 = gathered
    wout, wgate = wout4.reshape(d, d), wgate4.reshape(d, d)
    y, rb = _gate_fwd(y, yb, proj, sp["g_out_b"], d)

    def residual(acc, extra, outs):
        outs[0][...] = extra[0][...] + acc

    tile = pl.BlockSpec((tm, tn), lambda i, j, k: (i, j))
    tile_up = pl.BlockSpec((tm, tn_up), lambda i, j, k: (i, j))
    h = _matmul(
        "mm_out", y, wout, pl.BlockSpec((tm, d), lambda i, j, k: (i, 0)), pl.BlockSpec((d, tn), lambda i, j, k: (0, j)),
        (gm, gn, 1), None, jax.ShapeDtypeStruct((s, d), F32), tile, extras=(xs,), extra_specs=(tile,),
        epilogue=residual)
    hp, rh = _rms_fwd("rms_ple", h, sp["g_ple"])
    p16 = ps.astype(BF16)

    def head(acc, extra, outs):
        p_ref, wup_ref, h_ref, t_ref = extra
        dout_ref, dgl_ref, dup_ref, loss_ref = outs
        up = _dot(p_ref[...], wup_ref[...])
        gate = jax.nn.sigmoid(acc)
        err = h_ref[...] + gate * up - t_ref[...]
        sq = _colsum8(err * err)
        part = sq[:, 0:LANE]
        for c in range(1, sq.shape[1] // LANE):
            part = part + sq[:, c * LANE:(c + 1) * LANE]
        loss_ref[...] = part
        dout = err * (1.0 / d)
        dout_ref[...] = dout.astype(BF16)
        dup_ref[...] = (dout * gate).astype(BF16)
        dgl_ref[...] = (dout * up * gate * (1.0 - gate)).astype(BF16)

    tile_up = pl.BlockSpec((tm, tn_up), lambda i, j, k: (i, j))
    dout, dglin, dup, loss_parts = _matmul(
        "mm_gate_loss", hp, wgate, pl.BlockSpec((tm, d), lambda i, j, k: (i, 0)), pl.BlockSpec((d, tn_up), lambda i, j, k: (0, j)),
        (gm, d // tn_up, 1), None,
        (jax.ShapeDtypeStruct((s, d), BF16), jax.ShapeDtypeStruct((s, d), BF16), jax.ShapeDtypeStruct((s, d), BF16),
         jax.ShapeDtypeStruct((gm * 8, (d // tn_up) * LANE), F32)),
        (tile_up, tile_up, tile_up, pl.BlockSpec((8, LANE), lambda i, j, k: (i, j))),
        extras=(p16, wup4, h, tgt),
        extra_specs=(pl.BlockSpec((tm, dple), lambda i, j, k: (i, 0)), _shard_spec(dple, tn_up, per_up, 2, 1),
                     tile_up, tile_up),
        epilogue=head)
    loss = 0.5 * jnp.sum(loss_parts) * (1.0 / d)

    dhp = _matmul(
        "mm_dhp", dglin, wgate, pl.BlockSpec((tm, d), lambda i, j, k: (i, 0)), pl.BlockSpec((tn, d), lambda i, j, k: (j, 0)),
        (gm, gn, 1), None, jax.ShapeDtypeStruct((s, d), BF16), tile, tb=True)
    dh, dh16, g_g_ple = _rms_bwd("rms_ple_bwd", dhp, h, rh, sp["g_ple"], dout, True)
    tmw = _tile(d, 1024)
    g_wout = _matmul(
        "mm_gwout", y, dh16, pl.BlockSpec((s, tmw), lambda i, j, k: (0, i)), pl.BlockSpec((s, tn), lambda i, j, k: (0, j)),
        (d // tmw, gn, 1), None, jax.ShapeDtypeStruct((d, d), F32), pl.BlockSpec((tmw, tn), lambda i, j, k: (i, j)), ta=True)
    g_wout4 = g_wout.reshape(N_CHIPS, d // N_CHIPS, d)
    g_wgate = _matmul(
        "mm_gwgate", hp, dglin, pl.BlockSpec((s, tmw), lambda i, j, k: (0, i)), pl.BlockSpec((s, tn), lambda i, j, k: (0, j)),
        (d // tmw, gn, 1), None, jax.ShapeDtypeStruct((d, d), F32), pl.BlockSpec((tmw, tn), lambda i, j, k: (i, j)), ta=True,
        comm=_SwapHalves([g_wout4]) if dist else None)
    if dist:
        g_wgate, swapped_out = g_wgate
    g_wgate4 = g_wgate.reshape(N_CHIPS, d // N_CHIPS, d)
    g_wup = _matmul(
        "mm_gwup", p16, dup, pl.BlockSpec((s, dple), lambda i, j, k: (0, 0)), pl.BlockSpec((s, tn_up), lambda i, j, k: (0, j)),
        (1, d // tn_up, 1), None, jax.ShapeDtypeStruct((N_CHIPS, dple, d // N_CHIPS), F32),
        _shard_spec(dple, tn_up, per_up, 0, 1), ta=True)
    dy = _matmul(
        "mm_dy", dh16, wout, pl.BlockSpec((tm, d), lambda i, j, k: (i, 0)), pl.BlockSpec((tn, d), lambda i, j, k: (j, 0)),
        (gm, gn, 1), None, jax.ShapeDtypeStruct((s, d), BF16), tile, tb=True,
        comm=_SwapHalves([g_wgate4, g_wup]) if dist else None)
    early = (g_wout4, g_wgate4, g_wup)
    if dist:
        core = dist["core"]
        dy, swapped_rest = dy
        early_sums = [_chip_sum("chip_sum_" + n, dist["chip"], core, g, o)
                      for n, g, o in zip(LARGE[1:], early, list(swapped_out) + list(swapped_rest))]

    dproj, d_o, stats, g_g_out_b = _gate_bwd(dy, yb, lse, proj, rb, sp["g_out_b"], d)
    dproj, g_g_q, g_g_k, dbias, early_got = _attn_bwd(
        dproj, d_o, stats, proj, sp["g_q"], sp["g_k"], sp["rel_bias"], d,
        comm=_ExchangeChips([s16 for _, s16 in early_sums]) if dist else None)
    dproj, g_w_s, dz_sum, g_ln_g, g_ln_b, g_g_out_a = _gmlp_bwd(
        dproj, dy, proj, z, vn, mu_v, rs_v, ra, sp["w_s"], sp["ln_v_g"], sp["g_out_a"], d)

    if dist:
        tmw_half = _tile(d // 2, 1024)
        n_half = (d // 2) // tmw_half

        def half_of_gwin(name, own, comm):
            def rows(i, j, k, core_ref):
                return (0, (core_ref[0] if own else 1 - core_ref[0]) * n_half + i)
            return _matmul(
                name, hn, dproj, pl.BlockSpec((s, tmw_half), rows),
                pl.BlockSpec((s, tn_in), lambda i, j, k, core_ref: (0, j)),
                (n_half, d_in // tn_in, 1), None, jax.ShapeDtypeStruct((N_CHIPS, d // 2, c4), F32),
                pl.BlockSpec((None, tmw_half, tn_in), lambda i, j, k, core_ref: (j // per_in, i, j % per_in)), ta=True,
                comm=comm, prefetch=(core,))

        g_theirs = half_of_gwin("mm_gwin_theirs", False, None)
        g_mine, (g_from_sibling,) = half_of_gwin("mm_gwin_mine", True, _SwapHalves([g_theirs], whole=True))
        win_sum = _chip_sum("chip_sum_w_in", dist["chip"], core, g_mine, g_from_sibling)
    else:
        g_win = _matmul(
            "mm_gwin", hn, dproj, pl.BlockSpec((s, tmw), lambda i, j, k: (0, i)),
            pl.BlockSpec((s, tn_in), lambda i, j, k: (0, j)),
            (d // tmw, d_in // tn_in, 1), None, jax.ShapeDtypeStruct((N_CHIPS, d, c4), F32),
            _shard_spec(tmw, tn_in, per_in, 0, 1), ta=True)
    tk = c4
    tmh, tnh = _tile(s, 1024), _tile(d, 1024)
    dhn = _matmul(
        "mm_dhn", dproj, win4, pl.BlockSpec((tmh, tk), lambda i, j, k: (i, k)), _shard_spec(tnh, tk, 1, 1, 2),
        (s // tmh, d // tnh, d_in // tk), (tmh, tnh), jax.ShapeDtypeStruct((s, d), BF16),
        pl.BlockSpec((tmh, tnh), lambda i, j, k: (i, j)), tb=True,
        comm=_ExchangeChips([win_sum[1]]) if dist else None)
    if dist:
        dhn, win_got = dhn
    grad_x, g_g_pre = _rms_bwd("rms_pre_bwd", dhn, xs, rx, sp["g_pre"], dh, False)

    small = {
        "g_pre": g_g_pre,
        "w_s": g_w_s,
        "b_s": jnp.sum(dz_sum.reshape(CHUNK, heads, HEAD_DIM), axis=-1).T,
        "ln_v_g": g_ln_g, "ln_v_b": g_ln_b,
        "g_q": g_g_q, "g_k": g_g_k,
        "rel_bias": dbias[:, :, 0].T,
        "g_out_a": g_g_out_a, "g_out_b": g_g_out_b,
        "g_ple": g_g_ple,
    }
    if not dist:
        return loss, grad_x, (g_win, *early), small
    sums32 = [win_sum[0]] + [s32 for s32, _ in early_sums]
    return loss, grad_x, (sums32, list(win_got) + list(early_got)), small


def _place():
    x, y, c = lax.axis_index("x"), lax.axis_index("y"), lax.axis_index("c")
    chips = [(1 - x, y), (x, 1 - y), (1 - x, 1 - y)]
    return x, y, c, chips


def _remote(src, dst, send_sems, recv_sems, k, to):
    return pltpu.make_async_remote_copy(src_ref=src, dst_ref=dst, send_sem=send_sems.at[k], recv_sem=recv_sems.at[k],
                                        device_id=to, device_id_type=MESH)


def _cast_bf16(name, chip, w):
    r, c = w.shape
    tm = _tile(r, 256)

    def body(chip_ref, w_ref, o_ref):
        del chip_ref
        o_ref[...] = w_ref[...].astype(BF16)

    return pl.pallas_call(
        body, name=name,
        grid_spec=pltpu.PrefetchScalarGridSpec(
            num_scalar_prefetch=1, grid=(r // tm,),
            in_specs=[pl.BlockSpec((tm, c), lambda i, chip_ref: (i, 0))],
            out_specs=pl.BlockSpec((None, tm, c), lambda i, chip_ref: (chip_ref[0], i, 0))),
        out_shape=jax.ShapeDtypeStruct((N_CHIPS, r, c), BF16), compiler_params=_params("parallel"),
    )(chip, w)


class _Gather:
    in_place = True

    def __init__(self, fulls):
        self.ins = list(fulls)
        self.out_shape = [jax.ShapeDtypeStruct(f.shape, f.dtype) for f in fulls]
        n = len(fulls)
        self.scratch = [pltpu.SemaphoreType.DMA((6 * n,)), pltpu.SemaphoreType.DMA((6 * n,))]

    @staticmethod
    def _sends(outs, sems):
        send_sems, recv_sems = sems
        x, y, c, chips = _place()
        cps = []
        for w, ref in enumerate(outs):
            half = ref.shape[1] // 2
            blk = ref.at[2 * x + y, pl.ds(c * half, half)]
            cps += [_remote(blk, blk, send_sems, recv_sems, 6 * w + q, (*chip, c)) for q, chip in enumerate(chips)]
        return cps

    def start(self, ins, outs, sems):
        for cp in self._sends(outs, sems):
            cp.start()

    def finish(self, ins, outs, sems):
        send_sems, recv_sems = sems
        x, y, c, chips = _place()
        sibling = (x, y, 1 - c)
        forwards = []
        for w, ref in enumerate(outs):
            half = ref.shape[1] // 2
            for q, chip in enumerate(chips):
                blk = ref.at[2 * chip[0] + chip[1], pl.ds(c * half, half)]
                _remote(blk, blk, send_sems, recv_sems, 6 * w + q, sibling).wait_recv()
                fwd = _remote(blk, blk, send_sems, recv_sems, 6 * w + 3 + q, sibling)
                fwd.start()
                forwards.append(fwd)
        for w, ref in enumerate(outs):
            half = ref.shape[1] // 2
            for q, chip in enumerate(chips):
                blk = ref.at[2 * chip[0] + chip[1], pl.ds((1 - c) * half, half)]
                _remote(blk, blk, send_sems, recv_sems, 6 * w + 3 + q, sibling).wait_recv()
        for cp in self._sends(outs, sems) + forwards:
            cp.wait_send()


class _ExchangeChips:
    in_place = False

    def __init__(self, sums16):
        self.ins = list(sums16)
        self.out_shape = [jax.ShapeDtypeStruct(g.shape, g.dtype) for g in sums16]
        n = len(sums16)
        self.scratch = [pltpu.SemaphoreType.DMA((3 * n,)), pltpu.SemaphoreType.DMA((3 * n,))]

    @staticmethod
    def _sends(ins, outs, sems):
        send_sems, recv_sems = sems
        x, y, c, chips = _place()
        return [_remote(ins[w].at[2 * chip[0] + chip[1]], outs[w].at[2 * x + y], send_sems, recv_sems, 3 * w + q, (*chip, c))
                for w in range(len(ins)) for q, chip in enumerate(chips)]

    def start(self, ins, outs, sems):
        for cp in self._sends(ins, outs, sems):
            cp.start()

    def finish(self, ins, outs, sems):
        send_sems, recv_sems = sems
        x, y, c, chips = _place()
        for w in range(len(ins)):
            for q, chip in enumerate(chips):
                blk = outs[w].at[2 * chip[0] + chip[1]]
                _remote(blk, blk, send_sems, recv_sems, 3 * w + q, (*chip, c)).wait_recv()
        for cp in self._sends(ins, outs, sems):
            cp.wait_send()


def _mm_in_gather(chip, hn, win4):
    s, d = hn.shape
    c4 = win4.shape[2]
    tm, tn = _tile(s, 1024), _tile(c4, 1024)
    per = c4 // tn
    nj, gm = N_CHIPS * per, s // tm
    half = d // 2

    def tile_of(j):
        jj = j - per
        o = jnp.where(j < per, 0, jnp.where(j < 3 * per, 1 + jj % 2, 3))
        t = jnp.where(j < per, j, jnp.where(j < 3 * per, jj // 2, j - 3 * per))
        return o, t

    def step_of(o, t):
        return per + 2 * t + o - 1 if o in (1, 2) else o * per + t

    def block_of(j, chip_ref):
        o, _ = tile_of(j)
        return chip_ref[0] ^ (((o & 1) << 1) | (o >> 1))

    i_late = max(gm - 2, 0)

    def body(chip_ref, hn_ref, w_any, proj_ref, w_ref, wbuf, wsem, send_sems, recv_sems):
        del w_any
        j, i = pl.program_id(0), pl.program_id(1)
        x, y, c, chips = _place()
        sibling = (x, y, 1 - c)

        def region(k, rows_half, t):
            return w_ref.at[k, pl.ds(rows_half * half, half), pl.ds(t * tn, tn)]

        def send_mine(q, t):
            blk = region(2 * x + y, c, t)
            return _remote(blk, blk, send_sems, recv_sems, q * per + t, (*chips[q], c))

        def forward(q, t):
            blk = region(2 * chips[q][0] + chips[q][1], c, t)
            return _remote(blk, blk, send_sems, recv_sems, (3 + q) * per + t, sibling)

        def fetch(jj, mine, slot):
            rows = pl.ds((c if mine else 1 - c) * half, half)
            cols = pl.ds(pl.multiple_of(tile_of(jj)[1] * tn, LANE), tn)
            return pltpu.make_async_copy(w_ref.at[block_of(jj, chip_ref), rows, cols], wbuf.at[slot, rows],
                                         wsem.at[2 * slot + (0 if mine else 1)])

        def foreign_tiles():
            return [(o, t) for o in range(1, N_CHIPS) for t in range(per)]

        relayed = [(0, t) for t in range(1, per, 2)] + [(1, t) for t in range(0, per, 2)]

        def relay(q, t):
            blk = region(2 * chips[q][0] + chips[q][1], c, t)
            return _remote(blk, blk, send_sems, recv_sems, 2 * per + t, (*chips[1 - q], c))

        def landed(q, t):
            blk = region(2 * chips[q][0] + chips[q][1], c, t)
            _remote(blk, blk, send_sems, recv_sems, q * per + t, sibling).wait_recv()
            forward(q, t).start()
            if (q, t) in relayed:
                relay(q, t).start()

        @pl.when((j == 0) & (i == 0))
        def _():
            for t in range(per):
                send_mine(0, t).start()
                send_mine(1, t).start()
            fetch(0, True, 0).start()
            fetch(0, False, 0).start()

        nxt = j + 1

        @pl.when((i == 0) & (nxt < nj))
        def _():
            for o, t in foreign_tiles():
                @pl.when(nxt == step_of(o, t))
                def _():
                    if o == 1:
                        landed(0, t)
                        landed(1, t)
                    elif o == 3:
                        landed(2, t)

            fetch(nxt, True, nxt % 2).start()

            @pl.when(nxt < per)
            def _():
                fetch(nxt, False, nxt % 2).start()

        @pl.when((i == i_late) & (nxt < nj))
        def _():
            for o, t in foreign_tiles():
                @pl.when(nxt == step_of(o, t))
                def _():
                    blk = region(2 * chips[o - 1][0] + chips[o - 1][1], 1 - c, t)
                    _remote(blk, blk, send_sems, recv_sems, (3 + o - 1) * per + t, sibling).wait_recv()
                    fetch(nxt, False, nxt % 2).start()

        @pl.when(i == 0)
        def _():
            fetch(j, True, j % 2).wait()
            fetch(j, False, j % 2).wait()

        proj_ref[...] = _dot(hn_ref[...], wbuf[j % 2])

        @pl.when((j == nj - 1) & (i == gm - 1))
        def _():
            for t in range(per):
                send_mine(0, t).wait_send()
                send_mine(1, t).wait_send()
                for q in range(3):
                    forward(q, t).wait_send()
            for q, t in relayed:
                relay(q, t).wait_send()

    return pl.pallas_call(
        body, name="mm_in_gather",
        grid_spec=pltpu.PrefetchScalarGridSpec(
            num_scalar_prefetch=1, grid=(nj, gm),
            in_specs=[pl.BlockSpec((tm, d), lambda j, i, chip_ref: (i, 0)), ANY],
            out_specs=[pl.BlockSpec((tm, tn), lambda j, i, chip_ref: (i, block_of(j, chip_ref) * per + tile_of(j)[1])), ANY],
            scratch_shapes=[pltpu.VMEM((2, d, tn), BF16), pltpu.SemaphoreType.DMA((4,)),
                            pltpu.SemaphoreType.DMA((6 * per,)), pltpu.SemaphoreType.DMA((6 * per,))]),
        out_shape=[jax.ShapeDtypeStruct((s, N_CHIPS * c4), F32), jax.ShapeDtypeStruct(win4.shape, win4.dtype)],
        input_output_aliases={2: 1},
        compiler_params=_params("arbitrary", "arbitrary"),
    )(chip, hn, win4)


class _SwapHalves:
    in_place = False

    def __init__(self, grads, whole=False):
        self.ins = list(grads)
        self.whole = whole
        self.out_shape = [jax.ShapeDtypeStruct((N_CHIPS, g.shape[1] // (1 if whole else 2), g.shape[2]), g.dtype)
                          for g in grads]
        self.scratch = [pltpu.SemaphoreType.DMA((len(grads),)), pltpu.SemaphoreType.DMA((len(grads),))]

    def _copies(self, ins, outs, sems):
        x, y, c, _ = _place()
        cps = []
        for w in range(len(ins)):
            half = ins[w].shape[1] // 2
            src = ins[w] if self.whole else ins[w].at[:, pl.ds((1 - c) * half, half)]
            cps.append(_remote(src, outs[w], sems[0], sems[1], w, (x, y, 1 - c)))
        return cps

    def start(self, ins, outs, sems):
        for cp in self._copies(ins, outs, sems):
            cp.start()

    def finish(self, ins, outs, sems):
        for cp in self._copies(ins, outs, sems):
            cp.wait()


def _chip_sum(name, chip, core, grad, got):
    _, half, c = got.shape
    th = _tile(half, 128)
    n = half // th
    skip = n if grad.shape[1] != half else 0

    def body(chip_ref, core_ref, g_ref, o_ref, s_ref):
        del chip_ref, core_ref
        s_ref[...] = (g_ref[...] + o_ref[...]).astype(s_ref.dtype)

    def specs(first, out_block):
        def block(k, i, chip_ref):
            return (chip_ref[0] + first + k) % N_CHIPS

        return dict(
            in_specs=[pl.BlockSpec((None, th, c), lambda k, i, chip_ref, core_ref: (block(k, i, chip_ref), core_ref[0] * skip + i, 0)),
                      pl.BlockSpec((None, th, c), lambda k, i, chip_ref, core_ref: (block(k, i, chip_ref), i, 0))],
            out_specs=out_block(block))

    own = pl.pallas_call(
        body, name=name + "_own",
        grid_spec=pltpu.PrefetchScalarGridSpec(
            num_scalar_prefetch=2, grid=(1, n),
            **specs(0, lambda block: pl.BlockSpec((th, c), lambda k, i, chip_ref, core_ref: (i, 0)))),
        out_shape=jax.ShapeDtypeStruct((half, c), F32), compiler_params=_params("parallel", "parallel"),
    )(chip, core, grad, got)
    others = pl.pallas_call(
        functools.partial(body), name=name + "_others",
        grid_spec=pltpu.PrefetchScalarGridSpec(
            num_scalar_prefetch=2, grid=(N_CHIPS - 1, n),
            **specs(1, lambda block: pl.BlockSpec(
                (None, th, c), lambda k, i, chip_ref, core_ref: (block(k, i, chip_ref), i, 0)))),
        out_shape=jax.ShapeDtypeStruct((N_CHIPS, half, c), BF16), compiler_params=_params("parallel", "parallel"),
    )(chip, core, grad, got)
    return own, others


def _riding(comm, grid, body, n_pre, n_in, n_out):
    if comm is None:
        return body
    c_in, c_out, c_scr = len(comm.ins), len(comm.out_shape), len(comm.scratch)

    def wrapped(*refs):
        pre, rest = refs[:n_pre], refs[n_pre:]
        ins, rest = rest[:n_in], rest[n_in:]
        cin, rest = rest[:c_in], rest[c_in:]
        outs, rest = rest[:n_out], rest[n_out:]
        cout, rest = rest[:c_out], rest[c_out:]
        own, sems = rest[:len(rest) - c_scr], rest[len(rest) - c_scr:]
        ids = [pl.program_id(ax) for ax in range(len(grid))]
        first, last = ids[0] == 0, ids[0] == grid[0] - 1
        for ax in range(1, len(grid)):
            first, last = first & (ids[ax] == 0), last & (ids[ax] == grid[ax] - 1)

        @pl.when(first)
        def _():
            comm.start(cin, cout, sems)

        body(*pre, *ins, *outs, *own)

        @pl.when(last)
        def _():
            comm.finish(cin, cout, sems)

    return wrapped


def _total(name, chip, core, sum32, got16, comm=None):
    half, c = sum32.shape
    th = _tile(half, 128)
    n = half // th

    def body(chip_ref, core_ref, own_ref, a_ref, b_ref, c_ref, o_ref):
        del chip_ref, core_ref
        o_ref[...] = ((own_ref[...] + a_ref[...].astype(F32)) + b_ref[...].astype(F32)) + c_ref[...].astype(F32)

    def other(step):
        return pl.BlockSpec((None, th, c), lambda i, chip_ref, core_ref: ((chip_ref[0] + step) % N_CHIPS, i, 0))

    c_ins = list(comm.ins) if comm else []
    res = pl.pallas_call(
        _riding(comm, (n,), body, 2, 4, 1), name=name,
        grid_spec=pltpu.PrefetchScalarGridSpec(
            num_scalar_prefetch=2, grid=(n,),
            in_specs=[pl.BlockSpec((th, c), lambda i, chip_ref, core_ref: (i, 0)), other(1), other(2), other(3)]
            + [ANY] * len(c_ins),
            out_specs=[pl.BlockSpec((th, c), lambda i, chip_ref, core_ref: (core_ref[0] * n + i, 0))]
            + [ANY] * len(c_ins),
            scratch_shapes=list(comm.scratch) if comm else []),
        out_shape=[jax.ShapeDtypeStruct((2 * half, c), F32)] + (list(comm.out_shape) if comm else []),
        input_output_aliases={6 + i: 1 + i for i in range(len(c_ins))} if comm else {},
        compiler_params=_params("arbitrary" if comm else "parallel"),
    )(chip, core, sum32, got16, got16, got16, *c_ins)
    return (res[0], res[1:]) if comm else res[0]


class _JoinHalves:
    in_place = True

    def __init__(self, blocks):
        self.ins = list(blocks)
        self.out_shape = [jax.ShapeDtypeStruct(t.shape, t.dtype) for t in blocks]
        self.scratch = [pltpu.SemaphoreType.DMA((len(blocks),)), pltpu.SemaphoreType.DMA((len(blocks),))]

    @staticmethod
    def _copy(ref, w, mine, sems):
        x, y, c, _ = _place()
        half = ref.shape[0] // 2
        rows = ref.at[pl.ds((c if mine else 1 - c) * half, half)]
        return _remote(rows, rows, sems[0], sems[1], w, (x, y, 1 - c))

    def start(self, ins, outs, sems):
        for w, ref in enumerate(outs):
            self._copy(ref, w, True, sems).start()

    def finish(self, ins, outs, sems):
        for w, ref in enumerate(outs):
            self._copy(ref, w, True, sems).wait_send()
            self._copy(ref, w, False, sems).wait_recv()


class _Both:
    in_place = True

    def __init__(self, a, b):
        assert a.in_place and b.in_place
        self.parts = (a, b)
        self.ins = list(a.ins) + list(b.ins)
        self.out_shape = list(a.out_shape) + list(b.out_shape)
        self.scratch = list(a.scratch) + list(b.scratch)

    def _split(self, ins, outs, sems):
        a = self.parts[0]
        n, k = len(a.ins), len(a.scratch)
        return (a, ins[:n], outs[:n], sems[:k]), (self.parts[1], ins[n:], outs[n:], sems[k:])

    def start(self, ins, outs, sems):
        for part, i, o, s in self._split(ins, outs, sems):
            part.start(i, o, s)

    def finish(self, ins, outs, sems):
        for part, i, o, s in self._split(ins, outs, sems):
            part.finish(i, o, s)


def _alone(name, comm):
    n = len(comm.ins)

    def body(*refs):
        comm.start(refs[:n], refs[n:2 * n], refs[2 * n:])
        comm.finish(refs[:n], refs[n:2 * n], refs[2 * n:])

    return pl.pallas_call(
        body, name=name, in_specs=[ANY] * n, out_specs=[ANY] * n, out_shape=comm.out_shape,
        scratch_shapes=comm.scratch, input_output_aliases={i: i for i in range(n)},
    )(*comm.ins)


class _GatherSmall:
    in_place = True

    def __init__(self, full):
        self.ins = [full]
        self.out_shape = [jax.ShapeDtypeStruct(full.shape, full.dtype)]
        self.scratch = [pltpu.SemaphoreType.DMA((7,)), pltpu.SemaphoreType.DMA((7,))]

    @staticmethod
    def _copy(ref, sems, k, block, to):
        m_per = ref.shape[0] // N_DEV
        px, py, pc = block
        rows = ref.at[pl.ds((4 * px + 2 * py + pc) * m_per, m_per), :]
        return _remote(rows, rows, sems[0], sems[1], k, to)

    def _first(self, ref, sems):
        x, y, c, chips = _place()
        me = (x, y, c)
        return [self._copy(ref, sems, 0, me, (x, y, 1 - c))] + \
               [self._copy(ref, sems, 1 + j, me, (*chip, c)) for j, chip in enumerate(chips)]

    def start(self, ins, outs, sems):
        for cp in self._first(outs[0], sems):
            cp.start()

    def finish(self, ins, outs, sems):
        ref = outs[0]
        x, y, c, chips = _place()
        me, sibling = (x, y, c), (x, y, 1 - c)
        passed = [self._copy(ref, sems, 4 + j, (*chip, c), sibling) for j, chip in enumerate(chips)]
        for j, chip in enumerate(chips):
            self._copy(ref, sems, 1 + j, (*chip, c), me).wait_recv()
            passed[j].start()
        self._copy(ref, sems, 0, sibling, me).wait_recv()
        for j, chip in enumerate(chips):
            self._copy(ref, sems, 4 + j, (*chip, 1 - c), me).wait_recv()
        for cp in self._first(ref, sems) + passed:
            cp.wait_send()


def _adamw_math(w, g, m, v):
    m = ADAM_B1 * m + (1.0 - ADAM_B1) * g
    v = ADAM_B2 * v + (1.0 - ADAM_B2) * (g * g)
    m_hat = m / (1.0 - ADAM_B1 ** ADAM_STEP)
    v_hat = v / (1.0 - ADAM_B2 ** ADAM_STEP)
    delta = -ADAM_LR * (m_hat / (jnp.sqrt(v_hat) + ADAM_EPS) + ADAM_WD * w)
    return delta, m, v


def _adamw(name, w, g, m, v):
    r, c = w.shape
    tm = _tile(r, 128)

    def body(w_ref, g_ref, m_ref, v_ref, g_out, d_out, m_out, v_out):
        g = g_ref[...]
        g_out[...] = g
        d_out[...], m_out[...], v_out[...] = _adamw_math(w_ref[...], g, m_ref[...], v_ref[...])

    spec = pl.BlockSpec((tm, c), lambda i: (i, 0))
    return pl.pallas_call(
        body, name=name, grid=(r // tm,), in_specs=[spec] * 4, out_specs=[spec] * 4,
        out_shape=[jax.ShapeDtypeStruct((r, c), F32)] * 4, compiler_params=_params("parallel"),
    )(w, g, m, v)


def _adamw_small(gathered, w, m, v):
    rows = w.shape[0]

    def body(all_ref, w_ref, m_ref, v_ref, g_out, d_out, m_out, v_out):
        g = all_ref[0:rows, :]
        for dev in range(1, N_DEV):
            g = g + all_ref[dev * rows:(dev + 1) * rows, :]
        g_out[...] = g
        d_out[...], m_out[...], v_out[...] = _adamw_math(w_ref[...], g, m_ref[...], v_ref[...])

    return pl.pallas_call(
        body, name="adamw_small", out_shape=[jax.ShapeDtypeStruct(w.shape, F32)] * 4, compiler_params=_params(),
    )(gathered, w, m, v)


SMALL = ("g_pre", "w_s", "b_s", "ln_v_g", "ln_v_b", "g_q", "g_k", "rel_bias", "g_out_a", "g_out_b", "g_ple")
LARGE = ("w_in", "w_out", "w_ple_gate", "w_ple_up")
WEIGHTS = ("g_pre", "w_in", "w_s", "b_s", "ln_v_g", "ln_v_b", "g_q", "g_k", "rel_bias", "g_out_a", "g_out_b", "w_out",
           "g_ple", "w_ple_gate", "w_ple_up")


def _pack(parts):
    flat = jnp.concatenate([parts[n].reshape(-1).astype(F32) for n in SMALL])
    rows = -(-flat.shape[0] // (8 * LANE)) * 8
    return jnp.pad(flat, (0, rows * LANE - flat.shape[0])).reshape(rows, LANE)


def _unpack(pack, like):
    flat = pack.reshape(-1)
    out, at = {}, 0
    for n in SMALL:
        size = math.prod(like[n].shape)
        out[n] = flat[at:at + size].reshape(like[n].shape)
        at += size
    return out


def kernel(x, p, g_pre, w_in, w_s, b_s, ln_v_g, ln_v_b, g_q, g_k, rel_bias, g_out_a, g_out_b, w_out, g_ple, w_ple_gate, w_ple_up, loss_target, m_g_pre, m_w_in, m_w_s, m_b_s, m_ln_v_g, m_ln_v_b, m_g_q, m_g_k, m_rel_bias, m_g_out_a, m_g_out_b, m_w_out, m_g_ple, m_w_ple_gate, m_w_ple_up, v_g_pre, v_w_in, v_w_s, v_b_s, v_ln_v_g, v_ln_v_b, v_g_q, v_g_k, v_rel_bias, v_g_out_a, v_g_out_b, v_w_out, v_g_ple, v_w_ple_gate, v_w_ple_up):
    given = dict(locals())
    weights = {n: given[n] for n in WEIGHTS}
    mom_m = {n: given["m_" + n] for n in WEIGHTS}
    mom_v = {n: given["v_" + n] for n in WEIGHTS}
    xs, ps, tgt = x[0], p[0, 0], loss_target[0]
    d = xs.shape[1]

    core = lax.axis_index("c").astype(jnp.int32).reshape(1)
    chip = (2 * lax.axis_index("x") + lax.axis_index("y")).astype(jnp.int32).reshape(1)

    win4, wout4, wgate4, wup4 = [_cast_bf16("cast_" + n, chip, weights[n][0]) for n in LARGE]

    sp = {
        "g_pre": g_pre, "w_s": w_s[0], "b_s": b_s[0], "ln_v_g": ln_v_g, "ln_v_b": ln_v_b, "g_q": g_q, "g_k": g_k,
        "rel_bias": rel_bias, "g_out_a": g_out_a, "g_out_b": g_out_b, "g_ple": g_ple,
    }
    loss_local, grad_x, (sums32, got16), small = _local_step(xs, ps, tgt, sp, win4, wout4, wgate4, wup4,
                                                             dist={"chip": chip, "core": core})
    loss = lax.psum(loss_local, MESH_AXES)

    own = dict(zip(LARGE, sums32))
    got = dict(zip(LARGE, got16))
    halves = [_total("total_" + n, chip, core, own[n], got[n]) for n in LARGE[1:]]
    my_pack = _pack(small)
    device = 2 * chip[0] + core[0]
    everyone = lax.dynamic_update_slice(jnp.zeros((N_DEV * my_pack.shape[0], LANE), F32), my_pack,
                                        (device * my_pack.shape[0], 0))
    half_in, (*joined, gathered) = _total("total_w_in", chip, core, own["w_in"], got["w_in"],
                                          comm=_Both(_JoinHalves(halves), _GatherSmall(everyone)))
    grads = dict(zip(LARGE[1:], joined))
    grads["w_in"], = _alone("join_w_in", _JoinHalves([half_in]))

    out_g, out_d, out_m, out_v = {}, {}, {}, {}
    for n in LARGE:
        results = _adamw("adamw_" + n, weights[n][0], grads[n], mom_m[n][0], mom_v[n][0])
        out_g[n], out_d[n], out_m[n], out_v[n] = [r[None] for r in results]

    pg, pd, pm, pv = _adamw_small(gathered, _pack(weights), _pack(mom_m), _pack(mom_v))
    for packed, out in ((pg, out_g), (pd, out_d), (pm, out_m), (pv, out_v)):
        out.update(_unpack(packed, weights))

    return (loss, grad_x[None], *[out_g[n] for n in WEIGHTS], *[out_d[n] for n in WEIGHTS],
            *[out_m[n] for n in WEIGHTS], *[out_v[n] for n in WEIGHTS])
```

```python
import functools
import math

import numpy as np

import jax
import jax.numpy as jnp
from jax import lax
from jax.experimental import pallas as pl
from jax.experimental.pallas import tpu as pltpu

F32 = jnp.float32
BF16 = jnp.bfloat16

HEAD_DIM = 128
CHUNK = 128
BLK = 128
DILATED = ((128, 1), (512, 4), (2048, 16))
NUM_BUCKETS = 32
MAX_DISTANCE = 2048
ATTN_GROUP = 8
ATTN_GROUP_FWD = 16
EPS = 1e-6
NEG_INF = -1e30
N_CHIPS = 4
N_DEV = 8
MESH_AXES = ("x", "y", "c")

ADAM_LR = 0.001
ADAM_B1 = 0.9
ADAM_B2 = 0.999
ADAM_EPS = 1e-08
ADAM_WD = 0.01
ADAM_STEP = 10

V7X_VMEM_LIMIT = 58 * 1024 * 1024
LANE = 128
MESH = pl.DeviceIdType.MESH
ANY = pl.BlockSpec(memory_space=pl.ANY)


def _params(*sem):
    return pltpu.CompilerParams(dimension_semantics=sem or None, vmem_limit_bytes=V7X_VMEM_LIMIT)


def _tile(n, target):
    if n <= target:
        return n
    t = (target // LANE) * LANE
    while t > LANE and n % t:
        t -= LANE
    assert n % t == 0, (n, target)
    return t


def _gelu(x):
    return 0.5 * x * (1.0 + lax.erf(x * (1.0 / math.sqrt(2.0))))


def _gelu_and_grad(x):
    cdf = 0.5 * (1.0 + lax.erf(x * (1.0 / math.sqrt(2.0))))
    return x * cdf, cdf + x * jnp.exp(-0.5 * x * x) * (1.0 / math.sqrt(2.0 * math.pi))


def _silu_and_grad(x):
    s = jax.nn.sigmoid(x)
    return x * s, s * (1.0 + x * (1.0 - s))


def _dot(a, b, ta=False, tb=False):
    return lax.dot_general(a, b, (((0 if ta else 1,), (1 if tb else 0,)), ((), ())), preferred_element_type=F32)


def _colsum8(v):
    return jnp.sum(v.reshape(v.shape[0] // 8, 8, v.shape[1]), axis=0)


def _matmul(name, a, b, a_spec, b_spec, grid, acc_shape, out_shape, out_specs, ta=False, tb=False,
            extras=(), extra_specs=(), epilogue=None, comm=None, prefetch=()):
    nk = grid[2]
    n_pre = len(prefetch)
    n_extra = len(extras)
    single = not isinstance(out_shape, (tuple, list))
    outs_shape = (out_shape,) if single else tuple(out_shape)
    outs_specs = (out_specs,) if single else tuple(out_specs)
    n_out = len(outs_shape)
    c_ins = list(comm.ins) if comm else []
    c_outs = list(comm.out_shape) if comm else []
    c_scratch = list(comm.scratch) if comm else []
    n_cin, n_cout = len(c_ins), len(c_outs)

    def finish(acc, extra_refs, out_refs):
        if epilogue is None:
            out_refs[0][...] = acc.astype(out_refs[0].dtype)
        else:
            epilogue(acc, extra_refs, out_refs)

    def body(*refs):
        a_ref, b_ref, *rest = refs[n_pre:]
        extra_refs = rest[:n_extra]
        cin_refs = rest[n_extra:n_extra + n_cin]
        out_refs = rest[n_extra + n_cin:n_extra + n_cin + n_out]
        cout_refs = rest[n_extra + n_cin + n_out:n_extra + n_cin + n_out + n_cout]
        scratch_refs = rest[n_extra + n_cin + n_out + n_cout:]
        ids = [pl.program_id(ax) for ax in range(3)]
        if comm:
            sems = scratch_refs[len(scratch_refs) - len(c_scratch):]

            @pl.when((ids[0] == 0) & (ids[1] == 0) & (ids[2] == 0))
            def _():
                comm.start(cin_refs, cout_refs, sems)

        if nk == 1:
            finish(_dot(a_ref[...], b_ref[...], ta, tb), extra_refs, out_refs)
        else:
            acc_ref = scratch_refs[0]

            @pl.when(ids[2] == 0)
            def _():
                acc_ref[...] = jnp.zeros_like(acc_ref)

            acc_ref[...] += _dot(a_ref[...], b_ref[...], ta, tb)

            @pl.when(ids[2] == nk - 1)
            def _():
                finish(acc_ref[...], extra_refs, out_refs)

        if comm:
            @pl.when((ids[0] == grid[0] - 1) & (ids[1] == grid[1] - 1) & (ids[2] == nk - 1))
            def _():
                comm.finish(cin_refs, cout_refs, sems)

    scratch = ([] if nk == 1 else [pltpu.VMEM(acc_shape, F32)]) + c_scratch
    aliases = {n_pre + 2 + n_extra + i: n_out + i for i in range(n_cin)} if (comm and comm.in_place) else {}
    res = pl.pallas_call(
        body, name=name,
        grid_spec=pltpu.PrefetchScalarGridSpec(
            num_scalar_prefetch=n_pre, grid=grid,
            in_specs=[a_spec, b_spec, *extra_specs] + [ANY] * n_cin,
            out_specs=list(outs_specs) + [ANY] * n_cout, scratch_shapes=scratch),
        out_shape=list(outs_shape) + c_outs,
        input_output_aliases=aliases,
        compiler_params=_params(*(("arbitrary",) * 3 if comm else ("parallel", "parallel", "arbitrary"))),
    )(*prefetch, a, b, *extras, *c_ins)
    if comm:
        main = res[:n_out]
        return (main[0] if single else main), res[n_out:]
    return res[0] if single else res


def _shard_spec(rows, cols, per_shard, row_axis, col_axis):
    def index(i, j, k):
        g = (i, j, k)
        return (g[col_axis] // per_shard, g[row_axis], g[col_axis] % per_shard)
    return pl.BlockSpec((None, rows, cols), index)


def _rms_fwd(name, x, g):
    s, d = x.shape
    tm = _tile(s, 256)

    def body(x_ref, g_ref, y_ref, r_ref):
        xf = x_ref[...]
        r = lax.rsqrt(jnp.mean(xf * xf, axis=-1, keepdims=True) + EPS)
        y_ref[...] = (xf * r * g_ref[...]).astype(BF16)
        r_ref[...] = r

    return pl.pallas_call(
        body, name=name, grid=(s // tm,),
        in_specs=[pl.BlockSpec((tm, d), lambda i: (i, 0)), pl.BlockSpec((1, d), lambda i: (0, 0))],
        out_specs=(pl.BlockSpec((tm, d), lambda i: (i, 0)), pl.BlockSpec((tm, 1), lambda i: (i, 0))),
        out_shape=(jax.ShapeDtypeStruct((s, d), BF16), jax.ShapeDtypeStruct((s, 1), F32)),
        compiler_params=_params("parallel"),
    )(x, g)


def _rms_bwd(name, dy, x, r, g, skip, with_bf16):
    s, d = x.shape
    tm = _tile(s, 256)
    n = s // tm

    def body(dy_ref, x_ref, r_ref, g_ref, skip_ref, *outs):
        dx_ref = outs[0]
        dg_ref = outs[-2]
        acc_ref = outs[-1]
        i = pl.program_id(0)
        dyv, xv, rv = dy_ref[...].astype(F32), x_ref[...], r_ref[...]

        @pl.when(i == 0)
        def _():
            acc_ref[...] = jnp.zeros_like(acc_ref)

        acc_ref[...] += _colsum8(dyv * xv * rv)
        dg = dyv * g_ref[...]
        dx = skip_ref[...].astype(F32) + rv * (dg - xv * (rv * rv) * jnp.mean(dg * xv, axis=-1, keepdims=True))
        dx_ref[...] = dx
        if with_bf16:
            outs[1][...] = dx.astype(BF16)

        @pl.when(i == n - 1)
        def _():
            dg_ref[...] = jnp.sum(acc_ref[...], axis=0, keepdims=True)

    row = pl.BlockSpec((tm, d), lambda i: (i, 0))
    vec = pl.BlockSpec((1, d), lambda i: (0, 0))
    out_specs = [row] + ([row] if with_bf16 else []) + [vec]
    out_shape = [jax.ShapeDtypeStruct((s, d), F32)] + ([jax.ShapeDtypeStruct((s, d), BF16)] if with_bf16 else []) \
        + [jax.ShapeDtypeStruct((1, d), F32)]
    return pl.pallas_call(
        body, name=name, grid=(n,),
        in_specs=[row, row, pl.BlockSpec((tm, 1), lambda i: (i, 0)), vec, row],
        out_specs=out_specs, out_shape=out_shape, scratch_shapes=[pltpu.VMEM((8, d), F32)],
        compiler_params=_params("arbitrary"),
    )(dy, x, r, g, skip)


def _causal(w):
    t = lax.broadcasted_iota(jnp.int32, w.shape, 0)
    s_ = lax.broadcasted_iota(jnp.int32, w.shape, 1)
    return jnp.where(t >= s_, w, 0.0)


def _gmlp_fwd(proj, w_s, b_st, ln_g, ln_b, g_out, d_model):
    s = proj.shape[0]
    wa = d_model // 2
    groups = wa // HEAD_DIM
    tm = _tile(s, 256)
    n_chunks = tm // CHUNK

    def body(au_ref, av_ref, az_ref, ws_ref, bst_ref, lng_ref, lnb_ref, go_ref,
             y_ref, z_ref, vn_ref, mu_ref, rs_ref, ra_ref):
        gv = _gelu(av_ref[...])
        mu = jnp.mean(gv, axis=-1, keepdims=True)
        xc = gv - mu
        rs = lax.rsqrt(jnp.mean(xc * xc, axis=-1, keepdims=True) + EPS)
        vn = (xc * rs * lng_ref[...] + lnb_ref[...]).astype(BF16)
        vn_ref[...] = vn
        mu_ref[...] = mu
        rs_ref[...] = rs
        for g in range(groups):
            wm = _causal(ws_ref[g]).astype(BF16)
            cols = slice(g * HEAD_DIM, (g + 1) * HEAD_DIM)
            for ch in range(n_chunks):
                rows = slice(ch * CHUNK, (ch + 1) * CHUNK)
                z_ref[rows, cols] = _dot(wm, vn_ref[rows, cols]) + bst_ref[:, g:g + 1]
        ya = _gelu(au_ref[...]) * z_ref[...]
        ra = lax.rsqrt(jnp.mean(ya * ya, axis=-1, keepdims=True) + EPS)
        ra_ref[...] = ra
        sz, _ = _silu_and_grad(az_ref[...])
        y_ref[...] = (ya * ra * go_ref[...] * sz).astype(BF16)

    def col(j):
        return pl.BlockSpec((tm, wa), lambda i: (i, j))
    vec = pl.BlockSpec((1, wa), lambda i: (0, 0))
    stat = pl.BlockSpec((tm, 1), lambda i: (i, 0))
    return pl.pallas_call(
        body, name="gmlp_fwd", grid=(s // tm,),
        in_specs=[col(0), col(1), col(2),
                  pl.BlockSpec((groups, CHUNK, CHUNK), lambda i: (0, 0, 0)),
                  pl.BlockSpec((CHUNK, groups), lambda i: (0, 0)), vec, vec, vec],
        out_specs=(col(0), col(0), col(0), stat, stat, stat),
        out_shape=(jax.ShapeDtypeStruct((s, d_model), BF16), jax.ShapeDtypeStruct((s, wa), F32),
                   jax.ShapeDtypeStruct((s, wa), BF16), jax.ShapeDtypeStruct((s, 1), F32),
                   jax.ShapeDtypeStruct((s, 1), F32), jax.ShapeDtypeStruct((s, 1), F32)),
        compiler_params=_params("parallel"),
    )(proj, proj, proj, w_s, b_st, ln_g, ln_b, g_out)


def _gmlp_bwd(dproj, dy, proj, z, vn, mu, rs, ra, w_s, ln_g, g_out, d_model):
    s = proj.shape[0]
    wa = d_model // 2
    groups = wa // HEAD_DIM
    tm = _tile(s, 256)
    n_chunks = tm // CHUNK
    n = s // tm

    def causal_stack(w):
        t = lax.broadcasted_iota(jnp.int32, w.shape, 1)
        s_ = lax.broadcasted_iota(jnp.int32, w.shape, 2)
        return jnp.where(t >= s_, w, 0.0)

    def body(dproj_in, dy_ref, au_ref, av_ref, az_ref, z_ref, vn_ref, mu_ref, rs_ref, ra_ref, ws_ref, lng_ref, go_ref,
             dp_ref, gws_ref, dzs_ref, glg_ref, glb_ref, ggo_ref,
             dz_s, dvn_s, acc_lg, acc_lb, acc_go):
        del dproj_in
        i = pl.program_id(0)

        @pl.when(i == 0)
        def _():
            gws_ref[...] = jnp.zeros_like(gws_ref)
            dzs_ref[...] = jnp.zeros_like(dzs_ref)
            acc_lg[...] = jnp.zeros_like(acc_lg)
            acc_lb[...] = jnp.zeros_like(acc_lb)
            acc_go[...] = jnp.zeros_like(acc_go)

        au, az, zv, rav = au_ref[...], az_ref[...], z_ref[...], ra_ref[...]
        u, du = _gelu_and_grad(au)
        ya = u * zv
        sz, dsz = _silu_and_grad(az)
        dyv = dy_ref[...].astype(F32)
        dp_ref[:, 2 * wa:3 * wa] = (dyv * (ya * rav * go_ref[...]) * dsz).astype(BF16)
        dn = dyv * sz
        acc_go[...] += _colsum8(dn * ya * rav)
        dyg = dn * go_ref[...]
        dya = rav * (dyg - ya * (rav * rav) * jnp.mean(dyg * ya, axis=-1, keepdims=True))
        dp_ref[:, 0:wa] = (dya * zv * du).astype(BF16)
        dz_s[...] = dya * u
        for ch in range(n_chunks):
            dzs_ref[...] += dz_s[ch * CHUNK:(ch + 1) * CHUNK, :]
        for g in range(groups):
            wm = _causal(ws_ref[g]).astype(BF16)
            cols = slice(g * HEAD_DIM, (g + 1) * HEAD_DIM)
            for ch in range(n_chunks):
                rows = slice(ch * CHUNK, (ch + 1) * CHUNK)
                dzb = dz_s[rows, cols].astype(BF16)
                gws_ref[g] += _dot(dzb, vn_ref[rows, cols], tb=True)
                dvn_s[rows, cols] = _dot(wm, dzb, ta=True)
        av = av_ref[...]
        gv, dgelu_v = _gelu_and_grad(av)
        xh = (gv - mu_ref[...]) * rs_ref[...]
        dvn = dvn_s[...]
        acc_lb[...] += _colsum8(dvn)
        acc_lg[...] += _colsum8(dvn * xh)
        dxh = dvn * lng_ref[...]
        dgv = rs_ref[...] * (dxh - jnp.mean(dxh, axis=-1, keepdims=True) - xh * jnp.mean(dxh * xh, axis=-1, keepdims=True))
        dp_ref[:, wa:2 * wa] = (dgv * dgelu_v).astype(BF16)

        @pl.when(i == n - 1)
        def _():
            gws_ref[...] = causal_stack(gws_ref[...])
            glg_ref[...] = jnp.sum(acc_lg[...], axis=0, keepdims=True)
            glb_ref[...] = jnp.sum(acc_lb[...], axis=0, keepdims=True)
            ggo_ref[...] = jnp.sum(acc_go[...], axis=0, keepdims=True)

    def col(j):
        return pl.BlockSpec((tm, wa), lambda i: (i, j))
    vec = pl.BlockSpec((1, wa), lambda i: (0, 0))
    stat = pl.BlockSpec((tm, 1), lambda i: (i, 0))
    ws_spec = pl.BlockSpec((groups, CHUNK, CHUNK), lambda i: (0, 0, 0))
    d_in = dproj.shape[1]
    return pl.pallas_call(
        body, name="gmlp_bwd", grid=(n,),
        in_specs=[ANY, col(0), col(0), col(1), col(2), col(0), col(0), stat, stat, stat, ws_spec, vec, vec],
        out_specs=(pl.BlockSpec((tm, 3 * wa), lambda i: (i, 0)), ws_spec,
                   pl.BlockSpec((CHUNK, wa), lambda i: (0, 0)), vec, vec, vec),
        out_shape=(jax.ShapeDtypeStruct((s, d_in), BF16),
                   jax.ShapeDtypeStruct((groups, CHUNK, CHUNK), F32), jax.ShapeDtypeStruct((CHUNK, wa), F32))
        + (jax.ShapeDtypeStruct((1, wa), F32),) * 3,
        scratch_shapes=[pltpu.VMEM((tm, wa), F32), pltpu.VMEM((tm, wa), F32)] + [pltpu.VMEM((8, wa), F32)] * 3,
        input_output_aliases={0: 0},
        compiler_params=_params("arbitrary"),
    )(dproj, dy, proj, proj, proj, z, vn, mu, rs, ra, w_s, ln_g, g_out)


def _gate_fwd(y_in, yb, proj, g_out, d_model):
    s = proj.shape[0]
    wa = d_model // 2
    tm = _tile(s, 256)

    def body(y_any, yb_ref, bz_ref, go_ref, y_ref, rb_ref):
        del y_any
        ybv = yb_ref[...]
        rb = lax.rsqrt(jnp.mean(ybv * ybv, axis=-1, keepdims=True) + EPS)
        rb_ref[...] = rb
        sz, _ = _silu_and_grad(bz_ref[...])
        y_ref[...] = (ybv * rb * go_ref[...] * sz).astype(BF16)

    return pl.pallas_call(
        body, name="gate_b_fwd", grid=(s // tm,),
        in_specs=[ANY, pl.BlockSpec((tm, wa), lambda i: (i, 0)), pl.BlockSpec((tm, wa), lambda i: (i, 6)),
                  pl.BlockSpec((1, wa), lambda i: (0, 0))],
        out_specs=(pl.BlockSpec((tm, wa), lambda i: (i, 1)), pl.BlockSpec((tm, 1), lambda i: (i, 0))),
        out_shape=(jax.ShapeDtypeStruct((s, d_model), BF16), jax.ShapeDtypeStruct((s, 1), F32)),
        input_output_aliases={0: 0},
        compiler_params=_params("parallel"),
    )(y_in, yb, proj, g_out)


def _gate_bwd(dy, yb, lse, proj, rb, g_out, d_model):
    s, d_in = proj.shape
    wa = d_model // 2
    heads = wa // HEAD_DIM
    tm = _tile(s, 256)
    n = s // tm

    def body(dy_ref, yb_ref, lse_ref, bz_ref, rb_ref, go_ref, dp_ref, do_ref, st_ref, ggo_ref, acc):
        i = pl.program_id(0)

        @pl.when(i == 0)
        def _():
            acc[...] = jnp.zeros_like(acc)

        dyv, ybv, rbv = dy_ref[...].astype(F32), yb_ref[...], rb_ref[...]
        sz, dsz = _silu_and_grad(bz_ref[...])
        dp_ref[...] = (dyv * (ybv * rbv * go_ref[...]) * dsz).astype(BF16)
        dn = dyv * sz
        acc[...] += _colsum8(dn * ybv * rbv)
        dyg = dn * go_ref[...]
        do = rbv * (dyg - ybv * (rbv * rbv) * jnp.mean(dyg * ybv, axis=-1, keepdims=True))
        do_ref[...] = do
        prod = do * ybv
        first = lax.broadcasted_iota(jnp.int32, (tm, 2), 1) == 0
        for h in range(heads):
            delta = jnp.sum(prod[:, h * HEAD_DIM:(h + 1) * HEAD_DIM], axis=-1, keepdims=True)
            st_ref[h] = jnp.where(first, lse_ref[h], delta)

        @pl.when(i == n - 1)
        def _():
            ggo_ref[...] = jnp.sum(acc[...], axis=0, keepdims=True)

    vec = pl.BlockSpec((1, wa), lambda i: (0, 0))
    return pl.pallas_call(
        body, name="gate_b_bwd", grid=(n,),
        in_specs=[pl.BlockSpec((tm, wa), lambda i: (i, 1)), pl.BlockSpec((tm, wa), lambda i: (i, 0)),
                  pl.BlockSpec((heads, tm, 1), lambda i: (0, i, 0)),
                  pl.BlockSpec((tm, wa), lambda i: (i, 6)), pl.BlockSpec((tm, 1), lambda i: (i, 0)), vec],
        out_specs=(pl.BlockSpec((tm, wa), lambda i: (i, 6)), pl.BlockSpec((tm, wa), lambda i: (i, 0)),
                   pl.BlockSpec((heads, tm, 2), lambda i: (0, i, 0)), vec),
        out_shape=(jax.ShapeDtypeStruct((s, d_in), BF16), jax.ShapeDtypeStruct((s, wa), F32),
                   jax.ShapeDtypeStruct((heads, s, 2), F32), jax.ShapeDtypeStruct((1, wa), F32)),
        scratch_shapes=[pltpu.VMEM((8, wa), F32)],
        compiler_params=_params("arbitrary"),
    )(dy, yb, lse, proj, rb, g_out)


def _bucket_tables():
    qi = BLK + np.arange(BLK)
    kj = np.arange(2 * BLK)
    delta = qi[:, None] - kj[None, :]
    max_exact = NUM_BUCKETS // 2
    tabs = []
    for window, dil in DILATED:
        band = (delta >= 0) & (delta <= window // dil)
        dist = np.clip(delta, 0, None) * dil
        d = np.maximum(dist, 1).astype(np.float32)
        large = max_exact + (np.log(d / np.float32(max_exact)) / np.float32(math.log(MAX_DISTANCE / max_exact))
                             * np.float32(NUM_BUCKETS - max_exact)).astype(np.int32)
        large = np.minimum(large, NUM_BUCKETS - 1)
        tabs.append(np.where(band, np.where(dist < max_exact, dist, large), -1).astype(np.int32))
    return np.stack(tabs)


def _bias_tiles(tab_ref, rb_ref, h, bias_s):
    col = lax.broadcasted_iota(jnp.int32, (BLK, 2 * BLK), 1)
    for i in range(len(DILATED)):
        t = tab_ref[i]
        bias = jnp.zeros(t.shape, F32)
        for b in range(NUM_BUCKETS):
            bias = jnp.where(t == b, rb_ref[b, h], bias)
        bias = jnp.where(t >= 0, bias, NEG_INF)
        bias_s[2 * i] = jnp.where(col >= BLK, bias, NEG_INF)
        bias_s[2 * i + 1] = bias


def _block_rows(b, n_blocks, dil):
    nb = n_blocks // dil
    r = b // nb
    n = b % nb
    start = r + dil * BLK * n
    if dil == 1:
        return pl.ds(pl.multiple_of(start, BLK), BLK), n
    return pl.ds(start, BLK, stride=dil), n


def _sub_block(b, rows=BLK, pad=0):
    return pl.ds(pl.multiple_of(b * BLK + pad, BLK), rows)


def _regroup(b, n_blocks, small, big):
    ratio = big // small
    piece = BLK // ratio
    nb_small, nb_big = n_blocks // small, n_blocks // big
    r, n = b // nb_small, b % nb_small
    for q in range(ratio):
        block = (r + small * q) * nb_big + n // ratio
        yield q, ratio, piece, pl.multiple_of(block * BLK + piece * (n % ratio), piece)


def _rms_rows(x, gain):
    return x * lax.rsqrt(jnp.mean(x * x, axis=-1, keepdims=True) + EPS) * gain


def _attn_fwd(proj, g_q, g_k, rel_bias, d_model, comm=None):
    s = proj.shape[0]
    heads = d_model // 2 // HEAD_DIM
    n_blocks = s // BLK
    scale = HEAD_DIM ** -0.5
    tables = jnp.asarray(_bucket_tables())
    c_ins = list(comm.ins) if comm else []
    c_outs = list(comm.out_shape) if comm else []
    c_scratch = list(comm.scratch) if comm else []

    def body(q_ref, k_ref, v_ref, gq_ref, gk_ref, rb_ref, tab_ref, *rest):
        cin_refs, rest = rest[:len(c_ins)], rest[len(c_ins):]
        yb_ref, lse_ref = rest[:2]
        cout_refs, rest = rest[2:2 + len(c_outs)], rest[2 + len(c_outs):]
        qs, ks, vs, qs_w, ks_w, vs_w, tmp, m_s, l_s, acc_s, bias_s, sc_s, p_s = rest[:13]
        sems = rest[13:]
        h = pl.program_id(0)
        if comm:
            @pl.when(h == 0)
            def _():
                comm.start(cin_refs, cout_refs, sems)

        _bias_tiles(tab_ref, rb_ref, h, bias_s)
        gq = gq_ref[...] * scale
        gk = gk_ref[...]
        for buf in (ks, vs, ks_w, vs_w):
            buf[0:BLK, :] = jnp.zeros((BLK, HEAD_DIM), BF16)

        for i, (_, dil) in enumerate(DILATED):
            def prepare(b, carry, i=i, dil=dil):
                rows, _ = _block_rows(b, n_blocks, dil)
                vals = (_rms_rows(q_ref[rows, :], gq), _rms_rows(k_ref[rows, :], gk), v_ref[rows, :])
                for val, buf, pad in zip(vals, (qs, ks, vs), (0, BLK, BLK)):
                    buf[_sub_block(b, pad=pad), :] = val.astype(BF16)
                if i == 1:
                    slot = b % 2
                    for a, val in enumerate(vals):
                        tmp[slot, a] = val
                    for q, ratio, piece, dst in _regroup(b, n_blocks, dil, DILATED[2][1]):
                        for a, (buf, pad) in enumerate(((qs_w, 0), (ks_w, BLK), (vs_w, BLK))):
                            buf[pl.ds(dst + pad, piece), :] = tmp[slot, a, pl.ds(q, piece, stride=ratio), :].astype(BF16)
                return carry
            if i < 2:
                lax.fori_loop(0, n_blocks, prepare, 0, unroll=8)
            qs, ks, vs = (qs, ks, vs) if i < 2 else (qs_w, ks_w, vs_w)

            def group(g, carry, i=i, dil=dil, qs=qs, ks=ks, vs=vs):
                blocks = [g * ATTN_GROUP_FWD + t for t in range(ATTN_GROUP_FWD)]
                for t, b in enumerate(blocks):
                    sc_s[t] = _dot(qs[_sub_block(b), :], ks[_sub_block(b, 2 * BLK), :], tb=True)
                for t, b in enumerate(blocks):
                    rows, n = _block_rows(b, n_blocks, dil)
                    sc = sc_s[t] + bias_s[2 * i + jnp.minimum(n, 1)]
                    m_i = jnp.max(sc, axis=-1, keepdims=True)
                    pr = jnp.exp(sc - m_i)
                    l_i = jnp.sum(pr, axis=-1, keepdims=True)
                    p_s[t] = pr.astype(BF16)
                    m_s[new, rows, :] = m_i
                    l_s[new, rows, :] = l_i
                for t, b in enumerate(blocks):
                    rows, _ = _block_rows(b, n_blocks, dil)
                    acc_s[new, rows, :] = _dot(p_s[t], vs[_sub_block(b, 2 * BLK), :])
                return carry
            new = min(i, 1)
            lax.fori_loop(0, n_blocks // ATTN_GROUP_FWD, group, 0)
            if i > 0:
                m = jnp.maximum(m_s[0], m_s[1])
                w_old, w_new = jnp.exp(m_s[0] - m), jnp.exp(m_s[1] - m)
                m_s[0] = m
                l_s[0] = l_s[0] * w_old + l_s[1] * w_new
                acc_s[0] = acc_s[0] * w_old + acc_s[1] * w_new

        l = l_s[0]
        yb_ref[...] = acc_s[0] / l
        lse_ref[...] = m_s[0] + jnp.log(l)
        if comm:
            @pl.when(h == heads - 1)
            def _():
                comm.finish(cin_refs, cout_refs, sems)

    def head_col(off):
        return pl.BlockSpec((s, HEAD_DIM), lambda h: (0, off * heads + h))
    vec = pl.BlockSpec((1, HEAD_DIM), lambda h: (0, 0))
    res = pl.pallas_call(
        body, name="attn_fwd", grid=(heads,),
        in_specs=[head_col(3), head_col(4), head_col(5), vec, vec,
                  pl.BlockSpec(memory_space=pltpu.SMEM),
                  pl.BlockSpec((len(DILATED), BLK, 2 * BLK), lambda h: (0, 0, 0))] + [ANY] * len(c_ins),
        out_specs=[pl.BlockSpec((s, HEAD_DIM), lambda h: (0, h)), pl.BlockSpec((None, s, 1), lambda h: (h, 0, 0))]
        + [ANY] * len(c_outs),
        out_shape=[jax.ShapeDtypeStruct((s, heads * HEAD_DIM), F32), jax.ShapeDtypeStruct((heads, s, 1), F32)] + c_outs,
        scratch_shapes=[pltpu.VMEM((s, HEAD_DIM), BF16), pltpu.VMEM((s + BLK, HEAD_DIM), BF16),
                        pltpu.VMEM((s + BLK, HEAD_DIM), BF16),
                        pltpu.VMEM((s, HEAD_DIM), BF16), pltpu.VMEM((s + BLK, HEAD_DIM), BF16),
                        pltpu.VMEM((s + BLK, HEAD_DIM), BF16), pltpu.VMEM((2, 3, BLK, HEAD_DIM), F32),
                        pltpu.VMEM((2, s, 1), F32), pltpu.VMEM((2, s, 1), F32), pltpu.VMEM((2, s, HEAD_DIM), F32),
                        pltpu.VMEM((2 * len(DILATED), BLK, 2 * BLK), F32),
                        pltpu.VMEM((ATTN_GROUP_FWD, BLK, 2 * BLK), F32),
                        pltpu.VMEM((ATTN_GROUP_FWD, BLK, 2 * BLK), BF16)] + c_scratch,
        input_output_aliases={7 + i: 2 + i for i in range(len(c_ins))} if (comm and comm.in_place) else {},
        compiler_params=_params("arbitrary"),
    )(proj, proj, proj, g_q, g_k, rel_bias, tables, *c_ins)
    return res[0], res[1], res[2:]


def _attn_bwd(dproj, d_o, stats, proj, g_q, g_k, rel_bias, d_model, comm=None):
    s, d_in = proj.shape
    heads = d_model // 2 // HEAD_DIM
    n_blocks = s // BLK
    scale = HEAD_DIM ** -0.5
    tables = jnp.asarray(_bucket_tables())
    n_dil = len(DILATED)
    c_ins = list(comm.ins) if comm else []
    c_outs = list(comm.out_shape) if comm else []
    c_scratch = list(comm.scratch) if comm else []

    def body(dp_any, q_ref, k_ref, v_ref, do_ref, st_ref, gq_ref, gk_ref, rb_ref, tab_ref, *rest):
        cin_refs, rest = rest[:len(c_ins)], rest[len(c_ins):]
        dp_out, ggq_ref, ggk_ref, db_ref = rest[:4]
        cout_refs, rest = rest[4:4 + len(c_outs)], rest[4 + len(c_outs):]
        (qs, ks, vs, dos, st_s, qs_w, ks_w, vs_w, dos_w, st_w, tmp, tmp_st,
         dqn, dkn, dvv, dq_u, dk_u, dv_u, bias_s, dbias_s, sc_s, dp_s, p_s, ds_s, sems) = rest[:25]
        comm_sems = rest[25:]
        del dp_any
        h = pl.program_id(0)
        if comm:
            @pl.when(h == 0)
            def _():
                comm.start(cin_refs, cout_refs, comm_sems)


        @pl.when(h == 0)
        def _():
            ggq_ref[...] = jnp.zeros_like(ggq_ref)
            ggk_ref[...] = jnp.zeros_like(ggk_ref)

        dbias_s[...] = jnp.zeros_like(dbias_s)
        _bias_tiles(tab_ref, rb_ref, h, bias_s)
        gq = gq_ref[...] * scale
        gk = gk_ref[...]
        for buf in (ks, vs, ks_w, vs_w):
            buf[0:BLK, :] = jnp.zeros((BLK, HEAD_DIM), BF16)

        for i, (_, dil) in enumerate(DILATED):
            def prepare(b, carry, i=i, dil=dil):
                rows, _ = _block_rows(b, n_blocks, dil)
                vals = (_rms_rows(q_ref[rows, :], gq), _rms_rows(k_ref[rows, :], gk), v_ref[rows, :], do_ref[rows, :])
                for val, buf, pad in zip(vals, (qs, ks, vs, dos), (0, BLK, BLK, 0)):
                    buf[_sub_block(b, pad=pad), :] = val.astype(BF16)
                st = st_ref[rows, :]
                st_s[_sub_block(b), :] = st
                if i == 1:
                    slot = b % 2
                    for a, val in enumerate(vals):
                        tmp[slot, a] = val
                    tmp_st[slot] = st
                    for q, ratio, piece, dst in _regroup(b, n_blocks, dil, DILATED[2][1]):
                        for a, (buf, pad) in enumerate(((qs_w, 0), (ks_w, BLK), (vs_w, BLK), (dos_w, 0))):
                            buf[pl.ds(dst + pad, piece), :] = tmp[slot, a, pl.ds(q, piece, stride=ratio), :].astype(BF16)
                        st_w[pl.ds(dst, piece), :] = tmp_st[slot, pl.ds(q, piece, stride=ratio), :]
                return carry
            if i < 2:
                lax.fori_loop(0, n_blocks, prepare, 0, unroll=8)
            dk_u[0:BLK, :] = jnp.zeros((BLK, HEAD_DIM), F32)
            dv_u[0:BLK, :] = jnp.zeros((BLK, HEAD_DIM), F32)
            operands = (qs, ks, vs, dos, st_s) if i < 2 else (qs_w, ks_w, vs_w, dos_w, st_w)

            def group(g, carry, i=i, dil=dil, operands=operands):
                qs, ks, vs, dos, st_s = operands
                blocks = [g * ATTN_GROUP + t for t in range(ATTN_GROUP)]
                for t, b in enumerate(blocks):
                    sc_s[t] = _dot(qs[_sub_block(b), :], ks[_sub_block(b, 2 * BLK), :], tb=True)
                    dp_s[t] = _dot(dos[_sub_block(b), :], vs[_sub_block(b, 2 * BLK), :], tb=True)
                for t, b in enumerate(blocks):
                    _, n = _block_rows(b, n_blocks, dil)
                    st = st_s[_sub_block(b), :]
                    pr = jnp.exp(sc_s[t] + bias_s[2 * i + jnp.minimum(n, 1)] - st[:, 0:1])
                    ds = pr * (dp_s[t] - st[:, 1:2])
                    dbias_s[i] += ds
                    p_s[t] = pr.astype(BF16)
                    ds_s[t] = ds.astype(BF16)
                for t, b in enumerate(blocks):
                    dq_u[_sub_block(b), :] = _dot(ds_s[t], ks[_sub_block(b, 2 * BLK), :])
                    for part, lhs, rhs in ((dk_u, ds_s, qs), (dv_u, p_s, dos)):
                        both = _dot(lhs[t], rhs[_sub_block(b), :], ta=True)
                        part[_sub_block(b), :] += both[:BLK]
                        part[_sub_block(b, pad=BLK), :] = both[BLK:]
                return carry
            lax.fori_loop(0, n_blocks // ATTN_GROUP, group, 0)

            def scatter(b, carry, i=i, dil=dil):
                if i < 2:
                    rows, _ = _block_rows(b, n_blocks, dil)
                    for acc, part, pad in ((dqn, dq_u, 0), (dkn, dk_u, BLK), (dvv, dv_u, BLK)):
                        val = part[_sub_block(b, pad=pad), :]
                        acc[rows, :] = val if i == 0 else acc[rows, :] + val
                else:
                    middle = DILATED[1][1]
                    rows, _ = _block_rows(b, n_blocks, middle)
                    slot = b % 2
                    for a, (acc, part, pad) in enumerate(((dqn, dq_u, 0), (dkn, dk_u, BLK), (dvv, dv_u, BLK))):
                        for q, ratio, piece, src in _regroup(b, n_blocks, middle, dil):
                            tmp[slot, a, pl.ds(q, piece, stride=ratio), :] = part[pl.ds(src + pad, piece), :]
                        acc[rows, :] += tmp[slot, a]
                return carry
            lax.fori_loop(0, n_blocks, scatter, 0, unroll=8)

        q = q_ref[...]
        rq = lax.rsqrt(jnp.mean(q * q, axis=-1, keepdims=True) + EPS)
        k = k_ref[...]
        rk = lax.rsqrt(jnp.mean(k * k, axis=-1, keepdims=True) + EPS)
        dq_n = dqn[...]
        ggq_ref[...] += jnp.sum(dq_n * (q * rq) * scale, axis=0, keepdims=True)
        dg = dq_n * gq_ref[...] * scale
        qs[...] = (rq * (dg - q * (rq * rq) * jnp.mean(dg * q, axis=-1, keepdims=True))).astype(BF16)
        dk_n = dkn[...]
        ggk_ref[...] += jnp.sum(dk_n * (k * rk), axis=0, keepdims=True)
        dg = dk_n * gk_ref[...]
        dos[...] = (rk * (dg - k * (rk * rk) * jnp.mean(dg * k, axis=-1, keepdims=True))).astype(BF16)
        vs[BLK:, :] = dvv[...].astype(BF16)
        copies = [pltpu.make_async_copy(src, dp_out.at[:, pl.ds(pl.multiple_of(((3 + j) * heads + h) * HEAD_DIM, HEAD_DIM), HEAD_DIM)],
                                        sems.at[j]) for j, src in enumerate((qs, dos, vs.at[pl.ds(BLK, s)]))]
        for cp in copies:
            cp.start()
        for b in range(NUM_BUCKETS):
            tot = jnp.zeros((BLK, 2 * BLK), F32)
            for i in range(n_dil):
                tot = tot + jnp.where(tab_ref[i] == b, dbias_s[i], 0.0)
            db_ref[b:b + 1, :] = jnp.full((1, LANE), jnp.sum(tot), F32)
        for cp in copies:
            cp.wait()
        if comm:
            @pl.when(h == heads - 1)
            def _():
                comm.finish(cin_refs, cout_refs, comm_sems)

    def head_col(off):
        return pl.BlockSpec((s, HEAD_DIM), lambda h: (0, off * heads + h))
    own_col = pl.BlockSpec((s, HEAD_DIM), lambda h: (0, h))
    vec = pl.BlockSpec((1, HEAD_DIM), lambda h: (0, 0))
    big = pltpu.VMEM((s, HEAD_DIM), F32)
    padded32 = pltpu.VMEM((s + BLK, HEAD_DIM), F32)
    padded16 = pltpu.VMEM((s + BLK, HEAD_DIM), BF16)
    res = pl.pallas_call(
        body, name="attn_bwd", grid=(heads,),
        in_specs=[ANY, head_col(3), head_col(4), head_col(5), own_col,
                  pl.BlockSpec((None, s, 2), lambda h: (h, 0, 0)), vec, vec,
                  pl.BlockSpec(memory_space=pltpu.SMEM),
                  pl.BlockSpec((n_dil, BLK, 2 * BLK), lambda h: (0, 0, 0))] + [ANY] * len(c_ins),
        out_specs=[ANY, vec, vec, pl.BlockSpec((None, NUM_BUCKETS, LANE), lambda h: (h, 0, 0))] + [ANY] * len(c_outs),
        out_shape=[jax.ShapeDtypeStruct((s, d_in), BF16), jax.ShapeDtypeStruct((1, HEAD_DIM), F32),
                   jax.ShapeDtypeStruct((1, HEAD_DIM), F32), jax.ShapeDtypeStruct((heads, NUM_BUCKETS, LANE), F32)]
        + c_outs,
        scratch_shapes=[pltpu.VMEM((s, HEAD_DIM), BF16), padded16, padded16, pltpu.VMEM((s, HEAD_DIM), BF16),
                        pltpu.VMEM((s, 2), F32),
                        pltpu.VMEM((s, HEAD_DIM), BF16), padded16, padded16, pltpu.VMEM((s, HEAD_DIM), BF16),
                        pltpu.VMEM((s, 2), F32),
                        pltpu.VMEM((2, 4, BLK, HEAD_DIM), F32), pltpu.VMEM((2, BLK, 2), F32),
                        big, big, big, big, padded32, padded32,
                        pltpu.VMEM((2 * n_dil, BLK, 2 * BLK), F32), pltpu.VMEM((n_dil, BLK, 2 * BLK), F32),
                        pltpu.VMEM((ATTN_GROUP, BLK, 2 * BLK), F32), pltpu.VMEM((ATTN_GROUP, BLK, 2 * BLK), F32),
                        pltpu.VMEM((ATTN_GROUP, BLK, 2 * BLK), BF16), pltpu.VMEM((ATTN_GROUP, BLK, 2 * BLK), BF16),
                        pltpu.SemaphoreType.DMA((3,))] + c_scratch,
        input_output_aliases={0: 0},
        compiler_params=_params("arbitrary"),
    )(dproj, proj, proj, proj, d_o, stats, g_q, g_k, rel_bias, tables, *c_ins)
    return res[0], res[1], res[2], res[3], res[4:]


def _local_step(xs, ps, tgt, sp, win4, wout4, wgate4, wup4, dist=None):
    s, d = xs.shape
    wa = d // 2
    heads = wa // HEAD_DIM
    d_in = 7 * wa
    c4 = d_in // N_CHIPS
    dple = ps.shape[1]

    tm, tn = _tile(s, 1024), _tile(d, 1024)
    tn_in = _tile(c4, 1024)
    per_in = c4 // tn_in
    tn_up = _tile(d // N_CHIPS, 512)
    per_up = (d // N_CHIPS) // tn_up
    gm, gn = s // tm, d // tn

    hn, rx = _rms_fwd("rms_pre", xs, sp["g_pre"])
    if dist:
        proj, win4 = _mm_in_gather(dist["chip"], hn, win4)
    else:
        proj = _matmul(
            "mm_in", hn, win4, pl.BlockSpec((tm, d), lambda i, j, k: (i, 0)), _shard_spec(d, tn_in, per_in, 2, 1),
            (gm, d_in // tn_in, 1), None, jax.ShapeDtypeStruct((s, d_in), F32),
            pl.BlockSpec((tm, tn_in), lambda i, j, k: (i, j)))
    b_st = sp["b_s"].T
    y, z, vn, mu_v, rs_v, ra = _gmlp_fwd(proj, sp["w_s"], b_st, sp["ln_v_g"], sp["ln_v_b"], sp["g_out_a"], d)
    yb, lse, gathered = _attn_fwd(proj, sp["g_q"], sp["g_k"], sp["rel_bias"], d,
                                  comm=_Gather([wout4, wgate4, wup4]) if dist else None)
    if dist:
        wout4, wgate4, wup4 = gathered
    wout, wgate = wout4.reshape(d, d), wgate4.reshape(d, d)
    y, rb = _gate_fwd(y, yb, proj, sp["g_out_b"], d)

    def residual(acc, extra, outs):
        outs[0][...] = extra[0][...] + acc

    tile = pl.BlockSpec((tm, tn), lambda i, j, k: (i, j))
    tile_up = pl.BlockSpec((tm, tn_up), lambda i, j, k: (i, j))
    h = _matmul(
        "mm_out", y, wout, pl.BlockSpec((tm, d), lambda i, j, k: (i, 0)), pl.BlockSpec((d, tn), lambda i, j, k: (0, j)),
        (gm, gn, 1), None, jax.ShapeDtypeStruct((s, d), F32), tile, extras=(xs,), extra_specs=(tile,),
        epilogue=residual)
    hp, rh = _rms_fwd("rms_ple", h, sp["g_ple"])
    p16 = ps.astype(BF16)

    def head(acc, extra, outs):
        p_ref, wup_ref, h_ref, t_ref = extra
        dout_ref, dgl_ref, dup_ref, loss_ref = outs
        up = _dot(p_ref[...], wup_ref[...])
        gate = jax.nn.sigmoid(acc)
        err = h_ref[...] + gate * up - t_ref[...]
        sq = _colsum8(err * err)
        part = sq[:, 0:LANE]
        for c in range(1, sq.shape[1] // LANE):
            part = part + sq[:, c * LANE:(c + 1) * LANE]
        loss_ref[...] = part
        dout = err * (1.0 / d)
        dout_ref[...] = dout.astype(BF16)
        dup_ref[...] = (dout * gate).astype(BF16)
        dgl_ref[...] = (dout * up * gate * (1.0 - gate)).astype(BF16)

    tile_up = pl.BlockSpec((tm, tn_up), lambda i, j, k: (i, j))
    dout, dglin, dup, loss_parts = _matmul(
        "mm_gate_loss", hp, wgate, pl.BlockSpec((tm, d), lambda i, j, k: (i, 0)), pl.BlockSpec((d, tn_up), lambda i, j, k: (0, j)),
        (gm, d // tn_up, 1), None,
        (jax.ShapeDtypeStruct((s, d), BF16), jax.ShapeDtypeStruct((s, d), BF16), jax.ShapeDtypeStruct((s, d), BF16),
         jax.ShapeDtypeStruct((gm * 8, (d // tn_up) * LANE), F32)),
        (tile_up, tile_up, tile_up, pl.BlockSpec((8, LANE), lambda i, j, k: (i, j))),
        extras=(p16, wup4, h, tgt),
        extra_specs=(pl.BlockSpec((tm, dple), lambda i, j, k: (i, 0)), _shard_spec(dple, tn_up, per_up, 2, 1),
                     tile_up, tile_up),
        epilogue=head)
    loss = 0.5 * jnp.sum(loss_parts) * (1.0 / d)

    dhp = _matmul(
        "mm_dhp", dglin, wgate, pl.BlockSpec((tm, d), lambda i, j, k: (i, 0)), pl.BlockSpec((tn, d), lambda i, j, k: (j, 0)),
        (gm, gn, 1), None, jax.ShapeDtypeStruct((s, d), BF16), tile, tb=True)
    dh, dh16, g_g_ple = _rms_bwd("rms_ple_bwd", dhp, h, rh, sp["g_ple"], dout, True)
    tmw = _tile(d, 1024)
    g_wout = _matmul(
        "mm_gwout", y, dh16, pl.BlockSpec((s, tmw), lambda i, j, k: (0, i)), pl.BlockSpec((s, tn), lambda i, j, k: (0, j)),
        (d // tmw, gn, 1), None, jax.ShapeDtypeStruct((d, d), F32), pl.BlockSpec((tmw, tn), lambda i, j, k: (i, j)), ta=True)
    g_wout4 = g_wout.reshape(N_CHIPS, d // N_CHIPS, d)
    g_wgate = _matmul(
        "mm_gwgate", hp, dglin, pl.BlockSpec((s, tmw), lambda i, j, k: (0, i)), pl.BlockSpec((s, tn), lambda i, j, k: (0, j)),
        (d // tmw, gn, 1), None, jax.ShapeDtypeStruct((d, d), F32), pl.BlockSpec((tmw, tn), lambda i, j, k: (i, j)), ta=True,
        comm=_SwapHalves([g_wout4]) if dist else None)
    if dist:
        g_wgate, swapped_out = g_wgate
    g_wgate4 = g_wgate.reshape(N_CHIPS, d // N_CHIPS, d)
    g_wup = _matmul(
        "mm_gwup", p16, dup, pl.BlockSpec((s, dple), lambda i, j, k: (0, 0)), pl.BlockSpec((s, tn_up), lambda i, j, k: (0, j)),
        (1, d // tn_up, 1), None, jax.ShapeDtypeStruct((N_CHIPS, dple, d // N_CHIPS), F32),
        _shard_spec(dple, tn_up, per_up, 0, 1), ta=True)
    dy = _matmul(
        "mm_dy", dh16, wout, pl.BlockSpec((tm, d), lambda i, j, k: (i, 0)), pl.BlockSpec((tn, d), lambda i, j, k: (j, 0)),
        (gm, gn, 1), None, jax.ShapeDtypeStruct((s, d), BF16), tile, tb=True,
        comm=_SwapHalves([g_wgate4, g_wup]) if dist else None)
    early = (g_wout4, g_wgate4, g_wup)
    if dist:
        core = dist["core"]
        dy, swapped_rest = dy
        early_sums = [_chip_sum("chip_sum_" + n, dist["chip"], core, g, o)
                      for n, g, o in zip(LARGE[1:], early, list(swapped_out) + list(swapped_rest))]

    dproj, d_o, stats, g_g_out_b = _gate_bwd(dy, yb, lse, proj, rb, sp["g_out_b"], d)
    dproj, g_g_q, g_g_k, dbias, early_got = _attn_bwd(
        dproj, d_o, stats, proj, sp["g_q"], sp["g_k"], sp["rel_bias"], d,
        comm=_ExchangeChips([s16 for _, s16 in early_sums]) if dist else None)
    dproj, g_w_s, dz_sum, g_ln_g, g_ln_b, g_g_out_a = _gmlp_bwd(
        dproj, dy, proj, z, vn, mu_v, rs_v, ra, sp["w_s"], sp["ln_v_g"], sp["g_out_a"], d)

    if dist:
        tmw_half = _tile(d // 2, 1024)
        n_half = (d // 2) // tmw_half

        def half_of_gwin(name, own, comm):
            def rows(i, j, k, core_ref):
                return (0, (core_ref[0] if own else 1 - core_ref[0]) * n_half + i)
            return _matmul(
                name, hn, dproj, pl.BlockSpec((s, tmw_half), rows),
                pl.BlockSpec((s, tn_in), lambda i, j, k, core_ref: (0, j)),
                (n_half, d_in // tn_in, 1), None, jax.ShapeDtypeStruct((N_CHIPS, d // 2, c4), F32 if own else BF16),
                pl.BlockSpec((None, tmw_half, tn_in), lambda i, j, k, core_ref: (j // per_in, i, j % per_in)), ta=True,
                comm=comm, prefetch=(core,))

        g_theirs = half_of_gwin("mm_gwin_theirs", False, None)
        g_mine, (g_from_sibling,) = half_of_gwin("mm_gwin_mine", True, _SwapHalves([g_theirs], whole=True))
        win_sum = _chip_sum("chip_sum_w_in", dist["chip"], core, g_mine, g_from_sibling)
    else:
        g_win = _matmul(
            "mm_gwin", hn, dproj, pl.BlockSpec((s, tmw), lambda i, j, k: (0, i)),
            pl.BlockSpec((s, tn_in), lambda i, j, k: (0, j)),
            (d // tmw, d_in // tn_in, 1), None, jax.ShapeDtypeStruct((N_CHIPS, d, c4), F32),
            _shard_spec(tmw, tn_in, per_in, 0, 1), ta=True)
    tk = c4
    tmh, tnh = _tile(s, 1024), _tile(d, 1024)
    dhn = _matmul(
        "mm_dhn", dproj, win4, pl.BlockSpec((tmh, tk), lambda i, j, k: (i, k)), _shard_spec(tnh, tk, 1, 1, 2),
        (s // tmh, d // tnh, d_in // tk), (tmh, tnh), jax.ShapeDtypeStruct((s, d), BF16),
        pl.BlockSpec((tmh, tnh), lambda i, j, k: (i, j)), tb=True,
        comm=_ExchangeChips([win_sum[1]]) if dist else None)
    if dist:
        dhn, win_got = dhn
    grad_x, g_g_pre = _rms_bwd("rms_pre_bwd", dhn, xs, rx, sp["g_pre"], dh, False)

    small = {
        "g_pre": g_g_pre,
        "w_s": g_w_s,
        "b_s": jnp.sum(dz_sum.reshape(CHUNK, heads, HEAD_DIM), axis=-1).T,
        "ln_v_g": g_ln_g, "ln_v_b": g_ln_b,
        "g_q": g_g_q, "g_k": g_g_k,
        "rel_bias": dbias[:, :, 0].T,
        "g_out_a": g_g_out_a, "g_out_b": g_g_out_b,
        "g_ple": g_g_ple,
    }
    if not dist:
        return loss, grad_x, (g_win, *early), small
    sums32 = [win_sum[0]] + [s32 for s32, _ in early_sums]
    return loss, grad_x, (sums32, list(win_got) + list(early_got)), small


def _place():
    x, y, c = lax.axis_index("x"), lax.axis_index("y"), lax.axis_index("c")
    chips = [(1 - x, y), (x, 1 - y), (1 - x, 1 - y)]
    return x, y, c, chips


def _remote(src, dst, send_sems, recv_sems, k, to):
    return pltpu.make_async_remote_copy(src_ref=src, dst_ref=dst, send_sem=send_sems.at[k], recv_sem=recv_sems.at[k],
                                        device_id=to, device_id_type=MESH)


def _cast_bf16(name, chip, w):
    r, c = w.shape
    tm = _tile(r, 256)

    def body(chip_ref, w_ref, o_ref):
        del chip_ref
        o_ref[...] = w_ref[...].astype(BF16)

    return pl.pallas_call(
        body, name=name,
        grid_spec=pltpu.PrefetchScalarGridSpec(
            num_scalar_prefetch=1, grid=(r // tm,),
            in_specs=[pl.BlockSpec((tm, c), lambda i, chip_ref: (i, 0))],
            out_specs=pl.BlockSpec((None, tm, c), lambda i, chip_ref: (chip_ref[0], i, 0))),
        out_shape=jax.ShapeDtypeStruct((N_CHIPS, r, c), BF16), compiler_params=_params("parallel"),
    )(chip, w)


class _Gather:
    in_place = True

    def __init__(self, fulls):
        self.ins = list(fulls)
        self.out_shape = [jax.ShapeDtypeStruct(f.shape, f.dtype) for f in fulls]
        n = len(fulls)
        self.scratch = [pltpu.SemaphoreType.DMA((6 * n,)), pltpu.SemaphoreType.DMA((6 * n,))]

    @staticmethod
    def _sends(outs, sems):
        send_sems, recv_sems = sems
        x, y, c, chips = _place()
        cps = []
        for w, ref in enumerate(outs):
            half = ref.shape[1] // 2
            blk = ref.at[2 * x + y, pl.ds(c * half, half)]
            cps += [_remote(blk, blk, send_sems, recv_sems, 6 * w + q, (*chip, c)) for q, chip in enumerate(chips)]
        return cps

    def start(self, ins, outs, sems):
        for cp in self._sends(outs, sems):
            cp.start()

    def finish(self, ins, outs, sems):
        send_sems, recv_sems = sems
        x, y, c, chips = _place()
        sibling = (x, y, 1 - c)
        forwards = []
        for w, ref in enumerate(outs):
            half = ref.shape[1] // 2
            for q, chip in enumerate(chips):
                blk = ref.at[2 * chip[0] + chip[1], pl.ds(c * half, half)]
                _remote(blk, blk, send_sems, recv_sems, 6 * w + q, sibling).wait_recv()
                fwd = _remote(blk, blk, send_sems, recv_sems, 6 * w + 3 + q, sibling)
                fwd.start()
                forwards.append(fwd)
        for w, ref in enumerate(outs):
            half = ref.shape[1] // 2
            for q, chip in enumerate(chips):
                blk = ref.at[2 * chip[0] + chip[1], pl.ds((1 - c) * half, half)]
                _remote(blk, blk, send_sems, recv_sems, 6 * w + 3 + q, sibling).wait_recv()
        for cp in self._sends(outs, sems) + forwards:
            cp.wait_send()


class _ExchangeChips:
    in_place = False

    def __init__(self, sums16):
        self.ins = list(sums16)
        self.out_shape = [jax.ShapeDtypeStruct(g.shape, g.dtype) for g in sums16]
        n = len(sums16)
        self.scratch = [pltpu.SemaphoreType.DMA((3 * n,)), pltpu.SemaphoreType.DMA((3 * n,))]

    @staticmethod
    def _sends(ins, outs, sems):
        send_sems, recv_sems = sems
        x, y, c, chips = _place()
        return [_remote(ins[w].at[2 * chip[0] + chip[1]], outs[w].at[2 * x + y], send_sems, recv_sems, 3 * w + q, (*chip, c))
                for w in range(len(ins)) for q, chip in enumerate(chips)]

    def start(self, ins, outs, sems):
        for cp in self._sends(ins, outs, sems):
            cp.start()

    def finish(self, ins, outs, sems):
        send_sems, recv_sems = sems
        x, y, c, chips = _place()
        for w in range(len(ins)):
            for q, chip in enumerate(chips):
                blk = outs[w].at[2 * chip[0] + chip[1]]
                _remote(blk, blk, send_sems, recv_sems, 3 * w + q, (*chip, c)).wait_recv()
        for cp in self._sends(ins, outs, sems):
            cp.wait_send()


def _mm_in_gather(chip, hn, win4):
    s, d = hn.shape
    c4 = win4.shape[2]
    tm, tn = _tile(s, 1024), _tile(c4, 1024)
    per = c4 // tn
    nj, gm = N_CHIPS * per, s // tm
    half = d // 2

    def tile_of(j):
        jj = j - per
        o = jnp.where(j < per, 0, jnp.where(j < 3 * per, 1 + jj % 2, 3))
        t = jnp.where(j < per, j, jnp.where(j < 3 * per, jj // 2, j - 3 * per))
        return o, t

    def step_of(o, t):
        return per + 2 * t + o - 1 if o in (1, 2) else o * per + t

    def block_of(j, chip_ref):
        o, _ = tile_of(j)
        return chip_ref[0] ^ (((o & 1) << 1) | (o >> 1))

    i_late = max(gm - 2, 0)

    def body(chip_ref, hn_ref, w_any, proj_ref, w_ref, wbuf, wsem, send_sems, recv_sems):
        del w_any
        j, i = pl.program_id(0), pl.program_id(1)
        x, y, c, chips = _place()
        sibling = (x, y, 1 - c)

        def region(k, rows_half, t):
            return w_ref.at[k, pl.ds(rows_half * half, half), pl.ds(t * tn, tn)]

        def send_mine(q, t):
            blk = region(2 * x + y, c, t)
            return _remote(blk, blk, send_sems, recv_sems, q * per + t, (*chips[q], c))

        def forward(q, t):
            blk = region(2 * chips[q][0] + chips[q][1], c, t)
            return _remote(blk, blk, send_sems, recv_sems, (3 + q) * per + t, sibling)

        def fetch(jj, mine, slot):
            rows = pl.ds((c if mine else 1 - c) * half, half)
            cols = pl.ds(pl.multiple_of(tile_of(jj)[1] * tn, LANE), tn)
            return pltpu.make_async_copy(w_ref.at[block_of(jj, chip_ref), rows, cols], wbuf.at[slot, rows],
                                         wsem.at[2 * slot + (0 if mine else 1)])

        def foreign_tiles():
            return [(o, t) for o in range(1, N_CHIPS) for t in range(per)]

        relayed = [(0, t) for t in range(1, per, 2)] + [(1, t) for t in range(0, per, 2)]

        def relay(q, t):
            blk = region(2 * chips[q][0] + chips[q][1], c, t)
            return _remote(blk, blk, send_sems, recv_sems, 2 * per + t, (*chips[1 - q], c))

        def landed(q, t):
            blk = region(2 * chips[q][0] + chips[q][1], c, t)
            _remote(blk, blk, send_sems, recv_sems, q * per + t, sibling).wait_recv()
            forward(q, t).start()
            if (q, t) in relayed:
                relay(q, t).start()

        @pl.when((j == 0) & (i == 0))
        def _():
            for t in range(per):
                send_mine(0, t).start()
                send_mine(1, t).start()
            fetch(0, True, 0).start()
            fetch(0, False, 0).start()

        nxt = j + 1

        @pl.when((i == 0) & (nxt < nj))
        def _():
            for o, t in foreign_tiles():
                @pl.when(nxt == step_of(o, t))
                def _():
                    if o == 1:
                        landed(0, t)
                        landed(1, t)
                    elif o == 3:
                        landed(2, t)

            fetch(nxt, True, nxt % 2).start()

            @pl.when(nxt < per)
            def _():
                fetch(nxt, False, nxt % 2).start()

        @pl.when((i == i_late) & (nxt < nj))
        def _():
            for o, t in foreign_tiles():
                @pl.when(nxt == step_of(o, t))
                def _():
                    blk = region(2 * chips[o - 1][0] + chips[o - 1][1], 1 - c, t)
                    _remote(blk, blk, send_sems, recv_sems, (3 + o - 1) * per + t, sibling).wait_recv()
                    fetch(nxt, False, nxt % 2).start()

        @pl.when(i == 0)
        def _():
            fetch(j, True, j % 2).wait()
            fetch(j, False, j % 2).wait()

        proj_ref[...] = _dot(hn_ref[...], wbuf[j % 2])

        @pl.when((j == nj - 1) & (i == gm - 1))
        def _():
            for t in range(per):
                send_mine(0, t).wait_send()
                send_mine(1, t).wait_send()
                for q in range(3):
                    forward(q, t).wait_send()
            for q, t in relayed:
                relay(q, t).wait_send()

    return pl.pallas_call(
        body, name="mm_in_gather",
        grid_spec=pltpu.PrefetchScalarGridSpec(
            num_scalar_prefetch=1, grid=(nj, gm),
            in_specs=[pl.BlockSpec((tm, d), lambda j, i, chip_ref: (i, 0)), ANY],
            out_specs=[pl.BlockSpec((tm, tn), lambda j, i, chip_ref: (i, block_of(j, chip_ref) * per + tile_of(j)[1])), ANY],
            scratch_shapes=[pltpu.VMEM((2, d, tn), BF16), pltpu.SemaphoreType.DMA((4,)),
                            pltpu.SemaphoreType.DMA((6 * per,)), pltpu.SemaphoreType.DMA((6 * per,))]),
        out_shape=[jax.ShapeDtypeStruct((s, N_CHIPS * c4), F32), jax.ShapeDtypeStruct(win4.shape, win4.dtype)],
        input_output_aliases={2: 1},
        compiler_params=_params("arbitrary", "arbitrary"),
    )(chip, hn, win4)


class _SwapHalves:
    in_place = False

    def __init__(self, grads, whole=False):
        self.ins = list(grads)
        self.whole = whole
        self.out_shape = [jax.ShapeDtypeStruct((N_CHIPS, g.shape[1] // (1 if whole else 2), g.shape[2]), g.dtype)
                          for g in grads]
        self.scratch = [pltpu.SemaphoreType.DMA((len(grads),)), pltpu.SemaphoreType.DMA((len(grads),))]

    def _copies(self, ins, outs, sems):
        x, y, c, _ = _place()
        cps = []
        for w in range(len(ins)):
            half = ins[w].shape[1] // 2
            src = ins[w] if self.whole else ins[w].at[:, pl.ds((1 - c) * half, half)]
            cps.append(_remote(src, outs[w], sems[0], sems[1], w, (x, y, 1 - c)))
        return cps

    def start(self, ins, outs, sems):
        for cp in self._copies(ins, outs, sems):
            cp.start()

    def finish(self, ins, outs, sems):
        for cp in self._copies(ins, outs, sems):
            cp.wait()


def _chip_sum(name, chip, core, grad, got):
    _, half, c = got.shape
    th = _tile(half, 128)
    n = half // th
    skip = n if grad.shape[1] != half else 0

    def body(chip_ref, core_ref, g_ref, o_ref, s_ref):
        del chip_ref, core_ref
        s_ref[...] = (g_ref[...] + o_ref[...].astype(F32)).astype(s_ref.dtype)

    def specs(first, out_block):
        def block(k, i, chip_ref):
            return (chip_ref[0] + first + k) % N_CHIPS

        return dict(
            in_specs=[pl.BlockSpec((None, th, c), lambda k, i, chip_ref, core_ref: (block(k, i, chip_ref), core_ref[0] * skip + i, 0)),
                      pl.BlockSpec((None, th, c), lambda k, i, chip_ref, core_ref: (block(k, i, chip_ref), i, 0))],
            out_specs=out_block(block))

    own = pl.pallas_call(
        body, name=name + "_own",
        grid_spec=pltpu.PrefetchScalarGridSpec(
            num_scalar_prefetch=2, grid=(1, n),
            **specs(0, lambda block: pl.BlockSpec((th, c), lambda k, i, chip_ref, core_ref: (i, 0)))),
        out_shape=jax.ShapeDtypeStruct((half, c), F32), compiler_params=_params("parallel", "parallel"),
    )(chip, core, grad, got)
    others = pl.pallas_call(
        functools.partial(body), name=name + "_others",
        grid_spec=pltpu.PrefetchScalarGridSpec(
            num_scalar_prefetch=2, grid=(N_CHIPS - 1, n),
            **specs(1, lambda block: pl.BlockSpec(
                (None, th, c), lambda k, i, chip_ref, core_ref: (block(k, i, chip_ref), i, 0)))),
        out_shape=jax.ShapeDtypeStruct((N_CHIPS, half, c), BF16), compiler_params=_params("parallel", "parallel"),
    )(chip, core, grad, got)
    return own, others


def _riding(comm, grid, body, n_pre, n_in, n_out):
    if comm is None:
        return body
    c_in, c_out, c_scr = len(comm.ins), len(comm.out_shape), len(comm.scratch)

    def wrapped(*refs):
        pre, rest = refs[:n_pre], refs[n_pre:]
        ins, rest = rest[:n_in], rest[n_in:]
        cin, rest = rest[:c_in], rest[c_in:]
        outs, rest = rest[:n_out], rest[n_out:]
        cout, rest = rest[:c_out], rest[c_out:]
        own, sems = rest[:len(rest) - c_scr], rest[len(rest) - c_scr:]
        ids = [pl.program_id(ax) for ax in range(len(grid))]
        first, last = ids[0] == 0, ids[0] == grid[0] - 1
        for ax in range(1, len(grid)):
            first, last = first & (ids[ax] == 0), last & (ids[ax] == grid[ax] - 1)

        @pl.when(first)
        def _():
            comm.start(cin, cout, sems)

        body(*pre, *ins, *outs, *own)

        @pl.when(last)
        def _():
            comm.finish(cin, cout, sems)

    return wrapped


def _total(name, chip, core, sum32, got16, comm=None):
    half, c = sum32.shape
    th = _tile(half, 128)
    n = half // th

    def body(chip_ref, core_ref, own_ref, a_ref, b_ref, c_ref, o_ref):
        del chip_ref, core_ref
        o_ref[...] = ((own_ref[...] + a_ref[...].astype(F32)) + b_ref[...].astype(F32)) + c_ref[...].astype(F32)

    def other(step):
        return pl.BlockSpec((None, th, c), lambda i, chip_ref, core_ref: ((chip_ref[0] + step) % N_CHIPS, i, 0))

    c_ins = list(comm.ins) if comm else []
    res = pl.pallas_call(
        _riding(comm, (n,), body, 2, 4, 1), name=name,
        grid_spec=pltpu.PrefetchScalarGridSpec(
            num_scalar_prefetch=2, grid=(n,),
            in_specs=[pl.BlockSpec((th, c), lambda i, chip_ref, core_ref: (i, 0)), other(1), other(2), other(3)]
            + [ANY] * len(c_ins),
            out_specs=[pl.BlockSpec((th, c), lambda i, chip_ref, core_ref: (core_ref[0] * n + i, 0))]
            + [ANY] * len(c_ins),
            scratch_shapes=list(comm.scratch) if comm else []),
        out_shape=[jax.ShapeDtypeStruct((2 * half, c), F32)] + (list(comm.out_shape) if comm else []),
        input_output_aliases={6 + i: 1 + i for i in range(len(c_ins))} if comm else {},
        compiler_params=_params("arbitrary" if comm else "parallel"),
    )(chip, core, sum32, got16, got16, got16, *c_ins)
    return (res[0], res[1:]) if comm else res[0]


class _JoinHalves:
    in_place = True

    def __init__(self, blocks):
        self.ins = list(blocks)
        self.out_shape = [jax.ShapeDtypeStruct(t.shape, t.dtype) for t in blocks]
        self.scratch = [pltpu.SemaphoreType.DMA((len(blocks),)), pltpu.SemaphoreType.DMA((len(blocks),))]

    @staticmethod
    def _copy(ref, w, mine, sems):
        x, y, c, _ = _place()
        half = ref.shape[0] // 2
        rows = ref.at[pl.ds((c if mine else 1 - c) * half, half)]
        return _remote(rows, rows, sems[0], sems[1], w, (x, y, 1 - c))

    def start(self, ins, outs, sems):
        for w, ref in enumerate(outs):
            self._copy(ref, w, True, sems).start()

    def finish(self, ins, outs, sems):
        for w, ref in enumerate(outs):
            self._copy(ref, w, True, sems).wait_send()
            self._copy(ref, w, False, sems).wait_recv()


class _Both:
    in_place = True

    def __init__(self, a, b):
        assert a.in_place and b.in_place
        self.parts = (a, b)
        self.ins = list(a.ins) + list(b.ins)
        self.out_shape = list(a.out_shape) + list(b.out_shape)
        self.scratch = list(a.scratch) + list(b.scratch)

    def _split(self, ins, outs, sems):
        a = self.parts[0]
        n, k = len(a.ins), len(a.scratch)
        return (a, ins[:n], outs[:n], sems[:k]), (self.parts[1], ins[n:], outs[n:], sems[k:])

    def start(self, ins, outs, sems):
        for part, i, o, s in self._split(ins, outs, sems):
            part.start(i, o, s)

    def finish(self, ins, outs, sems):
        for part, i, o, s in self._split(ins, outs, sems):
            part.finish(i, o, s)


def _alone(name, comm):
    n = len(comm.ins)

    def body(*refs):
        comm.start(refs[:n], refs[n:2 * n], refs[2 * n:])
        comm.finish(refs[:n], refs[n:2 * n], refs[2 * n:])

    return pl.pallas_call(
        body, name=name, in_specs=[ANY] * n, out_specs=[ANY] * n, out_shape=comm.out_shape,
        scratch_shapes=comm.scratch, input_output_aliases={i: i for i in range(n)},
    )(*comm.ins)


class _GatherSmall:
    in_place = True

    def __init__(self, full):
        self.ins = [full]
        self.out_shape = [jax.ShapeDtypeStruct(full.shape, full.dtype)]
        self.scratch = [pltpu.SemaphoreType.DMA((7,)), pltpu.SemaphoreType.DMA((7,))]

    @staticmethod
    def _copy(ref, sems, k, block, to):
        m_per = ref.shape[0] // N_DEV
        px, py, pc = block
        rows = ref.at[pl.ds((4 * px + 2 * py + pc) * m_per, m_per), :]
        return _remote(rows, rows, sems[0], sems[1], k, to)

    def _first(self, ref, sems):
        x, y, c, chips = _place()
        me = (x, y, c)
        return [self._copy(ref, sems, 0, me, (x, y, 1 - c))] + \
               [self._copy(ref, sems, 1 + j, me, (*chip, c)) for j, chip in enumerate(chips)]

    def start(self, ins, outs, sems):
        for cp in self._first(outs[0], sems):
            cp.start()

    def finish(self, ins, outs, sems):
        ref = outs[0]
        x, y, c, chips = _place()
        me, sibling = (x, y, c), (x, y, 1 - c)
        passed = [self._copy(ref, sems, 4 + j, (*chip, c), sibling) for j, chip in enumerate(chips)]
        for j, chip in enumerate(chips):
            self._copy(ref, sems, 1 + j, (*chip, c), me).wait_recv()
            passed[j].start()
        self._copy(ref, sems, 0, sibling, me).wait_recv()
        for j, chip in enumerate(chips):
            self._copy(ref, sems, 4 + j, (*chip, 1 - c), me).wait_recv()
        for cp in self._first(ref, sems) + passed:
            cp.wait_send()


def _adamw_math(w, g, m, v):
    m = ADAM_B1 * m + (1.0 - ADAM_B1) * g
    v = ADAM_B2 * v + (1.0 - ADAM_B2) * (g * g)
    m_hat = m / (1.0 - ADAM_B1 ** ADAM_STEP)
    v_hat = v / (1.0 - ADAM_B2 ** ADAM_STEP)
    delta = -ADAM_LR * (m_hat / (jnp.sqrt(v_hat) + ADAM_EPS) + ADAM_WD * w)
    return delta, m, v


def _adamw(name, w, g, m, v):
    r, c = w.shape
    tm = _tile(r, 128)

    def body(w_ref, g_ref, m_ref, v_ref, g_out, d_out, m_out, v_out):
        g = g_ref[...]
        g_out[...] = g
        d_out[...], m_out[...], v_out[...] = _adamw_math(w_ref[...], g, m_ref[...], v_ref[...])

    spec = pl.BlockSpec((tm, c), lambda i: (i, 0))
    return pl.pallas_call(
        body, name=name, grid=(r // tm,), in_specs=[spec] * 4, out_specs=[spec] * 4,
        out_shape=[jax.ShapeDtypeStruct((r, c), F32)] * 4, compiler_params=_params("parallel"),
    )(w, g, m, v)


def _adamw_small(gathered, w, m, v):
    rows = w.shape[0]

    def body(all_ref, w_ref, m_ref, v_ref, g_out, d_out, m_out, v_out):
        g = all_ref[0:rows, :]
        for dev in range(1, N_DEV):
            g = g + all_ref[dev * rows:(dev + 1) * rows, :]
        g_out[...] = g
        d_out[...], m_out[...], v_out[...] = _adamw_math(w_ref[...], g, m_ref[...], v_ref[...])

    return pl.pallas_call(
        body, name="adamw_small", out_shape=[jax.ShapeDtypeStruct(w.shape, F32)] * 4, compiler_params=_params(),
    )(gathered, w, m, v)


SMALL = ("g_pre", "w_s", "b_s", "ln_v_g", "ln_v_b", "g_q", "g_k", "rel_bias", "g_out_a", "g_out_b", "g_ple")
LARGE = ("w_in", "w_out", "w_ple_gate", "w_ple_up")
WEIGHTS = ("g_pre", "w_in", "w_s", "b_s", "ln_v_g", "ln_v_b", "g_q", "g_k", "rel_bias", "g_out_a", "g_out_b", "w_out",
           "g_ple", "w_ple_gate", "w_ple_up")


def _pack(parts):
    flat = jnp.concatenate([parts[n].reshape(-1).astype(F32) for n in SMALL])
    rows = -(-flat.shape[0] // (8 * LANE)) * 8
    return jnp.pad(flat, (0, rows * LANE - flat.shape[0])).reshape(rows, LANE)


def _unpack(pack, like):
    flat = pack.reshape(-1)
    out, at = {}, 0
    for n in SMALL:
        size = math.prod(like[n].shape)
        out[n] = flat[at:at + size].reshape(like[n].shape)
        at += size
    return out


def kernel(x, p, g_pre, w_in, w_s, b_s, ln_v_g, ln_v_b, g_q, g_k, rel_bias, g_out_a, g_out_b, w_out, g_ple, w_ple_gate, w_ple_up, loss_target, m_g_pre, m_w_in, m_w_s, m_b_s, m_ln_v_g, m_ln_v_b, m_g_q, m_g_k, m_rel_bias, m_g_out_a, m_g_out_b, m_w_out, m_g_ple, m_w_ple_gate, m_w_ple_up, v_g_pre, v_w_in, v_w_s, v_b_s, v_ln_v_g, v_ln_v_b, v_g_q, v_g_k, v_rel_bias, v_g_out_a, v_g_out_b, v_w_out, v_g_ple, v_w_ple_gate, v_w_ple_up):
    given = dict(locals())
    weights = {n: given[n] for n in WEIGHTS}
    mom_m = {n: given["m_" + n] for n in WEIGHTS}
    mom_v = {n: given["v_" + n] for n in WEIGHTS}
    xs, ps, tgt = x[0], p[0, 0], loss_target[0]
    d = xs.shape[1]

    core = lax.axis_index("c").astype(jnp.int32).reshape(1)
    chip = (2 * lax.axis_index("x") + lax.axis_index("y")).astype(jnp.int32).reshape(1)

    win4, wout4, wgate4, wup4 = [_cast_bf16("cast_" + n, chip, weights[n][0]) for n in LARGE]

    sp = {
        "g_pre": g_pre, "w_s": w_s[0], "b_s": b_s[0], "ln_v_g": ln_v_g, "ln_v_b": ln_v_b, "g_q": g_q, "g_k": g_k,
        "rel_bias": rel_bias, "g_out_a": g_out_a, "g_out_b": g_out_b, "g_ple": g_ple,
    }
    loss_local, grad_x, (sums32, got16), small = _local_step(xs, ps, tgt, sp, win4, wout4, wgate4, wup4,
                                                             dist={"chip": chip, "core": core})
    loss = lax.psum(loss_local, MESH_AXES)

    own = dict(zip(LARGE, sums32))
    got = dict(zip(LARGE, got16))
    halves = [_total("total_" + n, chip, core, own[n], got[n]) for n in LARGE[1:]]
    my_pack = _pack(small)
    device = 2 * chip[0] + core[0]
    everyone = lax.dynamic_update_slice(jnp.zeros((N_DEV * my_pack.shape[0], LANE), F32), my_pack,
                                        (device * my_pack.shape[0], 0))
    half_in, (*joined, gathered) = _total("total_w_in", chip, core, own["w_in"], got["w_in"],
                                          comm=_Both(_JoinHalves(halves), _GatherSmall(everyone)))
    grads = dict(zip(LARGE[1:], joined))
    grads["w_in"], = _alone("join_w_in", _JoinHalves([half_in]))

    out_g, out_d, out_m, out_v = {}, {}, {}, {}
    for n in LARGE:
        results = _adamw("adamw_" + n, weights[n][0], grads[n], mom_m[n][0], mom_v[n][0])
        out_g[n], out_d[n], out_m[n], out_v[n] = [r[None] for r in results]

    pg, pd, pm, pv = _adamw_small(gathered, _pack(weights), _pack(mom_m), _pack(mom_v))
    for packed, out in ((pg, out_g), (pd, out_d), (pm, out_m), (pv, out_v)):
        out.update(_unpack(packed, weights))

    return (loss, grad_x[None], *[out_g[n] for n in WEIGHTS], *[out_d[n] for n in WEIGHTS],
            *[out_m[n] for n in WEIGHTS], *[out_v[n] for n in WEIGHTS])
```

```python
import functools
import math

import numpy as np

import jax
import jax.numpy as jnp
from jax import lax
from jax.experimental import pallas as pl
from jax.experimental.pallas import tpu as pltpu

F32 = jnp.float32
BF16 = jnp.bfloat16

HEAD_DIM = 128
CHUNK = 128
BLK = 128
DILATED = ((128, 1), (512, 4), (2048, 16))
NUM_BUCKETS = 32
MAX_DISTANCE = 2048
ATTN_GROUP = 8
ATTN_GROUP_FWD = 16
EPS = 1e-6
NEG_INF = -1e30
N_CHIPS = 4
N_DEV = 8
MESH_AXES = ("x", "y", "c")

ADAM_LR = 0.001
ADAM_B1 = 0.9
ADAM_B2 = 0.999
ADAM_EPS = 1e-08
ADAM_WD = 0.01
ADAM_STEP = 10

V7X_VMEM_LIMIT = 58 * 1024 * 1024
LANE = 128
MESH = pl.DeviceIdType.MESH
ANY = pl.BlockSpec(memory_space=pl.ANY)


def _params(*sem):
    return pltpu.CompilerParams(dimension_semantics=sem or None, vmem_limit_bytes=V7X_VMEM_LIMIT)


def _tile(n, target):
    if n <= target:
        return n
    t = (target // LANE) * LANE
    while t > LANE and n % t:
        t -= LANE
    assert n % t == 0, (n, target)
    return t


def _gelu(x):
    return 0.5 * x * (1.0 + lax.erf(x * (1.0 / math.sqrt(2.0))))


def _gelu_and_grad(x):
    cdf = 0.5 * (1.0 + lax.erf(x * (1.0 / math.sqrt(2.0))))
    return x * cdf, cdf + x * jnp.exp(-0.5 * x * x) * (1.0 / math.sqrt(2.0 * math.pi))


def _silu_and_grad(x):
    s = jax.nn.sigmoid(x)
    return x * s, s * (1.0 + x * (1.0 - s))


def _dot(a, b, ta=False, tb=False):
    return lax.dot_general(a, b, (((0 if ta else 1,), (1 if tb else 0,)), ((), ())), preferred_element_type=F32)


def _colsum8(v):
    return jnp.sum(v.reshape(v.shape[0] // 8, 8, v.shape[1]), axis=0)


def _matmul(name, a, b, a_spec, b_spec, grid, acc_shape, out_shape, out_specs, ta=False, tb=False,
            extras=(), extra_specs=(), epilogue=None, comm=None, prefetch=()):
    nk = grid[2]
    n_pre = len(prefetch)
    n_extra = len(extras)
    single = not isinstance(out_shape, (tuple, list))
    outs_shape = (out_shape,) if single else tuple(out_shape)
    outs_specs = (out_specs,) if single else tuple(out_specs)
    n_out = len(outs_shape)
    c_ins = list(comm.ins) if comm else []
    c_outs = list(comm.out_shape) if comm else []
    c_scratch = list(comm.scratch) if comm else []
    n_cin, n_cout = len(c_ins), len(c_outs)

    def finish(acc, extra_refs, out_refs):
        if epilogue is None:
            out_refs[0][...] = acc.astype(out_refs[0].dtype)
        else:
            epilogue(acc, extra_refs, out_refs)

    def body(*refs):
        a_ref, b_ref, *rest = refs[n_pre:]
        extra_refs = rest[:n_extra]
        cin_refs = rest[n_extra:n_extra + n_cin]
        out_refs = rest[n_extra + n_cin:n_extra + n_cin + n_out]
        cout_refs = rest[n_extra + n_cin + n_out:n_extra + n_cin + n_out + n_cout]
        scratch_refs = rest[n_extra + n_cin + n_out + n_cout:]
        ids = [pl.program_id(ax) for ax in range(3)]
        if comm:
            sems = scratch_refs[len(scratch_refs) - len(c_scratch):]

            @pl.when((ids[0] == 0) & (ids[1] == 0) & (ids[2] == 0))
            def _():
                comm.start(cin_refs, cout_refs, sems)

        if nk == 1:
            finish(_dot(a_ref[...], b_ref[...], ta, tb), extra_refs, out_refs)
        else:
            acc_ref = scratch_refs[0]

            @pl.when(ids[2] == 0)
            def _():
                acc_ref[...] = jnp.zeros_like(acc_ref)

            acc_ref[...] += _dot(a_ref[...], b_ref[...], ta, tb)

            @pl.when(ids[2] == nk - 1)
            def _():
                finish(acc_ref[...], extra_refs, out_refs)

        if comm:
            @pl.when((ids[0] == grid[0] - 1) & (ids[1] == grid[1] - 1) & (ids[2] == nk - 1))
            def _():
                comm.finish(cin_refs, cout_refs, sems)

    scratch = ([] if nk == 1 else [pltpu.VMEM(acc_shape, F32)]) + c_scratch
    aliases = {n_pre + 2 + n_extra + i: n_out + i for i in range(n_cin)} if (comm and comm.in_place) else {}
    res = pl.pallas_call(
        body, name=name,
        grid_spec=pltpu.PrefetchScalarGridSpec(
            num_scalar_prefetch=n_pre, grid=grid,
            in_specs=[a_spec, b_spec, *extra_specs] + [ANY] * n_cin,
            out_specs=list(outs_specs) + [ANY] * n_cout, scratch_shapes=scratch),
        out_shape=list(outs_shape) + c_outs,
        input_output_aliases=aliases,
        compiler_params=_params(*(("arbitrary",) * 3 if comm else ("parallel", "parallel", "arbitrary"))),
    )(*prefetch, a, b, *extras, *c_ins)
    if comm:
        main = res[:n_out]
        return (main[0] if single else main), res[n_out:]
    return res[0] if single else res


def _shard_spec(rows, cols, per_shard, row_axis, col_axis):
    def index(i, j, k):
        g = (i, j, k)
        return (g[col_axis] // per_shard, g[row_axis], g[col_axis] % per_shard)
    return pl.BlockSpec((None, rows, cols), index)


ROW_CHUNK = 512


def _column_chunks(width):
    chunk = _tile(width, ROW_CHUNK)
    return (lambda c: slice(c * chunk, (c + 1) * chunk)), width // chunk


def _rms_fwd(name, x, g):
    s, d = x.shape
    tm = _tile(s, 256)

    cols, n_chunks = _column_chunks(d)

    def body(x_ref, g_ref, y_ref, r_ref):
        total = jnp.zeros((tm, 1), F32)
        for c in range(n_chunks):
            xc = x_ref[:, cols(c)]
            total = total + jnp.sum(xc * xc, axis=-1, keepdims=True)
        r = lax.rsqrt(total * (1.0 / d) + EPS)
        r_ref[...] = r
        for c in range(n_chunks):
            y_ref[:, cols(c)] = (x_ref[:, cols(c)] * r * g_ref[:, cols(c)]).astype(BF16)

    return pl.pallas_call(
        body, name=name, grid=(s // tm,),
        in_specs=[pl.BlockSpec((tm, d), lambda i: (i, 0)), pl.BlockSpec((1, d), lambda i: (0, 0))],
        out_specs=(pl.BlockSpec((tm, d), lambda i: (i, 0)), pl.BlockSpec((tm, 1), lambda i: (i, 0))),
        out_shape=(jax.ShapeDtypeStruct((s, d), BF16), jax.ShapeDtypeStruct((s, 1), F32)),
        compiler_params=_params("parallel"),
    )(x, g)


def _rms_bwd(name, dy, x, r, g, skip, with_bf16):
    s, d = x.shape
    tm = _tile(s, 256)
    n = s // tm
    cols, n_chunks = _column_chunks(d)

    def body(dy_ref, x_ref, r_ref, g_ref, skip_ref, *outs):
        dx_ref = outs[0]
        dg_ref = outs[-2]
        acc_ref = outs[-1]
        i = pl.program_id(0)
        rv = r_ref[...]

        @pl.when(i == 0)
        def _():
            acc_ref[...] = jnp.zeros_like(acc_ref)

        total = jnp.zeros((tm, 1), F32)
        for c in range(n_chunks):
            dyc, xc = dy_ref[:, cols(c)].astype(F32), x_ref[:, cols(c)]
            acc_ref[:, cols(c)] += _colsum8(dyc * xc * rv)
            total = total + jnp.sum(dyc * g_ref[:, cols(c)] * xc, axis=-1, keepdims=True)
        coef = (rv * rv) * (total * (1.0 / d))
        for c in range(n_chunks):
            dyc, xc = dy_ref[:, cols(c)].astype(F32), x_ref[:, cols(c)]
            dx = skip_ref[:, cols(c)].astype(F32) + rv * (dyc * g_ref[:, cols(c)] - xc * coef)
            dx_ref[:, cols(c)] = dx
            if with_bf16:
                outs[1][:, cols(c)] = dx.astype(BF16)

        @pl.when(i == n - 1)
        def _():
            dg_ref[...] = jnp.sum(acc_ref[...], axis=0, keepdims=True)

    row = pl.BlockSpec((tm, d), lambda i: (i, 0))
    vec = pl.BlockSpec((1, d), lambda i: (0, 0))
    out_specs = [row] + ([row] if with_bf16 else []) + [vec]
    out_shape = [jax.ShapeDtypeStruct((s, d), F32)] + ([jax.ShapeDtypeStruct((s, d), BF16)] if with_bf16 else []) \
        + [jax.ShapeDtypeStruct((1, d), F32)]
    return pl.pallas_call(
        body, name=name, grid=(n,),
        in_specs=[row, row, pl.BlockSpec((tm, 1), lambda i: (i, 0)), vec, row],
        out_specs=out_specs, out_shape=out_shape, scratch_shapes=[pltpu.VMEM((8, d), F32)],
        compiler_params=_params("arbitrary"),
    )(dy, x, r, g, skip)


def _causal(w):
    t = lax.broadcasted_iota(jnp.int32, w.shape, 0)
    s_ = lax.broadcasted_iota(jnp.int32, w.shape, 1)
    return jnp.where(t >= s_, w, 0.0)


def _gmlp_fwd(proj, w_s, b_st, ln_g, ln_b, g_out, d_model):
    s = proj.shape[0]
    wa = d_model // 2
    groups = wa // HEAD_DIM
    tm = _tile(s, 256)
    n_chunks = tm // CHUNK

    def body(au_ref, av_ref, az_ref, ws_ref, bst_ref, lng_ref, lnb_ref, go_ref,
             y_ref, z_ref, vn_ref, mu_ref, rs_ref, ra_ref):
        gv = _gelu(av_ref[...])
        mu = jnp.mean(gv, axis=-1, keepdims=True)
        xc = gv - mu
        rs = lax.rsqrt(jnp.mean(xc * xc, axis=-1, keepdims=True) + EPS)
        vn = (xc * rs * lng_ref[...] + lnb_ref[...]).astype(BF16)
        vn_ref[...] = vn
        mu_ref[...] = mu
        rs_ref[...] = rs
        for g in range(groups):
            wm = _causal(ws_ref[g]).astype(BF16)
            cols = slice(g * HEAD_DIM, (g + 1) * HEAD_DIM)
            for ch in range(n_chunks):
                rows = slice(ch * CHUNK, (ch + 1) * CHUNK)
                z_ref[rows, cols] = _dot(wm, vn_ref[rows, cols]) + bst_ref[:, g:g + 1]
        ya = _gelu(au_ref[...]) * z_ref[...]
        ra = lax.rsqrt(jnp.mean(ya * ya, axis=-1, keepdims=True) + EPS)
        ra_ref[...] = ra
        sz, _ = _silu_and_grad(az_ref[...])
        y_ref[...] = (ya * ra * go_ref[...] * sz).astype(BF16)

    def col(j):
        return pl.BlockSpec((tm, wa), lambda i: (i, j))
    vec = pl.BlockSpec((1, wa), lambda i: (0, 0))
    stat = pl.BlockSpec((tm, 1), lambda i: (i, 0))
    return pl.pallas_call(
        body, name="gmlp_fwd", grid=(s // tm,),
        in_specs=[col(0), col(1), col(2),
                  pl.BlockSpec((groups, CHUNK, CHUNK), lambda i: (0, 0, 0)),
                  pl.BlockSpec((CHUNK, groups), lambda i: (0, 0)), vec, vec, vec],
        out_specs=(col(0), col(0), col(0), stat, stat, stat),
        out_shape=(jax.ShapeDtypeStruct((s, d_model), BF16), jax.ShapeDtypeStruct((s, wa), F32),
                   jax.ShapeDtypeStruct((s, wa), BF16), jax.ShapeDtypeStruct((s, 1), F32),
                   jax.ShapeDtypeStruct((s, 1), F32), jax.ShapeDtypeStruct((s, 1), F32)),
        compiler_params=_params("parallel"),
    )(proj, proj, proj, w_s, b_st, ln_g, ln_b, g_out)


def _gmlp_bwd(dproj, dy, proj, z, vn, mu, rs, ra, w_s, ln_g, g_out, d_model):
    s = proj.shape[0]
    wa = d_model // 2
    groups = wa // HEAD_DIM
    tm = _tile(s, 256)
    n_chunks = tm // CHUNK
    n = s // tm

    def causal_stack(w):
        t = lax.broadcasted_iota(jnp.int32, w.shape, 1)
        s_ = lax.broadcasted_iota(jnp.int32, w.shape, 2)
        return jnp.where(t >= s_, w, 0.0)

    def body(dproj_in, dy_ref, au_ref, av_ref, az_ref, z_ref, vn_ref, mu_ref, rs_ref, ra_ref, ws_ref, lng_ref, go_ref,
             dp_ref, gws_ref, dzs_ref, glg_ref, glb_ref, ggo_ref,
             dz_s, dvn_s, acc_lg, acc_lb, acc_go):
        del dproj_in
        i = pl.program_id(0)

        @pl.when(i == 0)
        def _():
            gws_ref[...] = jnp.zeros_like(gws_ref)
            dzs_ref[...] = jnp.zeros_like(dzs_ref)
            acc_lg[...] = jnp.zeros_like(acc_lg)
            acc_lb[...] = jnp.zeros_like(acc_lb)
            acc_go[...] = jnp.zeros_like(acc_go)

        au, az, zv, rav = au_ref[...], az_ref[...], z_ref[...], ra_ref[...]
        u, du = _gelu_and_grad(au)
        ya = u * zv
        sz, dsz = _silu_and_grad(az)
        dyv = dy_ref[...].astype(F32)
        dp_ref[:, 2 * wa:3 * wa] = (dyv * (ya * rav * go_ref[...]) * dsz).astype(BF16)
        dn = dyv * sz
        acc_go[...] += _colsum8(dn * ya * rav)
        dyg = dn * go_ref[...]
        dya = rav * (dyg - ya * (rav * rav) * jnp.mean(dyg * ya, axis=-1, keepdims=True))
        dp_ref[:, 0:wa] = (dya * zv * du).astype(BF16)
        dz_s[...] = dya * u
        for ch in range(n_chunks):
            dzs_ref[...] += dz_s[ch * CHUNK:(ch + 1) * CHUNK, :]
        for g in range(groups):
            wm = _causal(ws_ref[g]).astype(BF16)
            cols = slice(g * HEAD_DIM, (g + 1) * HEAD_DIM)
            for ch in range(n_chunks):
                rows = slice(ch * CHUNK, (ch + 1) * CHUNK)
                dzb = dz_s[rows, cols].astype(BF16)
                gws_ref[g] += _dot(dzb, vn_ref[rows, cols], tb=True)
                dvn_s[rows, cols] = _dot(wm, dzb, ta=True)
        av = av_ref[...]
        gv, dgelu_v = _gelu_and_grad(av)
        xh = (gv - mu_ref[...]) * rs_ref[...]
        dvn = dvn_s[...]
        acc_lb[...] += _colsum8(dvn)
        acc_lg[...] += _colsum8(dvn * xh)
        dxh = dvn * lng_ref[...]
        dgv = rs_ref[...] * (dxh - jnp.mean(dxh, axis=-1, keepdims=True) - xh * jnp.mean(dxh * xh, axis=-1, keepdims=True))
        dp_ref[:, wa:2 * wa] = (dgv * dgelu_v).astype(BF16)

        @pl.when(i == n - 1)
        def _():
            gws_ref[...] = causal_stack(gws_ref[...])
            glg_ref[...] = jnp.sum(acc_lg[...], axis=0, keepdims=True)
            glb_ref[...] = jnp.sum(acc_lb[...], axis=0, keepdims=True)
            ggo_ref[...] = jnp.sum(acc_go[...], axis=0, keepdims=True)

    def col(j):
        return pl.BlockSpec((tm, wa), lambda i: (i, j))
    vec = pl.BlockSpec((1, wa), lambda i: (0, 0))
    stat = pl.BlockSpec((tm, 1), lambda i: (i, 0))
    ws_spec = pl.BlockSpec((groups, CHUNK, CHUNK), lambda i: (0, 0, 0))
    d_in = dproj.shape[1]
    return pl.pallas_call(
        body, name="gmlp_bwd", grid=(n,),
        in_specs=[ANY, col(0), col(0), col(1), col(2), col(0), col(0), stat, stat, stat, ws_spec, vec, vec],
        out_specs=(pl.BlockSpec((tm, 3 * wa), lambda i: (i, 0)), ws_spec,
                   pl.BlockSpec((CHUNK, wa), lambda i: (0, 0)), vec, vec, vec),
        out_shape=(jax.ShapeDtypeStruct((s, d_in), BF16),
                   jax.ShapeDtypeStruct((groups, CHUNK, CHUNK), F32), jax.ShapeDtypeStruct((CHUNK, wa), F32))
        + (jax.ShapeDtypeStruct((1, wa), F32),) * 3,
        scratch_shapes=[pltpu.VMEM((tm, wa), F32), pltpu.VMEM((tm, wa), F32)] + [pltpu.VMEM((8, wa), F32)] * 3,
        input_output_aliases={0: 0},
        compiler_params=_params("arbitrary"),
    )(dproj, dy, proj, proj, proj, z, vn, mu, rs, ra, w_s, ln_g, g_out)


def _gate_fwd(y_in, yb, proj, g_out, d_model):
    s = proj.shape[0]
    wa = d_model // 2
    tm = _tile(s, 256)

    cols, n_chunks = _column_chunks(wa)

    def body(y_any, yb_ref, bz_ref, go_ref, y_ref, rb_ref):
        del y_any
        total = jnp.zeros((tm, 1), F32)
        for c in range(n_chunks):
            ybc = yb_ref[:, cols(c)]
            total = total + jnp.sum(ybc * ybc, axis=-1, keepdims=True)
        rb = lax.rsqrt(total * (1.0 / wa) + EPS)
        rb_ref[...] = rb
        for c in range(n_chunks):
            sz, _ = _silu_and_grad(bz_ref[:, cols(c)])
            y_ref[:, cols(c)] = (yb_ref[:, cols(c)] * rb * go_ref[:, cols(c)] * sz).astype(BF16)

    return pl.pallas_call(
        body, name="gate_b_fwd", grid=(s // tm,),
        in_specs=[ANY, pl.BlockSpec((tm, wa), lambda i: (i, 0)), pl.BlockSpec((tm, wa), lambda i: (i, 6)),
                  pl.BlockSpec((1, wa), lambda i: (0, 0))],
        out_specs=(pl.BlockSpec((tm, wa), lambda i: (i, 1)), pl.BlockSpec((tm, 1), lambda i: (i, 0))),
        out_shape=(jax.ShapeDtypeStruct((s, d_model), BF16), jax.ShapeDtypeStruct((s, 1), F32)),
        input_output_aliases={0: 0},
        compiler_params=_params("parallel"),
    )(y_in, yb, proj, g_out)


def _gate_bwd(dy, yb, lse, proj, rb, g_out, d_model):
    s, d_in = proj.shape
    wa = d_model // 2
    heads = wa // HEAD_DIM
    tm = _tile(s, 256)
    n = s // tm
    cols, n_chunks = _column_chunks(wa)

    def body(dy_ref, yb_ref, lse_ref, bz_ref, rb_ref, go_ref, dp_ref, do_ref, st_ref, ggo_ref, acc):
        i = pl.program_id(0)

        @pl.when(i == 0)
        def _():
            acc[...] = jnp.zeros_like(acc)

        rbv = rb_ref[...]
        total = jnp.zeros((tm, 1), F32)
        for c in range(n_chunks):
            dyc, ybc = dy_ref[:, cols(c)].astype(F32), yb_ref[:, cols(c)]
            sz, dsz = _silu_and_grad(bz_ref[:, cols(c)])
            dp_ref[:, cols(c)] = (dyc * (ybc * rbv * go_ref[:, cols(c)]) * dsz).astype(BF16)
            dn = dyc * sz
            acc[:, cols(c)] += _colsum8(dn * ybc * rbv)
            total = total + jnp.sum(dn * go_ref[:, cols(c)] * ybc, axis=-1, keepdims=True)
        coef = (rbv * rbv) * (total * (1.0 / wa))
        first = lax.broadcasted_iota(jnp.int32, (tm, 2), 1) == 0
        per_chunk = (wa // n_chunks) // HEAD_DIM
        for c in range(n_chunks):
            dyc, ybc = dy_ref[:, cols(c)].astype(F32), yb_ref[:, cols(c)]
            sz, _ = _silu_and_grad(bz_ref[:, cols(c)])
            do = rbv * (dyc * sz * go_ref[:, cols(c)] - ybc * coef)
            do_ref[:, cols(c)] = do
            prod = do * ybc
            for k in range(per_chunk):
                h = c * per_chunk + k
                delta = jnp.sum(prod[:, k * HEAD_DIM:(k + 1) * HEAD_DIM], axis=-1, keepdims=True)
                st_ref[h] = jnp.where(first, lse_ref[h], delta)

        @pl.when(i == n - 1)
        def _():
            ggo_ref[...] = jnp.sum(acc[...], axis=0, keepdims=True)

    vec = pl.BlockSpec((1, wa), lambda i: (0, 0))
    return pl.pallas_call(
        body, name="gate_b_bwd", grid=(n,),
        in_specs=[pl.BlockSpec((tm, wa), lambda i: (i, 1)), pl.BlockSpec((tm, wa), lambda i: (i, 0)),
                  pl.BlockSpec((heads, tm, 1), lambda i: (0, i, 0)),
                  pl.BlockSpec((tm, wa), lambda i: (i, 6)), pl.BlockSpec((tm, 1), lambda i: (i, 0)), vec],
        out_specs=(pl.BlockSpec((tm, wa), lambda i: (i, 6)), pl.BlockSpec((tm, wa), lambda i: (i, 0)),
                   pl.BlockSpec((heads, tm, 2), lambda i: (0, i, 0)), vec),
        out_shape=(jax.ShapeDtypeStruct((s, d_in), BF16), jax.ShapeDtypeStruct((s, wa), F32),
                   jax.ShapeDtypeStruct((heads, s, 2), F32), jax.ShapeDtypeStruct((1, wa), F32)),
        scratch_shapes=[pltpu.VMEM((8, wa), F32)],
        compiler_params=_params("arbitrary"),
    )(dy, yb, lse, proj, rb, g_out)


def _bucket_tables():
    qi = BLK + np.arange(BLK)
    kj = np.arange(2 * BLK)
    delta = qi[:, None] - kj[None, :]
    max_exact = NUM_BUCKETS // 2
    tabs = []
    for window, dil in DILATED:
        band = (delta >= 0) & (delta <= window // dil)
        dist = np.clip(delta, 0, None) * dil
        d = np.maximum(dist, 1).astype(np.float32)
        large = max_exact + (np.log(d / np.float32(max_exact)) / np.float32(math.log(MAX_DISTANCE / max_exact))
                             * np.float32(NUM_BUCKETS - max_exact)).astype(np.int32)
        large = np.minimum(large, NUM_BUCKETS - 1)
        tabs.append(np.where(band, np.where(dist < max_exact, dist, large), -1).astype(np.int32))
    return np.stack(tabs)


def _bias_tiles(tab_ref, rb_ref, h, bias_s):
    col = lax.broadcasted_iota(jnp.int32, (BLK, 2 * BLK), 1)
    for i in range(len(DILATED)):
        t = tab_ref[i]
        bias = jnp.zeros(t.shape, F32)
        for b in range(NUM_BUCKETS):
            bias = jnp.where(t == b, rb_ref[b, h], bias)
        bias = jnp.where(t >= 0, bias, NEG_INF)
        bias_s[2 * i] = jnp.where(col >= BLK, bias, NEG_INF)
        bias_s[2 * i + 1] = bias


def _block_rows(b, n_blocks, dil):
    nb = n_blocks // dil
    r = b // nb
    n = b % nb
    start = r + dil * BLK * n
    if dil == 1:
        return pl.ds(pl.multiple_of(start, BLK), BLK), n
    return pl.ds(start, BLK, stride=dil), n


def _sub_block(b, rows=BLK, pad=0):
    return pl.ds(pl.multiple_of(b * BLK + pad, BLK), rows)


def _regroup(b, n_blocks, small, big):
    ratio = big // small
    piece = BLK // ratio
    nb_small, nb_big = n_blocks // small, n_blocks // big
    r, n = b // nb_small, b % nb_small
    for q in range(ratio):
        block = (r + small * q) * nb_big + n // ratio
        yield q, ratio, piece, pl.multiple_of(block * BLK + piece * (n % ratio), piece)


def _rms_rows(x, gain):
    return x * lax.rsqrt(jnp.mean(x * x, axis=-1, keepdims=True) + EPS) * gain


def _attn_fwd(proj, g_q, g_k, rel_bias, d_model, comm=None):
    s = proj.shape[0]
    heads = d_model // 2 // HEAD_DIM
    n_blocks = s // BLK
    scale = HEAD_DIM ** -0.5
    tables = jnp.asarray(_bucket_tables())
    c_ins = list(comm.ins) if comm else []
    c_outs = list(comm.out_shape) if comm else []
    c_scratch = list(comm.scratch) if comm else []

    def body(q_ref, k_ref, v_ref, gq_ref, gk_ref, rb_ref, tab_ref, *rest):
        cin_refs, rest = rest[:len(c_ins)], rest[len(c_ins):]
        yb_ref, lse_ref = rest[:2]
        cout_refs, rest = rest[2:2 + len(c_outs)], rest[2 + len(c_outs):]
        qs, ks, vs, qs_w, ks_w, vs_w, tmp, m_s, l_s, acc_s, bias_s, sc_s, p_s = rest[:13]
        sems = rest[13:]
        h = pl.program_id(0)
        if comm:
            @pl.when(h == 0)
            def _():
                comm.start(cin_refs, cout_refs, sems)

        _bias_tiles(tab_ref, rb_ref, h, bias_s)
        gq = gq_ref[...] * scale
        gk = gk_ref[...]
        for buf in (ks, vs, ks_w, vs_w):
            buf[0:BLK, :] = jnp.zeros((BLK, HEAD_DIM), BF16)

        for i, (_, dil) in enumerate(DILATED):
            def prepare(b, carry, i=i, dil=dil):
                rows, _ = _block_rows(b, n_blocks, dil)
                vals = (_rms_rows(q_ref[rows, :], gq), _rms_rows(k_ref[rows, :], gk), v_ref[rows, :])
                for val, buf, pad in zip(vals, (qs, ks, vs), (0, BLK, BLK)):
                    buf[_sub_block(b, pad=pad), :] = val.astype(BF16)
                if i == 1:
                    slot = b % 2
                    for a, val in enumerate(vals):
                        tmp[slot, a] = val
                    for q, ratio, piece, dst in _regroup(b, n_blocks, dil, DILATED[2][1]):
                        for a, (buf, pad) in enumerate(((qs_w, 0), (ks_w, BLK), (vs_w, BLK))):
                            buf[pl.ds(dst + pad, piece), :] = tmp[slot, a, pl.ds(q, piece, stride=ratio), :].astype(BF16)
                return carry
            if i < 2:
                lax.fori_loop(0, n_blocks, prepare, 0, unroll=8)
            qs, ks, vs = (qs, ks, vs) if i < 2 else (qs_w, ks_w, vs_w)

            def group(g, carry, i=i, dil=dil, qs=qs, ks=ks, vs=vs):
                blocks = [g * ATTN_GROUP_FWD + t for t in range(ATTN_GROUP_FWD)]
                for t, b in enumerate(blocks):
                    sc_s[t] = _dot(qs[_sub_block(b), :], ks[_sub_block(b, 2 * BLK), :], tb=True)
                for t, b in enumerate(blocks):
                    rows, n = _block_rows(b, n_blocks, dil)
                    sc = sc_s[t] + bias_s[2 * i + jnp.minimum(n, 1)]
                    m_i = jnp.max(sc, axis=-1, keepdims=True)
                    pr = jnp.exp(sc - m_i)
                    l_i = jnp.sum(pr, axis=-1, keepdims=True)
                    p_s[t] = pr.astype(BF16)
                    m_s[new, rows, :] = m_i
                    l_s[new, rows, :] = l_i
                for t, b in enumerate(blocks):
                    rows, _ = _block_rows(b, n_blocks, dil)
                    acc_s[new, rows, :] = _dot(p_s[t], vs[_sub_block(b, 2 * BLK), :])
                return carry
            new = min(i, 1)
            lax.fori_loop(0, n_blocks // ATTN_GROUP_FWD, group, 0)
            if i > 0:
                m = jnp.maximum(m_s[0], m_s[1])
                w_old, w_new = jnp.exp(m_s[0] - m), jnp.exp(m_s[1] - m)
                m_s[0] = m
                l_s[0] = l_s[0] * w_old + l_s[1] * w_new
                acc_s[0] = acc_s[0] * w_old + acc_s[1] * w_new

        l = l_s[0]
        yb_ref[...] = acc_s[0] / l
        lse_ref[...] = m_s[0] + jnp.log(l)
        if comm:
            @pl.when(h == heads - 1)
            def _():
                comm.finish(cin_refs, cout_refs, sems)

    def head_col(off):
        return pl.BlockSpec((s, HEAD_DIM), lambda h: (0, off * heads + h))
    vec = pl.BlockSpec((1, HEAD_DIM), lambda h: (0, 0))
    res = pl.pallas_call(
        body, name="attn_fwd", grid=(heads,),
        in_specs=[head_col(3), head_col(4), head_col(5), vec, vec,
                  pl.BlockSpec(memory_space=pltpu.SMEM),
                  pl.BlockSpec((len(DILATED), BLK, 2 * BLK), lambda h: (0, 0, 0))] + [ANY] * len(c_ins),
        out_specs=[pl.BlockSpec((s, HEAD_DIM), lambda h: (0, h)), pl.BlockSpec((None, s, 1), lambda h: (h, 0, 0))]
        + [ANY] * len(c_outs),
        out_shape=[jax.ShapeDtypeStruct((s, heads * HEAD_DIM), F32), jax.ShapeDtypeStruct((heads, s, 1), F32)] + c_outs,
        scratch_shapes=[pltpu.VMEM((s, HEAD_DIM), BF16), pltpu.VMEM((s + BLK, HEAD_DIM), BF16),
                        pltpu.VMEM((s + BLK, HEAD_DIM), BF16),
                        pltpu.VMEM((s, HEAD_DIM), BF16), pltpu.VMEM((s + BLK, HEAD_DIM), BF16),
                        pltpu.VMEM((s + BLK, HEAD_DIM), BF16), pltpu.VMEM((2, 3, BLK, HEAD_DIM), F32),
                        pltpu.VMEM((2, s, 1), F32), pltpu.VMEM((2, s, 1), F32), pltpu.VMEM((2, s, HEAD_DIM), F32),
                        pltpu.VMEM((2 * len(DILATED), BLK, 2 * BLK), F32),
                        pltpu.VMEM((ATTN_GROUP_FWD, BLK, 2 * BLK), F32),
                        pltpu.VMEM((ATTN_GROUP_FWD, BLK, 2 * BLK), BF16)] + c_scratch,
        input_output_aliases={7 + i: 2 + i for i in range(len(c_ins))} if (comm and comm.in_place) else {},
        compiler_params=_params("arbitrary"),
    )(proj, proj, proj, g_q, g_k, rel_bias, tables, *c_ins)
    return res[0], res[1], res[2:]


def _attn_bwd(dproj, d_o, stats, proj, g_q, g_k, rel_bias, d_model, comm=None):
    s, d_in = proj.shape
    heads = d_model // 2 // HEAD_DIM
    n_blocks = s // BLK
    scale = HEAD_DIM ** -0.5
    tables = jnp.asarray(_bucket_tables())
    n_dil = len(DILATED)
    c_ins = list(comm.ins) if comm else []
    c_outs = list(comm.out_shape) if comm else []
    c_scratch = list(comm.scratch) if comm else []

    def body(dp_any, q_ref, k_ref, v_ref, do_ref, st_ref, gq_ref, gk_ref, rb_ref, tab_ref, *rest):
        cin_refs, rest = rest[:len(c_ins)], rest[len(c_ins):]
        dp_out, ggq_ref, ggk_ref, db_ref = rest[:4]
        cout_refs, rest = rest[4:4 + len(c_outs)], rest[4 + len(c_outs):]
        (qs, ks, vs, dos, st_s, qs_w, ks_w, vs_w, dos_w, st_w, tmp, tmp_st,
         dqn, dkn, dvv, dq_u, dk_u, dv_u, bias_s, dbias_s, sc_s, dp_s, p_s, ds_s, sems) = rest[:25]
        comm_sems = rest[25:]
        del dp_any
        h = pl.program_id(0)
        if comm:
            @pl.when(h == 0)
            def _():
                comm.start(cin_refs, cout_refs, comm_sems)


        @pl.when(h == 0)
        def _():
            ggq_ref[...] = jnp.zeros_like(ggq_ref)
            ggk_ref[...] = jnp.zeros_like(ggk_ref)

        dbias_s[...] = jnp.zeros_like(dbias_s)
        _bias_tiles(tab_ref, rb_ref, h, bias_s)
        gq = gq_ref[...] * scale
        gk = gk_ref[...]
        for buf in (ks, vs, ks_w, vs_w):
            buf[0:BLK, :] = jnp.zeros((BLK, HEAD_DIM), BF16)

        for i, (_, dil) in enumerate(DILATED):
            def prepare(b, carry, i=i, dil=dil):
                rows, _ = _block_rows(b, n_blocks, dil)
                vals = (_rms_rows(q_ref[rows, :], gq), _rms_rows(k_ref[rows, :], gk), v_ref[rows, :], do_ref[rows, :])
                for val, buf, pad in zip(vals, (qs, ks, vs, dos), (0, BLK, BLK, 0)):
                    buf[_sub_block(b, pad=pad), :] = val.astype(BF16)
                st = st_ref[rows, :]
                st_s[_sub_block(b), :] = st
                if i == 1:
                    slot = b % 2
                    for a, val in enumerate(vals):
                        tmp[slot, a] = val
                    tmp_st[slot] = st
                    for q, ratio, piece, dst in _regroup(b, n_blocks, dil, DILATED[2][1]):
                        for a, (buf, pad) in enumerate(((qs_w, 0), (ks_w, BLK), (vs_w, BLK), (dos_w, 0))):
                            buf[pl.ds(dst + pad, piece), :] = tmp[slot, a, pl.ds(q, piece, stride=ratio), :].astype(BF16)
                        st_w[pl.ds(dst, piece), :] = tmp_st[slot, pl.ds(q, piece, stride=ratio), :]
                return carry
            if i < 2:
                lax.fori_loop(0, n_blocks, prepare, 0, unroll=8)
            dk_u[0:BLK, :] = jnp.zeros((BLK, HEAD_DIM), F32)
            dv_u[0:BLK, :] = jnp.zeros((BLK, HEAD_DIM), F32)
            operands = (qs, ks, vs, dos, st_s) if i < 2 else (qs_w, ks_w, vs_w, dos_w, st_w)

            def group(g, carry, i=i, dil=dil, operands=operands):
                qs, ks, vs, dos, st_s = operands
                blocks = [g * ATTN_GROUP + t for t in range(ATTN_GROUP)]
                for t, b in enumerate(blocks):
                    sc_s[t] = _dot(qs[_sub_block(b), :], ks[_sub_block(b, 2 * BLK), :], tb=True)
                    dp_s[t] = _dot(dos[_sub_block(b), :], vs[_sub_block(b, 2 * BLK), :], tb=True)
                for t, b in enumerate(blocks):
                    _, n = _block_rows(b, n_blocks, dil)
                    st = st_s[_sub_block(b), :]
                    pr = jnp.exp(sc_s[t] + bias_s[2 * i + jnp.minimum(n, 1)] - st[:, 0:1])
                    ds = pr * (dp_s[t] - st[:, 1:2])
                    dbias_s[i] += ds
                    p_s[t] = pr.astype(BF16)
                    ds_s[t] = ds.astype(BF16)
                for t, b in enumerate(blocks):
                    dq_u[_sub_block(b), :] = _dot(ds_s[t], ks[_sub_block(b, 2 * BLK), :])
                    for part, lhs, rhs in ((dk_u, ds_s, qs), (dv_u, p_s, dos)):
                        both = _dot(lhs[t], rhs[_sub_block(b), :], ta=True)
                        part[_sub_block(b), :] += both[:BLK]
                        part[_sub_block(b, pad=BLK), :] = both[BLK:]
                return carry
            lax.fori_loop(0, n_blocks // ATTN_GROUP, group, 0)

            def scatter(b, carry, i=i, dil=dil):
                if i < 2:
                    rows, _ = _block_rows(b, n_blocks, dil)
                    for acc, part, pad in ((dqn, dq_u, 0), (dkn, dk_u, BLK), (dvv, dv_u, BLK)):
                        val = part[_sub_block(b, pad=pad), :]
                        acc[rows, :] = val if i == 0 else acc[rows, :] + val
                else:
                    middle = DILATED[1][1]
                    rows, _ = _block_rows(b, n_blocks, middle)
                    slot = b % 2
                    for a, (acc, part, pad) in enumerate(((dqn, dq_u, 0), (dkn, dk_u, BLK), (dvv, dv_u, BLK))):
                        for q, ratio, piece, src in _regroup(b, n_blocks, middle, dil):
                            tmp[slot, a, pl.ds(q, piece, stride=ratio), :] = part[pl.ds(src + pad, piece), :]
                        acc[rows, :] += tmp[slot, a]
                return carry
            lax.fori_loop(0, n_blocks, scatter, 0, unroll=8)

        q = q_ref[...]
        rq = lax.rsqrt(jnp.mean(q * q, axis=-1, keepdims=True) + EPS)
        k = k_ref[...]
        rk = lax.rsqrt(jnp.mean(k * k, axis=-1, keepdims=True) + EPS)
        dq_n = dqn[...]
        ggq_ref[...] += jnp.sum(dq_n * (q * rq) * scale, axis=0, keepdims=True)
        dg = dq_n * gq_ref[...] * scale
        qs[...] = (rq * (dg - q * (rq * rq) * jnp.mean(dg * q, axis=-1, keepdims=True))).astype(BF16)
        dk_n = dkn[...]
        ggk_ref[...] += jnp.sum(dk_n * (k * rk), axis=0, keepdims=True)
        dg = dk_n * gk_ref[...]
        dos[...] = (rk * (dg - k * (rk * rk) * jnp.mean(dg * k, axis=-1, keepdims=True))).astype(BF16)
        vs[BLK:, :] = dvv[...].astype(BF16)
        copies = [pltpu.make_async_copy(src, dp_out.at[:, pl.ds(pl.multiple_of(((3 + j) * heads + h) * HEAD_DIM, HEAD_DIM), HEAD_DIM)],
                                        sems.at[j]) for j, src in enumerate((qs, dos, vs.at[pl.ds(BLK, s)]))]
        for cp in copies:
            cp.start()
        for b in range(NUM_BUCKETS):
            tot = jnp.zeros((BLK, 2 * BLK), F32)
            for i in range(n_dil):
                tot = tot + jnp.where(tab_ref[i] == b, dbias_s[i], 0.0)
            db_ref[b:b + 1, :] = jnp.full((1, LANE), jnp.sum(tot), F32)
        for cp in copies:
            cp.wait()
        if comm:
            @pl.when(h == heads - 1)
            def _():
                comm.finish(cin_refs, cout_refs, comm_sems)

    def head_col(off):
        return pl.BlockSpec((s, HEAD_DIM), lambda h: (0, off * heads + h))
    own_col = pl.BlockSpec((s, HEAD_DIM), lambda h: (0, h))
    vec = pl.BlockSpec((1, HEAD_DIM), lambda h: (0, 0))
    big = pltpu.VMEM((s, HEAD_DIM), F32)
    padded32 = pltpu.VMEM((s + BLK, HEAD_DIM), F32)
    padded16 = pltpu.VMEM((s + BLK, HEAD_DIM), BF16)
    res = pl.pallas_call(
        body, name="attn_bwd", grid=(heads,),
        in_specs=[ANY, head_col(3), head_col(4), head_col(5), own_col,
                  pl.BlockSpec((None, s, 2), lambda h: (h, 0, 0)), vec, vec,
                  pl.BlockSpec(memory_space=pltpu.SMEM),
                  pl.BlockSpec((n_dil, BLK, 2 * BLK), lambda h: (0, 0, 0))] + [ANY] * len(c_ins),
        out_specs=[ANY, vec, vec, pl.BlockSpec((None, NUM_BUCKETS, LANE), lambda h: (h, 0, 0))] + [ANY] * len(c_outs),
        out_shape=[jax.ShapeDtypeStruct((s, d_in), BF16), jax.ShapeDtypeStruct((1, HEAD_DIM), F32),
                   jax.ShapeDtypeStruct((1, HEAD_DIM), F32), jax.ShapeDtypeStruct((heads, NUM_BUCKETS, LANE), F32)]
        + c_outs,
        scratch_shapes=[pltpu.VMEM((s, HEAD_DIM), BF16), padded16, padded16, pltpu.VMEM((s, HEAD_DIM), BF16),
                        pltpu.VMEM((s, 2), F32),
                        pltpu.VMEM((s, HEAD_DIM), BF16), padded16, padded16, pltpu.VMEM((s, HEAD_DIM), BF16),
                        pltpu.VMEM((s, 2), F32),
                        pltpu.VMEM((2, 4, BLK, HEAD_DIM), F32), pltpu.VMEM((2, BLK, 2), F32),
                        big, big, big, big, padded32, padded32,
                        pltpu.VMEM((2 * n_dil, BLK, 2 * BLK), F32), pltpu.VMEM((n_dil, BLK, 2 * BLK), F32),
                        pltpu.VMEM((ATTN_GROUP, BLK, 2 * BLK), F32), pltpu.VMEM((ATTN_GROUP, BLK, 2 * BLK), F32),
                        pltpu.VMEM((ATTN_GROUP, BLK, 2 * BLK), BF16), pltpu.VMEM((ATTN_GROUP, BLK, 2 * BLK), BF16),
                        pltpu.SemaphoreType.DMA((3,))] + c_scratch,
        input_output_aliases={0: 0},
        compiler_params=_params("arbitrary"),
    )(dproj, proj, proj, proj, d_o, stats, g_q, g_k, rel_bias, tables, *c_ins)
    return res[0], res[1], res[2], res[3], res[4:]


def _local_step(xs, ps, tgt, sp, win4, wout4, wgate4, wup4, dist=None):
    s, d = xs.shape
    wa = d // 2
    heads = wa // HEAD_DIM
    d_in = 7 * wa
    c4 = d_in // N_CHIPS
    dple = ps.shape[1]

    tm, tn = _tile(s, 1024), _tile(d, 1024)
    tn_in = _tile(c4, 1024)
    per_in = c4 // tn_in
    tn_up = _tile(d // N_CHIPS, 512)
    per_up = (d // N_CHIPS) // tn_up
    gm, gn = s // tm, d // tn

    hn, rx = _rms_fwd("rms_pre", xs, sp["g_pre"])
    if dist:
        proj, win4 = _mm_in_gather(dist["chip"], hn, win4)
    else:
        proj = _matmul(
            "mm_in", hn, win4, pl.BlockSpec((tm, d), lambda i, j, k: (i, 0)), _shard_spec(d, tn_in, per_in, 2, 1),
            (gm, d_in // tn_in, 1), None, jax.ShapeDtypeStruct((s, d_in), F32),
            pl.BlockSpec((tm, tn_in), lambda i, j, k: (i, j)))
    b_st = sp["b_s"].T
    y, z, vn, mu_v, rs_v, ra = _gmlp_fwd(proj, sp["w_s"], b_st, sp["ln_v_g"], sp["ln_v_b"], sp["g_out_a"], d)
    yb, lse, gathered = _attn_fwd(proj, sp["g_q"], sp["g_k"], sp["rel_bias"], d,
                                  comm=_Gather([wout4, wgate4, wup4]) if dist else None)
    if dist:
        wout4, wgate4, wup4 = gathered
    wout, wgate = wout4.reshape(d, d), wgate4.reshape(d, d)
    y, rb = _gate_fwd(y, yb, proj, sp["g_out_b"], d)

    def residual(acc, extra, outs):
        outs[0][...] = extra[0][...] + acc

    tile = pl.BlockSpec((tm, tn), lambda i, j, k: (i, j))
    tile_up = pl.BlockSpec((tm, tn_up), lambda i, j, k: (i, j))
    h = _matmul(
        "mm_out", y, wout, pl.BlockSpec((tm, d), lambda i, j, k: (i, 0)), pl.BlockSpec((d, tn), lambda i, j, k: (0, j)),
        (gm, gn, 1), None, jax.ShapeDtypeStruct((s, d), F32), tile, extras=(xs,), extra_specs=(tile,),
        epilogue=residual)
    hp, rh = _rms_fwd("rms_ple", h, sp["g_ple"])
    p16 = ps.astype(BF16)

    def head(acc, extra, outs):
        p_ref, wup_ref, h_ref, t_ref = extra
        dout_ref, dgl_ref, dup_ref, loss_ref = outs
        up = _dot(p_ref[...], wup_ref[...])
        gate = jax.nn.sigmoid(acc)
        err = h_ref[...] + gate * up - t_ref[...]
        sq = _colsum8(err * err)
        part = sq[:, 0:LANE]
        for c in range(1, sq.shape[1] // LANE):
            part = part + sq[:, c * LANE:(c + 1) * LANE]
        loss_ref[...] = part
        dout = err * (1.0 / d)
        dout_ref[...] = dout.astype(BF16)
        dup_ref[...] = (dout * gate).astype(BF16)
        dgl_ref[...] = (dout * up * gate * (1.0 - gate)).astype(BF16)

    tile_up = pl.BlockSpec((tm, tn_up), lambda i, j, k: (i, j))
    dout, dglin, dup, loss_parts = _matmul(
        "mm_gate_loss", hp, wgate, pl.BlockSpec((tm, d), lambda i, j, k: (i, 0)), pl.BlockSpec((d, tn_up), lambda i, j, k: (0, j)),
        (gm, d // tn_up, 1), None,
        (jax.ShapeDtypeStruct((s, d), BF16), jax.ShapeDtypeStruct((s, d), BF16), jax.ShapeDtypeStruct((s, d), BF16),
         jax.ShapeDtypeStruct((gm * 8, (d // tn_up) * LANE), F32)),
        (tile_up, tile_up, tile_up, pl.BlockSpec((8, LANE), lambda i, j, k: (i, j))),
        extras=(p16, wup4, h, tgt),
        extra_specs=(pl.BlockSpec((tm, dple), lambda i, j, k: (i, 0)), _shard_spec(dple, tn_up, per_up, 2, 1),
                     tile_up, tile_up),
        epilogue=head)
    loss = 0.5 * jnp.sum(loss_parts) * (1.0 / d)

    dhp = _matmul(
        "mm_dhp", dglin, wgate, pl.BlockSpec((tm, d), lambda i, j, k: (i, 0)), pl.BlockSpec((tn, d), lambda i, j, k: (j, 0)),
        (gm, gn, 1), None, jax.ShapeDtypeStruct((s, d), BF16), tile, tb=True)
    dh, dh16, g_g_ple = _rms_bwd("rms_ple_bwd", dhp, h, rh, sp["g_ple"], dout, True)
    tmw = _tile(d, 1024)
    g_wout = _matmul(
        "mm_gwout", y, dh16, pl.BlockSpec((s, tmw), lambda i, j, k: (0, i)), pl.BlockSpec((s, tn), lambda i, j, k: (0, j)),
        (d // tmw, gn, 1), None, jax.ShapeDtypeStruct((d, d), F32), pl.BlockSpec((tmw, tn), lambda i, j, k: (i, j)), ta=True)
    g_wout4 = g_wout.reshape(N_CHIPS, d // N_CHIPS, d)
    g_wgate = _matmul(
        "mm_gwgate", hp, dglin, pl.BlockSpec((s, tmw), lambda i, j, k: (0, i)), pl.BlockSpec((s, tn), lambda i, j, k: (0, j)),
        (d // tmw, gn, 1), None, jax.ShapeDtypeStruct((d, d), F32), pl.BlockSpec((tmw, tn), lambda i, j, k: (i, j)), ta=True,
        comm=_SwapHalves([g_wout4]) if dist else None)
    if dist:
        g_wgate, swapped_out = g_wgate
    g_wgate4 = g_wgate.reshape(N_CHIPS, d // N_CHIPS, d)
    g_wup = _matmul(
        "mm_gwup", p16, dup, pl.BlockSpec((s, dple), lambda i, j, k: (0, 0)), pl.BlockSpec((s, tn_up), lambda i, j, k: (0, j)),
        (1, d // tn_up, 1), None, jax.ShapeDtypeStruct((N_CHIPS, dple, d // N_CHIPS), F32),
        _shard_spec(dple, tn_up, per_up, 0, 1), ta=True)
    dy = _matmul(
        "mm_dy", dh16, wout, pl.BlockSpec((tm, d), lambda i, j, k: (i, 0)), pl.BlockSpec((tn, d), lambda i, j, k: (j, 0)),
        (gm, gn, 1), None, jax.ShapeDtypeStruct((s, d), BF16), tile, tb=True,
        comm=_SwapHalves([g_wgate4, g_wup]) if dist else None)
    early = (g_wout4, g_wgate4, g_wup)
    if dist:
        core = dist["core"]
        dy, swapped_rest = dy
        early_sums = [_chip_sum("chip_sum_" + n, dist["chip"], core, g, o)
                      for n, g, o in zip(LARGE[1:], early, list(swapped_out) + list(swapped_rest))]

    dproj, d_o, stats, g_g_out_b = _gate_bwd(dy, yb, lse, proj, rb, sp["g_out_b"], d)
    dproj, g_g_q, g_g_k, dbias, early_got = _attn_bwd(
        dproj, d_o, stats, proj, sp["g_q"], sp["g_k"], sp["rel_bias"], d,
        comm=_ExchangeChips([s16 for _, s16 in early_sums]) if dist else None)
    dproj, g_w_s, dz_sum, g_ln_g, g_ln_b, g_g_out_a = _gmlp_bwd(
        dproj, dy, proj, z, vn, mu_v, rs_v, ra, sp["w_s"], sp["ln_v_g"], sp["g_out_a"], d)

    if dist:
        tmw_half = _tile(d // 2, 1024)
        n_half = (d // 2) // tmw_half

        def half_of_gwin(name, own, comm):
            def rows(i, j, k, core_ref):
                return (0, (core_ref[0] if own else 1 - core_ref[0]) * n_half + i)
            return _matmul(
                name, hn, dproj, pl.BlockSpec((s, tmw_half), rows),
                pl.BlockSpec((s, tn_in), lambda i, j, k, core_ref: (0, j)),
                (n_half, d_in // tn_in, 1), None, jax.ShapeDtypeStruct((N_CHIPS, d // 2, c4), F32 if own else BF16),
                pl.BlockSpec((None, tmw_half, tn_in), lambda i, j, k, core_ref: (j // per_in, i, j % per_in)), ta=True,
                comm=comm, prefetch=(core,))

        g_theirs = half_of_gwin("mm_gwin_theirs", False, None)
        g_mine, (g_from_sibling,) = half_of_gwin("mm_gwin_mine", True, _SwapHalves([g_theirs], whole=True))
        win_sum = _chip_sum("chip_sum_w_in", dist["chip"], core, g_mine, g_from_sibling)
    else:
        g_win = _matmul(
            "mm_gwin", hn, dproj, pl.BlockSpec((s, tmw), lambda i, j, k: (0, i)),
            pl.BlockSpec((s, tn_in), lambda i, j, k: (0, j)),
            (d // tmw, d_in // tn_in, 1), None, jax.ShapeDtypeStruct((N_CHIPS, d, c4), F32),
            _shard_spec(tmw, tn_in, per_in, 0, 1), ta=True)
    tk = c4
    tmh, tnh = _tile(s, 1024), _tile(d, 1024)
    dhn = _matmul(
        "mm_dhn", dproj, win4, pl.BlockSpec((tmh, tk), lambda i, j, k: (i, k)), _shard_spec(tnh, tk, 1, 1, 2),
        (s // tmh, d // tnh, d_in // tk), (tmh, tnh), jax.ShapeDtypeStruct((s, d), BF16),
        pl.BlockSpec((tmh, tnh), lambda i, j, k: (i, j)), tb=True,
        comm=_ExchangeChips([win_sum[1]]) if dist else None)
    if dist:
        dhn, win_got = dhn
    grad_x, g_g_pre = _rms_bwd("rms_pre_bwd", dhn, xs, rx, sp["g_pre"], dh, False)

    small = {
        "g_pre": g_g_pre,
        "w_s": g_w_s,
        "b_s": jnp.sum(dz_sum.reshape(CHUNK, heads, HEAD_DIM), axis=-1).T,
        "ln_v_g": g_ln_g, "ln_v_b": g_ln_b,
        "g_q": g_g_q, "g_k": g_g_k,
        "rel_bias": dbias[:, :, 0].T,
        "g_out_a": g_g_out_a, "g_out_b": g_g_out_b,
        "g_ple": g_g_ple,
    }
    if not dist:
        return loss, grad_x, (g_win, *early), small
    sums32 = [win_sum[0]] + [s32 for s32, _ in early_sums]
    return loss, grad_x, (sums32, list(win_got) + list(early_got)), small


def _place():
    x, y, c = lax.axis_index("x"), lax.axis_index("y"), lax.axis_index("c")
    chips = [(1 - x, y), (x, 1 - y), (1 - x, 1 - y)]
    return x, y, c, chips


def _remote(src, dst, send_sems, recv_sems, k, to):
    return pltpu.make_async_remote_copy(src_ref=src, dst_ref=dst, send_sem=send_sems.at[k], recv_sem=recv_sems.at[k],
                                        device_id=to, device_id_type=MESH)


def _cast_bf16(name, chip, w):
    r, c = w.shape
    tm = _tile(r, 256)

    def body(chip_ref, w_ref, o_ref):
        del chip_ref
        o_ref[...] = w_ref[...].astype(BF16)

    return pl.pallas_call(
        body, name=name,
        grid_spec=pltpu.PrefetchScalarGridSpec(
            num_scalar_prefetch=1, grid=(r // tm,),
            in_specs=[pl.BlockSpec((tm, c), lambda i, chip_ref: (i, 0))],
            out_specs=pl.BlockSpec((None, tm, c), lambda i, chip_ref: (chip_ref[0], i, 0))),
        out_shape=jax.ShapeDtypeStruct((N_CHIPS, r, c), BF16), compiler_params=_params("parallel"),
    )(chip, w)


class _Gather:
    in_place = True

    def __init__(self, fulls):
        self.ins = list(fulls)
        self.out_shape = [jax.ShapeDtypeStruct(f.shape, f.dtype) for f in fulls]
        n = len(fulls)
        self.scratch = [pltpu.SemaphoreType.DMA((6 * n,)), pltpu.SemaphoreType.DMA((6 * n,))]

    @staticmethod
    def _sends(outs, sems):
        send_sems, recv_sems = sems
        x, y, c, chips = _place()
        cps = []
        for w, ref in enumerate(outs):
            half = ref.shape[1] // 2
            blk = ref.at[2 * x + y, pl.ds(c * half, half)]
            cps += [_remote(blk, blk, send_sems, recv_sems, 6 * w + q, (*chip, c)) for q, chip in enumerate(chips)]
        return cps

    def start(self, ins, outs, sems):
        for cp in self._sends(outs, sems):
            cp.start()

    def finish(self, ins, outs, sems):
        send_sems, recv_sems = sems
        x, y, c, chips = _place()
        sibling = (x, y, 1 - c)
        forwards = []
        for w, ref in enumerate(outs):
            half = ref.shape[1] // 2
            for q, chip in enumerate(chips):
                blk = ref.at[2 * chip[0] + chip[1], pl.ds(c * half, half)]
                _remote(blk, blk, send_sems, recv_sems, 6 * w + q, sibling).wait_recv()
                fwd = _remote(blk, blk, send_sems, recv_sems, 6 * w + 3 + q, sibling)
                fwd.start()
                forwards.append(fwd)
        for w, ref in enumerate(outs):
            half = ref.shape[1] // 2
            for q, chip in enumerate(chips):
                blk = ref.at[2 * chip[0] + chip[1], pl.ds((1 - c) * half, half)]
                _remote(blk, blk, send_sems, recv_sems, 6 * w + 3 + q, sibling).wait_recv()
        for cp in self._sends(outs, sems) + forwards:
            cp.wait_send()


class _ExchangeChips:
    in_place = False

    def __init__(self, sums16):
        self.ins = list(sums16)
        self.out_shape = [jax.ShapeDtypeStruct(g.shape, g.dtype) for g in sums16]
        n = len(sums16)
        self.scratch = [pltpu.SemaphoreType.DMA((3 * n,)), pltpu.SemaphoreType.DMA((3 * n,))]

    @staticmethod
    def _sends(ins, outs, sems):
        send_sems, recv_sems = sems
        x, y, c, chips = _place()
        return [_remote(ins[w].at[2 * chip[0] + chip[1]], outs[w].at[2 * x + y], send_sems, recv_sems, 3 * w + q, (*chip, c))
                for w in range(len(ins)) for q, chip in enumerate(chips)]

    def start(self, ins, outs, sems):
        for cp in self._sends(ins, outs, sems):
            cp.start()

    def finish(self, ins, outs, sems):
        send_sems, recv_sems = sems
        x, y, c, chips = _place()
        for w in range(len(ins)):
            for q, chip in enumerate(chips):
                blk = outs[w].at[2 * chip[0] + chip[1]]
                _remote(blk, blk, send_sems, recv_sems, 3 * w + q, (*chip, c)).wait_recv()
        for cp in self._sends(ins, outs, sems):
            cp.wait_send()


def _mm_in_gather(chip, hn, win4):
    s, d = hn.shape
    c4 = win4.shape[2]
    tm, tn = _tile(s, 1024), _tile(c4, 1024)
    per = c4 // tn
    nj, gm = N_CHIPS * per, s // tm
    half = d // 2

    def tile_of(j):
        jj = j - per
        o = jnp.where(j < per, 0, jnp.where(j < 3 * per, 1 + jj % 2, 3))
        t = jnp.where(j < per, j, jnp.where(j < 3 * per, jj // 2, j - 3 * per))
        return o, t

    def step_of(o, t):
        return per + 2 * t + o - 1 if o in (1, 2) else o * per + t

    def block_of(j, chip_ref):
        o, _ = tile_of(j)
        return chip_ref[0] ^ (((o & 1) << 1) | (o >> 1))

    i_late = max(gm - 2, 0)

    def body(chip_ref, hn_ref, w_any, proj_ref, w_ref, wbuf, wsem, send_sems, recv_sems):
        del w_any
        j, i = pl.program_id(0), pl.program_id(1)
        x, y, c, chips = _place()
        sibling = (x, y, 1 - c)

        def region(k, rows_half, t):
            return w_ref.at[k, pl.ds(rows_half * half, half), pl.ds(t * tn, tn)]

        def send_mine(q, t):
            blk = region(2 * x + y, c, t)
            return _remote(blk, blk, send_sems, recv_sems, q * per + t, (*chips[q], c))

        def forward(q, t):
            blk = region(2 * chips[q][0] + chips[q][1], c, t)
            return _remote(blk, blk, send_sems, recv_sems, (3 + q) * per + t, sibling)

        def fetch(jj, mine, slot):
            rows = pl.ds((c if mine else 1 - c) * half, half)
            cols = pl.ds(pl.multiple_of(tile_of(jj)[1] * tn, LANE), tn)
            return pltpu.make_async_copy(w_ref.at[block_of(jj, chip_ref), rows, cols], wbuf.at[slot, rows],
                                         wsem.at[2 * slot + (0 if mine else 1)])

        def foreign_tiles():
            return [(o, t) for o in range(1, N_CHIPS) for t in range(per)]

        relayed = [(0, t) for t in range(1, per, 2)] + [(1, t) for t in range(0, per, 2)]

        def relay(q, t):
            blk = region(2 * chips[q][0] + chips[q][1], c, t)
            return _remote(blk, blk, send_sems, recv_sems, 2 * per + t, (*chips[1 - q], c))

        def landed(q, t):
            blk = region(2 * chips[q][0] + chips[q][1], c, t)
            _remote(blk, blk, send_sems, recv_sems, q * per + t, sibling).wait_recv()
            forward(q, t).start()
            if (q, t) in relayed:
                relay(q, t).start()

        @pl.when((j == 0) & (i == 0))
        def _():
            for t in range(per):
                send_mine(0, t).start()
                send_mine(1, t).start()
            fetch(0, True, 0).start()
            fetch(0, False, 0).start()

        nxt = j + 1

        @pl.when((i == 0) & (nxt < nj))
        def _():
            for o, t in foreign_tiles():
                @pl.when(nxt == step_of(o, t))
                def _():
                    if o == 1:
                        landed(0, t)
                        landed(1, t)
                    elif o == 3:
                        landed(2, t)

            fetch(nxt, True, nxt % 2).start()

            @pl.when(nxt < per)
            def _():
                fetch(nxt, False, nxt % 2).start()

        @pl.when((i == i_late) & (nxt < nj))
        def _():
            for o, t in foreign_tiles():
                @pl.when(nxt == step_of(o, t))
                def _():
                    blk = region(2 * chips[o - 1][0] + chips[o - 1][1], 1 - c, t)
                    _remote(blk, blk, send_sems, recv_sems, (3 + o - 1) * per + t, sibling).wait_recv()
                    fetch(nxt, False, nxt % 2).start()

        @pl.when(i == 0)
        def _():
            fetch(j, True, j % 2).wait()
            fetch(j, False, j % 2).wait()

        proj_ref[...] = _dot(hn_ref[...], wbuf[j % 2])

        @pl.when((j == nj - 1) & (i == gm - 1))
        def _():
            for t in range(per):
                send_mine(0, t).wait_send()
                send_mine(1, t).wait_send()
                for q in range(3):
                    forward(q, t).wait_send()
            for q, t in relayed:
                relay(q, t).wait_send()

    return pl.pallas_call(
        body, name="mm_in_gather",
        grid_spec=pltpu.PrefetchScalarGridSpec(
            num_scalar_prefetch=1, grid=(nj, gm),
            in_specs=[pl.BlockSpec((tm, d), lambda j, i, chip_ref: (i, 0)), ANY],
            out_specs=[pl.BlockSpec((tm, tn), lambda j, i, chip_ref: (i, block_of(j, chip_ref) * per + tile_of(j)[1])), ANY],
            scratch_shapes=[pltpu.VMEM((2, d, tn), BF16), pltpu.SemaphoreType.DMA((4,)),
                            pltpu.SemaphoreType.DMA((6 * per,)), pltpu.SemaphoreType.DMA((6 * per,))]),
        out_shape=[jax.ShapeDtypeStruct((s, N_CHIPS * c4), F32), jax.ShapeDtypeStruct(win4.shape, win4.dtype)],
        input_output_aliases={2: 1},
        compiler_params=_params("arbitrary", "arbitrary"),
    )(chip, hn, win4)


class _SwapHalves:
    in_place = False

    def __init__(self, grads, whole=False):
        self.ins = list(grads)
        self.whole = whole
        self.out_shape = [jax.ShapeDtypeStruct((N_CHIPS, g.shape[1] // (1 if whole else 2), g.shape[2]), g.dtype)
                          for g in grads]
        self.scratch = [pltpu.SemaphoreType.DMA((len(grads),)), pltpu.SemaphoreType.DMA((len(grads),))]

    def _copies(self, ins, outs, sems):
        x, y, c, _ = _place()
        cps = []
        for w in range(len(ins)):
            half = ins[w].shape[1] // 2
            src = ins[w] if self.whole else ins[w].at[:, pl.ds((1 - c) * half, half)]
            cps.append(_remote(src, outs[w], sems[0], sems[1], w, (x, y, 1 - c)))
        return cps

    def start(self, ins, outs, sems):
        for cp in self._copies(ins, outs, sems):
            cp.start()

    def finish(self, ins, outs, sems):
        for cp in self._copies(ins, outs, sems):
            cp.wait()


def _chip_sum(name, chip, core, grad, got):
    _, half, c = got.shape
    th = _tile(half, 128)
    n = half // th
    skip = n if grad.shape[1] != half else 0

    def body(chip_ref, core_ref, g_ref, o_ref, s_ref):
        del chip_ref, core_ref
        s_ref[...] = (g_ref[...] + o_ref[...].astype(F32)).astype(s_ref.dtype)

    def specs(first, out_block):
        def block(k, i, chip_ref):
            return (chip_ref[0] + first + k) % N_CHIPS

        return dict(
            in_specs=[pl.BlockSpec((None, th, c), lambda k, i, chip_ref, core_ref: (block(k, i, chip_ref), core_ref[0] * skip + i, 0)),
                      pl.BlockSpec((None, th, c), lambda k, i, chip_ref, core_ref: (block(k, i, chip_ref), i, 0))],
            out_specs=out_block(block))

    own = pl.pallas_call(
        body, name=name + "_own",
        grid_spec=pltpu.PrefetchScalarGridSpec(
            num_scalar_prefetch=2, grid=(1, n),
            **specs(0, lambda block: pl.BlockSpec((th, c), lambda k, i, chip_ref, core_ref: (i, 0)))),
        out_shape=jax.ShapeDtypeStruct((half, c), F32), compiler_params=_params("parallel", "parallel"),
    )(chip, core, grad, got)
    others = pl.pallas_call(
        functools.partial(body), name=name + "_others",
        grid_spec=pltpu.PrefetchScalarGridSpec(
            num_scalar_prefetch=2, grid=(N_CHIPS - 1, n),
            **specs(1, lambda block: pl.BlockSpec(
                (None, th, c), lambda k, i, chip_ref, core_ref: (block(k, i, chip_ref), i, 0)))),
        out_shape=jax.ShapeDtypeStruct((N_CHIPS, half, c), BF16), compiler_params=_params("parallel", "parallel"),
    )(chip, core, grad, got)
    return own, others


def _riding(comm, grid, body, n_pre, n_in, n_out):
    if comm is None:
        return body
    c_in, c_out, c_scr = len(comm.ins), len(comm.out_shape), len(comm.scratch)

    def wrapped(*refs):
        pre, rest = refs[:n_pre], refs[n_pre:]
        ins, rest = rest[:n_in], rest[n_in:]
        cin, rest = rest[:c_in], rest[c_in:]
        outs, rest = rest[:n_out], rest[n_out:]
        cout, rest = rest[:c_out], rest[c_out:]
        own, sems = rest[:len(rest) - c_scr], rest[len(rest) - c_scr:]
        ids = [pl.program_id(ax) for ax in range(len(grid))]
        first, last = ids[0] == 0, ids[0] == grid[0] - 1
        for ax in range(1, len(grid)):
            first, last = first & (ids[ax] == 0), last & (ids[ax] == grid[ax] - 1)

        @pl.when(first)
        def _():
            comm.start(cin, cout, sems)

        body(*pre, *ins, *outs, *own)

        @pl.when(last)
        def _():
            comm.finish(cin, cout, sems)

    return wrapped


def _total(name, chip, core, sum32, got16, comm=None):
    half, c = sum32.shape
    th = _tile(half, 128)
    n = half // th

    def body(chip_ref, core_ref, own_ref, a_ref, b_ref, c_ref, o_ref):
        del chip_ref, core_ref
        o_ref[...] = ((own_ref[...] + a_ref[...].astype(F32)) + b_ref[...].astype(F32)) + c_ref[...].astype(F32)

    def other(step):
        return pl.BlockSpec((None, th, c), lambda i, chip_ref, core_ref: ((chip_ref[0] + step) % N_CHIPS, i, 0))

    c_ins = list(comm.ins) if comm else []
    res = pl.pallas_call(
        _riding(comm, (n,), body, 2, 4, 1), name=name,
        grid_spec=pltpu.PrefetchScalarGridSpec(
            num_scalar_prefetch=2, grid=(n,),
            in_specs=[pl.BlockSpec((th, c), lambda i, chip_ref, core_ref: (i, 0)), other(1), other(2), other(3)]
            + [ANY] * len(c_ins),
            out_specs=[pl.BlockSpec((th, c), lambda i, chip_ref, core_ref: (core_ref[0] * n + i, 0))]
            + [ANY] * len(c_ins),
            scratch_shapes=list(comm.scratch) if comm else []),
        out_shape=[jax.ShapeDtypeStruct((2 * half, c), F32)] + (list(comm.out_shape) if comm else []),
        input_output_aliases={6 + i: 1 + i for i in range(len(c_ins))} if comm else {},
        compiler_params=_params("arbitrary" if comm else "parallel"),
    )(chip, core, sum32, got16, got16, got16, *c_ins)
    return (res[0], res[1:]) if comm else res[0]


class _JoinHalves:
    in_place = True

    def __init__(self, blocks):
        self.ins = list(blocks)
        self.out_shape = [jax.ShapeDtypeStruct(t.shape, t.dtype) for t in blocks]
        self.scratch = [pltpu.SemaphoreType.DMA((len(blocks),)), pltpu.SemaphoreType.DMA((len(blocks),))]

    @staticmethod
    def _copy(ref, w, mine, sems):
        x, y, c, _ = _place()
        half = ref.shape[0] // 2
        rows = ref.at[pl.ds((c if mine else 1 - c) * half, half)]
        return _remote(rows, rows, sems[0], sems[1], w, (x, y, 1 - c))

    def start(self, ins, outs, sems):
        for w, ref in enumerate(outs):
            self._copy(ref, w, True, sems).start()

    def finish(self, ins, outs, sems):
        for w, ref in enumerate(outs):
            self._copy(ref, w, True, sems).wait_send()
            self._copy(ref, w, False, sems).wait_recv()


class _Both:
    in_place = True

    def __init__(self, a, b):
        assert a.in_place and b.in_place
        self.parts = (a, b)
        self.ins = list(a.ins) + list(b.ins)
        self.out_shape = list(a.out_shape) + list(b.out_shape)
        self.scratch = list(a.scratch) + list(b.scratch)

    def _split(self, ins, outs, sems):
        a = self.parts[0]
        n, k = len(a.ins), len(a.scratch)
        return (a, ins[:n], outs[:n], sems[:k]), (self.parts[1], ins[n:], outs[n:], sems[k:])

    def start(self, ins, outs, sems):
        for part, i, o, s in self._split(ins, outs, sems):
            part.start(i, o, s)

    def finish(self, ins, outs, sems):
        for part, i, o, s in self._split(ins, outs, sems):
            part.finish(i, o, s)


def _alone(name, comm):
    n = len(comm.ins)

    def body(*refs):
        comm.start(refs[:n], refs[n:2 * n], refs[2 * n:])
        comm.finish(refs[:n], refs[n:2 * n], refs[2 * n:])

    return pl.pallas_call(
        body, name=name, in_specs=[ANY] * n, out_specs=[ANY] * n, out_shape=comm.out_shape,
        scratch_shapes=comm.scratch, input_output_aliases={i: i for i in range(n)},
    )(*comm.ins)


class _GatherSmall:
    in_place = True

    def __init__(self, full):
        self.ins = [full]
        self.out_shape = [jax.ShapeDtypeStruct(full.shape, full.dtype)]
        self.scratch = [pltpu.SemaphoreType.DMA((7,)), pltpu.SemaphoreType.DMA((7,))]

    @staticmethod
    def _copy(ref, sems, k, block, to):
        m_per = ref.shape[0] // N_DEV
        px, py, pc = block
        rows = ref.at[pl.ds((4 * px + 2 * py + pc) * m_per, m_per), :]
        return _remote(rows, rows, sems[0], sems[1], k, to)

    def _first(self, ref, sems):
        x, y, c, chips = _place()
        me = (x, y, c)
        return [self._copy(ref, sems, 0, me, (x, y, 1 - c))] + \
               [self._copy(ref, sems, 1 + j, me, (*chip, c)) for j, chip in enumerate(chips)]

    def start(self, ins, outs, sems):
        for cp in self._first(outs[0], sems):
            cp.start()

    def finish(self, ins, outs, sems):
        ref = outs[0]
        x, y, c, chips = _place()
        me, sibling = (x, y, c), (x, y, 1 - c)
        passed = [self._copy(ref, sems, 4 + j, (*chip, c), sibling) for j, chip in enumerate(chips)]
        for j, chip in enumerate(chips):
            self._copy(ref, sems, 1 + j, (*chip, c), me).wait_recv()
            passed[j].start()
        self._copy(ref, sems, 0, sibling, me).wait_recv()
        for j, chip in enumerate(chips):
            self._copy(ref, sems, 4 + j, (*chip, 1 - c), me).wait_recv()
        for cp in self._first(ref, sems) + passed:
            cp.wait_send()


def _adamw_math(w, g, m, v):
    m = ADAM_B1 * m + (1.0 - ADAM_B1) * g
    v = ADAM_B2 * v + (1.0 - ADAM_B2) * (g * g)
    m_hat = m / (1.0 - ADAM_B1 ** ADAM_STEP)
    v_hat = v / (1.0 - ADAM_B2 ** ADAM_STEP)
    delta = -ADAM_LR * (m_hat / (jnp.sqrt(v_hat) + ADAM_EPS) + ADAM_WD * w)
    return delta, m, v


def _adamw(name, w, g, m, v):
    r, c = w.shape
    tm = _tile(r, 128)

    def body(w_ref, g_ref, m_ref, v_ref, g_out, d_out, m_out, v_out):
        g = g_ref[...]
        g_out[...] = g
        d_out[...], m_out[...], v_out[...] = _adamw_math(w_ref[...], g, m_ref[...], v_ref[...])

    spec = pl.BlockSpec((tm, c), lambda i: (i, 0))
    return pl.pallas_call(
        body, name=name, grid=(r // tm,), in_specs=[spec] * 4, out_specs=[spec] * 4,
        out_shape=[jax.ShapeDtypeStruct((r, c), F32)] * 4, compiler_params=_params("parallel"),
    )(w, g, m, v)


def _adamw_small(gathered, w, m, v):
    rows = w.shape[0]

    def body(all_ref, w_ref, m_ref, v_ref, g_out, d_out, m_out, v_out):
        g = all_ref[0:rows, :]
        for dev in range(1, N_DEV):
            g = g + all_ref[dev * rows:(dev + 1) * rows, :]
        g_out[...] = g
        d_out[...], m_out[...], v_out[...] = _adamw_math(w_ref[...], g, m_ref[...], v_ref[...])

    return pl.pallas_call(
        body, name="adamw_small", out_shape=[jax.ShapeDtypeStruct(w.shape, F32)] * 4, compiler_params=_params(),
    )(gathered, w, m, v)


SMALL = ("g_pre", "w_s", "b_s", "ln_v_g", "ln_v_b", "g_q", "g_k", "rel_bias", "g_out_a", "g_out_b", "g_ple")
LARGE = ("w_in", "w_out", "w_ple_gate", "w_ple_up")
WEIGHTS = ("g_pre", "w_in", "w_s", "b_s", "ln_v_g", "ln_v_b", "g_q", "g_k", "rel_bias", "g_out_a", "g_out_b", "w_out",
           "g_ple", "w_ple_gate", "w_ple_up")


def _pack(parts):
    flat = jnp.concatenate([parts[n].reshape(-1).astype(F32) for n in SMALL])
    rows = -(-flat.shape[0] // (8 * LANE)) * 8
    return jnp.pad(flat, (0, rows * LANE - flat.shape[0])).reshape(rows, LANE)


def _unpack(pack, like):
    flat = pack.reshape(-1)
    out, at = {}, 0
    for n in SMALL:
        size = math.prod(like[n].shape)
        out[n] = flat[at:at + size].reshape(like[n].shape)
        at += size
    return out


def kernel(x, p, g_pre, w_in, w_s, b_s, ln_v_g, ln_v_b, g_q, g_k, rel_bias, g_out_a, g_out_b, w_out, g_ple, w_ple_gate, w_ple_up, loss_target, m_g_pre, m_w_in, m_w_s, m_b_s, m_ln_v_g, m_ln_v_b, m_g_q, m_g_k, m_rel_bias, m_g_out_a, m_g_out_b, m_w_out, m_g_ple, m_w_ple_gate, m_w_ple_up, v_g_pre, v_w_in, v_w_s, v_b_s, v_ln_v_g, v_ln_v_b, v_g_q, v_g_k, v_rel_bias, v_g_out_a, v_g_out_b, v_w_out, v_g_ple, v_w_ple_gate, v_w_ple_up):
    given = dict(locals())
    weights = {n: given[n] for n in WEIGHTS}
    mom_m = {n: given["m_" + n] for n in WEIGHTS}
    mom_v = {n: given["v_" + n] for n in WEIGHTS}
    xs, ps, tgt = x[0], p[0, 0], loss_target[0]
    d = xs.shape[1]

    core = lax.axis_index("c").astype(jnp.int32).reshape(1)
    chip = (2 * lax.axis_index("x") + lax.axis_index("y")).astype(jnp.int32).reshape(1)

    win4, wout4, wgate4, wup4 = [_cast_bf16("cast_" + n, chip, weights[n][0]) for n in LARGE]

    sp = {
        "g_pre": g_pre, "w_s": w_s[0], "b_s": b_s[0], "ln_v_g": ln_v_g, "ln_v_b": ln_v_b, "g_q": g_q, "g_k": g_k,
        "rel_bias": rel_bias, "g_out_a": g_out_a, "g_out_b": g_out_b, "g_ple": g_ple,
    }
    loss_local, grad_x, (sums32, got16), small = _local_step(xs, ps, tgt, sp, win4, wout4, wgate4, wup4,
                                                             dist={"chip": chip, "core": core})
    loss = lax.psum(loss_local, MESH_AXES)

    own = dict(zip(LARGE, sums32))
    got = dict(zip(LARGE, got16))
    halves = [_total("total_" + n, chip, core, own[n], got[n]) for n in LARGE[1:]]
    my_pack = _pack(small)
    device = 2 * chip[0] + core[0]
    everyone = lax.dynamic_update_slice(jnp.zeros((N_DEV * my_pack.shape[0], LANE), F32), my_pack,
                                        (device * my_pack.shape[0], 0))
    half_in, (*joined, gathered) = _total("total_w_in", chip, core, own["w_in"], got["w_in"],
                                          comm=_Both(_JoinHalves(halves), _GatherSmall(everyone)))
    grads = dict(zip(LARGE[1:], joined))
    grads["w_in"], = _alone("join_w_in", _JoinHalves([half_in]))

    out_g, out_d, out_m, out_v = {}, {}, {}, {}
    for n in LARGE:
        results = _adamw("adamw_" + n, weights[n][0], grads[n], mom_m[n][0], mom_v[n][0])
        out_g[n], out_d[n], out_m[n], out_v[n] = [r[None] for r in results]

    pg, pd, pm, pv = _adamw_small(gathered, _pack(weights), _pack(mom_m), _pack(mom_v))
    for packed, out in ((pg, out_g), (pd, out_d), (pm, out_m), (pv, out_v)):
        out.update(_unpack(packed, weights))

    return (loss, grad_x[None], *[out_g[n] for n in WEIGHTS], *[out_d[n] for n in WEIGHTS],
            *[out_m[n] for n in WEIGHTS], *[out_v[n] for n in WEIGHTS])
```

```python
import functools
import math

import numpy as np

import jax
import jax.numpy as jnp
from jax import lax
from jax.experimental import pallas as pl
from jax.experimental.pallas import tpu as pltpu

F32 = jnp.float32
BF16 = jnp.bfloat16

HEAD_DIM = 128
CHUNK = 128
BLK = 128
DILATED = ((128, 1), (512, 4), (2048, 16))
NUM_BUCKETS = 32
MAX_DISTANCE = 2048
ATTN_GROUP = 8
ATTN_GROUP_FWD = 16
EPS = 1e-6
NEG_INF = -1e30
N_CHIPS = 4
N_DEV = 8
MESH_AXES = ("x", "y", "c")

ADAM_LR = 0.001
ADAM_B1 = 0.9
ADAM_B2 = 0.999
ADAM_EPS = 1e-08
ADAM_WD = 0.01
ADAM_STEP = 10

V7X_VMEM_LIMIT = 58 * 1024 * 1024
LANE = 128
MESH = pl.DeviceIdType.MESH
ANY = pl.BlockSpec(memory_space=pl.ANY)


def _params(*sem):
    return pltpu.CompilerParams(dimension_semantics=sem or None, vmem_limit_bytes=V7X_VMEM_LIMIT)


def _tile(n, target):
    if n <= target:
        return n
    t = (target // LANE) * LANE
    while t > LANE and n % t:
        t -= LANE
    assert n % t == 0, (n, target)
    return t


def _gelu(x):
    return 0.5 * x * (1.0 + lax.erf(x * (1.0 / math.sqrt(2.0))))


def _gelu_and_grad(x):
    cdf = 0.5 * (1.0 + lax.erf(x * (1.0 / math.sqrt(2.0))))
    return x * cdf, cdf + x * jnp.exp(-0.5 * x * x) * (1.0 / math.sqrt(2.0 * math.pi))


def _silu_and_grad(x):
    s = jax.nn.sigmoid(x)
    return x * s, s * (1.0 + x * (1.0 - s))


def _dot(a, b, ta=False, tb=False):
    return lax.dot_general(a, b, (((0 if ta else 1,), (1 if tb else 0,)), ((), ())), preferred_element_type=F32)


def _colsum8(v):
    return jnp.sum(v.reshape(v.shape[0] // 8, 8, v.shape[1]), axis=0)


def _matmul(name, a, b, a_spec, b_spec, grid, acc_shape, out_shape, out_specs, ta=False, tb=False,
            extras=(), extra_specs=(), epilogue=None, comm=None, prefetch=()):
    nk = grid[2]
    n_pre = len(prefetch)
    n_extra = len(extras)
    single = not isinstance(out_shape, (tuple, list))
    outs_shape = (out_shape,) if single else tuple(out_shape)
    outs_specs = (out_specs,) if single else tuple(out_specs)
    n_out = len(outs_shape)
    c_ins = list(comm.ins) if comm else []
    c_outs = list(comm.out_shape) if comm else []
    c_scratch = list(comm.scratch) if comm else []
    n_cin, n_cout = len(c_ins), len(c_outs)

    def finish(acc, extra_refs, out_refs):
        if epilogue is None:
            out_refs[0][...] = acc.astype(out_refs[0].dtype)
        else:
            epilogue(acc, extra_refs, out_refs)

    def body(*refs):
        a_ref, b_ref, *rest = refs[n_pre:]
        extra_refs = rest[:n_extra]
        cin_refs = rest[n_extra:n_extra + n_cin]
        out_refs = rest[n_extra + n_cin:n_extra + n_cin + n_out]
        cout_refs = rest[n_extra + n_cin + n_out:n_extra + n_cin + n_out + n_cout]
        scratch_refs = rest[n_extra + n_cin + n_out + n_cout:]
        ids = [pl.program_id(ax) for ax in range(3)]
        if comm:
            sems = scratch_refs[len(scratch_refs) - len(c_scratch):]

            @pl.when((ids[0] == 0) & (ids[1] == 0) & (ids[2] == 0))
            def _():
                comm.start(cin_refs, cout_refs, sems)

        if nk == 1:
            finish(_dot(a_ref[...], b_ref[...], ta, tb), extra_refs, out_refs)
        else:
            acc_ref = scratch_refs[0]

            @pl.when(ids[2] == 0)
            def _():
                acc_ref[...] = jnp.zeros_like(acc_ref)

            acc_ref[...] += _dot(a_ref[...], b_ref[...], ta, tb)

            @pl.when(ids[2] == nk - 1)
            def _():
                finish(acc_ref[...], extra_refs, out_refs)

        if comm:
            @pl.when((ids[0] == grid[0] - 1) & (ids[1] == grid[1] - 1) & (ids[2] == nk - 1))
            def _():
                comm.finish(cin_refs, cout_refs, sems)

    scratch = ([] if nk == 1 else [pltpu.VMEM(acc_shape, F32)]) + c_scratch
    aliases = {n_pre + 2 + n_extra + i: n_out + i for i in range(n_cin)} if (comm and comm.in_place) else {}
    res = pl.pallas_call(
        body, name=name,
        grid_spec=pltpu.PrefetchScalarGridSpec(
            num_scalar_prefetch=n_pre, grid=grid,
            in_specs=[a_spec, b_spec, *extra_specs] + [ANY] * n_cin,
            out_specs=list(outs_specs) + [ANY] * n_cout, scratch_shapes=scratch),
        out_shape=list(outs_shape) + c_outs,
        input_output_aliases=aliases,
        compiler_params=_params(*(("arbitrary",) * 3 if comm else ("parallel", "parallel", "arbitrary"))),
    )(*prefetch, a, b, *extras, *c_ins)
    if comm:
        main = res[:n_out]
        return (main[0] if single else main), res[n_out:]
    return res[0] if single else res


def _shard_spec(rows, cols, per_shard, row_axis, col_axis):
    def index(i, j, k):
        g = (i, j, k)
        return (g[col_axis] // per_shard, g[row_axis], g[col_axis] % per_shard)
    return pl.BlockSpec((None, rows, cols), index)


ROW_CHUNK = 512


def _column_chunks(width):
    chunk = _tile(width, ROW_CHUNK)
    return (lambda c: slice(c * chunk, (c + 1) * chunk)), width // chunk


def _rms_fwd(name, x, g):
    s, d = x.shape
    tm = _tile(s, 256)

    cols, n_chunks = _column_chunks(d)

    def body(x_ref, g_ref, y_ref, r_ref):
        total = jnp.zeros((tm, 1), F32)
        for c in range(n_chunks):
            xc = x_ref[:, cols(c)]
            total = total + jnp.sum(xc * xc, axis=-1, keepdims=True)
        r = lax.rsqrt(total * (1.0 / d) + EPS)
        r_ref[...] = r
        for c in range(n_chunks):
            y_ref[:, cols(c)] = (x_ref[:, cols(c)] * r * g_ref[:, cols(c)]).astype(BF16)

    return pl.pallas_call(
        body, name=name, grid=(s // tm,),
        in_specs=[pl.BlockSpec((tm, d), lambda i: (i, 0)), pl.BlockSpec((1, d), lambda i: (0, 0))],
        out_specs=(pl.BlockSpec((tm, d), lambda i: (i, 0)), pl.BlockSpec((tm, 1), lambda i: (i, 0))),
        out_shape=(jax.ShapeDtypeStruct((s, d), BF16), jax.ShapeDtypeStruct((s, 1), F32)),
        compiler_params=_params("parallel"),
    )(x, g)


def _rms_bwd(name, dy, x, r, g, skip, with_bf16):
    s, d = x.shape
    tm = _tile(s, 256)
    n = s // tm
    cols, n_chunks = _column_chunks(d)

    def body(dy_ref, x_ref, r_ref, g_ref, skip_ref, *outs):
        dx_ref = outs[0]
        dg_ref = outs[-2]
        acc_ref = outs[-1]
        i = pl.program_id(0)
        rv = r_ref[...]

        @pl.when(i == 0)
        def _():
            acc_ref[...] = jnp.zeros_like(acc_ref)

        total = jnp.zeros((tm, 1), F32)
        for c in range(n_chunks):
            dyc, xc = dy_ref[:, cols(c)].astype(F32), x_ref[:, cols(c)]
            acc_ref[:, cols(c)] += _colsum8(dyc * xc * rv)
            total = total + jnp.sum(dyc * g_ref[:, cols(c)] * xc, axis=-1, keepdims=True)
        coef = (rv * rv) * (total * (1.0 / d))
        for c in range(n_chunks):
            dyc, xc = dy_ref[:, cols(c)].astype(F32), x_ref[:, cols(c)]
            dx = skip_ref[:, cols(c)].astype(F32) + rv * (dyc * g_ref[:, cols(c)] - xc * coef)
            dx_ref[:, cols(c)] = dx
            if with_bf16:
                outs[1][:, cols(c)] = dx.astype(BF16)

        @pl.when(i == n - 1)
        def _():
            dg_ref[...] = jnp.sum(acc_ref[...], axis=0, keepdims=True)

    row = pl.BlockSpec((tm, d), lambda i: (i, 0))
    vec = pl.BlockSpec((1, d), lambda i: (0, 0))
    out_specs = [row] + ([row] if with_bf16 else []) + [vec]
    out_shape = [jax.ShapeDtypeStruct((s, d), F32)] + ([jax.ShapeDtypeStruct((s, d), BF16)] if with_bf16 else []) \
        + [jax.ShapeDtypeStruct((1, d), F32)]
    return pl.pallas_call(
        body, name=name, grid=(n,),
        in_specs=[row, row, pl.BlockSpec((tm, 1), lambda i: (i, 0)), vec, row],
        out_specs=out_specs, out_shape=out_shape, scratch_shapes=[pltpu.VMEM((8, d), F32)],
        compiler_params=_params("arbitrary"),
    )(dy, x, r, g, skip)


def _causal(w):
    t = lax.broadcasted_iota(jnp.int32, w.shape, 0)
    s_ = lax.broadcasted_iota(jnp.int32, w.shape, 1)
    return jnp.where(t >= s_, w, 0.0)


def _gmlp_fwd(proj, w_s, b_st, ln_g, ln_b, g_out, d_model):
    s = proj.shape[0]
    wa = d_model // 2
    groups = wa // HEAD_DIM
    tm = _tile(s, 256)
    n_chunks = tm // CHUNK

    def body(au_ref, av_ref, az_ref, ws_ref, bst_ref, lng_ref, lnb_ref, go_ref,
             y_ref, z_ref, vn_ref, mu_ref, rs_ref, ra_ref):
        gv = _gelu(av_ref[...])
        mu = jnp.mean(gv, axis=-1, keepdims=True)
        xc = gv - mu
        rs = lax.rsqrt(jnp.mean(xc * xc, axis=-1, keepdims=True) + EPS)
        vn = (xc * rs * lng_ref[...] + lnb_ref[...]).astype(BF16)
        vn_ref[...] = vn
        mu_ref[...] = mu
        rs_ref[...] = rs
        for g in range(groups):
            wm = _causal(ws_ref[g]).astype(BF16)
            cols = slice(g * HEAD_DIM, (g + 1) * HEAD_DIM)
            for ch in range(n_chunks):
                rows = slice(ch * CHUNK, (ch + 1) * CHUNK)
                z_ref[rows, cols] = _dot(wm, vn_ref[rows, cols]) + bst_ref[:, g:g + 1]
        ya = _gelu(au_ref[...]) * z_ref[...]
        ra = lax.rsqrt(jnp.mean(ya * ya, axis=-1, keepdims=True) + EPS)
        ra_ref[...] = ra
        sz, _ = _silu_and_grad(az_ref[...])
        y_ref[...] = (ya * ra * go_ref[...] * sz).astype(BF16)

    def col(j):
        return pl.BlockSpec((tm, wa), lambda i: (i, j))
    vec = pl.BlockSpec((1, wa), lambda i: (0, 0))
    stat = pl.BlockSpec((tm, 1), lambda i: (i, 0))
    return pl.pallas_call(
        body, name="gmlp_fwd", grid=(s // tm,),
        in_specs=[col(0), col(1), col(2),
                  pl.BlockSpec((groups, CHUNK, CHUNK), lambda i: (0, 0, 0)),
                  pl.BlockSpec((CHUNK, groups), lambda i: (0, 0)), vec, vec, vec],
        out_specs=(col(0), col(0), col(0), stat, stat, stat),
        out_shape=(jax.ShapeDtypeStruct((s, d_model), BF16), jax.ShapeDtypeStruct((s, wa), F32),
                   jax.ShapeDtypeStruct((s, wa), BF16), jax.ShapeDtypeStruct((s, 1), F32),
                   jax.ShapeDtypeStruct((s, 1), F32), jax.ShapeDtypeStruct((s, 1), F32)),
        compiler_params=_params("parallel"),
    )(proj, proj, proj, w_s, b_st, ln_g, ln_b, g_out)


def _gmlp_bwd(dproj, dy, proj, z, vn, mu, rs, ra, w_s, ln_g, g_out, d_model):
    s = proj.shape[0]
    wa = d_model // 2
    groups = wa // HEAD_DIM
    tm = _tile(s, 256)
    n_chunks = tm // CHUNK
    n = s // tm

    def causal_stack(w):
        t = lax.broadcasted_iota(jnp.int32, w.shape, 1)
        s_ = lax.broadcasted_iota(jnp.int32, w.shape, 2)
        return jnp.where(t >= s_, w, 0.0)

    def body(dproj_in, dy_ref, au_ref, av_ref, az_ref, z_ref, vn_ref, mu_ref, rs_ref, ra_ref, ws_ref, lng_ref, go_ref,
             dp_ref, gws_ref, dzs_ref, glg_ref, glb_ref, ggo_ref,
             dz_s, dvn_s, acc_lg, acc_lb, acc_go):
        del dproj_in
        i = pl.program_id(0)

        @pl.when(i == 0)
        def _():
            gws_ref[...] = jnp.zeros_like(gws_ref)
            dzs_ref[...] = jnp.zeros_like(dzs_ref)
            acc_lg[...] = jnp.zeros_like(acc_lg)
            acc_lb[...] = jnp.zeros_like(acc_lb)
            acc_go[...] = jnp.zeros_like(acc_go)

        au, az, zv, rav = au_ref[...], az_ref[...], z_ref[...], ra_ref[...]
        u, du = _gelu_and_grad(au)
        ya = u * zv
        sz, dsz = _silu_and_grad(az)
        dyv = dy_ref[...].astype(F32)
        dp_ref[:, 2 * wa:3 * wa] = (dyv * (ya * rav * go_ref[...]) * dsz).astype(BF16)
        dn = dyv * sz
        acc_go[...] += _colsum8(dn * ya * rav)
        dyg = dn * go_ref[...]
        dya = rav * (dyg - ya * (rav * rav) * jnp.mean(dyg * ya, axis=-1, keepdims=True))
        dp_ref[:, 0:wa] = (dya * zv * du).astype(BF16)
        dz_s[...] = dya * u
        for ch in range(n_chunks):
            dzs_ref[...] += dz_s[ch * CHUNK:(ch + 1) * CHUNK, :]
        for g in range(groups):
            wm = _causal(ws_ref[g]).astype(BF16)
            cols = slice(g * HEAD_DIM, (g + 1) * HEAD_DIM)
            for ch in range(n_chunks):
                rows = slice(ch * CHUNK, (ch + 1) * CHUNK)
                dzb = dz_s[rows, cols].astype(BF16)
                gws_ref[g] += _dot(dzb, vn_ref[rows, cols], tb=True)
                dvn_s[rows, cols] = _dot(wm, dzb, ta=True)
        av = av_ref[...]
        gv, dgelu_v = _gelu_and_grad(av)
        xh = (gv - mu_ref[...]) * rs_ref[...]
        dvn = dvn_s[...]
        acc_lb[...] += _colsum8(dvn)
        acc_lg[...] += _colsum8(dvn * xh)
        dxh = dvn * lng_ref[...]
        dgv = rs_ref[...] * (dxh - jnp.mean(dxh, axis=-1, keepdims=True) - xh * jnp.mean(dxh * xh, axis=-1, keepdims=True))
        dp_ref[:, wa:2 * wa] = (dgv * dgelu_v).astype(BF16)

        @pl.when(i == n - 1)
        def _():
            gws_ref[...] = causal_stack(gws_ref[...])
            glg_ref[...] = jnp.sum(acc_lg[...], axis=0, keepdims=True)
            glb_ref[...] = jnp.sum(acc_lb[...], axis=0, keepdims=True)
            ggo_ref[...] = jnp.sum(acc_go[...], axis=0, keepdims=True)

    def col(j):
        return pl.BlockSpec((tm, wa), lambda i: (i, j))
    vec = pl.BlockSpec((1, wa), lambda i: (0, 0))
    stat = pl.BlockSpec((tm, 1), lambda i: (i, 0))
    ws_spec = pl.BlockSpec((groups, CHUNK, CHUNK), lambda i: (0, 0, 0))
    d_in = dproj.shape[1]
    return pl.pallas_call(
        body, name="gmlp_bwd", grid=(n,),
        in_specs=[ANY, col(0), col(0), col(1), col(2), col(0), col(0), stat, stat, stat, ws_spec, vec, vec],
        out_specs=(pl.BlockSpec((tm, 3 * wa), lambda i: (i, 0)), ws_spec,
                   pl.BlockSpec((CHUNK, wa), lambda i: (0, 0)), vec, vec, vec),
        out_shape=(jax.ShapeDtypeStruct((s, d_in), BF16),
                   jax.ShapeDtypeStruct((groups, CHUNK, CHUNK), F32), jax.ShapeDtypeStruct((CHUNK, wa), F32))
        + (jax.ShapeDtypeStruct((1, wa), F32),) * 3,
        scratch_shapes=[pltpu.VMEM((tm, wa), F32), pltpu.VMEM((tm, wa), F32)] + [pltpu.VMEM((8, wa), F32)] * 3,
        input_output_aliases={0: 0},
        compiler_params=_params("arbitrary"),
    )(dproj, dy, proj, proj, proj, z, vn, mu, rs, ra, w_s, ln_g, g_out)


def _gate_fwd(y_in, yb, proj, g_out, d_model):
    s = proj.shape[0]
    wa = d_model // 2
    tm = _tile(s, 256)

    cols, n_chunks = _column_chunks(wa)

    def body(y_any, yb_ref, bz_ref, go_ref, y_ref, rb_ref):
        del y_any
        total = jnp.zeros((tm, 1), F32)
        for c in range(n_chunks):
            ybc = yb_ref[:, cols(c)]
            total = total + jnp.sum(ybc * ybc, axis=-1, keepdims=True)
        rb = lax.rsqrt(total * (1.0 / wa) + EPS)
        rb_ref[...] = rb
        for c in range(n_chunks):
            sz, _ = _silu_and_grad(bz_ref[:, cols(c)])
            y_ref[:, cols(c)] = (yb_ref[:, cols(c)] * rb * go_ref[:, cols(c)] * sz).astype(BF16)

    return pl.pallas_call(
        body, name="gate_b_fwd", grid=(s // tm,),
        in_specs=[ANY, pl.BlockSpec((tm, wa), lambda i: (i, 0)), pl.BlockSpec((tm, wa), lambda i: (i, 6)),
                  pl.BlockSpec((1, wa), lambda i: (0, 0))],
        out_specs=(pl.BlockSpec((tm, wa), lambda i: (i, 1)), pl.BlockSpec((tm, 1), lambda i: (i, 0))),
        out_shape=(jax.ShapeDtypeStruct((s, d_model), BF16), jax.ShapeDtypeStruct((s, 1), F32)),
        input_output_aliases={0: 0},
        compiler_params=_params("parallel"),
    )(y_in, yb, proj, g_out)


def _gate_bwd(dy, yb, lse, proj, rb, g_out, d_model):
    s, d_in = proj.shape
    wa = d_model // 2
    heads = wa // HEAD_DIM
    tm = _tile(s, 256)
    n = s // tm
    cols, n_chunks = _column_chunks(wa)

    def body(dy_ref, yb_ref, lse_ref, bz_ref, rb_ref, go_ref, dp_ref, do_ref, st_ref, ggo_ref, acc):
        i = pl.program_id(0)

        @pl.when(i == 0)
        def _():
            acc[...] = jnp.zeros_like(acc)

        rbv = rb_ref[...]
        total = jnp.zeros((tm, 1), F32)
        for c in range(n_chunks):
            dyc, ybc = dy_ref[:, cols(c)].astype(F32), yb_ref[:, cols(c)]
            sz, dsz = _silu_and_grad(bz_ref[:, cols(c)])
            dp_ref[:, cols(c)] = (dyc * (ybc * rbv * go_ref[:, cols(c)]) * dsz).astype(BF16)
            dn = dyc * sz
            acc[:, cols(c)] += _colsum8(dn * ybc * rbv)
            total = total + jnp.sum(dn * go_ref[:, cols(c)] * ybc, axis=-1, keepdims=True)
        coef = (rbv * rbv) * (total * (1.0 / wa))
        first = lax.broadcasted_iota(jnp.int32, (tm, 2), 1) == 0
        per_chunk = (wa // n_chunks) // HEAD_DIM
        for c in range(n_chunks):
            dyc, ybc = dy_ref[:, cols(c)].astype(F32), yb_ref[:, cols(c)]
            sz, _ = _silu_and_grad(bz_ref[:, cols(c)])
            do = rbv * (dyc * sz * go_ref[:, cols(c)] - ybc * coef)
            do_ref[:, cols(c)] = do
            prod = do * ybc
            for k in range(per_chunk):
                h = c * per_chunk + k
                delta = jnp.sum(prod[:, k * HEAD_DIM:(k + 1) * HEAD_DIM], axis=-1, keepdims=True)
                st_ref[h] = jnp.where(first, lse_ref[h], delta)

        @pl.when(i == n - 1)
        def _():
            ggo_ref[...] = jnp.sum(acc[...], axis=0, keepdims=True)

    vec = pl.BlockSpec((1, wa), lambda i: (0, 0))
    return pl.pallas_call(
        body, name="gate_b_bwd", grid=(n,),
        in_specs=[pl.BlockSpec((tm, wa), lambda i: (i, 1)), pl.BlockSpec((tm, wa), lambda i: (i, 0)),
                  pl.BlockSpec((heads, tm, 1), lambda i: (0, i, 0)),
                  pl.BlockSpec((tm, wa), lambda i: (i, 6)), pl.BlockSpec((tm, 1), lambda i: (i, 0)), vec],
        out_specs=(pl.BlockSpec((tm, wa), lambda i: (i, 6)), pl.BlockSpec((tm, wa), lambda i: (i, 0)),
                   pl.BlockSpec((heads, tm, 2), lambda i: (0, i, 0)), vec),
        out_shape=(jax.ShapeDtypeStruct((s, d_in), BF16), jax.ShapeDtypeStruct((s, wa), F32),
                   jax.ShapeDtypeStruct((heads, s, 2), F32), jax.ShapeDtypeStruct((1, wa), F32)),
        scratch_shapes=[pltpu.VMEM((8, wa), F32)],
        compiler_params=_params("arbitrary"),
    )(dy, yb, lse, proj, rb, g_out)


def _bucket_tables():
    qi = BLK + np.arange(BLK)
    kj = np.arange(2 * BLK)
    delta = qi[:, None] - kj[None, :]
    max_exact = NUM_BUCKETS // 2
    tabs = []
    for window, dil in DILATED:
        band = (delta >= 0) & (delta <= window // dil)
        dist = np.clip(delta, 0, None) * dil
        d = np.maximum(dist, 1).astype(np.float32)
        large = max_exact + (np.log(d / np.float32(max_exact)) / np.float32(math.log(MAX_DISTANCE / max_exact))
                             * np.float32(NUM_BUCKETS - max_exact)).astype(np.int32)
        large = np.minimum(large, NUM_BUCKETS - 1)
        tabs.append(np.where(band, np.where(dist < max_exact, dist, large), -1).astype(np.int32))
    return np.stack(tabs)


def _bias_tiles(tab_ref, rb_ref, h, bias_s):
    col = lax.broadcasted_iota(jnp.int32, (BLK, 2 * BLK), 1)
    for i in range(len(DILATED)):
        t = tab_ref[i]
        bias = jnp.zeros(t.shape, F32)
        for b in range(NUM_BUCKETS):
            bias = jnp.where(t == b, rb_ref[b, h], bias)
        bias = jnp.where(t >= 0, bias, NEG_INF)
        bias_s[2 * i] = jnp.where(col >= BLK, bias, NEG_INF)
        bias_s[2 * i + 1] = bias


def _block_rows(b, n_blocks, dil):
    nb = n_blocks // dil
    r = b // nb
    n = b % nb
    start = r + dil * BLK * n
    if dil == 1:
        return pl.ds(pl.multiple_of(start, BLK), BLK), n
    return pl.ds(start, BLK, stride=dil), n


def _sub_block(b, rows=BLK, pad=0):
    return pl.ds(pl.multiple_of(b * BLK + pad, BLK), rows)


def _regroup(b, n_blocks, small, big):
    ratio = big // small
    piece = BLK // ratio
    nb_small, nb_big = n_blocks // small, n_blocks // big
    r, n = b // nb_small, b % nb_small
    for q in range(ratio):
        block = (r + small * q) * nb_big + n // ratio
        yield q, ratio, piece, pl.multiple_of(block * BLK + piece * (n % ratio), piece)


def _rms_rows(x, gain):
    return x * lax.rsqrt(jnp.mean(x * x, axis=-1, keepdims=True) + EPS) * gain


def _attn_fwd(proj, g_q, g_k, rel_bias, d_model, comm=None):
    s = proj.shape[0]
    heads = d_model // 2 // HEAD_DIM
    n_blocks = s // BLK
    scale = HEAD_DIM ** -0.5
    tables = jnp.asarray(_bucket_tables())
    c_ins = list(comm.ins) if comm else []
    c_outs = list(comm.out_shape) if comm else []
    c_scratch = list(comm.scratch) if comm else []

    def body(q_ref, k_ref, v_ref, gq_ref, gk_ref, rb_ref, tab_ref, *rest):
        cin_refs, rest = rest[:len(c_ins)], rest[len(c_ins):]
        yb_ref, lse_ref = rest[:2]
        cout_refs, rest = rest[2:2 + len(c_outs)], rest[2 + len(c_outs):]
        qs, ks, vs, qs_w, ks_w, vs_w, tmp, m_s, l_s, acc_s, bias_s, sc_s, p_s = rest[:13]
        sems = rest[13:]
        h = pl.program_id(0)
        if comm:
            @pl.when(h == 0)
            def _():
                comm.start(cin_refs, cout_refs, sems)

        _bias_tiles(tab_ref, rb_ref, h, bias_s)
        gq = gq_ref[...] * scale
        gk = gk_ref[...]
        for buf in (ks, vs, ks_w, vs_w):
            buf[0:BLK, :] = jnp.zeros((BLK, HEAD_DIM), BF16)

        for i, (_, dil) in enumerate(DILATED):
            def prepare(b, carry, i=i, dil=dil):
                rows, _ = _block_rows(b, n_blocks, dil)
                vals = (_rms_rows(q_ref[rows, :], gq), _rms_rows(k_ref[rows, :], gk), v_ref[rows, :])
                for val, buf, pad in zip(vals, (qs, ks, vs), (0, BLK, BLK)):
                    buf[_sub_block(b, pad=pad), :] = val.astype(BF16)
                if i == 1:
                    slot = b % 2
                    for a, val in enumerate(vals):
                        tmp[slot, a] = val
                    for q, ratio, piece, dst in _regroup(b, n_blocks, dil, DILATED[2][1]):
                        for a, (buf, pad) in enumerate(((qs_w, 0), (ks_w, BLK), (vs_w, BLK))):
                            buf[pl.ds(dst + pad, piece), :] = tmp[slot, a, pl.ds(q, piece, stride=ratio), :].astype(BF16)
                return carry
            if i < 2:
                lax.fori_loop(0, n_blocks, prepare, 0, unroll=8)
            qs, ks, vs = (qs, ks, vs) if i < 2 else (qs_w, ks_w, vs_w)

            def group(g, carry, i=i, dil=dil, qs=qs, ks=ks, vs=vs):
                blocks = [g * ATTN_GROUP_FWD + t for t in range(ATTN_GROUP_FWD)]
                for t, b in enumerate(blocks):
                    sc_s[t] = _dot(qs[_sub_block(b), :], ks[_sub_block(b, 2 * BLK), :], tb=True)
                for t, b in enumerate(blocks):
                    rows, n = _block_rows(b, n_blocks, dil)
                    sc = sc_s[t] + bias_s[2 * i + jnp.minimum(n, 1)]
                    m_i = jnp.max(sc, axis=-1, keepdims=True)
                    pr = jnp.exp(sc - m_i)
                    l_i = jnp.sum(pr, axis=-1, keepdims=True)
                    p_s[t] = pr.astype(BF16)
                    m_s[new, rows, :] = m_i
                    l_s[new, rows, :] = l_i
                for t, b in enumerate(blocks):
                    rows, _ = _block_rows(b, n_blocks, dil)
                    acc_s[new, rows, :] = _dot(p_s[t], vs[_sub_block(b, 2 * BLK), :])
                return carry
            new = min(i, 1)
            lax.fori_loop(0, n_blocks // ATTN_GROUP_FWD, group, 0)
            if i > 0:
                m = jnp.maximum(m_s[0], m_s[1])
                w_old, w_new = jnp.exp(m_s[0] - m), jnp.exp(m_s[1] - m)
                m_s[0] = m
                l_s[0] = l_s[0] * w_old + l_s[1] * w_new
                acc_s[0] = acc_s[0] * w_old + acc_s[1] * w_new

        l = l_s[0]
        yb_ref[...] = acc_s[0] / l
        lse_ref[...] = m_s[0] + jnp.log(l)
        if comm:
            @pl.when(h == heads - 1)
            def _():
                comm.finish(cin_refs, cout_refs, sems)

    def head_col(off):
        return pl.BlockSpec((s, HEAD_DIM), lambda h: (0, off * heads + h))
    vec = pl.BlockSpec((1, HEAD_DIM), lambda h: (0, 0))
    res = pl.pallas_call(
        body, name="attn_fwd", grid=(heads,),
        in_specs=[head_col(3), head_col(4), head_col(5), vec, vec,
                  pl.BlockSpec(memory_space=pltpu.SMEM),
                  pl.BlockSpec((len(DILATED), BLK, 2 * BLK), lambda h: (0, 0, 0))] + [ANY] * len(c_ins),
        out_specs=[pl.BlockSpec((s, HEAD_DIM), lambda h: (0, h)), pl.BlockSpec((None, s, 1), lambda h: (h, 0, 0))]
        + [ANY] * len(c_outs),
        out_shape=[jax.ShapeDtypeStruct((s, heads * HEAD_DIM), F32), jax.ShapeDtypeStruct((heads, s, 1), F32)] + c_outs,
        scratch_shapes=[pltpu.VMEM((s, HEAD_DIM), BF16), pltpu.VMEM((s + BLK, HEAD_DIM), BF16),
                        pltpu.VMEM((s + BLK, HEAD_DIM), BF16),
                        pltpu.VMEM((s, HEAD_DIM), BF16), pltpu.VMEM((s + BLK, HEAD_DIM), BF16),
                        pltpu.VMEM((s + BLK, HEAD_DIM), BF16), pltpu.VMEM((2, 3, BLK, HEAD_DIM), F32),
                        pltpu.VMEM((2, s, 1), F32), pltpu.VMEM((2, s, 1), F32), pltpu.VMEM((2, s, HEAD_DIM), F32),
                        pltpu.VMEM((2 * len(DILATED), BLK, 2 * BLK), F32),
                        pltpu.VMEM((ATTN_GROUP_FWD, BLK, 2 * BLK), F32),
                        pltpu.VMEM((ATTN_GROUP_FWD, BLK, 2 * BLK), BF16)] + c_scratch,
        input_output_aliases={7 + i: 2 + i for i in range(len(c_ins))} if (comm and comm.in_place) else {},
        compiler_params=_params("arbitrary"),
    )(proj, proj, proj, g_q, g_k, rel_bias, tables, *c_ins)
    return res[0], res[1], res[2:]


def _attn_bwd(dproj, d_o, stats, proj, g_q, g_k, rel_bias, d_model, comm=None):
    s, d_in = proj.shape
    heads = d_model // 2 // HEAD_DIM
    n_blocks = s // BLK
    scale = HEAD_DIM ** -0.5
    tables = jnp.asarray(_bucket_tables())
    n_dil = len(DILATED)
    c_ins = list(comm.ins) if comm else []
    c_outs = list(comm.out_shape) if comm else []
    c_scratch = list(comm.scratch) if comm else []

    def body(dp_any, q_ref, k_ref, v_ref, do_ref, st_ref, gq_ref, gk_ref, rb_ref, tab_ref, *rest):
        cin_refs, rest = rest[:len(c_ins)], rest[len(c_ins):]
        dp_out, ggq_ref, ggk_ref, db_ref = rest[:4]
        cout_refs, rest = rest[4:4 + len(c_outs)], rest[4 + len(c_outs):]
        (qs, ks, vs, dos, st_s, qs_w, ks_w, vs_w, dos_w, st_w, tmp, tmp_st,
         dqn, dkn, dvv, dq_u, dk_u, dv_u, bias_s, dbias_s, sc_s, dp_s, p_s, ds_s, sems) = rest[:25]
        comm_sems = rest[25:]
        del dp_any
        h = pl.program_id(0)
        if comm:
            @pl.when(h == 0)
            def _():
                comm.start(cin_refs, cout_refs, comm_sems)


        @pl.when(h == 0)
        def _():
            ggq_ref[...] = jnp.zeros_like(ggq_ref)
            ggk_ref[...] = jnp.zeros_like(ggk_ref)

        dbias_s[...] = jnp.zeros_like(dbias_s)
        _bias_tiles(tab_ref, rb_ref, h, bias_s)
        gq = gq_ref[...] * scale
        gk = gk_ref[...]
        for buf in (ks, vs, ks_w, vs_w):
            buf[0:BLK, :] = jnp.zeros((BLK, HEAD_DIM), BF16)

        for i, (_, dil) in enumerate(DILATED):
            def prepare(b, carry, i=i, dil=dil):
                rows, _ = _block_rows(b, n_blocks, dil)
                vals = (_rms_rows(q_ref[rows, :], gq), _rms_rows(k_ref[rows, :], gk), v_ref[rows, :], do_ref[rows, :])
                for val, buf, pad in zip(vals, (qs, ks, vs, dos), (0, BLK, BLK, 0)):
                    buf[_sub_block(b, pad=pad), :] = val.astype(BF16)
                st = st_ref[rows, :]
                st_s[_sub_block(b), :] = st
                if i == 1:
                    slot = b % 2
                    for a, val in enumerate(vals):
                        tmp[slot, a] = val
                    tmp_st[slot] = st
                    for q, ratio, piece, dst in _regroup(b, n_blocks, dil, DILATED[2][1]):
                        for a, (buf, pad) in enumerate(((qs_w, 0), (ks_w, BLK), (vs_w, BLK), (dos_w, 0))):
                            buf[pl.ds(dst + pad, piece), :] = tmp[slot, a, pl.ds(q, piece, stride=ratio), :].astype(BF16)
                        st_w[pl.ds(dst, piece), :] = tmp_st[slot, pl.ds(q, piece, stride=ratio), :]
                return carry
            if i < 2:
                lax.fori_loop(0, n_blocks, prepare, 0, unroll=8)
            dk_u[0:BLK, :] = jnp.zeros((BLK, HEAD_DIM), F32)
            dv_u[0:BLK, :] = jnp.zeros((BLK, HEAD_DIM), F32)
            operands = (qs, ks, vs, dos, st_s) if i < 2 else (qs_w, ks_w, vs_w, dos_w, st_w)

            def group(g, carry, i=i, dil=dil, operands=operands):
                qs, ks, vs, dos, st_s = operands
                blocks = [g * ATTN_GROUP + t for t in range(ATTN_GROUP)]
                for t, b in enumerate(blocks):
                    sc_s[t] = _dot(qs[_sub_block(b), :], ks[_sub_block(b, 2 * BLK), :], tb=True)
                    dp_s[t] = _dot(dos[_sub_block(b), :], vs[_sub_block(b, 2 * BLK), :], tb=True)
                for t, b in enumerate(blocks):
                    _, n = _block_rows(b, n_blocks, dil)
                    st = st_s[_sub_block(b), :]
                    pr = jnp.exp(sc_s[t] + bias_s[2 * i + jnp.minimum(n, 1)] - st[:, 0:1])
                    ds = pr * (dp_s[t] - st[:, 1:2])
                    dbias_s[i] += ds
                    p_s[t] = pr.astype(BF16)
                    ds_s[t] = ds.astype(BF16)
                for t, b in enumerate(blocks):
                    dq_u[_sub_block(b), :] = _dot(ds_s[t], ks[_sub_block(b, 2 * BLK), :])
                    for part, lhs, rhs in ((dk_u, ds_s, qs), (dv_u, p_s, dos)):
                        both = _dot(lhs[t], rhs[_sub_block(b), :], ta=True)
                        part[_sub_block(b), :] += both[:BLK]
                        part[_sub_block(b, pad=BLK), :] = both[BLK:]
                return carry
            lax.fori_loop(0, n_blocks // ATTN_GROUP, group, 0)

            def scatter(b, carry, i=i, dil=dil):
                if i < 2:
                    rows, _ = _block_rows(b, n_blocks, dil)
                    for acc, part, pad in ((dqn, dq_u, 0), (dkn, dk_u, BLK), (dvv, dv_u, BLK)):
                        val = part[_sub_block(b, pad=pad), :]
                        acc[rows, :] = val if i == 0 else acc[rows, :] + val
                else:
                    middle = DILATED[1][1]
                    rows, _ = _block_rows(b, n_blocks, middle)
                    slot = b % 2
                    for a, (acc, part, pad) in enumerate(((dqn, dq_u, 0), (dkn, dk_u, BLK), (dvv, dv_u, BLK))):
                        for q, ratio, piece, src in _regroup(b, n_blocks, middle, dil):
                            tmp[slot, a, pl.ds(q, piece, stride=ratio), :] = part[pl.ds(src + pad, piece), :]
                        acc[rows, :] += tmp[slot, a]
                return carry
            lax.fori_loop(0, n_blocks, scatter, 0, unroll=8)

        q = q_ref[...]
        rq = lax.rsqrt(jnp.mean(q * q, axis=-1, keepdims=True) + EPS)
        k = k_ref[...]
        rk = lax.rsqrt(jnp.mean(k * k, axis=-1, keepdims=True) + EPS)
        dq_n = dqn[...]
        ggq_ref[...] += jnp.sum(dq_n * (q * rq) * scale, axis=0, keepdims=True)
        dg = dq_n * gq_ref[...] * scale
        qs[...] = (rq * (dg - q * (rq * rq) * jnp.mean(dg * q, axis=-1, keepdims=True))).astype(BF16)
        dk_n = dkn[...]
        ggk_ref[...] += jnp.sum(dk_n * (k * rk), axis=0, keepdims=True)
        dg = dk_n * gk_ref[...]
        dos[...] = (rk * (dg - k * (rk * rk) * jnp.mean(dg * k, axis=-1, keepdims=True))).astype(BF16)
        vs[BLK:, :] = dvv[...].astype(BF16)
        copies = [pltpu.make_async_copy(src, dp_out.at[:, pl.ds(pl.multiple_of(((3 + j) * heads + h) * HEAD_DIM, HEAD_DIM), HEAD_DIM)],
                                        sems.at[j]) for j, src in enumerate((qs, dos, vs.at[pl.ds(BLK, s)]))]
        for cp in copies:
            cp.start()
        for b in range(NUM_BUCKETS):
            tot = jnp.zeros((BLK, 2 * BLK), F32)
            for i in range(n_dil):
                tot = tot + jnp.where(tab_ref[i] == b, dbias_s[i], 0.0)
            db_ref[b:b + 1, :] = jnp.full((1, LANE), jnp.sum(tot), F32)
        for cp in copies:
            cp.wait()
        if comm:
            @pl.when(h == heads - 1)
            def _():
                comm.finish(cin_refs, cout_refs, comm_sems)

    def head_col(off):
        return pl.BlockSpec((s, HEAD_DIM), lambda h: (0, off * heads + h))
    own_col = pl.BlockSpec((s, HEAD_DIM), lambda h: (0, h))
    vec = pl.BlockSpec((1, HEAD_DIM), lambda h: (0, 0))
    big = pltpu.VMEM((s, HEAD_DIM), F32)
    padded32 = pltpu.VMEM((s + BLK, HEAD_DIM), F32)
    padded16 = pltpu.VMEM((s + BLK, HEAD_DIM), BF16)
    res = pl.pallas_call(
        body, name="attn_bwd", grid=(heads,),
        in_specs=[ANY, head_col(3), head_col(4), head_col(5), own_col,
                  pl.BlockSpec((None, s, 2), lambda h: (h, 0, 0)), vec, vec,
                  pl.BlockSpec(memory_space=pltpu.SMEM),
                  pl.BlockSpec((n_dil, BLK, 2 * BLK), lambda h: (0, 0, 0))] + [ANY] * len(c_ins),
        out_specs=[ANY, vec, vec, pl.BlockSpec((None, NUM_BUCKETS, LANE), lambda h: (h, 0, 0))] + [ANY] * len(c_outs),
        out_shape=[jax.ShapeDtypeStruct((s, d_in), BF16), jax.ShapeDtypeStruct((1, HEAD_DIM), F32),
                   jax.ShapeDtypeStruct((1, HEAD_DIM), F32), jax.ShapeDtypeStruct((heads, NUM_BUCKETS, LANE), F32)]
        + c_outs,
        scratch_shapes=[pltpu.VMEM((s, HEAD_DIM), BF16), padded16, padded16, pltpu.VMEM((s, HEAD_DIM), BF16),
                        pltpu.VMEM((s, 2), F32),
                        pltpu.VMEM((s, HEAD_DIM), BF16), padded16, padded16, pltpu.VMEM((s, HEAD_DIM), BF16),
                        pltpu.VMEM((s, 2), F32),
                        pltpu.VMEM((2, 4, BLK, HEAD_DIM), F32), pltpu.VMEM((2, BLK, 2), F32),
                        big, big, big, big, padded32, padded32,
                        pltpu.VMEM((2 * n_dil, BLK, 2 * BLK), F32), pltpu.VMEM((n_dil, BLK, 2 * BLK), F32),
                        pltpu.VMEM((ATTN_GROUP, BLK, 2 * BLK), F32), pltpu.VMEM((ATTN_GROUP, BLK, 2 * BLK), F32),
                        pltpu.VMEM((ATTN_GROUP, BLK, 2 * BLK), BF16), pltpu.VMEM((ATTN_GROUP, BLK, 2 * BLK), BF16),
                        pltpu.SemaphoreType.DMA((3,))] + c_scratch,
        input_output_aliases={0: 0},
        compiler_params=_params("arbitrary"),
    )(dproj, proj, proj, proj, d_o, stats, g_q, g_k, rel_bias, tables, *c_ins)
    return res[0], res[1], res[2], res[3], res[4:]


def _local_step(xs, ps, tgt, sp, win4, wout4, wgate4, wup4, dist=None):
    s, d = xs.shape
    wa = d // 2
    heads = wa // HEAD_DIM
    d_in = 7 * wa
    c4 = d_in // N_CHIPS
    dple = ps.shape[1]

    tm, tn = _tile(s, 1024), _tile(d, 1024)
    tn_in = _tile(c4, 1024)
    per_in = c4 // tn_in
    tn_up = _tile(d // N_CHIPS, 512)
    per_up = (d // N_CHIPS) // tn_up
    gm, gn = s // tm, d // tn

    hn, rx = _rms_fwd("rms_pre", xs, sp["g_pre"])
    if dist:
        proj, win4 = _mm_in_gather(dist["chip"], hn, win4)
    else:
        proj = _matmul(
            "mm_in", hn, win4, pl.BlockSpec((tm, d), lambda i, j, k: (i, 0)), _shard_spec(d, tn_in, per_in, 2, 1),
            (gm, d_in // tn_in, 1), None, jax.ShapeDtypeStruct((s, d_in), F32),
            pl.BlockSpec((tm, tn_in), lambda i, j, k: (i, j)))
    b_st = sp["b_s"].T
    y, z, vn, mu_v, rs_v, ra = _gmlp_fwd(proj, sp["w_s"], b_st, sp["ln_v_g"], sp["ln_v_b"], sp["g_out_a"], d)
    yb, lse, gathered = _attn_fwd(proj, sp["g_q"], sp["g_k"], sp["rel_bias"], d,
                                  comm=_Gather([wout4, wgate4, wup4]) if dist else None)
    if dist:
        wout4, wgate4, wup4 = gathered
    wout, wgate = wout4.reshape(d, d), wgate4.reshape(d, d)
    y, rb = _gate_fwd(y, yb, proj, sp["g_out_b"], d)

    def residual(acc, extra, outs):
        outs[0][...] = extra[0][...] + acc

    tile = pl.BlockSpec((tm, tn), lambda i, j, k: (i, j))
    tile_up = pl.BlockSpec((tm, tn_up), lambda i, j, k: (i, j))
    h = _matmul(
        "mm_out", y, wout, pl.BlockSpec((tm, d), lambda i, j, k: (i, 0)), pl.BlockSpec((d, tn), lambda i, j, k: (0, j)),
        (gm, gn, 1), None, jax.ShapeDtypeStruct((s, d), F32), tile, extras=(xs,), extra_specs=(tile,),
        epilogue=residual)
    hp, rh = _rms_fwd("rms_ple", h, sp["g_ple"])
    p16 = ps.astype(BF16)

    def head(acc, extra, outs):
        p_ref, wup_ref, h_ref, t_ref = extra
        dout_ref, dgl_ref, dup_ref, loss_ref = outs
        up = _dot(p_ref[...], wup_ref[...])
        gate = jax.nn.sigmoid(acc)
        err = h_ref[...] + gate * up - t_ref[...]
        sq = _colsum8(err * err)
        part = sq[:, 0:LANE]
        for c in range(1, sq.shape[1] // LANE):
            part = part + sq[:, c * LANE:(c + 1) * LANE]
        loss_ref[...] = part
        dout = err * (1.0 / d)
        dout_ref[...] = dout.astype(BF16)
        dup_ref[...] = (dout * gate).astype(BF16)
        dgl_ref[...] = (dout * up * gate * (1.0 - gate)).astype(BF16)

    tile_up = pl.BlockSpec((tm, tn_up), lambda i, j, k: (i, j))
    dout, dglin, dup, loss_parts = _matmul(
        "mm_gate_loss", hp, wgate, pl.BlockSpec((tm, d), lambda i, j, k: (i, 0)), pl.BlockSpec((d, tn_up), lambda i, j, k: (0, j)),
        (gm, d // tn_up, 1), None,
        (jax.ShapeDtypeStruct((s, d), BF16), jax.ShapeDtypeStruct((s, d), BF16), jax.ShapeDtypeStruct((s, d), BF16),
         jax.ShapeDtypeStruct((gm * 8, (d // tn_up) * LANE), F32)),
        (tile_up, tile_up, tile_up, pl.BlockSpec((8, LANE), lambda i, j, k: (i, j))),
        extras=(p16, wup4, h, tgt),
        extra_specs=(pl.BlockSpec((tm, dple), lambda i, j, k: (i, 0)), _shard_spec(dple, tn_up, per_up, 2, 1),
                     tile_up, tile_up),
        epilogue=head)
    loss = 0.5 * jnp.sum(loss_parts) * (1.0 / d)

    dhp = _matmul(
        "mm_dhp", dglin, wgate, pl.BlockSpec((tm, d), lambda i, j, k: (i, 0)), pl.BlockSpec((tn, d), lambda i, j, k: (j, 0)),
        (gm, gn, 1), None, jax.ShapeDtypeStruct((s, d), BF16), tile, tb=True)
    dh, dh16, g_g_ple = _rms_bwd("rms_ple_bwd", dhp, h, rh, sp["g_ple"], dout, True)
    tmw = _tile(d, 1024)
    g_wout = _matmul(
        "mm_gwout", y, dh16, pl.BlockSpec((s, tmw), lambda i, j, k: (0, i)), pl.BlockSpec((s, tn), lambda i, j, k: (0, j)),
        (d // tmw, gn, 1), None, jax.ShapeDtypeStruct((d, d), F32), pl.BlockSpec((tmw, tn), lambda i, j, k: (i, j)), ta=True)
    g_wout4 = g_wout.reshape(N_CHIPS, d // N_CHIPS, d)
    g_wgate = _matmul(
        "mm_gwgate", hp, dglin, pl.BlockSpec((s, tmw), lambda i, j, k: (0, i)), pl.BlockSpec((s, tn), lambda i, j, k: (0, j)),
        (d // tmw, gn, 1), None, jax.ShapeDtypeStruct((d, d), F32), pl.BlockSpec((tmw, tn), lambda i, j, k: (i, j)), ta=True,
        comm=_SwapHalves([g_wout4]) if dist else None)
    if dist:
        g_wgate, swapped_out = g_wgate
    g_wgate4 = g_wgate.reshape(N_CHIPS, d // N_CHIPS, d)
    g_wup = _matmul(
        "mm_gwup", p16, dup, pl.BlockSpec((s, dple), lambda i, j, k: (0, 0)), pl.BlockSpec((s, tn_up), lambda i, j, k: (0, j)),
        (1, d // tn_up, 1), None, jax.ShapeDtypeStruct((N_CHIPS, dple, d // N_CHIPS), F32),
        _shard_spec(dple, tn_up, per_up, 0, 1), ta=True)
    dy = _matmul(
        "mm_dy", dh16, wout, pl.BlockSpec((tm, d), lambda i, j, k: (i, 0)), pl.BlockSpec((tn, d), lambda i, j, k: (j, 0)),
        (gm, gn, 1), None, jax.ShapeDtypeStruct((s, d), BF16), tile, tb=True,
        comm=_SwapHalves([g_wgate4, g_wup]) if dist else None)
    early = (g_wout4, g_wgate4, g_wup)
    if dist:
        core = dist["core"]
        dy, swapped_rest = dy
        early_sums = [_chip_sum("chip_sum_" + n, dist["chip"], core, g, o)
                      for n, g, o in zip(LARGE[1:], early, list(swapped_out) + list(swapped_rest))]

    dproj, d_o, stats, g_g_out_b = _gate_bwd(dy, yb, lse, proj, rb, sp["g_out_b"], d)
    dproj, g_g_q, g_g_k, dbias, early_got = _attn_bwd(
        dproj, d_o, stats, proj, sp["g_q"], sp["g_k"], sp["rel_bias"], d,
        comm=_ExchangeChips([s16 for _, s16 in early_sums]) if dist else None)
    dproj, g_w_s, dz_sum, g_ln_g, g_ln_b, g_g_out_a = _gmlp_bwd(
        dproj, dy, proj, z, vn, mu_v, rs_v, ra, sp["w_s"], sp["ln_v_g"], sp["g_out_a"], d)

    if dist:
        tmw_half = _tile(d // 2, 1024)
        n_half = (d // 2) // tmw_half

        def half_of_gwin(name, own, comm):
            def rows(i, j, k, core_ref):
                return (0, (core_ref[0] if own else 1 - core_ref[0]) * n_half + i)
            return _matmul(
                name, hn, dproj, pl.BlockSpec((s, tmw_half), rows),
                pl.BlockSpec((s, tn_in), lambda i, j, k, core_ref: (0, j)),
                (n_half, d_in // tn_in, 1), None, jax.ShapeDtypeStruct((N_CHIPS, d // 2, c4), F32 if own else BF16),
                pl.BlockSpec((None, tmw_half, tn_in), lambda i, j, k, core_ref: (j // per_in, i, j % per_in)), ta=True,
                comm=comm, prefetch=(core,))

        g_theirs = half_of_gwin("mm_gwin_theirs", False, None)
        g_mine, (g_from_sibling,) = half_of_gwin("mm_gwin_mine", True, _SwapHalves([g_theirs], whole=True))
        win_sum = _chip_sum("chip_sum_w_in", dist["chip"], core, g_mine, g_from_sibling)
    else:
        g_win = _matmul(
            "mm_gwin", hn, dproj, pl.BlockSpec((s, tmw), lambda i, j, k: (0, i)),
            pl.BlockSpec((s, tn_in), lambda i, j, k: (0, j)),
            (d // tmw, d_in // tn_in, 1), None, jax.ShapeDtypeStruct((N_CHIPS, d, c4), F32),
            _shard_spec(tmw, tn_in, per_in, 0, 1), ta=True)
    tk = c4
    tmh, tnh = _tile(s, 1024), _tile(d, 1024)
    dhn = _matmul(
        "mm_dhn", dproj, win4, pl.BlockSpec((tmh, tk), lambda i, j, k: (i, k)), _shard_spec(tnh, tk, 1, 1, 2),
        (s // tmh, d // tnh, d_in // tk), (tmh, tnh), jax.ShapeDtypeStruct((s, d), BF16),
        pl.BlockSpec((tmh, tnh), lambda i, j, k: (i, j)), tb=True,
        comm=_ExchangeChips([win_sum[1]]) if dist else None)
    if dist:
        dhn, win_got = dhn
    grad_x, g_g_pre = _rms_bwd("rms_pre_bwd", dhn, xs, rx, sp["g_pre"], dh, False)

    small = {
        "g_pre": g_g_pre,
        "w_s": g_w_s,
        "b_s": jnp.sum(dz_sum.reshape(CHUNK, heads, HEAD_DIM), axis=-1).T,
        "ln_v_g": g_ln_g, "ln_v_b": g_ln_b,
        "g_q": g_g_q, "g_k": g_g_k,
        "rel_bias": dbias[:, :, 0].T,
        "g_out_a": g_g_out_a, "g_out_b": g_g_out_b,
        "g_ple": g_g_ple,
    }
    if not dist:
        return loss, grad_x, (g_win, *early), small
    sums32 = [win_sum[0]] + [s32 for s32, _ in early_sums]
    return loss, grad_x, (sums32, list(win_got) + list(early_got)), small


def _place():
    x, y, c = lax.axis_index("x"), lax.axis_index("y"), lax.axis_index("c")
    chips = [(1 - x, y), (x, 1 - y), (1 - x, 1 - y)]
    return x, y, c, chips


def _remote(src, dst, send_sems, recv_sems, k, to):
    return pltpu.make_async_remote_copy(src_ref=src, dst_ref=dst, send_sem=send_sems.at[k], recv_sem=recv_sems.at[k],
                                        device_id=to, device_id_type=MESH)


def _cast_bf16(name, chip, w):
    r, c = w.shape
    tm = _tile(r, 256)

    def body(chip_ref, w_ref, o_ref):
        del chip_ref
        o_ref[...] = w_ref[...].astype(BF16)

    return pl.pallas_call(
        body, name=name,
        grid_spec=pltpu.PrefetchScalarGridSpec(
            num_scalar_prefetch=1, grid=(r // tm,),
            in_specs=[pl.BlockSpec((tm, c), lambda i, chip_ref: (i, 0))],
            out_specs=pl.BlockSpec((None, tm, c), lambda i, chip_ref: (chip_ref[0], i, 0))),
        out_shape=jax.ShapeDtypeStruct((N_CHIPS, r, c), BF16), compiler_params=_params("parallel"),
    )(chip, w)


class _Gather:
    in_place = True

    def __init__(self, fulls):
        self.ins = list(fulls)
        self.out_shape = [jax.ShapeDtypeStruct(f.shape, f.dtype) for f in fulls]
        n = len(fulls)
        self.scratch = [pltpu.SemaphoreType.DMA((6 * n,)), pltpu.SemaphoreType.DMA((6 * n,))]

    @staticmethod
    def _sends(outs, sems):
        send_sems, recv_sems = sems
        x, y, c, chips = _place()
        cps = []
        for w, ref in enumerate(outs):
            half = ref.shape[1] // 2
            blk = ref.at[2 * x + y, pl.ds(c * half, half)]
            cps += [_remote(blk, blk, send_sems, recv_sems, 6 * w + q, (*chip, c)) for q, chip in enumerate(chips)]
        return cps

    def start(self, ins, outs, sems):
        for cp in self._sends(outs, sems):
            cp.start()

    def finish(self, ins, outs, sems):
        send_sems, recv_sems = sems
        x, y, c, chips = _place()
        sibling = (x, y, 1 - c)
        forwards = []
        for w, ref in enumerate(outs):
            half = ref.shape[1] // 2
            for q, chip in enumerate(chips):
                blk = ref.at[2 * chip[0] + chip[1], pl.ds(c * half, half)]
                _remote(blk, blk, send_sems, recv_sems, 6 * w + q, sibling).wait_recv()
                fwd = _remote(blk, blk, send_sems, recv_sems, 6 * w + 3 + q, sibling)
                fwd.start()
                forwards.append(fwd)
        for w, ref in enumerate(outs):
            half = ref.shape[1] // 2
            for q, chip in enumerate(chips):
                blk = ref.at[2 * chip[0] + chip[1], pl.ds((1 - c) * half, half)]
                _remote(blk, blk, send_sems, recv_sems, 6 * w + 3 + q, sibling).wait_recv()
        for cp in self._sends(outs, sems) + forwards:
            cp.wait_send()


class _ExchangeChips:
    in_place = False

    def __init__(self, sums16):
        self.ins = list(sums16)
        self.out_shape = [jax.ShapeDtypeStruct(g.shape, g.dtype) for g in sums16]
        n = len(sums16)
        self.scratch = [pltpu.SemaphoreType.DMA((3 * n,)), pltpu.SemaphoreType.DMA((3 * n,))]

    @staticmethod
    def _sends(ins, outs, sems):
        send_sems, recv_sems = sems
        x, y, c, chips = _place()
        return [_remote(ins[w].at[2 * chip[0] + chip[1]], outs[w].at[2 * x + y], send_sems, recv_sems, 3 * w + q, (*chip, c))
                for w in range(len(ins)) for q, chip in enumerate(chips)]

    def start(self, ins, outs, sems):
        for cp in self._sends(ins, outs, sems):
            cp.start()

    def finish(self, ins, outs, sems):
        send_sems, recv_sems = sems
        x, y, c, chips = _place()
        for w in range(len(ins)):
            for q, chip in enumerate(chips):
                blk = outs[w].at[2 * chip[0] + chip[1]]
                _remote(blk, blk, send_sems, recv_sems, 3 * w + q, (*chip, c)).wait_recv()
        for cp in self._sends(ins, outs, sems):
            cp.wait_send()


def _mm_in_gather(chip, hn, win4):
    s, d = hn.shape
    c4 = win4.shape[2]
    tm, tn = _tile(s, 1024), _tile(c4, 1024)
    per = c4 // tn
    nj, gm = N_CHIPS * per, s // tm
    half = d // 2

    def tile_of(j):
        jj = j - per
        o = jnp.where(j < per, 0, jnp.where(j < 3 * per, 1 + jj % 2, 3))
        t = jnp.where(j < per, j, jnp.where(j < 3 * per, jj // 2, j - 3 * per))
        return o, t

    def step_of(o, t):
        return per + 2 * t + o - 1 if o in (1, 2) else o * per + t

    def block_of(j, chip_ref):
        o, _ = tile_of(j)
        return chip_ref[0] ^ (((o & 1) << 1) | (o >> 1))

    i_late = max(gm - 2, 0)

    def body(chip_ref, hn_ref, w_any, proj_ref, w_ref, wbuf, wsem, send_sems, recv_sems):
        del w_any
        j, i = pl.program_id(0), pl.program_id(1)
        x, y, c, chips = _place()
        sibling = (x, y, 1 - c)

        def region(k, rows_half, t):
            return w_ref.at[k, pl.ds(rows_half * half, half), pl.ds(t * tn, tn)]

        def send_mine(q, t):
            blk = region(2 * x + y, c, t)
            return _remote(blk, blk, send_sems, recv_sems, q * per + t, (*chips[q], c))

        def forward(q, t):
            blk = region(2 * chips[q][0] + chips[q][1], c, t)
            return _remote(blk, blk, send_sems, recv_sems, (3 + q) * per + t, sibling)

        def fetch(jj, mine, slot):
            rows = pl.ds((c if mine else 1 - c) * half, half)
            cols = pl.ds(pl.multiple_of(tile_of(jj)[1] * tn, LANE), tn)
            return pltpu.make_async_copy(w_ref.at[block_of(jj, chip_ref), rows, cols], wbuf.at[slot, rows],
                                         wsem.at[2 * slot + (0 if mine else 1)])

        def foreign_tiles():
            return [(o, t) for o in range(1, N_CHIPS) for t in range(per)]

        relayed = [(0, t) for t in range(1, per, 2)] + [(1, t) for t in range(0, per, 2)]

        def relay(q, t):
            blk = region(2 * chips[q][0] + chips[q][1], c, t)
            return _remote(blk, blk, send_sems, recv_sems, 2 * per + t, (*chips[1 - q], c))

        def landed(q, t):
            blk = region(2 * chips[q][0] + chips[q][1], c, t)
            _remote(blk, blk, send_sems, recv_sems, q * per + t, sibling).wait_recv()
            forward(q, t).start()
            if (q, t) in relayed:
                relay(q, t).start()

        @pl.when((j == 0) & (i == 0))
        def _():
            for t in range(per):
                send_mine(0, t).start()
                send_mine(1, t).start()
            fetch(0, True, 0).start()
            fetch(0, False, 0).start()

        nxt = j + 1

        @pl.when((i == 0) & (nxt < nj))
        def _():
            for o, t in foreign_tiles():
                @pl.when(nxt == step_of(o, t))
                def _():
                    if o == 1:
                        landed(0, t)
                        landed(1, t)
                    elif o == 3:
                        landed(2, t)

            fetch(nxt, True, nxt % 2).start()

            @pl.when(nxt < per)
            def _():
                fetch(nxt, False, nxt % 2).start()

        @pl.when((i == i_late) & (nxt < nj))
        def _():
            for o, t in foreign_tiles():
                @pl.when(nxt == step_of(o, t))
                def _():
                    blk = region(2 * chips[o - 1][0] + chips[o - 1][1], 1 - c, t)
                    _remote(blk, blk, send_sems, recv_sems, (3 + o - 1) * per + t, sibling).wait_recv()
                    fetch(nxt, False, nxt % 2).start()

        @pl.when(i == 0)
        def _():
            fetch(j, True, j % 2).wait()
            fetch(j, False, j % 2).wait()

        proj_ref[...] = _dot(hn_ref[...], wbuf[j % 2])

        @pl.when((j == nj - 1) & (i == gm - 1))
        def _():
            for t in range(per):
                send_mine(0, t).wait_send()
                send_mine(1, t).wait_send()
                for q in range(3):
                    forward(q, t).wait_send()
            for q, t in relayed:
                relay(q, t).wait_send()

    return pl.pallas_call(
        body, name="mm_in_gather",
        grid_spec=pltpu.PrefetchScalarGridSpec(
            num_scalar_prefetch=1, grid=(nj, gm),
            in_specs=[pl.BlockSpec((tm, d), lambda j, i, chip_ref: (i, 0)), ANY],
            out_specs=[pl.BlockSpec((tm, tn), lambda j, i, chip_ref: (i, block_of(j, chip_ref) * per + tile_of(j)[1])), ANY],
            scratch_shapes=[pltpu.VMEM((2, d, tn), BF16), pltpu.SemaphoreType.DMA((4,)),
                            pltpu.SemaphoreType.DMA((6 * per,)), pltpu.SemaphoreType.DMA((6 * per,))]),
        out_shape=[jax.ShapeDtypeStruct((s, N_CHIPS * c4), F32), jax.ShapeDtypeStruct(win4.shape, win4.dtype)],
        input_output_aliases={2: 1},
        compiler_params=_params("arbitrary", "arbitrary"),
    )(chip, hn, win4)


class _SwapHalves:
    in_place = False

    def __init__(self, grads, whole=False):
        self.ins = list(grads)
        self.whole = whole
        self.out_shape = [jax.ShapeDtypeStruct((N_CHIPS, g.shape[1] // (1 if whole else 2), g.shape[2]), g.dtype)
                          for g in grads]
        self.scratch = [pltpu.SemaphoreType.DMA((len(grads),)), pltpu.SemaphoreType.DMA((len(grads),))]

    def _copies(self, ins, outs, sems):
        x, y, c, _ = _place()
        cps = []
        for w in range(len(ins)):
            half = ins[w].shape[1] // 2
            src = ins[w] if self.whole else ins[w].at[:, pl.ds((1 - c) * half, half)]
            cps.append(_remote(src, outs[w], sems[0], sems[1], w, (x, y, 1 - c)))
        return cps

    def start(self, ins, outs, sems):
        for cp in self._copies(ins, outs, sems):
            cp.start()

    def finish(self, ins, outs, sems):
        for cp in self._copies(ins, outs, sems):
            cp.wait()


def _chip_sum(name, chip, core, grad, got):
    _, half, c = got.shape
    th = _tile(half, 256)
    n = half // th
    skip = n if grad.shape[1] != half else 0

    def body(chip_ref, core_ref, g_ref, o_ref, s_ref):
        del chip_ref, core_ref
        s_ref[...] = (g_ref[...] + o_ref[...].astype(F32)).astype(s_ref.dtype)

    def specs(first, out_block):
        def block(k, i, chip_ref):
            return (chip_ref[0] + first + k) % N_CHIPS

        return dict(
            in_specs=[pl.BlockSpec((None, th, c), lambda k, i, chip_ref, core_ref: (block(k, i, chip_ref), core_ref[0] * skip + i, 0)),
                      pl.BlockSpec((None, th, c), lambda k, i, chip_ref, core_ref: (block(k, i, chip_ref), i, 0))],
            out_specs=out_block(block))

    own = pl.pallas_call(
        body, name=name + "_own",
        grid_spec=pltpu.PrefetchScalarGridSpec(
            num_scalar_prefetch=2, grid=(1, n),
            **specs(0, lambda block: pl.BlockSpec((th, c), lambda k, i, chip_ref, core_ref: (i, 0)))),
        out_shape=jax.ShapeDtypeStruct((half, c), F32), compiler_params=_params("parallel", "parallel"),
    )(chip, core, grad, got)
    others = pl.pallas_call(
        functools.partial(body), name=name + "_others",
        grid_spec=pltpu.PrefetchScalarGridSpec(
            num_scalar_prefetch=2, grid=(N_CHIPS - 1, n),
            **specs(1, lambda block: pl.BlockSpec(
                (None, th, c), lambda k, i, chip_ref, core_ref: (block(k, i, chip_ref), i, 0)))),
        out_shape=jax.ShapeDtypeStruct((N_CHIPS, half, c), BF16), compiler_params=_params("parallel", "parallel"),
    )(chip, core, grad, got)
    return own, others


def _riding(comm, grid, body, n_pre, n_in, n_out):
    if comm is None:
        return body
    c_in, c_out, c_scr = len(comm.ins), len(comm.out_shape), len(comm.scratch)

    def wrapped(*refs):
        pre, rest = refs[:n_pre], refs[n_pre:]
        ins, rest = rest[:n_in], rest[n_in:]
        cin, rest = rest[:c_in], rest[c_in:]
        outs, rest = rest[:n_out], rest[n_out:]
        cout, rest = rest[:c_out], rest[c_out:]
        own, sems = rest[:len(rest) - c_scr], rest[len(rest) - c_scr:]
        ids = [pl.program_id(ax) for ax in range(len(grid))]
        first, last = ids[0] == 0, ids[0] == grid[0] - 1
        for ax in range(1, len(grid)):
            first, last = first & (ids[ax] == 0), last & (ids[ax] == grid[ax] - 1)

        @pl.when(first)
        def _():
            comm.start(cin, cout, sems)

        body(*pre, *ins, *outs, *own)

        @pl.when(last)
        def _():
            comm.finish(cin, cout, sems)

    return wrapped


def _total(name, chip, core, sum32, got16, comm=None):
    half, c = sum32.shape
    th = _tile(half, 256)
    n = half // th

    def body(chip_ref, core_ref, own_ref, a_ref, b_ref, c_ref, o_ref):
        del chip_ref, core_ref
        o_ref[...] = ((own_ref[...] + a_ref[...].astype(F32)) + b_ref[...].astype(F32)) + c_ref[...].astype(F32)

    def other(step):
        return pl.BlockSpec((None, th, c), lambda i, chip_ref, core_ref: ((chip_ref[0] + step) % N_CHIPS, i, 0))

    c_ins = list(comm.ins) if comm else []
    res = pl.pallas_call(
        _riding(comm, (n,), body, 2, 4, 1), name=name,
        grid_spec=pltpu.PrefetchScalarGridSpec(
            num_scalar_prefetch=2, grid=(n,),
            in_specs=[pl.BlockSpec((th, c), lambda i, chip_ref, core_ref: (i, 0)), other(1), other(2), other(3)]
            + [ANY] * len(c_ins),
            out_specs=[pl.BlockSpec((th, c), lambda i, chip_ref, core_ref: (core_ref[0] * n + i, 0))]
            + [ANY] * len(c_ins),
            scratch_shapes=list(comm.scratch) if comm else []),
        out_shape=[jax.ShapeDtypeStruct((2 * half, c), F32)] + (list(comm.out_shape) if comm else []),
        input_output_aliases={6 + i: 1 + i for i in range(len(c_ins))} if comm else {},
        compiler_params=_params("arbitrary" if comm else "parallel"),
    )(chip, core, sum32, got16, got16, got16, *c_ins)
    return (res[0], res[1:]) if comm else res[0]


class _JoinHalves:
    in_place = True

    def __init__(self, blocks):
        self.ins = list(blocks)
        self.out_shape = [jax.ShapeDtypeStruct(t.shape, t.dtype) for t in blocks]
        self.scratch = [pltpu.SemaphoreType.DMA((len(blocks),)), pltpu.SemaphoreType.DMA((len(blocks),))]

    @staticmethod
    def _copy(ref, w, mine, sems):
        x, y, c, _ = _place()
        half = ref.shape[0] // 2
        rows = ref.at[pl.ds((c if mine else 1 - c) * half, half)]
        return _remote(rows, rows, sems[0], sems[1], w, (x, y, 1 - c))

    def start(self, ins, outs, sems):
        for w, ref in enumerate(outs):
            self._copy(ref, w, True, sems).start()

    def finish(self, ins, outs, sems):
        for w, ref in enumerate(outs):
            self._copy(ref, w, True, sems).wait_send()
            self._copy(ref, w, False, sems).wait_recv()


class _Both:
    in_place = True

    def __init__(self, a, b):
        assert a.in_place and b.in_place
        self.parts = (a, b)
        self.ins = list(a.ins) + list(b.ins)
        self.out_shape = list(a.out_shape) + list(b.out_shape)
        self.scratch = list(a.scratch) + list(b.scratch)

    def _split(self, ins, outs, sems):
        a = self.parts[0]
        n, k = len(a.ins), len(a.scratch)
        return (a, ins[:n], outs[:n], sems[:k]), (self.parts[1], ins[n:], outs[n:], sems[k:])

    def start(self, ins, outs, sems):
        for part, i, o, s in self._split(ins, outs, sems):
            part.start(i, o, s)

    def finish(self, ins, outs, sems):
        for part, i, o, s in self._split(ins, outs, sems):
            part.finish(i, o, s)


def _alone(name, comm):
    n = len(comm.ins)

    def body(*refs):
        comm.start(refs[:n], refs[n:2 * n], refs[2 * n:])
        comm.finish(refs[:n], refs[n:2 * n], refs[2 * n:])

    return pl.pallas_call(
        body, name=name, in_specs=[ANY] * n, out_specs=[ANY] * n, out_shape=comm.out_shape,
        scratch_shapes=comm.scratch, input_output_aliases={i: i for i in range(n)},
    )(*comm.ins)


class _GatherSmall:
    in_place = True

    def __init__(self, full):
        self.ins = [full]
        self.out_shape = [jax.ShapeDtypeStruct(full.shape, full.dtype)]
        self.scratch = [pltpu.SemaphoreType.DMA((7,)), pltpu.SemaphoreType.DMA((7,))]

    @staticmethod
    def _copy(ref, sems, k, block, to):
        m_per = ref.shape[0] // N_DEV
        px, py, pc = block
        rows = ref.at[pl.ds((4 * px + 2 * py + pc) * m_per, m_per), :]
        return _remote(rows, rows, sems[0], sems[1], k, to)

    def _first(self, ref, sems):
        x, y, c, chips = _place()
        me = (x, y, c)
        return [self._copy(ref, sems, 0, me, (x, y, 1 - c))] + \
               [self._copy(ref, sems, 1 + j, me, (*chip, c)) for j, chip in enumerate(chips)]

    def start(self, ins, outs, sems):
        for cp in self._first(outs[0], sems):
            cp.start()

    def finish(self, ins, outs, sems):
        ref = outs[0]
        x, y, c, chips = _place()
        me, sibling = (x, y, c), (x, y, 1 - c)
        passed = [self._copy(ref, sems, 4 + j, (*chip, c), sibling) for j, chip in enumerate(chips)]
        for j, chip in enumerate(chips):
            self._copy(ref, sems, 1 + j, (*chip, c), me).wait_recv()
            passed[j].start()
        self._copy(ref, sems, 0, sibling, me).wait_recv()
        for j, chip in enumerate(chips):
            self._copy(ref, sems, 4 + j, (*chip, 1 - c), me).wait_recv()
        for cp in self._first(ref, sems) + passed:
            cp.wait_send()


def _adamw_math(w, g, m, v):
    m = ADAM_B1 * m + (1.0 - ADAM_B1) * g
    v = ADAM_B2 * v + (1.0 - ADAM_B2) * (g * g)
    m_hat = m / (1.0 - ADAM_B1 ** ADAM_STEP)
    v_hat = v / (1.0 - ADAM_B2 ** ADAM_STEP)
    delta = -ADAM_LR * (m_hat / (jnp.sqrt(v_hat) + ADAM_EPS) + ADAM_WD * w)
    return delta, m, v


def _adamw(name, w, g, m, v):
    r, c = w.shape
    tm = _tile(r, 128)

    def body(w_ref, g_ref, m_ref, v_ref, g_out, d_out, m_out, v_out):
        g = g_ref[...]
        g_out[...] = g
        d_out[...], m_out[...], v_out[...] = _adamw_math(w_ref[...], g, m_ref[...], v_ref[...])

    spec = pl.BlockSpec((tm, c), lambda i: (i, 0))
    return pl.pallas_call(
        body, name=name, grid=(r // tm,), in_specs=[spec] * 4, out_specs=[spec] * 4,
        out_shape=[jax.ShapeDtypeStruct((r, c), F32)] * 4, compiler_params=_params("parallel"),
    )(w, g, m, v)


def _adamw_small(gathered, w, m, v):
    rows = w.shape[0]

    def body(all_ref, w_ref, m_ref, v_ref, g_out, d_out, m_out, v_out):
        g = all_ref[0:rows, :]
        for dev in range(1, N_DEV):
            g = g + all_ref[dev * rows:(dev + 1) * rows, :]
        g_out[...] = g
        d_out[...], m_out[...], v_out[...] = _adamw_math(w_ref[...], g, m_ref[...], v_ref[...])

    return pl.pallas_call(
        body, name="adamw_small", out_shape=[jax.ShapeDtypeStruct(w.shape, F32)] * 4, compiler_params=_params(),
    )(gathered, w, m, v)


SMALL = ("g_pre", "w_s", "b_s", "ln_v_g", "ln_v_b", "g_q", "g_k", "rel_bias", "g_out_a", "g_out_b", "g_ple")
LARGE = ("w_in", "w_out", "w_ple_gate", "w_ple_up")
WEIGHTS = ("g_pre", "w_in", "w_s", "b_s", "ln_v_g", "ln_v_b", "g_q", "g_k", "rel_bias", "g_out_a", "g_out_b", "w_out",
           "g_ple", "w_ple_gate", "w_ple_up")


def _pack(parts):
    flat = jnp.concatenate([parts[n].reshape(-1).astype(F32) for n in SMALL])
    rows = -(-flat.shape[0] // (8 * LANE)) * 8
    return jnp.pad(flat, (0, rows * LANE - flat.shape[0])).reshape(rows, LANE)


def _unpack(pack, like):
    flat = pack.reshape(-1)
    out, at = {}, 0
    for n in SMALL:
        size = math.prod(like[n].shape)
        out[n] = flat[at:at + size].reshape(like[n].shape)
        at += size
    return out


def kernel(x, p, g_pre, w_in, w_s, b_s, ln_v_g, ln_v_b, g_q, g_k, rel_bias, g_out_a, g_out_b, w_out, g_ple, w_ple_gate, w_ple_up, loss_target, m_g_pre, m_w_in, m_w_s, m_b_s, m_ln_v_g, m_ln_v_b, m_g_q, m_g_k, m_rel_bias, m_g_out_a, m_g_out_b, m_w_out, m_g_ple, m_w_ple_gate, m_w_ple_up, v_g_pre, v_w_in, v_w_s, v_b_s, v_ln_v_g, v_ln_v_b, v_g_q, v_g_k, v_rel_bias, v_g_out_a, v_g_out_b, v_w_out, v_g_ple, v_w_ple_gate, v_w_ple_up):
    given = dict(locals())
    weights = {n: given[n] for n in WEIGHTS}
    mom_m = {n: given["m_" + n] for n in WEIGHTS}
    mom_v = {n: given["v_" + n] for n in WEIGHTS}
    xs, ps, tgt = x[0], p[0, 0], loss_target[0]
    d = xs.shape[1]

    core = lax.axis_index("c").astype(jnp.int32).reshape(1)
    chip = (2 * lax.axis_index("x") + lax.axis_index("y")).astype(jnp.int32).reshape(1)

    win4, wout4, wgate4, wup4 = [_cast_bf16("cast_" + n, chip, weights[n][0]) for n in LARGE]

    sp = {
        "g_pre": g_pre, "w_s": w_s[0], "b_s": b_s[0], "ln_v_g": ln_v_g, "ln_v_b": ln_v_b, "g_q": g_q, "g_k": g_k,
        "rel_bias": rel_bias, "g_out_a": g_out_a, "g_out_b": g_out_b, "g_ple": g_ple,
    }
    loss_local, grad_x, (sums32, got16), small = _local_step(xs, ps, tgt, sp, win4, wout4, wgate4, wup4,
                                                             dist={"chip": chip, "core": core})
    loss = lax.psum(loss_local, MESH_AXES)

    own = dict(zip(LARGE, sums32))
    got = dict(zip(LARGE, got16))
    halves = [_total("total_" + n, chip, core, own[n], got[n]) for n in LARGE[1:]]
    my_pack = _pack(small)
    device = 2 * chip[0] + core[0]
    everyone = lax.dynamic_update_slice(jnp.zeros((N_DEV * my_pack.shape[0], LANE), F32), my_pack,
                                        (device * my_pack.shape[0], 0))
    half_in, (*joined, gathered) = _total("total_w_in", chip, core, own["w_in"], got["w_in"],
                                          comm=_Both(_JoinHalves(halves), _GatherSmall(everyone)))
    grads = dict(zip(LARGE[1:], joined))
    grads["w_in"], = _alone("join_w_in", _JoinHalves([half_in]))

    out_g, out_d, out_m, out_v = {}, {}, {}, {}
    for n in LARGE:
        results = _adamw("adamw_" + n, weights[n][0], grads[n], mom_m[n][0], mom_v[n][0])
        out_g[n], out_d[n], out_m[n], out_v[n] = [r[None] for r in results]

    pg, pd, pm, pv = _adamw_small(gathered, _pack(weights), _pack(mom_m), _pack(mom_v))
    for packed, out in ((pg, out_g), (pd, out_d), (pm, out_m), (pv, out_v)):
        out.update(_unpack(packed, weights))

    return (loss, grad_x[None], *[out_g[n] for n in WEIGHTS], *[out_d[n] for n in WEIGHTS],
            *[out_m[n] for n in WEIGHTS], *[out_v[n] for n in WEIGHTS])
```

```python
import functools
import math

import numpy as np

import jax
import jax.numpy as jnp
from jax import lax
from jax.experimental import pallas as pl
from jax.experimental.pallas import tpu as pltpu

F32 = jnp.float32
BF16 = jnp.bfloat16

HEAD_DIM = 128
CHUNK = 128
BLK = 128
DILATED = ((128, 1), (512, 4), (2048, 16))
NUM_BUCKETS = 32
MAX_DISTANCE = 2048
ATTN_GROUP = 8
ATTN_GROUP_FWD = 16
EPS = 1e-6
NEG_INF = -1e30
N_CHIPS = 4
N_DEV = 8
MESH_AXES = ("x", "y", "c")

ADAM_LR = 0.001
ADAM_B1 = 0.9
ADAM_B2 = 0.999
ADAM_EPS = 1e-08
ADAM_WD = 0.01
ADAM_STEP = 10

V7X_VMEM_LIMIT = 58 * 1024 * 1024
LANE = 128
MESH = pl.DeviceIdType.MESH
ANY = pl.BlockSpec(memory_space=pl.ANY)


def _params(*sem):
    return pltpu.CompilerParams(dimension_semantics=sem or None, vmem_limit_bytes=V7X_VMEM_LIMIT)


def _tile(n, target):
    if n <= target:
        return n
    t = (target // LANE) * LANE
    while t > LANE and n % t:
        t -= LANE
    assert n % t == 0, (n, target)
    return t


def _gelu(x):
    return 0.5 * x * (1.0 + lax.erf(x * (1.0 / math.sqrt(2.0))))


def _gelu_and_grad(x):
    cdf = 0.5 * (1.0 + lax.erf(x * (1.0 / math.sqrt(2.0))))
    return x * cdf, cdf + x * jnp.exp(-0.5 * x * x) * (1.0 / math.sqrt(2.0 * math.pi))


def _silu_and_grad(x):
    s = jax.nn.sigmoid(x)
    return x * s, s * (1.0 + x * (1.0 - s))


def _dot(a, b, ta=False, tb=False):
    return lax.dot_general(a, b, (((0 if ta else 1,), (1 if tb else 0,)), ((), ())), preferred_element_type=F32)


def _colsum8(v):
    return jnp.sum(v.reshape(v.shape[0] // 8, 8, v.shape[1]), axis=0)


def _matmul(name, a, b, a_spec, b_spec, grid, acc_shape, out_shape, out_specs, ta=False, tb=False,
            extras=(), extra_specs=(), epilogue=None, comm=None, prefetch=(), fused=None):
    nk = grid[2]
    n_pre = len(prefetch)
    n_extra = len(extras)
    single = not isinstance(out_shape, (tuple, list))
    outs_shape = (out_shape,) if single else tuple(out_shape)
    outs_specs = (out_specs,) if single else tuple(out_specs)
    n_out = len(outs_shape)
    c_ins = list(comm.ins) if comm else []
    c_outs = list(comm.out_shape) if comm else []
    c_scratch = list(comm.scratch) if comm else []
    n_cin, n_cout = len(c_ins), len(c_outs)

    def finish(acc, extra_refs, out_refs):
        if epilogue is None:
            out_refs[0][...] = acc.astype(out_refs[0].dtype)
        else:
            epilogue(acc, extra_refs, out_refs)

    def body(*refs):
        a_ref, b_ref, *rest = refs[n_pre:]
        extra_refs = rest[:n_extra]
        cin_refs = rest[n_extra:n_extra + n_cin]
        out_refs = rest[n_extra + n_cin:n_extra + n_cin + n_out]
        cout_refs = rest[n_extra + n_cin + n_out:n_extra + n_cin + n_out + n_cout]
        scratch_refs = rest[n_extra + n_cin + n_out + n_cout:]
        ids = [pl.program_id(ax) for ax in range(3)]
        if comm:
            sems = scratch_refs[len(scratch_refs) - len(c_scratch):]

            @pl.when((ids[0] == 0) & (ids[1] == 0) & (ids[2] == 0))
            def _():
                comm.start(cin_refs, cout_refs, sems)

        if fused is not None:
            fused(a_ref, b_ref, extra_refs, out_refs)
        elif nk == 1:
            finish(_dot(a_ref[...], b_ref[...], ta, tb), extra_refs, out_refs)
        else:
            acc_ref = scratch_refs[0]

            @pl.when(ids[2] == 0)
            def _():
                acc_ref[...] = jnp.zeros_like(acc_ref)

            acc_ref[...] += _dot(a_ref[...], b_ref[...], ta, tb)

            @pl.when(ids[2] == nk - 1)
            def _():
                finish(acc_ref[...], extra_refs, out_refs)

        if comm:
            @pl.when((ids[0] == grid[0] - 1) & (ids[1] == grid[1] - 1) & (ids[2] == nk - 1))
            def _():
                comm.finish(cin_refs, cout_refs, sems)

    scratch = ([] if nk == 1 else [pltpu.VMEM(acc_shape, F32)]) + c_scratch
    aliases = {n_pre + 2 + n_extra + i: n_out + i for i in range(n_cin)} if (comm and comm.in_place) else {}
    res = pl.pallas_call(
        body, name=name,
        grid_spec=pltpu.PrefetchScalarGridSpec(
            num_scalar_prefetch=n_pre, grid=grid,
            in_specs=[a_spec, b_spec, *extra_specs] + [ANY] * n_cin,
            out_specs=list(outs_specs) + [ANY] * n_cout, scratch_shapes=scratch),
        out_shape=list(outs_shape) + c_outs,
        input_output_aliases=aliases,
        compiler_params=_params(*(("arbitrary",) * 3 if comm else ("parallel", "parallel", "arbitrary"))),
    )(*prefetch, a, b, *extras, *c_ins)
    if comm:
        main = res[:n_out]
        return (main[0] if single else main), res[n_out:]
    return res[0] if single else res


def _shard_spec(rows, cols, per_shard, row_axis, col_axis):
    def index(i, j, k):
        g = (i, j, k)
        return (g[col_axis] // per_shard, g[row_axis], g[col_axis] % per_shard)
    return pl.BlockSpec((None, rows, cols), index)


ROW_CHUNK = 512


def _column_chunks(width):
    chunk = _tile(width, ROW_CHUNK)
    return (lambda c: slice(c * chunk, (c + 1) * chunk)), width // chunk


def _rms_fwd(name, x, g):
    s, d = x.shape
    tm = _tile(s, 256)

    cols, n_chunks = _column_chunks(d)

    def body(x_ref, g_ref, y_ref, r_ref):
        total = jnp.zeros((tm, 1), F32)
        for c in range(n_chunks):
            xc = x_ref[:, cols(c)]
            total = total + jnp.sum(xc * xc, axis=-1, keepdims=True)
        r = lax.rsqrt(total * (1.0 / d) + EPS)
        r_ref[...] = r
        for c in range(n_chunks):
            y_ref[:, cols(c)] = (x_ref[:, cols(c)] * r * g_ref[:, cols(c)]).astype(BF16)

    return pl.pallas_call(
        body, name=name, grid=(s // tm,),
        in_specs=[pl.BlockSpec((tm, d), lambda i: (i, 0)), pl.BlockSpec((1, d), lambda i: (0, 0))],
        out_specs=(pl.BlockSpec((tm, d), lambda i: (i, 0)), pl.BlockSpec((tm, 1), lambda i: (i, 0))),
        out_shape=(jax.ShapeDtypeStruct((s, d), BF16), jax.ShapeDtypeStruct((s, 1), F32)),
        compiler_params=_params("parallel"),
    )(x, g)


def _rms_bwd(name, dy, x, r, g, skip, with_bf16):
    s, d = x.shape
    tm = _tile(s, 256)
    n = s // tm
    cols, n_chunks = _column_chunks(d)

    def body(dy_ref, x_ref, r_ref, g_ref, skip_ref, *outs):
        dx_ref = outs[0]
        dg_ref = outs[-2]
        acc_ref = outs[-1]
        i = pl.program_id(0)
        rv = r_ref[...]

        @pl.when(i == 0)
        def _():
            acc_ref[...] = jnp.zeros_like(acc_ref)

        total = jnp.zeros((tm, 1), F32)
        for c in range(n_chunks):
            dyc, xc = dy_ref[:, cols(c)].astype(F32), x_ref[:, cols(c)]
            acc_ref[:, cols(c)] += _colsum8(dyc * xc * rv)
            total = total + jnp.sum(dyc * g_ref[:, cols(c)] * xc, axis=-1, keepdims=True)
        coef = (rv * rv) * (total * (1.0 / d))
        for c in range(n_chunks):
            dyc, xc = dy_ref[:, cols(c)].astype(F32), x_ref[:, cols(c)]
            dx = skip_ref[:, cols(c)].astype(F32) + rv * (dyc * g_ref[:, cols(c)] - xc * coef)
            dx_ref[:, cols(c)] = dx
            if with_bf16:
                outs[1][:, cols(c)] = dx.astype(BF16)

        @pl.when(i == n - 1)
        def _():
            dg_ref[...] = jnp.sum(acc_ref[...], axis=0, keepdims=True)

    row = pl.BlockSpec((tm, d), lambda i: (i, 0))
    vec = pl.BlockSpec((1, d), lambda i: (0, 0))
    out_specs = [row] + ([row] if with_bf16 else []) + [vec]
    out_shape = [jax.ShapeDtypeStruct((s, d), F32)] + ([jax.ShapeDtypeStruct((s, d), BF16)] if with_bf16 else []) \
        + [jax.ShapeDtypeStruct((1, d), F32)]
    return pl.pallas_call(
        body, name=name, grid=(n,),
        in_specs=[row, row, pl.BlockSpec((tm, 1), lambda i: (i, 0)), vec, row],
        out_specs=out_specs, out_shape=out_shape, scratch_shapes=[pltpu.VMEM((8, d), F32)],
        compiler_params=_params("arbitrary"),
    )(dy, x, r, g, skip)


def _causal(w):
    t = lax.broadcasted_iota(jnp.int32, w.shape, 0)
    s_ = lax.broadcasted_iota(jnp.int32, w.shape, 1)
    return jnp.where(t >= s_, w, 0.0)


def _gmlp_fwd(proj, w_s, b_st, ln_g, ln_b, g_out, d_model):
    s = proj.shape[0]
    wa = d_model // 2
    groups = wa // HEAD_DIM
    tm = _tile(s, 256)
    n_chunks = tm // CHUNK

    def body(au_ref, av_ref, az_ref, ws_ref, bst_ref, lng_ref, lnb_ref, go_ref,
             y_ref, z_ref, vn_ref, mu_ref, rs_ref, ra_ref):
        gv = _gelu(av_ref[...])
        mu = jnp.mean(gv, axis=-1, keepdims=True)
        xc = gv - mu
        rs = lax.rsqrt(jnp.mean(xc * xc, axis=-1, keepdims=True) + EPS)
        vn = (xc * rs * lng_ref[...] + lnb_ref[...]).astype(BF16)
        vn_ref[...] = vn
        mu_ref[...] = mu
        rs_ref[...] = rs
        for g in range(groups):
            wm = _causal(ws_ref[g]).astype(BF16)
            cols = slice(g * HEAD_DIM, (g + 1) * HEAD_DIM)
            for ch in range(n_chunks):
                rows = slice(ch * CHUNK, (ch + 1) * CHUNK)
                z_ref[rows, cols] = _dot(wm, vn_ref[rows, cols]) + bst_ref[:, g:g + 1]
        ya = _gelu(au_ref[...]) * z_ref[...]
        ra = lax.rsqrt(jnp.mean(ya * ya, axis=-1, keepdims=True) + EPS)
        ra_ref[...] = ra
        sz, _ = _silu_and_grad(az_ref[...])
        y_ref[...] = (ya * ra * go_ref[...] * sz).astype(BF16)

    def col(j):
        return pl.BlockSpec((tm, wa), lambda i: (i, j))
    vec = pl.BlockSpec((1, wa), lambda i: (0, 0))
    stat = pl.BlockSpec((tm, 1), lambda i: (i, 0))
    return pl.pallas_call(
        body, name="gmlp_fwd", grid=(s // tm,),
        in_specs=[col(0), col(1), col(2),
                  pl.BlockSpec((groups, CHUNK, CHUNK), lambda i: (0, 0, 0)),
                  pl.BlockSpec((CHUNK, groups), lambda i: (0, 0)), vec, vec, vec],
        out_specs=(col(0), col(0), col(0), stat, stat, stat),
        out_shape=(jax.ShapeDtypeStruct((s, d_model), BF16), jax.ShapeDtypeStruct((s, wa), F32),
                   jax.ShapeDtypeStruct((s, wa), BF16), jax.ShapeDtypeStruct((s, 1), F32),
                   jax.ShapeDtypeStruct((s, 1), F32), jax.ShapeDtypeStruct((s, 1), F32)),
        compiler_params=_params("parallel"),
    )(proj, proj, proj, w_s, b_st, ln_g, ln_b, g_out)


def _gmlp_bwd(dproj, dy, proj, z, vn, mu, rs, ra, w_s, ln_g, g_out, d_model):
    s = proj.shape[0]
    wa = d_model // 2
    groups = wa // HEAD_DIM
    tm = _tile(s, 256)
    n_chunks = tm // CHUNK
    n = s // tm

    def causal_stack(w):
        t = lax.broadcasted_iota(jnp.int32, w.shape, 1)
        s_ = lax.broadcasted_iota(jnp.int32, w.shape, 2)
        return jnp.where(t >= s_, w, 0.0)

    def body(dproj_in, dy_ref, au_ref, av_ref, az_ref, z_ref, vn_ref, mu_ref, rs_ref, ra_ref, ws_ref, lng_ref, go_ref,
             dp_ref, gws_ref, dzs_ref, glg_ref, glb_ref, ggo_ref,
             dz_s, dvn_s, acc_lg, acc_lb, acc_go):
        del dproj_in
        i = pl.program_id(0)

        @pl.when(i == 0)
        def _():
            gws_ref[...] = jnp.zeros_like(gws_ref)
            dzs_ref[...] = jnp.zeros_like(dzs_ref)
            acc_lg[...] = jnp.zeros_like(acc_lg)
            acc_lb[...] = jnp.zeros_like(acc_lb)
            acc_go[...] = jnp.zeros_like(acc_go)

        au, az, zv, rav = au_ref[...], az_ref[...], z_ref[...], ra_ref[...]
        u, du = _gelu_and_grad(au)
        ya = u * zv
        sz, dsz = _silu_and_grad(az)
        dyv = dy_ref[...].astype(F32)
        dp_ref[:, 2 * wa:3 * wa] = (dyv * (ya * rav * go_ref[...]) * dsz).astype(BF16)
        dn = dyv * sz
        acc_go[...] += _colsum8(dn * ya * rav)
        dyg = dn * go_ref[...]
        dya = rav * (dyg - ya * (rav * rav) * jnp.mean(dyg * ya, axis=-1, keepdims=True))
        dp_ref[:, 0:wa] = (dya * zv * du).astype(BF16)
        dz_s[...] = dya * u
        for ch in range(n_chunks):
            dzs_ref[...] += dz_s[ch * CHUNK:(ch + 1) * CHUNK, :]
        for g in range(groups):
            wm = _causal(ws_ref[g]).astype(BF16)
            cols = slice(g * HEAD_DIM, (g + 1) * HEAD_DIM)
            for ch in range(n_chunks):
                rows = slice(ch * CHUNK, (ch + 1) * CHUNK)
                dzb = dz_s[rows, cols].astype(BF16)
                gws_ref[g] += _dot(dzb, vn_ref[rows, cols], tb=True)
                dvn_s[rows, cols] = _dot(wm, dzb, ta=True)
        av = av_ref[...]
        gv, dgelu_v = _gelu_and_grad(av)
        xh = (gv - mu_ref[...]) * rs_ref[...]
        dvn = dvn_s[...]
        acc_lb[...] += _colsum8(dvn)
        acc_lg[...] += _colsum8(dvn * xh)
        dxh = dvn * lng_ref[...]
        dgv = rs_ref[...] * (dxh - jnp.mean(dxh, axis=-1, keepdims=True) - xh * jnp.mean(dxh * xh, axis=-1, keepdims=True))
        dp_ref[:, wa:2 * wa] = (dgv * dgelu_v).astype(BF16)

        @pl.when(i == n - 1)
        def _():
            gws_ref[...] = causal_stack(gws_ref[...])
            glg_ref[...] = jnp.sum(acc_lg[...], axis=0, keepdims=True)
            glb_ref[...] = jnp.sum(acc_lb[...], axis=0, keepdims=True)
            ggo_ref[...] = jnp.sum(acc_go[...], axis=0, keepdims=True)

    def col(j):
        return pl.BlockSpec((tm, wa), lambda i: (i, j))
    vec = pl.BlockSpec((1, wa), lambda i: (0, 0))
    stat = pl.BlockSpec((tm, 1), lambda i: (i, 0))
    ws_spec = pl.BlockSpec((groups, CHUNK, CHUNK), lambda i: (0, 0, 0))
    d_in = dproj.shape[1]
    return pl.pallas_call(
        body, name="gmlp_bwd", grid=(n,),
        in_specs=[ANY, col(0), col(0), col(1), col(2), col(0), col(0), stat, stat, stat, ws_spec, vec, vec],
        out_specs=(pl.BlockSpec((tm, 3 * wa), lambda i: (i, 0)), ws_spec,
                   pl.BlockSpec((CHUNK, wa), lambda i: (0, 0)), vec, vec, vec),
        out_shape=(jax.ShapeDtypeStruct((s, d_in), BF16),
                   jax.ShapeDtypeStruct((groups, CHUNK, CHUNK), F32), jax.ShapeDtypeStruct((CHUNK, wa), F32))
        + (jax.ShapeDtypeStruct((1, wa), F32),) * 3,
        scratch_shapes=[pltpu.VMEM((tm, wa), F32), pltpu.VMEM((tm, wa), F32)] + [pltpu.VMEM((8, wa), F32)] * 3,
        input_output_aliases={0: 0},
        compiler_params=_params("arbitrary"),
    )(dproj, dy, proj, proj, proj, z, vn, mu, rs, ra, w_s, ln_g, g_out)


def _gate_fwd(y_in, yb, proj, g_out, d_model):
    s = proj.shape[0]
    wa = d_model // 2
    tm = _tile(s, 256)

    cols, n_chunks = _column_chunks(wa)

    def body(y_any, yb_ref, bz_ref, go_ref, y_ref, rb_ref):
        del y_any
        total = jnp.zeros((tm, 1), F32)
        for c in range(n_chunks):
            ybc = yb_ref[:, cols(c)]
            total = total + jnp.sum(ybc * ybc, axis=-1, keepdims=True)
        rb = lax.rsqrt(total * (1.0 / wa) + EPS)
        rb_ref[...] = rb
        for c in range(n_chunks):
            sz, _ = _silu_and_grad(bz_ref[:, cols(c)])
            y_ref[:, cols(c)] = (yb_ref[:, cols(c)] * rb * go_ref[:, cols(c)] * sz).astype(BF16)

    return pl.pallas_call(
        body, name="gate_b_fwd", grid=(s // tm,),
        in_specs=[ANY, pl.BlockSpec((tm, wa), lambda i: (i, 0)), pl.BlockSpec((tm, wa), lambda i: (i, 6)),
                  pl.BlockSpec((1, wa), lambda i: (0, 0))],
        out_specs=(pl.BlockSpec((tm, wa), lambda i: (i, 1)), pl.BlockSpec((tm, 1), lambda i: (i, 0))),
        out_shape=(jax.ShapeDtypeStruct((s, d_model), BF16), jax.ShapeDtypeStruct((s, 1), F32)),
        input_output_aliases={0: 0},
        compiler_params=_params("parallel"),
    )(y_in, yb, proj, g_out)


def _gate_bwd(dy, yb, lse, proj, rb, g_out, d_model):
    s, d_in = proj.shape
    wa = d_model // 2
    heads = wa // HEAD_DIM
    tm = _tile(s, 256)
    n = s // tm
    cols, n_chunks = _column_chunks(wa)

    def body(dy_ref, yb_ref, lse_ref, bz_ref, rb_ref, go_ref, dp_ref, do_ref, st_ref, ggo_ref, acc):
        i = pl.program_id(0)

        @pl.when(i == 0)
        def _():
            acc[...] = jnp.zeros_like(acc)

        rbv = rb_ref[...]
        total = jnp.zeros((tm, 1), F32)
        for c in range(n_chunks):
            dyc, ybc = dy_ref[:, cols(c)].astype(F32), yb_ref[:, cols(c)]
            sz, dsz = _silu_and_grad(bz_ref[:, cols(c)])
            dp_ref[:, cols(c)] = (dyc * (ybc * rbv * go_ref[:, cols(c)]) * dsz).astype(BF16)
            dn = dyc * sz
            acc[:, cols(c)] += _colsum8(dn * ybc * rbv)
            total = total + jnp.sum(dn * go_ref[:, cols(c)] * ybc, axis=-1, keepdims=True)
        coef = (rbv * rbv) * (total * (1.0 / wa))
        first = lax.broadcasted_iota(jnp.int32, (tm, 2), 1) == 0
        per_chunk = (wa // n_chunks) // HEAD_DIM
        for c in range(n_chunks):
            dyc, ybc = dy_ref[:, cols(c)].astype(F32), yb_ref[:, cols(c)]
            sz, _ = _silu_and_grad(bz_ref[:, cols(c)])
            do = rbv * (dyc * sz * go_ref[:, cols(c)] - ybc * coef)
            do_ref[:, cols(c)] = do
            prod = do * ybc
            for k in range(per_chunk):
                h = c * per_chunk + k
                delta = jnp.sum(prod[:, k * HEAD_DIM:(k + 1) * HEAD_DIM], axis=-1, keepdims=True)
                st_ref[h] = jnp.where(first, lse_ref[h], delta)

        @pl.when(i == n - 1)
        def _():
            ggo_ref[...] = jnp.sum(acc[...], axis=0, keepdims=True)

    vec = pl.BlockSpec((1, wa), lambda i: (0, 0))
    return pl.pallas_call(
        body, name="gate_b_bwd", grid=(n,),
        in_specs=[pl.BlockSpec((tm, wa), lambda i: (i, 1)), pl.BlockSpec((tm, wa), lambda i: (i, 0)),
                  pl.BlockSpec((heads, tm, 1), lambda i: (0, i, 0)),
                  pl.BlockSpec((tm, wa), lambda i: (i, 6)), pl.BlockSpec((tm, 1), lambda i: (i, 0)), vec],
        out_specs=(pl.BlockSpec((tm, wa), lambda i: (i, 6)), pl.BlockSpec((tm, wa), lambda i: (i, 0)),
                   pl.BlockSpec((heads, tm, 2), lambda i: (0, i, 0)), vec),
        out_shape=(jax.ShapeDtypeStruct((s, d_in), BF16), jax.ShapeDtypeStruct((s, wa), F32),
                   jax.ShapeDtypeStruct((heads, s, 2), F32), jax.ShapeDtypeStruct((1, wa), F32)),
        scratch_shapes=[pltpu.VMEM((8, wa), F32)],
        compiler_params=_params("arbitrary"),
    )(dy, yb, lse, proj, rb, g_out)


def _bucket_tables():
    qi = BLK + np.arange(BLK)
    kj = np.arange(2 * BLK)
    delta = qi[:, None] - kj[None, :]
    max_exact = NUM_BUCKETS // 2
    tabs = []
    for window, dil in DILATED:
        band = (delta >= 0) & (delta <= window // dil)
        dist = np.clip(delta, 0, None) * dil
        d = np.maximum(dist, 1).astype(np.float32)
        large = max_exact + (np.log(d / np.float32(max_exact)) / np.float32(math.log(MAX_DISTANCE / max_exact))
                             * np.float32(NUM_BUCKETS - max_exact)).astype(np.int32)
        large = np.minimum(large, NUM_BUCKETS - 1)
        tabs.append(np.where(band, np.where(dist < max_exact, dist, large), -1).astype(np.int32))
    return np.stack(tabs)


def _bias_tiles(tab_ref, rb_ref, h, bias_s):
    col = lax.broadcasted_iota(jnp.int32, (BLK, 2 * BLK), 1)
    for i in range(len(DILATED)):
        t = tab_ref[i]
        bias = jnp.zeros(t.shape, F32)
        for b in range(NUM_BUCKETS):
            bias = jnp.where(t == b, rb_ref[b, h], bias)
        bias = jnp.where(t >= 0, bias, NEG_INF)
        bias_s[2 * i] = jnp.where(col >= BLK, bias, NEG_INF)
        bias_s[2 * i + 1] = bias


def _block_rows(b, n_blocks, dil):
    nb = n_blocks // dil
    r = b // nb
    n = b % nb
    start = r + dil * BLK * n
    if dil == 1:
        return pl.ds(pl.multiple_of(start, BLK), BLK), n
    return pl.ds(start, BLK, stride=dil), n


def _sub_block(b, rows=BLK, pad=0):
    return pl.ds(pl.multiple_of(b * BLK + pad, BLK), rows)


def _regroup(b, n_blocks, small, big):
    ratio = big // small
    piece = BLK // ratio
    nb_small, nb_big = n_blocks // small, n_blocks // big
    r, n = b // nb_small, b % nb_small
    for q in range(ratio):
        block = (r + small * q) * nb_big + n // ratio
        yield q, ratio, piece, pl.multiple_of(block * BLK + piece * (n % ratio), piece)


def _rms_rows(x, gain):
    return x * lax.rsqrt(jnp.mean(x * x, axis=-1, keepdims=True) + EPS) * gain


def _attn_fwd(proj, g_q, g_k, rel_bias, d_model, comm=None):
    s = proj.shape[0]
    heads = d_model // 2 // HEAD_DIM
    n_blocks = s // BLK
    scale = HEAD_DIM ** -0.5
    tables = jnp.asarray(_bucket_tables())
    c_ins = list(comm.ins) if comm else []
    c_outs = list(comm.out_shape) if comm else []
    c_scratch = list(comm.scratch) if comm else []

    def body(q_ref, k_ref, v_ref, gq_ref, gk_ref, rb_ref, tab_ref, *rest):
        cin_refs, rest = rest[:len(c_ins)], rest[len(c_ins):]
        yb_ref, lse_ref = rest[:2]
        cout_refs, rest = rest[2:2 + len(c_outs)], rest[2 + len(c_outs):]
        qs, ks, vs, qs_w, ks_w, vs_w, tmp, m_s, l_s, acc_s, bias_s, sc_s, p_s = rest[:13]
        sems = rest[13:]
        h = pl.program_id(0)
        if comm:
            @pl.when(h == 0)
            def _():
                comm.start(cin_refs, cout_refs, sems)

        _bias_tiles(tab_ref, rb_ref, h, bias_s)
        gq = gq_ref[...] * scale
        gk = gk_ref[...]
        for buf in (ks, vs, ks_w, vs_w):
            buf[0:BLK, :] = jnp.zeros((BLK, HEAD_DIM), BF16)

        for i, (_, dil) in enumerate(DILATED):
            def prepare(b, carry, i=i, dil=dil):
                rows, _ = _block_rows(b, n_blocks, dil)
                vals = (_rms_rows(q_ref[rows, :], gq), _rms_rows(k_ref[rows, :], gk), v_ref[rows, :])
                for val, buf, pad in zip(vals, (qs, ks, vs), (0, BLK, BLK)):
                    buf[_sub_block(b, pad=pad), :] = val.astype(BF16)
                if i == 1:
                    slot = b % 2
                    for a, val in enumerate(vals):
                        tmp[slot, a] = val
                    for q, ratio, piece, dst in _regroup(b, n_blocks, dil, DILATED[2][1]):
                        for a, (buf, pad) in enumerate(((qs_w, 0), (ks_w, BLK), (vs_w, BLK))):
                            buf[pl.ds(dst + pad, piece), :] = tmp[slot, a, pl.ds(q, piece, stride=ratio), :].astype(BF16)
                return carry
            if i < 2:
                lax.fori_loop(0, n_blocks, prepare, 0, unroll=8)
            qs, ks, vs = (qs, ks, vs) if i < 2 else (qs_w, ks_w, vs_w)

            def group(g, carry, i=i, dil=dil, qs=qs, ks=ks, vs=vs):
                blocks = [g * ATTN_GROUP_FWD + t for t in range(ATTN_GROUP_FWD)]
                for t, b in enumerate(blocks):
                    sc_s[t] = _dot(qs[_sub_block(b), :], ks[_sub_block(b, 2 * BLK), :], tb=True)
                for t, b in enumerate(blocks):
                    rows, n = _block_rows(b, n_blocks, dil)
                    sc = sc_s[t] + bias_s[2 * i + jnp.minimum(n, 1)]
                    m_i = jnp.max(sc, axis=-1, keepdims=True)
                    pr = jnp.exp(sc - m_i)
                    l_i = jnp.sum(pr, axis=-1, keepdims=True)
                    p_s[t] = pr.astype(BF16)
                    m_s[new, rows, :] = m_i
                    l_s[new, rows, :] = l_i
                for t, b in enumerate(blocks):
                    rows, _ = _block_rows(b, n_blocks, dil)
                    acc_s[new, rows, :] = _dot(p_s[t], vs[_sub_block(b, 2 * BLK), :])
                return carry
            new = min(i, 1)
            lax.fori_loop(0, n_blocks // ATTN_GROUP_FWD, group, 0)
            if i > 0:
                m = jnp.maximum(m_s[0], m_s[1])
                w_old, w_new = jnp.exp(m_s[0] - m), jnp.exp(m_s[1] - m)
                m_s[0] = m
                l_s[0] = l_s[0] * w_old + l_s[1] * w_new
                acc_s[0] = acc_s[0] * w_old + acc_s[1] * w_new

        l = l_s[0]
        yb_ref[...] = acc_s[0] / l
        lse_ref[...] = m_s[0] + jnp.log(l)
        if comm:
            @pl.when(h == heads - 1)
            def _():
                comm.finish(cin_refs, cout_refs, sems)

    def head_col(off):
        return pl.BlockSpec((s, HEAD_DIM), lambda h: (0, off * heads + h))
    vec = pl.BlockSpec((1, HEAD_DIM), lambda h: (0, 0))
    res = pl.pallas_call(
        body, name="attn_fwd", grid=(heads,),
        in_specs=[head_col(3), head_col(4), head_col(5), vec, vec,
                  pl.BlockSpec(memory_space=pltpu.SMEM),
                  pl.BlockSpec((len(DILATED), BLK, 2 * BLK), lambda h: (0, 0, 0))] + [ANY] * len(c_ins),
        out_specs=[pl.BlockSpec((s, HEAD_DIM), lambda h: (0, h)), pl.BlockSpec((None, s, 1), lambda h: (h, 0, 0))]
        + [ANY] * len(c_outs),
        out_shape=[jax.ShapeDtypeStruct((s, heads * HEAD_DIM), F32), jax.ShapeDtypeStruct((heads, s, 1), F32)] + c_outs,
        scratch_shapes=[pltpu.VMEM((s, HEAD_DIM), BF16), pltpu.VMEM((s + BLK, HEAD_DIM), BF16),
                        pltpu.VMEM((s + BLK, HEAD_DIM), BF16),
                        pltpu.VMEM((s, HEAD_DIM), BF16), pltpu.VMEM((s + BLK, HEAD_DIM), BF16),
                        pltpu.VMEM((s + BLK, HEAD_DIM), BF16), pltpu.VMEM((2, 3, BLK, HEAD_DIM), F32),
                        pltpu.VMEM((2, s, 1), F32), pltpu.VMEM((2, s, 1), F32), pltpu.VMEM((2, s, HEAD_DIM), F32),
                        pltpu.VMEM((2 * len(DILATED), BLK, 2 * BLK), F32),
                        pltpu.VMEM((ATTN_GROUP_FWD, BLK, 2 * BLK), F32),
                        pltpu.VMEM((ATTN_GROUP_FWD, BLK, 2 * BLK), BF16)] + c_scratch,
        input_output_aliases={7 + i: 2 + i for i in range(len(c_ins))} if (comm and comm.in_place) else {},
        compiler_params=_params("arbitrary"),
    )(proj, proj, proj, g_q, g_k, rel_bias, tables, *c_ins)
    return res[0], res[1], res[2:]


def _attn_bwd(dproj, d_o, stats, proj, g_q, g_k, rel_bias, d_model, comm=None):
    s, d_in = proj.shape
    heads = d_model // 2 // HEAD_DIM
    n_blocks = s // BLK
    scale = HEAD_DIM ** -0.5
    tables = jnp.asarray(_bucket_tables())
    n_dil = len(DILATED)
    c_ins = list(comm.ins) if comm else []
    c_outs = list(comm.out_shape) if comm else []
    c_scratch = list(comm.scratch) if comm else []

    def body(dp_any, q_ref, k_ref, v_ref, do_ref, st_ref, gq_ref, gk_ref, rb_ref, tab_ref, *rest):
        cin_refs, rest = rest[:len(c_ins)], rest[len(c_ins):]
        dp_out, ggq_ref, ggk_ref, db_ref = rest[:4]
        cout_refs, rest = rest[4:4 + len(c_outs)], rest[4 + len(c_outs):]
        (qs, ks, vs, dos, st_s, qs_w, ks_w, vs_w, dos_w, st_w, tmp, tmp_st,
         dqn, dkn, dvv, dq_u, dk_u, dv_u, bias_s, dbias_s, sc_s, dp_s, p_s, ds_s, sems) = rest[:25]
        comm_sems = rest[25:]
        del dp_any
        h = pl.program_id(0)
        if comm:
            @pl.when(h == 0)
            def _():
                comm.start(cin_refs, cout_refs, comm_sems)


        @pl.when(h == 0)
        def _():
            ggq_ref[...] = jnp.zeros_like(ggq_ref)
            ggk_ref[...] = jnp.zeros_like(ggk_ref)

        dbias_s[...] = jnp.zeros_like(dbias_s)
        _bias_tiles(tab_ref, rb_ref, h, bias_s)
        gq = gq_ref[...] * scale
        gk = gk_ref[...]
        for buf in (ks, vs, ks_w, vs_w):
            buf[0:BLK, :] = jnp.zeros((BLK, HEAD_DIM), BF16)

        for i, (_, dil) in enumerate(DILATED):
            def prepare(b, carry, i=i, dil=dil):
                rows, _ = _block_rows(b, n_blocks, dil)
                vals = (_rms_rows(q_ref[rows, :], gq), _rms_rows(k_ref[rows, :], gk), v_ref[rows, :], do_ref[rows, :])
                for val, buf, pad in zip(vals, (qs, ks, vs, dos), (0, BLK, BLK, 0)):
                    buf[_sub_block(b, pad=pad), :] = val.astype(BF16)
                st = st_ref[rows, :]
                st_s[_sub_block(b), :] = st
                if i == 1:
                    slot = b % 2
                    for a, val in enumerate(vals):
                        tmp[slot, a] = val
                    tmp_st[slot] = st
                    for q, ratio, piece, dst in _regroup(b, n_blocks, dil, DILATED[2][1]):
                        for a, (buf, pad) in enumerate(((qs_w, 0), (ks_w, BLK), (vs_w, BLK), (dos_w, 0))):
                            buf[pl.ds(dst + pad, piece), :] = tmp[slot, a, pl.ds(q, piece, stride=ratio), :].astype(BF16)
                        st_w[pl.ds(dst, piece), :] = tmp_st[slot, pl.ds(q, piece, stride=ratio), :]
                return carry
            if i < 2:
                lax.fori_loop(0, n_blocks, prepare, 0, unroll=8)
            dk_u[0:BLK, :] = jnp.zeros((BLK, HEAD_DIM), F32)
            dv_u[0:BLK, :] = jnp.zeros((BLK, HEAD_DIM), F32)
            operands = (qs, ks, vs, dos, st_s) if i < 2 else (qs_w, ks_w, vs_w, dos_w, st_w)

            def group(g, carry, i=i, dil=dil, operands=operands):
                qs, ks, vs, dos, st_s = operands
                blocks = [g * ATTN_GROUP + t for t in range(ATTN_GROUP)]
                for t, b in enumerate(blocks):
                    sc_s[t] = _dot(qs[_sub_block(b), :], ks[_sub_block(b, 2 * BLK), :], tb=True)
                    dp_s[t] = _dot(dos[_sub_block(b), :], vs[_sub_block(b, 2 * BLK), :], tb=True)
                for t, b in enumerate(blocks):
                    _, n = _block_rows(b, n_blocks, dil)
                    st = st_s[_sub_block(b), :]
                    pr = jnp.exp(sc_s[t] + bias_s[2 * i + jnp.minimum(n, 1)] - st[:, 0:1])
                    ds = pr * (dp_s[t] - st[:, 1:2])
                    dbias_s[i] += ds
                    p_s[t] = pr.astype(BF16)
                    ds_s[t] = ds.astype(BF16)
                for t, b in enumerate(blocks):
                    dq_u[_sub_block(b), :] = _dot(ds_s[t], ks[_sub_block(b, 2 * BLK), :])
                    for part, lhs, rhs in ((dk_u, ds_s, qs), (dv_u, p_s, dos)):
                        both = _dot(lhs[t], rhs[_sub_block(b), :], ta=True)
                        part[_sub_block(b), :] += both[:BLK]
                        part[_sub_block(b, pad=BLK), :] = both[BLK:]
                return carry
            lax.fori_loop(0, n_blocks // ATTN_GROUP, group, 0)

            def scatter(b, carry, i=i, dil=dil):
                if i < 2:
                    rows, _ = _block_rows(b, n_blocks, dil)
                    for acc, part, pad in ((dqn, dq_u, 0), (dkn, dk_u, BLK), (dvv, dv_u, BLK)):
                        val = part[_sub_block(b, pad=pad), :]
                        acc[rows, :] = val if i == 0 else acc[rows, :] + val
                else:
                    middle = DILATED[1][1]
                    rows, _ = _block_rows(b, n_blocks, middle)
                    slot = b % 2
                    for a, (acc, part, pad) in enumerate(((dqn, dq_u, 0), (dkn, dk_u, BLK), (dvv, dv_u, BLK))):
                        for q, ratio, piece, src in _regroup(b, n_blocks, middle, dil):
                            tmp[slot, a, pl.ds(q, piece, stride=ratio), :] = part[pl.ds(src + pad, piece), :]
                        acc[rows, :] += tmp[slot, a]
                return carry
            lax.fori_loop(0, n_blocks, scatter, 0, unroll=8)

        q = q_ref[...]
        rq = lax.rsqrt(jnp.mean(q * q, axis=-1, keepdims=True) + EPS)
        k = k_ref[...]
        rk = lax.rsqrt(jnp.mean(k * k, axis=-1, keepdims=True) + EPS)
        dq_n = dqn[...]
        ggq_ref[...] += jnp.sum(dq_n * (q * rq) * scale, axis=0, keepdims=True)
        dg = dq_n * gq_ref[...] * scale
        qs[...] = (rq * (dg - q * (rq * rq) * jnp.mean(dg * q, axis=-1, keepdims=True))).astype(BF16)
        dk_n = dkn[...]
        ggk_ref[...] += jnp.sum(dk_n * (k * rk), axis=0, keepdims=True)
        dg = dk_n * gk_ref[...]
        dos[...] = (rk * (dg - k * (rk * rk) * jnp.mean(dg * k, axis=-1, keepdims=True))).astype(BF16)
        vs[BLK:, :] = dvv[...].astype(BF16)
        copies = [pltpu.make_async_copy(src, dp_out.at[:, pl.ds(pl.multiple_of(((3 + j) * heads + h) * HEAD_DIM, HEAD_DIM), HEAD_DIM)],
                                        sems.at[j]) for j, src in enumerate((qs, dos, vs.at[pl.ds(BLK, s)]))]
        for cp in copies:
            cp.start()
        for b in range(NUM_BUCKETS):
            tot = jnp.zeros((BLK, 2 * BLK), F32)
            for i in range(n_dil):
                tot = tot + jnp.where(tab_ref[i] == b, dbias_s[i], 0.0)
            db_ref[b:b + 1, :] = jnp.full((1, LANE), jnp.sum(tot), F32)
        for cp in copies:
            cp.wait()
        if comm:
            @pl.when(h == heads - 1)
            def _():
                comm.finish(cin_refs, cout_refs, comm_sems)

    def head_col(off):
        return pl.BlockSpec((s, HEAD_DIM), lambda h: (0, off * heads + h))
    own_col = pl.BlockSpec((s, HEAD_DIM), lambda h: (0, h))
    vec = pl.BlockSpec((1, HEAD_DIM), lambda h: (0, 0))
    big = pltpu.VMEM((s, HEAD_DIM), F32)
    padded32 = pltpu.VMEM((s + BLK, HEAD_DIM), F32)
    padded16 = pltpu.VMEM((s + BLK, HEAD_DIM), BF16)
    res = pl.pallas_call(
        body, name="attn_bwd", grid=(heads,),
        in_specs=[ANY, head_col(3), head_col(4), head_col(5), own_col,
                  pl.BlockSpec((None, s, 2), lambda h: (h, 0, 0)), vec, vec,
                  pl.BlockSpec(memory_space=pltpu.SMEM),
                  pl.BlockSpec((n_dil, BLK, 2 * BLK), lambda h: (0, 0, 0))] + [ANY] * len(c_ins),
        out_specs=[ANY, vec, vec, pl.BlockSpec((None, NUM_BUCKETS, LANE), lambda h: (h, 0, 0))] + [ANY] * len(c_outs),
        out_shape=[jax.ShapeDtypeStruct((s, d_in), BF16), jax.ShapeDtypeStruct((1, HEAD_DIM), F32),
                   jax.ShapeDtypeStruct((1, HEAD_DIM), F32), jax.ShapeDtypeStruct((heads, NUM_BUCKETS, LANE), F32)]
        + c_outs,
        scratch_shapes=[pltpu.VMEM((s, HEAD_DIM), BF16), padded16, padded16, pltpu.VMEM((s, HEAD_DIM), BF16),
                        pltpu.VMEM((s, 2), F32),
                        pltpu.VMEM((s, HEAD_DIM), BF16), padded16, padded16, pltpu.VMEM((s, HEAD_DIM), BF16),
                        pltpu.VMEM((s, 2), F32),
                        pltpu.VMEM((2, 4, BLK, HEAD_DIM), F32), pltpu.VMEM((2, BLK, 2), F32),
                        big, big, big, big, padded32, padded32,
                        pltpu.VMEM((2 * n_dil, BLK, 2 * BLK), F32), pltpu.VMEM((n_dil, BLK, 2 * BLK), F32),
                        pltpu.VMEM((ATTN_GROUP, BLK, 2 * BLK), F32), pltpu.VMEM((ATTN_GROUP, BLK, 2 * BLK), F32),
                        pltpu.VMEM((ATTN_GROUP, BLK, 2 * BLK), BF16), pltpu.VMEM((ATTN_GROUP, BLK, 2 * BLK), BF16),
                        pltpu.SemaphoreType.DMA((3,))] + c_scratch,
        input_output_aliases={0: 0},
        compiler_params=_params("arbitrary"),
    )(dproj, proj, proj, proj, d_o, stats, g_q, g_k, rel_bias, tables, *c_ins)
    return res[0], res[1], res[2], res[3], res[4:]


def _local_step(xs, ps, tgt, sp, win4, wout4, wgate4, wup4, dist=None):
    s, d = xs.shape
    wa = d // 2
    heads = wa // HEAD_DIM
    d_in = 7 * wa
    c4 = d_in // N_CHIPS
    dple = ps.shape[1]

    tm, tn = _tile(s, 1024), _tile(d, 1024)
    tn_in = _tile(c4, 1024)
    per_in = c4 // tn_in
    tn_up = _tile(d // N_CHIPS, 512)
    per_up = (d // N_CHIPS) // tn_up
    gm, gn = s // tm, d // tn

    hn, rx = _rms_fwd("rms_pre", xs, sp["g_pre"])
    if dist:
        proj, win4 = _mm_in_gather(dist["chip"], hn, win4)
    else:
        proj = _matmul(
            "mm_in", hn, win4, pl.BlockSpec((tm, d), lambda i, j, k: (i, 0)), _shard_spec(d, tn_in, per_in, 2, 1),
            (gm, d_in // tn_in, 1), None, jax.ShapeDtypeStruct((s, d_in), F32),
            pl.BlockSpec((tm, tn_in), lambda i, j, k: (i, j)))
    b_st = sp["b_s"].T
    y, z, vn, mu_v, rs_v, ra = _gmlp_fwd(proj, sp["w_s"], b_st, sp["ln_v_g"], sp["ln_v_b"], sp["g_out_a"], d)
    yb, lse, gathered = _attn_fwd(proj, sp["g_q"], sp["g_k"], sp["rel_bias"], d,
                                  comm=_Gather([wout4, wgate4, wup4]) if dist else None)
    if dist:
        wout4, wgate4, wup4 = gathered
    wout, wgate = wout4.reshape(d, d), wgate4.reshape(d, d)
    y, rb = _gate_fwd(y, yb, proj, sp["g_out_b"], d)

    def residual(acc, extra, outs):
        outs[0][...] = extra[0][...] + acc

    tile = pl.BlockSpec((tm, tn), lambda i, j, k: (i, j))
    tile_up = pl.BlockSpec((tm, tn_up), lambda i, j, k: (i, j))
    h = _matmul(
        "mm_out", y, wout, pl.BlockSpec((tm, d), lambda i, j, k: (i, 0)), pl.BlockSpec((d, tn), lambda i, j, k: (0, j)),
        (gm, gn, 1), None, jax.ShapeDtypeStruct((s, d), F32), tile, extras=(xs,), extra_specs=(tile,),
        epilogue=residual)
    hp, rh = _rms_fwd("rms_ple", h, sp["g_ple"])
    p16 = ps.astype(BF16)

    sub = _tile(tn_up, 256)

    def head(a_ref, b_ref, extra, outs):
        p_ref, wup_ref, h_ref, t_ref = extra
        dout_ref, dgl_ref, dup_ref, loss_ref = outs
        part = jnp.zeros((8, LANE), F32)
        for piece in range(tn_up // sub):
            cs = slice(piece * sub, (piece + 1) * sub)
            acc = _dot(a_ref[...], b_ref[:, cs])
            up = _dot(p_ref[...], wup_ref[:, cs])
            gate = jax.nn.sigmoid(acc)
            err = h_ref[:, cs] + gate * up - t_ref[:, cs]
            sq = _colsum8(err * err)
            for c in range(sub // LANE):
                part = part + sq[:, c * LANE:(c + 1) * LANE]
            dout = err * (1.0 / d)
            dout_ref[:, cs] = dout.astype(BF16)
            dup_ref[:, cs] = (dout * gate).astype(BF16)
            dgl_ref[:, cs] = (dout * up * gate * (1.0 - gate)).astype(BF16)
        loss_ref[...] = part

    tile_up = pl.BlockSpec((tm, tn_up), lambda i, j, k: (i, j))
    dout, dglin, dup, loss_parts = _matmul(
        "mm_gate_loss", hp, wgate, pl.BlockSpec((tm, d), lambda i, j, k: (i, 0)), pl.BlockSpec((d, tn_up), lambda i, j, k: (0, j)),
        (gm, d // tn_up, 1), None,
        (jax.ShapeDtypeStruct((s, d), BF16), jax.ShapeDtypeStruct((s, d), BF16), jax.ShapeDtypeStruct((s, d), BF16),
         jax.ShapeDtypeStruct((gm * 8, (d // tn_up) * LANE), F32)),
        (tile_up, tile_up, tile_up, pl.BlockSpec((8, LANE), lambda i, j, k: (i, j))),
        extras=(p16, wup4, h, tgt),
        extra_specs=(pl.BlockSpec((tm, dple), lambda i, j, k: (i, 0)), _shard_spec(dple, tn_up, per_up, 2, 1),
                     tile_up, tile_up),
        fused=head)
    loss = 0.5 * jnp.sum(loss_parts) * (1.0 / d)

    dhp = _matmul(
        "mm_dhp", dglin, wgate, pl.BlockSpec((tm, d), lambda i, j, k: (i, 0)), pl.BlockSpec((tn, d), lambda i, j, k: (j, 0)),
        (gm, gn, 1), None, jax.ShapeDtypeStruct((s, d), BF16), tile, tb=True)
    dh, dh16, g_g_ple = _rms_bwd("rms_ple_bwd", dhp, h, rh, sp["g_ple"], dout, True)
    tmw = _tile(d, 1024)
    g_wout = _matmul(
        "mm_gwout", y, dh16, pl.BlockSpec((s, tmw), lambda i, j, k: (0, i)), pl.BlockSpec((s, tn), lambda i, j, k: (0, j)),
        (d // tmw, gn, 1), None, jax.ShapeDtypeStruct((d, d), F32), pl.BlockSpec((tmw, tn), lambda i, j, k: (i, j)), ta=True)
    g_wout4 = g_wout.reshape(N_CHIPS, d // N_CHIPS, d)
    g_wgate = _matmul(
        "mm_gwgate", hp, dglin, pl.BlockSpec((s, tmw), lambda i, j, k: (0, i)), pl.BlockSpec((s, tn), lambda i, j, k: (0, j)),
        (d // tmw, gn, 1), None, jax.ShapeDtypeStruct((d, d), F32), pl.BlockSpec((tmw, tn), lambda i, j, k: (i, j)), ta=True,
        comm=_SwapHalves([g_wout4]) if dist else None)
    if dist:
        g_wgate, swapped_out = g_wgate
    g_wgate4 = g_wgate.reshape(N_CHIPS, d // N_CHIPS, d)
    g_wup = _matmul(
        "mm_gwup", p16, dup, pl.BlockSpec((s, dple), lambda i, j, k: (0, 0)), pl.BlockSpec((s, tn_up), lambda i, j, k: (0, j)),
        (1, d // tn_up, 1), None, jax.ShapeDtypeStruct((N_CHIPS, dple, d // N_CHIPS), F32),
        _shard_spec(dple, tn_up, per_up, 0, 1), ta=True)
    dy = _matmul(
        "mm_dy", dh16, wout, pl.BlockSpec((tm, d), lambda i, j, k: (i, 0)), pl.BlockSpec((tn, d), lambda i, j, k: (j, 0)),
        (gm, gn, 1), None, jax.ShapeDtypeStruct((s, d), BF16), tile, tb=True,
        comm=_SwapHalves([g_wgate4, g_wup]) if dist else None)
    early = (g_wout4, g_wgate4, g_wup)
    if dist:
        core = dist["core"]
        dy, swapped_rest = dy
        early_sums = [_chip_sum("chip_sum_" + n, dist["chip"], core, g, o)
                      for n, g, o in zip(LARGE[1:], early, list(swapped_out) + list(swapped_rest))]

    dproj, d_o, stats, g_g_out_b = _gate_bwd(dy, yb, lse, proj, rb, sp["g_out_b"], d)
    dproj, g_g_q, g_g_k, dbias, early_got = _attn_bwd(
        dproj, d_o, stats, proj, sp["g_q"], sp["g_k"], sp["rel_bias"], d,
        comm=_ExchangeChips([s16 for _, s16 in early_sums]) if dist else None)
    dproj, g_w_s, dz_sum, g_ln_g, g_ln_b, g_g_out_a = _gmlp_bwd(
        dproj, dy, proj, z, vn, mu_v, rs_v, ra, sp["w_s"], sp["ln_v_g"], sp["g_out_a"], d)

    if dist:
        tmw_half = _tile(d // 2, 1024)
        n_half = (d // 2) // tmw_half

        def half_of_gwin(name, own, comm):
            def rows(i, j, k, core_ref):
                return (0, (core_ref[0] if own else 1 - core_ref[0]) * n_half + i)
            return _matmul(
                name, hn, dproj, pl.BlockSpec((s, tmw_half), rows),
                pl.BlockSpec((s, tn_in), lambda i, j, k, core_ref: (0, j)),
                (n_half, d_in // tn_in, 1), None, jax.ShapeDtypeStruct((N_CHIPS, d // 2, c4), F32 if own else BF16),
                pl.BlockSpec((None, tmw_half, tn_in), lambda i, j, k, core_ref: (j // per_in, i, j % per_in)), ta=True,
                comm=comm, prefetch=(core,))

        g_theirs = half_of_gwin("mm_gwin_theirs", False, None)
        g_mine, (g_from_sibling,) = half_of_gwin("mm_gwin_mine", True, _SwapHalves([g_theirs], whole=True))
        win_sum = _chip_sum("chip_sum_w_in", dist["chip"], core, g_mine, g_from_sibling)
    else:
        g_win = _matmul(
            "mm_gwin", hn, dproj, pl.BlockSpec((s, tmw), lambda i, j, k: (0, i)),
            pl.BlockSpec((s, tn_in), lambda i, j, k: (0, j)),
            (d // tmw, d_in // tn_in, 1), None, jax.ShapeDtypeStruct((N_CHIPS, d, c4), F32),
            _shard_spec(tmw, tn_in, per_in, 0, 1), ta=True)
    tk = c4
    tmh, tnh = _tile(s, 1024), _tile(d, 1024)
    dhn = _matmul(
        "mm_dhn", dproj, win4, pl.BlockSpec((tmh, tk), lambda i, j, k: (i, k)), _shard_spec(tnh, tk, 1, 1, 2),
        (s // tmh, d // tnh, d_in // tk), (tmh, tnh), jax.ShapeDtypeStruct((s, d), BF16),
        pl.BlockSpec((tmh, tnh), lambda i, j, k: (i, j)), tb=True,
        comm=_ExchangeChips([win_sum[1]]) if dist else None)
    if dist:
        dhn, win_got = dhn
    grad_x, g_g_pre = _rms_bwd("rms_pre_bwd", dhn, xs, rx, sp["g_pre"], dh, False)

    small = {
        "g_pre": g_g_pre,
        "w_s": g_w_s,
        "b_s": jnp.sum(dz_sum.reshape(CHUNK, heads, HEAD_DIM), axis=-1).T,
        "ln_v_g": g_ln_g, "ln_v_b": g_ln_b,
        "g_q": g_g_q, "g_k": g_g_k,
        "rel_bias": dbias[:, :, 0].T,
        "g_out_a": g_g_out_a, "g_out_b": g_g_out_b,
        "g_ple": g_g_ple,
    }
    if not dist:
        return loss, grad_x, (g_win, *early), small
    sums32 = [win_sum[0]] + [s32 for s32, _ in early_sums]
    return loss, grad_x, (sums32, list(win_got) + list(early_got)), small


def _place():
    x, y, c = lax.axis_index("x"), lax.axis_index("y"), lax.axis_index("c")
    chips = [(1 - x, y), (x, 1 - y), (1 - x, 1 - y)]
    return x, y, c, chips


def _remote(src, dst, send_sems, recv_sems, k, to):
    return pltpu.make_async_remote_copy(src_ref=src, dst_ref=dst, send_sem=send_sems.at[k], recv_sem=recv_sems.at[k],
                                        device_id=to, device_id_type=MESH)


def _cast_bf16(name, chip, w):
    r, c = w.shape
    tm = _tile(r, 256)

    def body(chip_ref, w_ref, o_ref):
        del chip_ref
        o_ref[...] = w_ref[...].astype(BF16)

    return pl.pallas_call(
        body, name=name,
        grid_spec=pltpu.PrefetchScalarGridSpec(
            num_scalar_prefetch=1, grid=(r // tm,),
            in_specs=[pl.BlockSpec((tm, c), lambda i, chip_ref: (i, 0))],
            out_specs=pl.BlockSpec((None, tm, c), lambda i, chip_ref: (chip_ref[0], i, 0))),
        out_shape=jax.ShapeDtypeStruct((N_CHIPS, r, c), BF16), compiler_params=_params("parallel"),
    )(chip, w)


class _Gather:
    in_place = True

    def __init__(self, fulls):
        self.ins = list(fulls)
        self.out_shape = [jax.ShapeDtypeStruct(f.shape, f.dtype) for f in fulls]
        n = len(fulls)
        self.scratch = [pltpu.SemaphoreType.DMA((6 * n,)), pltpu.SemaphoreType.DMA((6 * n,))]

    @staticmethod
    def _sends(outs, sems):
        send_sems, recv_sems = sems
        x, y, c, chips = _place()
        cps = []
        for w, ref in enumerate(outs):
            half = ref.shape[1] // 2
            blk = ref.at[2 * x + y, pl.ds(c * half, half)]
            cps += [_remote(blk, blk, send_sems, recv_sems, 6 * w + q, (*chip, c)) for q, chip in enumerate(chips)]
        return cps

    def start(self, ins, outs, sems):
        for cp in self._sends(outs, sems):
            cp.start()

    def finish(self, ins, outs, sems):
        send_sems, recv_sems = sems
        x, y, c, chips = _place()
        sibling = (x, y, 1 - c)
        forwards = []
        for w, ref in enumerate(outs):
            half = ref.shape[1] // 2
            for q, chip in enumerate(chips):
                blk = ref.at[2 * chip[0] + chip[1], pl.ds(c * half, half)]
                _remote(blk, blk, send_sems, recv_sems, 6 * w + q, sibling).wait_recv()
                fwd = _remote(blk, blk, send_sems, recv_sems, 6 * w + 3 + q, sibling)
                fwd.start()
                forwards.append(fwd)
        for w, ref in enumerate(outs):
            half = ref.shape[1] // 2
            for q, chip in enumerate(chips):
                blk = ref.at[2 * chip[0] + chip[1], pl.ds((1 - c) * half, half)]
                _remote(blk, blk, send_sems, recv_sems, 6 * w + 3 + q, sibling).wait_recv()
        for cp in self._sends(outs, sems) + forwards:
            cp.wait_send()


class _ExchangeChips:
    in_place = False

    def __init__(self, sums16):
        self.ins = list(sums16)
        self.out_shape = [jax.ShapeDtypeStruct(g.shape, g.dtype) for g in sums16]
        n = len(sums16)
        self.scratch = [pltpu.SemaphoreType.DMA((3 * n,)), pltpu.SemaphoreType.DMA((3 * n,))]

    @staticmethod
    def _sends(ins, outs, sems):
        send_sems, recv_sems = sems
        x, y, c, chips = _place()
        return [_remote(ins[w].at[2 * chip[0] + chip[1]], outs[w].at[2 * x + y], send_sems, recv_sems, 3 * w + q, (*chip, c))
                for w in range(len(ins)) for q, chip in enumerate(chips)]

    def start(self, ins, outs, sems):
        for cp in self._sends(ins, outs, sems):
            cp.start()

    def finish(self, ins, outs, sems):
        send_sems, recv_sems = sems
        x, y, c, chips = _place()
        for w in range(len(ins)):
            for q, chip in enumerate(chips):
                blk = outs[w].at[2 * chip[0] + chip[1]]
                _remote(blk, blk, send_sems, recv_sems, 3 * w + q, (*chip, c)).wait_recv()
        for cp in self._sends(ins, outs, sems):
            cp.wait_send()


def _mm_in_gather(chip, hn, win4):
    s, d = hn.shape
    c4 = win4.shape[2]
    tm, tn = _tile(s, 1024), _tile(c4, 1024)
    per = c4 // tn
    nj, gm = N_CHIPS * per, s // tm
    half = d // 2

    def tile_of(j):
        jj = j - per
        o = jnp.where(j < per, 0, jnp.where(j < 3 * per, 1 + jj % 2, 3))
        t = jnp.where(j < per, j, jnp.where(j < 3 * per, jj // 2, j - 3 * per))
        return o, t

    def step_of(o, t):
        return per + 2 * t + o - 1 if o in (1, 2) else o * per + t

    def block_of(j, chip_ref):
        o, _ = tile_of(j)
        return chip_ref[0] ^ (((o & 1) << 1) | (o >> 1))

    i_late = max(gm - 2, 0)

    def body(chip_ref, hn_ref, w_any, proj_ref, w_ref, wbuf, wsem, send_sems, recv_sems):
        del w_any
        j, i = pl.program_id(0), pl.program_id(1)
        x, y, c, chips = _place()
        sibling = (x, y, 1 - c)

        def region(k, rows_half, t):
            return w_ref.at[k, pl.ds(rows_half * half, half), pl.ds(t * tn, tn)]

        def send_mine(q, t):
            blk = region(2 * x + y, c, t)
            return _remote(blk, blk, send_sems, recv_sems, q * per + t, (*chips[q], c))

        def forward(q, t):
            blk = region(2 * chips[q][0] + chips[q][1], c, t)
            return _remote(blk, blk, send_sems, recv_sems, (3 + q) * per + t, sibling)

        def fetch(jj, mine, slot):
            rows = pl.ds((c if mine else 1 - c) * half, half)
            cols = pl.ds(pl.multiple_of(tile_of(jj)[1] * tn, LANE), tn)
            return pltpu.make_async_copy(w_ref.at[block_of(jj, chip_ref), rows, cols], wbuf.at[slot, rows],
                                         wsem.at[2 * slot + (0 if mine else 1)])

        def foreign_tiles():
            return [(o, t) for o in range(1, N_CHIPS) for t in range(per)]

        relayed = [(0, t) for t in range(1, per, 2)] + [(1, t) for t in range(0, per, 2)]

        def relay(q, t):
            blk = region(2 * chips[q][0] + chips[q][1], c, t)
            return _remote(blk, blk, send_sems, recv_sems, 2 * per + t, (*chips[1 - q], c))

        def landed(q, t):
            blk = region(2 * chips[q][0] + chips[q][1], c, t)
            _remote(blk, blk, send_sems, recv_sems, q * per + t, sibling).wait_recv()
            forward(q, t).start()
            if (q, t) in relayed:
                relay(q, t).start()

        @pl.when((j == 0) & (i == 0))
        def _():
            for t in range(per):
                send_mine(0, t).start()
                send_mine(1, t).start()
            fetch(0, True, 0).start()
            fetch(0, False, 0).start()

        nxt = j + 1

        @pl.when((i == 0) & (nxt < nj))
        def _():
            for o, t in foreign_tiles():
                @pl.when(nxt == step_of(o, t))
                def _():
                    if o == 1:
                        landed(0, t)
                        landed(1, t)
                    elif o == 3:
                        landed(2, t)

            fetch(nxt, True, nxt % 2).start()

            @pl.when(nxt < per)
            def _():
                fetch(nxt, False, nxt % 2).start()

        @pl.when((i == i_late) & (nxt < nj))
        def _():
            for o, t in foreign_tiles():
                @pl.when(nxt == step_of(o, t))
                def _():
                    blk = region(2 * chips[o - 1][0] + chips[o - 1][1], 1 - c, t)
                    _remote(blk, blk, send_sems, recv_sems, (3 + o - 1) * per + t, sibling).wait_recv()
                    fetch(nxt, False, nxt % 2).start()

        @pl.when(i == 0)
        def _():
            fetch(j, True, j % 2).wait()
            fetch(j, False, j % 2).wait()

        proj_ref[...] = _dot(hn_ref[...], wbuf[j % 2])

        @pl.when((j == nj - 1) & (i == gm - 1))
        def _():
            for t in range(per):
                send_mine(0, t).wait_send()
                send_mine(1, t).wait_send()
                for q in range(3):
                    forward(q, t).wait_send()
            for q, t in relayed:
                relay(q, t).wait_send()

    return pl.pallas_call(
        body, name="mm_in_gather",
        grid_spec=pltpu.PrefetchScalarGridSpec(
            num_scalar_prefetch=1, grid=(nj, gm),
            in_specs=[pl.BlockSpec((tm, d), lambda j, i, chip_ref: (i, 0)), ANY],
            out_specs=[pl.BlockSpec((tm, tn), lambda j, i, chip_ref: (i, block_of(j, chip_ref) * per + tile_of(j)[1])), ANY],
            scratch_shapes=[pltpu.VMEM((2, d, tn), BF16), pltpu.SemaphoreType.DMA((4,)),
                            pltpu.SemaphoreType.DMA((6 * per,)), pltpu.SemaphoreType.DMA((6 * per,))]),
        out_shape=[jax.ShapeDtypeStruct((s, N_CHIPS * c4), F32), jax.ShapeDtypeStruct(win4.shape, win4.dtype)],
        input_output_aliases={2: 1},
        compiler_params=_params("arbitrary", "arbitrary"),
    )(chip, hn, win4)


class _SwapHalves:
    in_place = False

    def __init__(self, grads, whole=False):
        self.ins = list(grads)
        self.whole = whole
        self.out_shape = [jax.ShapeDtypeStruct((N_CHIPS, g.shape[1] // (1 if whole else 2), g.shape[2]), g.dtype)
                          for g in grads]
        self.scratch = [pltpu.SemaphoreType.DMA((len(grads),)), pltpu.SemaphoreType.DMA((len(grads),))]

    def _copies(self, ins, outs, sems):
        x, y, c, _ = _place()
        cps = []
        for w in range(len(ins)):
            half = ins[w].shape[1] // 2
            src = ins[w] if self.whole else ins[w].at[:, pl.ds((1 - c) * half, half)]
            cps.append(_remote(src, outs[w], sems[0], sems[1], w, (x, y, 1 - c)))
        return cps

    def start(self, ins, outs, sems):
        for cp in self._copies(ins, outs, sems):
            cp.start()

    def finish(self, ins, outs, sems):
        for cp in self._copies(ins, outs, sems):
            cp.wait()


def _chip_sum(name, chip, core, grad, got):
    _, half, c = got.shape
    th = _tile(half, 256)
    n = half // th
    skip = n if grad.shape[1] != half else 0

    def body(chip_ref, core_ref, g_ref, o_ref, s_ref):
        del chip_ref, core_ref
        s_ref[...] = (g_ref[...] + o_ref[...].astype(F32)).astype(s_ref.dtype)

    def specs(first, out_block):
        def block(k, i, chip_ref):
            return (chip_ref[0] + first + k) % N_CHIPS

        return dict(
            in_specs=[pl.BlockSpec((None, th, c), lambda k, i, chip_ref, core_ref: (block(k, i, chip_ref), core_ref[0] * skip + i, 0)),
                      pl.BlockSpec((None, th, c), lambda k, i, chip_ref, core_ref: (block(k, i, chip_ref), i, 0))],
            out_specs=out_block(block))

    own = pl.pallas_call(
        body, name=name + "_own",
        grid_spec=pltpu.PrefetchScalarGridSpec(
            num_scalar_prefetch=2, grid=(1, n),
            **specs(0, lambda block: pl.BlockSpec((th, c), lambda k, i, chip_ref, core_ref: (i, 0)))),
        out_shape=jax.ShapeDtypeStruct((half, c), F32), compiler_params=_params("parallel", "parallel"),
    )(chip, core, grad, got)
    others = pl.pallas_call(
        functools.partial(body), name=name + "_others",
        grid_spec=pltpu.PrefetchScalarGridSpec(
            num_scalar_prefetch=2, grid=(N_CHIPS - 1, n),
            **specs(1, lambda block: pl.BlockSpec(
                (None, th, c), lambda k, i, chip_ref, core_ref: (block(k, i, chip_ref), i, 0)))),
        out_shape=jax.ShapeDtypeStruct((N_CHIPS, half, c), BF16), compiler_params=_params("parallel", "parallel"),
    )(chip, core, grad, got)
    return own, others


def _riding(comm, grid, body, n_pre, n_in, n_out):
    if comm is None:
        return body
    c_in, c_out, c_scr = len(comm.ins), len(comm.out_shape), len(comm.scratch)

    def wrapped(*refs):
        pre, rest = refs[:n_pre], refs[n_pre:]
        ins, rest = rest[:n_in], rest[n_in:]
        cin, rest = rest[:c_in], rest[c_in:]
        outs, rest = rest[:n_out], rest[n_out:]
        cout, rest = rest[:c_out], rest[c_out:]
        own, sems = rest[:len(rest) - c_scr], rest[len(rest) - c_scr:]
        ids = [pl.program_id(ax) for ax in range(len(grid))]
        first, last = ids[0] == 0, ids[0] == grid[0] - 1
        for ax in range(1, len(grid)):
            first, last = first & (ids[ax] == 0), last & (ids[ax] == grid[ax] - 1)

        @pl.when(first)
        def _():
            comm.start(cin, cout, sems)

        body(*pre, *ins, *outs, *own)

        @pl.when(last)
        def _():
            comm.finish(cin, cout, sems)

    return wrapped


def _total(name, chip, core, sum32, got16, comm=None):
    half, c = sum32.shape
    th = _tile(half, 256)
    n = half // th

    def body(chip_ref, core_ref, own_ref, a_ref, b_ref, c_ref, o_ref):
        del chip_ref, core_ref
        o_ref[...] = ((own_ref[...] + a_ref[...].astype(F32)) + b_ref[...].astype(F32)) + c_ref[...].astype(F32)

    def other(step):
        return pl.BlockSpec((None, th, c), lambda i, chip_ref, core_ref: ((chip_ref[0] + step) % N_CHIPS, i, 0))

    c_ins = list(comm.ins) if comm else []
    res = pl.pallas_call(
        _riding(comm, (n,), body, 2, 4, 1), name=name,
        grid_spec=pltpu.PrefetchScalarGridSpec(
            num_scalar_prefetch=2, grid=(n,),
            in_specs=[pl.BlockSpec((th, c), lambda i, chip_ref, core_ref: (i, 0)), other(1), other(2), other(3)]
            + [ANY] * len(c_ins),
            out_specs=[pl.BlockSpec((th, c), lambda i, chip_ref, core_ref: (core_ref[0] * n + i, 0))]
            + [ANY] * len(c_ins),
            scratch_shapes=list(comm.scratch) if comm else []),
        out_shape=[jax.ShapeDtypeStruct((2 * half, c), F32)] + (list(comm.out_shape) if comm else []),
        input_output_aliases={6 + i: 1 + i for i in range(len(c_ins))} if comm else {},
        compiler_params=_params("arbitrary" if comm else "parallel"),
    )(chip, core, sum32, got16, got16, got16, *c_ins)
    return (res[0], res[1:]) if comm else res[0]


class _JoinHalves:
    in_place = True

    def __init__(self, blocks):
        self.ins = list(blocks)
        self.out_shape = [jax.ShapeDtypeStruct(t.shape, t.dtype) for t in blocks]
        self.scratch = [pltpu.SemaphoreType.DMA((len(blocks),)), pltpu.SemaphoreType.DMA((len(blocks),))]

    @staticmethod
    def _copy(ref, w, mine, sems):
        x, y, c, _ = _place()
        half = ref.shape[0] // 2
        rows = ref.at[pl.ds((c if mine else 1 - c) * half, half)]
        return _remote(rows, rows, sems[0], sems[1], w, (x, y, 1 - c))

    def start(self, ins, outs, sems):
        for w, ref in enumerate(outs):
            self._copy(ref, w, True, sems).start()

    def finish(self, ins, outs, sems):
        for w, ref in enumerate(outs):
            self._copy(ref, w, True, sems).wait_send()
            self._copy(ref, w, False, sems).wait_recv()


class _Both:
    in_place = True

    def __init__(self, a, b):
        assert a.in_place and b.in_place
        self.parts = (a, b)
        self.ins = list(a.ins) + list(b.ins)
        self.out_shape = list(a.out_shape) + list(b.out_shape)
        self.scratch = list(a.scratch) + list(b.scratch)

    def _split(self, ins, outs, sems):
        a = self.parts[0]
        n, k = len(a.ins), len(a.scratch)
        return (a, ins[:n], outs[:n], sems[:k]), (self.parts[1], ins[n:], outs[n:], sems[k:])

    def start(self, ins, outs, sems):
        for part, i, o, s in self._split(ins, outs, sems):
            part.start(i, o, s)

    def finish(self, ins, outs, sems):
        for part, i, o, s in self._split(ins, outs, sems):
            part.finish(i, o, s)


def _alone(name, comm):
    n = len(comm.ins)

    def body(*refs):
        comm.start(refs[:n], refs[n:2 * n], refs[2 * n:])
        comm.finish(refs[:n], refs[n:2 * n], refs[2 * n:])

    return pl.pallas_call(
        body, name=name, in_specs=[ANY] * n, out_specs=[ANY] * n, out_shape=comm.out_shape,
        scratch_shapes=comm.scratch, input_output_aliases={i: i for i in range(n)},
    )(*comm.ins)


class _GatherSmall:
    in_place = True

    def __init__(self, full):
        self.ins = [full]
        self.out_shape = [jax.ShapeDtypeStruct(full.shape, full.dtype)]
        self.scratch = [pltpu.SemaphoreType.DMA((7,)), pltpu.SemaphoreType.DMA((7,))]

    @staticmethod
    def _copy(ref, sems, k, block, to):
        m_per = ref.shape[0] // N_DEV
        px, py, pc = block
        rows = ref.at[pl.ds((4 * px + 2 * py + pc) * m_per, m_per), :]
        return _remote(rows, rows, sems[0], sems[1], k, to)

    def _first(self, ref, sems):
        x, y, c, chips = _place()
        me = (x, y, c)
        return [self._copy(ref, sems, 0, me, (x, y, 1 - c))] + \
               [self._copy(ref, sems, 1 + j, me, (*chip, c)) for j, chip in enumerate(chips)]

    def start(self, ins, outs, sems):
        for cp in self._first(outs[0], sems):
            cp.start()

    def finish(self, ins, outs, sems):
        ref = outs[0]
        x, y, c, chips = _place()
        me, sibling = (x, y, c), (x, y, 1 - c)
        passed = [self._copy(ref, sems, 4 + j, (*chip, c), sibling) for j, chip in enumerate(chips)]
        for j, chip in enumerate(chips):
            self._copy(ref, sems, 1 + j, (*chip, c), me).wait_recv()
            passed[j].start()
        self._copy(ref, sems, 0, sibling, me).wait_recv()
        for j, chip in enumerate(chips):
            self._copy(ref, sems, 4 + j, (*chip, 1 - c), me).wait_recv()
        for cp in self._first(ref, sems) + passed:
            cp.wait_send()


def _adamw_math(w, g, m, v):
    m = ADAM_B1 * m + (1.0 - ADAM_B1) * g
    v = ADAM_B2 * v + (1.0 - ADAM_B2) * (g * g)
    m_hat = m / (1.0 - ADAM_B1 ** ADAM_STEP)
    v_hat = v / (1.0 - ADAM_B2 ** ADAM_STEP)
    delta = -ADAM_LR * (m_hat / (jnp.sqrt(v_hat) + ADAM_EPS) + ADAM_WD * w)
    return delta, m, v


def _adamw(name, w, g, m, v):
    r, c = w.shape
    tm = _tile(r, 128)

    def body(w_ref, g_ref, m_ref, v_ref, g_out, d_out, m_out, v_out):
        g = g_ref[...]
        g_out[...] = g
        d_out[...], m_out[...], v_out[...] = _adamw_math(w_ref[...], g, m_ref[...], v_ref[...])

    spec = pl.BlockSpec((tm, c), lambda i: (i, 0))
    return pl.pallas_call(
        body, name=name, grid=(r // tm,), in_specs=[spec] * 4, out_specs=[spec] * 4,
        out_shape=[jax.ShapeDtypeStruct((r, c), F32)] * 4, compiler_params=_params("parallel"),
    )(w, g, m, v)


def _adamw_small(gathered, w, m, v):
    rows = w.shape[0]

    def body(all_ref, w_ref, m_ref, v_ref, g_out, d_out, m_out, v_out):
        g = all_ref[0:rows, :]
        for dev in range(1, N_DEV):
            g = g + all_ref[dev * rows:(dev + 1) * rows, :]
        g_out[...] = g
        d_out[...], m_out[...], v_out[...] = _adamw_math(w_ref[...], g, m_ref[...], v_ref[...])

    return pl.pallas_call(
        body, name="adamw_small", out_shape=[jax.ShapeDtypeStruct(w.shape, F32)] * 4, compiler_params=_params(),
    )(gathered, w, m, v)


SMALL = ("g_pre", "w_s", "b_s", "ln_v_g", "ln_v_b", "g_q", "g_k", "rel_bias", "g_out_a", "g_out_b", "g_ple")
LARGE = ("w_in", "w_out", "w_ple_gate", "w_ple_up")
WEIGHTS = ("g_pre", "w_in", "w_s", "b_s", "ln_v_g", "ln_v_b", "g_q", "g_k", "rel_bias", "g_out_a", "g_out_b", "w_out",
           "g_ple", "w_ple_gate", "w_ple_up")


def _pack(parts):
    flat = jnp.concatenate([parts[n].reshape(-1).astype(F32) for n in SMALL])
    rows = -(-flat.shape[0] // (8 * LANE)) * 8
    return jnp.pad(flat, (0, rows * LANE - flat.shape[0])).reshape(rows, LANE)


def _unpack(pack, like):
    flat = pack.reshape(-1)
    out, at = {}, 0
    for n in SMALL:
        size = math.prod(like[n].shape)
        out[n] = flat[at:at + size].reshape(like[n].shape)
        at += size
    return out


def kernel(x, p, g_pre, w_in, w_s, b_s, ln_v_g, ln_v_b, g_q, g_k, rel_bias, g_out_a, g_out_b, w_out, g_ple, w_ple_gate, w_ple_up, loss_target, m_g_pre, m_w_in, m_w_s, m_b_s, m_ln_v_g, m_ln_v_b, m_g_q, m_g_k, m_rel_bias, m_g_out_a, m_g_out_b, m_w_out, m_g_ple, m_w_ple_gate, m_w_ple_up, v_g_pre, v_w_in, v_w_s, v_b_s, v_ln_v_g, v_ln_v_b, v_g_q, v_g_k, v_rel_bias, v_g_out_a, v_g_out_b, v_w_out, v_g_ple, v_w_ple_gate, v_w_ple_up):
    given = dict(locals())
    weights = {n: given[n] for n in WEIGHTS}
    mom_m = {n: given["m_" + n] for n in WEIGHTS}
    mom_v = {n: given["v_" + n] for n in WEIGHTS}
    xs, ps, tgt = x[0], p[0, 0], loss_target[0]
    d = xs.shape[1]

    core = lax.axis_index("c").astype(jnp.int32).reshape(1)
    chip = (2 * lax.axis_index("x") + lax.axis_index("y")).astype(jnp.int32).reshape(1)

    win4, wout4, wgate4, wup4 = [_cast_bf16("cast_" + n, chip, weights[n][0]) for n in LARGE]

    sp = {
        "g_pre": g_pre, "w_s": w_s[0], "b_s": b_s[0], "ln_v_g": ln_v_g, "ln_v_b": ln_v_b, "g_q": g_q, "g_k": g_k,
        "rel_bias": rel_bias, "g_out_a": g_out_a, "g_out_b": g_out_b, "g_ple": g_ple,
    }
    loss_local, grad_x, (sums32, got16), small = _local_step(xs, ps, tgt, sp, win4, wout4, wgate4, wup4,
                                                             dist={"chip": chip, "core": core})
    loss = lax.psum(loss_local, MESH_AXES)

    own = dict(zip(LARGE, sums32))
    got = dict(zip(LARGE, got16))
    halves = [_total("total_" + n, chip, core, own[n], got[n]) for n in LARGE[1:]]
    my_pack = _pack(small)
    device = 2 * chip[0] + core[0]
    everyone = lax.dynamic_update_slice(jnp.zeros((N_DEV * my_pack.shape[0], LANE), F32), my_pack,
                                        (device * my_pack.shape[0], 0))
    half_in, (*joined, gathered) = _total("total_w_in", chip, core, own["w_in"], got["w_in"],
                                          comm=_Both(_JoinHalves(halves), _GatherSmall(everyone)))
    grads = dict(zip(LARGE[1:], joined))
    grads["w_in"], = _alone("join_w_in", _JoinHalves([half_in]))

    out_g, out_d, out_m, out_v = {}, {}, {}, {}
    for n in LARGE:
        results = _adamw("adamw_" + n, weights[n][0], grads[n], mom_m[n][0], mom_v[n][0])
        out_g[n], out_d[n], out_m[n], out_v[n] = [r[None] for r in results]

    pg, pd, pm, pv = _adamw_small(gathered, _pack(weights), _pack(mom_m), _pack(mom_v))
    for packed, out in ((pg, out_g), (pd, out_d), (pm, out_m), (pv, out_v)):
        out.update(_unpack(packed, weights))

    return (loss, grad_x[None], *[out_g[n] for n in WEIGHTS], *[out_d[n] for n in WEIGHTS],
            *[out_m[n] for n in WEIGHTS], *[out_v[n] for n in WEIGHTS])
```
